```python
import jax, jax.numpy as jnp
from jax import lax
import numpy as np

D_MODEL = 1024
BATCH = 16
SEQ = 2048
DEPTH = 2

CTX_LEN = 256
GRID_W = 64
EPS = 1e-6
NEG_INF = -1e30
NA_HEADS = 8
NA_HEAD_DIM = 64
NA_WIDTH = NA_HEADS * NA_HEAD_DIM
WIN_H = 8
WIN_W = 16
Q_COLS = 16
K_COLS = Q_COLS + WIN_W
SC_WIDTH = D_MODEL - NA_WIDTH
SC_CONV = 3
LRU_WIDTH = D_MODEL
LRU_BLOCKS = 4
LRU_BLOCK = LRU_WIDTH // LRU_BLOCKS
LRU_CONV = 4
LRU_C = 8.0
N_EXPERTS = 32
TOP_K = 4
D_EXPERT = D_MODEL
SWIGLU_LIMIT = 7.0
SWIGLU_ALPHA = 1.702
EXPERT_BLOCK = 256

kernel_name = "hybrid_natten_shortconv_rglru_moe_prefix_ctx"


def rms_norm(x, g):
    x32 = x.astype(jnp.float32)
    y = x32 * lax.rsqrt(jnp.mean(x32 * x32, axis=-1, keepdims=True) + EPS)
    return (y * g.astype(jnp.float32)).astype(x.dtype)


def modulate(h, shift, scale):
    return h * (1 + scale) + shift


def depthwise_conv(x, w, b):
    k = w.shape[0]
    left = (k - 1) // 2
    t = x.shape[1]
    xp = jnp.pad(x, ((0, 0), (left, k - 1 - left), (0, 0)))
    return sum(xp[:, j:j + t] * w[j] for j in range(k)) + b


def heads(t):
    return t.reshape(t.shape[:-1] + (NA_HEADS, NA_HEAD_DIM))


def na_static(rows):
    kh = min(WIN_H, rows)
    r = np.arange(rows)
    r0 = np.clip(r - kh // 2, 0, rows - kh)
    dr_idx = r0[:, None] + np.arange(kh)[None] - r[:, None] + WIN_H - 1
    ncb = GRID_W // Q_COLS
    kc0 = np.clip(np.arange(ncb) * Q_COLS - WIN_W // 2, 0, GRID_W - K_COLS)
    band = kc0[:, None] + np.arange(K_COLS)[None]
    qc = np.arange(ncb)[:, None] * Q_COLS + np.arange(Q_COLS)[None]
    c0 = np.clip(qc - WIN_W // 2, 0, GRID_W - WIN_W)[..., None]
    kcol = band[:, None, :]
    valid = (kcol >= c0) & (kcol < c0 + WIN_W)
    dc_idx = np.clip(kcol - qc[..., None] + WIN_W - 1, 0, 2 * WIN_W - 2)
    return kh, r0.astype(np.int32), dr_idx, band, valid, dc_idx


def neighbourhood_attention(q, k, v, kc, vc, rpb):
    b_, t = q.shape[:2]
    rows = t // GRID_W
    ncb = GRID_W // Q_COLS
    kh, r0, dr_idx, band, valid, dc_idx = na_static(rows)
    grid = lambda a: a.reshape(b_, rows, GRID_W, NA_HEADS, NA_HEAD_DIM)
    q, k, v = grid(q), grid(k), grid(v)
    bias = rpb[:, dr_idx[:, :, None, None, None], dc_idx[None, None]]
    bias = jnp.where(valid[None, None, None], bias.astype(jnp.float32), NEG_INF)
    bias = bias.transpose(1, 0, 3, 4, 2, 5).reshape(rows, NA_HEADS, ncb, Q_COLS, kh * K_COLS)
    r0 = jnp.asarray(r0)
    n_lat = kh * K_COLS

    def row_block(args):
        r, bias_r = args
        q_r = lax.dynamic_index_in_dim(q, r, axis=1, keepdims=False)
        q_r = q_r.reshape(b_, ncb, Q_COLS, NA_HEADS, NA_HEAD_DIM)
        k_rows = lax.dynamic_slice_in_dim(k, r0[r], kh, axis=1)
        v_rows = lax.dynamic_slice_in_dim(v, r0[r], kh, axis=1)
        k_band = k_rows[:, :, band]
        v_band = v_rows[:, :, band]
        s_lat = jnp.einsum('bnqhd,binchd->bhnqic', q_r, k_band).astype(jnp.float32)
        s_lat = s_lat.reshape(b_, NA_HEADS, ncb, Q_COLS, n_lat) + bias_r
        s_ctx = jnp.einsum('bnqhd,blhd->bhnql', q_r, kc).astype(jnp.float32)
        p = jax.nn.softmax(jnp.concatenate([s_lat, s_ctx], axis=-1), axis=-1).astype(v.dtype)
        p_lat = p[..., :n_lat].reshape(b_, NA_HEADS, ncb, Q_COLS, kh, K_COLS)
        o = (jnp.einsum('bhnqic,binchd->bnqhd', p_lat, v_band)
             + jnp.einsum('bhnql,blhd->bnqhd', p[..., n_lat:], vc))
        return o.reshape(b_, GRID_W, NA_WIDTH)

    out = lax.map(row_block, (jnp.arange(rows), bias))
    return out.transpose(1, 0, 2, 3).reshape(b_, t, NA_WIDTH)


def context_attention(qc, kc, vc):
    s = jnp.einsum('blhd,bmhd->bhlm', qc, kc).astype(jnp.float32)
    p = jax.nn.softmax(s, axis=-1).astype(vc.dtype)
    o = jnp.einsum('bhlm,bmhd->blhd', p, vc)
    return o.reshape(o.shape[0], o.shape[1], NA_WIDTH)


def even_mixer(h, hc, w_in, w_out, q_g, k_g, rpb, conv_w, conv_b, need_ctx):
    cuts = [NA_WIDTH, 2 * NA_WIDTH, 3 * NA_WIDTH, 3 * NA_WIDTH + SC_WIDTH, 3 * NA_WIDTH + 2 * SC_WIDTH]
    q_scale = NA_HEAD_DIM ** -0.5
    q, k, v, bg, cg, xin = jnp.split(h @ w_in, cuts, axis=-1)
    q = rms_norm(heads(q), q_g) * q_scale
    k = rms_norm(heads(k), k_g)
    if need_ctx:
        qc, kc, vc, bgc, cgc, xinc = jnp.split(hc @ w_in, cuts, axis=-1)
    else:
        kc, vc = jnp.split(hc @ w_in[:, NA_WIDTH:3 * NA_WIDTH], 2, axis=-1)
    kc = rms_norm(heads(kc), k_g)
    vc = heads(vc)
    o_a = neighbourhood_attention(q, k, heads(v), kc, vc, rpb)
    o_b = bg * depthwise_conv(cg * xin, conv_w, conv_b)
    y = jnp.concatenate([o_a, o_b], axis=-1) @ w_out
    if not need_ctx:
        return y, None
    oc_a = context_attention(rms_norm(heads(qc), q_g) * q_scale, kc, vc)
    oc_b = bgc * depthwise_conv(cgc * xinc, conv_w, conv_b)
    yc = jnp.concatenate([oc_a, oc_b], axis=-1) @ w_out
    return y, yc


def block_diag(u, w, b):
    b_, t, _ = u.shape
    ub = u.reshape(b_, t, LRU_BLOCKS, LRU_BLOCK)
    return jnp.einsum('bthi,hij->bthj', ub, w).reshape(b_, t, LRU_WIDTH) + b


def rglru_coeffs(u, wa, ba, wx, bx, lam):
    r = jax.nn.sigmoid(block_diag(u, wa, ba).astype(jnp.float32))
    i = jax.nn.sigmoid(block_diag(u, wx, bx).astype(jnp.float32))
    log_a = LRU_C * r * jax.nn.log_sigmoid(lam.astype(jnp.float32))
    a = jnp.exp(log_a)
    mult = jnp.sqrt(-jnp.expm1(2.0 * log_a))
    return a, mult * i * u.astype(jnp.float32)


def linear_scan(a, b, h0, reverse):
    idx = -1 if reverse else 0
    b = b.at[:, idx].add(a[:, idx] * h0)

    def combine(e1, e2):
        a1, b1 = e1
        a2, b2 = e2
        return a1 * a2, a2 * b1 + b2

    _, h = lax.associative_scan(combine, (a, b), axis=1, reverse=reverse)
    return h


def odd_mixer(h, hc, w_in, w_out, conv_w, conv_b, fwd, bwd, need_ctx):
    gate, u = jnp.split(h @ w_in, 2, axis=-1)
    u = depthwise_conv(u, conv_w, conv_b)
    if need_ctx:
        gate_c, uc = jnp.split(hc @ w_in, 2, axis=-1)
    else:
        uc = hc @ w_in[:, LRU_WIDTH:]
    uc = depthwise_conv(uc, conv_w, conv_b)
    zeros = jnp.zeros((h.shape[0], LRU_WIDTH), jnp.float32)
    a_c, b_c = rglru_coeffs(uc, *fwd)
    hc_f = linear_scan(a_c, b_c, zeros, False)
    a_l, b_l = rglru_coeffs(u, *fwd)
    h_f = linear_scan(a_l, b_l, hc_f[:, -1], False)
    a_c, b_c = rglru_coeffs(uc, *bwd)
    hc_b = linear_scan(a_c, b_c, zeros, True)
    a_l, b_l = rglru_coeffs(u, *bwd)
    h_b = linear_scan(a_l, b_l, hc_b[:, 0], True)
    y = ((h_f + h_b).astype(h.dtype) * jax.nn.gelu(gate)) @ w_out
    if not need_ctx:
        return y, None
    yc = ((hc_f + hc_b).astype(hc.dtype) * jax.nn.gelu(gate_c)) @ w_out
    return y, yc


def moe(x, w_r, b_r, wg, bg, wu, bu, wd, bd):
    n = x.shape[0]
    logits = (x @ w_r + b_r).astype(jnp.float32)
    top_v, top_i = lax.top_k(logits, TOP_K)
    gates = jax.nn.softmax(top_v, axis=-1).astype(x.dtype)
    nk = n * TOP_K
    e_flat = top_i.reshape(-1)
    tok_flat = jnp.arange(nk, dtype=jnp.int32) // TOP_K
    order = jnp.argsort(e_flat)
    e_sorted = e_flat[order]
    counts = jnp.bincount(e_flat, length=N_EXPERTS)
    padded = (counts + EXPERT_BLOCK - 1) // EXPERT_BLOCK * EXPERT_BLOCK
    starts = jnp.cumsum(counts) - counts
    pends = jnp.cumsum(padded)
    dest = (pends - padded)[e_sorted] + jnp.arange(nk, dtype=jnp.int32) - starts[e_sorted]
    n_blocks = -(-nk // EXPERT_BLOCK) + N_EXPERTS
    n_slots = n_blocks * EXPERT_BLOCK
    slot_tok = jnp.zeros((n_slots,), jnp.int32).at[dest].set(tok_flat[order])
    slot_w = jnp.zeros((n_slots,), x.dtype).at[dest].set(gates.reshape(-1)[order])
    block_e = jnp.minimum(jnp.searchsorted(pends, jnp.arange(n_blocks) * EXPERT_BLOCK, side='right'), N_EXPERTS - 1)
    xb = x[slot_tok].reshape(n_blocks, EXPERT_BLOCK, x.shape[1])

    def expert_block(args):
        xe, e = args
        g = jnp.minimum(xe @ wg[e] + bg[e], SWIGLU_LIMIT)
        u = jnp.clip(xe @ wu[e] + bu[e], -SWIGLU_LIMIT, SWIGLU_LIMIT)
        return (g * jax.nn.sigmoid(SWIGLU_ALPHA * g) * (u + 1)) @ wd[e] + bd[e]

    yb = lax.map(expert_block, (xb, block_e)).reshape(n_slots, x.shape[1])
    return jnp.zeros_like(x).at[slot_tok].add(yb * slot_w[:, None])


def setup_inputs(seed: int = 0) -> dict:
    key = jax.random.key(seed)
    ks = iter(jax.random.split(key, 64))
    d = D_MODEL
    n_even = (DEPTH + 1) // 2
    n_odd = DEPTH // 2

    def nrm(shape, scale):
        return jax.random.normal(next(ks), shape, jnp.float32) * scale

    def lam(shape):
        u = jax.random.uniform(next(ks), shape, jnp.float32, minval=0.9, maxval=0.999)
        s = u ** (1.0 / LRU_C)
        return jnp.log(s) - jnp.log1p(-s)

    inp = {}
    inp['x'] = nrm((BATCH, SEQ, d), 1.0)
    inp['c'] = nrm((BATCH, d), 1.0)
    inp['ctx'] = nrm((BATCH, CTX_LEN, d), 1.0)
    inp['c_ctx'] = nrm((d,), 1.0)
    inp['ada_w'] = nrm((DEPTH, d, 6 * d), 0.5 * d ** -0.5)
    inp['ada_b'] = nrm((DEPTH, 6 * d), 0.02)
    inp['norm1_g'] = 1.0 + nrm((DEPTH, d), 0.05)
    inp['norm2_g'] = 1.0 + nrm((DEPTH, d), 0.05)
    inp['ev_w_in'] = nrm((n_even, d, 3 * NA_WIDTH + 3 * SC_WIDTH), d ** -0.5)
    inp['ev_w_out'] = nrm((n_even, NA_WIDTH + SC_WIDTH, d), (NA_WIDTH + SC_WIDTH) ** -0.5)
    inp['ev_q_gain'] = 1.0 + nrm((n_even, NA_HEAD_DIM), 0.05)
    inp['ev_k_gain'] = 1.0 + nrm((n_even, NA_HEAD_DIM), 0.05)
    inp['ev_rpb'] = nrm((n_even, NA_HEADS, 2 * WIN_H - 1, 2 * WIN_W - 1), 0.1)
    inp['ev_conv_w'] = nrm((n_even, SC_CONV, SC_WIDTH), SC_CONV ** -0.5)
    inp['ev_conv_b'] = nrm((n_even, SC_WIDTH), 0.02)
    inp['od_w_in'] = nrm((n_odd, d, 2 * LRU_WIDTH), d ** -0.5)
    inp['od_w_out'] = nrm((n_odd, LRU_WIDTH, d), LRU_WIDTH ** -0.5)
    inp['od_conv_w'] = nrm((n_odd, LRU_CONV, LRU_WIDTH), LRU_CONV ** -0.5)
    inp['od_conv_b'] = nrm((n_odd, LRU_WIDTH), 0.02)
    for dr in ('fwd', 'bwd'):
        inp['od_' + dr + '_wa'] = nrm((n_odd, LRU_BLOCKS, LRU_BLOCK, LRU_BLOCK), LRU_BLOCK ** -0.5)
        inp['od_' + dr + '_ba'] = nrm((n_odd, LRU_WIDTH), 0.02)
        inp['od_' + dr + '_wx'] = nrm((n_odd, LRU_BLOCKS, LRU_BLOCK, LRU_BLOCK), LRU_BLOCK ** -0.5)
        inp['od_' + dr + '_bx'] = nrm((n_odd, LRU_WIDTH), 0.02)
        inp['od_' + dr + '_lam'] = lam((n_odd, LRU_WIDTH))
    inp['router_w'] = nrm((DEPTH, d, N_EXPERTS), d ** -0.5)
    inp['router_b'] = nrm((DEPTH, N_EXPERTS), 0.01)
    inp['exp_w_gate'] = nrm((DEPTH, N_EXPERTS, d, D_EXPERT), d ** -0.5)
    inp['exp_b_gate'] = nrm((DEPTH, N_EXPERTS, D_EXPERT), 0.02)
    inp['exp_w_up'] = nrm((DEPTH, N_EXPERTS, d, D_EXPERT), d ** -0.5)
    inp['exp_b_up'] = nrm((DEPTH, N_EXPERTS, D_EXPERT), 0.02)
    inp['exp_w_down'] = nrm((DEPTH, N_EXPERTS, D_EXPERT, d), D_EXPERT ** -0.5)
    inp['exp_b_down'] = nrm((DEPTH, N_EXPERTS, d), 0.02)
    return inp


def reference(x, c, ctx, c_ctx, ada_w, ada_b, norm1_g, norm2_g,
              ev_w_in, ev_w_out, ev_q_gain, ev_k_gain, ev_rpb, ev_conv_w, ev_conv_b,
              od_w_in, od_w_out, od_conv_w, od_conv_b,
              od_fwd_wa, od_fwd_ba, od_fwd_wx, od_fwd_bx, od_fwd_lam,
              od_bwd_wa, od_bwd_ba, od_bwd_wx, od_bwd_bx, od_bwd_lam,
              router_w, router_b, exp_w_gate, exp_b_gate, exp_w_up, exp_b_up, exp_w_down, exp_b_down):
    b_, t, d = x.shape
    n_ctx = ctx.shape[1]
    s_lat = jax.nn.silu(c)
    s_ctx = jax.nn.silu(c_ctx)
    hctx = ctx
    for l in range(DEPTH):
        last = l == DEPTH - 1
        mod = (s_lat @ ada_w[l] + ada_b[l])[:, None, :]
        modc = s_ctx @ ada_w[l] + ada_b[l]
        sh1, sc1, g1, sh2, sc2, g2 = jnp.split(mod, 6, axis=-1)
        csh1, csc1, cg1, csh2, csc2, cg2 = jnp.split(modc, 6, axis=-1)
        h = modulate(rms_norm(x, norm1_g[l]), sh1, sc1)
        hc = modulate(rms_norm(hctx, norm1_g[l]), csh1, csc1)
        if l % 2 == 0:
            j = l // 2
            y, yc = even_mixer(h, hc, ev_w_in[j], ev_w_out[j], ev_q_gain[j], ev_k_gain[j], ev_rpb[j],
                               ev_conv_w[j], ev_conv_b[j], not last)
        else:
            j = l // 2
            fwd = (od_fwd_wa[j], od_fwd_ba[j], od_fwd_wx[j], od_fwd_bx[j], od_fwd_lam[j])
            bwd = (od_bwd_wa[j], od_bwd_ba[j], od_bwd_wx[j], od_bwd_bx[j], od_bwd_lam[j])
            y, yc = odd_mixer(h, hc, od_w_in[j], od_w_out[j], od_conv_w[j], od_conv_b[j], fwd, bwd, not last)
        x = x + g1 * y
        h2 = modulate(rms_norm(x, norm2_g[l]), sh2, sc2)
        moe_w = (router_w[l], router_b[l], exp_w_gate[l], exp_b_gate[l], exp_w_up[l], exp_b_up[l],
                 exp_w_down[l], exp_b_down[l])
        if last:
            x = x + g2 * moe(h2.reshape(-1, d), *moe_w).reshape(b_, t, d)
        else:
            hctx = hctx + cg1 * yc
            h2c = modulate(rms_norm(hctx, norm2_g[l]), csh2, csc2)
            tokens = jnp.concatenate([h2c, h2], axis=1).reshape(-1, d)
            y2 = moe(tokens, *moe_w).reshape(b_, n_ctx + t, d)
            hctx = hctx + cg2 * y2[:, :n_ctx]
            x = x + g2 * y2[:, n_ctx:]
    return x
```

```python
import functools

import numpy as np
import jax
import jax.numpy as jnp
from jax import lax
from jax.experimental import pallas as pl
from jax.experimental.pallas import tpu as pltpu

DEPTH = 2
GRID_W = 64
EPS = 1e-6
NEG_INF = -1e30
NA_HEADS = 8
NA_HEAD_DIM = 64
NA_WIDTH = NA_HEADS * NA_HEAD_DIM
HEAD_PAIRS = NA_HEADS // 2
WIN_H = 8
WIN_W = 16
SC_CONV = 3
LRU_BLOCKS = 4
LRU_CONV = 4
LRU_C = 8.0
N_EXPERTS = 32
TOP_K = 4
SWIGLU_LIMIT = 7.0
SWIGLU_ALPHA = 1.702

LANES = 128
SUBLANES = 8
HALO = 16
ROUTER_PAD = LANES
EXPERT_TILE = 512
VMEM_LIMIT = 56 * 1024 * 1024

F32 = jnp.float32
BF16 = jnp.bfloat16


def _params(sem, vmem=VMEM_LIMIT):
    return pltpu.CompilerParams(dimension_semantics=sem, vmem_limit_bytes=vmem)


def _dot(a, b):
    return jnp.dot(a, b, preferred_element_type=F32)


def _dot_nt(a, b):
    return lax.dot_general(a, b, (((1,), (1,)), ((), ())), preferred_element_type=F32)


def _rms_mod(x, g, shift, scale):
    ms = jnp.mean(x * x, axis=-1, keepdims=True)
    y = x * lax.rsqrt(ms + EPS) * g
    return y * (1.0 + scale) + shift


def _ada_kernel(c_ref, w_ref, b_ref, o_ref):
    c = c_ref[...]
    s = (c * jax.nn.sigmoid(c)).astype(BF16)
    o_ref[0] = _dot(s, w_ref[0].astype(BF16)) + b_ref[0]


def _ada_mod(cvec, ada_w, ada_b):
    depth, d, n = ada_w.shape
    r = cvec.shape[0]
    tn = 1536
    return pl.pallas_call(
        _ada_kernel,
        out_shape=jax.ShapeDtypeStruct((depth, r, n), F32),
        grid=(depth, n // tn),
        in_specs=[
            pl.BlockSpec((r, d), lambda l, j: (0, 0)),
            pl.BlockSpec((1, d, tn), lambda l, j: (l, 0, j)),
            pl.BlockSpec((1, 1, tn), lambda l, j: (l, 0, j)),
        ],
        out_specs=pl.BlockSpec((1, r, tn), lambda l, j: (l, 0, j)),
        compiler_params=_params(("parallel", "parallel")),
        name="ada_mod",
    )(cvec, ada_w, ada_b.reshape(depth, 1, n))


def _inproj_kernel(x_ref, g_ref, sh_ref, sc_ref, w_ref, hg_ref, ones_ref, o_ref, *, n_tiles, tn, n_headnorm):
    h = _rms_mod(x_ref[0], g_ref[...], sh_ref[0], sc_ref[0]).astype(BF16)
    for j in range(n_tiles):
        y = _dot(h, w_ref[:, j * tn:(j + 1) * tn])
        if j < n_headnorm:
            ms = _dot((y * y).astype(BF16), ones_ref[...]) * (1.0 / NA_HEAD_DIM)
            y = y * lax.rsqrt(ms + EPS) * hg_ref[j]
        o_ref[0, :, j * tn:(j + 1) * tn] = y.astype(o_ref.dtype)


def _inproj(x, g, shift, scale, w, head_gain=None):
    b, t, d = x.shape
    n = w.shape[1]
    tn = NA_WIDTH
    tm = min(t, 512)
    n_headnorm = 0 if head_gain is None else head_gain.shape[0]
    if head_gain is None:
        head_gain = jnp.ones((1, 1, tn), F32)
    hid = np.arange(tn) // NA_HEAD_DIM
    ones_bd = jnp.asarray((hid[:, None] == hid[None, :]), BF16)
    kern = functools.partial(_inproj_kernel, n_tiles=n // tn, tn=tn, n_headnorm=n_headnorm)
    return pl.pallas_call(
        kern,
        out_shape=jax.ShapeDtypeStruct((b, t, n), BF16),
        grid=(b, t // tm),
        in_specs=[
            pl.BlockSpec((1, tm, d), lambda bi, i: (bi, i, 0)),
            pl.BlockSpec((1, d), lambda bi, i: (0, 0)),
            pl.BlockSpec((1, 1, d), lambda bi, i: (bi, 0, 0)),
            pl.BlockSpec((1, 1, d), lambda bi, i: (bi, 0, 0)),
            pl.BlockSpec((d, n), lambda bi, i: (0, 0)),
            pl.BlockSpec(head_gain.shape, lambda bi, i: (0, 0, 0)),
            pl.BlockSpec((tn, tn), lambda bi, i: (0, 0)),
        ],
        out_specs=pl.BlockSpec((1, tm, n), lambda bi, i: (bi, i, 0)),
        compiler_params=_params(("parallel", "parallel")),
        name="inproj",
    )(x, g, shift, scale, w, head_gain, ones_bd)


def _na_tables(rows):
    kh = min(WIN_H, rows)
    r = np.arange(rows)
    r0 = np.clip(r - kh // 2, 0, rows - kh)
    dr = r0[:, None] + np.arange(kh)[None] - r[:, None] + WIN_H - 1
    patterns, row_type = np.unique(dr, axis=0, return_inverse=True)
    return kh, r0.astype(np.int32), row_type.reshape(-1).astype(np.int32), patterns


def _na_bias_table(rpb, patterns):
    qc = np.arange(GRID_W)
    kc = np.arange(GRID_W)
    c0 = np.clip(qc - WIN_W // 2, 0, GRID_W - WIN_W)[:, None]
    valid = (kc[None] >= c0) & (kc[None] < c0 + WIN_W)
    dc = np.clip(kc[None] - qc[:, None] + WIN_W - 1, 0, 2 * WIN_W - 2)
    n_pat, kh = patterns.shape
    tab = rpb.astype(F32)[:, patterns[:, :, None, None], dc[None, None]]
    tab = jnp.where(valid[None, None, None], tab, NEG_INF)
    tab = tab.reshape(HEAD_PAIRS, 2, n_pat, kh, GRID_W, GRID_W)
    tab = tab.transpose(2, 0, 1, 4, 3, 5)
    return tab.reshape(n_pat, HEAD_PAIRS, 2 * GRID_W, kh * GRID_W)


def _pair_attention(q, keys, values, biases):
    m = q.shape[0]
    lane = lax.broadcasted_iota(jnp.int32, q.shape, 1)
    zero = jnp.zeros_like(q)
    qq = jnp.concatenate([jnp.where(lane < NA_HEAD_DIM, q, zero), jnp.where(lane >= NA_HEAD_DIM, q, zero)], axis=0)
    scores = []
    for k, bias in zip(keys, biases):
        s = _dot_nt(qq, k)
        scores.append(s if bias is None else s + bias)
    mx = functools.reduce(jnp.maximum, [jnp.max(s, axis=-1, keepdims=True) for s in scores])
    exps = [jnp.exp(s - mx) for s in scores]
    denom = functools.reduce(jnp.add, [jnp.sum(e, axis=-1, keepdims=True) for e in exps])
    o = functools.reduce(jnp.add, [_dot(e.astype(BF16), v) for e, v in zip(exps, values)])
    o = o * (1.0 / denom)
    lane_o = lax.broadcasted_iota(jnp.int32, (m, LANES), 1)
    return jnp.where(lane_o < NA_HEAD_DIM, o[:m], o[m:])


def _na_kernel(r0_ref, type_ref, q_ref, k_ref, v_ref, kc_ref, vc_ref, bias_ref, o_ref, *, kh):
    r = pl.program_id(1)
    start = pl.multiple_of(r0_ref[r] * GRID_W, GRID_W)
    rtype = type_ref[r]
    for p in range(HEAD_PAIRS):
        cols = slice(p * LANES, (p + 1) * LANES)
        o = _pair_attention(
            q_ref[0, :, cols],
            [k_ref[0, pl.ds(start, kh * GRID_W), cols], kc_ref[0, :, cols]],
            [v_ref[0, pl.ds(start, kh * GRID_W), cols], vc_ref[0, :, cols]],
            [bias_ref[rtype, p], None],
        )
        o_ref[0, :, cols] = o.astype(o_ref.dtype)


def _neighbourhood_attention(proj, proj_c, bias_tab, r0, row_type, kh):
    b, t, _ = proj.shape
    l = proj_c.shape[1]
    rows = t // GRID_W
    w = NA_WIDTH
    grid_spec = pltpu.PrefetchScalarGridSpec(
        num_scalar_prefetch=2,
        grid=(b, rows),
        in_specs=[
            pl.BlockSpec((1, GRID_W, w), lambda bi, r, *_: (bi, r, 0)),
            pl.BlockSpec((1, t, w), lambda bi, r, *_: (bi, 0, 1)),
            pl.BlockSpec((1, t, w), lambda bi, r, *_: (bi, 0, 2)),
            pl.BlockSpec((1, l, w), lambda bi, r, *_: (bi, 0, 1)),
            pl.BlockSpec((1, l, w), lambda bi, r, *_: (bi, 0, 2)),
            pl.BlockSpec(bias_tab.shape, lambda bi, r, *_: (0, 0, 0, 0)),
        ],
        out_specs=pl.BlockSpec((1, GRID_W, w), lambda bi, r, *_: (bi, r, 0)),
    )
    return pl.pallas_call(
        functools.partial(_na_kernel, kh=kh),
        out_shape=jax.ShapeDtypeStruct((b, t, w), BF16),
        grid_spec=grid_spec,
        compiler_params=_params(("parallel", "arbitrary")),
        name="na_attention",
    )(jnp.asarray(r0), jnp.asarray(row_type), proj, proj, proj, proj_c, proj_c, bias_tab)


def _ctx_attn_kernel(q_ref, k_ref, v_ref, o_ref):
    for p in range(HEAD_PAIRS):
        cols = slice(p * LANES, (p + 1) * LANES)
        o = _pair_attention(q_ref[0, :, cols], [k_ref[0, :, cols]], [v_ref[0, :, cols]], [None])
        o_ref[0, :, cols] = o.astype(o_ref.dtype)


def _context_attention(proj_c):
    b, l, _ = proj_c.shape
    w = NA_WIDTH
    return pl.pallas_call(
        _ctx_attn_kernel,
        out_shape=jax.ShapeDtypeStruct((b, l, w), BF16),
        grid=(b,),
        in_specs=[pl.BlockSpec((1, l, w), lambda bi, j=j: (bi, 0, j)) for j in range(3)],
        out_specs=pl.BlockSpec((1, l, w), lambda bi: (bi, 0, 0)),
        compiler_params=_params(("parallel",)),
        name="ctx_attention",
    )(proj_c, proj_c, proj_c)


def _top4(logits):
    lane = lax.broadcasted_iota(jnp.int32, logits.shape, 1)
    cur = jnp.where(lane < N_EXPERTS, logits, -jnp.inf)
    vals, idxs = [], []
    for _ in range(TOP_K):
        m = jnp.max(cur, axis=-1, keepdims=True)
        first = jnp.min(jnp.where(cur == m, lane, ROUTER_PAD).astype(F32), axis=-1, keepdims=True)
        idx = first.astype(jnp.int32)
        vals.append(m)
        idxs.append(idx)
        cur = jnp.where(lane == idx, -jnp.inf, cur)
    exps = [jnp.exp(v - vals[0]) for v in vals]
    inv = 1.0 / functools.reduce(jnp.add, exps)
    ids = jnp.zeros(logits.shape, jnp.int32)
    gates = jnp.zeros(logits.shape, F32)
    for k in range(TOP_K):
        ids = jnp.where(lane == k, idxs[k], ids)
        gates = jnp.where(lane == k, exps[k] * inv, gates)
    return ids, gates


def _layer_tail(y, x_ref, g1_ref, n2_ref, sh2_ref, sc2_ref, wr_ref, br_ref, xo_ref, h2_ref, ids_ref, gates_ref):
    x_new = x_ref[0] + g1_ref[0] * y
    xo_ref[0] = x_new
    h2 = _rms_mod(x_new, n2_ref[...], sh2_ref[0], sc2_ref[0])
    h2b = h2.astype(BF16)
    h2_ref[0] = h2b
    logits = _dot(h2b, wr_ref[...]) + br_ref[...]
    ids, gates = _top4(logits)
    ids_ref[0] = ids
    gates_ref[0] = gates


def _tail_specs(b, t, d, tm):
    row = lambda bi, i: (bi, i, 0)
    per_b = lambda bi, i: (bi, 0, 0)
    const = lambda bi, i: (0, 0)
    in_specs = [
        pl.BlockSpec((1, tm, d), row),
        pl.BlockSpec((1, 1, d), per_b),
        pl.BlockSpec((1, d), const),
        pl.BlockSpec((1, 1, d), per_b),
        pl.BlockSpec((1, 1, d), per_b),
        pl.BlockSpec((d, ROUTER_PAD), const),
        pl.BlockSpec((1, ROUTER_PAD), const),
    ]
    out_specs = [
        pl.BlockSpec((1, tm, d), row),
        pl.BlockSpec((1, tm, d), row),
        pl.BlockSpec((1, tm, ROUTER_PAD), row),
        pl.BlockSpec((1, tm, ROUTER_PAD), row),
    ]
    out_shape = [
        jax.ShapeDtypeStruct((b, t, d), F32),
        jax.ShapeDtypeStruct((b, t, d), BF16),
        jax.ShapeDtypeStruct((b, t, ROUTER_PAD), jnp.int32),
        jax.ShapeDtypeStruct((b, t, ROUTER_PAD), F32),
    ]
    return in_specs, out_specs, out_shape


def _halo_fix(rolled, row, at_row, halo_row, present):
    fill = jnp.where(present, halo_row, jnp.zeros_like(halo_row))
    return jnp.where(row == at_row, fill, rolled)


def _even_out_kernel(oa_ref, bg_ref, cg_ref, xin_ref, cgp_ref, xinp_ref, cgn_ref, xinn_ref, cw_ref, cb_ref,
                     wa_ref, wb_ref, *tail_refs, tm):
    i = pl.program_id(1)
    has_prev = i > 0
    has_next = i < pl.num_programs(1) - 1
    u = cg_ref[0].astype(F32) * xin_ref[0].astype(F32)
    u_prev = (cgp_ref[0].astype(F32) * xinp_ref[0].astype(F32))[HALO - 1:HALO]
    u_next = (cgn_ref[0].astype(F32) * xinn_ref[0].astype(F32))[0:1]
    row = lax.broadcasted_iota(jnp.int32, u.shape, 0)
    u_m1 = _halo_fix(pltpu.roll(u, 1, 0), row, 0, u_prev, has_prev)
    u_p1 = _halo_fix(pltpu.roll(u, tm - 1, 0), row, tm - 1, u_next, has_next)
    cw = cw_ref[...]
    conv = u_m1 * cw[0:1] + u * cw[1:2] + u_p1 * cw[2:3] + cb_ref[...]
    o_b = bg_ref[0].astype(F32) * conv
    y = _dot(oa_ref[0], wa_ref[...]) + _dot(o_b.astype(BF16), wb_ref[...])
    _layer_tail(y, *tail_refs)


def _even_out(o_a, proj, conv_w, conv_b, w_out, x, g1, n2, sh2, sc2, w_r, b_r):
    b, t, d = x.shape
    w = NA_WIDTH
    tm = min(t, 512)
    hb = tm // HALO
    n_hblocks = t // HALO
    row = lambda bi, i: (bi, i, 0)
    const = lambda bi, i: (0, 0)
    prev = lambda col: (lambda bi, i: (bi, jnp.maximum(i * hb - 1, 0), col))
    nxt = lambda col: (lambda bi, i: (bi, jnp.minimum((i + 1) * hb, n_hblocks - 1), col))
    tail_in, out_specs, out_shape = _tail_specs(b, t, d, tm)
    in_specs = [
        pl.BlockSpec((1, tm, w), row),
        pl.BlockSpec((1, tm, w), lambda bi, i: (bi, i, 3)),
        pl.BlockSpec((1, tm, w), lambda bi, i: (bi, i, 4)),
        pl.BlockSpec((1, tm, w), lambda bi, i: (bi, i, 5)),
        pl.BlockSpec((1, HALO, w), prev(4)),
        pl.BlockSpec((1, HALO, w), prev(5)),
        pl.BlockSpec((1, HALO, w), nxt(4)),
        pl.BlockSpec((1, HALO, w), nxt(5)),
        pl.BlockSpec((SC_CONV, w), const),
        pl.BlockSpec((1, w), const),
        pl.BlockSpec((w, d), const),
        pl.BlockSpec((w, d), const),
    ] + tail_in
    return pl.pallas_call(
        functools.partial(_even_out_kernel, tm=tm),
        out_shape=out_shape,
        grid=(b, t // tm),
        in_specs=in_specs,
        out_specs=out_specs,
        compiler_params=_params(("parallel", "parallel")),
        name="even_out",
    )(o_a, proj, proj, proj, proj, proj, proj, proj, conv_w, conv_b.reshape(1, w),
      w_out[:w], w_out[w:], x, g1, n2, sh2, sc2, w_r, b_r)


def _log_sigmoid(x):
    return jnp.minimum(x, 0.0) - jnp.log1p(jnp.exp(-jnp.abs(x)))


def _lru_tile(u_ref, up_ref, un_ref, has_prev, has_next, cw_ref, cb_ref, w_ref, ba_ref, bx_ref, lam_ref,
              carry_ref, o_ref, *, ts, reverse):
    width = u_ref.shape[-1]
    blk = width // LRU_BLOCKS
    u = u_ref[0].astype(F32)
    up = up_ref[0].astype(F32)
    un = un_ref[0].astype(F32)
    row = lax.broadcasted_iota(jnp.int32, u.shape, 0)
    u_m1 = _halo_fix(pltpu.roll(u, 1, 0), row, 0, up[HALO - 1:HALO], has_prev)
    u_p1 = _halo_fix(pltpu.roll(u, ts - 1, 0), row, ts - 1, un[0:1], has_next)
    u_p2 = _halo_fix(pltpu.roll(u, ts - 2, 0), row, ts - 2, un[0:1], has_next)
    u_p2 = _halo_fix(u_p2, row, ts - 1, un[1:2], has_next)
    cw = cw_ref[...]
    uc = u_m1 * cw[0:1] + u * cw[1:2] + u_p1 * cw[2:3] + u_p2 * cw[3:4] + cb_ref[...]
    ucb = uc.astype(BF16)
    za, zx = [], []
    for h in range(LRU_BLOCKS):
        z = _dot(ucb[:, h * blk:(h + 1) * blk], w_ref[h])
        za.append(z[:, :blk])
        zx.append(z[:, blk:])
    r = jax.nn.sigmoid(jnp.concatenate(za, axis=1) + ba_ref[...])
    gate_i = jax.nn.sigmoid(jnp.concatenate(zx, axis=1) + bx_ref[...])
    log_a = (LRU_C * _log_sigmoid(lam_ref[...])) * r
    a = jnp.exp(log_a)
    th = jnp.tanh(log_a)
    bcoef = jnp.sqrt(-2.0 * th / (1.0 - th)) * gate_i * uc

    groups = ts // SUBLANES
    a3 = a.reshape(groups, SUBLANES, width)
    b3 = bcoef.reshape(groups, SUBLANES, width)
    sub = lax.broadcasted_iota(jnp.int32, a3.shape, 1)
    for s in (1, 2, 4):
        shift = SUBLANES - s if reverse else s
        a_sh = pltpu.roll(a3, shift, 1)
        b_sh = pltpu.roll(b3, shift, 1)
        m = (sub < SUBLANES - s) if reverse else (sub >= s)
        b3 = jnp.where(m, a3 * b_sh + b3, b3)
        a3 = jnp.where(m, a3 * a_sh, a3)

    h = carry_ref[0:1, :]
    order = range(groups - 1, -1, -1) if reverse else range(groups)
    edge = 0 if reverse else SUBLANES - 1
    for g in order:
        hg = a3[g] * h + b3[g]
        if o_ref is not None:
            o_ref[0, g * SUBLANES:(g + 1) * SUBLANES, :] = hg.astype(o_ref.dtype)
        h = hg[edge:edge + 1, :]
    carry_ref[...] = jnp.broadcast_to(h, carry_ref.shape)


def _lru_kernel(uc_ref, ucp_ref, ucn_ref, ul_ref, ulp_ref, uln_ref, cw_ref, cb_ref, w_ref, ba_ref, bx_ref,
                lam_ref, o_ref, carry_ref, *, ts, n_ctx_tiles, n_lat_tiles, reverse):
    j = pl.program_id(1)
    shared = (cw_ref, cb_ref, w_ref, ba_ref, bx_ref, lam_ref, carry_ref)

    @pl.when(j == 0)
    def _():
        carry_ref[...] = jnp.zeros_like(carry_ref)

    def tile_pos(step, n):
        return (n - 1 - step) if reverse else step

    @pl.when(j < n_ctx_tiles)
    def _():
        pos = tile_pos(j, n_ctx_tiles)
        _lru_tile(uc_ref, ucp_ref, ucn_ref, pos > 0, pos < n_ctx_tiles - 1, *shared, None, ts=ts, reverse=reverse)

    @pl.when(j >= n_ctx_tiles)
    def _():
        pos = tile_pos(j - n_ctx_tiles, n_lat_tiles)
        _lru_tile(ul_ref, ulp_ref, uln_ref, pos > 0, pos < n_lat_tiles - 1, *shared, o_ref, ts=ts, reverse=reverse)


def _lru_scan(proj, u_ctx, conv_w, conv_b, w_cat, ba, bx, lam, reverse):
    b, t, _ = proj.shape
    l, width = u_ctx.shape[1], u_ctx.shape[2]
    ts = min(256, l, t)
    n_c, n_l = l // ts, t // ts
    hb = ts // HALO

    def pos_of(step, n):
        step = jnp.clip(step, 0, n - 1)
        return (n - 1 - step) if reverse else step

    def tile_map(off, n, col):
        return lambda bi, j: (bi, pos_of(j - off, n), col)

    def prev_map(off, n, col):
        return lambda bi, j: (bi, jnp.maximum(pos_of(j - off, n) * hb - 1, 0), col)

    def next_map(off, n, col):
        return lambda bi, j: (bi, jnp.minimum((pos_of(j - off, n) + 1) * hb, n * hb - 1), col)

    const2 = lambda bi, j: (0, 0)
    in_specs = [
        pl.BlockSpec((1, ts, width), tile_map(0, n_c, 0)),
        pl.BlockSpec((1, HALO, width), prev_map(0, n_c, 0)),
        pl.BlockSpec((1, HALO, width), next_map(0, n_c, 0)),
        pl.BlockSpec((1, ts, width), tile_map(n_c, n_l, 1)),
        pl.BlockSpec((1, HALO, width), prev_map(n_c, n_l, 1)),
        pl.BlockSpec((1, HALO, width), next_map(n_c, n_l, 1)),
        pl.BlockSpec((LRU_CONV, width), const2),
        pl.BlockSpec((1, width), const2),
        pl.BlockSpec(w_cat.shape, lambda bi, j: (0, 0, 0)),
        pl.BlockSpec((1, width), const2),
        pl.BlockSpec((1, width), const2),
        pl.BlockSpec((1, width), const2),
    ]
    kern = functools.partial(_lru_kernel, ts=ts, n_ctx_tiles=n_c, n_lat_tiles=n_l, reverse=reverse)
    return pl.pallas_call(
        kern,
        out_shape=jax.ShapeDtypeStruct((b, t, width), BF16),
        grid=(b, n_c + n_l),
        in_specs=in_specs,
        out_specs=pl.BlockSpec((1, ts, width), tile_map(n_c, n_l, 0)),
        scratch_shapes=[pltpu.VMEM((SUBLANES, width), F32)],
        compiler_params=_params(("parallel", "arbitrary")),
        name="lru_scan_bwd" if reverse else "lru_scan_fwd",
    )(u_ctx, u_ctx, u_ctx, proj, proj, proj, conv_w, conv_b.reshape(1, width), w_cat,
      ba.reshape(1, width), bx.reshape(1, width), lam.reshape(1, width))


def _odd_out_kernel(hf_ref, hb_ref, gate_ref, w_ref, *tail_refs):
    hsum = hf_ref[0].astype(F32) + hb_ref[0].astype(F32)
    z = hsum * jax.nn.gelu(gate_ref[0].astype(F32), approximate=True)
    y = _dot(z.astype(BF16), w_ref[...])
    _layer_tail(y, *tail_refs)


def _odd_out(h_f, h_b, proj, w_out, x, g1, n2, sh2, sc2, w_r, b_r):
    b, t, d = x.shape
    width = h_f.shape[-1]
    tm = min(t, 512)
    row = lambda bi, i: (bi, i, 0)
    tail_in, out_specs, out_shape = _tail_specs(b, t, d, tm)
    in_specs = [
        pl.BlockSpec((1, tm, width), row),
        pl.BlockSpec((1, tm, width), row),
        pl.BlockSpec((1, tm, width), row),
        pl.BlockSpec((width, d), lambda bi, i: (0, 0)),
    ] + tail_in
    return pl.pallas_call(
        _odd_out_kernel,
        out_shape=out_shape,
        grid=(b, t // tm),
        in_specs=in_specs,
        out_specs=out_specs,
        compiler_params=_params(("parallel", "parallel")),
        name="odd_out",
    )(h_f, h_b, proj, w_out, x, g1, n2, sh2, sc2, w_r, b_r)


def _expert_kernel(be_ref, nb_ref, x_ref, wg_ref, bg_ref, wu_ref, bu_ref, wd_ref, bd_ref, o_ref, *, chunk):
    i = pl.program_id(0)

    @pl.when(i < nb_ref[0])
    def _():
        x = x_ref[...]
        d_exp = wg_ref.shape[2]
        acc = jnp.zeros((x.shape[0], wd_ref.shape[2]), F32)
        for c in range(d_exp // chunk):
            cs = slice(c * chunk, (c + 1) * chunk)
            g = jnp.minimum(_dot(x, wg_ref[0, :, cs]) + bg_ref[0, :, cs], SWIGLU_LIMIT)
            u = jnp.clip(_dot(x, wu_ref[0, :, cs]) + bu_ref[0, :, cs], -SWIGLU_LIMIT, SWIGLU_LIMIT)
            h = g * jax.nn.sigmoid(SWIGLU_ALPHA * g) * (u + 1.0)
            acc = acc + _dot(h.astype(BF16), wd_ref[0, cs, :])
        o_ref[...] = (acc + bd_ref[0]).astype(o_ref.dtype)

    @pl.when(i >= nb_ref[0])
    def _():
        o_ref[...] = jnp.zeros_like(o_ref)


def _experts(xb, block_e, n_used, wg, bg, wu, bu, wd, bd):
    n_slots, d = xb.shape
    n_e, _, d_exp = wg.shape
    tm = EXPERT_TILE
    n_blocks = n_slots // tm
    xmap = lambda i, be, nb: (jnp.minimum(i, nb[0] - 1), 0)
    wmap = lambda i, be, nb: (be[i], 0, 0)
    grid_spec = pltpu.PrefetchScalarGridSpec(
        num_scalar_prefetch=2,
        grid=(n_blocks,),
        in_specs=[
            pl.BlockSpec((tm, d), xmap),
            pl.BlockSpec((1, d, d_exp), wmap),
            pl.BlockSpec((1, 1, d_exp), wmap),
            pl.BlockSpec((1, d, d_exp), wmap),
            pl.BlockSpec((1, 1, d_exp), wmap),
            pl.BlockSpec((1, d_exp, d), wmap),
            pl.BlockSpec((1, 1, d), wmap),
        ],
        out_specs=pl.BlockSpec((tm, d), lambda i, be, nb: (i, 0)),
    )
    return pl.pallas_call(
        functools.partial(_expert_kernel, chunk=256),
        out_shape=jax.ShapeDtypeStruct((n_slots, d), BF16),
        grid_spec=grid_spec,
        compiler_params=_params(("arbitrary",)),
        name="experts",
    )(block_e, n_used, xb, wg, bg.reshape(n_e, 1, d_exp), wu, bu.reshape(n_e, 1, d_exp), wd, bd.reshape(n_e, 1, d))


def _combine_kernel(x_ref, g2_ref, gates_ref, y_ref, o_ref):
    gates = gates_ref[0]
    acc = jnp.zeros(x_ref.shape[1:], F32)
    for k in range(TOP_K):
        acc = acc + gates[:, k:k + 1] * y_ref[k].astype(F32)
    o_ref[0] = x_ref[0] + g2_ref[0] * acc


def _combine(x, g2, gates, y_sel, tok_offset):
    b, t, d = x.shape
    tm = int(np.gcd(min(t, 512), tok_offset)) if tok_offset else min(t, 512)
    n_t = t // tm
    off = tok_offset // tm
    return pl.pallas_call(
        _combine_kernel,
        out_shape=jax.ShapeDtypeStruct((b, t, d), F32),
        grid=(b, n_t),
        in_specs=[
            pl.BlockSpec((1, tm, d), lambda bi, i: (bi, i, 0)),
            pl.BlockSpec((1, 1, d), lambda bi, i: (bi, 0, 0)),
            pl.BlockSpec((1, tm, ROUTER_PAD), lambda bi, i: (bi, i, 0)),
            pl.BlockSpec((TOP_K, tm, d), lambda bi, i: (0, off + bi * n_t + i, 0)),
        ],
        out_specs=pl.BlockSpec((1, tm, d), lambda bi, i: (bi, i, 0)),
        compiler_params=_params(("parallel", "parallel")),
        name="moe_combine",
    )(x, g2, gates, y_sel)


def _moe(h2, ids, w_exp):
    wg, bg, wu, bu, wd, bd = w_exp
    n, d = h2.shape
    nk = n * TOP_K
    tm = EXPERT_TILE
    e_flat = ids.reshape(-1)
    onehot = (e_flat[:, None] == jnp.arange(N_EXPERTS, dtype=jnp.int32)[None]).astype(jnp.int32)
    csum = jnp.cumsum(onehot, axis=0)
    rank = jnp.take_along_axis(csum, e_flat[:, None], axis=1)[:, 0] - 1
    counts = csum[-1]
    padded = (counts + tm - 1) // tm * tm
    pends = jnp.cumsum(padded)
    dest = (pends - padded)[e_flat] + rank
    n_blocks = -(-nk // tm) + N_EXPERTS
    n_slots = n_blocks * tm
    tok = jnp.arange(nk, dtype=jnp.int32) // TOP_K
    slot_tok = jnp.zeros((n_slots,), jnp.int32).at[dest].set(tok)
    block_e = jnp.minimum(jnp.searchsorted(pends, jnp.arange(n_blocks, dtype=jnp.int32) * tm, side='right'),
                          N_EXPERTS - 1).astype(jnp.int32)
    n_used = (pends[-1] // tm).astype(jnp.int32).reshape(1)
    xb = h2[slot_tok]
    yb = _experts(xb, block_e, n_used, wg, bg, wu, bu, wd, bd)
    return yb[dest.reshape(n, TOP_K).T]


def _mod_parts(mod_l, b):
    d = mod_l.shape[-1] // 6
    lat = [mod_l[:b, k * d:(k + 1) * d].reshape(b, 1, d) for k in range(6)]
    ctx = [jnp.broadcast_to(mod_l[b, k * d:(k + 1) * d].reshape(1, 1, d), (b, 1, d)) for k in range(6)]
    return lat, ctx


def kernel(x, c, ctx, c_ctx, ada_w, ada_b, norm1_g, norm2_g, ev_w_in, ev_w_out, ev_q_gain, ev_k_gain, ev_rpb, ev_conv_w, ev_conv_b, od_w_in, od_w_out, od_conv_w, od_conv_b, od_fwd_wa, od_fwd_ba, od_fwd_wx, od_fwd_bx, od_fwd_lam, od_bwd_wa, od_bwd_ba, od_bwd_wx, od_bwd_bx, od_bwd_lam, router_w, router_b, exp_w_gate, exp_b_gate, exp_w_up, exp_b_up, exp_w_down, exp_b_down):
    b, t, d = x.shape
    l = ctx.shape[1]
    assert ada_w.shape[0] == DEPTH == 2 and t % GRID_W == 0 and t // GRID_W >= WIN_H

    n_rows_c = -(-(b + 1) // SUBLANES) * SUBLANES
    cvec = jnp.zeros((n_rows_c, d), F32).at[:b].set(c).at[b].set(c_ctx)
    mod = _ada_mod(cvec, ada_w, ada_b)

    def router(layer):
        w_r = jnp.zeros((d, ROUTER_PAD), F32).at[:, :N_EXPERTS].set(router_w[layer]).astype(BF16)
        b_r = jnp.zeros((1, ROUTER_PAD), F32).at[0, :N_EXPERTS].set(router_b[layer])
        return w_r, b_r

    def experts(layer):
        return (exp_w_gate[layer].astype(BF16), exp_b_gate[layer], exp_w_up[layer].astype(BF16), exp_b_up[layer],
                exp_w_down[layer].astype(BF16), exp_b_down[layer])

    (sh1, sc1, g1, sh2, sc2, g2), (csh1, csc1, cg1, csh2, csc2, cg2) = _mod_parts(mod[0], b)
    n1 = norm1_g[0].reshape(1, d)
    n2 = norm2_g[0].reshape(1, d)
    w_in = ev_w_in[0].astype(BF16)
    w_out = ev_w_out[0].astype(BF16)
    q_scale = NA_HEAD_DIM ** -0.5
    head_gain = jnp.stack([jnp.tile(ev_q_gain[0] * q_scale, NA_HEADS), jnp.tile(ev_k_gain[0], NA_HEADS)])
    head_gain = head_gain.reshape(2, 1, NA_WIDTH).astype(F32)
    proj = _inproj(x, n1, sh1, sc1, w_in, head_gain)
    proj_c = _inproj(ctx, n1, csh1, csc1, w_in, head_gain)
    kh, r0, row_type, patterns = _na_tables(t // GRID_W)
    bias_tab = _na_bias_table(ev_rpb[0], patterns)
    o_a = _neighbourhood_attention(proj, proj_c, bias_tab, r0, row_type, kh)
    oc_a = _context_attention(proj_c)
    w_r, b_r = router(0)
    x1, h2, ids, gates = _even_out(o_a, proj, ev_conv_w[0], ev_conv_b[0], w_out, x, g1, n2, sh2, sc2, w_r, b_r)
    c1, h2c, ids_c, gates_c = _even_out(oc_a, proj_c, ev_conv_w[0], ev_conv_b[0], w_out, ctx, cg1, n2, csh2, csc2,
                                        w_r, b_r)
    tokens = jnp.concatenate([h2c.reshape(b * l, d), h2.reshape(b * t, d)], axis=0)
    ids_all = jnp.concatenate([ids_c.reshape(b * l, ROUTER_PAD), ids.reshape(b * t, ROUTER_PAD)], axis=0)[:, :TOP_K]
    y_sel = _moe(tokens, ids_all, experts(0))
    hctx = _combine(c1, cg2, gates_c, y_sel, 0)
    x = _combine(x1, g2, gates, y_sel, b * l)

    (sh1, sc1, g1, sh2, sc2, g2), (csh1, csc1, _, _, _, _) = _mod_parts(mod[1], b)
    n1 = norm1_g[1].reshape(1, d)
    n2 = norm2_g[1].reshape(1, d)
    w_in = od_w_in[0].astype(BF16)
    width = w_in.shape[1] // 2
    proj = _inproj(x, n1, sh1, sc1, w_in)
    u_ctx = _inproj(hctx, n1, csh1, csc1, w_in[:, width:])
    h_dir = []
    for reverse, (wa, ba, wx, bx, lam) in ((False, (od_fwd_wa, od_fwd_ba, od_fwd_wx, od_fwd_bx, od_fwd_lam)),
                                           (True, (od_bwd_wa, od_bwd_ba, od_bwd_wx, od_bwd_bx, od_bwd_lam))):
        w_cat = jnp.concatenate([wa[0], wx[0]], axis=-1).astype(BF16)
        h_dir.append(_lru_scan(proj, u_ctx, od_conv_w[0], od_conv_b[0], w_cat, ba[0], bx[0], lam[0], reverse))
    w_r, b_r = router(1)
    x1, h2, ids, gates = _odd_out(h_dir[0], h_dir[1], proj, od_w_out[0].astype(BF16), x, g1, n2, sh2, sc2, w_r, b_r)
    y_sel = _moe(h2.reshape(b * t, d), ids.reshape(b * t, ROUTER_PAD)[:, :TOP_K], experts(1))
    return _combine(x1, g2, gates, y_sel, 0)
```

```python
import functools

import numpy as np
import jax
import jax.numpy as jnp
from jax import lax
from jax.experimental import pallas as pl
from jax.experimental.pallas import tpu as pltpu

DEPTH = 2
GRID_W = 64
EPS = 1e-6
NEG_INF = -1e30
NA_HEADS = 8
NA_HEAD_DIM = 64
NA_WIDTH = NA_HEADS * NA_HEAD_DIM
HEAD_PAIRS = NA_HEADS // 2
WIN_H = 8
WIN_W = 16
SC_CONV = 3
LRU_BLOCKS = 4
LRU_CONV = 4
LRU_C = 8.0
N_EXPERTS = 32
TOP_K = 4
SWIGLU_LIMIT = 7.0
SWIGLU_ALPHA = 1.702

LANES = 128
SUBLANES = 8
HALO = 16
ROUTER_PAD = LANES
EXPERT_TILE = 512
VMEM_LIMIT = 56 * 1024 * 1024

F32 = jnp.float32
BF16 = jnp.bfloat16


def _params(sem, vmem=VMEM_LIMIT):
    return pltpu.CompilerParams(dimension_semantics=sem, vmem_limit_bytes=vmem)


def _dot(a, b):
    return jnp.dot(a, b, preferred_element_type=F32)


def _dot_nt(a, b):
    return lax.dot_general(a, b, (((1,), (1,)), ((), ())), preferred_element_type=F32)


def _rms_mod(x, g, shift, scale):
    ms = jnp.mean(x * x, axis=-1, keepdims=True)
    y = x * lax.rsqrt(ms + EPS) * g
    return y * (1.0 + scale) + shift


def _ada_kernel(c_ref, w_ref, b_ref, o_ref):
    c = c_ref[...]
    s = (c * jax.nn.sigmoid(c)).astype(BF16)
    o_ref[0] = _dot(s, w_ref[0].astype(BF16)) + b_ref[0]


def _ada_mod(cvec, ada_w, ada_b):
    depth, d, n = ada_w.shape
    r = cvec.shape[0]
    tn = 1536
    return pl.pallas_call(
        _ada_kernel,
        out_shape=jax.ShapeDtypeStruct((depth, r, n), F32),
        grid=(depth, n // tn),
        in_specs=[
            pl.BlockSpec((r, d), lambda l, j: (0, 0)),
            pl.BlockSpec((1, d, tn), lambda l, j: (l, 0, j)),
            pl.BlockSpec((1, 1, tn), lambda l, j: (l, 0, j)),
        ],
        out_specs=pl.BlockSpec((1, r, tn), lambda l, j: (l, 0, j)),
        compiler_params=_params(("parallel", "parallel")),
        name="ada_mod",
    )(cvec, ada_w, ada_b.reshape(depth, 1, n))


def _inproj_kernel(x_ref, g_ref, sh_ref, sc_ref, w_ref, hg_ref, ones_ref, o_ref, *, n_tiles, tn, n_headnorm):
    h = _rms_mod(x_ref[0], g_ref[...], sh_ref[0], sc_ref[0]).astype(BF16)
    for j in range(n_tiles):
        y = _dot(h, w_ref[:, j * tn:(j + 1) * tn])
        if j < n_headnorm:
            ms = _dot((y * y).astype(BF16), ones_ref[...]) * (1.0 / NA_HEAD_DIM)
            y = y * lax.rsqrt(ms + EPS) * hg_ref[j]
        o_ref[0, :, j * tn:(j + 1) * tn] = y.astype(o_ref.dtype)


def _inproj(x, g, shift, scale, w, head_gain=None):
    b, t, d = x.shape
    n = w.shape[1]
    tn = NA_WIDTH
    tm = min(t, 512)
    n_headnorm = 0 if head_gain is None else head_gain.shape[0]
    if head_gain is None:
        head_gain = jnp.ones((1, 1, tn), F32)
    hid = np.arange(tn) // NA_HEAD_DIM
    ones_bd = jnp.asarray((hid[:, None] == hid[None, :]), BF16)
    kern = functools.partial(_inproj_kernel, n_tiles=n // tn, tn=tn, n_headnorm=n_headnorm)
    return pl.pallas_call(
        kern,
        out_shape=jax.ShapeDtypeStruct((b, t, n), BF16),
        grid=(b, t // tm),
        in_specs=[
            pl.BlockSpec((1, tm, d), lambda bi, i: (bi, i, 0)),
            pl.BlockSpec((1, d), lambda bi, i: (0, 0)),
            pl.BlockSpec((1, 1, d), lambda bi, i: (bi, 0, 0)),
            pl.BlockSpec((1, 1, d), lambda bi, i: (bi, 0, 0)),
            pl.BlockSpec((d, n), lambda bi, i: (0, 0)),
            pl.BlockSpec(head_gain.shape, lambda bi, i: (0, 0, 0)),
            pl.BlockSpec((tn, tn), lambda bi, i: (0, 0)),
        ],
        out_specs=pl.BlockSpec((1, tm, n), lambda bi, i: (bi, i, 0)),
        compiler_params=_params(("parallel", "parallel")),
        name="inproj",
    )(x, g, shift, scale, w, head_gain, ones_bd)


def _na_tables(rows):
    kh = min(WIN_H, rows)
    r = np.arange(rows)
    r0 = np.clip(r - kh // 2, 0, rows - kh)
    dr = r0[:, None] + np.arange(kh)[None] - r[:, None] + WIN_H - 1
    patterns, row_type = np.unique(dr, axis=0, return_inverse=True)
    return kh, r0.astype(np.int32), row_type.reshape(-1).astype(np.int32), patterns


def _na_bias_table(rpb, patterns):
    qc = np.arange(GRID_W)
    kc = np.arange(GRID_W)
    c0 = np.clip(qc - WIN_W // 2, 0, GRID_W - WIN_W)[:, None]
    valid = (kc[None] >= c0) & (kc[None] < c0 + WIN_W)
    dc = np.clip(kc[None] - qc[:, None] + WIN_W - 1, 0, 2 * WIN_W - 2)
    n_pat, kh = patterns.shape
    onehot_dc = jnp.asarray(dc[None] == np.arange(2 * WIN_W - 1)[:, None, None], F32)
    tab = jnp.einsum('hpic,cqk->hpiqk', rpb.astype(F32)[:, patterns], onehot_dc,
                     precision=lax.Precision.HIGHEST)
    tab = jnp.where(valid[None, None, None], tab, NEG_INF)
    tab = tab.reshape(HEAD_PAIRS, 2, n_pat, kh, GRID_W, GRID_W)
    tab = tab.transpose(2, 0, 1, 4, 3, 5)
    return tab.reshape(n_pat, HEAD_PAIRS, 2 * GRID_W, kh * GRID_W)


def _pair_attention(q, keys, values, biases):
    m = q.shape[0]
    lane = lax.broadcasted_iota(jnp.int32, q.shape, 1)
    zero = jnp.zeros_like(q)
    qq = jnp.concatenate([jnp.where(lane < NA_HEAD_DIM, q, zero), jnp.where(lane >= NA_HEAD_DIM, q, zero)], axis=0)
    scores = []
    for k, bias in zip(keys, biases):
        s = _dot_nt(qq, k)
        scores.append(s if bias is None else s + bias)
    mx = functools.reduce(jnp.maximum, [jnp.max(s, axis=-1, keepdims=True) for s in scores])
    exps = [jnp.exp(s - mx) for s in scores]
    denom = functools.reduce(jnp.add, [jnp.sum(e, axis=-1, keepdims=True) for e in exps])
    o = functools.reduce(jnp.add, [_dot(e.astype(BF16), v) for e, v in zip(exps, values)])
    o = o * (1.0 / denom)
    lane_o = lax.broadcasted_iota(jnp.int32, (m, LANES), 1)
    return jnp.where(lane_o < NA_HEAD_DIM, o[:m], o[m:])


def _na_kernel(r0_ref, type_ref, q_ref, k_ref, v_ref, kc_ref, vc_ref, bias_ref, o_ref, *, kh):
    r = pl.program_id(1)
    start = pl.multiple_of(r0_ref[r] * GRID_W, GRID_W)
    rtype = type_ref[r]
    for p in range(HEAD_PAIRS):
        cols = slice(p * LANES, (p + 1) * LANES)
        o = _pair_attention(
            q_ref[0, :, cols],
            [k_ref[0, pl.ds(start, kh * GRID_W), cols], kc_ref[0, :, cols]],
            [v_ref[0, pl.ds(start, kh * GRID_W), cols], vc_ref[0, :, cols]],
            [bias_ref[rtype, p], None],
        )
        o_ref[0, :, cols] = o.astype(o_ref.dtype)


def _neighbourhood_attention(proj, proj_c, bias_tab, r0, row_type, kh):
    b, t, _ = proj.shape
    l = proj_c.shape[1]
    rows = t // GRID_W
    w = NA_WIDTH
    grid_spec = pltpu.PrefetchScalarGridSpec(
        num_scalar_prefetch=2,
        grid=(b, rows),
        in_specs=[
            pl.BlockSpec((1, GRID_W, w), lambda bi, r, *_: (bi, r, 0)),
            pl.BlockSpec((1, t, w), lambda bi, r, *_: (bi, 0, 1)),
            pl.BlockSpec((1, t, w), lambda bi, r, *_: (bi, 0, 2)),
            pl.BlockSpec((1, l, w), lambda bi, r, *_: (bi, 0, 1)),
            pl.BlockSpec((1, l, w), lambda bi, r, *_: (bi, 0, 2)),
            pl.BlockSpec(bias_tab.shape, lambda bi, r, *_: (0, 0, 0, 0)),
        ],
        out_specs=pl.BlockSpec((1, GRID_W, w), lambda bi, r, *_: (bi, r, 0)),
    )
    return pl.pallas_call(
        functools.partial(_na_kernel, kh=kh),
        out_shape=jax.ShapeDtypeStruct((b, t, w), BF16),
        grid_spec=grid_spec,
        compiler_params=_params(("parallel", "arbitrary")),
        name="na_attention",
    )(jnp.asarray(r0), jnp.asarray(row_type), proj, proj, proj, proj_c, proj_c, bias_tab)


def _ctx_attn_kernel(q_ref, k_ref, v_ref, o_ref):
    for p in range(HEAD_PAIRS):
        cols = slice(p * LANES, (p + 1) * LANES)
        o = _pair_attention(q_ref[0, :, cols], [k_ref[0, :, cols]], [v_ref[0, :, cols]], [None])
        o_ref[0, :, cols] = o.astype(o_ref.dtype)


def _context_attention(proj_c):
    b, l, _ = proj_c.shape
    w = NA_WIDTH
    return pl.pallas_call(
        _ctx_attn_kernel,
        out_shape=jax.ShapeDtypeStruct((b, l, w), BF16),
        grid=(b,),
        in_specs=[pl.BlockSpec((1, l, w), lambda bi, j=j: (bi, 0, j)) for j in range(3)],
        out_specs=pl.BlockSpec((1, l, w), lambda bi: (bi, 0, 0)),
        compiler_params=_params(("parallel",)),
        name="ctx_attention",
    )(proj_c, proj_c, proj_c)


def _top4(logits):
    lane = lax.broadcasted_iota(jnp.int32, logits.shape, 1)
    cur = jnp.where(lane < N_EXPERTS, logits, -jnp.inf)
    vals, idxs = [], []
    for _ in range(TOP_K):
        m = jnp.max(cur, axis=-1, keepdims=True)
        first = jnp.min(jnp.where(cur == m, lane, ROUTER_PAD).astype(F32), axis=-1, keepdims=True)
        idx = first.astype(jnp.int32)
        vals.append(m)
        idxs.append(idx)
        cur = jnp.where(lane == idx, -jnp.inf, cur)
    exps = [jnp.exp(v - vals[0]) for v in vals]
    inv = 1.0 / functools.reduce(jnp.add, exps)
    ids = jnp.zeros(logits.shape, jnp.int32)
    gates = jnp.zeros(logits.shape, F32)
    for k in range(TOP_K):
        ids = jnp.where(lane == k, idxs[k], ids)
        gates = jnp.where(lane == k, exps[k] * inv, gates)
    return ids, gates


def _layer_tail(y, x_ref, g1_ref, n2_ref, sh2_ref, sc2_ref, wr_ref, br_ref, xo_ref, h2_ref, ids_ref, gates_ref):
    x_new = x_ref[0] + g1_ref[0] * y
    xo_ref[0] = x_new
    h2 = _rms_mod(x_new, n2_ref[...], sh2_ref[0], sc2_ref[0])
    h2b = h2.astype(BF16)
    h2_ref[0] = h2b
    logits = _dot(h2b, wr_ref[...]) + br_ref[...]
    ids, gates = _top4(logits)
    ids_ref[0] = ids
    gates_ref[0] = gates


def _tail_specs(b, t, d, tm):
    row = lambda bi, i: (bi, i, 0)
    per_b = lambda bi, i: (bi, 0, 0)
    const = lambda bi, i: (0, 0)
    in_specs = [
        pl.BlockSpec((1, tm, d), row),
        pl.BlockSpec((1, 1, d), per_b),
        pl.BlockSpec((1, d), const),
        pl.BlockSpec((1, 1, d), per_b),
        pl.BlockSpec((1, 1, d), per_b),
        pl.BlockSpec((d, ROUTER_PAD), const),
        pl.BlockSpec((1, ROUTER_PAD), const),
    ]
    out_specs = [
        pl.BlockSpec((1, tm, d), row),
        pl.BlockSpec((1, tm, d), row),
        pl.BlockSpec((1, tm, ROUTER_PAD), row),
        pl.BlockSpec((1, tm, ROUTER_PAD), row),
    ]
    out_shape = [
        jax.ShapeDtypeStruct((b, t, d), F32),
        jax.ShapeDtypeStruct((b, t, d), BF16),
        jax.ShapeDtypeStruct((b, t, ROUTER_PAD), jnp.int32),
        jax.ShapeDtypeStruct((b, t, ROUTER_PAD), F32),
    ]
    return in_specs, out_specs, out_shape


def _halo_fix(rolled, row, at_row, halo_row, present):
    fill = jnp.where(present, halo_row, jnp.zeros_like(halo_row))
    return jnp.where(row == at_row, fill, rolled)


def _even_out_kernel(oa_ref, bg_ref, cg_ref, xin_ref, cgp_ref, xinp_ref, cgn_ref, xinn_ref, cw_ref, cb_ref,
                     wa_ref, wb_ref, *tail_refs, tm):
    i = pl.program_id(1)
    has_prev = i > 0
    has_next = i < pl.num_programs(1) - 1
    u = cg_ref[0].astype(F32) * xin_ref[0].astype(F32)
    u_prev = (cgp_ref[0].astype(F32) * xinp_ref[0].astype(F32))[HALO - 1:HALO]
    u_next = (cgn_ref[0].astype(F32) * xinn_ref[0].astype(F32))[0:1]
    row = lax.broadcasted_iota(jnp.int32, u.shape, 0)
    u_m1 = _halo_fix(pltpu.roll(u, 1, 0), row, 0, u_prev, has_prev)
    u_p1 = _halo_fix(pltpu.roll(u, tm - 1, 0), row, tm - 1, u_next, has_next)
    cw = cw_ref[...]
    conv = u_m1 * cw[0:1] + u * cw[1:2] + u_p1 * cw[2:3] + cb_ref[...]
    o_b = bg_ref[0].astype(F32) * conv
    y = _dot(oa_ref[0], wa_ref[...]) + _dot(o_b.astype(BF16), wb_ref[...])
    _layer_tail(y, *tail_refs)


def _even_out(o_a, proj, conv_w, conv_b, w_out, x, g1, n2, sh2, sc2, w_r, b_r):
    b, t, d = x.shape
    w = NA_WIDTH
    tm = min(t, 512)
    hb = tm // HALO
    n_hblocks = t // HALO
    row = lambda bi, i: (bi, i, 0)
    const = lambda bi, i: (0, 0)
    prev = lambda col: (lambda bi, i: (bi, jnp.maximum(i * hb - 1, 0), col))
    nxt = lambda col: (lambda bi, i: (bi, jnp.minimum((i + 1) * hb, n_hblocks - 1), col))
    tail_in, out_specs, out_shape = _tail_specs(b, t, d, tm)
    in_specs = [
        pl.BlockSpec((1, tm, w), row),
        pl.BlockSpec((1, tm, w), lambda bi, i: (bi, i, 3)),
        pl.BlockSpec((1, tm, w), lambda bi, i: (bi, i, 4)),
        pl.BlockSpec((1, tm, w), lambda bi, i: (bi, i, 5)),
        pl.BlockSpec((1, HALO, w), prev(4)),
        pl.BlockSpec((1, HALO, w), prev(5)),
        pl.BlockSpec((1, HALO, w), nxt(4)),
        pl.BlockSpec((1, HALO, w), nxt(5)),
        pl.BlockSpec((SC_CONV, w), const),
        pl.BlockSpec((1, w), const),
        pl.BlockSpec((w, d), const),
        pl.BlockSpec((w, d), const),
    ] + tail_in
    return pl.pallas_call(
        functools.partial(_even_out_kernel, tm=tm),
        out_shape=out_shape,
        grid=(b, t // tm),
        in_specs=in_specs,
        out_specs=out_specs,
        compiler_params=_params(("parallel", "parallel")),
        name="even_out",
    )(o_a, proj, proj, proj, proj, proj, proj, proj, conv_w, conv_b.reshape(1, w),
      w_out[:w], w_out[w:], x, g1, n2, sh2, sc2, w_r, b_r)


def _log_sigmoid(x):
    return jnp.minimum(x, 0.0) - jnp.log1p(jnp.exp(-jnp.abs(x)))


def _lru_tile(u_ref, up_ref, un_ref, has_prev, has_next, cw_ref, cb_ref, w_ref, ba_ref, bx_ref, lam_ref,
              carry_ref, o_ref, *, ts, reverse):
    width = u_ref.shape[-1]
    blk = width // LRU_BLOCKS
    u = u_ref[0].astype(F32)
    up = up_ref[0].astype(F32)
    un = un_ref[0].astype(F32)
    row = lax.broadcasted_iota(jnp.int32, u.shape, 0)
    u_m1 = _halo_fix(pltpu.roll(u, 1, 0), row, 0, up[HALO - 1:HALO], has_prev)
    u_p1 = _halo_fix(pltpu.roll(u, ts - 1, 0), row, ts - 1, un[0:1], has_next)
    u_p2 = _halo_fix(pltpu.roll(u, ts - 2, 0), row, ts - 2, un[0:1], has_next)
    u_p2 = _halo_fix(u_p2, row, ts - 1, un[1:2], has_next)
    cw = cw_ref[...]
    uc = u_m1 * cw[0:1] + u * cw[1:2] + u_p1 * cw[2:3] + u_p2 * cw[3:4] + cb_ref[...]
    ucb = uc.astype(BF16)
    za, zx = [], []
    for h in range(LRU_BLOCKS):
        z = _dot(ucb[:, h * blk:(h + 1) * blk], w_ref[h])
        za.append(z[:, :blk])
        zx.append(z[:, blk:])
    r = jax.nn.sigmoid(jnp.concatenate(za, axis=1) + ba_ref[...])
    gate_i = jax.nn.sigmoid(jnp.concatenate(zx, axis=1) + bx_ref[...])
    log_a = (LRU_C * _log_sigmoid(lam_ref[...])) * r
    a = jnp.exp(log_a)
    th = jnp.tanh(log_a)
    bcoef = jnp.sqrt(-2.0 * th / (1.0 - th)) * gate_i * uc

    groups = ts // SUBLANES
    a3 = a.reshape(groups, SUBLANES, width)
    b3 = bcoef.reshape(groups, SUBLANES, width)
    sub = lax.broadcasted_iota(jnp.int32, a3.shape, 1)
    for s in (1, 2, 4):
        shift = SUBLANES - s if reverse else s
        a_sh = pltpu.roll(a3, shift, 1)
        b_sh = pltpu.roll(b3, shift, 1)
        m = (sub < SUBLANES - s) if reverse else (sub >= s)
        b3 = jnp.where(m, a3 * b_sh + b3, b3)
        a3 = jnp.where(m, a3 * a_sh, a3)

    h = carry_ref[0:1, :]
    order = range(groups - 1, -1, -1) if reverse else range(groups)
    edge = 0 if reverse else SUBLANES - 1
    for g in order:
        hg = a3[g] * h + b3[g]
        if o_ref is not None:
            o_ref[0, g * SUBLANES:(g + 1) * SUBLANES, :] = hg.astype(o_ref.dtype)
        h = hg[edge:edge + 1, :]
    carry_ref[...] = jnp.broadcast_to(h, carry_ref.shape)


def _lru_kernel(uc_ref, ucp_ref, ucn_ref, ul_ref, ulp_ref, uln_ref, cw_ref, cb_ref, w_ref, ba_ref, bx_ref,
                lam_ref, o_ref, carry_ref, *, ts, n_ctx_tiles, n_lat_tiles, reverse):
    j = pl.program_id(1)
    shared = (cw_ref, cb_ref, w_ref, ba_ref, bx_ref, lam_ref, carry_ref)

    @pl.when(j == 0)
    def _():
        carry_ref[...] = jnp.zeros_like(carry_ref)

    def tile_pos(step, n):
        return (n - 1 - step) if reverse else step

    @pl.when(j < n_ctx_tiles)
    def _():
        pos = tile_pos(j, n_ctx_tiles)
        _lru_tile(uc_ref, ucp_ref, ucn_ref, pos > 0, pos < n_ctx_tiles - 1, *shared, None, ts=ts, reverse=reverse)

    @pl.when(j >= n_ctx_tiles)
    def _():
        pos = tile_pos(j - n_ctx_tiles, n_lat_tiles)
        _lru_tile(ul_ref, ulp_ref, uln_ref, pos > 0, pos < n_lat_tiles - 1, *shared, o_ref, ts=ts, reverse=reverse)


def _lru_scan(proj, u_ctx, conv_w, conv_b, w_cat, ba, bx, lam, reverse):
    b, t, _ = proj.shape
    l, width = u_ctx.shape[1], u_ctx.shape[2]
    ts = min(256, l, t)
    n_c, n_l = l // ts, t // ts
    hb = ts // HALO

    def pos_of(step, n):
        step = jnp.clip(step, 0, n - 1)
        return (n - 1 - step) if reverse else step

    def tile_map(off, n, col):
        return lambda bi, j: (bi, pos_of(j - off, n), col)

    def prev_map(off, n, col):
        return lambda bi, j: (bi, jnp.maximum(pos_of(j - off, n) * hb - 1, 0), col)

    def next_map(off, n, col):
        return lambda bi, j: (bi, jnp.minimum((pos_of(j - off, n) + 1) * hb, n * hb - 1), col)

    const2 = lambda bi, j: (0, 0)
    in_specs = [
        pl.BlockSpec((1, ts, width), tile_map(0, n_c, 0)),
        pl.BlockSpec((1, HALO, width), prev_map(0, n_c, 0)),
        pl.BlockSpec((1, HALO, width), next_map(0, n_c, 0)),
        pl.BlockSpec((1, ts, width), tile_map(n_c, n_l, 1)),
        pl.BlockSpec((1, HALO, width), prev_map(n_c, n_l, 1)),
        pl.BlockSpec((1, HALO, width), next_map(n_c, n_l, 1)),
        pl.BlockSpec((LRU_CONV, width), const2),
        pl.BlockSpec((1, width), const2),
        pl.BlockSpec(w_cat.shape, lambda bi, j: (0, 0, 0)),
        pl.BlockSpec((1, width), const2),
        pl.BlockSpec((1, width), const2),
        pl.BlockSpec((1, width), const2),
    ]
    kern = functools.partial(_lru_kernel, ts=ts, n_ctx_tiles=n_c, n_lat_tiles=n_l, reverse=reverse)
    return pl.pallas_call(
        kern,
        out_shape=jax.ShapeDtypeStruct((b, t, width), BF16),
        grid=(b, n_c + n_l),
        in_specs=in_specs,
        out_specs=pl.BlockSpec((1, ts, width), tile_map(n_c, n_l, 0)),
        scratch_shapes=[pltpu.VMEM((SUBLANES, width), F32)],
        compiler_params=_params(("parallel", "arbitrary")),
        name="lru_scan_bwd" if reverse else "lru_scan_fwd",
    )(u_ctx, u_ctx, u_ctx, proj, proj, proj, conv_w, conv_b.reshape(1, width), w_cat,
      ba.reshape(1, width), bx.reshape(1, width), lam.reshape(1, width))


def _odd_out_kernel(hf_ref, hb_ref, gate_ref, w_ref, *tail_refs):
    hsum = hf_ref[0].astype(F32) + hb_ref[0].astype(F32)
    z = hsum * jax.nn.gelu(gate_ref[0].astype(F32), approximate=True)
    y = _dot(z.astype(BF16), w_ref[...])
    _layer_tail(y, *tail_refs)


def _odd_out(h_f, h_b, proj, w_out, x, g1, n2, sh2, sc2, w_r, b_r):
    b, t, d = x.shape
    width = h_f.shape[-1]
    tm = min(t, 512)
    row = lambda bi, i: (bi, i, 0)
    tail_in, out_specs, out_shape = _tail_specs(b, t, d, tm)
    in_specs = [
        pl.BlockSpec((1, tm, width), row),
        pl.BlockSpec((1, tm, width), row),
        pl.BlockSpec((1, tm, width), row),
        pl.BlockSpec((width, d), lambda bi, i: (0, 0)),
    ] + tail_in
    return pl.pallas_call(
        _odd_out_kernel,
        out_shape=out_shape,
        grid=(b, t // tm),
        in_specs=in_specs,
        out_specs=out_specs,
        compiler_params=_params(("parallel", "parallel")),
        name="odd_out",
    )(h_f, h_b, proj, w_out, x, g1, n2, sh2, sc2, w_r, b_r)


def _expert_kernel(be_ref, nb_ref, x_ref, wg_ref, bg_ref, wu_ref, bu_ref, wd_ref, bd_ref, o_ref,
                   wg_bf, wu_bf, wd_bf, *, chunk):
    i = pl.program_id(0)

    @pl.when((i == 0) | (be_ref[i] != be_ref[jnp.maximum(i - 1, 0)]))
    def _():
        wg_bf[...] = wg_ref[0, 0].astype(BF16)
        wu_bf[...] = wu_ref[0, 0].astype(BF16)
        wd_bf[...] = wd_ref[0, 0].astype(BF16)

    @pl.when(i < nb_ref[0])
    def _():
        x = x_ref[...]
        d_exp = wg_bf.shape[1]
        acc = jnp.zeros((x.shape[0], wd_bf.shape[1]), F32)
        for c in range(d_exp // chunk):
            cs = slice(c * chunk, (c + 1) * chunk)
            g = jnp.minimum(_dot(x, wg_bf[:, cs]) + bg_ref[0, 0, :, cs], SWIGLU_LIMIT)
            u = jnp.clip(_dot(x, wu_bf[:, cs]) + bu_ref[0, 0, :, cs], -SWIGLU_LIMIT, SWIGLU_LIMIT)
            h = g * jax.nn.sigmoid(SWIGLU_ALPHA * g) * (u + 1.0)
            acc = acc + _dot(h.astype(BF16), wd_bf[cs, :])
        o_ref[...] = (acc + bd_ref[0, 0]).astype(o_ref.dtype)

    @pl.when(i >= nb_ref[0])
    def _():
        o_ref[...] = jnp.zeros_like(o_ref)


def _experts(xb, block_e, n_used, layer, wg, bg, wu, bu, wd, bd):
    n_slots, d = xb.shape
    depth, n_e, _, d_exp = wg.shape
    tm = EXPERT_TILE
    n_blocks = n_slots // tm
    xmap = lambda i, be, nb: (jnp.minimum(i, nb[0] - 1), 0)
    wmap = lambda i, be, nb: (layer, be[i], 0, 0)
    grid_spec = pltpu.PrefetchScalarGridSpec(
        num_scalar_prefetch=2,
        grid=(n_blocks,),
        in_specs=[
            pl.BlockSpec((tm, d), xmap),
            pl.BlockSpec((1, 1, d, d_exp), wmap),
            pl.BlockSpec((1, 1, 1, d_exp), wmap),
            pl.BlockSpec((1, 1, d, d_exp), wmap),
            pl.BlockSpec((1, 1, 1, d_exp), wmap),
            pl.BlockSpec((1, 1, d_exp, d), wmap),
            pl.BlockSpec((1, 1, 1, d), wmap),
        ],
        out_specs=pl.BlockSpec((tm, d), lambda i, be, nb: (i, 0)),
        scratch_shapes=[pltpu.VMEM((d, d_exp), BF16), pltpu.VMEM((d, d_exp), BF16), pltpu.VMEM((d_exp, d), BF16)],
    )
    return pl.pallas_call(
        functools.partial(_expert_kernel, chunk=256),
        out_shape=jax.ShapeDtypeStruct((n_slots, d), BF16),
        grid_spec=grid_spec,
        compiler_params=_params(("arbitrary",)),
        name="experts",
    )(block_e, n_used, xb, wg, bg.reshape(depth, n_e, 1, d_exp), wu, bu.reshape(depth, n_e, 1, d_exp),
      wd, bd.reshape(depth, n_e, 1, d))


def _combine_kernel(x_ref, g2_ref, gates_ref, y_ref, o_ref):
    gates = gates_ref[0]
    acc = jnp.zeros(x_ref.shape[1:], F32)
    for k in range(TOP_K):
        acc = acc + gates[:, k:k + 1] * y_ref[k].astype(F32)
    o_ref[0] = x_ref[0] + g2_ref[0] * acc


def _combine(x, g2, gates, y_sel, tok_offset):
    b, t, d = x.shape
    tm = int(np.gcd(min(t, 512), tok_offset)) if tok_offset else min(t, 512)
    n_t = t // tm
    off = tok_offset // tm
    return pl.pallas_call(
        _combine_kernel,
        out_shape=jax.ShapeDtypeStruct((b, t, d), F32),
        grid=(b, n_t),
        in_specs=[
            pl.BlockSpec((1, tm, d), lambda bi, i: (bi, i, 0)),
            pl.BlockSpec((1, 1, d), lambda bi, i: (bi, 0, 0)),
            pl.BlockSpec((1, tm, ROUTER_PAD), lambda bi, i: (bi, i, 0)),
            pl.BlockSpec((TOP_K, tm, d), lambda bi, i: (0, off + bi * n_t + i, 0)),
        ],
        out_specs=pl.BlockSpec((1, tm, d), lambda bi, i: (bi, i, 0)),
        compiler_params=_params(("parallel", "parallel")),
        name="moe_combine",
    )(x, g2, gates, y_sel)


def _moe(h2, ids, layer, w_exp):
    n, d = h2.shape
    nk = n * TOP_K
    tm = EXPERT_TILE
    e_flat = ids.reshape(-1)
    onehot = (e_flat[:, None] == jnp.arange(N_EXPERTS, dtype=jnp.int32)[None]).astype(jnp.int32)
    csum = jnp.cumsum(onehot, axis=0)
    counts = csum[-1]
    padded = (counts + tm - 1) // tm * tm
    pends = jnp.cumsum(padded)
    dest = jnp.sum(onehot * (csum - 1 + (pends - padded)[None]), axis=1)
    n_blocks = -(-nk // tm) + N_EXPERTS
    n_slots = n_blocks * tm
    tok = jnp.arange(nk, dtype=jnp.int32) // TOP_K
    slot_tok = jnp.zeros((n_slots,), jnp.int32).at[dest].set(tok)
    block_start = jnp.arange(n_blocks, dtype=jnp.int32) * tm
    block_e = jnp.minimum(jnp.sum((pends[None] <= block_start[:, None]).astype(jnp.int32), axis=1), N_EXPERTS - 1)
    n_used = (pends[-1] // tm).astype(jnp.int32).reshape(1)
    xb = h2[slot_tok]
    yb = _experts(xb, block_e, n_used, layer, *w_exp)
    return yb[dest.reshape(n, TOP_K).T]


def _mod_parts(mod_l, b):
    d = mod_l.shape[-1] // 6
    lat = [mod_l[:b, k * d:(k + 1) * d].reshape(b, 1, d) for k in range(6)]
    ctx = [jnp.broadcast_to(mod_l[b, k * d:(k + 1) * d].reshape(1, 1, d), (b, 1, d)) for k in range(6)]
    return lat, ctx


def kernel(x, c, ctx, c_ctx, ada_w, ada_b, norm1_g, norm2_g, ev_w_in, ev_w_out, ev_q_gain, ev_k_gain, ev_rpb, ev_conv_w, ev_conv_b, od_w_in, od_w_out, od_conv_w, od_conv_b, od_fwd_wa, od_fwd_ba, od_fwd_wx, od_fwd_bx, od_fwd_lam, od_bwd_wa, od_bwd_ba, od_bwd_wx, od_bwd_bx, od_bwd_lam, router_w, router_b, exp_w_gate, exp_b_gate, exp_w_up, exp_b_up, exp_w_down, exp_b_down):
    b, t, d = x.shape
    l = ctx.shape[1]
    assert ada_w.shape[0] == DEPTH == 2 and t % GRID_W == 0 and t // GRID_W >= WIN_H

    n_rows_c = -(-(b + 1) // SUBLANES) * SUBLANES
    cvec = jnp.zeros((n_rows_c, d), F32).at[:b].set(c).at[b].set(c_ctx)
    mod = _ada_mod(cvec, ada_w, ada_b)

    def router(layer):
        w_r = jnp.zeros((d, ROUTER_PAD), F32).at[:, :N_EXPERTS].set(router_w[layer]).astype(BF16)
        b_r = jnp.zeros((1, ROUTER_PAD), F32).at[0, :N_EXPERTS].set(router_b[layer])
        return w_r, b_r

    w_exp = (exp_w_gate, exp_b_gate, exp_w_up, exp_b_up, exp_w_down, exp_b_down)

    (sh1, sc1, g1, sh2, sc2, g2), (csh1, csc1, cg1, csh2, csc2, cg2) = _mod_parts(mod[0], b)
    n1 = norm1_g[0].reshape(1, d)
    n2 = norm2_g[0].reshape(1, d)
    w_in = ev_w_in[0].astype(BF16)
    w_out = ev_w_out[0].astype(BF16)
    q_scale = NA_HEAD_DIM ** -0.5
    head_gain = jnp.stack([jnp.tile(ev_q_gain[0] * q_scale, NA_HEADS), jnp.tile(ev_k_gain[0], NA_HEADS)])
    head_gain = head_gain.reshape(2, 1, NA_WIDTH).astype(F32)
    proj = _inproj(x, n1, sh1, sc1, w_in, head_gain)
    proj_c = _inproj(ctx, n1, csh1, csc1, w_in, head_gain)
    kh, r0, row_type, patterns = _na_tables(t // GRID_W)
    bias_tab = _na_bias_table(ev_rpb[0], patterns)
    o_a = _neighbourhood_attention(proj, proj_c, bias_tab, r0, row_type, kh)
    oc_a = _context_attention(proj_c)
    w_r, b_r = router(0)
    x1, h2, ids, gates = _even_out(o_a, proj, ev_conv_w[0], ev_conv_b[0], w_out, x, g1, n2, sh2, sc2, w_r, b_r)
    c1, h2c, ids_c, gates_c = _even_out(oc_a, proj_c, ev_conv_w[0], ev_conv_b[0], w_out, ctx, cg1, n2, csh2, csc2,
                                        w_r, b_r)
    tokens = jnp.concatenate([h2c.reshape(b * l, d), h2.reshape(b * t, d)], axis=0)
    ids_all = jnp.concatenate([ids_c.reshape(b * l, ROUTER_PAD), ids.reshape(b * t, ROUTER_PAD)], axis=0)[:, :TOP_K]
    y_sel = _moe(tokens, ids_all, 0, w_exp)
    hctx = _combine(c1, cg2, gates_c, y_sel, 0)
    x = _combine(x1, g2, gates, y_sel, b * l)

    (sh1, sc1, g1, sh2, sc2, g2), (csh1, csc1, _, _, _, _) = _mod_parts(mod[1], b)
    n1 = norm1_g[1].reshape(1, d)
    n2 = norm2_g[1].reshape(1, d)
    w_in = od_w_in[0].astype(BF16)
    width = w_in.shape[1] // 2
    proj = _inproj(x, n1, sh1, sc1, w_in)
    u_ctx = _inproj(hctx, n1, csh1, csc1, w_in[:, width:])
    h_dir = []
    for reverse, (wa, ba, wx, bx, lam) in ((False, (od_fwd_wa, od_fwd_ba, od_fwd_wx, od_fwd_bx, od_fwd_lam)),
                                           (True, (od_bwd_wa, od_bwd_ba, od_bwd_wx, od_bwd_bx, od_bwd_lam))):
        w_cat = jnp.concatenate([wa[0], wx[0]], axis=-1).astype(BF16)
        h_dir.append(_lru_scan(proj, u_ctx, od_conv_w[0], od_conv_b[0], w_cat, ba[0], bx[0], lam[0], reverse))
    w_r, b_r = router(1)
    x1, h2, ids, gates = _odd_out(h_dir[0], h_dir[1], proj, od_w_out[0].astype(BF16), x, g1, n2, sh2, sc2, w_r, b_r)
    y_sel = _moe(h2.reshape(b * t, d), ids.reshape(b * t, ROUTER_PAD)[:, :TOP_K], 1, w_exp)
    return _combine(x1, g2, gates, y_sel, 0)
```

```python
import functools

import numpy as np
import jax
import jax.numpy as jnp
from jax import lax
from jax.experimental import pallas as pl
from jax.experimental.pallas import tpu as pltpu
from jax.experimental.pallas import tpu_sc as plsc

DEPTH = 2
GRID_W = 64
EPS = 1e-6
NEG_INF = -1e30
NA_HEADS = 8
NA_HEAD_DIM = 64
NA_WIDTH = NA_HEADS * NA_HEAD_DIM
HEAD_PAIRS = NA_HEADS // 2
WIN_H = 8
WIN_W = 16
SC_CONV = 3
LRU_BLOCKS = 4
LRU_CONV = 4
LRU_C = 8.0
N_EXPERTS = 32
TOP_K = 4
SWIGLU_LIMIT = 7.0
SWIGLU_ALPHA = 1.702

LANES = 128
SUBLANES = 8
HALO = 16
ROUTER_PAD = LANES
EXPERT_TILE = 512
SC_GATHER_ROWS = 64
VMEM_LIMIT = 56 * 1024 * 1024

F32 = jnp.float32
BF16 = jnp.bfloat16


def _params(sem, vmem=VMEM_LIMIT):
    return pltpu.CompilerParams(dimension_semantics=sem, vmem_limit_bytes=vmem)


def _dot(a, b):
    return jnp.dot(a, b, preferred_element_type=F32)


def _dot_nt(a, b):
    return lax.dot_general(a, b, (((1,), (1,)), ((), ())), preferred_element_type=F32)


def _pack_rows(v):
    w = v.shape[-1] // 2
    lo = lax.bitcast_convert_type(v[:, :w].astype(BF16).astype(F32), jnp.int32)
    hi = lax.bitcast_convert_type(v[:, w:].astype(BF16).astype(F32), jnp.int32)
    return lax.shift_right_logical(lo, 16) | (hi & jnp.int32(-65536))


def _unpack_rows(p):
    lo = lax.bitcast_convert_type(lax.shift_left(p, 16), F32)
    hi = lax.bitcast_convert_type(p & jnp.int32(-65536), F32)
    return lo, hi


def _rms_mod(x, g, shift, scale):
    ms = jnp.mean(x * x, axis=-1, keepdims=True)
    y = x * lax.rsqrt(ms + EPS) * g
    return y * (1.0 + scale) + shift


def _ada_kernel(c_ref, w_ref, b_ref, o_ref):
    c = c_ref[...]
    s = (c * jax.nn.sigmoid(c)).astype(BF16)
    o_ref[0] = _dot(s, w_ref[0].astype(BF16)) + b_ref[0]


def _ada_mod(cvec, ada_w, ada_b):
    depth, d, n = ada_w.shape
    r = cvec.shape[0]
    tn = 1536
    return pl.pallas_call(
        _ada_kernel,
        out_shape=jax.ShapeDtypeStruct((depth, r, n), F32),
        grid=(depth, n // tn),
        in_specs=[
            pl.BlockSpec((r, d), lambda l, j: (0, 0)),
            pl.BlockSpec((1, d, tn), lambda l, j: (l, 0, j)),
            pl.BlockSpec((1, 1, tn), lambda l, j: (l, 0, j)),
        ],
        out_specs=pl.BlockSpec((1, r, tn), lambda l, j: (l, 0, j)),
        compiler_params=_params(("parallel", "parallel")),
        name="ada_mod",
    )(cvec, ada_w, ada_b.reshape(depth, 1, n))


def _inproj_kernel(x_ref, g_ref, sh_ref, sc_ref, w_ref, hg_ref, ones_ref, o_ref, *, n_tiles, tn, n_headnorm):
    h = _rms_mod(x_ref[0], g_ref[...], sh_ref[0], sc_ref[0]).astype(BF16)
    for j in range(n_tiles):
        y = _dot(h, w_ref[:, j * tn:(j + 1) * tn])
        if j < n_headnorm:
            ms = _dot((y * y).astype(BF16), ones_ref[...]) * (1.0 / NA_HEAD_DIM)
            y = y * lax.rsqrt(ms + EPS) * hg_ref[j]
        o_ref[0, :, j * tn:(j + 1) * tn] = y.astype(o_ref.dtype)


def _inproj(x, g, shift, scale, w, head_gain=None):
    b, t, d = x.shape
    n = w.shape[1]
    tn = NA_WIDTH
    tm = min(t, 512)
    n_headnorm = 0 if head_gain is None else head_gain.shape[0]
    if head_gain is None:
        head_gain = jnp.ones((1, 1, tn), F32)
    hid = np.arange(tn) // NA_HEAD_DIM
    ones_bd = jnp.asarray((hid[:, None] == hid[None, :]), BF16)
    kern = functools.partial(_inproj_kernel, n_tiles=n // tn, tn=tn, n_headnorm=n_headnorm)
    return pl.pallas_call(
        kern,
        out_shape=jax.ShapeDtypeStruct((b, t, n), BF16),
        grid=(b, t // tm),
        in_specs=[
            pl.BlockSpec((1, tm, d), lambda bi, i: (bi, i, 0)),
            pl.BlockSpec((1, d), lambda bi, i: (0, 0)),
            pl.BlockSpec((1, 1, d), lambda bi, i: (bi, 0, 0)),
            pl.BlockSpec((1, 1, d), lambda bi, i: (bi, 0, 0)),
            pl.BlockSpec((d, n), lambda bi, i: (0, 0)),
            pl.BlockSpec(head_gain.shape, lambda bi, i: (0, 0, 0)),
            pl.BlockSpec((tn, tn), lambda bi, i: (0, 0)),
        ],
        out_specs=pl.BlockSpec((1, tm, n), lambda bi, i: (bi, i, 0)),
        compiler_params=_params(("parallel", "parallel")),
        name="inproj",
    )(x, g, shift, scale, w, head_gain, ones_bd)


def _na_tables(rows):
    kh = min(WIN_H, rows)
    r = np.arange(rows)
    r0 = np.clip(r - kh // 2, 0, rows - kh)
    dr = r0[:, None] + np.arange(kh)[None] - r[:, None] + WIN_H - 1
    patterns, row_type = np.unique(dr, axis=0, return_inverse=True)
    return kh, r0.astype(np.int32), row_type.reshape(-1).astype(np.int32), patterns


def _na_bias_table(rpb, patterns):
    qc = np.arange(GRID_W)
    kc = np.arange(GRID_W)
    c0 = np.clip(qc - WIN_W // 2, 0, GRID_W - WIN_W)[:, None]
    valid = (kc[None] >= c0) & (kc[None] < c0 + WIN_W)
    dc = np.clip(kc[None] - qc[:, None] + WIN_W - 1, 0, 2 * WIN_W - 2)
    n_pat, kh = patterns.shape
    onehot_dc = jnp.asarray(dc[None] == np.arange(2 * WIN_W - 1)[:, None, None], F32)
    tab = jnp.einsum('hpic,cqk->hpiqk', rpb.astype(F32)[:, patterns], onehot_dc,
                     precision=lax.Precision.HIGHEST)
    tab = jnp.where(valid[None, None, None], tab, NEG_INF)
    tab = tab.reshape(HEAD_PAIRS, 2, n_pat, kh, GRID_W, GRID_W)
    tab = tab.transpose(2, 0, 1, 4, 3, 5)
    return tab.reshape(n_pat, HEAD_PAIRS, 2 * GRID_W, kh * GRID_W)


def _pair_attention(q, keys, values, biases):
    m = q.shape[0]
    lane = lax.broadcasted_iota(jnp.int32, q.shape, 1)
    zero = jnp.zeros_like(q)
    qq = jnp.concatenate([jnp.where(lane < NA_HEAD_DIM, q, zero), jnp.where(lane >= NA_HEAD_DIM, q, zero)], axis=0)
    scores = []
    for k, bias in zip(keys, biases):
        s = _dot_nt(qq, k)
        scores.append(s if bias is None else s + bias)
    mx = functools.reduce(jnp.maximum, [jnp.max(s, axis=-1, keepdims=True) for s in scores])
    exps = [jnp.exp(s - mx) for s in scores]
    denom = functools.reduce(jnp.add, [jnp.sum(e, axis=-1, keepdims=True) for e in exps])
    o = functools.reduce(jnp.add, [_dot(e.astype(BF16), v) for e, v in zip(exps, values)])
    o = o * (1.0 / denom)
    lane_o = lax.broadcasted_iota(jnp.int32, (m, LANES), 1)
    return jnp.where(lane_o < NA_HEAD_DIM, o[:m], o[m:])


def _na_kernel(r0_ref, type_ref, q_ref, k_ref, v_ref, kc_ref, vc_ref, bias_ref, o_ref, *, kh):
    r = pl.program_id(1)
    start = pl.multiple_of(r0_ref[r] * GRID_W, GRID_W)
    rtype = type_ref[r]
    for p in range(HEAD_PAIRS):
        cols = slice(p * LANES, (p + 1) * LANES)
        o = _pair_attention(
            q_ref[0, :, cols],
            [k_ref[0, pl.ds(start, kh * GRID_W), cols], kc_ref[0, :, cols]],
            [v_ref[0, pl.ds(start, kh * GRID_W), cols], vc_ref[0, :, cols]],
            [bias_ref[rtype, p], None],
        )
        o_ref[0, :, cols] = o.astype(o_ref.dtype)


def _neighbourhood_attention(proj, proj_c, bias_tab, r0, row_type, kh):
    b, t, _ = proj.shape
    l = proj_c.shape[1]
    rows = t // GRID_W
    w = NA_WIDTH
    grid_spec = pltpu.PrefetchScalarGridSpec(
        num_scalar_prefetch=2,
        grid=(b, rows),
        in_specs=[
            pl.BlockSpec((1, GRID_W, w), lambda bi, r, *_: (bi, r, 0)),
            pl.BlockSpec((1, t, w), lambda bi, r, *_: (bi, 0, 1)),
            pl.BlockSpec((1, t, w), lambda bi, r, *_: (bi, 0, 2)),
            pl.BlockSpec((1, l, w), lambda bi, r, *_: (bi, 0, 1)),
            pl.BlockSpec((1, l, w), lambda bi, r, *_: (bi, 0, 2)),
            pl.BlockSpec(bias_tab.shape, lambda bi, r, *_: (0, 0, 0, 0)),
        ],
        out_specs=pl.BlockSpec((1, GRID_W, w), lambda bi, r, *_: (bi, r, 0)),
    )
    return pl.pallas_call(
        functools.partial(_na_kernel, kh=kh),
        out_shape=jax.ShapeDtypeStruct((b, t, w), BF16),
        grid_spec=grid_spec,
        compiler_params=_params(("parallel", "arbitrary")),
        name="na_attention",
    )(jnp.asarray(r0), jnp.asarray(row_type), proj, proj, proj, proj_c, proj_c, bias_tab)


def _ctx_attn_kernel(q_ref, k_ref, v_ref, o_ref):
    for p in range(HEAD_PAIRS):
        cols = slice(p * LANES, (p + 1) * LANES)
        o = _pair_attention(q_ref[0, :, cols], [k_ref[0, :, cols]], [v_ref[0, :, cols]], [None])
        o_ref[0, :, cols] = o.astype(o_ref.dtype)


def _context_attention(proj_c):
    b, l, _ = proj_c.shape
    w = NA_WIDTH
    return pl.pallas_call(
        _ctx_attn_kernel,
        out_shape=jax.ShapeDtypeStruct((b, l, w), BF16),
        grid=(b,),
        in_specs=[pl.BlockSpec((1, l, w), lambda bi, j=j: (bi, 0, j)) for j in range(3)],
        out_specs=pl.BlockSpec((1, l, w), lambda bi: (bi, 0, 0)),
        compiler_params=_params(("parallel",)),
        name="ctx_attention",
    )(proj_c, proj_c, proj_c)


def _top4(logits):
    lane = lax.broadcasted_iota(jnp.int32, logits.shape, 1)
    cur = jnp.where(lane < N_EXPERTS, logits, -jnp.inf)
    vals, idxs = [], []
    for _ in range(TOP_K):
        m = jnp.max(cur, axis=-1, keepdims=True)
        first = jnp.min(jnp.where(cur == m, lane, ROUTER_PAD).astype(F32), axis=-1, keepdims=True)
        idx = first.astype(jnp.int32)
        vals.append(m)
        idxs.append(idx)
        cur = jnp.where(lane == idx, -jnp.inf, cur)
    exps = [jnp.exp(v - vals[0]) for v in vals]
    inv = 1.0 / functools.reduce(jnp.add, exps)
    ids = jnp.zeros(logits.shape, jnp.int32)
    gates = jnp.zeros(logits.shape, F32)
    for k in range(TOP_K):
        ids = jnp.where(lane == k, idxs[k], ids)
        gates = jnp.where(lane == k, exps[k] * inv, gates)
    return ids, gates


def _layer_tail(y, x_ref, g1_ref, n2_ref, sh2_ref, sc2_ref, wr_ref, br_ref, xo_ref, h2_ref, ids_ref, gates_ref):
    x_new = x_ref[0] + g1_ref[0] * y
    xo_ref[0] = x_new
    h2 = _rms_mod(x_new, n2_ref[...], sh2_ref[0], sc2_ref[0])
    h2_ref[0] = _pack_rows(h2)
    logits = _dot(h2.astype(BF16), wr_ref[...]) + br_ref[...]
    ids, gates = _top4(logits)
    ids_ref[0] = ids
    gates_ref[0] = gates


def _tail_specs(b, t, d, tm):
    row = lambda bi, i: (bi, i, 0)
    per_b = lambda bi, i: (bi, 0, 0)
    const = lambda bi, i: (0, 0)
    in_specs = [
        pl.BlockSpec((1, tm, d), row),
        pl.BlockSpec((1, 1, d), per_b),
        pl.BlockSpec((1, d), const),
        pl.BlockSpec((1, 1, d), per_b),
        pl.BlockSpec((1, 1, d), per_b),
        pl.BlockSpec((d, ROUTER_PAD), const),
        pl.BlockSpec((1, ROUTER_PAD), const),
    ]
    out_specs = [
        pl.BlockSpec((1, tm, d), row),
        pl.BlockSpec((1, tm, d // 2), row),
        pl.BlockSpec((1, tm, ROUTER_PAD), row),
        pl.BlockSpec((1, tm, ROUTER_PAD), row),
    ]
    out_shape = [
        jax.ShapeDtypeStruct((b, t, d), F32),
        jax.ShapeDtypeStruct((b, t, d // 2), jnp.int32),
        jax.ShapeDtypeStruct((b, t, ROUTER_PAD), jnp.int32),
        jax.ShapeDtypeStruct((b, t, ROUTER_PAD), F32),
    ]
    return in_specs, out_specs, out_shape


def _halo_fix(rolled, row, at_row, halo_row, present):
    fill = jnp.where(present, halo_row, jnp.zeros_like(halo_row))
    return jnp.where(row == at_row, fill, rolled)


def _even_out_kernel(oa_ref, bg_ref, cg_ref, xin_ref, cgp_ref, xinp_ref, cgn_ref, xinn_ref, cw_ref, cb_ref,
                     wa_ref, wb_ref, *tail_refs, tm):
    i = pl.program_id(1)
    has_prev = i > 0
    has_next = i < pl.num_programs(1) - 1
    u = cg_ref[0].astype(F32) * xin_ref[0].astype(F32)
    u_prev = (cgp_ref[0].astype(F32) * xinp_ref[0].astype(F32))[HALO - 1:HALO]
    u_next = (cgn_ref[0].astype(F32) * xinn_ref[0].astype(F32))[0:1]
    row = lax.broadcasted_iota(jnp.int32, u.shape, 0)
    u_m1 = _halo_fix(pltpu.roll(u, 1, 0), row, 0, u_prev, has_prev)
    u_p1 = _halo_fix(pltpu.roll(u, tm - 1, 0), row, tm - 1, u_next, has_next)
    cw = cw_ref[...]
    conv = u_m1 * cw[0:1] + u * cw[1:2] + u_p1 * cw[2:3] + cb_ref[...]
    o_b = bg_ref[0].astype(F32) * conv
    y = _dot(oa_ref[0], wa_ref[...]) + _dot(o_b.astype(BF16), wb_ref[...])
    _layer_tail(y, *tail_refs)


def _even_out(o_a, proj, conv_w, conv_b, w_out, x, g1, n2, sh2, sc2, w_r, b_r):
    b, t, d = x.shape
    w = NA_WIDTH
    tm = min(t, 512)
    hb = tm // HALO
    n_hblocks = t // HALO
    row = lambda bi, i: (bi, i, 0)
    const = lambda bi, i: (0, 0)
    prev = lambda col: (lambda bi, i: (bi, jnp.maximum(i * hb - 1, 0), col))
    nxt = lambda col: (lambda bi, i: (bi, jnp.minimum((i + 1) * hb, n_hblocks - 1), col))
    tail_in, out_specs, out_shape = _tail_specs(b, t, d, tm)
    in_specs = [
        pl.BlockSpec((1, tm, w), row),
        pl.BlockSpec((1, tm, w), lambda bi, i: (bi, i, 3)),
        pl.BlockSpec((1, tm, w), lambda bi, i: (bi, i, 4)),
        pl.BlockSpec((1, tm, w), lambda bi, i: (bi, i, 5)),
        pl.BlockSpec((1, HALO, w), prev(4)),
        pl.BlockSpec((1, HALO, w), prev(5)),
        pl.BlockSpec((1, HALO, w), nxt(4)),
        pl.BlockSpec((1, HALO, w), nxt(5)),
        pl.BlockSpec((SC_CONV, w), const),
        pl.BlockSpec((1, w), const),
        pl.BlockSpec((w, d), const),
        pl.BlockSpec((w, d), const),
    ] + tail_in
    return pl.pallas_call(
        functools.partial(_even_out_kernel, tm=tm),
        out_shape=out_shape,
        grid=(b, t // tm),
        in_specs=in_specs,
        out_specs=out_specs,
        compiler_params=_params(("parallel", "parallel")),
        name="even_out",
    )(o_a, proj, proj, proj, proj, proj, proj, proj, conv_w, conv_b.reshape(1, w),
      w_out[:w], w_out[w:], x, g1, n2, sh2, sc2, w_r, b_r)


def _log_sigmoid(x):
    return jnp.minimum(x, 0.0) - jnp.log1p(jnp.exp(-jnp.abs(x)))


def _lru_tile(u_ref, up_ref, un_ref, has_prev, has_next, cw_ref, cb_ref, w_ref, ba_ref, bx_ref, lam_ref,
              carry_ref, o_ref, *, ts, reverse):
    width = u_ref.shape[-1]
    blk = width // LRU_BLOCKS
    u = u_ref[0].astype(F32)
    up = up_ref[0].astype(F32)
    un = un_ref[0].astype(F32)
    row = lax.broadcasted_iota(jnp.int32, u.shape, 0)
    u_m1 = _halo_fix(pltpu.roll(u, 1, 0), row, 0, up[HALO - 1:HALO], has_prev)
    u_p1 = _halo_fix(pltpu.roll(u, ts - 1, 0), row, ts - 1, un[0:1], has_next)
    u_p2 = _halo_fix(pltpu.roll(u, ts - 2, 0), row, ts - 2, un[0:1], has_next)
    u_p2 = _halo_fix(u_p2, row, ts - 1, un[1:2], has_next)
    cw = cw_ref[...]
    uc = u_m1 * cw[0:1] + u * cw[1:2] + u_p1 * cw[2:3] + u_p2 * cw[3:4] + cb_ref[...]
    ucb = uc.astype(BF16)
    za, zx = [], []
    for h in range(LRU_BLOCKS):
        z = _dot(ucb[:, h * blk:(h + 1) * blk], w_ref[h])
        za.append(z[:, :blk])
        zx.append(z[:, blk:])
    r = jax.nn.sigmoid(jnp.concatenate(za, axis=1) + ba_ref[...])
    gate_i = jax.nn.sigmoid(jnp.concatenate(zx, axis=1) + bx_ref[...])
    log_a = (LRU_C * _log_sigmoid(lam_ref[...])) * r
    a = jnp.exp(log_a)
    th = jnp.tanh(log_a)
    bcoef = jnp.sqrt(-2.0 * th / (1.0 - th)) * gate_i * uc

    groups = ts // SUBLANES
    a3 = a.reshape(groups, SUBLANES, width)
    b3 = bcoef.reshape(groups, SUBLANES, width)
    sub = lax.broadcasted_iota(jnp.int32, a3.shape, 1)
    for s in (1, 2, 4):
        shift = SUBLANES - s if reverse else s
        a_sh = pltpu.roll(a3, shift, 1)
        b_sh = pltpu.roll(b3, shift, 1)
        m = (sub < SUBLANES - s) if reverse else (sub >= s)
        b3 = jnp.where(m, a3 * b_sh + b3, b3)
        a3 = jnp.where(m, a3 * a_sh, a3)

    h = carry_ref[0:1, :]
    order = range(groups - 1, -1, -1) if reverse else range(groups)
    edge = 0 if reverse else SUBLANES - 1
    for g in order:
        hg = a3[g] * h + b3[g]
        if o_ref is not None:
            o_ref[0, g * SUBLANES:(g + 1) * SUBLANES, :] = hg.astype(o_ref.dtype)
        h = hg[edge:edge + 1, :]
    carry_ref[...] = jnp.broadcast_to(h, carry_ref.shape)


def _lru_kernel(uc_ref, ucp_ref, ucn_ref, ul_ref, ulp_ref, uln_ref, cw_ref, cb_ref, w_ref, ba_ref, bx_ref,
                lam_ref, o_ref, carry_ref, *, ts, n_ctx_tiles, n_lat_tiles, reverse):
    j = pl.program_id(1)
    shared = (cw_ref, cb_ref, w_ref, ba_ref, bx_ref, lam_ref, carry_ref)

    @pl.when(j == 0)
    def _():
        carry_ref[...] = jnp.zeros_like(carry_ref)

    def tile_pos(step, n):
        return (n - 1 - step) if reverse else step

    @pl.when(j < n_ctx_tiles)
    def _():
        pos = tile_pos(j, n_ctx_tiles)
        _lru_tile(uc_ref, ucp_ref, ucn_ref, pos > 0, pos < n_ctx_tiles - 1, *shared, None, ts=ts, reverse=reverse)

    @pl.when(j >= n_ctx_tiles)
    def _():
        pos = tile_pos(j - n_ctx_tiles, n_lat_tiles)
        _lru_tile(ul_ref, ulp_ref, uln_ref, pos > 0, pos < n_lat_tiles - 1, *shared, o_ref, ts=ts, reverse=reverse)


def _lru_scan(proj, u_ctx, conv_w, conv_b, w_cat, ba, bx, lam, reverse):
    b, t, _ = proj.shape
    l, width = u_ctx.shape[1], u_ctx.shape[2]
    ts = min(256, l, t)
    n_c, n_l = l // ts, t // ts
    hb = ts // HALO

    def pos_of(step, n):
        step = jnp.clip(step, 0, n - 1)
        return (n - 1 - step) if reverse else step

    def tile_map(off, n, col):
        return lambda bi, j: (bi, pos_of(j - off, n), col)

    def prev_map(off, n, col):
        return lambda bi, j: (bi, jnp.maximum(pos_of(j - off, n) * hb - 1, 0), col)

    def next_map(off, n, col):
        return lambda bi, j: (bi, jnp.minimum((pos_of(j - off, n) + 1) * hb, n * hb - 1), col)

    const2 = lambda bi, j: (0, 0)
    in_specs = [
        pl.BlockSpec((1, ts, width), tile_map(0, n_c, 0)),
        pl.BlockSpec((1, HALO, width), prev_map(0, n_c, 0)),
        pl.BlockSpec((1, HALO, width), next_map(0, n_c, 0)),
        pl.BlockSpec((1, ts, width), tile_map(n_c, n_l, 1)),
        pl.BlockSpec((1, HALO, width), prev_map(n_c, n_l, 1)),
        pl.BlockSpec((1, HALO, width), next_map(n_c, n_l, 1)),
        pl.BlockSpec((LRU_CONV, width), const2),
        pl.BlockSpec((1, width), const2),
        pl.BlockSpec(w_cat.shape, lambda bi, j: (0, 0, 0)),
        pl.BlockSpec((1, width), const2),
        pl.BlockSpec((1, width), const2),
        pl.BlockSpec((1, width), const2),
    ]
    kern = functools.partial(_lru_kernel, ts=ts, n_ctx_tiles=n_c, n_lat_tiles=n_l, reverse=reverse)
    return pl.pallas_call(
        kern,
        out_shape=jax.ShapeDtypeStruct((b, t, width), BF16),
        grid=(b, n_c + n_l),
        in_specs=in_specs,
        out_specs=pl.BlockSpec((1, ts, width), tile_map(n_c, n_l, 0)),
        scratch_shapes=[pltpu.VMEM((SUBLANES, width), F32)],
        compiler_params=_params(("parallel", "arbitrary")),
        name="lru_scan_bwd" if reverse else "lru_scan_fwd",
    )(u_ctx, u_ctx, u_ctx, proj, proj, proj, conv_w, conv_b.reshape(1, width), w_cat,
      ba.reshape(1, width), bx.reshape(1, width), lam.reshape(1, width))


def _odd_out_kernel(hf_ref, hb_ref, gate_ref, w_ref, *tail_refs):
    hsum = hf_ref[0].astype(F32) + hb_ref[0].astype(F32)
    z = hsum * jax.nn.gelu(gate_ref[0].astype(F32), approximate=True)
    y = _dot(z.astype(BF16), w_ref[...])
    _layer_tail(y, *tail_refs)


def _odd_out(h_f, h_b, proj, w_out, x, g1, n2, sh2, sc2, w_r, b_r):
    b, t, d = x.shape
    width = h_f.shape[-1]
    tm = min(t, 512)
    row = lambda bi, i: (bi, i, 0)
    tail_in, out_specs, out_shape = _tail_specs(b, t, d, tm)
    in_specs = [
        pl.BlockSpec((1, tm, width), row),
        pl.BlockSpec((1, tm, width), row),
        pl.BlockSpec((1, tm, width), row),
        pl.BlockSpec((width, d), lambda bi, i: (0, 0)),
    ] + tail_in
    return pl.pallas_call(
        _odd_out_kernel,
        out_shape=out_shape,
        grid=(b, t // tm),
        in_specs=in_specs,
        out_specs=out_specs,
        compiler_params=_params(("parallel", "parallel")),
        name="odd_out",
    )(h_f, h_b, proj, w_out, x, g1, n2, sh2, sc2, w_r, b_r)


def _expert_kernel(be_ref, nb_ref, x_ref, wg_ref, bg_ref, wu_ref, bu_ref, wd_ref, bd_ref, o_ref,
                   wg_bf, wu_bf, wd_bf, *, chunk):
    i = pl.program_id(0)

    @pl.when((i == 0) | (be_ref[i] != be_ref[jnp.maximum(i - 1, 0)]))
    def _():
        wg_bf[...] = wg_ref[0, 0].astype(BF16)
        wu_bf[...] = wu_ref[0, 0].astype(BF16)
        wd_bf[...] = wd_ref[0, 0].astype(BF16)

    @pl.when(i < nb_ref[0])
    def _():
        x = jnp.concatenate(_unpack_rows(x_ref[...]), axis=1).astype(BF16)
        d_exp = wg_bf.shape[1]
        acc = jnp.zeros((x.shape[0], wd_bf.shape[1]), F32)
        for c in range(d_exp // chunk):
            cs = slice(c * chunk, (c + 1) * chunk)
            g = jnp.minimum(_dot(x, wg_bf[:, cs]) + bg_ref[0, 0, :, cs], SWIGLU_LIMIT)
            u = jnp.clip(_dot(x, wu_bf[:, cs]) + bu_ref[0, 0, :, cs], -SWIGLU_LIMIT, SWIGLU_LIMIT)
            h = g * jax.nn.sigmoid(SWIGLU_ALPHA * g) * (u + 1.0)
            acc = acc + _dot(h.astype(BF16), wd_bf[cs, :])
        o_ref[...] = _pack_rows(acc + bd_ref[0, 0])

    @pl.when(i >= nb_ref[0])
    def _():
        o_ref[...] = jnp.zeros_like(o_ref)


def _experts(xb, block_e, n_used, layer, wg, bg, wu, bu, wd, bd):
    n_slots = xb.shape[0]
    depth, n_e, d, d_exp = wg.shape
    tm = EXPERT_TILE
    n_blocks = n_slots // tm
    xmap = lambda i, be, nb: (jnp.minimum(i, nb[0] - 1), 0)
    wmap = lambda i, be, nb: (layer, be[i], 0, 0)
    grid_spec = pltpu.PrefetchScalarGridSpec(
        num_scalar_prefetch=2,
        grid=(n_blocks,),
        in_specs=[
            pl.BlockSpec((tm, d // 2), xmap),
            pl.BlockSpec((1, 1, d, d_exp), wmap),
            pl.BlockSpec((1, 1, 1, d_exp), wmap),
            pl.BlockSpec((1, 1, d, d_exp), wmap),
            pl.BlockSpec((1, 1, 1, d_exp), wmap),
            pl.BlockSpec((1, 1, d_exp, d), wmap),
            pl.BlockSpec((1, 1, 1, d), wmap),
        ],
        out_specs=pl.BlockSpec((tm, d // 2), lambda i, be, nb: (i, 0)),
        scratch_shapes=[pltpu.VMEM((d, d_exp), BF16), pltpu.VMEM((d, d_exp), BF16), pltpu.VMEM((d_exp, d), BF16)],
    )
    return pl.pallas_call(
        functools.partial(_expert_kernel, chunk=256),
        out_shape=jax.ShapeDtypeStruct((n_slots, d // 2), jnp.int32),
        grid_spec=grid_spec,
        compiler_params=_params(("arbitrary",)),
        name="experts",
    )(block_e, n_used, xb, wg, bg.reshape(depth, n_e, 1, d_exp), wu, bu.reshape(depth, n_e, 1, d_exp),
      wd, bd.reshape(depth, n_e, 1, d))


def _combine_kernel(x_ref, g2_ref, gates_ref, y_ref, o_ref):
    gates = gates_ref[0]
    half = y_ref.shape[-1]
    acc_lo = jnp.zeros((x_ref.shape[1], half), F32)
    acc_hi = jnp.zeros((x_ref.shape[1], half), F32)
    for k in range(TOP_K):
        lo, hi = _unpack_rows(y_ref[k])
        acc_lo = acc_lo + gates[:, k:k + 1] * lo
        acc_hi = acc_hi + gates[:, k:k + 1] * hi
    o_ref[0] = x_ref[0] + g2_ref[0] * jnp.concatenate([acc_lo, acc_hi], axis=1)


def _combine(x, g2, gates, y_sel, tok_offset):
    b, t, d = x.shape
    tm = int(np.gcd(min(t, 512), tok_offset)) if tok_offset else min(t, 512)
    n_t = t // tm
    off = tok_offset // tm
    return pl.pallas_call(
        _combine_kernel,
        out_shape=jax.ShapeDtypeStruct((b, t, d), F32),
        grid=(b, n_t),
        in_specs=[
            pl.BlockSpec((1, tm, d), lambda bi, i: (bi, i, 0)),
            pl.BlockSpec((1, 1, d), lambda bi, i: (bi, 0, 0)),
            pl.BlockSpec((1, tm, ROUTER_PAD), lambda bi, i: (bi, i, 0)),
            pl.BlockSpec((TOP_K, tm, d // 2), lambda bi, i: (0, off + bi * n_t + i, 0)),
        ],
        out_specs=pl.BlockSpec((1, tm, d), lambda bi, i: (bi, i, 0)),
        compiler_params=_params(("parallel", "parallel")),
        name="moe_combine",
    )(x, g2, gates, y_sel)


def _row_gather(table, idx):
    info = plsc.get_sparse_core_info()
    n_cores, n_workers = info.num_cores, info.num_cores * info.num_subcores
    n_rows, width = idx.shape[0], table.shape[1]
    per_worker = n_rows // n_workers
    n_chunks = per_worker // SC_GATHER_ROWS
    assert per_worker * n_workers == n_rows and n_chunks * SC_GATHER_ROWS == per_worker
    mesh = plsc.VectorSubcoreMesh(core_axis_name="c", subcore_axis_name="s")

    @functools.partial(
        pl.kernel, mesh=mesh,
        out_type=jax.ShapeDtypeStruct((n_rows, width), table.dtype),
        scratch_types=[pltpu.VMEM((SC_GATHER_ROWS,), jnp.int32), pltpu.VMEM((SC_GATHER_ROWS, width), table.dtype),
                       pltpu.SemaphoreType.DMA],
    )
    def gather_kernel(table_hbm, idx_hbm, out_hbm, idx_v, rows_v, sem):
        base = (lax.axis_index("s") * n_cores + lax.axis_index("c")) * per_worker

        @pl.loop(0, n_chunks)
        def _(c):
            off = base + c * SC_GATHER_ROWS
            pltpu.sync_copy(idx_hbm.at[pl.ds(off, SC_GATHER_ROWS)], idx_v)
            pltpu.async_copy(table_hbm.at[idx_v], rows_v, sem).wait()
            pltpu.sync_copy(rows_v, out_hbm.at[pl.ds(off, SC_GATHER_ROWS)])

    return gather_kernel(table, idx)


def _moe(h2, ids, layer, w_exp):
    n = h2.shape[0]
    nk = n * TOP_K
    tm = EXPERT_TILE
    e_flat = ids.reshape(-1)
    onehot = (e_flat[:, None] == jnp.arange(N_EXPERTS, dtype=jnp.int32)[None]).astype(jnp.int32)
    csum = jnp.cumsum(onehot, axis=0)
    counts = csum[-1]
    padded = (counts + tm - 1) // tm * tm
    pends = jnp.cumsum(padded)
    dest = jnp.sum(onehot * (csum - 1 + (pends - padded)[None]), axis=1)
    n_blocks = -(-nk // tm) + N_EXPERTS
    n_slots = n_blocks * tm
    tok = jnp.arange(nk, dtype=jnp.int32) // TOP_K
    slot_tok = jnp.zeros((n_slots,), jnp.int32).at[dest].set(tok)
    block_start = jnp.arange(n_blocks, dtype=jnp.int32) * tm
    block_e = jnp.minimum(jnp.sum((pends[None] <= block_start[:, None]).astype(jnp.int32), axis=1), N_EXPERTS - 1)
    n_used = (pends[-1] // tm).astype(jnp.int32).reshape(1)
    xb = _row_gather(h2, slot_tok)
    yb = _experts(xb, block_e, n_used, layer, *w_exp)
    return _row_gather(yb, dest.reshape(n, TOP_K).T.reshape(-1)).reshape(TOP_K, n, -1)


def _mod_parts(mod_l, b):
    d = mod_l.shape[-1] // 6
    lat = [mod_l[:b, k * d:(k + 1) * d].reshape(b, 1, d) for k in range(6)]
    ctx = [jnp.broadcast_to(mod_l[b, k * d:(k + 1) * d].reshape(1, 1, d), (b, 1, d)) for k in range(6)]
    return lat, ctx


def kernel(x, c, ctx, c_ctx, ada_w, ada_b, norm1_g, norm2_g, ev_w_in, ev_w_out, ev_q_gain, ev_k_gain, ev_rpb, ev_conv_w, ev_conv_b, od_w_in, od_w_out, od_conv_w, od_conv_b, od_fwd_wa, od_fwd_ba, od_fwd_wx, od_fwd_bx, od_fwd_lam, od_bwd_wa, od_bwd_ba, od_bwd_wx, od_bwd_bx, od_bwd_lam, router_w, router_b, exp_w_gate, exp_b_gate, exp_w_up, exp_b_up, exp_w_down, exp_b_down):
    b, t, d = x.shape
    l = ctx.shape[1]
    assert ada_w.shape[0] == DEPTH == 2 and t % GRID_W == 0 and t // GRID_W >= WIN_H

    n_rows_c = -(-(b + 1) // SUBLANES) * SUBLANES
    cvec = jnp.zeros((n_rows_c, d), F32).at[:b].set(c).at[b].set(c_ctx)
    mod = _ada_mod(cvec, ada_w, ada_b)

    def router(layer):
        w_r = jnp.zeros((d, ROUTER_PAD), F32).at[:, :N_EXPERTS].set(router_w[layer]).astype(BF16)
        b_r = jnp.zeros((1, ROUTER_PAD), F32).at[0, :N_EXPERTS].set(router_b[layer])
        return w_r, b_r

    w_exp = (exp_w_gate, exp_b_gate, exp_w_up, exp_b_up, exp_w_down, exp_b_down)

    (sh1, sc1, g1, sh2, sc2, g2), (csh1, csc1, cg1, csh2, csc2, cg2) = _mod_parts(mod[0], b)
    n1 = norm1_g[0].reshape(1, d)
    n2 = norm2_g[0].reshape(1, d)
    w_in = ev_w_in[0].astype(BF16)
    w_out = ev_w_out[0].astype(BF16)
    q_scale = NA_HEAD_DIM ** -0.5
    head_gain = jnp.stack([jnp.tile(ev_q_gain[0] * q_scale, NA_HEADS), jnp.tile(ev_k_gain[0], NA_HEADS)])
    head_gain = head_gain.reshape(2, 1, NA_WIDTH).astype(F32)
    proj = _inproj(x, n1, sh1, sc1, w_in, head_gain)
    proj_c = _inproj(ctx, n1, csh1, csc1, w_in, head_gain)
    kh, r0, row_type, patterns = _na_tables(t // GRID_W)
    bias_tab = _na_bias_table(ev_rpb[0], patterns)
    o_a = _neighbourhood_attention(proj, proj_c, bias_tab, r0, row_type, kh)
    oc_a = _context_attention(proj_c)
    w_r, b_r = router(0)
    x1, h2, ids, gates = _even_out(o_a, proj, ev_conv_w[0], ev_conv_b[0], w_out, x, g1, n2, sh2, sc2, w_r, b_r)
    c1, h2c, ids_c, gates_c = _even_out(oc_a, proj_c, ev_conv_w[0], ev_conv_b[0], w_out, ctx, cg1, n2, csh2, csc2,
                                        w_r, b_r)
    tokens = jnp.concatenate([h2c.reshape(b * l, d // 2), h2.reshape(b * t, d // 2)], axis=0)
    ids_all = jnp.concatenate([ids_c.reshape(b * l, ROUTER_PAD), ids.reshape(b * t, ROUTER_PAD)], axis=0)[:, :TOP_K]
    y_sel = _moe(tokens, ids_all, 0, w_exp)
    hctx = _combine(c1, cg2, gates_c, y_sel, 0)
    x = _combine(x1, g2, gates, y_sel, b * l)

    (sh1, sc1, g1, sh2, sc2, g2), (csh1, csc1, _, _, _, _) = _mod_parts(mod[1], b)
    n1 = norm1_g[1].reshape(1, d)
    n2 = norm2_g[1].reshape(1, d)
    w_in = od_w_in[0].astype(BF16)
    width = w_in.shape[1] // 2
    proj = _inproj(x, n1, sh1, sc1, w_in)
    u_ctx = _inproj(hctx, n1, csh1, csc1, w_in[:, width:])
    h_dir = []
    for reverse, (wa, ba, wx, bx, lam) in ((False, (od_fwd_wa, od_fwd_ba, od_fwd_wx, od_fwd_bx, od_fwd_lam)),
                                           (True, (od_bwd_wa, od_bwd_ba, od_bwd_wx, od_bwd_bx, od_bwd_lam))):
        w_cat = jnp.concatenate([wa[0], wx[0]], axis=-1).astype(BF16)
        h_dir.append(_lru_scan(proj, u_ctx, od_conv_w[0], od_conv_b[0], w_cat, ba[0], bx[0], lam[0], reverse))
    w_r, b_r = router(1)
    x1, h2, ids, gates = _odd_out(h_dir[0], h_dir[1], proj, od_w_out[0].astype(BF16), x, g1, n2, sh2, sc2, w_r, b_r)
    y_sel = _moe(h2.reshape(b * t, d // 2), ids.reshape(b * t, ROUTER_PAD)[:, :TOP_K], 1, w_exp)
    return _combine(x1, g2, gates, y_sel, 0)
```

```python
import functools

import numpy as np
import jax
import jax.numpy as jnp
from jax import lax
from jax.experimental import pallas as pl
from jax.experimental.pallas import tpu as pltpu
from jax.experimental.pallas import tpu_sc as plsc

DEPTH = 2
GRID_W = 64
EPS = 1e-6
NEG_INF = -1e30
NA_HEADS = 8
NA_HEAD_DIM = 64
NA_WIDTH = NA_HEADS * NA_HEAD_DIM
HEAD_PAIRS = NA_HEADS // 2
WIN_H = 8
WIN_W = 16
SC_CONV = 3
LRU_BLOCKS = 4
LRU_CONV = 4
LRU_C = 8.0
N_EXPERTS = 32
TOP_K = 4
SWIGLU_LIMIT = 7.0
SWIGLU_ALPHA = 1.702

LANES = 128
SUBLANES = 8
HALO = 16
ROUTER_PAD = LANES
EXPERT_TILE = 512
SC_GATHER_ROWS = 64
VMEM_LIMIT = 56 * 1024 * 1024

F32 = jnp.float32
BF16 = jnp.bfloat16


def _params(sem, vmem=VMEM_LIMIT):
    return pltpu.CompilerParams(dimension_semantics=sem, vmem_limit_bytes=vmem)


def _dot(a, b):
    return jnp.dot(a, b, preferred_element_type=F32)


def _dot_nt(a, b):
    return lax.dot_general(a, b, (((1,), (1,)), ((), ())), preferred_element_type=F32)


def _pack_rows(v):
    w = v.shape[-1] // 2
    lo = lax.bitcast_convert_type(v[:, :w].astype(BF16).astype(F32), jnp.int32)
    hi = lax.bitcast_convert_type(v[:, w:].astype(BF16).astype(F32), jnp.int32)
    return lax.shift_right_logical(lo, 16) | (hi & jnp.int32(-65536))


def _unpack_rows(p):
    lo = lax.bitcast_convert_type(lax.shift_left(p, 16), F32)
    hi = lax.bitcast_convert_type(p & jnp.int32(-65536), F32)
    return lo, hi


def _rms_mod(x, g, shift, scale):
    ms = jnp.mean(x * x, axis=-1, keepdims=True)
    y = x * lax.rsqrt(ms + EPS) * g
    return y * (1.0 + scale) + shift


def _ada_kernel(c_ref, w_ref, b_ref, o_ref):
    c = c_ref[...]
    s = (c * jax.nn.sigmoid(c)).astype(BF16)
    o_ref[0] = _dot(s, w_ref[0].astype(BF16)) + b_ref[0]


def _ada_mod(cvec, ada_w, ada_b):
    depth, d, n = ada_w.shape
    r = cvec.shape[0]
    tn = 1536
    return pl.pallas_call(
        _ada_kernel,
        out_shape=jax.ShapeDtypeStruct((depth, r, n), F32),
        grid=(depth, n // tn),
        in_specs=[
            pl.BlockSpec((r, d), lambda l, j: (0, 0)),
            pl.BlockSpec((1, d, tn), lambda l, j: (l, 0, j)),
            pl.BlockSpec((1, 1, tn), lambda l, j: (l, 0, j)),
        ],
        out_specs=pl.BlockSpec((1, r, tn), lambda l, j: (l, 0, j)),
        compiler_params=_params(("parallel", "parallel")),
        name="ada_mod",
    )(cvec, ada_w, ada_b.reshape(depth, 1, n))


def _inproj_kernel(x_ref, g_ref, sh_ref, sc_ref, w_ref, hg_ref, ones_ref, o_ref, *, n_tiles, tn, n_headnorm):
    h = _rms_mod(x_ref[0], g_ref[...], sh_ref[0], sc_ref[0]).astype(BF16)
    for j in range(n_tiles):
        y = _dot(h, w_ref[:, j * tn:(j + 1) * tn])
        if j < n_headnorm:
            ms = _dot((y * y).astype(BF16), ones_ref[...]) * (1.0 / NA_HEAD_DIM)
            y = y * lax.rsqrt(ms + EPS) * hg_ref[j]
        o_ref[0, :, j * tn:(j + 1) * tn] = y.astype(o_ref.dtype)


def _inproj(x, g, shift, scale, w, head_gain=None):
    b, t, d = x.shape
    n = w.shape[1]
    tn = NA_WIDTH
    tm = min(t, 512)
    n_headnorm = 0 if head_gain is None else head_gain.shape[0]
    if head_gain is None:
        head_gain = jnp.ones((1, 1, tn), F32)
    hid = np.arange(tn) // NA_HEAD_DIM
    ones_bd = jnp.asarray((hid[:, None] == hid[None, :]), BF16)
    kern = functools.partial(_inproj_kernel, n_tiles=n // tn, tn=tn, n_headnorm=n_headnorm)
    return pl.pallas_call(
        kern,
        out_shape=jax.ShapeDtypeStruct((b, t, n), BF16),
        grid=(b, t // tm),
        in_specs=[
            pl.BlockSpec((1, tm, d), lambda bi, i: (bi, i, 0)),
            pl.BlockSpec((1, d), lambda bi, i: (0, 0)),
            pl.BlockSpec((1, 1, d), lambda bi, i: (bi, 0, 0)),
            pl.BlockSpec((1, 1, d), lambda bi, i: (bi, 0, 0)),
            pl.BlockSpec((d, n), lambda bi, i: (0, 0)),
            pl.BlockSpec(head_gain.shape, lambda bi, i: (0, 0, 0)),
            pl.BlockSpec((tn, tn), lambda bi, i: (0, 0)),
        ],
        out_specs=pl.BlockSpec((1, tm, n), lambda bi, i: (bi, i, 0)),
        compiler_params=_params(("parallel", "parallel")),
        name="inproj",
    )(x, g, shift, scale, w, head_gain, ones_bd)


def _na_tables(rows):
    kh = min(WIN_H, rows)
    r = np.arange(rows)
    r0 = np.clip(r - kh // 2, 0, rows - kh)
    dr = r0[:, None] + np.arange(kh)[None] - r[:, None] + WIN_H - 1
    patterns, row_type = np.unique(dr, axis=0, return_inverse=True)
    return kh, r0.astype(np.int32), row_type.reshape(-1).astype(np.int32), patterns


def _na_bias_table(rpb, patterns):
    qc = np.arange(GRID_W)
    kc = np.arange(GRID_W)
    c0 = np.clip(qc - WIN_W // 2, 0, GRID_W - WIN_W)[:, None]
    valid = (kc[None] >= c0) & (kc[None] < c0 + WIN_W)
    dc = np.clip(kc[None] - qc[:, None] + WIN_W - 1, 0, 2 * WIN_W - 2)
    n_pat, kh = patterns.shape
    onehot_dc = jnp.asarray(dc[None] == np.arange(2 * WIN_W - 1)[:, None, None], F32)
    tab = jnp.einsum('hpic,cqk->hpiqk', rpb.astype(F32)[:, patterns], onehot_dc,
                     precision=lax.Precision.HIGHEST)
    tab = jnp.where(valid[None, None, None], tab, NEG_INF)
    tab = tab.reshape(HEAD_PAIRS, 2, n_pat, kh, GRID_W, GRID_W)
    tab = tab.transpose(2, 0, 1, 4, 3, 5)
    return tab.reshape(n_pat, HEAD_PAIRS, 2 * GRID_W, kh * GRID_W)


def _pair_attention(q, keys, values, biases):
    m = q.shape[0]
    lane = lax.broadcasted_iota(jnp.int32, q.shape, 1)
    zero = jnp.zeros_like(q)
    qq = jnp.concatenate([jnp.where(lane < NA_HEAD_DIM, q, zero), jnp.where(lane >= NA_HEAD_DIM, q, zero)], axis=0)
    scores = []
    for k, bias in zip(keys, biases):
        s = _dot_nt(qq, k)
        scores.append(s if bias is None else s + bias)
    mx = functools.reduce(jnp.maximum, [jnp.max(s, axis=-1, keepdims=True) for s in scores])
    exps = [jnp.exp(s - mx) for s in scores]
    denom = functools.reduce(jnp.add, [jnp.sum(e, axis=-1, keepdims=True) for e in exps])
    o = functools.reduce(jnp.add, [_dot(e.astype(BF16), v) for e, v in zip(exps, values)])
    o = o * (1.0 / denom)
    lane_o = lax.broadcasted_iota(jnp.int32, (m, LANES), 1)
    return jnp.where(lane_o < NA_HEAD_DIM, o[:m], o[m:])


def _na_kernel(r0_ref, type_ref, q_ref, k_ref, v_ref, kc_ref, vc_ref, bias_ref, o_ref, *, kh):
    r = pl.program_id(1)
    start = pl.multiple_of(r0_ref[r] * GRID_W, GRID_W)
    rtype = type_ref[r]
    for p in range(HEAD_PAIRS):
        cols = slice(p * LANES, (p + 1) * LANES)
        o = _pair_attention(
            q_ref[0, :, cols],
            [k_ref[0, pl.ds(start, kh * GRID_W), cols], kc_ref[0, :, cols]],
            [v_ref[0, pl.ds(start, kh * GRID_W), cols], vc_ref[0, :, cols]],
            [bias_ref[rtype, p], None],
        )
        o_ref[0, :, cols] = o.astype(o_ref.dtype)


def _neighbourhood_attention(proj, proj_c, bias_tab, r0, row_type, kh):
    b, t, _ = proj.shape
    l = proj_c.shape[1]
    rows = t // GRID_W
    w = NA_WIDTH
    grid_spec = pltpu.PrefetchScalarGridSpec(
        num_scalar_prefetch=2,
        grid=(b, rows),
        in_specs=[
            pl.BlockSpec((1, GRID_W, w), lambda bi, r, *_: (bi, r, 0)),
            pl.BlockSpec((1, t, w), lambda bi, r, *_: (bi, 0, 1)),
            pl.BlockSpec((1, t, w), lambda bi, r, *_: (bi, 0, 2)),
            pl.BlockSpec((1, l, w), lambda bi, r, *_: (bi, 0, 1)),
            pl.BlockSpec((1, l, w), lambda bi, r, *_: (bi, 0, 2)),
            pl.BlockSpec(bias_tab.shape, lambda bi, r, *_: (0, 0, 0, 0)),
        ],
        out_specs=pl.BlockSpec((1, GRID_W, w), lambda bi, r, *_: (bi, r, 0)),
    )
    return pl.pallas_call(
        functools.partial(_na_kernel, kh=kh),
        out_shape=jax.ShapeDtypeStruct((b, t, w), BF16),
        grid_spec=grid_spec,
        compiler_params=_params(("parallel", "arbitrary")),
        name="na_attention",
    )(jnp.asarray(r0), jnp.asarray(row_type), proj, proj, proj, proj_c, proj_c, bias_tab)


def _ctx_attn_kernel(q_ref, k_ref, v_ref, o_ref):
    for p in range(HEAD_PAIRS):
        cols = slice(p * LANES, (p + 1) * LANES)
        o = _pair_attention(q_ref[0, :, cols], [k_ref[0, :, cols]], [v_ref[0, :, cols]], [None])
        o_ref[0, :, cols] = o.astype(o_ref.dtype)


def _context_attention(proj_c):
    b, l, _ = proj_c.shape
    w = NA_WIDTH
    return pl.pallas_call(
        _ctx_attn_kernel,
        out_shape=jax.ShapeDtypeStruct((b, l, w), BF16),
        grid=(b,),
        in_specs=[pl.BlockSpec((1, l, w), lambda bi, j=j: (bi, 0, j)) for j in range(3)],
        out_specs=pl.BlockSpec((1, l, w), lambda bi: (bi, 0, 0)),
        compiler_params=_params(("parallel",)),
        name="ctx_attention",
    )(proj_c, proj_c, proj_c)


def _top4(logits):
    lane = lax.broadcasted_iota(jnp.int32, logits.shape, 1)
    cur = jnp.where(lane < N_EXPERTS, logits, -jnp.inf)
    vals, idxs = [], []
    for _ in range(TOP_K):
        m = jnp.max(cur, axis=-1, keepdims=True)
        first = jnp.min(jnp.where(cur == m, lane, ROUTER_PAD).astype(F32), axis=-1, keepdims=True)
        idx = first.astype(jnp.int32)
        vals.append(m)
        idxs.append(idx)
        cur = jnp.where(lane == idx, -jnp.inf, cur)
    exps = [jnp.exp(v - vals[0]) for v in vals]
    inv = 1.0 / functools.reduce(jnp.add, exps)
    ids = jnp.zeros(logits.shape, jnp.int32)
    gates = jnp.zeros(logits.shape, F32)
    for k in range(TOP_K):
        ids = jnp.where(lane == k, idxs[k], ids)
        gates = jnp.where(lane == k, exps[k] * inv, gates)
    return ids, gates


def _layer_tail(y, x_ref, g1_ref, n2_ref, sh2_ref, sc2_ref, wr_ref, br_ref, xo_ref, h2_ref, ids_ref, gates_ref):
    x_new = x_ref[0] + g1_ref[0] * y
    xo_ref[0] = x_new
    h2 = _rms_mod(x_new, n2_ref[...], sh2_ref[0], sc2_ref[0])
    h2_ref[0] = _pack_rows(h2)
    logits = _dot(h2.astype(BF16), wr_ref[...]) + br_ref[...]
    ids, gates = _top4(logits)
    ids_ref[0] = ids
    gates_ref[0] = gates


def _tail_specs(b, t, d, tm):
    row = lambda bi, i: (bi, i, 0)
    per_b = lambda bi, i: (bi, 0, 0)
    const = lambda bi, i: (0, 0)
    in_specs = [
        pl.BlockSpec((1, tm, d), row),
        pl.BlockSpec((1, 1, d), per_b),
        pl.BlockSpec((1, d), const),
        pl.BlockSpec((1, 1, d), per_b),
        pl.BlockSpec((1, 1, d), per_b),
        pl.BlockSpec((d, ROUTER_PAD), const),
        pl.BlockSpec((1, ROUTER_PAD), const),
    ]
    out_specs = [
        pl.BlockSpec((1, tm, d), row),
        pl.BlockSpec((1, tm, d // 2), row),
        pl.BlockSpec((1, tm, ROUTER_PAD), row),
        pl.BlockSpec((1, tm, ROUTER_PAD), row),
    ]
    out_shape = [
        jax.ShapeDtypeStruct((b, t, d), F32),
        jax.ShapeDtypeStruct((b, t, d // 2), jnp.int32),
        jax.ShapeDtypeStruct((b, t, ROUTER_PAD), jnp.int32),
        jax.ShapeDtypeStruct((b, t, ROUTER_PAD), F32),
    ]
    return in_specs, out_specs, out_shape


def _halo_fix(rolled, row, at_row, halo_row, present):
    fill = jnp.where(present, halo_row, jnp.zeros_like(halo_row))
    return jnp.where(row == at_row, fill, rolled)


def _even_out_kernel(oa_ref, bg_ref, cg_ref, xin_ref, cgp_ref, xinp_ref, cgn_ref, xinn_ref, cw_ref, cb_ref,
                     wa_ref, wb_ref, *tail_refs, tm):
    i = pl.program_id(1)
    has_prev = i > 0
    has_next = i < pl.num_programs(1) - 1
    u = cg_ref[0].astype(F32) * xin_ref[0].astype(F32)
    u_prev = (cgp_ref[0].astype(F32) * xinp_ref[0].astype(F32))[HALO - 1:HALO]
    u_next = (cgn_ref[0].astype(F32) * xinn_ref[0].astype(F32))[0:1]
    row = lax.broadcasted_iota(jnp.int32, u.shape, 0)
    u_m1 = _halo_fix(pltpu.roll(u, 1, 0), row, 0, u_prev, has_prev)
    u_p1 = _halo_fix(pltpu.roll(u, tm - 1, 0), row, tm - 1, u_next, has_next)
    cw = cw_ref[...]
    conv = u_m1 * cw[0:1] + u * cw[1:2] + u_p1 * cw[2:3] + cb_ref[...]
    o_b = bg_ref[0].astype(F32) * conv
    y = _dot(oa_ref[0], wa_ref[...]) + _dot(o_b.astype(BF16), wb_ref[...])
    _layer_tail(y, *tail_refs)


def _even_out(o_a, proj, conv_w, conv_b, w_out, x, g1, n2, sh2, sc2, w_r, b_r):
    b, t, d = x.shape
    w = NA_WIDTH
    tm = min(t, 512)
    hb = tm // HALO
    n_hblocks = t // HALO
    row = lambda bi, i: (bi, i, 0)
    const = lambda bi, i: (0, 0)
    prev = lambda col: (lambda bi, i: (bi, jnp.maximum(i * hb - 1, 0), col))
    nxt = lambda col: (lambda bi, i: (bi, jnp.minimum((i + 1) * hb, n_hblocks - 1), col))
    tail_in, out_specs, out_shape = _tail_specs(b, t, d, tm)
    in_specs = [
        pl.BlockSpec((1, tm, w), row),
        pl.BlockSpec((1, tm, w), lambda bi, i: (bi, i, 3)),
        pl.BlockSpec((1, tm, w), lambda bi, i: (bi, i, 4)),
        pl.BlockSpec((1, tm, w), lambda bi, i: (bi, i, 5)),
        pl.BlockSpec((1, HALO, w), prev(4)),
        pl.BlockSpec((1, HALO, w), prev(5)),
        pl.BlockSpec((1, HALO, w), nxt(4)),
        pl.BlockSpec((1, HALO, w), nxt(5)),
        pl.BlockSpec((SC_CONV, w), const),
        pl.BlockSpec((1, w), const),
        pl.BlockSpec((w, d), const),
        pl.BlockSpec((w, d), const),
    ] + tail_in
    return pl.pallas_call(
        functools.partial(_even_out_kernel, tm=tm),
        out_shape=out_shape,
        grid=(b, t // tm),
        in_specs=in_specs,
        out_specs=out_specs,
        compiler_params=_params(("parallel", "parallel")),
        name="even_out",
    )(o_a, proj, proj, proj, proj, proj, proj, proj, conv_w, conv_b.reshape(1, w),
      w_out[:w], w_out[w:], x, g1, n2, sh2, sc2, w_r, b_r)


def _log_sigmoid(x):
    return jnp.minimum(x, 0.0) - jnp.log1p(jnp.exp(-jnp.abs(x)))


def _lru_tile(u_ref, up_ref, un_ref, has_prev, has_next, cw_ref, cb_ref, w_ref, ba_ref, bx_ref, lam_ref,
              carry_ref, o_ref, *, ts, reverse):
    width = u_ref.shape[-1]
    blk = width // LRU_BLOCKS
    u = u_ref[0].astype(F32)
    up = up_ref[0].astype(F32)
    un = un_ref[0].astype(F32)
    row = lax.broadcasted_iota(jnp.int32, u.shape, 0)
    u_m1 = _halo_fix(pltpu.roll(u, 1, 0), row, 0, up[HALO - 1:HALO], has_prev)
    u_p1 = _halo_fix(pltpu.roll(u, ts - 1, 0), row, ts - 1, un[0:1], has_next)
    u_p2 = _halo_fix(pltpu.roll(u, ts - 2, 0), row, ts - 2, un[0:1], has_next)
    u_p2 = _halo_fix(u_p2, row, ts - 1, un[1:2], has_next)
    cw = cw_ref[...]
    uc = u_m1 * cw[0:1] + u * cw[1:2] + u_p1 * cw[2:3] + u_p2 * cw[3:4] + cb_ref[...]
    ucb = uc.astype(BF16)
    za, zx = [], []
    for h in range(LRU_BLOCKS):
        z = _dot(ucb[:, h * blk:(h + 1) * blk], w_ref[h])
        za.append(z[:, :blk])
        zx.append(z[:, blk:])
    r = jax.nn.sigmoid(jnp.concatenate(za, axis=1) + ba_ref[...])
    gate_i = jax.nn.sigmoid(jnp.concatenate(zx, axis=1) + bx_ref[...])
    log_a = (LRU_C * _log_sigmoid(lam_ref[...])) * r
    a = jnp.exp(log_a)
    th = jnp.tanh(log_a)
    bcoef = jnp.sqrt(-2.0 * th / (1.0 - th)) * gate_i * uc

    groups = ts // SUBLANES
    a3 = a.reshape(groups, SUBLANES, width)
    b3 = bcoef.reshape(groups, SUBLANES, width)
    sub = lax.broadcasted_iota(jnp.int32, a3.shape, 1)
    for s in (1, 2, 4):
        shift = SUBLANES - s if reverse else s
        a_sh = pltpu.roll(a3, shift, 1)
        b_sh = pltpu.roll(b3, shift, 1)
        m = (sub < SUBLANES - s) if reverse else (sub >= s)
        b3 = jnp.where(m, a3 * b_sh + b3, b3)
        a3 = jnp.where(m, a3 * a_sh, a3)

    h = carry_ref[0:1, :]
    order = range(groups - 1, -1, -1) if reverse else range(groups)
    edge = 0 if reverse else SUBLANES - 1
    for g in order:
        hg = a3[g] * h + b3[g]
        if o_ref is not None:
            o_ref[0, g * SUBLANES:(g + 1) * SUBLANES, :] = hg.astype(o_ref.dtype)
        h = hg[edge:edge + 1, :]
    carry_ref[...] = jnp.broadcast_to(h, carry_ref.shape)


def _lru_kernel(uc_ref, ucp_ref, ucn_ref, ul_ref, ulp_ref, uln_ref, cw_ref, cb_ref, w_ref, ba_ref, bx_ref,
                lam_ref, o_ref, carry_ref, *, ts, n_ctx_tiles, n_lat_tiles, reverse):
    j = pl.program_id(1)
    shared = (cw_ref, cb_ref, w_ref, ba_ref, bx_ref, lam_ref, carry_ref)

    @pl.when(j == 0)
    def _():
        carry_ref[...] = jnp.zeros_like(carry_ref)

    def tile_pos(step, n):
        return (n - 1 - step) if reverse else step

    @pl.when(j < n_ctx_tiles)
    def _():
        pos = tile_pos(j, n_ctx_tiles)
        _lru_tile(uc_ref, ucp_ref, ucn_ref, pos > 0, pos < n_ctx_tiles - 1, *shared, None, ts=ts, reverse=reverse)

    @pl.when(j >= n_ctx_tiles)
    def _():
        pos = tile_pos(j - n_ctx_tiles, n_lat_tiles)
        _lru_tile(ul_ref, ulp_ref, uln_ref, pos > 0, pos < n_lat_tiles - 1, *shared, o_ref, ts=ts, reverse=reverse)


def _lru_scan(proj, u_ctx, conv_w, conv_b, w_cat, ba, bx, lam, reverse):
    b, t, _ = proj.shape
    l, width = u_ctx.shape[1], u_ctx.shape[2]
    ts = min(256, l, t)
    n_c, n_l = l // ts, t // ts
    hb = ts // HALO

    def pos_of(step, n):
        step = jnp.clip(step, 0, n - 1)
        return (n - 1 - step) if reverse else step

    def tile_map(off, n, col):
        return lambda bi, j: (bi, pos_of(j - off, n), col)

    def prev_map(off, n, col):
        return lambda bi, j: (bi, jnp.maximum(pos_of(j - off, n) * hb - 1, 0), col)

    def next_map(off, n, col):
        return lambda bi, j: (bi, jnp.minimum((pos_of(j - off, n) + 1) * hb, n * hb - 1), col)

    const2 = lambda bi, j: (0, 0)
    in_specs = [
        pl.BlockSpec((1, ts, width), tile_map(0, n_c, 0)),
        pl.BlockSpec((1, HALO, width), prev_map(0, n_c, 0)),
        pl.BlockSpec((1, HALO, width), next_map(0, n_c, 0)),
        pl.BlockSpec((1, ts, width), tile_map(n_c, n_l, 1)),
        pl.BlockSpec((1, HALO, width), prev_map(n_c, n_l, 1)),
        pl.BlockSpec((1, HALO, width), next_map(n_c, n_l, 1)),
        pl.BlockSpec((LRU_CONV, width), const2),
        pl.BlockSpec((1, width), const2),
        pl.BlockSpec(w_cat.shape, lambda bi, j: (0, 0, 0)),
        pl.BlockSpec((1, width), const2),
        pl.BlockSpec((1, width), const2),
        pl.BlockSpec((1, width), const2),
    ]
    kern = functools.partial(_lru_kernel, ts=ts, n_ctx_tiles=n_c, n_lat_tiles=n_l, reverse=reverse)
    return pl.pallas_call(
        kern,
        out_shape=jax.ShapeDtypeStruct((b, t, width), BF16),
        grid=(b, n_c + n_l),
        in_specs=in_specs,
        out_specs=pl.BlockSpec((1, ts, width), tile_map(n_c, n_l, 0)),
        scratch_shapes=[pltpu.VMEM((SUBLANES, width), F32)],
        compiler_params=_params(("parallel", "arbitrary")),
        name="lru_scan_bwd" if reverse else "lru_scan_fwd",
    )(u_ctx, u_ctx, u_ctx, proj, proj, proj, conv_w, conv_b.reshape(1, width), w_cat,
      ba.reshape(1, width), bx.reshape(1, width), lam.reshape(1, width))


def _odd_out_kernel(hf_ref, hb_ref, gate_ref, w_ref, *tail_refs):
    hsum = hf_ref[0].astype(F32) + hb_ref[0].astype(F32)
    z = hsum * jax.nn.gelu(gate_ref[0].astype(F32), approximate=True)
    y = _dot(z.astype(BF16), w_ref[...])
    _layer_tail(y, *tail_refs)


def _odd_out(h_f, h_b, proj, w_out, x, g1, n2, sh2, sc2, w_r, b_r):
    b, t, d = x.shape
    width = h_f.shape[-1]
    tm = min(t, 512)
    row = lambda bi, i: (bi, i, 0)
    tail_in, out_specs, out_shape = _tail_specs(b, t, d, tm)
    in_specs = [
        pl.BlockSpec((1, tm, width), row),
        pl.BlockSpec((1, tm, width), row),
        pl.BlockSpec((1, tm, width), row),
        pl.BlockSpec((width, d), lambda bi, i: (0, 0)),
    ] + tail_in
    return pl.pallas_call(
        _odd_out_kernel,
        out_shape=out_shape,
        grid=(b, t // tm),
        in_specs=in_specs,
        out_specs=out_specs,
        compiler_params=_params(("parallel", "parallel")),
        name="odd_out",
    )(h_f, h_b, proj, w_out, x, g1, n2, sh2, sc2, w_r, b_r)


def _expert_kernel(be_ref, rows_ref, nb_ref, x_ref, wg_ref, bg_ref, wu_ref, bu_ref, wd_ref, bd_ref, o_ref,
                   wg_bf, wu_bf, wd_bf, *, chunk):
    i = pl.program_id(0)

    @pl.when((i == 0) | (be_ref[i] != be_ref[jnp.maximum(i - 1, 0)]))
    def _():
        wg_bf[...] = wg_ref[0, 0].astype(BF16)
        wu_bf[...] = wu_ref[0, 0].astype(BF16)
        wd_bf[...] = wd_ref[0, 0].astype(BF16)

    @pl.when(i < nb_ref[0])
    def _():
        xp = x_ref[...]
        row = lax.broadcasted_iota(jnp.int32, xp.shape, 0)
        xp = jnp.where(row < rows_ref[i], xp, 0)
        x = jnp.concatenate(_unpack_rows(xp), axis=1).astype(BF16)
        d_exp = wg_bf.shape[1]
        acc = jnp.zeros((x.shape[0], wd_bf.shape[1]), F32)
        for c in range(d_exp // chunk):
            cs = slice(c * chunk, (c + 1) * chunk)
            g = jnp.minimum(_dot(x, wg_bf[:, cs]) + bg_ref[0, 0, :, cs], SWIGLU_LIMIT)
            u = jnp.clip(_dot(x, wu_bf[:, cs]) + bu_ref[0, 0, :, cs], -SWIGLU_LIMIT, SWIGLU_LIMIT)
            h = g * jax.nn.sigmoid(SWIGLU_ALPHA * g) * (u + 1.0)
            acc = acc + _dot(h.astype(BF16), wd_bf[cs, :])
        o_ref[...] = _pack_rows(acc + bd_ref[0, 0])

    @pl.when(i >= nb_ref[0])
    def _():
        o_ref[...] = jnp.zeros_like(o_ref)


def _experts(xb, block_e, block_rows, n_used, layer, wg, bg, wu, bu, wd, bd):
    n_slots = xb.shape[0]
    depth, n_e, d, d_exp = wg.shape
    tm = EXPERT_TILE
    n_blocks = n_slots // tm
    xmap = lambda i, be, br, nb: (jnp.minimum(i, nb[0] - 1), 0)
    wmap = lambda i, be, br, nb: (layer, be[i], 0, 0)
    grid_spec = pltpu.PrefetchScalarGridSpec(
        num_scalar_prefetch=3,
        grid=(n_blocks,),
        in_specs=[
            pl.BlockSpec((tm, d // 2), xmap),
            pl.BlockSpec((1, 1, d, d_exp), wmap),
            pl.BlockSpec((1, 1, 1, d_exp), wmap),
            pl.BlockSpec((1, 1, d, d_exp), wmap),
            pl.BlockSpec((1, 1, 1, d_exp), wmap),
            pl.BlockSpec((1, 1, d_exp, d), wmap),
            pl.BlockSpec((1, 1, 1, d), wmap),
        ],
        out_specs=pl.BlockSpec((tm, d // 2), lambda i, be, br, nb: (i, 0)),
        scratch_shapes=[pltpu.VMEM((d, d_exp), BF16), pltpu.VMEM((d, d_exp), BF16), pltpu.VMEM((d_exp, d), BF16)],
    )
    return pl.pallas_call(
        functools.partial(_expert_kernel, chunk=256),
        out_shape=jax.ShapeDtypeStruct((n_slots, d // 2), jnp.int32),
        grid_spec=grid_spec,
        compiler_params=_params(("arbitrary",)),
        name="experts",
    )(block_e, block_rows, n_used, xb, wg, bg.reshape(depth, n_e, 1, d_exp), wu, bu.reshape(depth, n_e, 1, d_exp),
      wd, bd.reshape(depth, n_e, 1, d))


def _combine_kernel(x_ref, g2_ref, gates_ref, y_ref, o_ref):
    gates = gates_ref[0]
    half = y_ref.shape[-1]
    acc_lo = jnp.zeros((x_ref.shape[1], half), F32)
    acc_hi = jnp.zeros((x_ref.shape[1], half), F32)
    for k in range(TOP_K):
        lo, hi = _unpack_rows(y_ref[k])
        acc_lo = acc_lo + gates[:, k:k + 1] * lo
        acc_hi = acc_hi + gates[:, k:k + 1] * hi
    o_ref[0] = x_ref[0] + g2_ref[0] * jnp.concatenate([acc_lo, acc_hi], axis=1)


def _combine(x, g2, gates, y_sel, tok_offset):
    b, t, d = x.shape
    tm = int(np.gcd(min(t, 512), tok_offset)) if tok_offset else min(t, 512)
    n_t = t // tm
    off = tok_offset // tm
    return pl.pallas_call(
        _combine_kernel,
        out_shape=jax.ShapeDtypeStruct((b, t, d), F32),
        grid=(b, n_t),
        in_specs=[
            pl.BlockSpec((1, tm, d), lambda bi, i: (bi, i, 0)),
            pl.BlockSpec((1, 1, d), lambda bi, i: (bi, 0, 0)),
            pl.BlockSpec((1, tm, ROUTER_PAD), lambda bi, i: (bi, i, 0)),
            pl.BlockSpec((TOP_K, tm, d // 2), lambda bi, i: (0, off + bi * n_t + i, 0)),
        ],
        out_specs=pl.BlockSpec((1, tm, d), lambda bi, i: (bi, i, 0)),
        compiler_params=_params(("parallel", "parallel")),
        name="moe_combine",
    )(x, g2, gates, y_sel)


def _row_gather(table, idx):
    info = plsc.get_sparse_core_info()
    n_cores, n_workers = info.num_cores, info.num_cores * info.num_subcores
    n_rows, width = idx.shape[0], table.shape[1]
    per_worker = n_rows // n_workers
    n_chunks = per_worker // SC_GATHER_ROWS
    assert per_worker * n_workers == n_rows and n_chunks * SC_GATHER_ROWS == per_worker
    mesh = plsc.VectorSubcoreMesh(core_axis_name="c", subcore_axis_name="s")

    @functools.partial(
        pl.kernel, mesh=mesh,
        out_type=jax.ShapeDtypeStruct((n_rows, width), table.dtype),
        scratch_types=[pltpu.VMEM((SC_GATHER_ROWS,), jnp.int32), pltpu.VMEM((SC_GATHER_ROWS, width), table.dtype),
                       pltpu.SemaphoreType.DMA],
    )
    def gather_kernel(table_hbm, idx_hbm, out_hbm, idx_v, rows_v, sem):
        base = (lax.axis_index("s") * n_cores + lax.axis_index("c")) * per_worker

        @pl.loop(0, n_chunks)
        def _(c):
            off = base + c * SC_GATHER_ROWS
            pltpu.sync_copy(idx_hbm.at[pl.ds(off, SC_GATHER_ROWS)], idx_v)
            pltpu.async_copy(table_hbm.at[idx_v], rows_v, sem).wait()
            pltpu.sync_copy(rows_v, out_hbm.at[pl.ds(off, SC_GATHER_ROWS)])

    return gather_kernel(table, idx)


def _row_scatter(rows, dest_chunks, n_out):
    info = plsc.get_sparse_core_info()
    n_cores, n_workers = info.num_cores, info.num_cores * info.num_subcores
    n_chunks, n_choices, chunk_rows = dest_chunks.shape
    width = rows.shape[1]
    per_worker = n_chunks // n_workers
    assert per_worker * n_workers == n_chunks and n_chunks * chunk_rows == rows.shape[0]
    mesh = plsc.VectorSubcoreMesh(core_axis_name="c", subcore_axis_name="s")

    @functools.partial(
        pl.kernel, mesh=mesh,
        out_type=jax.ShapeDtypeStruct((n_out, width), rows.dtype),
        scratch_types=[pltpu.VMEM((n_choices, chunk_rows), jnp.int32), pltpu.VMEM((chunk_rows, width), rows.dtype)],
    )
    def scatter_kernel(rows_hbm, idx_hbm, out_hbm, idx_v, rows_v):
        base = (lax.axis_index("s") * n_cores + lax.axis_index("c")) * per_worker

        @pl.loop(0, per_worker)
        def _(c):
            chunk = base + c
            pltpu.sync_copy(idx_hbm.at[chunk], idx_v)
            pltpu.sync_copy(rows_hbm.at[pl.ds(chunk * chunk_rows, chunk_rows)], rows_v)
            for k in range(n_choices):
                pltpu.sync_copy(rows_v, out_hbm.at[idx_v.at[k]])

    return scatter_kernel(rows, dest_chunks)


def _moe(h2, ids, layer, w_exp):
    n = h2.shape[0]
    nk = n * TOP_K
    tm = EXPERT_TILE
    e_flat = ids.reshape(-1)
    onehot = (e_flat[:, None] == jnp.arange(N_EXPERTS, dtype=jnp.int32)[None]).astype(jnp.int32)
    csum = jnp.cumsum(onehot, axis=0)
    counts = csum[-1]
    padded = (counts + tm - 1) // tm * tm
    pends = jnp.cumsum(padded)
    dest = jnp.sum(onehot * (csum - 1 + (pends - padded)[None]), axis=1)
    n_blocks = -(-nk // tm) + N_EXPERTS
    n_slots = n_blocks * tm
    block_start = jnp.arange(n_blocks, dtype=jnp.int32) * tm
    block_e = jnp.minimum(jnp.sum((pends[None] <= block_start[:, None]).astype(jnp.int32), axis=1), N_EXPERTS - 1)
    is_e = (block_e[:, None] == jnp.arange(N_EXPERTS, dtype=jnp.int32)[None]).astype(jnp.int32)
    block_rows = jnp.clip(jnp.sum(is_e * (pends - padded + counts)[None], axis=1) - block_start, 0, tm)
    n_used = (pends[-1] // tm).astype(jnp.int32).reshape(1)
    dest_chunks = dest.reshape(n // SC_GATHER_ROWS, SC_GATHER_ROWS, TOP_K).transpose(0, 2, 1)
    xb = _row_scatter(h2, dest_chunks, n_slots)
    yb = _experts(xb, block_e, block_rows, n_used, layer, *w_exp)
    return _row_gather(yb, dest.reshape(n, TOP_K).T.reshape(-1)).reshape(TOP_K, n, -1)


def _mod_parts(mod_l, b):
    d = mod_l.shape[-1] // 6
    lat = [mod_l[:b, k * d:(k + 1) * d].reshape(b, 1, d) for k in range(6)]
    ctx = [jnp.broadcast_to(mod_l[b, k * d:(k + 1) * d].reshape(1, 1, d), (b, 1, d)) for k in range(6)]
    return lat, ctx


def kernel(x, c, ctx, c_ctx, ada_w, ada_b, norm1_g, norm2_g, ev_w_in, ev_w_out, ev_q_gain, ev_k_gain, ev_rpb, ev_conv_w, ev_conv_b, od_w_in, od_w_out, od_conv_w, od_conv_b, od_fwd_wa, od_fwd_ba, od_fwd_wx, od_fwd_bx, od_fwd_lam, od_bwd_wa, od_bwd_ba, od_bwd_wx, od_bwd_bx, od_bwd_lam, router_w, router_b, exp_w_gate, exp_b_gate, exp_w_up, exp_b_up, exp_w_down, exp_b_down):
    b, t, d = x.shape
    l = ctx.shape[1]
    assert ada_w.shape[0] == DEPTH == 2 and t % GRID_W == 0 and t // GRID_W >= WIN_H

    n_rows_c = -(-(b + 1) // SUBLANES) * SUBLANES
    cvec = jnp.zeros((n_rows_c, d), F32).at[:b].set(c).at[b].set(c_ctx)
    mod = _ada_mod(cvec, ada_w, ada_b)

    def router(layer):
        w_r = jnp.zeros((d, ROUTER_PAD), F32).at[:, :N_EXPERTS].set(router_w[layer]).astype(BF16)
        b_r = jnp.zeros((1, ROUTER_PAD), F32).at[0, :N_EXPERTS].set(router_b[layer])
        return w_r, b_r

    w_exp = (exp_w_gate, exp_b_gate, exp_w_up, exp_b_up, exp_w_down, exp_b_down)

    (sh1, sc1, g1, sh2, sc2, g2), (csh1, csc1, cg1, csh2, csc2, cg2) = _mod_parts(mod[0], b)
    n1 = norm1_g[0].reshape(1, d)
    n2 = norm2_g[0].reshape(1, d)
    w_in = ev_w_in[0].astype(BF16)
    w_out = ev_w_out[0].astype(BF16)
    q_scale = NA_HEAD_DIM ** -0.5
    head_gain = jnp.stack([jnp.tile(ev_q_gain[0] * q_scale, NA_HEADS), jnp.tile(ev_k_gain[0], NA_HEADS)])
    head_gain = head_gain.reshape(2, 1, NA_WIDTH).astype(F32)
    proj = _inproj(x, n1, sh1, sc1, w_in, head_gain)
    proj_c = _inproj(ctx, n1, csh1, csc1, w_in, head_gain)
    kh, r0, row_type, patterns = _na_tables(t // GRID_W)
    bias_tab = _na_bias_table(ev_rpb[0], patterns)
    o_a = _neighbourhood_attention(proj, proj_c, bias_tab, r0, row_type, kh)
    oc_a = _context_attention(proj_c)
    w_r, b_r = router(0)
    x1, h2, ids, gates = _even_out(o_a, proj, ev_conv_w[0], ev_conv_b[0], w_out, x, g1, n2, sh2, sc2, w_r, b_r)
    c1, h2c, ids_c, gates_c = _even_out(oc_a, proj_c, ev_conv_w[0], ev_conv_b[0], w_out, ctx, cg1, n2, csh2, csc2,
                                        w_r, b_r)
    tokens = jnp.concatenate([h2c.reshape(b * l, d // 2), h2.reshape(b * t, d // 2)], axis=0)
    ids_all = jnp.concatenate([ids_c.reshape(b * l, ROUTER_PAD), ids.reshape(b * t, ROUTER_PAD)], axis=0)[:, :TOP_K]
    y_sel = _moe(tokens, ids_all, 0, w_exp)
    hctx = _combine(c1, cg2, gates_c, y_sel, 0)
    x = _combine(x1, g2, gates, y_sel, b * l)

    (sh1, sc1, g1, sh2, sc2, g2), (csh1, csc1, _, _, _, _) = _mod_parts(mod[1], b)
    n1 = norm1_g[1].reshape(1, d)
    n2 = norm2_g[1].reshape(1, d)
    w_in = od_w_in[0].astype(BF16)
    width = w_in.shape[1] // 2
    proj = _inproj(x, n1, sh1, sc1, w_in)
    u_ctx = _inproj(hctx, n1, csh1, csc1, w_in[:, width:])
    h_dir = []
    for reverse, (wa, ba, wx, bx, lam) in ((False, (od_fwd_wa, od_fwd_ba, od_fwd_wx, od_fwd_bx, od_fwd_lam)),
                                           (True, (od_bwd_wa, od_bwd_ba, od_bwd_wx, od_bwd_bx, od_bwd_lam))):
        w_cat = jnp.concatenate([wa[0], wx[0]], axis=-1).astype(BF16)
        h_dir.append(_lru_scan(proj, u_ctx, od_conv_w[0], od_conv_b[0], w_cat, ba[0], bx[0], lam[0], reverse))
    w_r, b_r = router(1)
    x1, h2, ids, gates = _odd_out(h_dir[0], h_dir[1], proj, od_w_out[0].astype(BF16), x, g1, n2, sh2, sc2, w_r, b_r)
    y_sel = _moe(h2.reshape(b * t, d // 2), ids.reshape(b * t, ROUTER_PAD)[:, :TOP_K], 1, w_exp)
    return _combine(x1, g2, gates, y_sel, 0)
```

```python
import functools

import numpy as np
import jax
import jax.numpy as jnp
from jax import lax
from jax.experimental import pallas as pl
from jax.experimental.pallas import tpu as pltpu
from jax.experimental.pallas import tpu_sc as plsc

DEPTH = 2
GRID_W = 64
EPS = 1e-6
NEG_INF = -1e30
NA_HEADS = 8
NA_HEAD_DIM = 64
NA_WIDTH = NA_HEADS * NA_HEAD_DIM
HEAD_PAIRS = NA_HEADS // 2
WIN_H = 8
WIN_W = 16
SC_CONV = 3
LRU_BLOCKS = 4
LRU_CONV = 4
LRU_C = 8.0
N_EXPERTS = 32
TOP_K = 4
SWIGLU_LIMIT = 7.0
SWIGLU_ALPHA = 1.702

LANES = 128
SUBLANES = 8
HALO = 16
ROUTER_PAD = LANES
EXPERT_TILE = 512
SC_GATHER_ROWS = 64
VMEM_LIMIT = 56 * 1024 * 1024

F32 = jnp.float32
BF16 = jnp.bfloat16


def _params(sem, vmem=VMEM_LIMIT):
    return pltpu.CompilerParams(dimension_semantics=sem, vmem_limit_bytes=vmem)


def _dot(a, b):
    return jnp.dot(a, b, preferred_element_type=F32)


def _dot_nt(a, b):
    return lax.dot_general(a, b, (((1,), (1,)), ((), ())), preferred_element_type=F32)


def _pack_rows(v):
    w = v.shape[-1] // 2
    lo = lax.bitcast_convert_type(v[:, :w].astype(BF16).astype(F32), jnp.int32)
    hi = lax.bitcast_convert_type(v[:, w:].astype(BF16).astype(F32), jnp.int32)
    return lax.shift_right_logical(lo, 16) | (hi & jnp.int32(-65536))


def _unpack_rows(p):
    lo = lax.bitcast_convert_type(lax.shift_left(p, 16), F32)
    hi = lax.bitcast_convert_type(p & jnp.int32(-65536), F32)
    return lo, hi


def _rms_mod(x, g, shift, scale):
    ms = jnp.mean(x * x, axis=-1, keepdims=True)
    y = x * lax.rsqrt(ms + EPS) * g
    return y * (1.0 + scale) + shift


def _ada_kernel(c_ref, w_ref, b_ref, o_ref):
    c = c_ref[...]
    s = (c * jax.nn.sigmoid(c)).astype(BF16)
    o_ref[0] = _dot(s, w_ref[0].astype(BF16)) + b_ref[0]


def _ada_mod(cvec, ada_w, ada_b):
    depth, d, n = ada_w.shape
    r = cvec.shape[0]
    tn = 1536
    return pl.pallas_call(
        _ada_kernel,
        out_shape=jax.ShapeDtypeStruct((depth, r, n), F32),
        grid=(depth, n // tn),
        in_specs=[
            pl.BlockSpec((r, d), lambda l, j: (0, 0)),
            pl.BlockSpec((1, d, tn), lambda l, j: (l, 0, j)),
            pl.BlockSpec((1, 1, tn), lambda l, j: (l, 0, j)),
        ],
        out_specs=pl.BlockSpec((1, r, tn), lambda l, j: (l, 0, j)),
        compiler_params=_params(("parallel", "parallel")),
        name="ada_mod",
    )(cvec, ada_w, ada_b.reshape(depth, 1, n))


def _inproj_kernel(x_ref, g_ref, sh_ref, sc_ref, w_ref, hg_ref, ones_ref, o_ref, *, n_tiles, tn, n_headnorm):
    h = _rms_mod(x_ref[0], g_ref[...], sh_ref[0], sc_ref[0]).astype(BF16)
    for j in range(n_tiles):
        y = _dot(h, w_ref[:, j * tn:(j + 1) * tn])
        if j < n_headnorm:
            ms = _dot((y * y).astype(BF16), ones_ref[...]) * (1.0 / NA_HEAD_DIM)
            y = y * lax.rsqrt(ms + EPS) * hg_ref[j]
        o_ref[0, :, j * tn:(j + 1) * tn] = y.astype(o_ref.dtype)


def _inproj(x, g, shift, scale, w, head_gain=None):
    b, t, d = x.shape
    n = w.shape[1]
    tn = NA_WIDTH
    tm = min(t, 512)
    n_headnorm = 0 if head_gain is None else head_gain.shape[0]
    if head_gain is None:
        head_gain = jnp.ones((1, 1, tn), F32)
    hid = np.arange(tn) // NA_HEAD_DIM
    ones_bd = jnp.asarray((hid[:, None] == hid[None, :]), BF16)
    kern = functools.partial(_inproj_kernel, n_tiles=n // tn, tn=tn, n_headnorm=n_headnorm)
    return pl.pallas_call(
        kern,
        out_shape=jax.ShapeDtypeStruct((b, t, n), BF16),
        grid=(b, t // tm),
        in_specs=[
            pl.BlockSpec((1, tm, d), lambda bi, i: (bi, i, 0)),
            pl.BlockSpec((1, d), lambda bi, i: (0, 0)),
            pl.BlockSpec((1, 1, d), lambda bi, i: (bi, 0, 0)),
            pl.BlockSpec((1, 1, d), lambda bi, i: (bi, 0, 0)),
            pl.BlockSpec((d, n), lambda bi, i: (0, 0)),
            pl.BlockSpec(head_gain.shape, lambda bi, i: (0, 0, 0)),
            pl.BlockSpec((tn, tn), lambda bi, i: (0, 0)),
        ],
        out_specs=pl.BlockSpec((1, tm, n), lambda bi, i: (bi, i, 0)),
        compiler_params=_params(("parallel", "parallel")),
        name="inproj",
    )(x, g, shift, scale, w, head_gain, ones_bd)


def _na_tables(rows):
    kh = min(WIN_H, rows)
    r = np.arange(rows)
    r0 = np.clip(r - kh // 2, 0, rows - kh)
    dr = r0[:, None] + np.arange(kh)[None] - r[:, None] + WIN_H - 1
    patterns, row_type = np.unique(dr, axis=0, return_inverse=True)
    return kh, r0.astype(np.int32), row_type.reshape(-1).astype(np.int32), patterns


def _na_bias_table(rpb, patterns):
    qc = np.arange(GRID_W)
    kc = np.arange(GRID_W)
    c0 = np.clip(qc - WIN_W // 2, 0, GRID_W - WIN_W)[:, None]
    valid = (kc[None] >= c0) & (kc[None] < c0 + WIN_W)
    dc = np.clip(kc[None] - qc[:, None] + WIN_W - 1, 0, 2 * WIN_W - 2)
    n_pat, kh = patterns.shape
    onehot_dc = jnp.asarray(dc[None] == np.arange(2 * WIN_W - 1)[:, None, None], F32)
    tab = jnp.einsum('hpic,cqk->hpiqk', rpb.astype(F32)[:, patterns], onehot_dc,
                     precision=lax.Precision.HIGHEST)
    tab = jnp.where(valid[None, None, None], tab, NEG_INF)
    tab = tab.reshape(HEAD_PAIRS, 2, n_pat, kh, GRID_W, GRID_W)
    tab = tab.transpose(2, 0, 1, 4, 3, 5)
    return tab.reshape(n_pat, HEAD_PAIRS, 2 * GRID_W, kh * GRID_W)


def _pair_attention(q, keys, values, biases):
    m = q.shape[0]
    lane = lax.broadcasted_iota(jnp.int32, q.shape, 1)
    zero = jnp.zeros_like(q)
    qq = jnp.concatenate([jnp.where(lane < NA_HEAD_DIM, q, zero), jnp.where(lane >= NA_HEAD_DIM, q, zero)], axis=0)
    scores = []
    for k, bias in zip(keys, biases):
        s = _dot_nt(qq, k)
        scores.append(s if bias is None else s + bias)
    mx = functools.reduce(jnp.maximum, [jnp.max(s, axis=-1, keepdims=True) for s in scores])
    exps = [jnp.exp(s - mx) for s in scores]
    denom = functools.reduce(jnp.add, [jnp.sum(e, axis=-1, keepdims=True) for e in exps])
    o = functools.reduce(jnp.add, [_dot(e.astype(BF16), v) for e, v in zip(exps, values)])
    o = o * (1.0 / denom)
    lane_o = lax.broadcasted_iota(jnp.int32, (m, LANES), 1)
    return jnp.where(lane_o < NA_HEAD_DIM, o[:m], o[m:])


def _na_kernel(r0_ref, type_ref, q_ref, k_ref, v_ref, kc_ref, vc_ref, bias_ref, o_ref, *, kh):
    r = pl.program_id(1)
    start = pl.multiple_of(r0_ref[r] * GRID_W, GRID_W)
    rtype = type_ref[r]
    for p in range(HEAD_PAIRS):
        cols = slice(p * LANES, (p + 1) * LANES)
        o = _pair_attention(
            q_ref[0, :, cols],
            [k_ref[0, pl.ds(start, kh * GRID_W), cols], kc_ref[0, :, cols]],
            [v_ref[0, pl.ds(start, kh * GRID_W), cols], vc_ref[0, :, cols]],
            [bias_ref[rtype, p], None],
        )
        o_ref[0, :, cols] = o.astype(o_ref.dtype)


def _neighbourhood_attention(proj, proj_c, bias_tab, r0, row_type, kh):
    b, t, _ = proj.shape
    l = proj_c.shape[1]
    rows = t // GRID_W
    w = NA_WIDTH
    grid_spec = pltpu.PrefetchScalarGridSpec(
        num_scalar_prefetch=2,
        grid=(b, rows),
        in_specs=[
            pl.BlockSpec((1, GRID_W, w), lambda bi, r, *_: (bi, r, 0)),
            pl.BlockSpec((1, t, w), lambda bi, r, *_: (bi, 0, 1)),
            pl.BlockSpec((1, t, w), lambda bi, r, *_: (bi, 0, 2)),
            pl.BlockSpec((1, l, w), lambda bi, r, *_: (bi, 0, 1)),
            pl.BlockSpec((1, l, w), lambda bi, r, *_: (bi, 0, 2)),
            pl.BlockSpec(bias_tab.shape, lambda bi, r, *_: (0, 0, 0, 0)),
        ],
        out_specs=pl.BlockSpec((1, GRID_W, w), lambda bi, r, *_: (bi, r, 0)),
    )
    return pl.pallas_call(
        functools.partial(_na_kernel, kh=kh),
        out_shape=jax.ShapeDtypeStruct((b, t, w), BF16),
        grid_spec=grid_spec,
        compiler_params=_params(("parallel", "arbitrary")),
        name="na_attention",
    )(jnp.asarray(r0), jnp.asarray(row_type), proj, proj, proj, proj_c, proj_c, bias_tab)


def _ctx_attn_kernel(q_ref, k_ref, v_ref, o_ref):
    for p in range(HEAD_PAIRS):
        cols = slice(p * LANES, (p + 1) * LANES)
        o = _pair_attention(q_ref[0, :, cols], [k_ref[0, :, cols]], [v_ref[0, :, cols]], [None])
        o_ref[0, :, cols] = o.astype(o_ref.dtype)


def _context_attention(proj_c):
    b, l, _ = proj_c.shape
    w = NA_WIDTH
    return pl.pallas_call(
        _ctx_attn_kernel,
        out_shape=jax.ShapeDtypeStruct((b, l, w), BF16),
        grid=(b,),
        in_specs=[pl.BlockSpec((1, l, w), lambda bi, j=j: (bi, 0, j)) for j in range(3)],
        out_specs=pl.BlockSpec((1, l, w), lambda bi: (bi, 0, 0)),
        compiler_params=_params(("parallel",)),
        name="ctx_attention",
    )(proj_c, proj_c, proj_c)


def _top4(logits):
    lane = lax.broadcasted_iota(jnp.int32, logits.shape, 1)
    cur = jnp.where(lane < N_EXPERTS, logits, -jnp.inf)
    vals, idxs = [], []
    for _ in range(TOP_K):
        m = jnp.max(cur, axis=-1, keepdims=True)
        first = jnp.min(jnp.where(cur == m, lane, ROUTER_PAD).astype(F32), axis=-1, keepdims=True)
        idx = first.astype(jnp.int32)
        vals.append(m)
        idxs.append(idx)
        cur = jnp.where(lane == idx, -jnp.inf, cur)
    exps = [jnp.exp(v - vals[0]) for v in vals]
    inv = 1.0 / functools.reduce(jnp.add, exps)
    ids = jnp.zeros(logits.shape, jnp.int32)
    gates = jnp.zeros(logits.shape, F32)
    for k in range(TOP_K):
        ids = jnp.where(lane == k, idxs[k], ids)
        gates = jnp.where(lane == k, exps[k] * inv, gates)
    return ids, gates


def _layer_tail(y, x_ref, g1_ref, n2_ref, sh2_ref, sc2_ref, wr_ref, br_ref, xo_ref, h2_ref, ids_ref, gates_ref):
    x_new = x_ref[0] + g1_ref[0] * y
    xo_ref[0] = x_new
    h2 = _rms_mod(x_new, n2_ref[...], sh2_ref[0], sc2_ref[0])
    h2_ref[0] = _pack_rows(h2)
    logits = _dot(h2.astype(BF16), wr_ref[...]) + br_ref[...]
    ids, gates = _top4(logits)
    ids_ref[0] = ids
    gates_ref[0] = gates


def _tail_specs(b, t, d, tm):
    row = lambda bi, i: (bi, i, 0)
    per_b = lambda bi, i: (bi, 0, 0)
    const = lambda bi, i: (0, 0)
    in_specs = [
        pl.BlockSpec((1, tm, d), row),
        pl.BlockSpec((1, 1, d), per_b),
        pl.BlockSpec((1, d), const),
        pl.BlockSpec((1, 1, d), per_b),
        pl.BlockSpec((1, 1, d), per_b),
        pl.BlockSpec((d, ROUTER_PAD), const),
        pl.BlockSpec((1, ROUTER_PAD), const),
    ]
    out_specs = [
        pl.BlockSpec((1, tm, d), row),
        pl.BlockSpec((1, tm, d // 2), row),
        pl.BlockSpec((1, tm, ROUTER_PAD), row),
        pl.BlockSpec((1, tm, ROUTER_PAD), row),
    ]
    out_shape = [
        jax.ShapeDtypeStruct((b, t, d), F32),
        jax.ShapeDtypeStruct((b, t, d // 2), jnp.int32),
        jax.ShapeDtypeStruct((b, t, ROUTER_PAD), jnp.int32),
        jax.ShapeDtypeStruct((b, t, ROUTER_PAD), F32),
    ]
    return in_specs, out_specs, out_shape


def _halo_fix(rolled, row, at_row, halo_row, present):
    fill = jnp.where(present, halo_row, jnp.zeros_like(halo_row))
    return jnp.where(row == at_row, fill, rolled)


def _even_out_kernel(oa_ref, bg_ref, cg_ref, xin_ref, cgp_ref, xinp_ref, cgn_ref, xinn_ref, cw_ref, cb_ref,
                     wa_ref, wb_ref, *tail_refs, tm):
    i = pl.program_id(1)
    has_prev = i > 0
    has_next = i < pl.num_programs(1) - 1
    u = cg_ref[0].astype(F32) * xin_ref[0].astype(F32)
    u_prev = (cgp_ref[0].astype(F32) * xinp_ref[0].astype(F32))[HALO - 1:HALO]
    u_next = (cgn_ref[0].astype(F32) * xinn_ref[0].astype(F32))[0:1]
    row = lax.broadcasted_iota(jnp.int32, u.shape, 0)
    u_m1 = _halo_fix(pltpu.roll(u, 1, 0), row, 0, u_prev, has_prev)
    u_p1 = _halo_fix(pltpu.roll(u, tm - 1, 0), row, tm - 1, u_next, has_next)
    cw = cw_ref[...]
    conv = u_m1 * cw[0:1] + u * cw[1:2] + u_p1 * cw[2:3] + cb_ref[...]
    o_b = bg_ref[0].astype(F32) * conv
    y = _dot(oa_ref[0], wa_ref[...]) + _dot(o_b.astype(BF16), wb_ref[...])
    _layer_tail(y, *tail_refs)


def _even_out(o_a, proj, conv_w, conv_b, w_out, x, g1, n2, sh2, sc2, w_r, b_r):
    b, t, d = x.shape
    w = NA_WIDTH
    tm = min(t, 512)
    hb = tm // HALO
    n_hblocks = t // HALO
    row = lambda bi, i: (bi, i, 0)
    const = lambda bi, i: (0, 0)
    prev = lambda col: (lambda bi, i: (bi, jnp.maximum(i * hb - 1, 0), col))
    nxt = lambda col: (lambda bi, i: (bi, jnp.minimum((i + 1) * hb, n_hblocks - 1), col))
    tail_in, out_specs, out_shape = _tail_specs(b, t, d, tm)
    in_specs = [
        pl.BlockSpec((1, tm, w), row),
        pl.BlockSpec((1, tm, w), lambda bi, i: (bi, i, 3)),
        pl.BlockSpec((1, tm, w), lambda bi, i: (bi, i, 4)),
        pl.BlockSpec((1, tm, w), lambda bi, i: (bi, i, 5)),
        pl.BlockSpec((1, HALO, w), prev(4)),
        pl.BlockSpec((1, HALO, w), prev(5)),
        pl.BlockSpec((1, HALO, w), nxt(4)),
        pl.BlockSpec((1, HALO, w), nxt(5)),
        pl.BlockSpec((SC_CONV, w), const),
        pl.BlockSpec((1, w), const),
        pl.BlockSpec((w, d), const),
        pl.BlockSpec((w, d), const),
    ] + tail_in
    return pl.pallas_call(
        functools.partial(_even_out_kernel, tm=tm),
        out_shape=out_shape,
        grid=(b, t // tm),
        in_specs=in_specs,
        out_specs=out_specs,
        compiler_params=_params(("parallel", "parallel")),
        name="even_out",
    )(o_a, proj, proj, proj, proj, proj, proj, proj, conv_w, conv_b.reshape(1, w),
      w_out[:w], w_out[w:], x, g1, n2, sh2, sc2, w_r, b_r)


def _log_sigmoid(x):
    return jnp.minimum(x, 0.0) - jnp.log1p(jnp.exp(-jnp.abs(x)))


def _lru_tile(u_ref, up_ref, un_ref, has_prev, has_next, cw_ref, cb_ref, w_ref, ba_ref, bx_ref, lam_ref,
              carry_ref, o_ref, *, ts, reverse):
    width = u_ref.shape[-1]
    blk = width // LRU_BLOCKS
    u = u_ref[0].astype(F32)
    up = up_ref[0].astype(F32)
    un = un_ref[0].astype(F32)
    row = lax.broadcasted_iota(jnp.int32, u.shape, 0)
    u_m1 = _halo_fix(pltpu.roll(u, 1, 0), row, 0, up[HALO - 1:HALO], has_prev)
    u_p1 = _halo_fix(pltpu.roll(u, ts - 1, 0), row, ts - 1, un[0:1], has_next)
    u_p2 = _halo_fix(pltpu.roll(u, ts - 2, 0), row, ts - 2, un[0:1], has_next)
    u_p2 = _halo_fix(u_p2, row, ts - 1, un[1:2], has_next)
    cw = cw_ref[...]
    uc = u_m1 * cw[0:1] + u * cw[1:2] + u_p1 * cw[2:3] + u_p2 * cw[3:4] + cb_ref[...]
    ucb = uc.astype(BF16)
    za, zx = [], []
    for h in range(LRU_BLOCKS):
        z = _dot(ucb[:, h * blk:(h + 1) * blk], w_ref[h])
        za.append(z[:, :blk])
        zx.append(z[:, blk:])
    r = jax.nn.sigmoid(jnp.concatenate(za, axis=1) + ba_ref[...])
    gate_i = jax.nn.sigmoid(jnp.concatenate(zx, axis=1) + bx_ref[...])
    log_a = (LRU_C * _log_sigmoid(lam_ref[...])) * r
    a = jnp.exp(log_a)
    th = jnp.tanh(log_a)
    m2 = -2.0 * th / (1.0 - th)
    mult = jnp.where(m2 > 0.0, m2 * lax.rsqrt(m2), 0.0)
    bcoef = mult * gate_i * uc

    groups = ts // SUBLANES
    a3 = a.reshape(groups, SUBLANES, width)
    b3 = bcoef.reshape(groups, SUBLANES, width)
    sub = lax.broadcasted_iota(jnp.int32, a3.shape, 1)
    for s in (1, 2, 4):
        shift = SUBLANES - s if reverse else s
        a_sh = pltpu.roll(a3, shift, 1)
        b_sh = pltpu.roll(b3, shift, 1)
        m = (sub < SUBLANES - s) if reverse else (sub >= s)
        b3 = jnp.where(m, a3 * b_sh + b3, b3)
        a3 = jnp.where(m, a3 * a_sh, a3)

    h = carry_ref[0:1, :]
    order = range(groups - 1, -1, -1) if reverse else range(groups)
    edge = 0 if reverse else SUBLANES - 1
    for g in order:
        hg = a3[g] * h + b3[g]
        if o_ref is not None:
            o_ref[0, g * SUBLANES:(g + 1) * SUBLANES, :] = hg.astype(o_ref.dtype)
        h = hg[edge:edge + 1, :]
    carry_ref[...] = jnp.broadcast_to(h, carry_ref.shape)


def _lru_kernel(uc_ref, ucp_ref, ucn_ref, ul_ref, ulp_ref, uln_ref, cw_ref, cb_ref, w_ref, ba_ref, bx_ref,
                lam_ref, o_ref, carry_ref, *, ts, n_ctx_tiles, n_lat_tiles, reverse):
    j = pl.program_id(1)
    shared = (cw_ref, cb_ref, w_ref, ba_ref, bx_ref, lam_ref, carry_ref)

    @pl.when(j == 0)
    def _():
        carry_ref[...] = jnp.zeros_like(carry_ref)

    def tile_pos(step, n):
        return (n - 1 - step) if reverse else step

    @pl.when(j < n_ctx_tiles)
    def _():
        pos = tile_pos(j, n_ctx_tiles)
        _lru_tile(uc_ref, ucp_ref, ucn_ref, pos > 0, pos < n_ctx_tiles - 1, *shared, None, ts=ts, reverse=reverse)

    @pl.when(j >= n_ctx_tiles)
    def _():
        pos = tile_pos(j - n_ctx_tiles, n_lat_tiles)
        _lru_tile(ul_ref, ulp_ref, uln_ref, pos > 0, pos < n_lat_tiles - 1, *shared, o_ref, ts=ts, reverse=reverse)


def _lru_scan(proj, u_ctx, conv_w, conv_b, w_cat, ba, bx, lam, reverse):
    b, t, _ = proj.shape
    l, width = u_ctx.shape[1], u_ctx.shape[2]
    ts = min(256, l, t)
    n_c, n_l = l // ts, t // ts
    hb = ts // HALO

    def pos_of(step, n):
        step = jnp.clip(step, 0, n - 1)
        return (n - 1 - step) if reverse else step

    def tile_map(off, n, col):
        return lambda bi, j: (bi, pos_of(j - off, n), col)

    def prev_map(off, n, col):
        return lambda bi, j: (bi, jnp.maximum(pos_of(j - off, n) * hb - 1, 0), col)

    def next_map(off, n, col):
        return lambda bi, j: (bi, jnp.minimum((pos_of(j - off, n) + 1) * hb, n * hb - 1), col)

    const2 = lambda bi, j: (0, 0)
    in_specs = [
        pl.BlockSpec((1, ts, width), tile_map(0, n_c, 0)),
        pl.BlockSpec((1, HALO, width), prev_map(0, n_c, 0)),
        pl.BlockSpec((1, HALO, width), next_map(0, n_c, 0)),
        pl.BlockSpec((1, ts, width), tile_map(n_c, n_l, 1)),
        pl.BlockSpec((1, HALO, width), prev_map(n_c, n_l, 1)),
        pl.BlockSpec((1, HALO, width), next_map(n_c, n_l, 1)),
        pl.BlockSpec((LRU_CONV, width), const2),
        pl.BlockSpec((1, width), const2),
        pl.BlockSpec(w_cat.shape, lambda bi, j: (0, 0, 0)),
        pl.BlockSpec((1, width), const2),
        pl.BlockSpec((1, width), const2),
        pl.BlockSpec((1, width), const2),
    ]
    kern = functools.partial(_lru_kernel, ts=ts, n_ctx_tiles=n_c, n_lat_tiles=n_l, reverse=reverse)
    return pl.pallas_call(
        kern,
        out_shape=jax.ShapeDtypeStruct((b, t, width), BF16),
        grid=(b, n_c + n_l),
        in_specs=in_specs,
        out_specs=pl.BlockSpec((1, ts, width), tile_map(n_c, n_l, 0)),
        scratch_shapes=[pltpu.VMEM((SUBLANES, width), F32)],
        compiler_params=_params(("parallel", "arbitrary")),
        name="lru_scan_bwd" if reverse else "lru_scan_fwd",
    )(u_ctx, u_ctx, u_ctx, proj, proj, proj, conv_w, conv_b.reshape(1, width), w_cat,
      ba.reshape(1, width), bx.reshape(1, width), lam.reshape(1, width))


def _odd_out_kernel(hf_ref, hb_ref, gate_ref, w_ref, *tail_refs):
    hsum = hf_ref[0].astype(F32) + hb_ref[0].astype(F32)
    z = hsum * jax.nn.gelu(gate_ref[0].astype(F32), approximate=True)
    y = _dot(z.astype(BF16), w_ref[...])
    _layer_tail(y, *tail_refs)


def _odd_out(h_f, h_b, proj, w_out, x, g1, n2, sh2, sc2, w_r, b_r):
    b, t, d = x.shape
    width = h_f.shape[-1]
    tm = min(t, 512)
    row = lambda bi, i: (bi, i, 0)
    tail_in, out_specs, out_shape = _tail_specs(b, t, d, tm)
    in_specs = [
        pl.BlockSpec((1, tm, width), row),
        pl.BlockSpec((1, tm, width), row),
        pl.BlockSpec((1, tm, width), row),
        pl.BlockSpec((width, d), lambda bi, i: (0, 0)),
    ] + tail_in
    return pl.pallas_call(
        _odd_out_kernel,
        out_shape=out_shape,
        grid=(b, t // tm),
        in_specs=in_specs,
        out_specs=out_specs,
        compiler_params=_params(("parallel", "parallel")),
        name="odd_out",
    )(h_f, h_b, proj, w_out, x, g1, n2, sh2, sc2, w_r, b_r)


def _expert_kernel(be_ref, rows_ref, nb_ref, x_ref, wg_ref, bg_ref, wu_ref, bu_ref, wd_ref, bd_ref, o_ref,
                   wg_bf, wu_bf, wd_bf, *, chunk):
    i = pl.program_id(0)

    @pl.when((i == 0) | (be_ref[i] != be_ref[jnp.maximum(i - 1, 0)]))
    def _():
        wg_bf[...] = wg_ref[0, 0].astype(BF16)
        wu_bf[...] = wu_ref[0, 0].astype(BF16)
        wd_bf[...] = wd_ref[0, 0].astype(BF16)

    @pl.when(i < nb_ref[0])
    def _():
        xp = x_ref[...]
        row = lax.broadcasted_iota(jnp.int32, xp.shape, 0)
        xp = jnp.where(row < rows_ref[i], xp, 0)
        x = jnp.concatenate(_unpack_rows(xp), axis=1).astype(BF16)
        d_exp = wg_bf.shape[1]
        acc = jnp.zeros((x.shape[0], wd_bf.shape[1]), F32)
        for c in range(d_exp // chunk):
            cs = slice(c * chunk, (c + 1) * chunk)
            g = jnp.minimum(_dot(x, wg_bf[:, cs]) + bg_ref[0, 0, :, cs], SWIGLU_LIMIT)
            u = jnp.clip(_dot(x, wu_bf[:, cs]) + bu_ref[0, 0, :, cs], -SWIGLU_LIMIT, SWIGLU_LIMIT)
            h = g * jax.nn.sigmoid(SWIGLU_ALPHA * g) * (u + 1.0)
            acc = acc + _dot(h.astype(BF16), wd_bf[cs, :])
        o_ref[...] = _pack_rows(acc + bd_ref[0, 0])

    @pl.when(i >= nb_ref[0])
    def _():
        o_ref[...] = jnp.zeros_like(o_ref)


def _experts(xb, block_e, block_rows, n_used, layer, wg, bg, wu, bu, wd, bd):
    n_slots = xb.shape[0]
    depth, n_e, d, d_exp = wg.shape
    tm = EXPERT_TILE
    n_blocks = n_slots // tm
    xmap = lambda i, be, br, nb: (jnp.minimum(i, nb[0] - 1), 0)
    wmap = lambda i, be, br, nb: (layer, be[i], 0, 0)
    grid_spec = pltpu.PrefetchScalarGridSpec(
        num_scalar_prefetch=3,
        grid=(n_blocks,),
        in_specs=[
            pl.BlockSpec((tm, d // 2), xmap),
            pl.BlockSpec((1, 1, d, d_exp), wmap),
            pl.BlockSpec((1, 1, 1, d_exp), wmap),
            pl.BlockSpec((1, 1, d, d_exp), wmap),
            pl.BlockSpec((1, 1, 1, d_exp), wmap),
            pl.BlockSpec((1, 1, d_exp, d), wmap),
            pl.BlockSpec((1, 1, 1, d), wmap),
        ],
        out_specs=pl.BlockSpec((tm, d // 2), lambda i, be, br, nb: (i, 0)),
        scratch_shapes=[pltpu.VMEM((d, d_exp), BF16), pltpu.VMEM((d, d_exp), BF16), pltpu.VMEM((d_exp, d), BF16)],
    )
    return pl.pallas_call(
        functools.partial(_expert_kernel, chunk=256),
        out_shape=jax.ShapeDtypeStruct((n_slots, d // 2), jnp.int32),
        grid_spec=grid_spec,
        compiler_params=_params(("arbitrary",)),
        name="experts",
    )(block_e, block_rows, n_used, xb, wg, bg.reshape(depth, n_e, 1, d_exp), wu, bu.reshape(depth, n_e, 1, d_exp),
      wd, bd.reshape(depth, n_e, 1, d))


def _combine_kernel(x_ref, g2_ref, gates_ref, y_ref, o_ref):
    gates = gates_ref[0]
    half = y_ref.shape[-1]
    acc_lo = jnp.zeros((x_ref.shape[1], half), F32)
    acc_hi = jnp.zeros((x_ref.shape[1], half), F32)
    for k in range(TOP_K):
        lo, hi = _unpack_rows(y_ref[k])
        acc_lo = acc_lo + gates[:, k:k + 1] * lo
        acc_hi = acc_hi + gates[:, k:k + 1] * hi
    o_ref[0] = x_ref[0] + g2_ref[0] * jnp.concatenate([acc_lo, acc_hi], axis=1)


def _combine(x, g2, gates, y_sel, tok_offset):
    b, t, d = x.shape
    tm = int(np.gcd(min(t, 512), tok_offset)) if tok_offset else min(t, 512)
    n_t = t // tm
    off = tok_offset // tm
    return pl.pallas_call(
        _combine_kernel,
        out_shape=jax.ShapeDtypeStruct((b, t, d), F32),
        grid=(b, n_t),
        in_specs=[
            pl.BlockSpec((1, tm, d), lambda bi, i: (bi, i, 0)),
            pl.BlockSpec((1, 1, d), lambda bi, i: (bi, 0, 0)),
            pl.BlockSpec((1, tm, ROUTER_PAD), lambda bi, i: (bi, i, 0)),
            pl.BlockSpec((TOP_K, tm, d // 2), lambda bi, i: (0, off + bi * n_t + i, 0)),
        ],
        out_specs=pl.BlockSpec((1, tm, d), lambda bi, i: (bi, i, 0)),
        compiler_params=_params(("parallel", "parallel")),
        name="moe_combine",
    )(x, g2, gates, y_sel)


def _route_kernel(ids_ref, upper_ref, dest_ref, counts_ref, run_ref, *, tile):
    p = pl.program_id(0)
    i = pl.program_id(1)
    tm = ids_ref.shape[0]
    ids_t = ids_ref[...].T
    expert = lax.broadcasted_iota(jnp.int32, (N_EXPERTS, tm), 0)
    chosen = [ids_t[k:k + 1, :] == expert for k in range(TOP_K)]
    picks = functools.reduce(jnp.add, [c.astype(F32) for c in chosen])
    tile_counts = jnp.sum(picks, axis=1, keepdims=True)

    @pl.when((p == 0) & (i == 0))
    def _():
        run_ref[...] = jnp.zeros_like(run_ref)

    @pl.when(p == 0)
    def _():
        run_ref[...] += tile_counts

    @pl.when((p == 1) & (i == 0))
    def _():
        counts = run_ref[...]
        counts_ref[...] = counts.astype(jnp.int32)
        padded = jnp.floor((counts + (tile - 1)) * (1.0 / tile)) * tile
        row = lax.broadcasted_iota(jnp.int32, counts.shape, 0)
        ends = padded
        for s in (1, 2, 4, 8, 16):
            ends = ends + jnp.where(row >= s, pltpu.roll(ends, s, 0), 0.0)
        run_ref[...] = ends - padded

    @pl.when(p == 1)
    def _():
        before = _dot(picks.astype(BF16), upper_ref[...])
        slot = before + run_ref[:, 0:1]
        rows = [jnp.sum(jnp.where(c, slot, 0.0), axis=0, keepdims=True) for c in chosen]
        rows += [jnp.zeros_like(rows[0])] * (dest_ref.shape[0] - TOP_K)
        dest_ref[...] = jnp.concatenate(rows, axis=0).astype(jnp.int32)
        run_ref[...] += tile_counts


def _route(ids):
    n = ids.shape[0]
    tm = int(np.gcd(n, 1024))
    n_tiles = n // tm
    upper = jnp.asarray(np.triu(np.ones((tm, tm), np.float32), 1), BF16)
    dest, counts = pl.pallas_call(
        functools.partial(_route_kernel, tile=EXPERT_TILE),
        out_shape=[jax.ShapeDtypeStruct((SUBLANES, n), jnp.int32),
                   jax.ShapeDtypeStruct((N_EXPERTS, LANES), jnp.int32)],
        grid=(2, n_tiles),
        in_specs=[pl.BlockSpec((tm, ROUTER_PAD), lambda p, i: (i, 0)),
                  pl.BlockSpec((tm, tm), lambda p, i: (0, 0))],
        out_specs=[pl.BlockSpec((SUBLANES, tm), lambda p, i: (0, i * p)),
                   pl.BlockSpec((N_EXPERTS, LANES), lambda p, i: (0, 0))],
        scratch_shapes=[pltpu.VMEM((N_EXPERTS, LANES), F32)],
        compiler_params=_params(("arbitrary", "arbitrary")),
        name="route",
    )(ids, upper)
    return dest, counts[:, 0]


def _row_gather(table, idx):
    info = plsc.get_sparse_core_info()
    n_cores, n_workers = info.num_cores, info.num_cores * info.num_subcores
    n_rows, width = idx.shape[0], table.shape[1]
    per_worker = n_rows // n_workers
    n_chunks = per_worker // SC_GATHER_ROWS
    assert per_worker * n_workers == n_rows and n_chunks * SC_GATHER_ROWS == per_worker
    mesh = plsc.VectorSubcoreMesh(core_axis_name="c", subcore_axis_name="s")

    @functools.partial(
        pl.kernel, mesh=mesh,
        out_type=jax.ShapeDtypeStruct((n_rows, width), table.dtype),
        scratch_types=[pltpu.VMEM((SC_GATHER_ROWS,), jnp.int32), pltpu.VMEM((SC_GATHER_ROWS, width), table.dtype),
                       pltpu.SemaphoreType.DMA],
    )
    def gather_kernel(table_hbm, idx_hbm, out_hbm, idx_v, rows_v, sem):
        base = (lax.axis_index("s") * n_cores + lax.axis_index("c")) * per_worker

        @pl.loop(0, n_chunks)
        def _(c):
            off = base + c * SC_GATHER_ROWS
            pltpu.sync_copy(idx_hbm.at[pl.ds(off, SC_GATHER_ROWS)], idx_v)
            pltpu.async_copy(table_hbm.at[idx_v], rows_v, sem).wait()
            pltpu.sync_copy(rows_v, out_hbm.at[pl.ds(off, SC_GATHER_ROWS)])

    return gather_kernel(table, idx)


def _row_scatter(rows, dest_chunks, n_out):
    info = plsc.get_sparse_core_info()
    n_cores, n_workers = info.num_cores, info.num_cores * info.num_subcores
    n_choices, n_chunks, chunk_rows = dest_chunks.shape
    width = rows.shape[1]
    per_worker = n_chunks // n_workers
    assert per_worker * n_workers == n_chunks and n_chunks * chunk_rows == rows.shape[0]
    mesh = plsc.VectorSubcoreMesh(core_axis_name="c", subcore_axis_name="s")

    @functools.partial(
        pl.kernel, mesh=mesh,
        out_type=jax.ShapeDtypeStruct((n_out, width), rows.dtype),
        scratch_types=[pltpu.VMEM((n_choices, chunk_rows), jnp.int32), pltpu.VMEM((chunk_rows, width), rows.dtype)],
    )
    def scatter_kernel(rows_hbm, idx_hbm, out_hbm, idx_v, rows_v):
        base = (lax.axis_index("s") * n_cores + lax.axis_index("c")) * per_worker

        @pl.loop(0, per_worker)
        def _(c):
            chunk = base + c
            for k in range(n_choices):
                pltpu.sync_copy(idx_hbm.at[k, chunk], idx_v.at[k])
            pltpu.sync_copy(rows_hbm.at[pl.ds(chunk * chunk_rows, chunk_rows)], rows_v)
            for k in range(n_choices):
                pltpu.sync_copy(rows_v, out_hbm.at[idx_v.at[k]])

    return scatter_kernel(rows, dest_chunks)


def _moe(h2, ids, layer, w_exp):
    n = h2.shape[0]
    nk = n * TOP_K
    tm = EXPERT_TILE
    dest, counts = _route(ids)
    dest = dest[:TOP_K]
    padded = (counts + tm - 1) // tm * tm
    pends = jnp.cumsum(padded)
    n_blocks = -(-nk // tm) + N_EXPERTS
    n_slots = n_blocks * tm
    block_start = jnp.arange(n_blocks, dtype=jnp.int32) * tm
    block_e = jnp.minimum(jnp.sum((pends[None] <= block_start[:, None]).astype(jnp.int32), axis=1), N_EXPERTS - 1)
    is_e = (block_e[:, None] == jnp.arange(N_EXPERTS, dtype=jnp.int32)[None]).astype(jnp.int32)
    block_rows = jnp.clip(jnp.sum(is_e * (pends - padded + counts)[None], axis=1) - block_start, 0, tm)
    n_used = (pends[-1] // tm).astype(jnp.int32).reshape(1)
    xb = _row_scatter(h2, dest.reshape(TOP_K, n // SC_GATHER_ROWS, SC_GATHER_ROWS), n_slots)
    yb = _experts(xb, block_e, block_rows, n_used, layer, *w_exp)
    return _row_gather(yb, dest.reshape(-1)).reshape(TOP_K, n, -1)


def _mod_parts(mod_l, b):
    d = mod_l.shape[-1] // 6
    lat = [mod_l[:b, k * d:(k + 1) * d].reshape(b, 1, d) for k in range(6)]
    ctx = [jnp.broadcast_to(mod_l[b, k * d:(k + 1) * d].reshape(1, 1, d), (b, 1, d)) for k in range(6)]
    return lat, ctx


def kernel(x, c, ctx, c_ctx, ada_w, ada_b, norm1_g, norm2_g, ev_w_in, ev_w_out, ev_q_gain, ev_k_gain, ev_rpb, ev_conv_w, ev_conv_b, od_w_in, od_w_out, od_conv_w, od_conv_b, od_fwd_wa, od_fwd_ba, od_fwd_wx, od_fwd_bx, od_fwd_lam, od_bwd_wa, od_bwd_ba, od_bwd_wx, od_bwd_bx, od_bwd_lam, router_w, router_b, exp_w_gate, exp_b_gate, exp_w_up, exp_b_up, exp_w_down, exp_b_down):
    b, t, d = x.shape
    l = ctx.shape[1]
    assert ada_w.shape[0] == DEPTH == 2 and t % GRID_W == 0 and t // GRID_W >= WIN_H

    n_rows_c = -(-(b + 1) // SUBLANES) * SUBLANES
    cvec = jnp.zeros((n_rows_c, d), F32).at[:b].set(c).at[b].set(c_ctx)
    mod = _ada_mod(cvec, ada_w, ada_b)

    def router(layer):
        w_r = jnp.zeros((d, ROUTER_PAD), F32).at[:, :N_EXPERTS].set(router_w[layer]).astype(BF16)
        b_r = jnp.zeros((1, ROUTER_PAD), F32).at[0, :N_EXPERTS].set(router_b[layer])
        return w_r, b_r

    w_exp = (exp_w_gate, exp_b_gate, exp_w_up, exp_b_up, exp_w_down, exp_b_down)

    (sh1, sc1, g1, sh2, sc2, g2), (csh1, csc1, cg1, csh2, csc2, cg2) = _mod_parts(mod[0], b)
    n1 = norm1_g[0].reshape(1, d)
    n2 = norm2_g[0].reshape(1, d)
    w_in = ev_w_in[0].astype(BF16)
    w_out = ev_w_out[0].astype(BF16)
    q_scale = NA_HEAD_DIM ** -0.5
    head_gain = jnp.stack([jnp.tile(ev_q_gain[0] * q_scale, NA_HEADS), jnp.tile(ev_k_gain[0], NA_HEADS)])
    head_gain = head_gain.reshape(2, 1, NA_WIDTH).astype(F32)
    proj = _inproj(x, n1, sh1, sc1, w_in, head_gain)
    proj_c = _inproj(ctx, n1, csh1, csc1, w_in, head_gain)
    kh, r0, row_type, patterns = _na_tables(t // GRID_W)
    bias_tab = _na_bias_table(ev_rpb[0], patterns)
    o_a = _neighbourhood_attention(proj, proj_c, bias_tab, r0, row_type, kh)
    oc_a = _context_attention(proj_c)
    w_r, b_r = router(0)
    x1, h2, ids, gates = _even_out(o_a, proj, ev_conv_w[0], ev_conv_b[0], w_out, x, g1, n2, sh2, sc2, w_r, b_r)
    c1, h2c, ids_c, gates_c = _even_out(oc_a, proj_c, ev_conv_w[0], ev_conv_b[0], w_out, ctx, cg1, n2, csh2, csc2,
                                        w_r, b_r)
    tokens = jnp.concatenate([h2c.reshape(b * l, d // 2), h2.reshape(b * t, d // 2)], axis=0)
    ids_all = jnp.concatenate([ids_c.reshape(b * l, ROUTER_PAD), ids.reshape(b * t, ROUTER_PAD)], axis=0)
    y_sel = _moe(tokens, ids_all, 0, w_exp)
    hctx = _combine(c1, cg2, gates_c, y_sel, 0)
    x = _combine(x1, g2, gates, y_sel, b * l)

    (sh1, sc1, g1, sh2, sc2, g2), (csh1, csc1, _, _, _, _) = _mod_parts(mod[1], b)
    n1 = norm1_g[1].reshape(1, d)
    n2 = norm2_g[1].reshape(1, d)
    w_in = od_w_in[0].astype(BF16)
    width = w_in.shape[1] // 2
    proj = _inproj(x, n1, sh1, sc1, w_in)
    u_ctx = _inproj(hctx, n1, csh1, csc1, w_in[:, width:])
    h_dir = []
    for reverse, (wa, ba, wx, bx, lam) in ((False, (od_fwd_wa, od_fwd_ba, od_fwd_wx, od_fwd_bx, od_fwd_lam)),
                                           (True, (od_bwd_wa, od_bwd_ba, od_bwd_wx, od_bwd_bx, od_bwd_lam))):
        w_cat = jnp.concatenate([wa[0], wx[0]], axis=-1).astype(BF16)
        h_dir.append(_lru_scan(proj, u_ctx, od_conv_w[0], od_conv_b[0], w_cat, ba[0], bx[0], lam[0], reverse))
    w_r, b_r = router(1)
    x1, h2, ids, gates = _odd_out(h_dir[0], h_dir[1], proj, od_w_out[0].astype(BF16), x, g1, n2, sh2, sc2, w_r, b_r)
    y_sel = _moe(h2.reshape(b * t, d // 2), ids.reshape(b * t, ROUTER_PAD), 1, w_exp)
    return _combine(x1, g2, gates, y_sel, 0)
```

```python
import functools

import numpy as np
import jax
import jax.numpy as jnp
from jax import lax
from jax.experimental import pallas as pl
from jax.experimental.pallas import tpu as pltpu
from jax.experimental.pallas import tpu_sc as plsc

DEPTH = 2
GRID_W = 64
EPS = 1e-6
NEG_INF = -1e30
LOG2_E = 1.4426950408889634
NA_HEADS = 8
NA_HEAD_DIM = 64
NA_WIDTH = NA_HEADS * NA_HEAD_DIM
HEAD_PAIRS = NA_HEADS // 2
WIN_H = 8
WIN_W = 16
SC_CONV = 3
LRU_BLOCKS = 4
LRU_CONV = 4
LRU_C = 8.0
N_EXPERTS = 32
TOP_K = 4
SWIGLU_LIMIT = 7.0
SWIGLU_ALPHA = 1.702

LANES = 128
SUBLANES = 8
HALO = 16
ROUTER_PAD = LANES
EXPERT_TILE = 512
SC_GATHER_ROWS = 64
VMEM_LIMIT = 56 * 1024 * 1024

F32 = jnp.float32
BF16 = jnp.bfloat16


def _params(sem, vmem=VMEM_LIMIT):
    return pltpu.CompilerParams(dimension_semantics=sem, vmem_limit_bytes=vmem)


def _dot(a, b):
    return jnp.dot(a, b, preferred_element_type=F32)


def _dot_nt(a, b):
    return lax.dot_general(a, b, (((1,), (1,)), ((), ())), preferred_element_type=F32)


def _pack_rows(v):
    w = v.shape[-1] // 2
    lo = lax.bitcast_convert_type(v[:, :w].astype(BF16).astype(F32), jnp.int32)
    hi = lax.bitcast_convert_type(v[:, w:].astype(BF16).astype(F32), jnp.int32)
    return lax.shift_right_logical(lo, 16) | (hi & jnp.int32(-65536))


def _unpack_rows(p):
    lo = lax.bitcast_convert_type(lax.shift_left(p, 16), F32)
    hi = lax.bitcast_convert_type(p & jnp.int32(-65536), F32)
    return lo, hi


def _rms_mod(x, g, shift, scale):
    ms = jnp.mean(x * x, axis=-1, keepdims=True)
    y = x * lax.rsqrt(ms + EPS) * g
    return y * (1.0 + scale) + shift


def _ada_kernel(c_ref, w_ref, b_ref, o_ref):
    c = c_ref[...]
    s = (c * jax.nn.sigmoid(c)).astype(BF16)
    o_ref[0] = _dot(s, w_ref[0].astype(BF16)) + b_ref[0]


def _ada_mod(cvec, ada_w, ada_b):
    depth, d, n = ada_w.shape
    r = cvec.shape[0]
    tn = 1536
    return pl.pallas_call(
        _ada_kernel,
        out_shape=jax.ShapeDtypeStruct((depth, r, n), F32),
        grid=(depth, n // tn),
        in_specs=[
            pl.BlockSpec((r, d), lambda l, j: (0, 0)),
            pl.BlockSpec((1, d, tn), lambda l, j: (l, 0, j)),
            pl.BlockSpec((1, 1, tn), lambda l, j: (l, 0, j)),
        ],
        out_specs=pl.BlockSpec((1, r, tn), lambda l, j: (l, 0, j)),
        compiler_params=_params(("parallel", "parallel")),
        name="ada_mod",
    )(cvec, ada_w, ada_b.reshape(depth, 1, n))


def _inproj_kernel(x_ref, g_ref, sh_ref, sc_ref, w_ref, hg_ref, ones_ref, o_ref, *, n_tiles, tn, n_headnorm):
    h = _rms_mod(x_ref[0], g_ref[...], sh_ref[0], sc_ref[0]).astype(BF16)
    for j in range(n_tiles):
        y = _dot(h, w_ref[:, j * tn:(j + 1) * tn])
        if j < n_headnorm:
            ms = _dot((y * y).astype(BF16), ones_ref[...]) * (1.0 / NA_HEAD_DIM)
            y = y * lax.rsqrt(ms + EPS) * hg_ref[j]
        o_ref[0, :, j * tn:(j + 1) * tn] = y.astype(o_ref.dtype)


def _inproj(x, g, shift, scale, w, head_gain=None):
    b, t, d = x.shape
    n = w.shape[1]
    tn = NA_WIDTH
    tm = min(t, 512)
    n_headnorm = 0 if head_gain is None else head_gain.shape[0]
    if head_gain is None:
        head_gain = jnp.ones((1, 1, tn), F32)
    hid = np.arange(tn) // NA_HEAD_DIM
    ones_bd = jnp.asarray((hid[:, None] == hid[None, :]), BF16)
    kern = functools.partial(_inproj_kernel, n_tiles=n // tn, tn=tn, n_headnorm=n_headnorm)
    return pl.pallas_call(
        kern,
        out_shape=jax.ShapeDtypeStruct((b, t, n), BF16),
        grid=(b, t // tm),
        in_specs=[
            pl.BlockSpec((1, tm, d), lambda bi, i: (bi, i, 0)),
            pl.BlockSpec((1, d), lambda bi, i: (0, 0)),
            pl.BlockSpec((1, 1, d), lambda bi, i: (bi, 0, 0)),
            pl.BlockSpec((1, 1, d), lambda bi, i: (bi, 0, 0)),
            pl.BlockSpec((d, n), lambda bi, i: (0, 0)),
            pl.BlockSpec(head_gain.shape, lambda bi, i: (0, 0, 0)),
            pl.BlockSpec((tn, tn), lambda bi, i: (0, 0)),
        ],
        out_specs=pl.BlockSpec((1, tm, n), lambda bi, i: (bi, i, 0)),
        compiler_params=_params(("parallel", "parallel")),
        name="inproj",
    )(x, g, shift, scale, w, head_gain, ones_bd)


def _na_tables(rows):
    kh = min(WIN_H, rows)
    r = np.arange(rows)
    r0 = np.clip(r - kh // 2, 0, rows - kh)
    dr = r0[:, None] + np.arange(kh)[None] - r[:, None] + WIN_H - 1
    patterns, row_type = np.unique(dr, axis=0, return_inverse=True)
    return kh, r0.astype(np.int32), row_type.reshape(-1).astype(np.int32), patterns


def _na_bias_table(rpb, patterns):
    qc = np.arange(GRID_W)
    kc = np.arange(GRID_W)
    c0 = np.clip(qc - WIN_W // 2, 0, GRID_W - WIN_W)[:, None]
    valid = (kc[None] >= c0) & (kc[None] < c0 + WIN_W)
    dc = np.clip(kc[None] - qc[:, None] + WIN_W - 1, 0, 2 * WIN_W - 2)
    n_pat, kh = patterns.shape
    onehot_dc = jnp.asarray(dc[None] == np.arange(2 * WIN_W - 1)[:, None, None], F32)
    tab = jnp.einsum('hpic,cqk->hpiqk', rpb.astype(F32)[:, patterns], onehot_dc,
                     precision=lax.Precision.HIGHEST)
    tab = jnp.where(valid[None, None, None], tab * LOG2_E, NEG_INF)
    tab = tab.reshape(HEAD_PAIRS, 2, n_pat, kh, GRID_W, GRID_W)
    tab = tab.transpose(2, 0, 1, 4, 3, 5)
    return tab.reshape(n_pat, HEAD_PAIRS, 2 * GRID_W, kh * GRID_W)


def _pair_attention(q, keys, values, biases):
    m = q.shape[0]
    lane = lax.broadcasted_iota(jnp.int32, q.shape, 1)
    zero = jnp.zeros_like(q)
    qq = jnp.concatenate([jnp.where(lane < NA_HEAD_DIM, q, zero), jnp.where(lane >= NA_HEAD_DIM, q, zero)], axis=0)
    scores = []
    for k, bias in zip(keys, biases):
        s = _dot_nt(qq, k)
        scores.append(s if bias is None else s + bias)
    s = jnp.concatenate(scores, axis=1)
    e = jnp.exp2(s - jnp.max(s, axis=-1, keepdims=True))
    denom = jnp.sum(e, axis=-1, keepdims=True)
    e = e.astype(BF16)
    o, start = 0.0, 0
    for v in values:
        o = o + _dot(e[:, start:start + v.shape[0]], v)
        start += v.shape[0]
    o = o * (1.0 / denom)
    lane_o = lax.broadcasted_iota(jnp.int32, (m, LANES), 1)
    return jnp.where(lane_o < NA_HEAD_DIM, o[:m], o[m:])


def _na_kernel(r0_ref, type_ref, q_ref, k_ref, v_ref, kc_ref, vc_ref, bias_ref, o_ref, *, kh):
    r = pl.program_id(1)
    start = pl.multiple_of(r0_ref[r] * GRID_W, GRID_W)
    rtype = type_ref[r]
    for p in range(HEAD_PAIRS):
        cols = slice(p * LANES, (p + 1) * LANES)
        o = _pair_attention(
            q_ref[0, :, cols],
            [k_ref[0, pl.ds(start, kh * GRID_W), cols], kc_ref[0, :, cols]],
            [v_ref[0, pl.ds(start, kh * GRID_W), cols], vc_ref[0, :, cols]],
            [bias_ref[rtype, p], None],
        )
        o_ref[0, :, cols] = o.astype(o_ref.dtype)


def _neighbourhood_attention(proj, proj_c, bias_tab, r0, row_type, kh):
    b, t, _ = proj.shape
    l = proj_c.shape[1]
    rows = t // GRID_W
    w = NA_WIDTH
    grid_spec = pltpu.PrefetchScalarGridSpec(
        num_scalar_prefetch=2,
        grid=(b, rows),
        in_specs=[
            pl.BlockSpec((1, GRID_W, w), lambda bi, r, *_: (bi, r, 0)),
            pl.BlockSpec((1, t, w), lambda bi, r, *_: (bi, 0, 1)),
            pl.BlockSpec((1, t, w), lambda bi, r, *_: (bi, 0, 2)),
            pl.BlockSpec((1, l, w), lambda bi, r, *_: (bi, 0, 1)),
            pl.BlockSpec((1, l, w), lambda bi, r, *_: (bi, 0, 2)),
            pl.BlockSpec(bias_tab.shape, lambda bi, r, *_: (0, 0, 0, 0)),
        ],
        out_specs=pl.BlockSpec((1, GRID_W, w), lambda bi, r, *_: (bi, r, 0)),
    )
    return pl.pallas_call(
        functools.partial(_na_kernel, kh=kh),
        out_shape=jax.ShapeDtypeStruct((b, t, w), BF16),
        grid_spec=grid_spec,
        compiler_params=_params(("parallel", "arbitrary")),
        name="na_attention",
    )(jnp.asarray(r0), jnp.asarray(row_type), proj, proj, proj, proj_c, proj_c, bias_tab)


def _ctx_attn_kernel(q_ref, k_ref, v_ref, o_ref):
    for p in range(HEAD_PAIRS):
        cols = slice(p * LANES, (p + 1) * LANES)
        o = _pair_attention(q_ref[0, :, cols], [k_ref[0, :, cols]], [v_ref[0, :, cols]], [None])
        o_ref[0, :, cols] = o.astype(o_ref.dtype)


def _context_attention(proj_c):
    b, l, _ = proj_c.shape
    w = NA_WIDTH
    return pl.pallas_call(
        _ctx_attn_kernel,
        out_shape=jax.ShapeDtypeStruct((b, l, w), BF16),
        grid=(b,),
        in_specs=[pl.BlockSpec((1, l, w), lambda bi, j=j: (bi, 0, j)) for j in range(3)],
        out_specs=pl.BlockSpec((1, l, w), lambda bi: (bi, 0, 0)),
        compiler_params=_params(("parallel",)),
        name="ctx_attention",
    )(proj_c, proj_c, proj_c)


def _top4(logits):
    lane = lax.broadcasted_iota(jnp.int32, logits.shape, 1)
    cur = jnp.where(lane < N_EXPERTS, logits, -jnp.inf)
    vals, idxs = [], []
    for _ in range(TOP_K):
        m = jnp.max(cur, axis=-1, keepdims=True)
        first = jnp.min(jnp.where(cur == m, lane, ROUTER_PAD).astype(F32), axis=-1, keepdims=True)
        idx = first.astype(jnp.int32)
        vals.append(m)
        idxs.append(idx)
        cur = jnp.where(lane == idx, -jnp.inf, cur)
    exps = [jnp.exp(v - vals[0]) for v in vals]
    inv = 1.0 / functools.reduce(jnp.add, exps)
    ids = jnp.zeros(logits.shape, jnp.int32)
    gates = jnp.zeros(logits.shape, F32)
    for k in range(TOP_K):
        ids = jnp.where(lane == k, idxs[k], ids)
        gates = jnp.where(lane == k, exps[k] * inv, gates)
    return ids, gates


def _layer_tail(y, x_ref, g1_ref, n2_ref, sh2_ref, sc2_ref, wr_ref, br_ref, xo_ref, h2_ref, ids_ref, gates_ref):
    x_new = x_ref[0] + g1_ref[0] * y
    xo_ref[0] = x_new
    h2 = _rms_mod(x_new, n2_ref[...], sh2_ref[0], sc2_ref[0])
    h2_ref[0] = _pack_rows(h2)
    logits = _dot(h2.astype(BF16), wr_ref[...]) + br_ref[...]
    ids, gates = _top4(logits)
    ids_ref[0] = ids
    gates_ref[0] = gates


def _tail_specs(b, t, d, tm):
    row = lambda bi, i: (bi, i, 0)
    per_b = lambda bi, i: (bi, 0, 0)
    const = lambda bi, i: (0, 0)
    in_specs = [
        pl.BlockSpec((1, tm, d), row),
        pl.BlockSpec((1, 1, d), per_b),
        pl.BlockSpec((1, d), const),
        pl.BlockSpec((1, 1, d), per_b),
        pl.BlockSpec((1, 1, d), per_b),
        pl.BlockSpec((d, ROUTER_PAD), const),
        pl.BlockSpec((1, ROUTER_PAD), const),
    ]
    out_specs = [
        pl.BlockSpec((1, tm, d), row),
        pl.BlockSpec((1, tm, d // 2), row),
        pl.BlockSpec((1, tm, ROUTER_PAD), row),
        pl.BlockSpec((1, tm, ROUTER_PAD), row),
    ]
    out_shape = [
        jax.ShapeDtypeStruct((b, t, d), F32),
        jax.ShapeDtypeStruct((b, t, d // 2), jnp.int32),
        jax.ShapeDtypeStruct((b, t, ROUTER_PAD), jnp.int32),
        jax.ShapeDtypeStruct((b, t, ROUTER_PAD), F32),
    ]
    return in_specs, out_specs, out_shape


def _halo_fix(rolled, row, at_row, halo_row, present):
    fill = jnp.where(present, halo_row, jnp.zeros_like(halo_row))
    return jnp.where(row == at_row, fill, rolled)


def _even_out_kernel(oa_ref, bg_ref, cg_ref, xin_ref, cgp_ref, xinp_ref, cgn_ref, xinn_ref, cw_ref, cb_ref,
                     wa_ref, wb_ref, *tail_refs, tm):
    i = pl.program_id(1)
    has_prev = i > 0
    has_next = i < pl.num_programs(1) - 1
    u = cg_ref[0].astype(F32) * xin_ref[0].astype(F32)
    u_prev = (cgp_ref[0].astype(F32) * xinp_ref[0].astype(F32))[HALO - 1:HALO]
    u_next = (cgn_ref[0].astype(F32) * xinn_ref[0].astype(F32))[0:1]
    row = lax.broadcasted_iota(jnp.int32, u.shape, 0)
    u_m1 = _halo_fix(pltpu.roll(u, 1, 0), row, 0, u_prev, has_prev)
    u_p1 = _halo_fix(pltpu.roll(u, tm - 1, 0), row, tm - 1, u_next, has_next)
    cw = cw_ref[...]
    conv = u_m1 * cw[0:1] + u * cw[1:2] + u_p1 * cw[2:3] + cb_ref[...]
    o_b = bg_ref[0].astype(F32) * conv
    y = _dot(oa_ref[0], wa_ref[...]) + _dot(o_b.astype(BF16), wb_ref[...])
    _layer_tail(y, *tail_refs)


def _even_out(o_a, proj, conv_w, conv_b, w_out, x, g1, n2, sh2, sc2, w_r, b_r):
    b, t, d = x.shape
    w = NA_WIDTH
    tm = min(t, 512)
    hb = tm // HALO
    n_hblocks = t // HALO
    row = lambda bi, i: (bi, i, 0)
    const = lambda bi, i: (0, 0)
    prev = lambda col: (lambda bi, i: (bi, jnp.maximum(i * hb - 1, 0), col))
    nxt = lambda col: (lambda bi, i: (bi, jnp.minimum((i + 1) * hb, n_hblocks - 1), col))
    tail_in, out_specs, out_shape = _tail_specs(b, t, d, tm)
    in_specs = [
        pl.BlockSpec((1, tm, w), row),
        pl.BlockSpec((1, tm, w), lambda bi, i: (bi, i, 3)),
        pl.BlockSpec((1, tm, w), lambda bi, i: (bi, i, 4)),
        pl.BlockSpec((1, tm, w), lambda bi, i: (bi, i, 5)),
        pl.BlockSpec((1, HALO, w), prev(4)),
        pl.BlockSpec((1, HALO, w), prev(5)),
        pl.BlockSpec((1, HALO, w), nxt(4)),
        pl.BlockSpec((1, HALO, w), nxt(5)),
        pl.BlockSpec((SC_CONV, w), const),
        pl.BlockSpec((1, w), const),
        pl.BlockSpec((w, d), const),
        pl.BlockSpec((w, d), const),
    ] + tail_in
    return pl.pallas_call(
        functools.partial(_even_out_kernel, tm=tm),
        out_shape=out_shape,
        grid=(b, t // tm),
        in_specs=in_specs,
        out_specs=out_specs,
        compiler_params=_params(("parallel", "parallel")),
        name="even_out",
    )(o_a, proj, proj, proj, proj, proj, proj, proj, conv_w, conv_b.reshape(1, w),
      w_out[:w], w_out[w:], x, g1, n2, sh2, sc2, w_r, b_r)


def _log_sigmoid(x):
    return jnp.minimum(x, 0.0) - jnp.log1p(jnp.exp(-jnp.abs(x)))


def _lru_tile(u_ref, up_ref, un_ref, has_prev, has_next, cw_ref, cb_ref, w_ref, ba_ref, bx_ref, lam_ref,
              carry_ref, o_ref, *, ts, reverse):
    width = u_ref.shape[-1]
    blk = width // LRU_BLOCKS
    u = u_ref[0].astype(F32)
    up = up_ref[0].astype(F32)
    un = un_ref[0].astype(F32)
    row = lax.broadcasted_iota(jnp.int32, u.shape, 0)
    u_m1 = _halo_fix(pltpu.roll(u, 1, 0), row, 0, up[HALO - 1:HALO], has_prev)
    u_p1 = _halo_fix(pltpu.roll(u, ts - 1, 0), row, ts - 1, un[0:1], has_next)
    u_p2 = _halo_fix(pltpu.roll(u, ts - 2, 0), row, ts - 2, un[0:1], has_next)
    u_p2 = _halo_fix(u_p2, row, ts - 1, un[1:2], has_next)
    cw = cw_ref[...]
    uc = u_m1 * cw[0:1] + u * cw[1:2] + u_p1 * cw[2:3] + u_p2 * cw[3:4] + cb_ref[...]
    ucb = uc.astype(BF16)
    za, zx = [], []
    for h in range(LRU_BLOCKS):
        z = _dot(ucb[:, h * blk:(h + 1) * blk], w_ref[h])
        za.append(z[:, :blk])
        zx.append(z[:, blk:])
    r = jax.nn.sigmoid(jnp.concatenate(za, axis=1) + ba_ref[...])
    gate_i = jax.nn.sigmoid(jnp.concatenate(zx, axis=1) + bx_ref[...])
    log_a = (LRU_C * _log_sigmoid(lam_ref[...])) * r
    a = jnp.exp(log_a)
    th = jnp.tanh(log_a)
    m2 = -2.0 * th / (1.0 - th)
    mult = jnp.where(m2 > 0.0, m2 * lax.rsqrt(m2), 0.0)
    bcoef = mult * gate_i * uc

    groups = ts // SUBLANES
    a3 = a.reshape(groups, SUBLANES, width)
    b3 = bcoef.reshape(groups, SUBLANES, width)
    sub = lax.broadcasted_iota(jnp.int32, a3.shape, 1)
    for s in (1, 2, 4):
        shift = SUBLANES - s if reverse else s
        a_sh = pltpu.roll(a3, shift, 1)
        b_sh = pltpu.roll(b3, shift, 1)
        m = (sub < SUBLANES - s) if reverse else (sub >= s)
        b3 = jnp.where(m, a3 * b_sh + b3, b3)
        a3 = jnp.where(m, a3 * a_sh, a3)

    h = carry_ref[0:1, :]
    order = range(groups - 1, -1, -1) if reverse else range(groups)
    edge = 0 if reverse else SUBLANES - 1
    for g in order:
        hg = a3[g] * h + b3[g]
        if o_ref is not None:
            o_ref[0, g * SUBLANES:(g + 1) * SUBLANES, :] = hg.astype(o_ref.dtype)
        h = hg[edge:edge + 1, :]
    carry_ref[...] = jnp.broadcast_to(h, carry_ref.shape)


def _lru_kernel(uc_ref, ucp_ref, ucn_ref, ul_ref, ulp_ref, uln_ref, cw_ref, cb_ref, w_ref, ba_ref, bx_ref,
                lam_ref, o_ref, carry_ref, *, ts, n_ctx_tiles, n_lat_tiles, reverse):
    j = pl.program_id(1)
    shared = (cw_ref, cb_ref, w_ref, ba_ref, bx_ref, lam_ref, carry_ref)

    @pl.when(j == 0)
    def _():
        carry_ref[...] = jnp.zeros_like(carry_ref)

    def tile_pos(step, n):
        return (n - 1 - step) if reverse else step

    @pl.when(j < n_ctx_tiles)
    def _():
        pos = tile_pos(j, n_ctx_tiles)
        _lru_tile(uc_ref, ucp_ref, ucn_ref, pos > 0, pos < n_ctx_tiles - 1, *shared, None, ts=ts, reverse=reverse)

    @pl.when(j >= n_ctx_tiles)
    def _():
        pos = tile_pos(j - n_ctx_tiles, n_lat_tiles)
        _lru_tile(ul_ref, ulp_ref, uln_ref, pos > 0, pos < n_lat_tiles - 1, *shared, o_ref, ts=ts, reverse=reverse)


def _lru_scan(proj, u_ctx, conv_w, conv_b, w_cat, ba, bx, lam, reverse):
    b, t, _ = proj.shape
    l, width = u_ctx.shape[1], u_ctx.shape[2]
    ts = min(256, l, t)
    n_c, n_l = l // ts, t // ts
    hb = ts // HALO

    def pos_of(step, n):
        step = jnp.clip(step, 0, n - 1)
        return (n - 1 - step) if reverse else step

    def tile_map(off, n, col):
        return lambda bi, j: (bi, pos_of(j - off, n), col)

    def prev_map(off, n, col):
        return lambda bi, j: (bi, jnp.maximum(pos_of(j - off, n) * hb - 1, 0), col)

    def next_map(off, n, col):
        return lambda bi, j: (bi, jnp.minimum((pos_of(j - off, n) + 1) * hb, n * hb - 1), col)

    const2 = lambda bi, j: (0, 0)
    in_specs = [
        pl.BlockSpec((1, ts, width), tile_map(0, n_c, 0)),
        pl.BlockSpec((1, HALO, width), prev_map(0, n_c, 0)),
        pl.BlockSpec((1, HALO, width), next_map(0, n_c, 0)),
        pl.BlockSpec((1, ts, width), tile_map(n_c, n_l, 1)),
        pl.BlockSpec((1, HALO, width), prev_map(n_c, n_l, 1)),
        pl.BlockSpec((1, HALO, width), next_map(n_c, n_l, 1)),
        pl.BlockSpec((LRU_CONV, width), const2),
        pl.BlockSpec((1, width), const2),
        pl.BlockSpec(w_cat.shape, lambda bi, j: (0, 0, 0)),
        pl.BlockSpec((1, width), const2),
        pl.BlockSpec((1, width), const2),
        pl.BlockSpec((1, width), const2),
    ]
    kern = functools.partial(_lru_kernel, ts=ts, n_ctx_tiles=n_c, n_lat_tiles=n_l, reverse=reverse)
    return pl.pallas_call(
        kern,
        out_shape=jax.ShapeDtypeStruct((b, t, width), BF16),
        grid=(b, n_c + n_l),
        in_specs=in_specs,
        out_specs=pl.BlockSpec((1, ts, width), tile_map(n_c, n_l, 0)),
        scratch_shapes=[pltpu.VMEM((SUBLANES, width), F32)],
        compiler_params=_params(("parallel", "arbitrary")),
        name="lru_scan_bwd" if reverse else "lru_scan_fwd",
    )(u_ctx, u_ctx, u_ctx, proj, proj, proj, conv_w, conv_b.reshape(1, width), w_cat,
      ba.reshape(1, width), bx.reshape(1, width), lam.reshape(1, width))


def _odd_out_kernel(hf_ref, hb_ref, gate_ref, w_ref, *tail_refs):
    hsum = hf_ref[0].astype(F32) + hb_ref[0].astype(F32)
    z = hsum * jax.nn.gelu(gate_ref[0].astype(F32), approximate=True)
    y = _dot(z.astype(BF16), w_ref[...])
    _layer_tail(y, *tail_refs)


def _odd_out(h_f, h_b, proj, w_out, x, g1, n2, sh2, sc2, w_r, b_r):
    b, t, d = x.shape
    width = h_f.shape[-1]
    tm = min(t, 512)
    row = lambda bi, i: (bi, i, 0)
    tail_in, out_specs, out_shape = _tail_specs(b, t, d, tm)
    in_specs = [
        pl.BlockSpec((1, tm, width), row),
        pl.BlockSpec((1, tm, width), row),
        pl.BlockSpec((1, tm, width), row),
        pl.BlockSpec((width, d), lambda bi, i: (0, 0)),
    ] + tail_in
    return pl.pallas_call(
        _odd_out_kernel,
        out_shape=out_shape,
        grid=(b, t // tm),
        in_specs=in_specs,
        out_specs=out_specs,
        compiler_params=_params(("parallel", "parallel")),
        name="odd_out",
    )(h_f, h_b, proj, w_out, x, g1, n2, sh2, sc2, w_r, b_r)


def _expert_kernel(be_ref, next_ref, rows_ref, nb_ref, x_ref, wg_hbm, bg_ref, wu_hbm, bu_ref, wd_hbm, bd_ref,
                   o_ref, w_f32, wg_bf, wu_bf, wd_bf, h_bf, sem, *, layer, chunk):
    i = pl.program_id(0)
    expert = be_ref[i]
    used = i < nb_ref[0]

    def weight_copies(e):
        return [pltpu.make_async_copy(src.at[layer, e], w_f32.at[j], sem.at[j])
                for j, src in enumerate((wg_hbm, wu_hbm, wd_hbm))]

    @pl.when(i == 0)
    def _():
        for cp in weight_copies(expert):
            cp.start()

    @pl.when(used & ((i == 0) | (expert != be_ref[jnp.maximum(i - 1, 0)])))
    def _():
        for cp in weight_copies(expert):
            cp.wait()
        wg_bf[...] = w_f32[0].astype(BF16)
        wu_bf[...] = w_f32[1].astype(BF16)
        wd_bf[...] = w_f32[2].astype(BF16)

        @pl.when(next_ref[i] >= 0)
        def _():
            for cp in weight_copies(next_ref[i]):
                cp.start()

    @pl.when(used)
    def _():
        xp = x_ref[...]
        row = lax.broadcasted_iota(jnp.int32, xp.shape, 0)
        xp = jnp.where(row < rows_ref[i], xp, 0)
        x = jnp.concatenate(_unpack_rows(xp), axis=1).astype(BF16)
        for c in range(wg_bf.shape[1] // chunk):
            cs = slice(c * chunk, (c + 1) * chunk)
            g = jnp.minimum(_dot(x, wg_bf[:, cs]) + bg_ref[0, 0, :, cs], SWIGLU_LIMIT)
            u = jnp.clip(_dot(x, wu_bf[:, cs]) + bu_ref[0, 0, :, cs], -SWIGLU_LIMIT, SWIGLU_LIMIT)
            h_bf[:, cs] = (g * jax.nn.sigmoid(SWIGLU_ALPHA * g) * (u + 1.0)).astype(BF16)
        o_ref[...] = _pack_rows(_dot(h_bf[...], wd_bf[...]) + bd_ref[0, 0])

    @pl.when(jnp.logical_not(used))
    def _():
        o_ref[...] = jnp.zeros_like(o_ref)


def _experts(xb, block_e, next_e, block_rows, n_used, layer, wg, bg, wu, bu, wd, bd):
    n_slots = xb.shape[0]
    depth, n_e, d, d_exp = wg.shape
    assert d == d_exp
    tm = EXPERT_TILE
    n_blocks = n_slots // tm
    xmap = lambda i, be, ne, br, nb: (jnp.minimum(i, nb[0] - 1), 0)
    bmap = lambda i, be, ne, br, nb: (layer, be[i], 0, 0)
    hbm = pl.BlockSpec(memory_space=pl.ANY)
    grid_spec = pltpu.PrefetchScalarGridSpec(
        num_scalar_prefetch=4,
        grid=(n_blocks,),
        in_specs=[
            pl.BlockSpec((tm, d // 2), xmap),
            hbm,
            pl.BlockSpec((1, 1, 1, d_exp), bmap),
            hbm,
            pl.BlockSpec((1, 1, 1, d_exp), bmap),
            hbm,
            pl.BlockSpec((1, 1, 1, d), bmap),
        ],
        out_specs=pl.BlockSpec((tm, d // 2), lambda i, be, ne, br, nb: (i, 0)),
        scratch_shapes=[pltpu.VMEM((3, d, d_exp), F32), pltpu.VMEM((d, d_exp), BF16), pltpu.VMEM((d, d_exp), BF16),
                        pltpu.VMEM((d_exp, d), BF16), pltpu.VMEM((tm, d_exp), BF16), pltpu.SemaphoreType.DMA((3,))],
    )
    return pl.pallas_call(
        functools.partial(_expert_kernel, layer=layer, chunk=256),
        out_shape=jax.ShapeDtypeStruct((n_slots, d // 2), jnp.int32),
        grid_spec=grid_spec,
        compiler_params=_params(("arbitrary",)),
        name="experts",
    )(block_e, next_e, block_rows, n_used, xb, wg, bg.reshape(depth, n_e, 1, d_exp), wu,
      bu.reshape(depth, n_e, 1, d_exp), wd, bd.reshape(depth, n_e, 1, d))


def _combine_kernel(x_ref, g2_ref, gates_ref, y_ref, o_ref):
    gates = gates_ref[0]
    half = y_ref.shape[-1]
    acc_lo = jnp.zeros((x_ref.shape[1], half), F32)
    acc_hi = jnp.zeros((x_ref.shape[1], half), F32)
    for k in range(TOP_K):
        lo, hi = _unpack_rows(y_ref[k])
        acc_lo = acc_lo + gates[:, k:k + 1] * lo
        acc_hi = acc_hi + gates[:, k:k + 1] * hi
    o_ref[0] = x_ref[0] + g2_ref[0] * jnp.concatenate([acc_lo, acc_hi], axis=1)


def _combine(x, g2, gates, y_sel, tok_offset):
    b, t, d = x.shape
    tm = int(np.gcd(min(t, 512), tok_offset)) if tok_offset else min(t, 512)
    n_t = t // tm
    off = tok_offset // tm
    return pl.pallas_call(
        _combine_kernel,
        out_shape=jax.ShapeDtypeStruct((b, t, d), F32),
        grid=(b, n_t),
        in_specs=[
            pl.BlockSpec((1, tm, d), lambda bi, i: (bi, i, 0)),
            pl.BlockSpec((1, 1, d), lambda bi, i: (bi, 0, 0)),
            pl.BlockSpec((1, tm, ROUTER_PAD), lambda bi, i: (bi, i, 0)),
            pl.BlockSpec((TOP_K, tm, d // 2), lambda bi, i: (0, off + bi * n_t + i, 0)),
        ],
        out_specs=pl.BlockSpec((1, tm, d), lambda bi, i: (bi, i, 0)),
        compiler_params=_params(("parallel", "parallel")),
        name="moe_combine",
    )(x, g2, gates, y_sel)


def _route_kernel(ids_ref, upper_ref, dest_ref, counts_ref, run_ref, *, tile):
    p = pl.program_id(0)
    i = pl.program_id(1)
    tm = ids_ref.shape[0]
    ids_t = ids_ref[...].T
    expert = lax.broadcasted_iota(jnp.int32, (N_EXPERTS, tm), 0)
    chosen = [ids_t[k:k + 1, :] == expert for k in range(TOP_K)]
    picks = functools.reduce(jnp.add, [c.astype(F32) for c in chosen])
    tile_counts = jnp.sum(picks, axis=1, keepdims=True)

    @pl.when((p == 0) & (i == 0))
    def _():
        run_ref[...] = jnp.zeros_like(run_ref)

    @pl.when(p == 0)
    def _():
        run_ref[...] += tile_counts

    @pl.when((p == 1) & (i == 0))
    def _():
        counts = run_ref[...]
        counts_ref[...] = counts.astype(jnp.int32)
        padded = jnp.floor((counts + (tile - 1)) * (1.0 / tile)) * tile
        row = lax.broadcasted_iota(jnp.int32, counts.shape, 0)
        ends = padded
        for s in (1, 2, 4, 8, 16):
            ends = ends + jnp.where(row >= s, pltpu.roll(ends, s, 0), 0.0)
        run_ref[...] = ends - padded

    @pl.when(p == 1)
    def _():
        before = _dot(picks.astype(BF16), upper_ref[...])
        slot = before + run_ref[:, 0:1]
        rows = [jnp.sum(jnp.where(c, slot, 0.0), axis=0, keepdims=True) for c in chosen]
        rows += [jnp.zeros_like(rows[0])] * (dest_ref.shape[0] - TOP_K)
        dest_ref[...] = jnp.concatenate(rows, axis=0).astype(jnp.int32)
        run_ref[...] += tile_counts


def _route(ids):
    n = ids.shape[0]
    tm = int(np.gcd(n, 1024))
    n_tiles = n // tm
    upper = jnp.asarray(np.triu(np.ones((tm, tm), np.float32), 1), BF16)
    dest, counts = pl.pallas_call(
        functools.partial(_route_kernel, tile=EXPERT_TILE),
        out_shape=[jax.ShapeDtypeStruct((SUBLANES, n), jnp.int32),
                   jax.ShapeDtypeStruct((N_EXPERTS, LANES), jnp.int32)],
        grid=(2, n_tiles),
        in_specs=[pl.BlockSpec((tm, ROUTER_PAD), lambda p, i: (i, 0)),
                  pl.BlockSpec((tm, tm), lambda p, i: (0, 0))],
        out_specs=[pl.BlockSpec((SUBLANES, tm), lambda p, i: (0, i * p)),
                   pl.BlockSpec((N_EXPERTS, LANES), lambda p, i: (0, 0))],
        scratch_shapes=[pltpu.VMEM((N_EXPERTS, LANES), F32)],
        compiler_params=_params(("arbitrary", "arbitrary")),
        name="route",
    )(ids, upper)
    return dest, counts[:, 0]


def _row_gather(table, idx):
    info = plsc.get_sparse_core_info()
    n_cores, n_workers = info.num_cores, info.num_cores * info.num_subcores
    n_rows, width = idx.shape[0], table.shape[1]
    per_worker = n_rows // n_workers
    n_chunks = per_worker // SC_GATHER_ROWS
    assert per_worker * n_workers == n_rows and n_chunks * SC_GATHER_ROWS == per_worker
    mesh = plsc.VectorSubcoreMesh(core_axis_name="c", subcore_axis_name="s")

    @functools.partial(
        pl.kernel, mesh=mesh,
        out_type=jax.ShapeDtypeStruct((n_rows, width), table.dtype),
        scratch_types=[pltpu.VMEM((SC_GATHER_ROWS,), jnp.int32), pltpu.VMEM((SC_GATHER_ROWS, width), table.dtype),
                       pltpu.SemaphoreType.DMA],
    )
    def gather_kernel(table_hbm, idx_hbm, out_hbm, idx_v, rows_v, sem):
        base = (lax.axis_index("s") * n_cores + lax.axis_index("c")) * per_worker

        @pl.loop(0, n_chunks)
        def _(c):
            off = base + c * SC_GATHER_ROWS
            pltpu.sync_copy(idx_hbm.at[pl.ds(off, SC_GATHER_ROWS)], idx_v)
            pltpu.async_copy(table_hbm.at[idx_v], rows_v, sem).wait()
            pltpu.sync_copy(rows_v, out_hbm.at[pl.ds(off, SC_GATHER_ROWS)])

    return gather_kernel(table, idx)


def _row_scatter(rows, dest, n_out):
    info = plsc.get_sparse_core_info()
    n_cores, n_workers = info.num_cores, info.num_cores * info.num_subcores
    n_choices, n_rows = dest.shape
    width = rows.shape[1]
    chunk_rows = SC_GATHER_ROWS
    per_worker = n_rows // (n_workers * chunk_rows)
    assert per_worker * n_workers * chunk_rows == n_rows == rows.shape[0]
    idx = dest.reshape(n_choices, n_workers, per_worker, chunk_rows).transpose(1, 0, 2, 3)
    mesh = plsc.VectorSubcoreMesh(core_axis_name="c", subcore_axis_name="s")

    @functools.partial(
        pl.kernel, mesh=mesh,
        out_type=jax.ShapeDtypeStruct((n_out, width), rows.dtype),
        scratch_types=[pltpu.VMEM((n_choices, per_worker, chunk_rows), jnp.int32),
                       pltpu.VMEM((chunk_rows, width), rows.dtype), pltpu.SemaphoreType.DMA],
    )
    def scatter_kernel(rows_hbm, idx_hbm, out_hbm, idx_v, rows_v, sem):
        worker = lax.axis_index("s") * n_cores + lax.axis_index("c")
        pltpu.sync_copy(idx_hbm.at[worker], idx_v)

        @pl.loop(0, per_worker)
        def _(c):
            first_row = (worker * per_worker + c) * chunk_rows
            pltpu.sync_copy(rows_hbm.at[pl.ds(first_row, chunk_rows)], rows_v)
            copies = [pltpu.async_copy(rows_v, out_hbm.at[idx_v.at[k, c]], sem) for k in range(n_choices)]
            for cp in copies:
                cp.wait()

    return scatter_kernel(rows, idx)


def _moe(h2, ids, layer, w_exp):
    n = h2.shape[0]
    nk = n * TOP_K
    tm = EXPERT_TILE
    dest, counts = _route(ids)
    dest = dest[:TOP_K]
    padded = (counts + tm - 1) // tm * tm
    pends = jnp.cumsum(padded)
    n_blocks = -(-nk // tm) + N_EXPERTS
    n_slots = n_blocks * tm
    block_start = jnp.arange(n_blocks, dtype=jnp.int32) * tm
    block_e = jnp.minimum(jnp.sum((pends[None] <= block_start[:, None]).astype(jnp.int32), axis=1), N_EXPERTS - 1)
    is_e = (block_e[:, None] == jnp.arange(N_EXPERTS, dtype=jnp.int32)[None]).astype(jnp.int32)
    block_rows = jnp.clip(jnp.sum(is_e * (pends - padded + counts)[None], axis=1) - block_start, 0, tm)
    n_used = (pends[-1] // tm).astype(jnp.int32).reshape(1)
    e_ids = jnp.arange(N_EXPERTS, dtype=jnp.int32)
    later = (e_ids[None] > e_ids[:, None]) & (counts[None] > 0)
    next_nonempty = jnp.min(jnp.where(later, e_ids[None], N_EXPERTS), axis=1)
    next_nonempty = jnp.where(next_nonempty == N_EXPERTS, -1, next_nonempty)
    next_e = jnp.sum(is_e * next_nonempty[None], axis=1)
    xb = _row_scatter(h2, dest, n_slots)
    yb = _experts(xb, block_e, next_e, block_rows, n_used, layer, *w_exp)
    return _row_gather(yb, dest.reshape(-1)).reshape(TOP_K, n, -1)


def _mod_parts(mod_l, b):
    d = mod_l.shape[-1] // 6
    lat = [mod_l[:b, k * d:(k + 1) * d].reshape(b, 1, d) for k in range(6)]
    ctx = [jnp.broadcast_to(mod_l[b, k * d:(k + 1) * d].reshape(1, 1, d), (b, 1, d)) for k in range(6)]
    return lat, ctx


def kernel(x, c, ctx, c_ctx, ada_w, ada_b, norm1_g, norm2_g, ev_w_in, ev_w_out, ev_q_gain, ev_k_gain, ev_rpb, ev_conv_w, ev_conv_b, od_w_in, od_w_out, od_conv_w, od_conv_b, od_fwd_wa, od_fwd_ba, od_fwd_wx, od_fwd_bx, od_fwd_lam, od_bwd_wa, od_bwd_ba, od_bwd_wx, od_bwd_bx, od_bwd_lam, router_w, router_b, exp_w_gate, exp_b_gate, exp_w_up, exp_b_up, exp_w_down, exp_b_down):
    b, t, d = x.shape
    l = ctx.shape[1]
    assert ada_w.shape[0] == DEPTH == 2 and t % GRID_W == 0 and t // GRID_W >= WIN_H

    n_rows_c = -(-(b + 1) // SUBLANES) * SUBLANES
    cvec = jnp.zeros((n_rows_c, d), F32).at[:b].set(c).at[b].set(c_ctx)
    mod = _ada_mod(cvec, ada_w, ada_b)

    def router(layer):
        w_r = jnp.zeros((d, ROUTER_PAD), F32).at[:, :N_EXPERTS].set(router_w[layer]).astype(BF16)
        b_r = jnp.zeros((1, ROUTER_PAD), F32).at[0, :N_EXPERTS].set(router_b[layer])
        return w_r, b_r

    w_exp = (exp_w_gate, exp_b_gate, exp_w_up, exp_b_up, exp_w_down, exp_b_down)

    (sh1, sc1, g1, sh2, sc2, g2), (csh1, csc1, cg1, csh2, csc2, cg2) = _mod_parts(mod[0], b)
    n1 = norm1_g[0].reshape(1, d)
    n2 = norm2_g[0].reshape(1, d)
    w_in = ev_w_in[0].astype(BF16)
    w_out = ev_w_out[0].astype(BF16)
    q_scale = NA_HEAD_DIM ** -0.5 * LOG2_E
    head_gain = jnp.stack([jnp.tile(ev_q_gain[0] * q_scale, NA_HEADS), jnp.tile(ev_k_gain[0], NA_HEADS)])
    head_gain = head_gain.reshape(2, 1, NA_WIDTH).astype(F32)
    proj = _inproj(x, n1, sh1, sc1, w_in, head_gain)
    proj_c = _inproj(ctx, n1, csh1, csc1, w_in, head_gain)
    kh, r0, row_type, patterns = _na_tables(t // GRID_W)
    bias_tab = _na_bias_table(ev_rpb[0], patterns)
    o_a = _neighbourhood_attention(proj, proj_c, bias_tab, r0, row_type, kh)
    oc_a = _context_attention(proj_c)
    w_r, b_r = router(0)
    x1, h2, ids, gates = _even_out(o_a, proj, ev_conv_w[0], ev_conv_b[0], w_out, x, g1, n2, sh2, sc2, w_r, b_r)
    c1, h2c, ids_c, gates_c = _even_out(oc_a, proj_c, ev_conv_w[0], ev_conv_b[0], w_out, ctx, cg1, n2, csh2, csc2,
                                        w_r, b_r)
    tokens = jnp.concatenate([h2c.reshape(b * l, d // 2), h2.reshape(b * t, d // 2)], axis=0)
    ids_all = jnp.concatenate([ids_c.reshape(b * l, ROUTER_PAD), ids.reshape(b * t, ROUTER_PAD)], axis=0)
    y_sel = _moe(tokens, ids_all, 0, w_exp)
    hctx = _combine(c1, cg2, gates_c, y_sel, 0)
    x = _combine(x1, g2, gates, y_sel, b * l)

    (sh1, sc1, g1, sh2, sc2, g2), (csh1, csc1, _, _, _, _) = _mod_parts(mod[1], b)
    n1 = norm1_g[1].reshape(1, d)
    n2 = norm2_g[1].reshape(1, d)
    w_in = od_w_in[0].astype(BF16)
    width = w_in.shape[1] // 2
    proj = _inproj(x, n1, sh1, sc1, w_in)
    u_ctx = _inproj(hctx, n1, csh1, csc1, w_in[:, width:])
    h_dir = []
    for reverse, (wa, ba, wx, bx, lam) in ((False, (od_fwd_wa, od_fwd_ba, od_fwd_wx, od_fwd_bx, od_fwd_lam)),
                                           (True, (od_bwd_wa, od_bwd_ba, od_bwd_wx, od_bwd_bx, od_bwd_lam))):
        w_cat = jnp.concatenate([wa[0], wx[0]], axis=-1).astype(BF16)
        h_dir.append(_lru_scan(proj, u_ctx, od_conv_w[0], od_conv_b[0], w_cat, ba[0], bx[0], lam[0], reverse))
    w_r, b_r = router(1)
    x1, h2, ids, gates = _odd_out(h_dir[0], h_dir[1], proj, od_w_out[0].astype(BF16), x, g1, n2, sh2, sc2, w_r, b_r)
    y_sel = _moe(h2.reshape(b * t, d // 2), ids.reshape(b * t, ROUTER_PAD), 1, w_exp)
    return _combine(x1, g2, gates, y_sel, 0)
```

```python
import functools

import numpy as np
import jax
import jax.numpy as jnp
from jax import lax
from jax.experimental import pallas as pl
from jax.experimental.pallas import tpu as pltpu
from jax.experimental.pallas import tpu_sc as plsc

DEPTH = 2
GRID_W = 64
EPS = 1e-6
NEG_INF = -1e30
LOG2_E = 1.4426950408889634
NA_HEADS = 8
NA_HEAD_DIM = 64
NA_WIDTH = NA_HEADS * NA_HEAD_DIM
HEAD_PAIRS = NA_HEADS // 2
WIN_H = 8
WIN_W = 16
SC_CONV = 3
LRU_BLOCKS = 4
LRU_CONV = 4
LRU_C = 8.0
N_EXPERTS = 32
TOP_K = 4
SWIGLU_LIMIT = 7.0
SWIGLU_ALPHA = 1.702

LANES = 128
SUBLANES = 8
HALO = 16
ROUTER_PAD = LANES
NA_ROWS_PER_STEP = 4
EXPERT_TILE = 512
SC_GATHER_ROWS = 64
VMEM_LIMIT = 56 * 1024 * 1024

F32 = jnp.float32
BF16 = jnp.bfloat16


def _params(sem, vmem=VMEM_LIMIT):
    return pltpu.CompilerParams(dimension_semantics=sem, vmem_limit_bytes=vmem)


def _dot(a, b):
    return jnp.dot(a, b, preferred_element_type=F32)


def _dot_nt(a, b):
    return lax.dot_general(a, b, (((1,), (1,)), ((), ())), preferred_element_type=F32)


def _pack_rows(v):
    w = v.shape[-1] // 2
    lo = lax.bitcast_convert_type(v[:, :w].astype(BF16).astype(F32), jnp.int32)
    hi = lax.bitcast_convert_type(v[:, w:].astype(BF16).astype(F32), jnp.int32)
    return lax.shift_right_logical(lo, 16) | (hi & jnp.int32(-65536))


def _unpack_rows(p):
    lo = lax.bitcast_convert_type(lax.shift_left(p, 16), F32)
    hi = lax.bitcast_convert_type(p & jnp.int32(-65536), F32)
    return lo, hi


def _rms_mod(x, g, shift, scale):
    ms = jnp.mean(x * x, axis=-1, keepdims=True)
    y = x * lax.rsqrt(ms + EPS) * g
    return y * (1.0 + scale) + shift


def _ada_kernel(c_ref, w_ref, b_ref, o_ref):
    c = c_ref[...]
    s = (c * jax.nn.sigmoid(c)).astype(BF16)
    o_ref[0] = _dot(s, w_ref[0].astype(BF16)) + b_ref[0]


def _ada_mod(cvec, ada_w, ada_b):
    depth, d, n = ada_w.shape
    r = cvec.shape[0]
    tn = 1536
    return pl.pallas_call(
        _ada_kernel,
        out_shape=jax.ShapeDtypeStruct((depth, r, n), F32),
        grid=(depth, n // tn),
        in_specs=[
            pl.BlockSpec((r, d), lambda l, j: (0, 0)),
            pl.BlockSpec((1, d, tn), lambda l, j: (l, 0, j)),
            pl.BlockSpec((1, 1, tn), lambda l, j: (l, 0, j)),
        ],
        out_specs=pl.BlockSpec((1, r, tn), lambda l, j: (l, 0, j)),
        compiler_params=_params(("parallel", "parallel")),
        name="ada_mod",
    )(cvec, ada_w, ada_b.reshape(depth, 1, n))


def _inproj_kernel(x_ref, g_ref, sh_ref, sc_ref, w_ref, hg_ref, ones_ref, o_ref, *, n_tiles, tn, n_headnorm):
    h = _rms_mod(x_ref[0], g_ref[...], sh_ref[0], sc_ref[0]).astype(BF16)
    for j in range(n_tiles):
        y = _dot(h, w_ref[:, j * tn:(j + 1) * tn])
        if j < n_headnorm:
            ms = _dot((y * y).astype(BF16), ones_ref[...]) * (1.0 / NA_HEAD_DIM)
            y = y * lax.rsqrt(ms + EPS) * hg_ref[j]
        o_ref[0, :, j * tn:(j + 1) * tn] = y.astype(o_ref.dtype)


def _inproj(x, g, shift, scale, w, head_gain=None):
    b, t, d = x.shape
    n = w.shape[1]
    tn = NA_WIDTH
    tm = min(t, 512)
    n_headnorm = 0 if head_gain is None else head_gain.shape[0]
    if head_gain is None:
        head_gain = jnp.ones((1, 1, tn), F32)
    hid = np.arange(tn) // NA_HEAD_DIM
    ones_bd = jnp.asarray((hid[:, None] == hid[None, :]), BF16)
    kern = functools.partial(_inproj_kernel, n_tiles=n // tn, tn=tn, n_headnorm=n_headnorm)
    return pl.pallas_call(
        kern,
        out_shape=jax.ShapeDtypeStruct((b, t, n), BF16),
        grid=(b, t // tm),
        in_specs=[
            pl.BlockSpec((1, tm, d), lambda bi, i: (bi, i, 0)),
            pl.BlockSpec((1, d), lambda bi, i: (0, 0)),
            pl.BlockSpec((1, 1, d), lambda bi, i: (bi, 0, 0)),
            pl.BlockSpec((1, 1, d), lambda bi, i: (bi, 0, 0)),
            pl.BlockSpec((d, n), lambda bi, i: (0, 0)),
            pl.BlockSpec(head_gain.shape, lambda bi, i: (0, 0, 0)),
            pl.BlockSpec((tn, tn), lambda bi, i: (0, 0)),
        ],
        out_specs=pl.BlockSpec((1, tm, n), lambda bi, i: (bi, i, 0)),
        compiler_params=_params(("parallel", "parallel")),
        name="inproj",
    )(x, g, shift, scale, w, head_gain, ones_bd)


def _na_tables(rows):
    kh = min(WIN_H, rows)
    r = np.arange(rows)
    r0 = np.clip(r - kh // 2, 0, rows - kh)
    dr = r0[:, None] + np.arange(kh)[None] - r[:, None] + WIN_H - 1
    patterns, row_type = np.unique(dr, axis=0, return_inverse=True)
    return kh, r0.astype(np.int32), row_type.reshape(-1).astype(np.int32), patterns


def _na_bias_table(rpb, patterns):
    qc = np.arange(GRID_W)
    kc = np.arange(GRID_W)
    c0 = np.clip(qc - WIN_W // 2, 0, GRID_W - WIN_W)[:, None]
    valid = (kc[None] >= c0) & (kc[None] < c0 + WIN_W)
    dc = np.clip(kc[None] - qc[:, None] + WIN_W - 1, 0, 2 * WIN_W - 2)
    n_pat, kh = patterns.shape
    onehot_dc = jnp.asarray(dc[None] == np.arange(2 * WIN_W - 1)[:, None, None], F32)
    tab = jnp.einsum('hpic,cqk->hpiqk', rpb.astype(F32)[:, patterns], onehot_dc,
                     precision=lax.Precision.HIGHEST)
    tab = jnp.where(valid[None, None, None], tab * LOG2_E, NEG_INF)
    tab = tab.reshape(HEAD_PAIRS, 2, n_pat, kh, GRID_W, GRID_W)
    tab = tab.transpose(2, 0, 1, 4, 3, 5)
    return tab.reshape(n_pat, HEAD_PAIRS, 2 * GRID_W, kh * GRID_W)


def _pair_attention(q, keys, values, biases):
    m = q.shape[0]
    lane = lax.broadcasted_iota(jnp.int32, q.shape, 1)
    zero = jnp.zeros_like(q)
    qq = jnp.concatenate([jnp.where(lane < NA_HEAD_DIM, q, zero), jnp.where(lane >= NA_HEAD_DIM, q, zero)], axis=0)
    scores = []
    for k, bias in zip(keys, biases):
        s = _dot_nt(qq, k)
        scores.append(s if bias is None else s + bias)
    s = jnp.concatenate(scores, axis=1)
    e = jnp.exp2(s - jnp.max(s, axis=-1, keepdims=True))
    denom = jnp.sum(e, axis=-1, keepdims=True)
    e = e.astype(BF16)
    o, start = 0.0, 0
    for v in values:
        o = o + _dot(e[:, start:start + v.shape[0]], v)
        start += v.shape[0]
    o = o * (1.0 / denom)
    lane_o = lax.broadcasted_iota(jnp.int32, (m, LANES), 1)
    return jnp.where(lane_o < NA_HEAD_DIM, o[:m], o[m:])


def _na_kernel(r0_ref, type_ref, q_ref, k_ref, v_ref, kc_ref, vc_ref, bias_ref, o_ref, *, kh, rows_per_step):
    for j in range(rows_per_step):
        r = pl.program_id(1) * rows_per_step + j
        start = pl.multiple_of(r0_ref[r] * GRID_W, GRID_W)
        rtype = type_ref[r]
        rows = slice(j * GRID_W, (j + 1) * GRID_W)
        for p in range(HEAD_PAIRS):
            cols = slice(p * LANES, (p + 1) * LANES)
            o = _pair_attention(
                q_ref[0, rows, cols],
                [k_ref[0, pl.ds(start, kh * GRID_W), cols], kc_ref[0, :, cols]],
                [v_ref[0, pl.ds(start, kh * GRID_W), cols], vc_ref[0, :, cols]],
                [bias_ref[rtype, p], None],
            )
            o_ref[0, rows, cols] = o.astype(o_ref.dtype)


def _neighbourhood_attention(proj, proj_c, bias_tab, r0, row_type, kh):
    b, t, _ = proj.shape
    l = proj_c.shape[1]
    rows = t // GRID_W
    w = NA_WIDTH
    rps = int(np.gcd(rows, NA_ROWS_PER_STEP))
    q_rows = rps * GRID_W
    grid_spec = pltpu.PrefetchScalarGridSpec(
        num_scalar_prefetch=2,
        grid=(b, rows // rps),
        in_specs=[
            pl.BlockSpec((1, q_rows, w), lambda bi, r, *_: (bi, r, 0)),
            pl.BlockSpec((1, t, w), lambda bi, r, *_: (bi, 0, 1)),
            pl.BlockSpec((1, t, w), lambda bi, r, *_: (bi, 0, 2)),
            pl.BlockSpec((1, l, w), lambda bi, r, *_: (bi, 0, 1)),
            pl.BlockSpec((1, l, w), lambda bi, r, *_: (bi, 0, 2)),
            pl.BlockSpec(bias_tab.shape, lambda bi, r, *_: (0, 0, 0, 0)),
        ],
        out_specs=pl.BlockSpec((1, q_rows, w), lambda bi, r, *_: (bi, r, 0)),
    )
    return pl.pallas_call(
        functools.partial(_na_kernel, kh=kh, rows_per_step=rps),
        out_shape=jax.ShapeDtypeStruct((b, t, w), BF16),
        grid_spec=grid_spec,
        compiler_params=_params(("parallel", "arbitrary")),
        name="na_attention",
    )(jnp.asarray(r0), jnp.asarray(row_type), proj, proj, proj, proj_c, proj_c, bias_tab)


def _ctx_attn_kernel(q_ref, k_ref, v_ref, o_ref):
    for p in range(HEAD_PAIRS):
        cols = slice(p * LANES, (p + 1) * LANES)
        o = _pair_attention(q_ref[0, :, cols], [k_ref[0, :, cols]], [v_ref[0, :, cols]], [None])
        o_ref[0, :, cols] = o.astype(o_ref.dtype)


def _context_attention(proj_c):
    b, l, _ = proj_c.shape
    w = NA_WIDTH
    return pl.pallas_call(
        _ctx_attn_kernel,
        out_shape=jax.ShapeDtypeStruct((b, l, w), BF16),
        grid=(b,),
        in_specs=[pl.BlockSpec((1, l, w), lambda bi, j=j: (bi, 0, j)) for j in range(3)],
        out_specs=pl.BlockSpec((1, l, w), lambda bi: (bi, 0, 0)),
        compiler_params=_params(("parallel",)),
        name="ctx_attention",
    )(proj_c, proj_c, proj_c)


def _top4(logits):
    lane = lax.broadcasted_iota(jnp.int32, logits.shape, 1)
    cur = jnp.where(lane < N_EXPERTS, logits, -jnp.inf)
    vals, idxs = [], []
    for _ in range(TOP_K):
        m = jnp.max(cur, axis=-1, keepdims=True)
        first = jnp.min(jnp.where(cur == m, lane, ROUTER_PAD).astype(F32), axis=-1, keepdims=True)
        idx = first.astype(jnp.int32)
        vals.append(m)
        idxs.append(idx)
        cur = jnp.where(lane == idx, -jnp.inf, cur)
    exps = [jnp.exp(v - vals[0]) for v in vals]
    inv = 1.0 / functools.reduce(jnp.add, exps)
    ids = jnp.zeros(logits.shape, jnp.int32)
    gates = jnp.zeros(logits.shape, F32)
    for k in range(TOP_K):
        ids = jnp.where(lane == k, idxs[k], ids)
        gates = jnp.where(lane == k, exps[k] * inv, gates)
    return ids, gates


def _layer_tail(y, x_ref, g1_ref, n2_ref, sh2_ref, sc2_ref, wr_ref, br_ref, xo_ref, h2_ref, ids_ref, gates_ref):
    x_new = x_ref[0] + g1_ref[0] * y
    xo_ref[0] = x_new
    h2 = _rms_mod(x_new, n2_ref[...], sh2_ref[0], sc2_ref[0])
    h2_ref[0] = _pack_rows(h2)
    logits = _dot(h2.astype(BF16), wr_ref[...]) + br_ref[...]
    ids, gates = _top4(logits)
    ids_ref[0] = ids
    gates_ref[0] = gates


def _tail_specs(b, t, d, tm):
    row = lambda bi, i: (bi, i, 0)
    per_b = lambda bi, i: (bi, 0, 0)
    const = lambda bi, i: (0, 0)
    in_specs = [
        pl.BlockSpec((1, tm, d), row),
        pl.BlockSpec((1, 1, d), per_b),
        pl.BlockSpec((1, d), const),
        pl.BlockSpec((1, 1, d), per_b),
        pl.BlockSpec((1, 1, d), per_b),
        pl.BlockSpec((d, ROUTER_PAD), const),
        pl.BlockSpec((1, ROUTER_PAD), const),
    ]
    out_specs = [
        pl.BlockSpec((1, tm, d), row),
        pl.BlockSpec((1, tm, d // 2), row),
        pl.BlockSpec((1, tm, ROUTER_PAD), row),
        pl.BlockSpec((1, tm, ROUTER_PAD), row),
    ]
    out_shape = [
        jax.ShapeDtypeStruct((b, t, d), F32),
        jax.ShapeDtypeStruct((b, t, d // 2), jnp.int32),
        jax.ShapeDtypeStruct((b, t, ROUTER_PAD), jnp.int32),
        jax.ShapeDtypeStruct((b, t, ROUTER_PAD), F32),
    ]
    return in_specs, out_specs, out_shape


def _halo_fix(rolled, at_row, halo_row, present):
    n = rolled.shape[0]
    first = at_row < SUBLANES
    assert first or at_row >= n - SUBLANES
    slab = rolled[:SUBLANES] if first else rolled[n - SUBLANES:]
    sub = lax.broadcasted_iota(jnp.int32, slab.shape, 0)
    fill = jnp.where(present, halo_row, jnp.zeros_like(halo_row))
    slab = jnp.where(sub == at_row % SUBLANES, fill, slab)
    return jnp.concatenate([slab, rolled[SUBLANES:]] if first else [rolled[:n - SUBLANES], slab], axis=0)


def _even_out_kernel(oa_ref, bg_ref, cg_ref, xin_ref, cgp_ref, xinp_ref, cgn_ref, xinn_ref, cw_ref, cb_ref,
                     wa_ref, wb_ref, *tail_refs, tm):
    i = pl.program_id(1)
    has_prev = i > 0
    has_next = i < pl.num_programs(1) - 1
    u = cg_ref[0].astype(F32) * xin_ref[0].astype(F32)
    u_prev = (cgp_ref[0].astype(F32) * xinp_ref[0].astype(F32))[HALO - 1:HALO]
    u_next = (cgn_ref[0].astype(F32) * xinn_ref[0].astype(F32))[0:1]
    u_m1 = _halo_fix(pltpu.roll(u, 1, 0), 0, u_prev, has_prev)
    u_p1 = _halo_fix(pltpu.roll(u, tm - 1, 0), tm - 1, u_next, has_next)
    cw = cw_ref[...]
    conv = u_m1 * cw[0:1] + u * cw[1:2] + u_p1 * cw[2:3] + cb_ref[...]
    o_b = bg_ref[0].astype(F32) * conv
    y = _dot(oa_ref[0], wa_ref[...]) + _dot(o_b.astype(BF16), wb_ref[...])
    _layer_tail(y, *tail_refs)


def _even_out(o_a, proj, conv_w, conv_b, w_out, x, g1, n2, sh2, sc2, w_r, b_r):
    b, t, d = x.shape
    w = NA_WIDTH
    tm = min(t, 512)
    hb = tm // HALO
    n_hblocks = t // HALO
    row = lambda bi, i: (bi, i, 0)
    const = lambda bi, i: (0, 0)
    prev = lambda col: (lambda bi, i: (bi, jnp.maximum(i * hb - 1, 0), col))
    nxt = lambda col: (lambda bi, i: (bi, jnp.minimum((i + 1) * hb, n_hblocks - 1), col))
    tail_in, out_specs, out_shape = _tail_specs(b, t, d, tm)
    in_specs = [
        pl.BlockSpec((1, tm, w), row),
        pl.BlockSpec((1, tm, w), lambda bi, i: (bi, i, 3)),
        pl.BlockSpec((1, tm, w), lambda bi, i: (bi, i, 4)),
        pl.BlockSpec((1, tm, w), lambda bi, i: (bi, i, 5)),
        pl.BlockSpec((1, HALO, w), prev(4)),
        pl.BlockSpec((1, HALO, w), prev(5)),
        pl.BlockSpec((1, HALO, w), nxt(4)),
        pl.BlockSpec((1, HALO, w), nxt(5)),
        pl.BlockSpec((SC_CONV, w), const),
        pl.BlockSpec((1, w), const),
        pl.BlockSpec((w, d), const),
        pl.BlockSpec((w, d), const),
    ] + tail_in
    return pl.pallas_call(
        functools.partial(_even_out_kernel, tm=tm),
        out_shape=out_shape,
        grid=(b, t // tm),
        in_specs=in_specs,
        out_specs=out_specs,
        compiler_params=_params(("parallel", "parallel")),
        name="even_out",
    )(o_a, proj, proj, proj, proj, proj, proj, proj, conv_w, conv_b.reshape(1, w),
      w_out[:w], w_out[w:], x, g1, n2, sh2, sc2, w_r, b_r)


def _log_sigmoid(x):
    return jnp.minimum(x, 0.0) - jnp.log1p(jnp.exp(-jnp.abs(x)))


def _lru_tile(u_ref, up_ref, un_ref, has_prev, has_next, cw_ref, cb_ref, w_ref, ba_ref, bx_ref, lam_ref,
              carry_ref, o_ref, *, ts, reverse):
    width = u_ref.shape[-1]
    blk = width // LRU_BLOCKS
    u = u_ref[0].astype(F32)
    up = up_ref[0].astype(F32)
    un = un_ref[0].astype(F32)
    u_m1 = _halo_fix(pltpu.roll(u, 1, 0), 0, up[HALO - 1:HALO], has_prev)
    u_p1 = _halo_fix(pltpu.roll(u, ts - 1, 0), ts - 1, un[0:1], has_next)
    u_p2 = _halo_fix(pltpu.roll(u, ts - 2, 0), ts - 2, un[0:1], has_next)
    u_p2 = _halo_fix(u_p2, ts - 1, un[1:2], has_next)
    cw = cw_ref[...]
    uc = u_m1 * cw[0:1] + u * cw[1:2] + u_p1 * cw[2:3] + u_p2 * cw[3:4] + cb_ref[...]
    ucb = uc.astype(BF16)
    za, zx = [], []
    for h in range(LRU_BLOCKS):
        z = _dot(ucb[:, h * blk:(h + 1) * blk], w_ref[h])
        za.append(z[:, :blk])
        zx.append(z[:, blk:])
    r = jax.nn.sigmoid(jnp.concatenate(za, axis=1) + ba_ref[...])
    gate_i = jax.nn.sigmoid(jnp.concatenate(zx, axis=1) + bx_ref[...])
    log_a = (LRU_C * _log_sigmoid(lam_ref[...])) * r
    a = jnp.exp(log_a)
    th = jnp.tanh(log_a)
    m2 = -2.0 * th / (1.0 - th)
    mult = jnp.where(m2 > 0.0, m2 * lax.rsqrt(m2), 0.0)
    bcoef = mult * gate_i * uc

    groups = ts // SUBLANES
    a3 = a.reshape(groups, SUBLANES, width)
    b3 = bcoef.reshape(groups, SUBLANES, width)
    sub = lax.broadcasted_iota(jnp.int32, a3.shape, 1)
    for s in (1, 2, 4):
        shift = SUBLANES - s if reverse else s
        a_sh = pltpu.roll(a3, shift, 1)
        b_sh = pltpu.roll(b3, shift, 1)
        m = (sub < SUBLANES - s) if reverse else (sub >= s)
        b3 = jnp.where(m, a3 * b_sh + b3, b3)
        a3 = jnp.where(m, a3 * a_sh, a3)

    h = carry_ref[0:1, :]
    order = range(groups - 1, -1, -1) if reverse else range(groups)
    edge = 0 if reverse else SUBLANES - 1
    for g in order:
        hg = a3[g] * h + b3[g]
        if o_ref is not None:
            o_ref[0, g * SUBLANES:(g + 1) * SUBLANES, :] = hg.astype(o_ref.dtype)
        h = hg[edge:edge + 1, :]
    carry_ref[...] = jnp.broadcast_to(h, carry_ref.shape)


def _lru_kernel(uc_ref, ucp_ref, ucn_ref, ul_ref, ulp_ref, uln_ref, cw_ref, cb_ref, w_ref, ba_ref, bx_ref,
                lam_ref, o_ref, carry_ref, *, ts, n_ctx_tiles, n_lat_tiles, reverse):
    j = pl.program_id(1)
    shared = (cw_ref, cb_ref, w_ref, ba_ref, bx_ref, lam_ref, carry_ref)

    @pl.when(j == 0)
    def _():
        carry_ref[...] = jnp.zeros_like(carry_ref)

    def tile_pos(step, n):
        return (n - 1 - step) if reverse else step

    @pl.when(j < n_ctx_tiles)
    def _():
        pos = tile_pos(j, n_ctx_tiles)
        _lru_tile(uc_ref, ucp_ref, ucn_ref, pos > 0, pos < n_ctx_tiles - 1, *shared, None, ts=ts, reverse=reverse)

    @pl.when(j >= n_ctx_tiles)
    def _():
        pos = tile_pos(j - n_ctx_tiles, n_lat_tiles)
        _lru_tile(ul_ref, ulp_ref, uln_ref, pos > 0, pos < n_lat_tiles - 1, *shared, o_ref, ts=ts, reverse=reverse)


def _lru_scan(proj, u_ctx, conv_w, conv_b, w_cat, ba, bx, lam, reverse):
    b, t, _ = proj.shape
    l, width = u_ctx.shape[1], u_ctx.shape[2]
    ts = min(256, l, t)
    n_c, n_l = l // ts, t // ts
    hb = ts // HALO

    def pos_of(step, n):
        step = jnp.clip(step, 0, n - 1)
        return (n - 1 - step) if reverse else step

    def tile_map(off, n, col):
        return lambda bi, j: (bi, pos_of(j - off, n), col)

    def prev_map(off, n, col):
        return lambda bi, j: (bi, jnp.maximum(pos_of(j - off, n) * hb - 1, 0), col)

    def next_map(off, n, col):
        return lambda bi, j: (bi, jnp.minimum((pos_of(j - off, n) + 1) * hb, n * hb - 1), col)

    const2 = lambda bi, j: (0, 0)
    in_specs = [
        pl.BlockSpec((1, ts, width), tile_map(0, n_c, 0)),
        pl.BlockSpec((1, HALO, width), prev_map(0, n_c, 0)),
        pl.BlockSpec((1, HALO, width), next_map(0, n_c, 0)),
        pl.BlockSpec((1, ts, width), tile_map(n_c, n_l, 1)),
        pl.BlockSpec((1, HALO, width), prev_map(n_c, n_l, 1)),
        pl.BlockSpec((1, HALO, width), next_map(n_c, n_l, 1)),
        pl.BlockSpec((LRU_CONV, width), const2),
        pl.BlockSpec((1, width), const2),
        pl.BlockSpec(w_cat.shape, lambda bi, j: (0, 0, 0)),
        pl.BlockSpec((1, width), const2),
        pl.BlockSpec((1, width), const2),
        pl.BlockSpec((1, width), const2),
    ]
    kern = functools.partial(_lru_kernel, ts=ts, n_ctx_tiles=n_c, n_lat_tiles=n_l, reverse=reverse)
    return pl.pallas_call(
        kern,
        out_shape=jax.ShapeDtypeStruct((b, t, width), BF16),
        grid=(b, n_c + n_l),
        in_specs=in_specs,
        out_specs=pl.BlockSpec((1, ts, width), tile_map(n_c, n_l, 0)),
        scratch_shapes=[pltpu.VMEM((SUBLANES, width), F32)],
        compiler_params=_params(("parallel", "arbitrary")),
        name="lru_scan_bwd" if reverse else "lru_scan_fwd",
    )(u_ctx, u_ctx, u_ctx, proj, proj, proj, conv_w, conv_b.reshape(1, width), w_cat,
      ba.reshape(1, width), bx.reshape(1, width), lam.reshape(1, width))


def _odd_out_kernel(hf_ref, hb_ref, gate_ref, w_ref, *tail_refs):
    hsum = hf_ref[0].astype(F32) + hb_ref[0].astype(F32)
    z = hsum * jax.nn.gelu(gate_ref[0].astype(F32), approximate=True)
    y = _dot(z.astype(BF16), w_ref[...])
    _layer_tail(y, *tail_refs)


def _odd_out(h_f, h_b, proj, w_out, x, g1, n2, sh2, sc2, w_r, b_r):
    b, t, d = x.shape
    width = h_f.shape[-1]
    tm = min(t, 512)
    row = lambda bi, i: (bi, i, 0)
    tail_in, out_specs, out_shape = _tail_specs(b, t, d, tm)
    in_specs = [
        pl.BlockSpec((1, tm, width), row),
        pl.BlockSpec((1, tm, width), row),
        pl.BlockSpec((1, tm, width), row),
        pl.BlockSpec((width, d), lambda bi, i: (0, 0)),
    ] + tail_in
    return pl.pallas_call(
        _odd_out_kernel,
        out_shape=out_shape,
        grid=(b, t // tm),
        in_specs=in_specs,
        out_specs=out_specs,
        compiler_params=_params(("parallel", "parallel")),
        name="odd_out",
    )(h_f, h_b, proj, w_out, x, g1, n2, sh2, sc2, w_r, b_r)


def _expert_kernel(be_ref, next_ref, rows_ref, nb_ref, x_ref, wg_hbm, bg_ref, wu_hbm, bu_ref, wd_hbm, bd_ref,
                   o_ref, w_f32, wg_bf, wu_bf, wd_bf, h_bf, sem, *, layer, chunk):
    i = pl.program_id(0)
    expert = be_ref[i]
    used = i < nb_ref[0]

    def weight_copies(e):
        return [pltpu.make_async_copy(src.at[layer, e], w_f32.at[j], sem.at[j])
                for j, src in enumerate((wg_hbm, wu_hbm, wd_hbm))]

    @pl.when(i == 0)
    def _():
        for cp in weight_copies(expert):
            cp.start()

    @pl.when(used & ((i == 0) | (expert != be_ref[jnp.maximum(i - 1, 0)])))
    def _():
        for cp in weight_copies(expert):
            cp.wait()
        wg_bf[...] = w_f32[0].astype(BF16)
        wu_bf[...] = w_f32[1].astype(BF16)
        wd_bf[...] = w_f32[2].astype(BF16)

        @pl.when(next_ref[i] >= 0)
        def _():
            for cp in weight_copies(next_ref[i]):
                cp.start()

    @pl.when(used)
    def _():
        xp = x_ref[...]
        row = lax.broadcasted_iota(jnp.int32, xp.shape, 0)
        xp = jnp.where(row < rows_ref[i], xp, 0)
        x = jnp.concatenate(_unpack_rows(xp), axis=1).astype(BF16)
        for c in range(wg_bf.shape[1] // chunk):
            cs = slice(c * chunk, (c + 1) * chunk)
            g = jnp.minimum(_dot(x, wg_bf[:, cs]) + bg_ref[0, 0, :, cs], SWIGLU_LIMIT)
            u = jnp.clip(_dot(x, wu_bf[:, cs]) + bu_ref[0, 0, :, cs], -SWIGLU_LIMIT, SWIGLU_LIMIT)
            h_bf[:, cs] = (g * jax.nn.sigmoid(SWIGLU_ALPHA * g) * (u + 1.0)).astype(BF16)
        o_ref[...] = _pack_rows(_dot(h_bf[...], wd_bf[...]) + bd_ref[0, 0])

    @pl.when(jnp.logical_not(used))
    def _():
        o_ref[...] = jnp.zeros_like(o_ref)


def _experts(xb, block_e, next_e, block_rows, n_used, layer, wg, bg, wu, bu, wd, bd):
    n_slots = xb.shape[0]
    depth, n_e, d, d_exp = wg.shape
    assert d == d_exp
    tm = EXPERT_TILE
    n_blocks = n_slots // tm
    xmap = lambda i, be, ne, br, nb: (jnp.minimum(i, nb[0] - 1), 0)
    bmap = lambda i, be, ne, br, nb: (layer, be[i], 0, 0)
    hbm = pl.BlockSpec(memory_space=pl.ANY)
    grid_spec = pltpu.PrefetchScalarGridSpec(
        num_scalar_prefetch=4,
        grid=(n_blocks,),
        in_specs=[
            pl.BlockSpec((tm, d // 2), xmap),
            hbm,
            pl.BlockSpec((1, 1, 1, d_exp), bmap),
            hbm,
            pl.BlockSpec((1, 1, 1, d_exp), bmap),
            hbm,
            pl.BlockSpec((1, 1, 1, d), bmap),
        ],
        out_specs=pl.BlockSpec((tm, d // 2), lambda i, be, ne, br, nb: (i, 0)),
        scratch_shapes=[pltpu.VMEM((3, d, d_exp), F32), pltpu.VMEM((d, d_exp), BF16), pltpu.VMEM((d, d_exp), BF16),
                        pltpu.VMEM((d_exp, d), BF16), pltpu.VMEM((tm, d_exp), BF16), pltpu.SemaphoreType.DMA((3,))],
    )
    return pl.pallas_call(
        functools.partial(_expert_kernel, layer=layer, chunk=256),
        out_shape=jax.ShapeDtypeStruct((n_slots, d // 2), jnp.int32),
        grid_spec=grid_spec,
        compiler_params=_params(("arbitrary",)),
        name="experts",
    )(block_e, next_e, block_rows, n_used, xb, wg, bg.reshape(depth, n_e, 1, d_exp), wu,
      bu.reshape(depth, n_e, 1, d_exp), wd, bd.reshape(depth, n_e, 1, d))


def _combine_kernel(x_ref, g2_ref, gates_ref, y_ref, o_ref):
    gates = gates_ref[0]
    half = y_ref.shape[-1]
    acc_lo = jnp.zeros((x_ref.shape[1], half), F32)
    acc_hi = jnp.zeros((x_ref.shape[1], half), F32)
    for k in range(TOP_K):
        lo, hi = _unpack_rows(y_ref[k])
        acc_lo = acc_lo + gates[:, k:k + 1] * lo
        acc_hi = acc_hi + gates[:, k:k + 1] * hi
    o_ref[0] = x_ref[0] + g2_ref[0] * jnp.concatenate([acc_lo, acc_hi], axis=1)


def _combine(x, g2, gates, y_sel, tok_offset):
    b, t, d = x.shape
    tm = int(np.gcd(min(t, 512), tok_offset)) if tok_offset else min(t, 512)
    n_t = t // tm
    off = tok_offset // tm
    return pl.pallas_call(
        _combine_kernel,
        out_shape=jax.ShapeDtypeStruct((b, t, d), F32),
        grid=(b, n_t),
        in_specs=[
            pl.BlockSpec((1, tm, d), lambda bi, i: (bi, i, 0)),
            pl.BlockSpec((1, 1, d), lambda bi, i: (bi, 0, 0)),
            pl.BlockSpec((1, tm, ROUTER_PAD), lambda bi, i: (bi, i, 0)),
            pl.BlockSpec((TOP_K, tm, d // 2), lambda bi, i: (0, off + bi * n_t + i, 0)),
        ],
        out_specs=pl.BlockSpec((1, tm, d), lambda bi, i: (bi, i, 0)),
        compiler_params=_params(("parallel", "parallel")),
        name="moe_combine",
    )(x, g2, gates, y_sel)


def _route_kernel(ids_ref, upper_ref, dest_ref, counts_ref, run_ref, *, tile):
    p = pl.program_id(0)
    i = pl.program_id(1)
    tm = ids_ref.shape[0]
    ids_t = ids_ref[...].T
    expert = lax.broadcasted_iota(jnp.int32, (N_EXPERTS, tm), 0)
    chosen = [ids_t[k:k + 1, :] == expert for k in range(TOP_K)]
    picks = functools.reduce(jnp.add, [c.astype(F32) for c in chosen])
    tile_counts = jnp.sum(picks, axis=1, keepdims=True)

    @pl.when((p == 0) & (i == 0))
    def _():
        run_ref[...] = jnp.zeros_like(run_ref)

    @pl.when(p == 0)
    def _():
        run_ref[...] += tile_counts

    @pl.when((p == 1) & (i == 0))
    def _():
        counts = run_ref[...]
        counts_ref[...] = counts.astype(jnp.int32)
        padded = jnp.floor((counts + (tile - 1)) * (1.0 / tile)) * tile
        row = lax.broadcasted_iota(jnp.int32, counts.shape, 0)
        ends = padded
        for s in (1, 2, 4, 8, 16):
            ends = ends + jnp.where(row >= s, pltpu.roll(ends, s, 0), 0.0)
        run_ref[...] = ends - padded

    @pl.when(p == 1)
    def _():
        before = _dot(picks.astype(BF16), upper_ref[...])
        slot = before + run_ref[:, 0:1]
        rows = [jnp.sum(jnp.where(c, slot, 0.0), axis=0, keepdims=True) for c in chosen]
        rows += [jnp.zeros_like(rows[0])] * (dest_ref.shape[0] - TOP_K)
        dest_ref[...] = jnp.concatenate(rows, axis=0).astype(jnp.int32)
        run_ref[...] += tile_counts


def _route(ids):
    n = ids.shape[0]
    tm = int(np.gcd(n, 1024))
    n_tiles = n // tm
    upper = jnp.asarray(np.triu(np.ones((tm, tm), np.float32), 1), BF16)
    dest, counts = pl.pallas_call(
        functools.partial(_route_kernel, tile=EXPERT_TILE),
        out_shape=[jax.ShapeDtypeStruct((SUBLANES, n), jnp.int32),
                   jax.ShapeDtypeStruct((N_EXPERTS, LANES), jnp.int32)],
        grid=(2, n_tiles),
        in_specs=[pl.BlockSpec((tm, ROUTER_PAD), lambda p, i: (i, 0)),
                  pl.BlockSpec((tm, tm), lambda p, i: (0, 0))],
        out_specs=[pl.BlockSpec((SUBLANES, tm), lambda p, i: (0, i * p)),
                   pl.BlockSpec((N_EXPERTS, LANES), lambda p, i: (0, 0))],
        scratch_shapes=[pltpu.VMEM((N_EXPERTS, LANES), F32)],
        compiler_params=_params(("arbitrary", "arbitrary")),
        name="route",
    )(ids, upper)
    return dest, counts[:, 0]


def _row_gather(table, idx):
    info = plsc.get_sparse_core_info()
    n_cores, n_workers = info.num_cores, info.num_cores * info.num_subcores
    n_rows, width = idx.shape[0], table.shape[1]
    per_worker = n_rows // n_workers
    n_chunks = per_worker // SC_GATHER_ROWS
    assert per_worker * n_workers == n_rows and n_chunks * SC_GATHER_ROWS == per_worker
    mesh = plsc.VectorSubcoreMesh(core_axis_name="c", subcore_axis_name="s")

    @functools.partial(
        pl.kernel, mesh=mesh,
        out_type=jax.ShapeDtypeStruct((n_rows, width), table.dtype),
        scratch_types=[pltpu.VMEM((SC_GATHER_ROWS,), jnp.int32), pltpu.VMEM((SC_GATHER_ROWS, width), table.dtype),
                       pltpu.SemaphoreType.DMA],
    )
    def gather_kernel(table_hbm, idx_hbm, out_hbm, idx_v, rows_v, sem):
        base = (lax.axis_index("s") * n_cores + lax.axis_index("c")) * per_worker

        @pl.loop(0, n_chunks)
        def _(c):
            off = base + c * SC_GATHER_ROWS
            pltpu.sync_copy(idx_hbm.at[pl.ds(off, SC_GATHER_ROWS)], idx_v)
            pltpu.async_copy(table_hbm.at[idx_v], rows_v, sem).wait()
            pltpu.sync_copy(rows_v, out_hbm.at[pl.ds(off, SC_GATHER_ROWS)])

    return gather_kernel(table, idx)


def _row_scatter(rows, dest, n_out):
    info = plsc.get_sparse_core_info()
    n_cores, n_workers = info.num_cores, info.num_cores * info.num_subcores
    n_choices, n_rows = dest.shape
    width = rows.shape[1]
    chunk_rows = SC_GATHER_ROWS
    per_worker = n_rows // (n_workers * chunk_rows)
    assert per_worker * n_workers * chunk_rows == n_rows == rows.shape[0]
    idx = dest.reshape(n_choices, n_workers, per_worker, chunk_rows).transpose(1, 0, 2, 3)
    mesh = plsc.VectorSubcoreMesh(core_axis_name="c", subcore_axis_name="s")

    @functools.partial(
        pl.kernel, mesh=mesh,
        out_type=jax.ShapeDtypeStruct((n_out, width), rows.dtype),
        scratch_types=[pltpu.VMEM((n_choices, per_worker, chunk_rows), jnp.int32),
                       pltpu.VMEM((chunk_rows, width), rows.dtype), pltpu.SemaphoreType.DMA],
    )
    def scatter_kernel(rows_hbm, idx_hbm, out_hbm, idx_v, rows_v, sem):
        worker = lax.axis_index("s") * n_cores + lax.axis_index("c")
        pltpu.sync_copy(idx_hbm.at[worker], idx_v)

        @pl.loop(0, per_worker)
        def _(c):
            first_row = (worker * per_worker + c) * chunk_rows
            pltpu.sync_copy(rows_hbm.at[pl.ds(first_row, chunk_rows)], rows_v)
            copies = [pltpu.async_copy(rows_v, out_hbm.at[idx_v.at[k, c]], sem) for k in range(n_choices)]
            for cp in copies:
                cp.wait()

    return scatter_kernel(rows, idx)


def _moe(h2, ids, layer, w_exp):
    n = h2.shape[0]
    nk = n * TOP_K
    tm = EXPERT_TILE
    dest, counts = _route(ids)
    dest = dest[:TOP_K]
    padded = (counts + tm - 1) // tm * tm
    pends = jnp.cumsum(padded)
    n_blocks = -(-nk // tm) + N_EXPERTS
    n_slots = n_blocks * tm
    block_start = jnp.arange(n_blocks, dtype=jnp.int32) * tm
    block_e = jnp.minimum(jnp.sum((pends[None] <= block_start[:, None]).astype(jnp.int32), axis=1), N_EXPERTS - 1)
    is_e = (block_e[:, None] == jnp.arange(N_EXPERTS, dtype=jnp.int32)[None]).astype(jnp.int32)
    block_rows = jnp.clip(jnp.sum(is_e * (pends - padded + counts)[None], axis=1) - block_start, 0, tm)
    n_used = (pends[-1] // tm).astype(jnp.int32).reshape(1)
    e_ids = jnp.arange(N_EXPERTS, dtype=jnp.int32)
    later = (e_ids[None] > e_ids[:, None]) & (counts[None] > 0)
    next_nonempty = jnp.min(jnp.where(later, e_ids[None], N_EXPERTS), axis=1)
    next_nonempty = jnp.where(next_nonempty == N_EXPERTS, -1, next_nonempty)
    next_e = jnp.sum(is_e * next_nonempty[None], axis=1)
    xb = _row_scatter(h2, dest, n_slots)
    yb = _experts(xb, block_e, next_e, block_rows, n_used, layer, *w_exp)
    return _row_gather(yb, dest.reshape(-1)).reshape(TOP_K, n, -1)


def _mod_parts(mod_l, b):
    d = mod_l.shape[-1] // 6
    lat = [mod_l[:b, k * d:(k + 1) * d].reshape(b, 1, d) for k in range(6)]
    ctx = [jnp.broadcast_to(mod_l[b, k * d:(k + 1) * d].reshape(1, 1, d), (b, 1, d)) for k in range(6)]
    return lat, ctx


def kernel(x, c, ctx, c_ctx, ada_w, ada_b, norm1_g, norm2_g, ev_w_in, ev_w_out, ev_q_gain, ev_k_gain, ev_rpb, ev_conv_w, ev_conv_b, od_w_in, od_w_out, od_conv_w, od_conv_b, od_fwd_wa, od_fwd_ba, od_fwd_wx, od_fwd_bx, od_fwd_lam, od_bwd_wa, od_bwd_ba, od_bwd_wx, od_bwd_bx, od_bwd_lam, router_w, router_b, exp_w_gate, exp_b_gate, exp_w_up, exp_b_up, exp_w_down, exp_b_down):
    b, t, d = x.shape
    l = ctx.shape[1]
    assert ada_w.shape[0] == DEPTH == 2 and t % GRID_W == 0 and t // GRID_W >= WIN_H

    n_rows_c = -(-(b + 1) // SUBLANES) * SUBLANES
    cvec = jnp.zeros((n_rows_c, d), F32).at[:b].set(c).at[b].set(c_ctx)
    mod = _ada_mod(cvec, ada_w, ada_b)

    def router(layer):
        w_r = jnp.zeros((d, ROUTER_PAD), F32).at[:, :N_EXPERTS].set(router_w[layer]).astype(BF16)
        b_r = jnp.zeros((1, ROUTER_PAD), F32).at[0, :N_EXPERTS].set(router_b[layer])
        return w_r, b_r

    w_exp = (exp_w_gate, exp_b_gate, exp_w_up, exp_b_up, exp_w_down, exp_b_down)

    (sh1, sc1, g1, sh2, sc2, g2), (csh1, csc1, cg1, csh2, csc2, cg2) = _mod_parts(mod[0], b)
    n1 = norm1_g[0].reshape(1, d)
    n2 = norm2_g[0].reshape(1, d)
    w_in = ev_w_in[0].astype(BF16)
    w_out = ev_w_out[0].astype(BF16)
    q_scale = NA_HEAD_DIM ** -0.5 * LOG2_E
    head_gain = jnp.stack([jnp.tile(ev_q_gain[0] * q_scale, NA_HEADS), jnp.tile(ev_k_gain[0], NA_HEADS)])
    head_gain = head_gain.reshape(2, 1, NA_WIDTH).astype(F32)
    proj = _inproj(x, n1, sh1, sc1, w_in, head_gain)
    proj_c = _inproj(ctx, n1, csh1, csc1, w_in, head_gain)
    kh, r0, row_type, patterns = _na_tables(t // GRID_W)
    bias_tab = _na_bias_table(ev_rpb[0], patterns)
    o_a = _neighbourhood_attention(proj, proj_c, bias_tab, r0, row_type, kh)
    oc_a = _context_attention(proj_c)
    w_r, b_r = router(0)
    x1, h2, ids, gates = _even_out(o_a, proj, ev_conv_w[0], ev_conv_b[0], w_out, x, g1, n2, sh2, sc2, w_r, b_r)
    c1, h2c, ids_c, gates_c = _even_out(oc_a, proj_c, ev_conv_w[0], ev_conv_b[0], w_out, ctx, cg1, n2, csh2, csc2,
                                        w_r, b_r)
    tokens = jnp.concatenate([h2c.reshape(b * l, d // 2), h2.reshape(b * t, d // 2)], axis=0)
    ids_all = jnp.concatenate([ids_c.reshape(b * l, ROUTER_PAD), ids.reshape(b * t, ROUTER_PAD)], axis=0)
    y_sel = _moe(tokens, ids_all, 0, w_exp)
    hctx = _combine(c1, cg2, gates_c, y_sel, 0)
    x = _combine(x1, g2, gates, y_sel, b * l)

    (sh1, sc1, g1, sh2, sc2, g2), (csh1, csc1, _, _, _, _) = _mod_parts(mod[1], b)
    n1 = norm1_g[1].reshape(1, d)
    n2 = norm2_g[1].reshape(1, d)
    w_in = od_w_in[0].astype(BF16)
    width = w_in.shape[1] // 2
    proj = _inproj(x, n1, sh1, sc1, w_in)
    u_ctx = _inproj(hctx, n1, csh1, csc1, w_in[:, width:])
    h_dir = []
    for reverse, (wa, ba, wx, bx, lam) in ((False, (od_fwd_wa, od_fwd_ba, od_fwd_wx, od_fwd_bx, od_fwd_lam)),
                                           (True, (od_bwd_wa, od_bwd_ba, od_bwd_wx, od_bwd_bx, od_bwd_lam))):
        w_cat = jnp.concatenate([wa[0], wx[0]], axis=-1).astype(BF16)
        h_dir.append(_lru_scan(proj, u_ctx, od_conv_w[0], od_conv_b[0], w_cat, ba[0], bx[0], lam[0], reverse))
    w_r, b_r = router(1)
    x1, h2, ids, gates = _odd_out(h_dir[0], h_dir[1], proj, od_w_out[0].astype(BF16), x, g1, n2, sh2, sc2, w_r, b_r)
    y_sel = _moe(h2.reshape(b * t, d // 2), ids.reshape(b * t, ROUTER_PAD), 1, w_exp)
    return _combine(x1, g2, gates, y_sel, 0)
```

```python
import functools

import numpy as np
import jax
import jax.numpy as jnp
from jax import lax
from jax.experimental import pallas as pl
from jax.experimental.pallas import tpu as pltpu
from jax.experimental.pallas import tpu_sc as plsc

DEPTH = 2
GRID_W = 64
EPS = 1e-6
NEG_INF = -1e30
LOG2_E = 1.4426950408889634
NA_HEADS = 8
NA_HEAD_DIM = 64
NA_WIDTH = NA_HEADS * NA_HEAD_DIM
HEAD_PAIRS = NA_HEADS // 2
WIN_H = 8
WIN_W = 16
SC_CONV = 3
LRU_BLOCKS = 4
LRU_CONV = 4
LRU_C = 8.0
N_EXPERTS = 32
TOP_K = 4
SWIGLU_LIMIT = 7.0
SWIGLU_ALPHA = 1.702

LANES = 128
SUBLANES = 8
HALO = 16
ROUTER_PAD = LANES
NA_ROWS_PER_STEP = 4
EXPERT_TILE = 512
SC_GATHER_ROWS = 64
SC_WORKERS = 32
VMEM_LIMIT = 56 * 1024 * 1024

F32 = jnp.float32
BF16 = jnp.bfloat16


def _params(sem, vmem=VMEM_LIMIT):
    return pltpu.CompilerParams(dimension_semantics=sem, vmem_limit_bytes=vmem)


def _dot(a, b):
    return jnp.dot(a, b, preferred_element_type=F32)


def _dot_nt(a, b):
    return lax.dot_general(a, b, (((1,), (1,)), ((), ())), preferred_element_type=F32)


def _pack_rows(v):
    w = v.shape[-1] // 2
    lo = lax.bitcast_convert_type(v[:, :w].astype(BF16).astype(F32), jnp.int32)
    hi = lax.bitcast_convert_type(v[:, w:].astype(BF16).astype(F32), jnp.int32)
    return lax.shift_right_logical(lo, 16) | (hi & jnp.int32(-65536))


def _unpack_rows(p):
    lo = lax.bitcast_convert_type(lax.shift_left(p, 16), F32)
    hi = lax.bitcast_convert_type(p & jnp.int32(-65536), F32)
    return lo, hi


def _rms_mod(x, g, shift, scale):
    ms = jnp.mean(x * x, axis=-1, keepdims=True)
    y = x * lax.rsqrt(ms + EPS) * g
    return y * (1.0 + scale) + shift


def _ada_kernel(c_ref, w_ref, b_ref, o_ref):
    c = c_ref[...]
    s = (c * jax.nn.sigmoid(c)).astype(BF16)
    o_ref[0] = _dot(s, w_ref[0].astype(BF16)) + b_ref[0]


def _ada_mod(cvec, ada_w, ada_b):
    depth, d, n = ada_w.shape
    r = cvec.shape[0]
    tn = 1536
    return pl.pallas_call(
        _ada_kernel,
        out_shape=jax.ShapeDtypeStruct((depth, r, n), F32),
        grid=(depth, n // tn),
        in_specs=[
            pl.BlockSpec((r, d), lambda l, j: (0, 0)),
            pl.BlockSpec((1, d, tn), lambda l, j: (l, 0, j)),
            pl.BlockSpec((1, 1, tn), lambda l, j: (l, 0, j)),
        ],
        out_specs=pl.BlockSpec((1, r, tn), lambda l, j: (l, 0, j)),
        compiler_params=_params(("parallel", "parallel")),
        name="ada_mod",
    )(cvec, ada_w, ada_b.reshape(depth, 1, n))


def _inproj_kernel(x_ref, g_ref, sh_ref, sc_ref, w_ref, hg_ref, ones_ref, o_ref, *, n_tiles, tn, n_headnorm):
    h = _rms_mod(x_ref[0], g_ref[...], sh_ref[0], sc_ref[0]).astype(BF16)
    for j in range(n_tiles):
        y = _dot(h, w_ref[:, j * tn:(j + 1) * tn])
        if j < n_headnorm:
            ms = _dot((y * y).astype(BF16), ones_ref[...]) * (1.0 / NA_HEAD_DIM)
            y = y * lax.rsqrt(ms + EPS) * hg_ref[j]
        o_ref[0, :, j * tn:(j + 1) * tn] = y.astype(o_ref.dtype)


def _inproj(x, g, shift, scale, w, head_gain=None):
    b, t, d = x.shape
    n = w.shape[1]
    tn = NA_WIDTH
    tm = min(t, 512)
    n_headnorm = 0 if head_gain is None else head_gain.shape[0]
    if head_gain is None:
        head_gain = jnp.ones((1, 1, tn), F32)
    hid = np.arange(tn) // NA_HEAD_DIM
    ones_bd = jnp.asarray((hid[:, None] == hid[None, :]), BF16)
    kern = functools.partial(_inproj_kernel, n_tiles=n // tn, tn=tn, n_headnorm=n_headnorm)
    return pl.pallas_call(
        kern,
        out_shape=jax.ShapeDtypeStruct((b, t, n), BF16),
        grid=(b, t // tm),
        in_specs=[
            pl.BlockSpec((1, tm, d), lambda bi, i: (bi, i, 0)),
            pl.BlockSpec((1, d), lambda bi, i: (0, 0)),
            pl.BlockSpec((1, 1, d), lambda bi, i: (bi, 0, 0)),
            pl.BlockSpec((1, 1, d), lambda bi, i: (bi, 0, 0)),
            pl.BlockSpec((d, n), lambda bi, i: (0, 0)),
            pl.BlockSpec(head_gain.shape, lambda bi, i: (0, 0, 0)),
            pl.BlockSpec((tn, tn), lambda bi, i: (0, 0)),
        ],
        out_specs=pl.BlockSpec((1, tm, n), lambda bi, i: (bi, i, 0)),
        compiler_params=_params(("parallel", "parallel")),
        name="inproj",
    )(x, g, shift, scale, w, head_gain, ones_bd)


def _na_tables(rows):
    kh = min(WIN_H, rows)
    r = np.arange(rows)
    r0 = np.clip(r - kh // 2, 0, rows - kh)
    dr = r0[:, None] + np.arange(kh)[None] - r[:, None] + WIN_H - 1
    patterns, row_type = np.unique(dr, axis=0, return_inverse=True)
    return kh, r0.astype(np.int32), row_type.reshape(-1).astype(np.int32), patterns


def _na_bias_table(rpb, patterns):
    qc = np.arange(GRID_W)
    kc = np.arange(GRID_W)
    c0 = np.clip(qc - WIN_W // 2, 0, GRID_W - WIN_W)[:, None]
    valid = (kc[None] >= c0) & (kc[None] < c0 + WIN_W)
    dc = np.clip(kc[None] - qc[:, None] + WIN_W - 1, 0, 2 * WIN_W - 2)
    n_pat, kh = patterns.shape
    onehot_dc = jnp.asarray(dc[None] == np.arange(2 * WIN_W - 1)[:, None, None], F32)
    tab = jnp.einsum('hpic,cqk->hpiqk', rpb.astype(F32)[:, patterns], onehot_dc,
                     precision=lax.Precision.HIGHEST)
    tab = jnp.where(valid[None, None, None], tab * LOG2_E, NEG_INF)
    tab = tab.reshape(HEAD_PAIRS, 2, n_pat, kh, GRID_W, GRID_W)
    tab = tab.transpose(2, 0, 1, 4, 3, 5)
    return tab.reshape(n_pat, HEAD_PAIRS, 2 * GRID_W, kh * GRID_W)


def _pair_attention(q, keys, values, biases):
    m = q.shape[0]
    lane = lax.broadcasted_iota(jnp.int32, q.shape, 1)
    zero = jnp.zeros_like(q)
    qq = jnp.concatenate([jnp.where(lane < NA_HEAD_DIM, q, zero), jnp.where(lane >= NA_HEAD_DIM, q, zero)], axis=0)
    scores = []
    for k, bias in zip(keys, biases):
        s = _dot_nt(qq, k)
        scores.append(s if bias is None else s + bias)
    s = jnp.concatenate(scores, axis=1)
    e = jnp.exp2(s - jnp.max(s, axis=-1, keepdims=True))
    denom = jnp.sum(e, axis=-1, keepdims=True)
    e = e.astype(BF16)
    o, start = 0.0, 0
    for v in values:
        o = o + _dot(e[:, start:start + v.shape[0]], v)
        start += v.shape[0]
    o = o * (1.0 / denom)
    lane_o = lax.broadcasted_iota(jnp.int32, (m, LANES), 1)
    return jnp.where(lane_o < NA_HEAD_DIM, o[:m], o[m:])


def _na_kernel(r0_ref, type_ref, q_ref, k_ref, v_ref, kc_ref, vc_ref, bias_ref, o_ref, *, kh, rows_per_step):
    for j in range(rows_per_step):
        r = pl.program_id(1) * rows_per_step + j
        start = pl.multiple_of(r0_ref[r] * GRID_W, GRID_W)
        rtype = type_ref[r]
        rows = slice(j * GRID_W, (j + 1) * GRID_W)
        for p in range(HEAD_PAIRS):
            cols = slice(p * LANES, (p + 1) * LANES)
            o = _pair_attention(
                q_ref[0, rows, cols],
                [k_ref[0, pl.ds(start, kh * GRID_W), cols], kc_ref[0, :, cols]],
                [v_ref[0, pl.ds(start, kh * GRID_W), cols], vc_ref[0, :, cols]],
                [bias_ref[rtype, p], None],
            )
            o_ref[0, rows, cols] = o.astype(o_ref.dtype)


def _neighbourhood_attention(proj, proj_c, bias_tab, r0, row_type, kh):
    b, t, _ = proj.shape
    l = proj_c.shape[1]
    rows = t // GRID_W
    w = NA_WIDTH
    rps = int(np.gcd(rows, NA_ROWS_PER_STEP))
    q_rows = rps * GRID_W
    grid_spec = pltpu.PrefetchScalarGridSpec(
        num_scalar_prefetch=2,
        grid=(b, rows // rps),
        in_specs=[
            pl.BlockSpec((1, q_rows, w), lambda bi, r, *_: (bi, r, 0)),
            pl.BlockSpec((1, t, w), lambda bi, r, *_: (bi, 0, 1)),
            pl.BlockSpec((1, t, w), lambda bi, r, *_: (bi, 0, 2)),
            pl.BlockSpec((1, l, w), lambda bi, r, *_: (bi, 0, 1)),
            pl.BlockSpec((1, l, w), lambda bi, r, *_: (bi, 0, 2)),
            pl.BlockSpec(bias_tab.shape, lambda bi, r, *_: (0, 0, 0, 0)),
        ],
        out_specs=pl.BlockSpec((1, q_rows, w), lambda bi, r, *_: (bi, r, 0)),
    )
    return pl.pallas_call(
        functools.partial(_na_kernel, kh=kh, rows_per_step=rps),
        out_shape=jax.ShapeDtypeStruct((b, t, w), BF16),
        grid_spec=grid_spec,
        compiler_params=_params(("parallel", "arbitrary")),
        name="na_attention",
    )(jnp.asarray(r0), jnp.asarray(row_type), proj, proj, proj, proj_c, proj_c, bias_tab)


def _ctx_attn_kernel(q_ref, k_ref, v_ref, o_ref):
    for p in range(HEAD_PAIRS):
        cols = slice(p * LANES, (p + 1) * LANES)
        o = _pair_attention(q_ref[0, :, cols], [k_ref[0, :, cols]], [v_ref[0, :, cols]], [None])
        o_ref[0, :, cols] = o.astype(o_ref.dtype)


def _context_attention(proj_c):
    b, l, _ = proj_c.shape
    w = NA_WIDTH
    return pl.pallas_call(
        _ctx_attn_kernel,
        out_shape=jax.ShapeDtypeStruct((b, l, w), BF16),
        grid=(b,),
        in_specs=[pl.BlockSpec((1, l, w), lambda bi, j=j: (bi, 0, j)) for j in range(3)],
        out_specs=pl.BlockSpec((1, l, w), lambda bi: (bi, 0, 0)),
        compiler_params=_params(("parallel",)),
        name="ctx_attention",
    )(proj_c, proj_c, proj_c)


def _top4(logits):
    lane = lax.broadcasted_iota(jnp.int32, logits.shape, 1)
    cur = jnp.where(lane < N_EXPERTS, logits, -jnp.inf)
    vals, idxs = [], []
    for _ in range(TOP_K):
        m = jnp.max(cur, axis=-1, keepdims=True)
        first = jnp.min(jnp.where(cur == m, lane, ROUTER_PAD).astype(F32), axis=-1, keepdims=True)
        idx = first.astype(jnp.int32)
        vals.append(m)
        idxs.append(idx)
        cur = jnp.where(lane == idx, -jnp.inf, cur)
    exps = [jnp.exp(v - vals[0]) for v in vals]
    inv = 1.0 / functools.reduce(jnp.add, exps)
    ids = jnp.zeros(logits.shape, jnp.int32)
    gates = jnp.zeros(logits.shape, F32)
    for k in range(TOP_K):
        ids = jnp.where(lane == k, idxs[k], ids)
        gates = jnp.where(lane == k, exps[k] * inv, gates)
    return ids, gates


def _layer_tail(y, x_ref, g1_ref, n2_ref, sh2_ref, sc2_ref, wr_ref, br_ref, xo_ref, h2_ref, ids_ref, gates_ref):
    x_new = x_ref[0] + g1_ref[0] * y
    xo_ref[0] = x_new
    h2 = _rms_mod(x_new, n2_ref[...], sh2_ref[0], sc2_ref[0])
    h2_ref[0] = _pack_rows(h2)
    logits = _dot(h2.astype(BF16), wr_ref[...]) + br_ref[...]
    ids, gates = _top4(logits)
    ids_ref[0] = ids
    gates_ref[0] = gates


def _tail_specs(b, t, d, tm):
    row = lambda bi, i: (bi, i, 0)
    per_b = lambda bi, i: (bi, 0, 0)
    const = lambda bi, i: (0, 0)
    in_specs = [
        pl.BlockSpec((1, tm, d), row),
        pl.BlockSpec((1, 1, d), per_b),
        pl.BlockSpec((1, d), const),
        pl.BlockSpec((1, 1, d), per_b),
        pl.BlockSpec((1, 1, d), per_b),
        pl.BlockSpec((d, ROUTER_PAD), const),
        pl.BlockSpec((1, ROUTER_PAD), const),
    ]
    out_specs = [
        pl.BlockSpec((1, tm, d), row),
        pl.BlockSpec((1, tm, d // 2), row),
        pl.BlockSpec((1, tm, ROUTER_PAD), row),
        pl.BlockSpec((1, tm, ROUTER_PAD), row),
    ]
    out_shape = [
        jax.ShapeDtypeStruct((b, t, d), F32),
        jax.ShapeDtypeStruct((b, t, d // 2), jnp.int32),
        jax.ShapeDtypeStruct((b, t, ROUTER_PAD), jnp.int32),
        jax.ShapeDtypeStruct((b, t, ROUTER_PAD), F32),
    ]
    return in_specs, out_specs, out_shape


def _halo_fix(rolled, at_row, halo_row, present):
    n = rolled.shape[0]
    first = at_row < SUBLANES
    assert first or at_row >= n - SUBLANES
    slab = rolled[:SUBLANES] if first else rolled[n - SUBLANES:]
    sub = lax.broadcasted_iota(jnp.int32, slab.shape, 0)
    fill = jnp.where(present, halo_row, jnp.zeros_like(halo_row))
    slab = jnp.where(sub == at_row % SUBLANES, fill, slab)
    return jnp.concatenate([slab, rolled[SUBLANES:]] if first else [rolled[:n - SUBLANES], slab], axis=0)


def _even_out_kernel(oa_ref, bg_ref, cg_ref, xin_ref, cgp_ref, xinp_ref, cgn_ref, xinn_ref, cw_ref, cb_ref,
                     wa_ref, wb_ref, *tail_refs, tm):
    i = pl.program_id(1)
    has_prev = i > 0
    has_next = i < pl.num_programs(1) - 1
    u = cg_ref[0].astype(F32) * xin_ref[0].astype(F32)
    u_prev = (cgp_ref[0].astype(F32) * xinp_ref[0].astype(F32))[HALO - 1:HALO]
    u_next = (cgn_ref[0].astype(F32) * xinn_ref[0].astype(F32))[0:1]
    u_m1 = _halo_fix(pltpu.roll(u, 1, 0), 0, u_prev, has_prev)
    u_p1 = _halo_fix(pltpu.roll(u, tm - 1, 0), tm - 1, u_next, has_next)
    cw = cw_ref[...]
    conv = u_m1 * cw[0:1] + u * cw[1:2] + u_p1 * cw[2:3] + cb_ref[...]
    o_b = bg_ref[0].astype(F32) * conv
    y = _dot(oa_ref[0], wa_ref[...]) + _dot(o_b.astype(BF16), wb_ref[...])
    _layer_tail(y, *tail_refs)


def _even_out(o_a, proj, conv_w, conv_b, w_out, x, g1, n2, sh2, sc2, w_r, b_r):
    b, t, d = x.shape
    w = NA_WIDTH
    tm = min(t, 512)
    hb = tm // HALO
    n_hblocks = t // HALO
    row = lambda bi, i: (bi, i, 0)
    const = lambda bi, i: (0, 0)
    prev = lambda col: (lambda bi, i: (bi, jnp.maximum(i * hb - 1, 0), col))
    nxt = lambda col: (lambda bi, i: (bi, jnp.minimum((i + 1) * hb, n_hblocks - 1), col))
    tail_in, out_specs, out_shape = _tail_specs(b, t, d, tm)
    in_specs = [
        pl.BlockSpec((1, tm, w), row),
        pl.BlockSpec((1, tm, w), lambda bi, i: (bi, i, 3)),
        pl.BlockSpec((1, tm, w), lambda bi, i: (bi, i, 4)),
        pl.BlockSpec((1, tm, w), lambda bi, i: (bi, i, 5)),
        pl.BlockSpec((1, HALO, w), prev(4)),
        pl.BlockSpec((1, HALO, w), prev(5)),
        pl.BlockSpec((1, HALO, w), nxt(4)),
        pl.BlockSpec((1, HALO, w), nxt(5)),
        pl.BlockSpec((SC_CONV, w), const),
        pl.BlockSpec((1, w), const),
        pl.BlockSpec((w, d), const),
        pl.BlockSpec((w, d), const),
    ] + tail_in
    return pl.pallas_call(
        functools.partial(_even_out_kernel, tm=tm),
        out_shape=out_shape,
        grid=(b, t // tm),
        in_specs=in_specs,
        out_specs=out_specs,
        compiler_params=_params(("parallel", "parallel")),
        name="even_out",
    )(o_a, proj, proj, proj, proj, proj, proj, proj, conv_w, conv_b.reshape(1, w),
      w_out[:w], w_out[w:], x, g1, n2, sh2, sc2, w_r, b_r)


def _log_sigmoid(x):
    return jnp.minimum(x, 0.0) - jnp.log1p(jnp.exp(-jnp.abs(x)))


def _lru_tile(u_ref, up_ref, un_ref, has_prev, has_next, cw_ref, cb_ref, w_ref, ba_ref, bx_ref, lam_ref,
              carry_ref, o_ref, *, ts, reverse):
    width = u_ref.shape[-1]
    blk = width // LRU_BLOCKS
    u = u_ref[0].astype(F32)
    up = up_ref[0].astype(F32)
    un = un_ref[0].astype(F32)
    u_m1 = _halo_fix(pltpu.roll(u, 1, 0), 0, up[HALO - 1:HALO], has_prev)
    u_p1 = _halo_fix(pltpu.roll(u, ts - 1, 0), ts - 1, un[0:1], has_next)
    u_p2 = _halo_fix(pltpu.roll(u, ts - 2, 0), ts - 2, un[0:1], has_next)
    u_p2 = _halo_fix(u_p2, ts - 1, un[1:2], has_next)
    cw = cw_ref[...]
    uc = u_m1 * cw[0:1] + u * cw[1:2] + u_p1 * cw[2:3] + u_p2 * cw[3:4] + cb_ref[...]
    ucb = uc.astype(BF16)
    za, zx = [], []
    for h in range(LRU_BLOCKS):
        z = _dot(ucb[:, h * blk:(h + 1) * blk], w_ref[h])
        za.append(z[:, :blk])
        zx.append(z[:, blk:])
    r = jax.nn.sigmoid(jnp.concatenate(za, axis=1) + ba_ref[...])
    gate_i = jax.nn.sigmoid(jnp.concatenate(zx, axis=1) + bx_ref[...])
    log_a = (LRU_C * _log_sigmoid(lam_ref[...])) * r
    a = jnp.exp(log_a)
    th = jnp.tanh(log_a)
    m2 = -2.0 * th / (1.0 - th)
    mult = jnp.where(m2 > 0.0, m2 * lax.rsqrt(m2), 0.0)
    bcoef = mult * gate_i * uc

    groups = ts // SUBLANES
    a3 = a.reshape(groups, SUBLANES, width)
    b3 = bcoef.reshape(groups, SUBLANES, width)
    sub = lax.broadcasted_iota(jnp.int32, a3.shape, 1)
    for s in (1, 2, 4):
        shift = SUBLANES - s if reverse else s
        a_sh = pltpu.roll(a3, shift, 1)
        b_sh = pltpu.roll(b3, shift, 1)
        m = (sub < SUBLANES - s) if reverse else (sub >= s)
        b3 = jnp.where(m, a3 * b_sh + b3, b3)
        a3 = jnp.where(m, a3 * a_sh, a3)

    h = carry_ref[0:1, :]
    order = range(groups - 1, -1, -1) if reverse else range(groups)
    edge = 0 if reverse else SUBLANES - 1
    for g in order:
        hg = a3[g] * h + b3[g]
        if o_ref is not None:
            o_ref[0, g * SUBLANES:(g + 1) * SUBLANES, :] = hg.astype(o_ref.dtype)
        h = hg[edge:edge + 1, :]
    carry_ref[...] = jnp.broadcast_to(h, carry_ref.shape)


def _lru_kernel(uc_ref, ucp_ref, ucn_ref, ul_ref, ulp_ref, uln_ref, cw_ref, cb_ref, w_ref, ba_ref, bx_ref,
                lam_ref, o_ref, carry_ref, *, ts, n_ctx_tiles, n_lat_tiles, reverse):
    j = pl.program_id(1)
    shared = (cw_ref, cb_ref, w_ref, ba_ref, bx_ref, lam_ref, carry_ref)

    @pl.when(j == 0)
    def _():
        carry_ref[...] = jnp.zeros_like(carry_ref)

    def tile_pos(step, n):
        return (n - 1 - step) if reverse else step

    @pl.when(j < n_ctx_tiles)
    def _():
        pos = tile_pos(j, n_ctx_tiles)
        _lru_tile(uc_ref, ucp_ref, ucn_ref, pos > 0, pos < n_ctx_tiles - 1, *shared, None, ts=ts, reverse=reverse)

    @pl.when(j >= n_ctx_tiles)
    def _():
        pos = tile_pos(j - n_ctx_tiles, n_lat_tiles)
        _lru_tile(ul_ref, ulp_ref, uln_ref, pos > 0, pos < n_lat_tiles - 1, *shared, o_ref, ts=ts, reverse=reverse)


def _lru_scan(proj, u_ctx, conv_w, conv_b, w_cat, ba, bx, lam, reverse):
    b, t, _ = proj.shape
    l, width = u_ctx.shape[1], u_ctx.shape[2]
    ts = min(256, l, t)
    n_c, n_l = l // ts, t // ts
    hb = ts // HALO

    def pos_of(step, n):
        step = jnp.clip(step, 0, n - 1)
        return (n - 1 - step) if reverse else step

    def tile_map(off, n, col):
        return lambda bi, j: (bi, pos_of(j - off, n), col)

    def prev_map(off, n, col):
        return lambda bi, j: (bi, jnp.maximum(pos_of(j - off, n) * hb - 1, 0), col)

    def next_map(off, n, col):
        return lambda bi, j: (bi, jnp.minimum((pos_of(j - off, n) + 1) * hb, n * hb - 1), col)

    const2 = lambda bi, j: (0, 0)
    in_specs = [
        pl.BlockSpec((1, ts, width), tile_map(0, n_c, 0)),
        pl.BlockSpec((1, HALO, width), prev_map(0, n_c, 0)),
        pl.BlockSpec((1, HALO, width), next_map(0, n_c, 0)),
        pl.BlockSpec((1, ts, width), tile_map(n_c, n_l, 1)),
        pl.BlockSpec((1, HALO, width), prev_map(n_c, n_l, 1)),
        pl.BlockSpec((1, HALO, width), next_map(n_c, n_l, 1)),
        pl.BlockSpec((LRU_CONV, width), const2),
        pl.BlockSpec((1, width), const2),
        pl.BlockSpec(w_cat.shape, lambda bi, j: (0, 0, 0)),
        pl.BlockSpec((1, width), const2),
        pl.BlockSpec((1, width), const2),
        pl.BlockSpec((1, width), const2),
    ]
    kern = functools.partial(_lru_kernel, ts=ts, n_ctx_tiles=n_c, n_lat_tiles=n_l, reverse=reverse)
    return pl.pallas_call(
        kern,
        out_shape=jax.ShapeDtypeStruct((b, t, width), BF16),
        grid=(b, n_c + n_l),
        in_specs=in_specs,
        out_specs=pl.BlockSpec((1, ts, width), tile_map(n_c, n_l, 0)),
        scratch_shapes=[pltpu.VMEM((SUBLANES, width), F32)],
        compiler_params=_params(("parallel", "arbitrary")),
        name="lru_scan_bwd" if reverse else "lru_scan_fwd",
    )(u_ctx, u_ctx, u_ctx, proj, proj, proj, conv_w, conv_b.reshape(1, width), w_cat,
      ba.reshape(1, width), bx.reshape(1, width), lam.reshape(1, width))


def _odd_out_kernel(hf_ref, hb_ref, gate_ref, w_ref, *tail_refs):
    hsum = hf_ref[0].astype(F32) + hb_ref[0].astype(F32)
    z = hsum * jax.nn.gelu(gate_ref[0].astype(F32), approximate=True)
    y = _dot(z.astype(BF16), w_ref[...])
    _layer_tail(y, *tail_refs)


def _odd_out(h_f, h_b, proj, w_out, x, g1, n2, sh2, sc2, w_r, b_r):
    b, t, d = x.shape
    width = h_f.shape[-1]
    tm = min(t, 512)
    row = lambda bi, i: (bi, i, 0)
    tail_in, out_specs, out_shape = _tail_specs(b, t, d, tm)
    in_specs = [
        pl.BlockSpec((1, tm, width), row),
        pl.BlockSpec((1, tm, width), row),
        pl.BlockSpec((1, tm, width), row),
        pl.BlockSpec((width, d), lambda bi, i: (0, 0)),
    ] + tail_in
    return pl.pallas_call(
        _odd_out_kernel,
        out_shape=out_shape,
        grid=(b, t // tm),
        in_specs=in_specs,
        out_specs=out_specs,
        compiler_params=_params(("parallel", "parallel")),
        name="odd_out",
    )(h_f, h_b, proj, w_out, x, g1, n2, sh2, sc2, w_r, b_r)


def _expert_kernel(be_ref, next_ref, rows_ref, nb_ref, x_ref, wg_hbm, bg_ref, wu_hbm, bu_ref, wd_hbm, bd_ref,
                   o_ref, w_f32, wg_bf, wu_bf, wd_bf, h_bf, sem, *, layer, chunk):
    i = pl.program_id(0)
    expert = be_ref[i]
    used = i < nb_ref[0]

    def weight_copies(e):
        return [pltpu.make_async_copy(src.at[layer, e], w_f32.at[j], sem.at[j])
                for j, src in enumerate((wg_hbm, wu_hbm, wd_hbm))]

    @pl.when(i == 0)
    def _():
        for cp in weight_copies(expert):
            cp.start()

    @pl.when(used & ((i == 0) | (expert != be_ref[jnp.maximum(i - 1, 0)])))
    def _():
        for cp in weight_copies(expert):
            cp.wait()
        wg_bf[...] = w_f32[0].astype(BF16)
        wu_bf[...] = w_f32[1].astype(BF16)
        wd_bf[...] = w_f32[2].astype(BF16)

        @pl.when(next_ref[i] >= 0)
        def _():
            for cp in weight_copies(next_ref[i]):
                cp.start()

    @pl.when(used)
    def _():
        xp = x_ref[...]
        row = lax.broadcasted_iota(jnp.int32, xp.shape, 0)
        xp = jnp.where(row < rows_ref[i], xp, 0)
        x = jnp.concatenate(_unpack_rows(xp), axis=1).astype(BF16)
        for c in range(wg_bf.shape[1] // chunk):
            cs = slice(c * chunk, (c + 1) * chunk)
            g = jnp.minimum(_dot(x, wg_bf[:, cs]) + bg_ref[0, 0, :, cs], SWIGLU_LIMIT)
            u = jnp.clip(_dot(x, wu_bf[:, cs]) + bu_ref[0, 0, :, cs], -SWIGLU_LIMIT, SWIGLU_LIMIT)
            h_bf[:, cs] = (g * jax.nn.sigmoid(SWIGLU_ALPHA * g) * (u + 1.0)).astype(BF16)
        o_ref[...] = _pack_rows(_dot(h_bf[...], wd_bf[...]) + bd_ref[0, 0])

    @pl.when(jnp.logical_not(used))
    def _():
        o_ref[...] = jnp.zeros_like(o_ref)


def _experts(xb, block_e, next_e, block_rows, n_used, layer, wg, bg, wu, bu, wd, bd):
    n_slots = xb.shape[0]
    depth, n_e, d, d_exp = wg.shape
    assert d == d_exp
    tm = EXPERT_TILE
    n_blocks = n_slots // tm
    xmap = lambda i, be, ne, br, nb: (jnp.minimum(i, nb[0] - 1), 0)
    bmap = lambda i, be, ne, br, nb: (layer, be[i], 0, 0)
    hbm = pl.BlockSpec(memory_space=pl.ANY)
    grid_spec = pltpu.PrefetchScalarGridSpec(
        num_scalar_prefetch=4,
        grid=(n_blocks,),
        in_specs=[
            pl.BlockSpec((tm, d // 2), xmap),
            hbm,
            pl.BlockSpec((1, 1, 1, d_exp), bmap),
            hbm,
            pl.BlockSpec((1, 1, 1, d_exp), bmap),
            hbm,
            pl.BlockSpec((1, 1, 1, d), bmap),
        ],
        out_specs=pl.BlockSpec((tm, d // 2), lambda i, be, ne, br, nb: (i, 0)),
        scratch_shapes=[pltpu.VMEM((3, d, d_exp), F32), pltpu.VMEM((d, d_exp), BF16), pltpu.VMEM((d, d_exp), BF16),
                        pltpu.VMEM((d_exp, d), BF16), pltpu.VMEM((tm, d_exp), BF16), pltpu.SemaphoreType.DMA((3,))],
    )
    return pl.pallas_call(
        functools.partial(_expert_kernel, layer=layer, chunk=256),
        out_shape=jax.ShapeDtypeStruct((n_slots, d // 2), jnp.int32),
        grid_spec=grid_spec,
        compiler_params=_params(("arbitrary",)),
        name="experts",
    )(block_e, next_e, block_rows, n_used, xb, wg, bg.reshape(depth, n_e, 1, d_exp), wu,
      bu.reshape(depth, n_e, 1, d_exp), wd, bd.reshape(depth, n_e, 1, d))


def _combine_kernel(x_ref, g2_ref, gates_ref, y_ref, o_ref):
    gates = gates_ref[0]
    half = y_ref.shape[-1]
    acc_lo = jnp.zeros((x_ref.shape[1], half), F32)
    acc_hi = jnp.zeros((x_ref.shape[1], half), F32)
    for k in range(TOP_K):
        lo, hi = _unpack_rows(y_ref[k])
        acc_lo = acc_lo + gates[:, k:k + 1] * lo
        acc_hi = acc_hi + gates[:, k:k + 1] * hi
    o_ref[0] = x_ref[0] + g2_ref[0] * jnp.concatenate([acc_lo, acc_hi], axis=1)


def _combine(x, g2, gates, y_sel, tok_offset, batches=None):
    b, t, d = x.shape
    lo, hi = batches or (0, b)
    tm = int(np.gcd(min(t, 512), tok_offset)) if tok_offset else min(t, 512)
    n_t = t // tm
    off = tok_offset // tm
    row = lambda bi, i: (bi + lo, i, 0)
    return pl.pallas_call(
        _combine_kernel,
        out_shape=jax.ShapeDtypeStruct((b, t, d), F32),
        grid=(hi - lo, n_t),
        in_specs=[
            pl.BlockSpec((1, tm, d), row),
            pl.BlockSpec((1, 1, d), lambda bi, i: (bi + lo, 0, 0)),
            pl.BlockSpec((1, tm, ROUTER_PAD), row),
            pl.BlockSpec((TOP_K, tm, d // 2), lambda bi, i: (0, off + bi * n_t + i, 0)),
        ],
        out_specs=pl.BlockSpec((1, tm, d), row),
        input_output_aliases={0: 0},
        compiler_params=_params(("parallel", "parallel")),
        name="moe_combine",
    )(x, g2, gates, y_sel)


def _route_kernel(ids_ref, upper_ref, dest_ref, counts_ref, run_ref, *, tile):
    p = pl.program_id(0)
    i = pl.program_id(1)
    tm = ids_ref.shape[0]
    ids_t = ids_ref[...].T
    expert = lax.broadcasted_iota(jnp.int32, (N_EXPERTS, tm), 0)
    chosen = [ids_t[k:k + 1, :] == expert for k in range(TOP_K)]
    picks = functools.reduce(jnp.add, [c.astype(F32) for c in chosen])
    tile_counts = jnp.sum(picks, axis=1, keepdims=True)

    @pl.when((p == 0) & (i == 0))
    def _():
        run_ref[...] = jnp.zeros_like(run_ref)

    @pl.when(p == 0)
    def _():
        run_ref[...] += tile_counts

    @pl.when((p == 1) & (i == 0))
    def _():
        counts = run_ref[...]
        counts_ref[...] = counts.astype(jnp.int32)
        padded = jnp.floor((counts + (tile - 1)) * (1.0 / tile)) * tile
        row = lax.broadcasted_iota(jnp.int32, counts.shape, 0)
        ends = padded
        for s in (1, 2, 4, 8, 16):
            ends = ends + jnp.where(row >= s, pltpu.roll(ends, s, 0), 0.0)
        run_ref[...] = ends - padded

    @pl.when(p == 1)
    def _():
        before = _dot(picks.astype(BF16), upper_ref[...])
        slot = before + run_ref[:, 0:1]
        rows = [jnp.sum(jnp.where(c, slot, 0.0), axis=0, keepdims=True) for c in chosen]
        rows += [jnp.zeros_like(rows[0])] * (dest_ref.shape[0] - TOP_K)
        dest_ref[...] = jnp.concatenate(rows, axis=0).astype(jnp.int32)
        run_ref[...] += tile_counts


def _route(ids):
    n = ids.shape[0]
    tm = int(np.gcd(n, 1024))
    n_tiles = n // tm
    upper = jnp.asarray(np.triu(np.ones((tm, tm), np.float32), 1), BF16)
    dest, counts = pl.pallas_call(
        functools.partial(_route_kernel, tile=EXPERT_TILE),
        out_shape=[jax.ShapeDtypeStruct((SUBLANES, n), jnp.int32),
                   jax.ShapeDtypeStruct((N_EXPERTS, LANES), jnp.int32)],
        grid=(2, n_tiles),
        in_specs=[pl.BlockSpec((tm, ROUTER_PAD), lambda p, i: (i, 0)),
                  pl.BlockSpec((tm, tm), lambda p, i: (0, 0))],
        out_specs=[pl.BlockSpec((SUBLANES, tm), lambda p, i: (0, i * p)),
                   pl.BlockSpec((N_EXPERTS, LANES), lambda p, i: (0, 0))],
        scratch_shapes=[pltpu.VMEM((N_EXPERTS, LANES), F32)],
        compiler_params=_params(("arbitrary", "arbitrary")),
        name="route",
    )(ids, upper)
    return dest, counts[:, 0]


def _row_gather(table, idx):
    info = plsc.get_sparse_core_info()
    n_cores, n_workers = info.num_cores, info.num_cores * info.num_subcores
    n_rows, width = idx.shape[0], table.shape[1]
    chunk_rows = SC_GATHER_ROWS
    per_worker = n_rows // n_workers
    n_chunks = per_worker // chunk_rows
    assert per_worker * n_workers == n_rows and n_chunks * chunk_rows == per_worker and n_chunks % 2 == 0
    mesh = plsc.VectorSubcoreMesh(core_axis_name="c", subcore_axis_name="s")

    @functools.partial(
        pl.kernel, mesh=mesh,
        out_type=jax.ShapeDtypeStruct((n_rows, width), table.dtype),
        scratch_types=[pltpu.VMEM((n_chunks, chunk_rows), jnp.int32), pltpu.VMEM((2, chunk_rows, width), table.dtype),
                       pltpu.SemaphoreType.DMA((2,)), pltpu.SemaphoreType.DMA((2,))],
    )
    def gather_kernel(table_hbm, idx_hbm, out_hbm, idx_v, rows_v, gather_sem, write_sem):
        worker = lax.axis_index("s") * n_cores + lax.axis_index("c")
        pltpu.sync_copy(idx_hbm.at[worker], idx_v)

        def gather(c, slot):
            return pltpu.make_async_copy(table_hbm.at[idx_v.at[c]], rows_v.at[slot], gather_sem.at[slot])

        def write(c, slot):
            first_row = worker * per_worker + c * chunk_rows
            return pltpu.make_async_copy(rows_v.at[slot], out_hbm.at[pl.ds(first_row, chunk_rows)], write_sem.at[slot])

        gather(0, 0).start()

        @pl.loop(0, n_chunks, step=2)
        def _(c0):
            for slot in (0, 1):
                c = c0 + slot
                gather(c, slot).wait()

                @pl.when(c >= 1)
                def _():
                    write(c - 1, 1 - slot).wait()

                @pl.when(c + 1 < n_chunks)
                def _():
                    gather(c + 1, 1 - slot).start()

                write(c, slot).start()

        write(n_chunks - 1, 1).wait()

    return gather_kernel(table, idx.reshape(n_workers, n_chunks, chunk_rows))


def _row_scatter(rows, dest, n_out):
    info = plsc.get_sparse_core_info()
    n_cores, n_workers = info.num_cores, info.num_cores * info.num_subcores
    n_choices, n_rows = dest.shape
    width = rows.shape[1]
    chunk_rows = SC_GATHER_ROWS
    per_worker = n_rows // (n_workers * chunk_rows)
    assert per_worker * n_workers * chunk_rows == n_rows == rows.shape[0]
    idx = dest.reshape(n_choices, n_workers, per_worker, chunk_rows).transpose(1, 0, 2, 3)
    mesh = plsc.VectorSubcoreMesh(core_axis_name="c", subcore_axis_name="s")

    @functools.partial(
        pl.kernel, mesh=mesh,
        out_type=jax.ShapeDtypeStruct((n_out, width), rows.dtype),
        scratch_types=[pltpu.VMEM((n_choices, per_worker, chunk_rows), jnp.int32),
                       pltpu.VMEM((chunk_rows, width), rows.dtype), pltpu.SemaphoreType.DMA],
    )
    def scatter_kernel(rows_hbm, idx_hbm, out_hbm, idx_v, rows_v, sem):
        worker = lax.axis_index("s") * n_cores + lax.axis_index("c")
        pltpu.sync_copy(idx_hbm.at[worker], idx_v)

        @pl.loop(0, per_worker)
        def _(c):
            first_row = (worker * per_worker + c) * chunk_rows
            pltpu.sync_copy(rows_hbm.at[pl.ds(first_row, chunk_rows)], rows_v)
            copies = [pltpu.async_copy(rows_v, out_hbm.at[idx_v.at[k, c]], sem) for k in range(n_choices)]
            for cp in copies:
                cp.wait()

    return scatter_kernel(rows, idx)


def _moe(h2, ids, layer, w_exp):
    n = h2.shape[0]
    nk = n * TOP_K
    tm = EXPERT_TILE
    dest, counts = _route(ids)
    dest = dest[:TOP_K]
    padded = (counts + tm - 1) // tm * tm
    pends = jnp.cumsum(padded)
    n_blocks = -(-nk // tm) + N_EXPERTS
    n_slots = n_blocks * tm
    block_start = jnp.arange(n_blocks, dtype=jnp.int32) * tm
    block_e = jnp.minimum(jnp.sum((pends[None] <= block_start[:, None]).astype(jnp.int32), axis=1), N_EXPERTS - 1)
    is_e = (block_e[:, None] == jnp.arange(N_EXPERTS, dtype=jnp.int32)[None]).astype(jnp.int32)
    block_rows = jnp.clip(jnp.sum(is_e * (pends - padded + counts)[None], axis=1) - block_start, 0, tm)
    n_used = (pends[-1] // tm).astype(jnp.int32).reshape(1)
    e_ids = jnp.arange(N_EXPERTS, dtype=jnp.int32)
    later = (e_ids[None] > e_ids[:, None]) & (counts[None] > 0)
    next_nonempty = jnp.min(jnp.where(later, e_ids[None], N_EXPERTS), axis=1)
    next_nonempty = jnp.where(next_nonempty == N_EXPERTS, -1, next_nonempty)
    next_e = jnp.sum(is_e * next_nonempty[None], axis=1)
    xb = _row_scatter(h2, dest, n_slots)
    return _experts(xb, block_e, next_e, block_rows, n_used, layer, *w_exp), dest


def _gather_choices(yb, dest, lo, hi):
    return _row_gather(yb, dest[:, lo:hi].reshape(-1)).reshape(TOP_K, hi - lo, -1)


def _split_batch(b, t, lead):
    unit = 2 * SC_GATHER_ROWS * SC_WORKERS // TOP_K
    fits = [bs for bs in range(1, b) if (lead + bs * t) % unit == 0 and ((b - bs) * t) % unit == 0]
    return min(fits, key=lambda bs: abs(lead + bs * t - (b - bs) * t)) if fits else b


def _mod_parts(mod_l, b):
    d = mod_l.shape[-1] // 6
    lat = [mod_l[:b, k * d:(k + 1) * d].reshape(b, 1, d) for k in range(6)]
    ctx = [jnp.broadcast_to(mod_l[b, k * d:(k + 1) * d].reshape(1, 1, d), (b, 1, d)) for k in range(6)]
    return lat, ctx


def kernel(x, c, ctx, c_ctx, ada_w, ada_b, norm1_g, norm2_g, ev_w_in, ev_w_out, ev_q_gain, ev_k_gain, ev_rpb, ev_conv_w, ev_conv_b, od_w_in, od_w_out, od_conv_w, od_conv_b, od_fwd_wa, od_fwd_ba, od_fwd_wx, od_fwd_bx, od_fwd_lam, od_bwd_wa, od_bwd_ba, od_bwd_wx, od_bwd_bx, od_bwd_lam, router_w, router_b, exp_w_gate, exp_b_gate, exp_w_up, exp_b_up, exp_w_down, exp_b_down):
    b, t, d = x.shape
    l = ctx.shape[1]
    assert ada_w.shape[0] == DEPTH == 2 and t % GRID_W == 0 and t // GRID_W >= WIN_H

    n_rows_c = -(-(b + 1) // SUBLANES) * SUBLANES
    cvec = jnp.zeros((n_rows_c, d), F32).at[:b].set(c).at[b].set(c_ctx)
    mod = _ada_mod(cvec, ada_w, ada_b)

    def router(layer):
        w_r = jnp.zeros((d, ROUTER_PAD), F32).at[:, :N_EXPERTS].set(router_w[layer]).astype(BF16)
        b_r = jnp.zeros((1, ROUTER_PAD), F32).at[0, :N_EXPERTS].set(router_b[layer])
        return w_r, b_r

    w_exp = (exp_w_gate, exp_b_gate, exp_w_up, exp_b_up, exp_w_down, exp_b_down)

    (sh1, sc1, g1, sh2, sc2, g2), (csh1, csc1, cg1, csh2, csc2, cg2) = _mod_parts(mod[0], b)
    n1 = norm1_g[0].reshape(1, d)
    n2 = norm2_g[0].reshape(1, d)
    w_in = ev_w_in[0].astype(BF16)
    w_out = ev_w_out[0].astype(BF16)
    q_scale = NA_HEAD_DIM ** -0.5 * LOG2_E
    head_gain = jnp.stack([jnp.tile(ev_q_gain[0] * q_scale, NA_HEADS), jnp.tile(ev_k_gain[0], NA_HEADS)])
    head_gain = head_gain.reshape(2, 1, NA_WIDTH).astype(F32)
    proj = _inproj(x, n1, sh1, sc1, w_in, head_gain)
    proj_c = _inproj(ctx, n1, csh1, csc1, w_in, head_gain)
    kh, r0, row_type, patterns = _na_tables(t // GRID_W)
    bias_tab = _na_bias_table(ev_rpb[0], patterns)
    o_a = _neighbourhood_attention(proj, proj_c, bias_tab, r0, row_type, kh)
    oc_a = _context_attention(proj_c)
    w_r, b_r = router(0)
    x1, h2, ids, gates = _even_out(o_a, proj, ev_conv_w[0], ev_conv_b[0], w_out, x, g1, n2, sh2, sc2, w_r, b_r)
    c1, h2c, ids_c, gates_c = _even_out(oc_a, proj_c, ev_conv_w[0], ev_conv_b[0], w_out, ctx, cg1, n2, csh2, csc2,
                                        w_r, b_r)
    tokens = jnp.concatenate([h2c.reshape(b * l, d // 2), h2.reshape(b * t, d // 2)], axis=0)
    ids_all = jnp.concatenate([ids_c.reshape(b * l, ROUTER_PAD), ids.reshape(b * t, ROUTER_PAD)], axis=0)
    yb, dest = _moe(tokens, ids_all, 0, w_exp)
    bs = _split_batch(b, t, b * l)
    n_first = b * l + bs * t
    y_first = _gather_choices(yb, dest, 0, n_first)
    hctx = _combine(c1, cg2, gates_c, y_first, 0)
    x = _combine(x1, g2, gates, y_first, b * l, (0, bs))
    if bs < b:
        x = _combine(x, g2, gates, _gather_choices(yb, dest, n_first, b * (l + t)), 0, (bs, b))

    (sh1, sc1, g1, sh2, sc2, g2), (csh1, csc1, _, _, _, _) = _mod_parts(mod[1], b)
    n1 = norm1_g[1].reshape(1, d)
    n2 = norm2_g[1].reshape(1, d)
    w_in = od_w_in[0].astype(BF16)
    width = w_in.shape[1] // 2
    proj = _inproj(x, n1, sh1, sc1, w_in)
    u_ctx = _inproj(hctx, n1, csh1, csc1, w_in[:, width:])
    h_dir = []
    for reverse, (wa, ba, wx, bx, lam) in ((False, (od_fwd_wa, od_fwd_ba, od_fwd_wx, od_fwd_bx, od_fwd_lam)),
                                           (True, (od_bwd_wa, od_bwd_ba, od_bwd_wx, od_bwd_bx, od_bwd_lam))):
        w_cat = jnp.concatenate([wa[0], wx[0]], axis=-1).astype(BF16)
        h_dir.append(_lru_scan(proj, u_ctx, od_conv_w[0], od_conv_b[0], w_cat, ba[0], bx[0], lam[0], reverse))
    w_r, b_r = router(1)
    x1, h2, ids, gates = _odd_out(h_dir[0], h_dir[1], proj, od_w_out[0].astype(BF16), x, g1, n2, sh2, sc2, w_r, b_r)
    yb, dest = _moe(h2.reshape(b * t, d // 2), ids.reshape(b * t, ROUTER_PAD), 1, w_exp)
    bs = _split_batch(b, t, 0)
    x = _combine(x1, g2, gates, _gather_choices(yb, dest, 0, bs * t), 0, (0, bs))
    if bs < b:
        x = _combine(x, g2, gates, _gather_choices(yb, dest, bs * t, b * t), 0, (bs, b))
    return x
```

```python
import functools

import numpy as np
import jax
import jax.numpy as jnp
from jax import lax
from jax.experimental import pallas as pl
from jax.experimental.pallas import tpu as pltpu
from jax.experimental.pallas import tpu_sc as plsc

DEPTH = 2
GRID_W = 64
EPS = 1e-6
NEG_INF = -1e30
LOG2_E = 1.4426950408889634
NA_HEADS = 8
NA_HEAD_DIM = 64
NA_WIDTH = NA_HEADS * NA_HEAD_DIM
HEAD_PAIRS = NA_HEADS // 2
WIN_H = 8
WIN_W = 16
SC_CONV = 3
LRU_BLOCKS = 4
LRU_CONV = 4
LRU_C = 8.0
N_EXPERTS = 32
TOP_K = 4
SWIGLU_LIMIT = 7.0
SWIGLU_ALPHA = 1.702

LANES = 128
SUBLANES = 8
HALO = 16
ROUTER_PAD = LANES
NA_ROWS_PER_STEP = 4
OUT_TILE = 1024
OUT_SUB_TILE = 512
EXPERT_TILE = 512
SC_GATHER_ROWS = 64
SC_WORKERS = 32
VMEM_LIMIT = 56 * 1024 * 1024

F32 = jnp.float32
BF16 = jnp.bfloat16


def _params(sem, vmem=VMEM_LIMIT):
    return pltpu.CompilerParams(dimension_semantics=sem, vmem_limit_bytes=vmem)


def _dot(a, b):
    return jnp.dot(a, b, preferred_element_type=F32)


def _dot_nt(a, b):
    return lax.dot_general(a, b, (((1,), (1,)), ((), ())), preferred_element_type=F32)


def _pack_rows(v):
    w = v.shape[-1] // 2
    lo = lax.bitcast_convert_type(v[:, :w].astype(BF16).astype(F32), jnp.int32)
    hi = lax.bitcast_convert_type(v[:, w:].astype(BF16).astype(F32), jnp.int32)
    return lax.shift_right_logical(lo, 16) | (hi & jnp.int32(-65536))


def _unpack_rows(p):
    lo = lax.bitcast_convert_type(lax.shift_left(p, 16), F32)
    hi = lax.bitcast_convert_type(p & jnp.int32(-65536), F32)
    return lo, hi


def _rms_mod(x, g, shift, scale):
    ms = jnp.mean(x * x, axis=-1, keepdims=True)
    y = x * lax.rsqrt(ms + EPS) * g
    return y * (1.0 + scale) + shift


def _ada_kernel(c_ref, w_ref, b_ref, o_ref):
    c = c_ref[...]
    s = (c * jax.nn.sigmoid(c)).astype(BF16)
    o_ref[0] = _dot(s, w_ref[0].astype(BF16)) + b_ref[0]


def _ada_mod(cvec, ada_w, ada_b):
    depth, d, n = ada_w.shape
    r = cvec.shape[0]
    tn = 1536
    return pl.pallas_call(
        _ada_kernel,
        out_shape=jax.ShapeDtypeStruct((depth, r, n), F32),
        grid=(depth, n // tn),
        in_specs=[
            pl.BlockSpec((r, d), lambda l, j: (0, 0)),
            pl.BlockSpec((1, d, tn), lambda l, j: (l, 0, j)),
            pl.BlockSpec((1, 1, tn), lambda l, j: (l, 0, j)),
        ],
        out_specs=pl.BlockSpec((1, r, tn), lambda l, j: (l, 0, j)),
        compiler_params=_params(("parallel", "parallel")),
        name="ada_mod",
    )(cvec, ada_w, ada_b.reshape(depth, 1, n))


def _inproj_kernel(x_ref, g_ref, sh_ref, sc_ref, w_ref, hg_ref, ones_ref, o_ref, *, n_tiles, tn, n_headnorm):
    h = _rms_mod(x_ref[0], g_ref[...], sh_ref[0], sc_ref[0]).astype(BF16)
    for j in range(n_tiles):
        y = _dot(h, w_ref[:, j * tn:(j + 1) * tn])
        if j < n_headnorm:
            ms = _dot((y * y).astype(BF16), ones_ref[...]) * (1.0 / NA_HEAD_DIM)
            y = y * lax.rsqrt(ms + EPS) * hg_ref[j]
        o_ref[0, :, j * tn:(j + 1) * tn] = y.astype(o_ref.dtype)


def _inproj(x, g, shift, scale, w, head_gain=None):
    b, t, d = x.shape
    n = w.shape[1]
    tn = NA_WIDTH
    tm = min(t, 512)
    n_headnorm = 0 if head_gain is None else head_gain.shape[0]
    if head_gain is None:
        head_gain = jnp.ones((1, 1, tn), F32)
    hid = np.arange(tn) // NA_HEAD_DIM
    ones_bd = jnp.asarray((hid[:, None] == hid[None, :]), BF16)
    kern = functools.partial(_inproj_kernel, n_tiles=n // tn, tn=tn, n_headnorm=n_headnorm)
    return pl.pallas_call(
        kern,
        out_shape=jax.ShapeDtypeStruct((b, t, n), BF16),
        grid=(b, t // tm),
        in_specs=[
            pl.BlockSpec((1, tm, d), lambda bi, i: (bi, i, 0)),
            pl.BlockSpec((1, d), lambda bi, i: (0, 0)),
            pl.BlockSpec((1, 1, d), lambda bi, i: (bi, 0, 0)),
            pl.BlockSpec((1, 1, d), lambda bi, i: (bi, 0, 0)),
            pl.BlockSpec((d, n), lambda bi, i: (0, 0)),
            pl.BlockSpec(head_gain.shape, lambda bi, i: (0, 0, 0)),
            pl.BlockSpec((tn, tn), lambda bi, i: (0, 0)),
        ],
        out_specs=pl.BlockSpec((1, tm, n), lambda bi, i: (bi, i, 0)),
        compiler_params=_params(("parallel", "parallel")),
        name="inproj",
    )(x, g, shift, scale, w, head_gain, ones_bd)


def _na_tables(rows):
    kh = min(WIN_H, rows)
    r = np.arange(rows)
    r0 = np.clip(r - kh // 2, 0, rows - kh)
    dr = r0[:, None] + np.arange(kh)[None] - r[:, None] + WIN_H - 1
    patterns, row_type = np.unique(dr, axis=0, return_inverse=True)
    return kh, r0.astype(np.int32), row_type.reshape(-1).astype(np.int32), patterns


def _na_bias_table(rpb, patterns):
    qc = np.arange(GRID_W)
    kc = np.arange(GRID_W)
    c0 = np.clip(qc - WIN_W // 2, 0, GRID_W - WIN_W)[:, None]
    valid = (kc[None] >= c0) & (kc[None] < c0 + WIN_W)
    dc = np.clip(kc[None] - qc[:, None] + WIN_W - 1, 0, 2 * WIN_W - 2)
    n_pat, kh = patterns.shape
    onehot_dc = jnp.asarray(dc[None] == np.arange(2 * WIN_W - 1)[:, None, None], F32)
    tab = jnp.einsum('hpic,cqk->hpiqk', rpb.astype(F32)[:, patterns], onehot_dc,
                     precision=lax.Precision.HIGHEST)
    tab = jnp.where(valid[None, None, None], tab * LOG2_E, NEG_INF)
    tab = tab.reshape(HEAD_PAIRS, 2, n_pat, kh, GRID_W, GRID_W)
    tab = tab.transpose(2, 0, 1, 4, 3, 5)
    return tab.reshape(n_pat, HEAD_PAIRS, 2 * GRID_W, kh * GRID_W)


def _pair_attention(q, keys, values, biases):
    m = q.shape[0]
    lane = lax.broadcasted_iota(jnp.int32, q.shape, 1)
    zero = jnp.zeros_like(q)
    qq = jnp.concatenate([jnp.where(lane < NA_HEAD_DIM, q, zero), jnp.where(lane >= NA_HEAD_DIM, q, zero)], axis=0)
    scores = []
    for k, bias in zip(keys, biases):
        s = _dot_nt(qq, k)
        scores.append(s if bias is None else s + bias)
    s = jnp.concatenate(scores, axis=1)
    e = jnp.exp2(s - jnp.max(s, axis=-1, keepdims=True))
    denom = jnp.sum(e, axis=-1, keepdims=True)
    e = e.astype(BF16)
    o, start = 0.0, 0
    for v in values:
        o = o + _dot(e[:, start:start + v.shape[0]], v)
        start += v.shape[0]
    o = o * (1.0 / denom)
    lane_o = lax.broadcasted_iota(jnp.int32, (m, LANES), 1)
    return jnp.where(lane_o < NA_HEAD_DIM, o[:m], o[m:])


def _na_kernel(r0_ref, type_ref, q_ref, k_ref, v_ref, kc_ref, vc_ref, bias_ref, o_ref, *, kh, rows_per_step):
    for j in range(rows_per_step):
        r = pl.program_id(1) * rows_per_step + j
        start = pl.multiple_of(r0_ref[r] * GRID_W, GRID_W)
        rtype = type_ref[r]
        rows = slice(j * GRID_W, (j + 1) * GRID_W)
        for p in range(HEAD_PAIRS):
            cols = slice(p * LANES, (p + 1) * LANES)
            o = _pair_attention(
                q_ref[0, rows, cols],
                [k_ref[0, pl.ds(start, kh * GRID_W), cols], kc_ref[0, :, cols]],
                [v_ref[0, pl.ds(start, kh * GRID_W), cols], vc_ref[0, :, cols]],
                [bias_ref[rtype, p], None],
            )
            o_ref[0, rows, cols] = o.astype(o_ref.dtype)


def _neighbourhood_attention(proj, proj_c, bias_tab, r0, row_type, kh):
    b, t, _ = proj.shape
    l = proj_c.shape[1]
    rows = t // GRID_W
    w = NA_WIDTH
    rps = int(np.gcd(rows, NA_ROWS_PER_STEP))
    q_rows = rps * GRID_W
    grid_spec = pltpu.PrefetchScalarGridSpec(
        num_scalar_prefetch=2,
        grid=(b, rows // rps),
        in_specs=[
            pl.BlockSpec((1, q_rows, w), lambda bi, r, *_: (bi, r, 0)),
            pl.BlockSpec((1, t, w), lambda bi, r, *_: (bi, 0, 1)),
            pl.BlockSpec((1, t, w), lambda bi, r, *_: (bi, 0, 2)),
            pl.BlockSpec((1, l, w), lambda bi, r, *_: (bi, 0, 1)),
            pl.BlockSpec((1, l, w), lambda bi, r, *_: (bi, 0, 2)),
            pl.BlockSpec(bias_tab.shape, lambda bi, r, *_: (0, 0, 0, 0)),
        ],
        out_specs=pl.BlockSpec((1, q_rows, w), lambda bi, r, *_: (bi, r, 0)),
    )
    return pl.pallas_call(
        functools.partial(_na_kernel, kh=kh, rows_per_step=rps),
        out_shape=jax.ShapeDtypeStruct((b, t, w), BF16),
        grid_spec=grid_spec,
        compiler_params=_params(("parallel", "arbitrary")),
        name="na_attention",
    )(jnp.asarray(r0), jnp.asarray(row_type), proj, proj, proj, proj_c, proj_c, bias_tab)


def _ctx_attn_kernel(q_ref, k_ref, v_ref, o_ref):
    for p in range(HEAD_PAIRS):
        cols = slice(p * LANES, (p + 1) * LANES)
        o = _pair_attention(q_ref[0, :, cols], [k_ref[0, :, cols]], [v_ref[0, :, cols]], [None])
        o_ref[0, :, cols] = o.astype(o_ref.dtype)


def _context_attention(proj_c):
    b, l, _ = proj_c.shape
    w = NA_WIDTH
    return pl.pallas_call(
        _ctx_attn_kernel,
        out_shape=jax.ShapeDtypeStruct((b, l, w), BF16),
        grid=(b,),
        in_specs=[pl.BlockSpec((1, l, w), lambda bi, j=j: (bi, 0, j)) for j in range(3)],
        out_specs=pl.BlockSpec((1, l, w), lambda bi: (bi, 0, 0)),
        compiler_params=_params(("parallel",)),
        name="ctx_attention",
    )(proj_c, proj_c, proj_c)


def _top4(logits):
    lane = lax.broadcasted_iota(jnp.int32, logits.shape, 1)
    cur = jnp.where(lane < N_EXPERTS, logits, -jnp.inf)
    vals, idxs = [], []
    for _ in range(TOP_K):
        m = jnp.max(cur, axis=-1, keepdims=True)
        first = jnp.min(jnp.where(cur == m, lane, ROUTER_PAD).astype(F32), axis=-1, keepdims=True)
        idx = first.astype(jnp.int32)
        vals.append(m)
        idxs.append(idx)
        cur = jnp.where(lane == idx, -jnp.inf, cur)
    exps = [jnp.exp(v - vals[0]) for v in vals]
    inv = 1.0 / functools.reduce(jnp.add, exps)
    ids = jnp.zeros(logits.shape, jnp.int32)
    gates = jnp.zeros(logits.shape, F32)
    for k in range(TOP_K):
        ids = jnp.where(lane == k, idxs[k], ids)
        gates = jnp.where(lane == k, exps[k] * inv, gates)
    return ids, gates


def _layer_tail(y, rows, x_ref, g1_ref, n2_ref, sh2_ref, sc2_ref, wr_ref, br_ref, xo_ref, h2_ref, ids_ref, gates_ref):
    x_new = x_ref[0, rows] + g1_ref[0] * y
    xo_ref[0, rows] = x_new
    h2 = _rms_mod(x_new, n2_ref[...], sh2_ref[0], sc2_ref[0])
    h2_ref[0, rows] = _pack_rows(h2)
    logits = _dot(h2.astype(BF16), wr_ref[...]) + br_ref[...]
    ids, gates = _top4(logits)
    ids_ref[0, rows] = ids
    gates_ref[0, rows] = gates


def _sub_tiles(n_rows):
    sub = min(OUT_SUB_TILE, n_rows)
    return [slice(s, s + sub) for s in range(0, n_rows, sub)]


def _tail_specs(b, t, d, tm):
    row = lambda bi, i: (bi, i, 0)
    per_b = lambda bi, i: (bi, 0, 0)
    const = lambda bi, i: (0, 0)
    in_specs = [
        pl.BlockSpec((1, tm, d), row),
        pl.BlockSpec((1, 1, d), per_b),
        pl.BlockSpec((1, d), const),
        pl.BlockSpec((1, 1, d), per_b),
        pl.BlockSpec((1, 1, d), per_b),
        pl.BlockSpec((d, ROUTER_PAD), const),
        pl.BlockSpec((1, ROUTER_PAD), const),
    ]
    out_specs = [
        pl.BlockSpec((1, tm, d), row),
        pl.BlockSpec((1, tm, d // 2), row),
        pl.BlockSpec((1, tm, ROUTER_PAD), row),
        pl.BlockSpec((1, tm, ROUTER_PAD), row),
    ]
    out_shape = [
        jax.ShapeDtypeStruct((b, t, d), F32),
        jax.ShapeDtypeStruct((b, t, d // 2), jnp.int32),
        jax.ShapeDtypeStruct((b, t, ROUTER_PAD), jnp.int32),
        jax.ShapeDtypeStruct((b, t, ROUTER_PAD), F32),
    ]
    return in_specs, out_specs, out_shape


def _halo_fix(rolled, at_row, halo_row, present):
    n = rolled.shape[0]
    first = at_row < SUBLANES
    assert first or at_row >= n - SUBLANES
    slab = rolled[:SUBLANES] if first else rolled[n - SUBLANES:]
    sub = lax.broadcasted_iota(jnp.int32, slab.shape, 0)
    fill = jnp.where(present, halo_row, jnp.zeros_like(halo_row))
    slab = jnp.where(sub == at_row % SUBLANES, fill, slab)
    return jnp.concatenate([slab, rolled[SUBLANES:]] if first else [rolled[:n - SUBLANES], slab], axis=0)


def _even_out_kernel(oa_ref, bg_ref, cg_ref, xin_ref, cgp_ref, xinp_ref, cgn_ref, xinn_ref, cw_ref, cb_ref,
                     wa_ref, wb_ref, *tail_refs, tm):
    i = pl.program_id(1)
    has_prev = i > 0
    has_next = i < pl.num_programs(1) - 1
    u = cg_ref[0].astype(F32) * xin_ref[0].astype(F32)
    u_prev = (cgp_ref[0].astype(F32) * xinp_ref[0].astype(F32))[HALO - 1:HALO]
    u_next = (cgn_ref[0].astype(F32) * xinn_ref[0].astype(F32))[0:1]
    u_m1 = _halo_fix(pltpu.roll(u, 1, 0), 0, u_prev, has_prev)
    u_p1 = _halo_fix(pltpu.roll(u, tm - 1, 0), tm - 1, u_next, has_next)
    cw = cw_ref[...]
    conv = u_m1 * cw[0:1] + u * cw[1:2] + u_p1 * cw[2:3] + cb_ref[...]
    o_b = (bg_ref[0].astype(F32) * conv).astype(BF16)
    for rows in _sub_tiles(tm):
        y = _dot(oa_ref[0, rows], wa_ref[...]) + _dot(o_b[rows], wb_ref[...])
        _layer_tail(y, rows, *tail_refs)


def _even_out(o_a, proj, conv_w, conv_b, w_out, x, g1, n2, sh2, sc2, w_r, b_r):
    b, t, d = x.shape
    w = NA_WIDTH
    tm = min(t, OUT_TILE)
    hb = tm // HALO
    n_hblocks = t // HALO
    row = lambda bi, i: (bi, i, 0)
    const = lambda bi, i: (0, 0)
    prev = lambda col: (lambda bi, i: (bi, jnp.maximum(i * hb - 1, 0), col))
    nxt = lambda col: (lambda bi, i: (bi, jnp.minimum((i + 1) * hb, n_hblocks - 1), col))
    tail_in, out_specs, out_shape = _tail_specs(b, t, d, tm)
    in_specs = [
        pl.BlockSpec((1, tm, w), row),
        pl.BlockSpec((1, tm, w), lambda bi, i: (bi, i, 3)),
        pl.BlockSpec((1, tm, w), lambda bi, i: (bi, i, 4)),
        pl.BlockSpec((1, tm, w), lambda bi, i: (bi, i, 5)),
        pl.BlockSpec((1, HALO, w), prev(4)),
        pl.BlockSpec((1, HALO, w), prev(5)),
        pl.BlockSpec((1, HALO, w), nxt(4)),
        pl.BlockSpec((1, HALO, w), nxt(5)),
        pl.BlockSpec((SC_CONV, w), const),
        pl.BlockSpec((1, w), const),
        pl.BlockSpec((w, d), const),
        pl.BlockSpec((w, d), const),
    ] + tail_in
    return pl.pallas_call(
        functools.partial(_even_out_kernel, tm=tm),
        out_shape=out_shape,
        grid=(b, t // tm),
        in_specs=in_specs,
        out_specs=out_specs,
        compiler_params=_params(("parallel", "parallel")),
        name="even_out",
    )(o_a, proj, proj, proj, proj, proj, proj, proj, conv_w, conv_b.reshape(1, w),
      w_out[:w], w_out[w:], x, g1, n2, sh2, sc2, w_r, b_r)


def _log_sigmoid(x):
    return jnp.minimum(x, 0.0) - jnp.log1p(jnp.exp(-jnp.abs(x)))


def _lru_tile(u_ref, up_ref, un_ref, has_prev, has_next, cw_ref, cb_ref, w_ref, ba_ref, bx_ref, lam_ref,
              carry_ref, o_ref, *, ts, reverse):
    width = u_ref.shape[-1]
    blk = width // LRU_BLOCKS
    u = u_ref[0].astype(F32)
    up = up_ref[0].astype(F32)
    un = un_ref[0].astype(F32)
    u_m1 = _halo_fix(pltpu.roll(u, 1, 0), 0, up[HALO - 1:HALO], has_prev)
    u_p1 = _halo_fix(pltpu.roll(u, ts - 1, 0), ts - 1, un[0:1], has_next)
    u_p2 = _halo_fix(pltpu.roll(u, ts - 2, 0), ts - 2, un[0:1], has_next)
    u_p2 = _halo_fix(u_p2, ts - 1, un[1:2], has_next)
    cw = cw_ref[...]
    uc = u_m1 * cw[0:1] + u * cw[1:2] + u_p1 * cw[2:3] + u_p2 * cw[3:4] + cb_ref[...]
    ucb = uc.astype(BF16)
    za, zx = [], []
    for h in range(LRU_BLOCKS):
        z = _dot(ucb[:, h * blk:(h + 1) * blk], w_ref[h])
        za.append(z[:, :blk])
        zx.append(z[:, blk:])
    r = jax.nn.sigmoid(jnp.concatenate(za, axis=1) + ba_ref[...])
    gate_i = jax.nn.sigmoid(jnp.concatenate(zx, axis=1) + bx_ref[...])
    log_a = (LRU_C * _log_sigmoid(lam_ref[...])) * r
    a = jnp.exp(log_a)
    th = jnp.tanh(log_a)
    m2 = -2.0 * th / (1.0 - th)
    mult = jnp.where(m2 > 0.0, m2 * lax.rsqrt(m2), 0.0)
    bcoef = mult * gate_i * uc

    groups = ts // SUBLANES
    a3 = a.reshape(groups, SUBLANES, width)
    b3 = bcoef.reshape(groups, SUBLANES, width)
    sub = lax.broadcasted_iota(jnp.int32, a3.shape, 1)
    for s in (1, 2, 4):
        shift = SUBLANES - s if reverse else s
        a_sh = pltpu.roll(a3, shift, 1)
        b_sh = pltpu.roll(b3, shift, 1)
        m = (sub < SUBLANES - s) if reverse else (sub >= s)
        b3 = jnp.where(m, a3 * b_sh + b3, b3)
        a3 = jnp.where(m, a3 * a_sh, a3)

    h = carry_ref[0:1, :]
    order = range(groups - 1, -1, -1) if reverse else range(groups)
    edge = 0 if reverse else SUBLANES - 1
    for g in order:
        hg = a3[g] * h + b3[g]
        if o_ref is not None:
            o_ref[0, g * SUBLANES:(g + 1) * SUBLANES, :] = hg.astype(o_ref.dtype)
        h = hg[edge:edge + 1, :]
    carry_ref[...] = jnp.broadcast_to(h, carry_ref.shape)


def _lru_kernel(uc_ref, ucp_ref, ucn_ref, ul_ref, ulp_ref, uln_ref, cw_ref, cb_ref, w_ref, ba_ref, bx_ref,
                lam_ref, o_ref, carry_ref, *, ts, n_ctx_tiles, n_lat_tiles, reverse):
    j = pl.program_id(1)
    shared = (cw_ref, cb_ref, w_ref, ba_ref, bx_ref, lam_ref, carry_ref)

    @pl.when(j == 0)
    def _():
        carry_ref[...] = jnp.zeros_like(carry_ref)

    def tile_pos(step, n):
        return (n - 1 - step) if reverse else step

    @pl.when(j < n_ctx_tiles)
    def _():
        pos = tile_pos(j, n_ctx_tiles)
        _lru_tile(uc_ref, ucp_ref, ucn_ref, pos > 0, pos < n_ctx_tiles - 1, *shared, None, ts=ts, reverse=reverse)

    @pl.when(j >= n_ctx_tiles)
    def _():
        pos = tile_pos(j - n_ctx_tiles, n_lat_tiles)
        _lru_tile(ul_ref, ulp_ref, uln_ref, pos > 0, pos < n_lat_tiles - 1, *shared, o_ref, ts=ts, reverse=reverse)


def _lru_scan(proj, u_ctx, conv_w, conv_b, w_cat, ba, bx, lam, reverse):
    b, t, _ = proj.shape
    l, width = u_ctx.shape[1], u_ctx.shape[2]
    ts = min(256, l, t)
    n_c, n_l = l // ts, t // ts
    hb = ts // HALO

    def pos_of(step, n):
        step = jnp.clip(step, 0, n - 1)
        return (n - 1 - step) if reverse else step

    def tile_map(off, n, col):
        return lambda bi, j: (bi, pos_of(j - off, n), col)

    def prev_map(off, n, col):
        return lambda bi, j: (bi, jnp.maximum(pos_of(j - off, n) * hb - 1, 0), col)

    def next_map(off, n, col):
        return lambda bi, j: (bi, jnp.minimum((pos_of(j - off, n) + 1) * hb, n * hb - 1), col)

    const2 = lambda bi, j: (0, 0)
    in_specs = [
        pl.BlockSpec((1, ts, width), tile_map(0, n_c, 0)),
        pl.BlockSpec((1, HALO, width), prev_map(0, n_c, 0)),
        pl.BlockSpec((1, HALO, width), next_map(0, n_c, 0)),
        pl.BlockSpec((1, ts, width), tile_map(n_c, n_l, 1)),
        pl.BlockSpec((1, HALO, width), prev_map(n_c, n_l, 1)),
        pl.BlockSpec((1, HALO, width), next_map(n_c, n_l, 1)),
        pl.BlockSpec((LRU_CONV, width), const2),
        pl.BlockSpec((1, width), const2),
        pl.BlockSpec(w_cat.shape, lambda bi, j: (0, 0, 0)),
        pl.BlockSpec((1, width), const2),
        pl.BlockSpec((1, width), const2),
        pl.BlockSpec((1, width), const2),
    ]
    kern = functools.partial(_lru_kernel, ts=ts, n_ctx_tiles=n_c, n_lat_tiles=n_l, reverse=reverse)
    return pl.pallas_call(
        kern,
        out_shape=jax.ShapeDtypeStruct((b, t, width), BF16),
        grid=(b, n_c + n_l),
        in_specs=in_specs,
        out_specs=pl.BlockSpec((1, ts, width), tile_map(n_c, n_l, 0)),
        scratch_shapes=[pltpu.VMEM((SUBLANES, width), F32)],
        compiler_params=_params(("parallel", "arbitrary")),
        name="lru_scan_bwd" if reverse else "lru_scan_fwd",
    )(u_ctx, u_ctx, u_ctx, proj, proj, proj, conv_w, conv_b.reshape(1, width), w_cat,
      ba.reshape(1, width), bx.reshape(1, width), lam.reshape(1, width))


def _odd_out_kernel(hf_ref, hb_ref, gate_ref, w_ref, *tail_refs):
    for rows in _sub_tiles(hf_ref.shape[1]):
        hsum = hf_ref[0, rows].astype(F32) + hb_ref[0, rows].astype(F32)
        z = hsum * jax.nn.gelu(gate_ref[0, rows].astype(F32), approximate=True)
        y = _dot(z.astype(BF16), w_ref[...])
        _layer_tail(y, rows, *tail_refs)


def _odd_out(h_f, h_b, proj, w_out, x, g1, n2, sh2, sc2, w_r, b_r):
    b, t, d = x.shape
    width = h_f.shape[-1]
    tm = min(t, OUT_TILE)
    row = lambda bi, i: (bi, i, 0)
    tail_in, out_specs, out_shape = _tail_specs(b, t, d, tm)
    in_specs = [
        pl.BlockSpec((1, tm, width), row),
        pl.BlockSpec((1, tm, width), row),
        pl.BlockSpec((1, tm, width), row),
        pl.BlockSpec((width, d), lambda bi, i: (0, 0)),
    ] + tail_in
    return pl.pallas_call(
        _odd_out_kernel,
        out_shape=out_shape,
        grid=(b, t // tm),
        in_specs=in_specs,
        out_specs=out_specs,
        compiler_params=_params(("parallel", "parallel")),
        name="odd_out",
    )(h_f, h_b, proj, w_out, x, g1, n2, sh2, sc2, w_r, b_r)


def _expert_kernel(be_ref, next_ref, rows_ref, nb_ref, x_ref, wg_hbm, bg_ref, wu_hbm, bu_ref, wd_hbm, bd_ref,
                   o_ref, w_f32, wg_bf, wu_bf, wd_bf, h_bf, sem, *, layer, chunk):
    i = pl.program_id(0)
    expert = be_ref[i]
    used = i < nb_ref[0]

    def weight_copies(e):
        return [pltpu.make_async_copy(src.at[layer, e], w_f32.at[j], sem.at[j])
                for j, src in enumerate((wg_hbm, wu_hbm, wd_hbm))]

    @pl.when(i == 0)
    def _():
        for cp in weight_copies(expert):
            cp.start()

    @pl.when(used & ((i == 0) | (expert != be_ref[jnp.maximum(i - 1, 0)])))
    def _():
        for cp in weight_copies(expert):
            cp.wait()
        wg_bf[...] = w_f32[0].astype(BF16)
        wu_bf[...] = w_f32[1].astype(BF16)
        wd_bf[...] = w_f32[2].astype(BF16)

        @pl.when(next_ref[i] >= 0)
        def _():
            for cp in weight_copies(next_ref[i]):
                cp.start()

    @pl.when(used)
    def _():
        xp = x_ref[...]
        row = lax.broadcasted_iota(jnp.int32, xp.shape, 0)
        xp = jnp.where(row < rows_ref[i], xp, 0)
        x = jnp.concatenate(_unpack_rows(xp), axis=1).astype(BF16)
        for c in range(wg_bf.shape[1] // chunk):
            cs = slice(c * chunk, (c + 1) * chunk)
            g = jnp.minimum(_dot(x, wg_bf[:, cs]) + bg_ref[0, 0, :, cs], SWIGLU_LIMIT)
            u = jnp.clip(_dot(x, wu_bf[:, cs]) + bu_ref[0, 0, :, cs], -SWIGLU_LIMIT, SWIGLU_LIMIT)
            h_bf[:, cs] = (g * jax.nn.sigmoid(SWIGLU_ALPHA * g) * (u + 1.0)).astype(BF16)
        o_ref[...] = _pack_rows(_dot(h_bf[...], wd_bf[...]) + bd_ref[0, 0])

    @pl.when(jnp.logical_not(used))
    def _():
        o_ref[...] = jnp.zeros_like(o_ref)


def _experts(xb, block_e, next_e, block_rows, n_used, layer, wg, bg, wu, bu, wd, bd):
    n_slots = xb.shape[0]
    depth, n_e, d, d_exp = wg.shape
    assert d == d_exp
    tm = EXPERT_TILE
    n_blocks = n_slots // tm
    xmap = lambda i, be, ne, br, nb: (jnp.minimum(i, nb[0] - 1), 0)
    bmap = lambda i, be, ne, br, nb: (layer, be[i], 0, 0)
    hbm = pl.BlockSpec(memory_space=pl.ANY)
    grid_spec = pltpu.PrefetchScalarGridSpec(
        num_scalar_prefetch=4,
        grid=(n_blocks,),
        in_specs=[
            pl.BlockSpec((tm, d // 2), xmap),
            hbm,
            pl.BlockSpec((1, 1, 1, d_exp), bmap),
            hbm,
            pl.BlockSpec((1, 1, 1, d_exp), bmap),
            hbm,
            pl.BlockSpec((1, 1, 1, d), bmap),
        ],
        out_specs=pl.BlockSpec((tm, d // 2), lambda i, be, ne, br, nb: (i, 0)),
        scratch_shapes=[pltpu.VMEM((3, d, d_exp), F32), pltpu.VMEM((d, d_exp), BF16), pltpu.VMEM((d, d_exp), BF16),
                        pltpu.VMEM((d_exp, d), BF16), pltpu.VMEM((tm, d_exp), BF16), pltpu.SemaphoreType.DMA((3,))],
    )
    return pl.pallas_call(
        functools.partial(_expert_kernel, layer=layer, chunk=256),
        out_shape=jax.ShapeDtypeStruct((n_slots, d // 2), jnp.int32),
        grid_spec=grid_spec,
        compiler_params=_params(("arbitrary",)),
        name="experts",
    )(block_e, next_e, block_rows, n_used, xb, wg, bg.reshape(depth, n_e, 1, d_exp), wu,
      bu.reshape(depth, n_e, 1, d_exp), wd, bd.reshape(depth, n_e, 1, d))


def _combine_kernel(x_ref, g2_ref, gates_ref, y_ref, o_ref):
    gates = gates_ref[0]
    half = y_ref.shape[-1]
    acc_lo = jnp.zeros((x_ref.shape[1], half), F32)
    acc_hi = jnp.zeros((x_ref.shape[1], half), F32)
    for k in range(TOP_K):
        lo, hi = _unpack_rows(y_ref[k])
        acc_lo = acc_lo + gates[:, k:k + 1] * lo
        acc_hi = acc_hi + gates[:, k:k + 1] * hi
    o_ref[0] = x_ref[0] + g2_ref[0] * jnp.concatenate([acc_lo, acc_hi], axis=1)


def _combine(x, g2, gates, y_sel, tok_offset, batches=None):
    b, t, d = x.shape
    lo, hi = batches or (0, b)
    tm = int(np.gcd(min(t, 512), tok_offset)) if tok_offset else min(t, 512)
    n_t = t // tm
    off = tok_offset // tm
    row = lambda bi, i: (bi + lo, i, 0)
    return pl.pallas_call(
        _combine_kernel,
        out_shape=jax.ShapeDtypeStruct((b, t, d), F32),
        grid=(hi - lo, n_t),
        in_specs=[
            pl.BlockSpec((1, tm, d), row),
            pl.BlockSpec((1, 1, d), lambda bi, i: (bi + lo, 0, 0)),
            pl.BlockSpec((1, tm, ROUTER_PAD), row),
            pl.BlockSpec((TOP_K, tm, d // 2), lambda bi, i: (0, off + bi * n_t + i, 0)),
        ],
        out_specs=pl.BlockSpec((1, tm, d), row),
        input_output_aliases={0: 0},
        compiler_params=_params(("parallel", "parallel")),
        name="moe_combine",
    )(x, g2, gates, y_sel)


def _route_kernel(ids_ref, upper_ref, dest_ref, counts_ref, run_ref, *, tile):
    p = pl.program_id(0)
    i = pl.program_id(1)
    tm = ids_ref.shape[0]
    ids_t = ids_ref[...].T
    expert = lax.broadcasted_iota(jnp.int32, (N_EXPERTS, tm), 0)
    chosen = [ids_t[k:k + 1, :] == expert for k in range(TOP_K)]
    picks = functools.reduce(jnp.add, [c.astype(F32) for c in chosen])
    tile_counts = jnp.sum(picks, axis=1, keepdims=True)

    @pl.when((p == 0) & (i == 0))
    def _():
        run_ref[...] = jnp.zeros_like(run_ref)

    @pl.when(p == 0)
    def _():
        run_ref[...] += tile_counts

    @pl.when((p == 1) & (i == 0))
    def _():
        counts = run_ref[...]
        counts_ref[...] = counts.astype(jnp.int32)
        padded = jnp.floor((counts + (tile - 1)) * (1.0 / tile)) * tile
        row = lax.broadcasted_iota(jnp.int32, counts.shape, 0)
        ends = padded
        for s in (1, 2, 4, 8, 16):
            ends = ends + jnp.where(row >= s, pltpu.roll(ends, s, 0), 0.0)
        run_ref[...] = ends - padded

    @pl.when(p == 1)
    def _():
        before = _dot(picks.astype(BF16), upper_ref[...])
        slot = before + run_ref[:, 0:1]
        rows = [jnp.sum(jnp.where(c, slot, 0.0), axis=0, keepdims=True) for c in chosen]
        rows += [jnp.zeros_like(rows[0])] * (dest_ref.shape[0] - TOP_K)
        dest_ref[...] = jnp.concatenate(rows, axis=0).astype(jnp.int32)
        run_ref[...] += tile_counts


def _route(ids):
    n = ids.shape[0]
    tm = int(np.gcd(n, 1024))
    n_tiles = n // tm
    upper = jnp.asarray(np.triu(np.ones((tm, tm), np.float32), 1), BF16)
    dest, counts = pl.pallas_call(
        functools.partial(_route_kernel, tile=EXPERT_TILE),
        out_shape=[jax.ShapeDtypeStruct((SUBLANES, n), jnp.int32),
                   jax.ShapeDtypeStruct((N_EXPERTS, LANES), jnp.int32)],
        grid=(2, n_tiles),
        in_specs=[pl.BlockSpec((tm, ROUTER_PAD), lambda p, i: (i, 0)),
                  pl.BlockSpec((tm, tm), lambda p, i: (0, 0))],
        out_specs=[pl.BlockSpec((SUBLANES, tm), lambda p, i: (0, i * p)),
                   pl.BlockSpec((N_EXPERTS, LANES), lambda p, i: (0, 0))],
        scratch_shapes=[pltpu.VMEM((N_EXPERTS, LANES), F32)],
        compiler_params=_params(("arbitrary", "arbitrary")),
        name="route",
    )(ids, upper)
    return dest, counts[:, 0]


def _row_gather(table, idx):
    info = plsc.get_sparse_core_info()
    n_cores, n_workers = info.num_cores, info.num_cores * info.num_subcores
    n_rows, width = idx.shape[0], table.shape[1]
    chunk_rows = SC_GATHER_ROWS
    per_worker = n_rows // n_workers
    n_chunks = per_worker // chunk_rows
    assert per_worker * n_workers == n_rows and n_chunks * chunk_rows == per_worker and n_chunks % 2 == 0
    mesh = plsc.VectorSubcoreMesh(core_axis_name="c", subcore_axis_name="s")

    @functools.partial(
        pl.kernel, mesh=mesh,
        out_type=jax.ShapeDtypeStruct((n_rows, width), table.dtype),
        scratch_types=[pltpu.VMEM((n_chunks, chunk_rows), jnp.int32), pltpu.VMEM((2, chunk_rows, width), table.dtype),
                       pltpu.SemaphoreType.DMA((2,)), pltpu.SemaphoreType.DMA((2,))],
    )
    def gather_kernel(table_hbm, idx_hbm, out_hbm, idx_v, rows_v, gather_sem, write_sem):
        worker = lax.axis_index("s") * n_cores + lax.axis_index("c")
        pltpu.sync_copy(idx_hbm.at[worker], idx_v)

        def gather(c, slot):
            return pltpu.make_async_copy(table_hbm.at[idx_v.at[c]], rows_v.at[slot], gather_sem.at[slot])

        def write(c, slot):
            first_row = worker * per_worker + c * chunk_rows
            return pltpu.make_async_copy(rows_v.at[slot], out_hbm.at[pl.ds(first_row, chunk_rows)], write_sem.at[slot])

        gather(0, 0).start()

        @pl.loop(0, n_chunks, step=2)
        def _(c0):
            for slot in (0, 1):
                c = c0 + slot
                gather(c, slot).wait()

                @pl.when(c >= 1)
                def _():
                    write(c - 1, 1 - slot).wait()

                @pl.when(c + 1 < n_chunks)
                def _():
                    gather(c + 1, 1 - slot).start()

                write(c, slot).start()

        write(n_chunks - 1, 1).wait()

    return gather_kernel(table, idx.reshape(n_workers, n_chunks, chunk_rows))


def _row_scatter(rows, dest, n_out):
    info = plsc.get_sparse_core_info()
    n_cores, n_workers = info.num_cores, info.num_cores * info.num_subcores
    n_choices, n_rows = dest.shape
    width = rows.shape[1]
    chunk_rows = SC_GATHER_ROWS
    per_worker = n_rows // (n_workers * chunk_rows)
    assert per_worker * n_workers * chunk_rows == n_rows == rows.shape[0]
    idx = dest.reshape(n_choices, n_workers, per_worker, chunk_rows).transpose(1, 0, 2, 3)
    mesh = plsc.VectorSubcoreMesh(core_axis_name="c", subcore_axis_name="s")

    @functools.partial(
        pl.kernel, mesh=mesh,
        out_type=jax.ShapeDtypeStruct((n_out, width), rows.dtype),
        scratch_types=[pltpu.VMEM((n_choices, per_worker, chunk_rows), jnp.int32),
                       pltpu.VMEM((chunk_rows, width), rows.dtype), pltpu.SemaphoreType.DMA],
    )
    def scatter_kernel(rows_hbm, idx_hbm, out_hbm, idx_v, rows_v, sem):
        worker = lax.axis_index("s") * n_cores + lax.axis_index("c")
        pltpu.sync_copy(idx_hbm.at[worker], idx_v)

        @pl.loop(0, per_worker)
        def _(c):
            first_row = (worker * per_worker + c) * chunk_rows
            pltpu.sync_copy(rows_hbm.at[pl.ds(first_row, chunk_rows)], rows_v)
            copies = [pltpu.async_copy(rows_v, out_hbm.at[idx_v.at[k, c]], sem) for k in range(n_choices)]
            for cp in copies:
                cp.wait()

    return scatter_kernel(rows, idx)


def _moe(h2, ids, layer, w_exp):
    n = h2.shape[0]
    nk = n * TOP_K
    tm = EXPERT_TILE
    dest, counts = _route(ids)
    dest = dest[:TOP_K]
    padded = (counts + tm - 1) // tm * tm
    pends = jnp.cumsum(padded)
    n_blocks = -(-nk // tm) + N_EXPERTS
    n_slots = n_blocks * tm
    block_start = jnp.arange(n_blocks, dtype=jnp.int32) * tm
    block_e = jnp.minimum(jnp.sum((pends[None] <= block_start[:, None]).astype(jnp.int32), axis=1), N_EXPERTS - 1)
    is_e = (block_e[:, None] == jnp.arange(N_EXPERTS, dtype=jnp.int32)[None]).astype(jnp.int32)
    block_rows = jnp.clip(jnp.sum(is_e * (pends - padded + counts)[None], axis=1) - block_start, 0, tm)
    n_used = (pends[-1] // tm).astype(jnp.int32).reshape(1)
    e_ids = jnp.arange(N_EXPERTS, dtype=jnp.int32)
    later = (e_ids[None] > e_ids[:, None]) & (counts[None] > 0)
    next_nonempty = jnp.min(jnp.where(later, e_ids[None], N_EXPERTS), axis=1)
    next_nonempty = jnp.where(next_nonempty == N_EXPERTS, -1, next_nonempty)
    next_e = jnp.sum(is_e * next_nonempty[None], axis=1)
    xb = _row_scatter(h2, dest, n_slots)
    return _experts(xb, block_e, next_e, block_rows, n_used, layer, *w_exp), dest


def _gather_choices(yb, dest, lo, hi):
    return _row_gather(yb, dest[:, lo:hi].reshape(-1)).reshape(TOP_K, hi - lo, -1)


def _split_batch(b, t, lead):
    unit = 2 * SC_GATHER_ROWS * SC_WORKERS // TOP_K
    fits = [bs for bs in range(1, b) if (lead + bs * t) % unit == 0 and ((b - bs) * t) % unit == 0]
    return min(fits, key=lambda bs: abs(lead + bs * t - (b - bs) * t)) if fits else b


def _mod_parts(mod_l, b):
    d = mod_l.shape[-1] // 6
    lat = [mod_l[:b, k * d:(k + 1) * d].reshape(b, 1, d) for k in range(6)]
    ctx = [jnp.broadcast_to(mod_l[b, k * d:(k + 1) * d].reshape(1, 1, d), (b, 1, d)) for k in range(6)]
    return lat, ctx


def kernel(x, c, ctx, c_ctx, ada_w, ada_b, norm1_g, norm2_g, ev_w_in, ev_w_out, ev_q_gain, ev_k_gain, ev_rpb, ev_conv_w, ev_conv_b, od_w_in, od_w_out, od_conv_w, od_conv_b, od_fwd_wa, od_fwd_ba, od_fwd_wx, od_fwd_bx, od_fwd_lam, od_bwd_wa, od_bwd_ba, od_bwd_wx, od_bwd_bx, od_bwd_lam, router_w, router_b, exp_w_gate, exp_b_gate, exp_w_up, exp_b_up, exp_w_down, exp_b_down):
    b, t, d = x.shape
    l = ctx.shape[1]
    assert ada_w.shape[0] == DEPTH == 2 and t % GRID_W == 0 and t // GRID_W >= WIN_H

    n_rows_c = -(-(b + 1) // SUBLANES) * SUBLANES
    cvec = jnp.zeros((n_rows_c, d), F32).at[:b].set(c).at[b].set(c_ctx)
    mod = _ada_mod(cvec, ada_w, ada_b)

    def router(layer):
        w_r = jnp.zeros((d, ROUTER_PAD), F32).at[:, :N_EXPERTS].set(router_w[layer]).astype(BF16)
        b_r = jnp.zeros((1, ROUTER_PAD), F32).at[0, :N_EXPERTS].set(router_b[layer])
        return w_r, b_r

    w_exp = (exp_w_gate, exp_b_gate, exp_w_up, exp_b_up, exp_w_down, exp_b_down)

    (sh1, sc1, g1, sh2, sc2, g2), (csh1, csc1, cg1, csh2, csc2, cg2) = _mod_parts(mod[0], b)
    n1 = norm1_g[0].reshape(1, d)
    n2 = norm2_g[0].reshape(1, d)
    w_in = ev_w_in[0].astype(BF16)
    w_out = ev_w_out[0].astype(BF16)
    q_scale = NA_HEAD_DIM ** -0.5 * LOG2_E
    head_gain = jnp.stack([jnp.tile(ev_q_gain[0] * q_scale, NA_HEADS), jnp.tile(ev_k_gain[0], NA_HEADS)])
    head_gain = head_gain.reshape(2, 1, NA_WIDTH).astype(F32)
    proj = _inproj(x, n1, sh1, sc1, w_in, head_gain)
    proj_c = _inproj(ctx, n1, csh1, csc1, w_in, head_gain)
    kh, r0, row_type, patterns = _na_tables(t // GRID_W)
    bias_tab = _na_bias_table(ev_rpb[0], patterns)
    o_a = _neighbourhood_attention(proj, proj_c, bias_tab, r0, row_type, kh)
    oc_a = _context_attention(proj_c)
    w_r, b_r = router(0)
    x1, h2, ids, gates = _even_out(o_a, proj, ev_conv_w[0], ev_conv_b[0], w_out, x, g1, n2, sh2, sc2, w_r, b_r)
    c1, h2c, ids_c, gates_c = _even_out(oc_a, proj_c, ev_conv_w[0], ev_conv_b[0], w_out, ctx, cg1, n2, csh2, csc2,
                                        w_r, b_r)
    tokens = jnp.concatenate([h2c.reshape(b * l, d // 2), h2.reshape(b * t, d // 2)], axis=0)
    ids_all = jnp.concatenate([ids_c.reshape(b * l, ROUTER_PAD), ids.reshape(b * t, ROUTER_PAD)], axis=0)
    yb, dest = _moe(tokens, ids_all, 0, w_exp)
    bs = _split_batch(b, t, b * l)
    n_first = b * l + bs * t
    y_first = _gather_choices(yb, dest, 0, n_first)
    hctx = _combine(c1, cg2, gates_c, y_first, 0)
    x = _combine(x1, g2, gates, y_first, b * l, (0, bs))
    if bs < b:
        x = _combine(x, g2, gates, _gather_choices(yb, dest, n_first, b * (l + t)), 0, (bs, b))

    (sh1, sc1, g1, sh2, sc2, g2), (csh1, csc1, _, _, _, _) = _mod_parts(mod[1], b)
    n1 = norm1_g[1].reshape(1, d)
    n2 = norm2_g[1].reshape(1, d)
    w_in = od_w_in[0].astype(BF16)
    width = w_in.shape[1] // 2
    proj = _inproj(x, n1, sh1, sc1, w_in)
    u_ctx = _inproj(hctx, n1, csh1, csc1, w_in[:, width:])
    h_dir = []
    for reverse, (wa, ba, wx, bx, lam) in ((False, (od_fwd_wa, od_fwd_ba, od_fwd_wx, od_fwd_bx, od_fwd_lam)),
                                           (True, (od_bwd_wa, od_bwd_ba, od_bwd_wx, od_bwd_bx, od_bwd_lam))):
        w_cat = jnp.concatenate([wa[0], wx[0]], axis=-1).astype(BF16)
        h_dir.append(_lru_scan(proj, u_ctx, od_conv_w[0], od_conv_b[0], w_cat, ba[0], bx[0], lam[0], reverse))
    w_r, b_r = router(1)
    x1, h2, ids, gates = _odd_out(h_dir[0], h_dir[1], proj, od_w_out[0].astype(BF16), x, g1, n2, sh2, sc2, w_r, b_r)
    yb, dest = _moe(h2.reshape(b * t, d // 2), ids.reshape(b * t, ROUTER_PAD), 1, w_exp)
    bs = _split_batch(b, t, 0)
    x = _combine(x1, g2, gates, _gather_choices(yb, dest, 0, bs * t), 0, (0, bs))
    if bs < b:
        x = _combine(x, g2, gates, _gather_choices(yb, dest, bs * t, b * t), 0, (bs, b))
    return x
```

```python
import functools

import numpy as np
import jax
import jax.numpy as jnp
from jax import lax
from jax.experimental import pallas as pl
from jax.experimental.pallas import tpu as pltpu
from jax.experimental.pallas import tpu_sc as plsc

DEPTH = 2
GRID_W = 64
EPS = 1e-6
NEG_INF = -1e30
LOG2_E = 1.4426950408889634
NA_HEADS = 8
NA_HEAD_DIM = 64
NA_WIDTH = NA_HEADS * NA_HEAD_DIM
HEAD_PAIRS = NA_HEADS // 2
WIN_H = 8
WIN_W = 16
SC_CONV = 3
LRU_BLOCKS = 4
LRU_CONV = 4
LRU_C = 8.0
N_EXPERTS = 32
TOP_K = 4
SWIGLU_LIMIT = 7.0
SWIGLU_ALPHA = 1.702

LANES = 128
SUBLANES = 8
HALO = 16
ROUTER_PAD = LANES
NA_ROWS_PER_STEP = 4
OUT_TILE = 1024
OUT_SUB_TILE = 512
EXPERT_TILE = 512
SC_GATHER_ROWS = 64
SC_WORKERS = 32
VMEM_LIMIT = 56 * 1024 * 1024

F32 = jnp.float32
BF16 = jnp.bfloat16


def _params(sem, vmem=VMEM_LIMIT):
    return pltpu.CompilerParams(dimension_semantics=sem, vmem_limit_bytes=vmem)


def _dot(a, b):
    return jnp.dot(a, b, preferred_element_type=F32)


def _dot_nt(a, b):
    return lax.dot_general(a, b, (((1,), (1,)), ((), ())), preferred_element_type=F32)


def _pack_rows(v):
    w = v.shape[-1] // 2
    lo = lax.bitcast_convert_type(v[:, :w].astype(BF16).astype(F32), jnp.int32)
    hi = lax.bitcast_convert_type(v[:, w:].astype(BF16).astype(F32), jnp.int32)
    return lax.shift_right_logical(lo, 16) | (hi & jnp.int32(-65536))


def _unpack_rows(p):
    lo = lax.bitcast_convert_type(lax.shift_left(p, 16), F32)
    hi = lax.bitcast_convert_type(p & jnp.int32(-65536), F32)
    return lo, hi


def _rms_mod(x, g, shift, scale):
    ms = jnp.mean(x * x, axis=-1, keepdims=True)
    y = x * lax.rsqrt(ms + EPS) * g
    return y * (1.0 + scale) + shift


def _ada_kernel(c_ref, w_ref, b_ref, o_ref):
    c = c_ref[...]
    s = (c * jax.nn.sigmoid(c)).astype(BF16)
    o_ref[0] = _dot(s, w_ref[0].astype(BF16)) + b_ref[0]


def _ada_mod(cvec, ada_w, ada_b):
    depth, d, n = ada_w.shape
    r = cvec.shape[0]
    tn = 1536
    return pl.pallas_call(
        _ada_kernel,
        out_shape=jax.ShapeDtypeStruct((depth, r, n), F32),
        grid=(depth, n // tn),
        in_specs=[
            pl.BlockSpec((r, d), lambda l, j: (0, 0)),
            pl.BlockSpec((1, d, tn), lambda l, j: (l, 0, j)),
            pl.BlockSpec((1, 1, tn), lambda l, j: (l, 0, j)),
        ],
        out_specs=pl.BlockSpec((1, r, tn), lambda l, j: (l, 0, j)),
        compiler_params=_params(("parallel", "parallel")),
        name="ada_mod",
    )(cvec, ada_w, ada_b.reshape(depth, 1, n))


def _inproj_kernel(x_ref, g_ref, sh_ref, sc_ref, w_ref, hg_ref, ones_ref, o_ref, *, n_tiles, tn, n_headnorm):
    h = _rms_mod(x_ref[0], g_ref[...], sh_ref[0], sc_ref[0]).astype(BF16)
    for j in range(n_tiles):
        y = _dot(h, w_ref[:, j * tn:(j + 1) * tn])
        if j < n_headnorm:
            ms = _dot((y * y).astype(BF16), ones_ref[...]) * (1.0 / NA_HEAD_DIM)
            y = y * lax.rsqrt(ms + EPS) * hg_ref[j]
        o_ref[0, :, j * tn:(j + 1) * tn] = y.astype(o_ref.dtype)


def _inproj(x, g, shift, scale, w, head_gain=None):
    b, t, d = x.shape
    n = w.shape[1]
    tn = NA_WIDTH
    tm = min(t, 512)
    n_headnorm = 0 if head_gain is None else head_gain.shape[0]
    if head_gain is None:
        head_gain = jnp.ones((1, 1, tn), F32)
    hid = np.arange(tn) // NA_HEAD_DIM
    ones_bd = jnp.asarray((hid[:, None] == hid[None, :]), BF16)
    kern = functools.partial(_inproj_kernel, n_tiles=n // tn, tn=tn, n_headnorm=n_headnorm)
    return pl.pallas_call(
        kern,
        out_shape=jax.ShapeDtypeStruct((b, t, n), BF16),
        grid=(b, t // tm),
        in_specs=[
            pl.BlockSpec((1, tm, d), lambda bi, i: (bi, i, 0)),
            pl.BlockSpec((1, d), lambda bi, i: (0, 0)),
            pl.BlockSpec((1, 1, d), lambda bi, i: (bi, 0, 0)),
            pl.BlockSpec((1, 1, d), lambda bi, i: (bi, 0, 0)),
            pl.BlockSpec((d, n), lambda bi, i: (0, 0)),
            pl.BlockSpec(head_gain.shape, lambda bi, i: (0, 0, 0)),
            pl.BlockSpec((tn, tn), lambda bi, i: (0, 0)),
        ],
        out_specs=pl.BlockSpec((1, tm, n), lambda bi, i: (bi, i, 0)),
        compiler_params=_params(("parallel", "parallel")),
        name="inproj",
    )(x, g, shift, scale, w, head_gain, ones_bd)


def _na_tables(rows):
    kh = min(WIN_H, rows)
    r = np.arange(rows)
    r0 = np.clip(r - kh // 2, 0, rows - kh)
    dr = r0[:, None] + np.arange(kh)[None] - r[:, None] + WIN_H - 1
    patterns, row_type = np.unique(dr, axis=0, return_inverse=True)
    return kh, r0.astype(np.int32), row_type.reshape(-1).astype(np.int32), patterns


def _na_bias_table(rpb, patterns):
    qc = np.arange(GRID_W)
    kc = np.arange(GRID_W)
    c0 = np.clip(qc - WIN_W // 2, 0, GRID_W - WIN_W)[:, None]
    valid = (kc[None] >= c0) & (kc[None] < c0 + WIN_W)
    dc = np.clip(kc[None] - qc[:, None] + WIN_W - 1, 0, 2 * WIN_W - 2)
    n_pat, kh = patterns.shape
    onehot_dc = jnp.asarray(dc[None] == np.arange(2 * WIN_W - 1)[:, None, None], F32)
    tab = jnp.einsum('hpic,cqk->hpiqk', rpb.astype(F32)[:, patterns], onehot_dc,
                     precision=lax.Precision.HIGHEST)
    tab = jnp.where(valid[None, None, None], tab * LOG2_E, NEG_INF)
    tab = tab.reshape(HEAD_PAIRS, 2, n_pat, kh, GRID_W, GRID_W)
    tab = tab.transpose(2, 0, 1, 4, 3, 5)
    return tab.reshape(n_pat, HEAD_PAIRS, 2 * GRID_W, kh * GRID_W)


def _pair_attention(q, keys, values, biases):
    m = q.shape[0]
    qq = _stack_heads(q)
    scores = []
    for k, bias in zip(keys, biases):
        s = _dot_nt(qq, k)
        scores.append(s if bias is None else s + bias)
    s = jnp.concatenate(scores, axis=1)
    e = jnp.exp2(s - jnp.max(s, axis=-1, keepdims=True))
    denom = jnp.sum(e, axis=-1, keepdims=True)
    e = e.astype(BF16)
    o, start = 0.0, 0
    for v in values:
        o = o + _dot(e[:, start:start + v.shape[0]], v)
        start += v.shape[0]
    o = o * (1.0 / denom)
    lane_o = lax.broadcasted_iota(jnp.int32, (m, LANES), 1)
    return jnp.where(lane_o < NA_HEAD_DIM, o[:m], o[m:])


def _stack_heads(q):
    lane = lax.broadcasted_iota(jnp.int32, q.shape, 1)
    zero = jnp.zeros_like(q)
    return jnp.concatenate([jnp.where(lane < NA_HEAD_DIM, q, zero), jnp.where(lane >= NA_HEAD_DIM, q, zero)], axis=0)


def _na_kernel(r0_ref, type_ref, q_ref, k_ref, v_ref, kc_ref, vc_ref, bias_ref, o_ref, s_ref, p_ref, *,
               kh, rows_per_step):
    n_lat = kh * GRID_W
    tiles = [(j, p) for j in range(rows_per_step) for p in range(HEAD_PAIRS)]
    window = []
    for j in range(rows_per_step):
        r = pl.program_id(1) * rows_per_step + j
        window.append((pl.multiple_of(r0_ref[r] * GRID_W, GRID_W), type_ref[r]))

    for idx, (j, p) in enumerate(tiles):
        start, rtype = window[j]
        cols = slice(p * LANES, (p + 1) * LANES)
        qq = _stack_heads(q_ref[0, j * GRID_W:(j + 1) * GRID_W, cols])
        s_ref[idx, :, :n_lat] = _dot_nt(qq, k_ref[0, pl.ds(start, n_lat), cols]) + bias_ref[rtype, p]
        s_ref[idx, :, n_lat:] = _dot_nt(qq, kc_ref[0, :, cols])

    denoms = []
    for idx in range(len(tiles)):
        s = s_ref[idx]
        e = jnp.exp2(s - jnp.max(s, axis=-1, keepdims=True))
        denoms.append(jnp.sum(e, axis=-1, keepdims=True))
        p_ref[idx] = e.astype(BF16)

    lane = lax.broadcasted_iota(jnp.int32, (GRID_W, LANES), 1)
    for idx, (j, p) in enumerate(tiles):
        start, _ = window[j]
        cols = slice(p * LANES, (p + 1) * LANES)
        o = _dot(p_ref[idx, :, :n_lat], v_ref[0, pl.ds(start, n_lat), cols]) + _dot(p_ref[idx, :, n_lat:],
                                                                                 vc_ref[0, :, cols])
        o = o * (1.0 / denoms[idx])
        o = jnp.where(lane < NA_HEAD_DIM, o[:GRID_W], o[GRID_W:])
        o_ref[0, j * GRID_W:(j + 1) * GRID_W, cols] = o.astype(o_ref.dtype)


def _neighbourhood_attention(proj, proj_c, bias_tab, r0, row_type, kh):
    b, t, _ = proj.shape
    l = proj_c.shape[1]
    rows = t // GRID_W
    w = NA_WIDTH
    rps = int(np.gcd(rows, NA_ROWS_PER_STEP))
    q_rows = rps * GRID_W
    grid_spec = pltpu.PrefetchScalarGridSpec(
        num_scalar_prefetch=2,
        grid=(b, rows // rps),
        in_specs=[
            pl.BlockSpec((1, q_rows, w), lambda bi, r, *_: (bi, r, 0)),
            pl.BlockSpec((1, t, w), lambda bi, r, *_: (bi, 0, 1)),
            pl.BlockSpec((1, t, w), lambda bi, r, *_: (bi, 0, 2)),
            pl.BlockSpec((1, l, w), lambda bi, r, *_: (bi, 0, 1)),
            pl.BlockSpec((1, l, w), lambda bi, r, *_: (bi, 0, 2)),
            pl.BlockSpec(bias_tab.shape, lambda bi, r, *_: (0, 0, 0, 0)),
        ],
        out_specs=pl.BlockSpec((1, q_rows, w), lambda bi, r, *_: (bi, r, 0)),
        scratch_shapes=[pltpu.VMEM((rps * HEAD_PAIRS, 2 * GRID_W, kh * GRID_W + l), F32),
                        pltpu.VMEM((rps * HEAD_PAIRS, 2 * GRID_W, kh * GRID_W + l), BF16)],
    )
    return pl.pallas_call(
        functools.partial(_na_kernel, kh=kh, rows_per_step=rps),
        out_shape=jax.ShapeDtypeStruct((b, t, w), BF16),
        grid_spec=grid_spec,
        compiler_params=_params(("parallel", "arbitrary")),
        name="na_attention",
    )(jnp.asarray(r0), jnp.asarray(row_type), proj, proj, proj, proj_c, proj_c, bias_tab)


def _ctx_attn_kernel(q_ref, k_ref, v_ref, o_ref):
    for p in range(HEAD_PAIRS):
        cols = slice(p * LANES, (p + 1) * LANES)
        o = _pair_attention(q_ref[0, :, cols], [k_ref[0, :, cols]], [v_ref[0, :, cols]], [None])
        o_ref[0, :, cols] = o.astype(o_ref.dtype)


def _context_attention(proj_c):
    b, l, _ = proj_c.shape
    w = NA_WIDTH
    return pl.pallas_call(
        _ctx_attn_kernel,
        out_shape=jax.ShapeDtypeStruct((b, l, w), BF16),
        grid=(b,),
        in_specs=[pl.BlockSpec((1, l, w), lambda bi, j=j: (bi, 0, j)) for j in range(3)],
        out_specs=pl.BlockSpec((1, l, w), lambda bi: (bi, 0, 0)),
        compiler_params=_params(("parallel",)),
        name="ctx_attention",
    )(proj_c, proj_c, proj_c)


def _top4(logits):
    lane = lax.broadcasted_iota(jnp.int32, logits.shape, 1)
    cur = jnp.where(lane < N_EXPERTS, logits, -jnp.inf)
    vals, idxs = [], []
    for _ in range(TOP_K):
        m = jnp.max(cur, axis=-1, keepdims=True)
        first = jnp.min(jnp.where(cur == m, lane, ROUTER_PAD).astype(F32), axis=-1, keepdims=True)
        idx = first.astype(jnp.int32)
        vals.append(m)
        idxs.append(idx)
        cur = jnp.where(lane == idx, -jnp.inf, cur)
    exps = [jnp.exp(v - vals[0]) for v in vals]
    inv = 1.0 / functools.reduce(jnp.add, exps)
    ids = jnp.zeros(logits.shape, jnp.int32)
    gates = jnp.zeros(logits.shape, F32)
    for k in range(TOP_K):
        ids = jnp.where(lane == k, idxs[k], ids)
        gates = jnp.where(lane == k, exps[k] * inv, gates)
    return ids, gates


def _layer_tail(y, rows, x_ref, g1_ref, n2_ref, sh2_ref, sc2_ref, wr_ref, br_ref, xo_ref, h2_ref, ids_ref, gates_ref):
    x_new = x_ref[0, rows] + g1_ref[0] * y
    xo_ref[0, rows] = x_new
    h2 = _rms_mod(x_new, n2_ref[...], sh2_ref[0], sc2_ref[0])
    h2_ref[0, rows] = _pack_rows(h2)
    logits = _dot(h2.astype(BF16), wr_ref[...]) + br_ref[...]
    ids, gates = _top4(logits)
    ids_ref[0, rows] = ids
    gates_ref[0, rows] = gates


def _sub_tiles(n_rows):
    sub = min(OUT_SUB_TILE, n_rows)
    return [slice(s, s + sub) for s in range(0, n_rows, sub)]


def _tail_specs(b, t, d, tm):
    row = lambda bi, i: (bi, i, 0)
    per_b = lambda bi, i: (bi, 0, 0)
    const = lambda bi, i: (0, 0)
    in_specs = [
        pl.BlockSpec((1, tm, d), row),
        pl.BlockSpec((1, 1, d), per_b),
        pl.BlockSpec((1, d), const),
        pl.BlockSpec((1, 1, d), per_b),
        pl.BlockSpec((1, 1, d), per_b),
        pl.BlockSpec((d, ROUTER_PAD), const),
        pl.BlockSpec((1, ROUTER_PAD), const),
    ]
    out_specs = [
        pl.BlockSpec((1, tm, d), row),
        pl.BlockSpec((1, tm, d // 2), row),
        pl.BlockSpec((1, tm, ROUTER_PAD), row),
        pl.BlockSpec((1, tm, ROUTER_PAD), row),
    ]
    out_shape = [
        jax.ShapeDtypeStruct((b, t, d), F32),
        jax.ShapeDtypeStruct((b, t, d // 2), jnp.int32),
        jax.ShapeDtypeStruct((b, t, ROUTER_PAD), jnp.int32),
        jax.ShapeDtypeStruct((b, t, ROUTER_PAD), F32),
    ]
    return in_specs, out_specs, out_shape


def _halo_fix(rolled, at_row, halo_row, present):
    n = rolled.shape[0]
    first = at_row < SUBLANES
    assert first or at_row >= n - SUBLANES
    slab = rolled[:SUBLANES] if first else rolled[n - SUBLANES:]
    sub = lax.broadcasted_iota(jnp.int32, slab.shape, 0)
    fill = jnp.where(present, halo_row, jnp.zeros_like(halo_row))
    slab = jnp.where(sub == at_row % SUBLANES, fill, slab)
    return jnp.concatenate([slab, rolled[SUBLANES:]] if first else [rolled[:n - SUBLANES], slab], axis=0)


def _even_out_kernel(oa_ref, bg_ref, cg_ref, xin_ref, cgp_ref, xinp_ref, cgn_ref, xinn_ref, cw_ref, cb_ref,
                     wa_ref, wb_ref, *tail_refs, tm):
    i = pl.program_id(1)
    has_prev = i > 0
    has_next = i < pl.num_programs(1) - 1
    u = cg_ref[0].astype(F32) * xin_ref[0].astype(F32)
    u_prev = (cgp_ref[0].astype(F32) * xinp_ref[0].astype(F32))[HALO - 1:HALO]
    u_next = (cgn_ref[0].astype(F32) * xinn_ref[0].astype(F32))[0:1]
    u_m1 = _halo_fix(pltpu.roll(u, 1, 0), 0, u_prev, has_prev)
    u_p1 = _halo_fix(pltpu.roll(u, tm - 1, 0), tm - 1, u_next, has_next)
    cw = cw_ref[...]
    conv = u_m1 * cw[0:1] + u * cw[1:2] + u_p1 * cw[2:3] + cb_ref[...]
    o_b = (bg_ref[0].astype(F32) * conv).astype(BF16)
    for rows in _sub_tiles(tm):
        y = _dot(oa_ref[0, rows], wa_ref[...]) + _dot(o_b[rows], wb_ref[...])
        _layer_tail(y, rows, *tail_refs)


def _even_out(o_a, proj, conv_w, conv_b, w_out, x, g1, n2, sh2, sc2, w_r, b_r):
    b, t, d = x.shape
    w = NA_WIDTH
    tm = min(t, OUT_TILE)
    hb = tm // HALO
    n_hblocks = t // HALO
    row = lambda bi, i: (bi, i, 0)
    const = lambda bi, i: (0, 0)
    prev = lambda col: (lambda bi, i: (bi, jnp.maximum(i * hb - 1, 0), col))
    nxt = lambda col: (lambda bi, i: (bi, jnp.minimum((i + 1) * hb, n_hblocks - 1), col))
    tail_in, out_specs, out_shape = _tail_specs(b, t, d, tm)
    in_specs = [
        pl.BlockSpec((1, tm, w), row),
        pl.BlockSpec((1, tm, w), lambda bi, i: (bi, i, 3)),
        pl.BlockSpec((1, tm, w), lambda bi, i: (bi, i, 4)),
        pl.BlockSpec((1, tm, w), lambda bi, i: (bi, i, 5)),
        pl.BlockSpec((1, HALO, w), prev(4)),
        pl.BlockSpec((1, HALO, w), prev(5)),
        pl.BlockSpec((1, HALO, w), nxt(4)),
        pl.BlockSpec((1, HALO, w), nxt(5)),
        pl.BlockSpec((SC_CONV, w), const),
        pl.BlockSpec((1, w), const),
        pl.BlockSpec((w, d), const),
        pl.BlockSpec((w, d), const),
    ] + tail_in
    return pl.pallas_call(
        functools.partial(_even_out_kernel, tm=tm),
        out_shape=out_shape,
        grid=(b, t // tm),
        in_specs=in_specs,
        out_specs=out_specs,
        compiler_params=_params(("parallel", "parallel")),
        name="even_out",
    )(o_a, proj, proj, proj, proj, proj, proj, proj, conv_w, conv_b.reshape(1, w),
      w_out[:w], w_out[w:], x, g1, n2, sh2, sc2, w_r, b_r)


def _log_sigmoid(x):
    return jnp.minimum(x, 0.0) - jnp.log1p(jnp.exp(-jnp.abs(x)))


def _lru_tile(u_ref, up_ref, un_ref, has_prev, has_next, cw_ref, cb_ref, w_ref, ba_ref, bx_ref, lam_ref,
              carry_ref, o_ref, *, ts, reverse):
    width = u_ref.shape[-1]
    blk = width // LRU_BLOCKS
    u = u_ref[0].astype(F32)
    up = up_ref[0].astype(F32)
    un = un_ref[0].astype(F32)
    u_m1 = _halo_fix(pltpu.roll(u, 1, 0), 0, up[HALO - 1:HALO], has_prev)
    u_p1 = _halo_fix(pltpu.roll(u, ts - 1, 0), ts - 1, un[0:1], has_next)
    u_p2 = _halo_fix(pltpu.roll(u, ts - 2, 0), ts - 2, un[0:1], has_next)
    u_p2 = _halo_fix(u_p2, ts - 1, un[1:2], has_next)
    cw = cw_ref[...]
    uc = u_m1 * cw[0:1] + u * cw[1:2] + u_p1 * cw[2:3] + u_p2 * cw[3:4] + cb_ref[...]
    ucb = uc.astype(BF16)
    za, zx = [], []
    for h in range(LRU_BLOCKS):
        z = _dot(ucb[:, h * blk:(h + 1) * blk], w_ref[h])
        za.append(z[:, :blk])
        zx.append(z[:, blk:])
    r = jax.nn.sigmoid(jnp.concatenate(za, axis=1) + ba_ref[...])
    gate_i = jax.nn.sigmoid(jnp.concatenate(zx, axis=1) + bx_ref[...])
    log_a = (LRU_C * _log_sigmoid(lam_ref[...])) * r
    a = jnp.exp(log_a)
    th = jnp.tanh(log_a)
    m2 = -2.0 * th / (1.0 - th)
    mult = jnp.where(m2 > 0.0, m2 * lax.rsqrt(m2), 0.0)
    bcoef = mult * gate_i * uc

    groups = ts // SUBLANES
    a3 = a.reshape(groups, SUBLANES, width)
    b3 = bcoef.reshape(groups, SUBLANES, width)
    sub = lax.broadcasted_iota(jnp.int32, a3.shape, 1)
    for s in (1, 2, 4):
        shift = SUBLANES - s if reverse else s
        a_sh = pltpu.roll(a3, shift, 1)
        b_sh = pltpu.roll(b3, shift, 1)
        m = (sub < SUBLANES - s) if reverse else (sub >= s)
        b3 = jnp.where(m, a3 * b_sh + b3, b3)
        a3 = jnp.where(m, a3 * a_sh, a3)

    h = carry_ref[0:1, :]
    order = range(groups - 1, -1, -1) if reverse else range(groups)
    edge = 0 if reverse else SUBLANES - 1
    for g in order:
        hg = a3[g] * h + b3[g]
        if o_ref is not None:
            o_ref[0, g * SUBLANES:(g + 1) * SUBLANES, :] = hg.astype(o_ref.dtype)
        h = hg[edge:edge + 1, :]
    carry_ref[...] = jnp.broadcast_to(h, carry_ref.shape)


def _lru_kernel(uc_ref, ucp_ref, ucn_ref, ul_ref, ulp_ref, uln_ref, cw_ref, cb_ref, w_ref, ba_ref, bx_ref,
                lam_ref, o_ref, carry_ref, *, ts, n_ctx_tiles, n_lat_tiles, reverse):
    j = pl.program_id(1)
    shared = (cw_ref, cb_ref, w_ref, ba_ref, bx_ref, lam_ref, carry_ref)

    @pl.when(j == 0)
    def _():
        carry_ref[...] = jnp.zeros_like(carry_ref)

    def tile_pos(step, n):
        return (n - 1 - step) if reverse else step

    @pl.when(j < n_ctx_tiles)
    def _():
        pos = tile_pos(j, n_ctx_tiles)
        _lru_tile(uc_ref, ucp_ref, ucn_ref, pos > 0, pos < n_ctx_tiles - 1, *shared, None, ts=ts, reverse=reverse)

    @pl.when(j >= n_ctx_tiles)
    def _():
        pos = tile_pos(j - n_ctx_tiles, n_lat_tiles)
        _lru_tile(ul_ref, ulp_ref, uln_ref, pos > 0, pos < n_lat_tiles - 1, *shared, o_ref, ts=ts, reverse=reverse)


def _lru_scan(proj, u_ctx, conv_w, conv_b, w_cat, ba, bx, lam, reverse):
    b, t, _ = proj.shape
    l, width = u_ctx.shape[1], u_ctx.shape[2]
    ts = min(256, l, t)
    n_c, n_l = l // ts, t // ts
    hb = ts // HALO

    def pos_of(step, n):
        step = jnp.clip(step, 0, n - 1)
        return (n - 1 - step) if reverse else step

    def tile_map(off, n, col):
        return lambda bi, j: (bi, pos_of(j - off, n), col)

    def prev_map(off, n, col):
        return lambda bi, j: (bi, jnp.maximum(pos_of(j - off, n) * hb - 1, 0), col)

    def next_map(off, n, col):
        return lambda bi, j: (bi, jnp.minimum((pos_of(j - off, n) + 1) * hb, n * hb - 1), col)

    const2 = lambda bi, j: (0, 0)
    in_specs = [
        pl.BlockSpec((1, ts, width), tile_map(0, n_c, 0)),
        pl.BlockSpec((1, HALO, width), prev_map(0, n_c, 0)),
        pl.BlockSpec((1, HALO, width), next_map(0, n_c, 0)),
        pl.BlockSpec((1, ts, width), tile_map(n_c, n_l, 1)),
        pl.BlockSpec((1, HALO, width), prev_map(n_c, n_l, 1)),
        pl.BlockSpec((1, HALO, width), next_map(n_c, n_l, 1)),
        pl.BlockSpec((LRU_CONV, width), const2),
        pl.BlockSpec((1, width), const2),
        pl.BlockSpec(w_cat.shape, lambda bi, j: (0, 0, 0)),
        pl.BlockSpec((1, width), const2),
        pl.BlockSpec((1, width), const2),
        pl.BlockSpec((1, width), const2),
    ]
    kern = functools.partial(_lru_kernel, ts=ts, n_ctx_tiles=n_c, n_lat_tiles=n_l, reverse=reverse)
    return pl.pallas_call(
        kern,
        out_shape=jax.ShapeDtypeStruct((b, t, width), BF16),
        grid=(b, n_c + n_l),
        in_specs=in_specs,
        out_specs=pl.BlockSpec((1, ts, width), tile_map(n_c, n_l, 0)),
        scratch_shapes=[pltpu.VMEM((SUBLANES, width), F32)],
        compiler_params=_params(("parallel", "arbitrary")),
        name="lru_scan_bwd" if reverse else "lru_scan_fwd",
    )(u_ctx, u_ctx, u_ctx, proj, proj, proj, conv_w, conv_b.reshape(1, width), w_cat,
      ba.reshape(1, width), bx.reshape(1, width), lam.reshape(1, width))


def _odd_out_kernel(hf_ref, hb_ref, gate_ref, w_ref, *tail_refs):
    for rows in _sub_tiles(hf_ref.shape[1]):
        hsum = hf_ref[0, rows].astype(F32) + hb_ref[0, rows].astype(F32)
        z = hsum * jax.nn.gelu(gate_ref[0, rows].astype(F32), approximate=True)
        y = _dot(z.astype(BF16), w_ref[...])
        _layer_tail(y, rows, *tail_refs)


def _odd_out(h_f, h_b, proj, w_out, x, g1, n2, sh2, sc2, w_r, b_r):
    b, t, d = x.shape
    width = h_f.shape[-1]
    tm = min(t, OUT_TILE)
    row = lambda bi, i: (bi, i, 0)
    tail_in, out_specs, out_shape = _tail_specs(b, t, d, tm)
    in_specs = [
        pl.BlockSpec((1, tm, width), row),
        pl.BlockSpec((1, tm, width), row),
        pl.BlockSpec((1, tm, width), row),
        pl.BlockSpec((width, d), lambda bi, i: (0, 0)),
    ] + tail_in
    return pl.pallas_call(
        _odd_out_kernel,
        out_shape=out_shape,
        grid=(b, t // tm),
        in_specs=in_specs,
        out_specs=out_specs,
        compiler_params=_params(("parallel", "parallel")),
        name="odd_out",
    )(h_f, h_b, proj, w_out, x, g1, n2, sh2, sc2, w_r, b_r)


def _expert_kernel(be_ref, next_ref, rows_ref, nb_ref, x_ref, wg_hbm, bg_ref, wu_hbm, bu_ref, wd_hbm, bd_ref,
                   o_ref, w_f32, wg_bf, wu_bf, wd_bf, h_bf, sem, *, layer, chunk):
    i = pl.program_id(0)
    expert = be_ref[i]
    used = i < nb_ref[0]

    def weight_copies(e):
        return [pltpu.make_async_copy(src.at[layer, e], w_f32.at[j], sem.at[j])
                for j, src in enumerate((wg_hbm, wu_hbm, wd_hbm))]

    @pl.when(i == 0)
    def _():
        for cp in weight_copies(expert):
            cp.start()

    @pl.when(used & ((i == 0) | (expert != be_ref[jnp.maximum(i - 1, 0)])))
    def _():
        for cp in weight_copies(expert):
            cp.wait()
        wg_bf[...] = w_f32[0].astype(BF16)
        wu_bf[...] = w_f32[1].astype(BF16)
        wd_bf[...] = w_f32[2].astype(BF16)

        @pl.when(next_ref[i] >= 0)
        def _():
            for cp in weight_copies(next_ref[i]):
                cp.start()

    @pl.when(used)
    def _():
        xp = x_ref[...]
        row = lax.broadcasted_iota(jnp.int32, xp.shape, 0)
        xp = jnp.where(row < rows_ref[i], xp, 0)
        x = jnp.concatenate(_unpack_rows(xp), axis=1).astype(BF16)
        for c in range(wg_bf.shape[1] // chunk):
            cs = slice(c * chunk, (c + 1) * chunk)
            g = jnp.minimum(_dot(x, wg_bf[:, cs]) + bg_ref[0, 0, :, cs], SWIGLU_LIMIT)
            u = jnp.clip(_dot(x, wu_bf[:, cs]) + bu_ref[0, 0, :, cs], -SWIGLU_LIMIT, SWIGLU_LIMIT)
            h_bf[:, cs] = (g * jax.nn.sigmoid(SWIGLU_ALPHA * g) * (u + 1.0)).astype(BF16)
        o_ref[...] = _pack_rows(_dot(h_bf[...], wd_bf[...]) + bd_ref[0, 0])

    @pl.when(jnp.logical_not(used))
    def _():
        o_ref[...] = jnp.zeros_like(o_ref)


def _experts(xb, block_e, next_e, block_rows, n_used, layer, wg, bg, wu, bu, wd, bd):
    n_slots = xb.shape[0]
    depth, n_e, d, d_exp = wg.shape
    assert d == d_exp
    tm = EXPERT_TILE
    n_blocks = n_slots // tm
    xmap = lambda i, be, ne, br, nb: (jnp.minimum(i, nb[0] - 1), 0)
    bmap = lambda i, be, ne, br, nb: (layer, be[i], 0, 0)
    hbm = pl.BlockSpec(memory_space=pl.ANY)
    grid_spec = pltpu.PrefetchScalarGridSpec(
        num_scalar_prefetch=4,
        grid=(n_blocks,),
        in_specs=[
            pl.BlockSpec((tm, d // 2), xmap),
            hbm,
            pl.BlockSpec((1, 1, 1, d_exp), bmap),
            hbm,
            pl.BlockSpec((1, 1, 1, d_exp), bmap),
            hbm,
            pl.BlockSpec((1, 1, 1, d), bmap),
        ],
        out_specs=pl.BlockSpec((tm, d // 2), lambda i, be, ne, br, nb: (i, 0)),
        scratch_shapes=[pltpu.VMEM((3, d, d_exp), F32), pltpu.VMEM((d, d_exp), BF16), pltpu.VMEM((d, d_exp), BF16),
                        pltpu.VMEM((d_exp, d), BF16), pltpu.VMEM((tm, d_exp), BF16), pltpu.SemaphoreType.DMA((3,))],
    )
    return pl.pallas_call(
        functools.partial(_expert_kernel, layer=layer, chunk=256),
        out_shape=jax.ShapeDtypeStruct((n_slots, d // 2), jnp.int32),
        grid_spec=grid_spec,
        compiler_params=_params(("arbitrary",)),
        name="experts",
    )(block_e, next_e, block_rows, n_used, xb, wg, bg.reshape(depth, n_e, 1, d_exp), wu,
      bu.reshape(depth, n_e, 1, d_exp), wd, bd.reshape(depth, n_e, 1, d))


def _combine_kernel(x_ref, g2_ref, gates_ref, y_ref, o_ref):
    gates = gates_ref[0]
    half = y_ref.shape[-1]
    acc_lo = jnp.zeros((x_ref.shape[1], half), F32)
    acc_hi = jnp.zeros((x_ref.shape[1], half), F32)
    for k in range(TOP_K):
        lo, hi = _unpack_rows(y_ref[k])
        acc_lo = acc_lo + gates[:, k:k + 1] * lo
        acc_hi = acc_hi + gates[:, k:k + 1] * hi
    o_ref[0] = x_ref[0] + g2_ref[0] * jnp.concatenate([acc_lo, acc_hi], axis=1)


def _combine(x, g2, gates, y_sel, tok_offset, batches=None):
    b, t, d = x.shape
    lo, hi = batches or (0, b)
    tm = int(np.gcd(min(t, 512), tok_offset)) if tok_offset else min(t, 512)
    n_t = t // tm
    off = tok_offset // tm
    row = lambda bi, i: (bi + lo, i, 0)
    return pl.pallas_call(
        _combine_kernel,
        out_shape=jax.ShapeDtypeStruct((b, t, d), F32),
        grid=(hi - lo, n_t),
        in_specs=[
            pl.BlockSpec((1, tm, d), row),
            pl.BlockSpec((1, 1, d), lambda bi, i: (bi + lo, 0, 0)),
            pl.BlockSpec((1, tm, ROUTER_PAD), row),
            pl.BlockSpec((TOP_K, tm, d // 2), lambda bi, i: (0, off + bi * n_t + i, 0)),
        ],
        out_specs=pl.BlockSpec((1, tm, d), row),
        input_output_aliases={0: 0},
        compiler_params=_params(("parallel", "parallel")),
        name="moe_combine",
    )(x, g2, gates, y_sel)


def _route_kernel(ids_ref, upper_ref, dest_ref, counts_ref, run_ref, *, tile):
    p = pl.program_id(0)
    i = pl.program_id(1)
    tm = ids_ref.shape[0]
    ids_t = ids_ref[...].T
    expert = lax.broadcasted_iota(jnp.int32, (N_EXPERTS, tm), 0)
    chosen = [ids_t[k:k + 1, :] == expert for k in range(TOP_K)]
    picks = functools.reduce(jnp.add, [c.astype(F32) for c in chosen])
    tile_counts = jnp.sum(picks, axis=1, keepdims=True)

    @pl.when((p == 0) & (i == 0))
    def _():
        run_ref[...] = jnp.zeros_like(run_ref)

    @pl.when(p == 0)
    def _():
        run_ref[...] += tile_counts

    @pl.when((p == 1) & (i == 0))
    def _():
        counts = run_ref[...]
        counts_ref[...] = counts.astype(jnp.int32)
        padded = jnp.floor((counts + (tile - 1)) * (1.0 / tile)) * tile
        row = lax.broadcasted_iota(jnp.int32, counts.shape, 0)
        ends = padded
        for s in (1, 2, 4, 8, 16):
            ends = ends + jnp.where(row >= s, pltpu.roll(ends, s, 0), 0.0)
        run_ref[...] = ends - padded

    @pl.when(p == 1)
    def _():
        before = _dot(picks.astype(BF16), upper_ref[...])
        slot = before + run_ref[:, 0:1]
        rows = [jnp.sum(jnp.where(c, slot, 0.0), axis=0, keepdims=True) for c in chosen]
        rows += [jnp.zeros_like(rows[0])] * (dest_ref.shape[0] - TOP_K)
        dest_ref[...] = jnp.concatenate(rows, axis=0).astype(jnp.int32)
        run_ref[...] += tile_counts


def _route(ids):
    n = ids.shape[0]
    tm = int(np.gcd(n, 1024))
    n_tiles = n // tm
    upper = jnp.asarray(np.triu(np.ones((tm, tm), np.float32), 1), BF16)
    dest, counts = pl.pallas_call(
        functools.partial(_route_kernel, tile=EXPERT_TILE),
        out_shape=[jax.ShapeDtypeStruct((SUBLANES, n), jnp.int32),
                   jax.ShapeDtypeStruct((N_EXPERTS, LANES), jnp.int32)],
        grid=(2, n_tiles),
        in_specs=[pl.BlockSpec((tm, ROUTER_PAD), lambda p, i: (i, 0)),
                  pl.BlockSpec((tm, tm), lambda p, i: (0, 0))],
        out_specs=[pl.BlockSpec((SUBLANES, tm), lambda p, i: (0, i * p)),
                   pl.BlockSpec((N_EXPERTS, LANES), lambda p, i: (0, 0))],
        scratch_shapes=[pltpu.VMEM((N_EXPERTS, LANES), F32)],
        compiler_params=_params(("arbitrary", "arbitrary")),
        name="route",
    )(ids, upper)
    return dest, counts[:, 0]


def _row_gather(table, idx):
    info = plsc.get_sparse_core_info()
    n_cores, n_workers = info.num_cores, info.num_cores * info.num_subcores
    n_rows, width = idx.shape[0], table.shape[1]
    chunk_rows = SC_GATHER_ROWS
    per_worker = n_rows // n_workers
    n_chunks = per_worker // chunk_rows
    assert per_worker * n_workers == n_rows and n_chunks * chunk_rows == per_worker and n_chunks % 2 == 0
    mesh = plsc.VectorSubcoreMesh(core_axis_name="c", subcore_axis_name="s")

    @functools.partial(
        pl.kernel, mesh=mesh,
        out_type=jax.ShapeDtypeStruct((n_rows, width), table.dtype),
        scratch_types=[pltpu.VMEM((n_chunks, chunk_rows), jnp.int32), pltpu.VMEM((2, chunk_rows, width), table.dtype),
                       pltpu.SemaphoreType.DMA((2,)), pltpu.SemaphoreType.DMA((2,))],
    )
    def gather_kernel(table_hbm, idx_hbm, out_hbm, idx_v, rows_v, gather_sem, write_sem):
        worker = lax.axis_index("s") * n_cores + lax.axis_index("c")
        pltpu.sync_copy(idx_hbm.at[worker], idx_v)

        def gather(c, slot):
            return pltpu.make_async_copy(table_hbm.at[idx_v.at[c]], rows_v.at[slot], gather_sem.at[slot])

        def write(c, slot):
            first_row = worker * per_worker + c * chunk_rows
            return pltpu.make_async_copy(rows_v.at[slot], out_hbm.at[pl.ds(first_row, chunk_rows)], write_sem.at[slot])

        gather(0, 0).start()

        @pl.loop(0, n_chunks, step=2)
        def _(c0):
            for slot in (0, 1):
                c = c0 + slot
                gather(c, slot).wait()

                @pl.when(c >= 1)
                def _():
                    write(c - 1, 1 - slot).wait()

                @pl.when(c + 1 < n_chunks)
                def _():
                    gather(c + 1, 1 - slot).start()

                write(c, slot).start()

        write(n_chunks - 1, 1).wait()

    return gather_kernel(table, idx.reshape(n_workers, n_chunks, chunk_rows))


def _row_scatter(rows, dest, n_out):
    info = plsc.get_sparse_core_info()
    n_cores, n_workers = info.num_cores, info.num_cores * info.num_subcores
    n_choices, n_rows = dest.shape
    width = rows.shape[1]
    chunk_rows = SC_GATHER_ROWS
    per_worker = n_rows // (n_workers * chunk_rows)
    assert per_worker * n_workers * chunk_rows == n_rows == rows.shape[0]
    idx = dest.reshape(n_choices, n_workers, per_worker, chunk_rows).transpose(1, 0, 2, 3)
    mesh = plsc.VectorSubcoreMesh(core_axis_name="c", subcore_axis_name="s")

    @functools.partial(
        pl.kernel, mesh=mesh,
        out_type=jax.ShapeDtypeStruct((n_out, width), rows.dtype),
        scratch_types=[pltpu.VMEM((n_choices, per_worker, chunk_rows), jnp.int32),
                       pltpu.VMEM((chunk_rows, width), rows.dtype), pltpu.SemaphoreType.DMA],
    )
    def scatter_kernel(rows_hbm, idx_hbm, out_hbm, idx_v, rows_v, sem):
        worker = lax.axis_index("s") * n_cores + lax.axis_index("c")
        pltpu.sync_copy(idx_hbm.at[worker], idx_v)

        @pl.loop(0, per_worker)
        def _(c):
            first_row = (worker * per_worker + c) * chunk_rows
            pltpu.sync_copy(rows_hbm.at[pl.ds(first_row, chunk_rows)], rows_v)
            copies = [pltpu.async_copy(rows_v, out_hbm.at[idx_v.at[k, c]], sem) for k in range(n_choices)]
            for cp in copies:
                cp.wait()

    return scatter_kernel(rows, idx)


def _moe(h2, ids, layer, w_exp):
    n = h2.shape[0]
    nk = n * TOP_K
    tm = EXPERT_TILE
    dest, counts = _route(ids)
    dest = dest[:TOP_K]
    padded = (counts + tm - 1) // tm * tm
    pends = jnp.cumsum(padded)
    n_blocks = -(-nk // tm) + N_EXPERTS
    n_slots = n_blocks * tm
    block_start = jnp.arange(n_blocks, dtype=jnp.int32) * tm
    block_e = jnp.minimum(jnp.sum((pends[None] <= block_start[:, None]).astype(jnp.int32), axis=1), N_EXPERTS - 1)
    is_e = (block_e[:, None] == jnp.arange(N_EXPERTS, dtype=jnp.int32)[None]).astype(jnp.int32)
    block_rows = jnp.clip(jnp.sum(is_e * (pends - padded + counts)[None], axis=1) - block_start, 0, tm)
    n_used = (pends[-1] // tm).astype(jnp.int32).reshape(1)
    e_ids = jnp.arange(N_EXPERTS, dtype=jnp.int32)
    later = (e_ids[None] > e_ids[:, None]) & (counts[None] > 0)
    next_nonempty = jnp.min(jnp.where(later, e_ids[None], N_EXPERTS), axis=1)
    next_nonempty = jnp.where(next_nonempty == N_EXPERTS, -1, next_nonempty)
    next_e = jnp.sum(is_e * next_nonempty[None], axis=1)
    xb = _row_scatter(h2, dest, n_slots)
    return _experts(xb, block_e, next_e, block_rows, n_used, layer, *w_exp), dest


def _gather_choices(yb, dest, lo, hi):
    return _row_gather(yb, dest[:, lo:hi].reshape(-1)).reshape(TOP_K, hi - lo, -1)


def _split_batch(b, t, lead):
    unit = 2 * SC_GATHER_ROWS * SC_WORKERS // TOP_K
    fits = [bs for bs in range(1, b) if (lead + bs * t) % unit == 0 and ((b - bs) * t) % unit == 0]
    return min(fits, key=lambda bs: abs(lead + bs * t - (b - bs) * t)) if fits else b


def _mod_parts(mod_l, b):
    d = mod_l.shape[-1] // 6
    lat = [mod_l[:b, k * d:(k + 1) * d].reshape(b, 1, d) for k in range(6)]
    ctx = [jnp.broadcast_to(mod_l[b, k * d:(k + 1) * d].reshape(1, 1, d), (b, 1, d)) for k in range(6)]
    return lat, ctx


def kernel(x, c, ctx, c_ctx, ada_w, ada_b, norm1_g, norm2_g, ev_w_in, ev_w_out, ev_q_gain, ev_k_gain, ev_rpb, ev_conv_w, ev_conv_b, od_w_in, od_w_out, od_conv_w, od_conv_b, od_fwd_wa, od_fwd_ba, od_fwd_wx, od_fwd_bx, od_fwd_lam, od_bwd_wa, od_bwd_ba, od_bwd_wx, od_bwd_bx, od_bwd_lam, router_w, router_b, exp_w_gate, exp_b_gate, exp_w_up, exp_b_up, exp_w_down, exp_b_down):
    b, t, d = x.shape
    l = ctx.shape[1]
    assert ada_w.shape[0] == DEPTH == 2 and t % GRID_W == 0 and t // GRID_W >= WIN_H

    n_rows_c = -(-(b + 1) // SUBLANES) * SUBLANES
    cvec = jnp.zeros((n_rows_c, d), F32).at[:b].set(c).at[b].set(c_ctx)
    mod = _ada_mod(cvec, ada_w, ada_b)

    def router(layer):
        w_r = jnp.zeros((d, ROUTER_PAD), F32).at[:, :N_EXPERTS].set(router_w[layer]).astype(BF16)
        b_r = jnp.zeros((1, ROUTER_PAD), F32).at[0, :N_EXPERTS].set(router_b[layer])
        return w_r, b_r

    w_exp = (exp_w_gate, exp_b_gate, exp_w_up, exp_b_up, exp_w_down, exp_b_down)

    (sh1, sc1, g1, sh2, sc2, g2), (csh1, csc1, cg1, csh2, csc2, cg2) = _mod_parts(mod[0], b)
    n1 = norm1_g[0].reshape(1, d)
    n2 = norm2_g[0].reshape(1, d)
    w_in = ev_w_in[0].astype(BF16)
    w_out = ev_w_out[0].astype(BF16)
    q_scale = NA_HEAD_DIM ** -0.5 * LOG2_E
    head_gain = jnp.stack([jnp.tile(ev_q_gain[0] * q_scale, NA_HEADS), jnp.tile(ev_k_gain[0], NA_HEADS)])
    head_gain = head_gain.reshape(2, 1, NA_WIDTH).astype(F32)
    proj = _inproj(x, n1, sh1, sc1, w_in, head_gain)
    proj_c = _inproj(ctx, n1, csh1, csc1, w_in, head_gain)
    kh, r0, row_type, patterns = _na_tables(t // GRID_W)
    bias_tab = _na_bias_table(ev_rpb[0], patterns)
    o_a = _neighbourhood_attention(proj, proj_c, bias_tab, r0, row_type, kh)
    oc_a = _context_attention(proj_c)
    w_r, b_r = router(0)
    x1, h2, ids, gates = _even_out(o_a, proj, ev_conv_w[0], ev_conv_b[0], w_out, x, g1, n2, sh2, sc2, w_r, b_r)
    c1, h2c, ids_c, gates_c = _even_out(oc_a, proj_c, ev_conv_w[0], ev_conv_b[0], w_out, ctx, cg1, n2, csh2, csc2,
                                        w_r, b_r)
    tokens = jnp.concatenate([h2c.reshape(b * l, d // 2), h2.reshape(b * t, d // 2)], axis=0)
    ids_all = jnp.concatenate([ids_c.reshape(b * l, ROUTER_PAD), ids.reshape(b * t, ROUTER_PAD)], axis=0)
    yb, dest = _moe(tokens, ids_all, 0, w_exp)
    bs = _split_batch(b, t, b * l)
    n_first = b * l + bs * t
    y_first = _gather_choices(yb, dest, 0, n_first)
    hctx = _combine(c1, cg2, gates_c, y_first, 0)
    x = _combine(x1, g2, gates, y_first, b * l, (0, bs))
    if bs < b:
        x = _combine(x, g2, gates, _gather_choices(yb, dest, n_first, b * (l + t)), 0, (bs, b))

    (sh1, sc1, g1, sh2, sc2, g2), (csh1, csc1, _, _, _, _) = _mod_parts(mod[1], b)
    n1 = norm1_g[1].reshape(1, d)
    n2 = norm2_g[1].reshape(1, d)
    w_in = od_w_in[0].astype(BF16)
    width = w_in.shape[1] // 2
    proj = _inproj(x, n1, sh1, sc1, w_in)
    u_ctx = _inproj(hctx, n1, csh1, csc1, w_in[:, width:])
    h_dir = []
    for reverse, (wa, ba, wx, bx, lam) in ((False, (od_fwd_wa, od_fwd_ba, od_fwd_wx, od_fwd_bx, od_fwd_lam)),
                                           (True, (od_bwd_wa, od_bwd_ba, od_bwd_wx, od_bwd_bx, od_bwd_lam))):
        w_cat = jnp.concatenate([wa[0], wx[0]], axis=-1).astype(BF16)
        h_dir.append(_lru_scan(proj, u_ctx, od_conv_w[0], od_conv_b[0], w_cat, ba[0], bx[0], lam[0], reverse))
    w_r, b_r = router(1)
    x1, h2, ids, gates = _odd_out(h_dir[0], h_dir[1], proj, od_w_out[0].astype(BF16), x, g1, n2, sh2, sc2, w_r, b_r)
    yb, dest = _moe(h2.reshape(b * t, d // 2), ids.reshape(b * t, ROUTER_PAD), 1, w_exp)
    bs = _split_batch(b, t, 0)
    x = _combine(x1, g2, gates, _gather_choices(yb, dest, 0, bs * t), 0, (0, bs))
    if bs < b:
        x = _combine(x, g2, gates, _gather_choices(yb, dest, bs * t, b * t), 0, (bs, b))
    return x
```

```python
import functools

import numpy as np
import jax
import jax.numpy as jnp
from jax import lax
from jax.experimental import pallas as pl
from jax.experimental.pallas import tpu as pltpu
from jax.experimental.pallas import tpu_sc as plsc

DEPTH = 2
GRID_W = 64
EPS = 1e-6
NEG_INF = -1e30
LOG2_E = 1.4426950408889634
NA_HEADS = 8
NA_HEAD_DIM = 64
NA_WIDTH = NA_HEADS * NA_HEAD_DIM
HEAD_PAIRS = NA_HEADS // 2
WIN_H = 8
WIN_W = 16
SC_CONV = 3
LRU_BLOCKS = 4
LRU_CONV = 4
LRU_C = 8.0
N_EXPERTS = 32
TOP_K = 4
SWIGLU_LIMIT = 7.0
SWIGLU_ALPHA = 1.702

LANES = 128
SUBLANES = 8
HALO = 16
ROUTER_PAD = LANES
NA_ROWS_PER_STEP = 4
OUT_TILE = 1024
OUT_SUB_TILE = 512
EXPERT_TILE = 512
SC_GATHER_ROWS = 64
SC_WORKERS = 32
VMEM_LIMIT = 56 * 1024 * 1024

F32 = jnp.float32
BF16 = jnp.bfloat16


def _params(sem, vmem=VMEM_LIMIT):
    return pltpu.CompilerParams(dimension_semantics=sem, vmem_limit_bytes=vmem)


def _dot(a, b):
    return jnp.dot(a, b, preferred_element_type=F32)


def _dot_nt(a, b):
    return lax.dot_general(a, b, (((1,), (1,)), ((), ())), preferred_element_type=F32)


def _pack_rows(v):
    w = v.shape[-1] // 2
    lo = lax.bitcast_convert_type(v[:, :w].astype(BF16).astype(F32), jnp.int32)
    hi = lax.bitcast_convert_type(v[:, w:].astype(BF16).astype(F32), jnp.int32)
    return lax.shift_right_logical(lo, 16) | (hi & jnp.int32(-65536))


def _unpack_rows(p):
    lo = lax.bitcast_convert_type(lax.shift_left(p, 16), F32)
    hi = lax.bitcast_convert_type(p & jnp.int32(-65536), F32)
    return lo, hi


def _rms_mod(x, g, shift, scale):
    ms = jnp.mean(x * x, axis=-1, keepdims=True)
    y = x * lax.rsqrt(ms + EPS) * g
    return y * (1.0 + scale) + shift


def _ada_kernel(c_ref, w_ref, b_ref, o_ref):
    c = c_ref[...]
    s = (c * jax.nn.sigmoid(c)).astype(BF16)
    o_ref[0] = _dot(s, w_ref[0].astype(BF16)) + b_ref[0]


def _ada_mod(cvec, ada_w, ada_b):
    depth, d, n = ada_w.shape
    r = cvec.shape[0]
    tn = 1536
    return pl.pallas_call(
        _ada_kernel,
        out_shape=jax.ShapeDtypeStruct((depth, r, n), F32),
        grid=(depth, n // tn),
        in_specs=[
            pl.BlockSpec((r, d), lambda l, j: (0, 0)),
            pl.BlockSpec((1, d, tn), lambda l, j: (l, 0, j)),
            pl.BlockSpec((1, 1, tn), lambda l, j: (l, 0, j)),
        ],
        out_specs=pl.BlockSpec((1, r, tn), lambda l, j: (l, 0, j)),
        compiler_params=_params(("parallel", "parallel")),
        name="ada_mod",
    )(cvec, ada_w, ada_b.reshape(depth, 1, n))


def _inproj_kernel(x_ref, g_ref, sh_ref, sc_ref, w_ref, hg_ref, ones_ref, o_ref, *, n_tiles, tn, n_headnorm):
    h = _rms_mod(x_ref[0], g_ref[...], sh_ref[0], sc_ref[0]).astype(BF16)
    for j in range(n_tiles):
        y = _dot(h, w_ref[:, j * tn:(j + 1) * tn])
        if j < n_headnorm:
            ms = _dot((y * y).astype(BF16), ones_ref[...]) * (1.0 / NA_HEAD_DIM)
            y = y * lax.rsqrt(ms + EPS) * hg_ref[j]
        o_ref[0, :, j * tn:(j + 1) * tn] = y.astype(o_ref.dtype)


def _inproj(x, g, shift, scale, w, head_gain=None):
    b, t, d = x.shape
    n = w.shape[1]
    tn = NA_WIDTH
    tm = min(t, 512)
    n_headnorm = 0 if head_gain is None else head_gain.shape[0]
    if head_gain is None:
        head_gain = jnp.ones((1, 1, tn), F32)
    hid = np.arange(tn) // NA_HEAD_DIM
    ones_bd = jnp.asarray((hid[:, None] == hid[None, :]), BF16)
    kern = functools.partial(_inproj_kernel, n_tiles=n // tn, tn=tn, n_headnorm=n_headnorm)
    return pl.pallas_call(
        kern,
        out_shape=jax.ShapeDtypeStruct((b, t, n), BF16),
        grid=(b, t // tm),
        in_specs=[
            pl.BlockSpec((1, tm, d), lambda bi, i: (bi, i, 0)),
            pl.BlockSpec((1, d), lambda bi, i: (0, 0)),
            pl.BlockSpec((1, 1, d), lambda bi, i: (bi, 0, 0)),
            pl.BlockSpec((1, 1, d), lambda bi, i: (bi, 0, 0)),
            pl.BlockSpec((d, n), lambda bi, i: (0, 0)),
            pl.BlockSpec(head_gain.shape, lambda bi, i: (0, 0, 0)),
            pl.BlockSpec((tn, tn), lambda bi, i: (0, 0)),
        ],
        out_specs=pl.BlockSpec((1, tm, n), lambda bi, i: (bi, i, 0)),
        compiler_params=_params(("parallel", "parallel")),
        name="inproj",
    )(x, g, shift, scale, w, head_gain, ones_bd)


def _inproj_conv_kernel(x_ref, xp_ref, xn_ref, g_ref, sh_ref, sc_ref, w_ref, cw_ref, cb_ref, o_ref, *,
                        n_tiles, n_plain, tn, tm):
    i = pl.program_id(1)
    edge = SUBLANES
    x_ext = jnp.concatenate([xp_ref[0], x_ref[0], xn_ref[0]], axis=0)
    h = _rms_mod(x_ext, g_ref[...], sh_ref[0], sc_ref[0]).astype(BF16)
    keep_prev = jnp.where(i > 0, 1.0, 0.0)
    keep_next = jnp.where(i < pl.num_programs(1) - 1, 1.0, 0.0)
    for j in list(range(n_plain, n_tiles)) + list(range(n_plain)):
        y = _dot(h, w_ref[:, j * tn:(j + 1) * tn])
        if j < n_plain:
            o_ref[0, :, j * tn:(j + 1) * tn] = y[edge:edge + tm].astype(o_ref.dtype)
            continue
        u = jnp.concatenate([y[:edge] * keep_prev, y[edge:edge + tm], y[edge + tm:] * keep_next], axis=0)
        cw = cw_ref[:, (j - n_plain) * tn:(j - n_plain + 1) * tn]
        left = (LRU_CONV - 1) // 2
        uc = cb_ref[:, (j - n_plain) * tn:(j - n_plain + 1) * tn]
        n_ext = u.shape[0]
        for k in range(LRU_CONV):
            shifted = u if k == left else pltpu.roll(u, (left - k) % n_ext, 0)
            uc = uc + shifted[edge:edge + tm] * cw[k:k + 1]
        o_ref[0, :, j * tn:(j + 1) * tn] = uc.astype(o_ref.dtype)


def _inproj_conv(x, g, shift, scale, w, conv_w, conv_b, n_plain_cols):
    b, t, d = x.shape
    n = w.shape[1]
    tn = NA_WIDTH
    tm = min(t, 512)
    hb = tm // SUBLANES
    n_hblocks = t // SUBLANES
    kern = functools.partial(_inproj_conv_kernel, n_tiles=n // tn, n_plain=n_plain_cols // tn, tn=tn, tm=tm)
    const = lambda bi, i: (0, 0)
    return pl.pallas_call(
        kern,
        out_shape=jax.ShapeDtypeStruct((b, t, n), BF16),
        grid=(b, t // tm),
        in_specs=[
            pl.BlockSpec((1, tm, d), lambda bi, i: (bi, i, 0)),
            pl.BlockSpec((1, SUBLANES, d), lambda bi, i: (bi, jnp.maximum(i * hb - 1, 0), 0)),
            pl.BlockSpec((1, SUBLANES, d), lambda bi, i: (bi, jnp.minimum((i + 1) * hb, n_hblocks - 1), 0)),
            pl.BlockSpec((1, d), const),
            pl.BlockSpec((1, 1, d), lambda bi, i: (bi, 0, 0)),
            pl.BlockSpec((1, 1, d), lambda bi, i: (bi, 0, 0)),
            pl.BlockSpec((d, n), const),
            pl.BlockSpec(conv_w.shape, const),
            pl.BlockSpec((1, conv_w.shape[1]), const),
        ],
        out_specs=pl.BlockSpec((1, tm, n), lambda bi, i: (bi, i, 0)),
        compiler_params=_params(("parallel", "parallel")),
        name="inproj_conv",
    )(x, x, x, g, shift, scale, w, conv_w, conv_b.reshape(1, -1))


def _na_tables(rows):
    kh = min(WIN_H, rows)
    r = np.arange(rows)
    r0 = np.clip(r - kh // 2, 0, rows - kh)
    dr = r0[:, None] + np.arange(kh)[None] - r[:, None] + WIN_H - 1
    patterns, row_type = np.unique(dr, axis=0, return_inverse=True)
    return kh, r0.astype(np.int32), row_type.reshape(-1).astype(np.int32), patterns


def _na_bias_table(rpb, patterns):
    qc = np.arange(GRID_W)
    kc = np.arange(GRID_W)
    c0 = np.clip(qc - WIN_W // 2, 0, GRID_W - WIN_W)[:, None]
    valid = (kc[None] >= c0) & (kc[None] < c0 + WIN_W)
    dc = np.clip(kc[None] - qc[:, None] + WIN_W - 1, 0, 2 * WIN_W - 2)
    n_pat, kh = patterns.shape
    onehot_dc = jnp.asarray(dc[None] == np.arange(2 * WIN_W - 1)[:, None, None], F32)
    tab = jnp.einsum('hpic,cqk->hpiqk', rpb.astype(F32)[:, patterns], onehot_dc,
                     precision=lax.Precision.HIGHEST)
    tab = jnp.where(valid[None, None, None], tab * LOG2_E, NEG_INF)
    tab = tab.reshape(HEAD_PAIRS, 2, n_pat, kh, GRID_W, GRID_W)
    tab = tab.transpose(2, 0, 1, 4, 3, 5)
    return tab.reshape(n_pat, HEAD_PAIRS, 2 * GRID_W, kh * GRID_W)


def _pair_attention(q, keys, values, biases):
    m = q.shape[0]
    qq = _stack_heads(q)
    scores = []
    for k, bias in zip(keys, biases):
        s = _dot_nt(qq, k)
        scores.append(s if bias is None else s + bias)
    s = jnp.concatenate(scores, axis=1)
    e = jnp.exp2(s - jnp.max(s, axis=-1, keepdims=True))
    denom = jnp.sum(e, axis=-1, keepdims=True)
    e = e.astype(BF16)
    o, start = 0.0, 0
    for v in values:
        o = o + _dot(e[:, start:start + v.shape[0]], v)
        start += v.shape[0]
    o = o * (1.0 / denom)
    lane_o = lax.broadcasted_iota(jnp.int32, (m, LANES), 1)
    return jnp.where(lane_o < NA_HEAD_DIM, o[:m], o[m:])


def _stack_heads(q):
    lane = lax.broadcasted_iota(jnp.int32, q.shape, 1)
    zero = jnp.zeros_like(q)
    return jnp.concatenate([jnp.where(lane < NA_HEAD_DIM, q, zero), jnp.where(lane >= NA_HEAD_DIM, q, zero)], axis=0)


def _na_kernel(r0_ref, type_ref, q_ref, k_ref, v_ref, kc_ref, vc_ref, bias_ref, o_ref, s_ref, p_ref, *,
               kh, rows_per_step):
    n_lat = kh * GRID_W
    tiles = [(j, p) for j in range(rows_per_step) for p in range(HEAD_PAIRS)]
    window = []
    for j in range(rows_per_step):
        r = pl.program_id(1) * rows_per_step + j
        window.append((pl.multiple_of(r0_ref[r] * GRID_W, GRID_W), type_ref[r]))

    for idx, (j, p) in enumerate(tiles):
        start, rtype = window[j]
        cols = slice(p * LANES, (p + 1) * LANES)
        qq = _stack_heads(q_ref[0, j * GRID_W:(j + 1) * GRID_W, cols])
        s_ref[idx, :, :n_lat] = _dot_nt(qq, k_ref[0, pl.ds(start, n_lat), cols]) + bias_ref[rtype, p]
        s_ref[idx, :, n_lat:] = _dot_nt(qq, kc_ref[0, :, cols])

    denoms = []
    for idx in range(len(tiles)):
        s = s_ref[idx]
        e = jnp.exp2(s - jnp.max(s, axis=-1, keepdims=True))
        denoms.append(jnp.sum(e, axis=-1, keepdims=True))
        p_ref[idx] = e.astype(BF16)

    lane = lax.broadcasted_iota(jnp.int32, (GRID_W, LANES), 1)
    for idx, (j, p) in enumerate(tiles):
        start, _ = window[j]
        cols = slice(p * LANES, (p + 1) * LANES)
        o = _dot(p_ref[idx, :, :n_lat], v_ref[0, pl.ds(start, n_lat), cols]) + _dot(p_ref[idx, :, n_lat:],
                                                                                 vc_ref[0, :, cols])
        o = o * (1.0 / denoms[idx])
        o = jnp.where(lane < NA_HEAD_DIM, o[:GRID_W], o[GRID_W:])
        o_ref[0, j * GRID_W:(j + 1) * GRID_W, cols] = o.astype(o_ref.dtype)


def _neighbourhood_attention(proj, proj_c, bias_tab, r0, row_type, kh):
    b, t, _ = proj.shape
    l = proj_c.shape[1]
    rows = t // GRID_W
    w = NA_WIDTH
    rps = int(np.gcd(rows, NA_ROWS_PER_STEP))
    q_rows = rps * GRID_W
    grid_spec = pltpu.PrefetchScalarGridSpec(
        num_scalar_prefetch=2,
        grid=(b, rows // rps),
        in_specs=[
            pl.BlockSpec((1, q_rows, w), lambda bi, r, *_: (bi, r, 0)),
            pl.BlockSpec((1, t, w), lambda bi, r, *_: (bi, 0, 1)),
            pl.BlockSpec((1, t, w), lambda bi, r, *_: (bi, 0, 2)),
            pl.BlockSpec((1, l, w), lambda bi, r, *_: (bi, 0, 1)),
            pl.BlockSpec((1, l, w), lambda bi, r, *_: (bi, 0, 2)),
            pl.BlockSpec(bias_tab.shape, lambda bi, r, *_: (0, 0, 0, 0)),
        ],
        out_specs=pl.BlockSpec((1, q_rows, w), lambda bi, r, *_: (bi, r, 0)),
        scratch_shapes=[pltpu.VMEM((rps * HEAD_PAIRS, 2 * GRID_W, kh * GRID_W + l), F32),
                        pltpu.VMEM((rps * HEAD_PAIRS, 2 * GRID_W, kh * GRID_W + l), BF16)],
    )
    return pl.pallas_call(
        functools.partial(_na_kernel, kh=kh, rows_per_step=rps),
        out_shape=jax.ShapeDtypeStruct((b, t, w), BF16),
        grid_spec=grid_spec,
        compiler_params=_params(("parallel", "arbitrary")),
        name="na_attention",
    )(jnp.asarray(r0), jnp.asarray(row_type), proj, proj, proj, proj_c, proj_c, bias_tab)


def _ctx_attn_kernel(q_ref, k_ref, v_ref, o_ref):
    for p in range(HEAD_PAIRS):
        cols = slice(p * LANES, (p + 1) * LANES)
        o = _pair_attention(q_ref[0, :, cols], [k_ref[0, :, cols]], [v_ref[0, :, cols]], [None])
        o_ref[0, :, cols] = o.astype(o_ref.dtype)


def _context_attention(proj_c):
    b, l, _ = proj_c.shape
    w = NA_WIDTH
    return pl.pallas_call(
        _ctx_attn_kernel,
        out_shape=jax.ShapeDtypeStruct((b, l, w), BF16),
        grid=(b,),
        in_specs=[pl.BlockSpec((1, l, w), lambda bi, j=j: (bi, 0, j)) for j in range(3)],
        out_specs=pl.BlockSpec((1, l, w), lambda bi: (bi, 0, 0)),
        compiler_params=_params(("parallel",)),
        name="ctx_attention",
    )(proj_c, proj_c, proj_c)


def _top4(logits):
    lane = lax.broadcasted_iota(jnp.int32, logits.shape, 1)
    cur = jnp.where(lane < N_EXPERTS, logits, -jnp.inf)
    vals, idxs = [], []
    for _ in range(TOP_K):
        m = jnp.max(cur, axis=-1, keepdims=True)
        first = jnp.min(jnp.where(cur == m, lane, ROUTER_PAD).astype(F32), axis=-1, keepdims=True)
        idx = first.astype(jnp.int32)
        vals.append(m)
        idxs.append(idx)
        cur = jnp.where(lane == idx, -jnp.inf, cur)
    exps = [jnp.exp(v - vals[0]) for v in vals]
    inv = 1.0 / functools.reduce(jnp.add, exps)
    ids = jnp.zeros(logits.shape, jnp.int32)
    gates = jnp.zeros(logits.shape, F32)
    for k in range(TOP_K):
        ids = jnp.where(lane == k, idxs[k], ids)
        gates = jnp.where(lane == k, exps[k] * inv, gates)
    return ids, gates


def _layer_tail(y, rows, x_ref, g1_ref, n2_ref, sh2_ref, sc2_ref, wr_ref, br_ref, xo_ref, h2_ref, ids_ref, gates_ref):
    x_new = x_ref[0, rows] + g1_ref[0] * y
    xo_ref[0, rows] = x_new
    h2 = _rms_mod(x_new, n2_ref[...], sh2_ref[0], sc2_ref[0])
    h2_ref[0, rows] = _pack_rows(h2)
    logits = _dot(h2.astype(BF16), wr_ref[...]) + br_ref[...]
    ids, gates = _top4(logits)
    ids_ref[0, rows] = ids
    gates_ref[0, rows] = gates


def _sub_tiles(n_rows):
    sub = min(OUT_SUB_TILE, n_rows)
    return [slice(s, s + sub) for s in range(0, n_rows, sub)]


def _tail_specs(b, t, d, tm):
    row = lambda bi, i: (bi, i, 0)
    per_b = lambda bi, i: (bi, 0, 0)
    const = lambda bi, i: (0, 0)
    in_specs = [
        pl.BlockSpec((1, tm, d), row),
        pl.BlockSpec((1, 1, d), per_b),
        pl.BlockSpec((1, d), const),
        pl.BlockSpec((1, 1, d), per_b),
        pl.BlockSpec((1, 1, d), per_b),
        pl.BlockSpec((d, ROUTER_PAD), const),
        pl.BlockSpec((1, ROUTER_PAD), const),
    ]
    out_specs = [
        pl.BlockSpec((1, tm, d), row),
        pl.BlockSpec((1, tm, d // 2), row),
        pl.BlockSpec((1, tm, ROUTER_PAD), row),
        pl.BlockSpec((1, tm, ROUTER_PAD), row),
    ]
    out_shape = [
        jax.ShapeDtypeStruct((b, t, d), F32),
        jax.ShapeDtypeStruct((b, t, d // 2), jnp.int32),
        jax.ShapeDtypeStruct((b, t, ROUTER_PAD), jnp.int32),
        jax.ShapeDtypeStruct((b, t, ROUTER_PAD), F32),
    ]
    return in_specs, out_specs, out_shape


def _halo_fix(rolled, at_row, halo_row, present):
    n = rolled.shape[0]
    first = at_row < SUBLANES
    assert first or at_row >= n - SUBLANES
    slab = rolled[:SUBLANES] if first else rolled[n - SUBLANES:]
    sub = lax.broadcasted_iota(jnp.int32, slab.shape, 0)
    fill = jnp.where(present, halo_row, jnp.zeros_like(halo_row))
    slab = jnp.where(sub == at_row % SUBLANES, fill, slab)
    return jnp.concatenate([slab, rolled[SUBLANES:]] if first else [rolled[:n - SUBLANES], slab], axis=0)


def _even_out_kernel(oa_ref, bg_ref, cg_ref, xin_ref, cgp_ref, xinp_ref, cgn_ref, xinn_ref, cw_ref, cb_ref,
                     wa_ref, wb_ref, *tail_refs, tm):
    i = pl.program_id(1)
    has_prev = i > 0
    has_next = i < pl.num_programs(1) - 1
    u = cg_ref[0].astype(F32) * xin_ref[0].astype(F32)
    u_prev = (cgp_ref[0].astype(F32) * xinp_ref[0].astype(F32))[HALO - 1:HALO]
    u_next = (cgn_ref[0].astype(F32) * xinn_ref[0].astype(F32))[0:1]
    u_m1 = _halo_fix(pltpu.roll(u, 1, 0), 0, u_prev, has_prev)
    u_p1 = _halo_fix(pltpu.roll(u, tm - 1, 0), tm - 1, u_next, has_next)
    cw = cw_ref[...]
    conv = u_m1 * cw[0:1] + u * cw[1:2] + u_p1 * cw[2:3] + cb_ref[...]
    o_b = (bg_ref[0].astype(F32) * conv).astype(BF16)
    for rows in _sub_tiles(tm):
        y = _dot(oa_ref[0, rows], wa_ref[...]) + _dot(o_b[rows], wb_ref[...])
        _layer_tail(y, rows, *tail_refs)


def _even_out(o_a, proj, conv_w, conv_b, w_out, x, g1, n2, sh2, sc2, w_r, b_r):
    b, t, d = x.shape
    w = NA_WIDTH
    tm = min(t, OUT_TILE)
    hb = tm // HALO
    n_hblocks = t // HALO
    row = lambda bi, i: (bi, i, 0)
    const = lambda bi, i: (0, 0)
    prev = lambda col: (lambda bi, i: (bi, jnp.maximum(i * hb - 1, 0), col))
    nxt = lambda col: (lambda bi, i: (bi, jnp.minimum((i + 1) * hb, n_hblocks - 1), col))
    tail_in, out_specs, out_shape = _tail_specs(b, t, d, tm)
    in_specs = [
        pl.BlockSpec((1, tm, w), row),
        pl.BlockSpec((1, tm, w), lambda bi, i: (bi, i, 3)),
        pl.BlockSpec((1, tm, w), lambda bi, i: (bi, i, 4)),
        pl.BlockSpec((1, tm, w), lambda bi, i: (bi, i, 5)),
        pl.BlockSpec((1, HALO, w), prev(4)),
        pl.BlockSpec((1, HALO, w), prev(5)),
        pl.BlockSpec((1, HALO, w), nxt(4)),
        pl.BlockSpec((1, HALO, w), nxt(5)),
        pl.BlockSpec((SC_CONV, w), const),
        pl.BlockSpec((1, w), const),
        pl.BlockSpec((w, d), const),
        pl.BlockSpec((w, d), const),
    ] + tail_in
    return pl.pallas_call(
        functools.partial(_even_out_kernel, tm=tm),
        out_shape=out_shape,
        grid=(b, t // tm),
        in_specs=in_specs,
        out_specs=out_specs,
        compiler_params=_params(("parallel", "parallel")),
        name="even_out",
    )(o_a, proj, proj, proj, proj, proj, proj, proj, conv_w, conv_b.reshape(1, w),
      w_out[:w], w_out[w:], x, g1, n2, sh2, sc2, w_r, b_r)


def _log_sigmoid(x):
    return jnp.minimum(x, 0.0) - jnp.log1p(jnp.exp(-jnp.abs(x)))


def _lru_tile(u_ref, w_ref, ba_ref, bx_ref, lam_ref, carry_ref, o_ref, *, ts, reverse):
    width = u_ref.shape[-1]
    blk = width // LRU_BLOCKS
    ucb = u_ref[0]
    uc = ucb.astype(F32)
    za, zx = [], []
    for h in range(LRU_BLOCKS):
        z = _dot(ucb[:, h * blk:(h + 1) * blk], w_ref[h])
        za.append(z[:, :blk])
        zx.append(z[:, blk:])
    r = jax.nn.sigmoid(jnp.concatenate(za, axis=1) + ba_ref[...])
    gate_i = jax.nn.sigmoid(jnp.concatenate(zx, axis=1) + bx_ref[...])
    log_a = (LRU_C * _log_sigmoid(lam_ref[...])) * r
    a = jnp.exp(log_a)
    th = jnp.tanh(log_a)
    m2 = -2.0 * th / (1.0 - th)
    mult = jnp.where(m2 > 0.0, m2 * lax.rsqrt(m2), 0.0)
    bcoef = mult * gate_i * uc

    groups = ts // SUBLANES
    a3 = a.reshape(groups, SUBLANES, width)
    b3 = bcoef.reshape(groups, SUBLANES, width)
    sub = lax.broadcasted_iota(jnp.int32, a3.shape, 1)
    for s in (1, 2, 4):
        shift = SUBLANES - s if reverse else s
        a_sh = pltpu.roll(a3, shift, 1)
        b_sh = pltpu.roll(b3, shift, 1)
        m = (sub < SUBLANES - s) if reverse else (sub >= s)
        b3 = jnp.where(m, a3 * b_sh + b3, b3)
        a3 = jnp.where(m, a3 * a_sh, a3)

    h = carry_ref[0:1, :]
    order = range(groups - 1, -1, -1) if reverse else range(groups)
    edge = 0 if reverse else SUBLANES - 1
    for g in order:
        hg = a3[g] * h + b3[g]
        if o_ref is not None:
            o_ref[0, g * SUBLANES:(g + 1) * SUBLANES, :] = hg.astype(o_ref.dtype)
        h = hg[edge:edge + 1, :]
    carry_ref[...] = jnp.broadcast_to(h, carry_ref.shape)


def _lru_kernel(uc_ref, ul_ref, w_ref, ba_ref, bx_ref, lam_ref, o_ref, carry_ref, *, ts, n_ctx_tiles, reverse):
    j = pl.program_id(1)
    shared = (w_ref, ba_ref, bx_ref, lam_ref, carry_ref)

    @pl.when(j == 0)
    def _():
        carry_ref[...] = jnp.zeros_like(carry_ref)

    @pl.when(j < n_ctx_tiles)
    def _():
        _lru_tile(uc_ref, *shared, None, ts=ts, reverse=reverse)

    @pl.when(j >= n_ctx_tiles)
    def _():
        _lru_tile(ul_ref, *shared, o_ref, ts=ts, reverse=reverse)


def _lru_scan(proj, u_ctx, w_cat, ba, bx, lam, reverse):
    b, t, _ = proj.shape
    l, width = u_ctx.shape[1], u_ctx.shape[2]
    ts = min(256, l, t)
    n_c, n_l = l // ts, t // ts

    def pos_of(step, n):
        step = jnp.clip(step, 0, n - 1)
        return (n - 1 - step) if reverse else step

    def tile_map(off, n, col):
        return lambda bi, j: (bi, pos_of(j - off, n), col)

    const2 = lambda bi, j: (0, 0)
    in_specs = [
        pl.BlockSpec((1, ts, width), tile_map(0, n_c, 0)),
        pl.BlockSpec((1, ts, width), tile_map(n_c, n_l, 1)),
        pl.BlockSpec(w_cat.shape, lambda bi, j: (0, 0, 0)),
        pl.BlockSpec((1, width), const2),
        pl.BlockSpec((1, width), const2),
        pl.BlockSpec((1, width), const2),
    ]
    kern = functools.partial(_lru_kernel, ts=ts, n_ctx_tiles=n_c, reverse=reverse)
    return pl.pallas_call(
        kern,
        out_shape=jax.ShapeDtypeStruct((b, t, width), BF16),
        grid=(b, n_c + n_l),
        in_specs=in_specs,
        out_specs=pl.BlockSpec((1, ts, width), tile_map(n_c, n_l, 0)),
        scratch_shapes=[pltpu.VMEM((SUBLANES, width), F32)],
        compiler_params=_params(("parallel", "arbitrary")),
        name="lru_scan_bwd" if reverse else "lru_scan_fwd",
    )(u_ctx, proj, w_cat, ba.reshape(1, width), bx.reshape(1, width), lam.reshape(1, width))


def _odd_out_kernel(hf_ref, hb_ref, gate_ref, w_ref, *tail_refs):
    for rows in _sub_tiles(hf_ref.shape[1]):
        hsum = hf_ref[0, rows].astype(F32) + hb_ref[0, rows].astype(F32)
        z = hsum * jax.nn.gelu(gate_ref[0, rows].astype(F32), approximate=True)
        y = _dot(z.astype(BF16), w_ref[...])
        _layer_tail(y, rows, *tail_refs)


def _odd_out(h_f, h_b, proj, w_out, x, g1, n2, sh2, sc2, w_r, b_r):
    b, t, d = x.shape
    width = h_f.shape[-1]
    tm = min(t, OUT_TILE)
    row = lambda bi, i: (bi, i, 0)
    tail_in, out_specs, out_shape = _tail_specs(b, t, d, tm)
    in_specs = [
        pl.BlockSpec((1, tm, width), row),
        pl.BlockSpec((1, tm, width), row),
        pl.BlockSpec((1, tm, width), row),
        pl.BlockSpec((width, d), lambda bi, i: (0, 0)),
    ] + tail_in
    return pl.pallas_call(
        _odd_out_kernel,
        out_shape=out_shape,
        grid=(b, t // tm),
        in_specs=in_specs,
        out_specs=out_specs,
        compiler_params=_params(("parallel", "parallel")),
        name="odd_out",
    )(h_f, h_b, proj, w_out, x, g1, n2, sh2, sc2, w_r, b_r)


def _expert_kernel(be_ref, next_ref, rows_ref, nb_ref, x_ref, wg_hbm, bg_ref, wu_hbm, bu_ref, wd_hbm, bd_ref,
                   o_ref, w_f32, wg_bf, wu_bf, wd_bf, h_bf, sem, *, layer, chunk):
    i = pl.program_id(0)
    expert = be_ref[i]
    used = i < nb_ref[0]

    def weight_copies(e):
        return [pltpu.make_async_copy(src.at[layer, e], w_f32.at[j], sem.at[j])
                for j, src in enumerate((wg_hbm, wu_hbm, wd_hbm))]

    @pl.when(i == 0)
    def _():
        for cp in weight_copies(expert):
            cp.start()

    @pl.when(used & ((i == 0) | (expert != be_ref[jnp.maximum(i - 1, 0)])))
    def _():
        for cp in weight_copies(expert):
            cp.wait()
        wg_bf[...] = w_f32[0].astype(BF16)
        wu_bf[...] = w_f32[1].astype(BF16)
        wd_bf[...] = w_f32[2].astype(BF16)

        @pl.when(next_ref[i] >= 0)
        def _():
            for cp in weight_copies(next_ref[i]):
                cp.start()

    @pl.when(used)
    def _():
        xp = x_ref[...]
        row = lax.broadcasted_iota(jnp.int32, xp.shape, 0)
        xp = jnp.where(row < rows_ref[i], xp, 0)
        x = jnp.concatenate(_unpack_rows(xp), axis=1).astype(BF16)
        for c in range(wg_bf.shape[1] // chunk):
            cs = slice(c * chunk, (c + 1) * chunk)
            g = jnp.minimum(_dot(x, wg_bf[:, cs]) + bg_ref[0, 0, :, cs], SWIGLU_LIMIT)
            u = jnp.clip(_dot(x, wu_bf[:, cs]) + bu_ref[0, 0, :, cs], -SWIGLU_LIMIT, SWIGLU_LIMIT)
            h_bf[:, cs] = (g * jax.nn.sigmoid(SWIGLU_ALPHA * g) * (u + 1.0)).astype(BF16)
        o_ref[...] = _pack_rows(_dot(h_bf[...], wd_bf[...]) + bd_ref[0, 0])

    @pl.when(jnp.logical_not(used))
    def _():
        o_ref[...] = jnp.zeros_like(o_ref)


def _experts(xb, block_e, next_e, block_rows, n_used, layer, wg, bg, wu, bu, wd, bd):
    n_slots = xb.shape[0]
    depth, n_e, d, d_exp = wg.shape
    assert d == d_exp
    tm = EXPERT_TILE
    n_blocks = n_slots // tm
    xmap = lambda i, be, ne, br, nb: (jnp.minimum(i, nb[0] - 1), 0)
    bmap = lambda i, be, ne, br, nb: (layer, be[i], 0, 0)
    hbm = pl.BlockSpec(memory_space=pl.ANY)
    grid_spec = pltpu.PrefetchScalarGridSpec(
        num_scalar_prefetch=4,
        grid=(n_blocks,),
        in_specs=[
            pl.BlockSpec((tm, d // 2), xmap),
            hbm,
            pl.BlockSpec((1, 1, 1, d_exp), bmap),
            hbm,
            pl.BlockSpec((1, 1, 1, d_exp), bmap),
            hbm,
            pl.BlockSpec((1, 1, 1, d), bmap),
        ],
        out_specs=pl.BlockSpec((tm, d // 2), lambda i, be, ne, br, nb: (i, 0)),
        scratch_shapes=[pltpu.VMEM((3, d, d_exp), F32), pltpu.VMEM((d, d_exp), BF16), pltpu.VMEM((d, d_exp), BF16),
                        pltpu.VMEM((d_exp, d), BF16), pltpu.VMEM((tm, d_exp), BF16), pltpu.SemaphoreType.DMA((3,))],
    )
    return pl.pallas_call(
        functools.partial(_expert_kernel, layer=layer, chunk=256),
        out_shape=jax.ShapeDtypeStruct((n_slots, d // 2), jnp.int32),
        grid_spec=grid_spec,
        compiler_params=_params(("arbitrary",)),
        name="experts",
    )(block_e, next_e, block_rows, n_used, xb, wg, bg.reshape(depth, n_e, 1, d_exp), wu,
      bu.reshape(depth, n_e, 1, d_exp), wd, bd.reshape(depth, n_e, 1, d))


def _combine_kernel(x_ref, g2_ref, gates_ref, y_ref, o_ref):
    gates = gates_ref[0]
    half = y_ref.shape[-1]
    acc_lo = jnp.zeros((x_ref.shape[1], half), F32)
    acc_hi = jnp.zeros((x_ref.shape[1], half), F32)
    for k in range(TOP_K):
        lo, hi = _unpack_rows(y_ref[k])
        acc_lo = acc_lo + gates[:, k:k + 1] * lo
        acc_hi = acc_hi + gates[:, k:k + 1] * hi
    o_ref[0] = x_ref[0] + g2_ref[0] * jnp.concatenate([acc_lo, acc_hi], axis=1)


def _combine(x, g2, gates, y_sel, tok_offset, batches=None):
    b, t, d = x.shape
    lo, hi = batches or (0, b)
    tm = int(np.gcd(min(t, 512), tok_offset)) if tok_offset else min(t, 512)
    n_t = t // tm
    off = tok_offset // tm
    row = lambda bi, i: (bi + lo, i, 0)
    return pl.pallas_call(
        _combine_kernel,
        out_shape=jax.ShapeDtypeStruct((b, t, d), F32),
        grid=(hi - lo, n_t),
        in_specs=[
            pl.BlockSpec((1, tm, d), row),
            pl.BlockSpec((1, 1, d), lambda bi, i: (bi + lo, 0, 0)),
            pl.BlockSpec((1, tm, ROUTER_PAD), row),
            pl.BlockSpec((TOP_K, tm, d // 2), lambda bi, i: (0, off + bi * n_t + i, 0)),
        ],
        out_specs=pl.BlockSpec((1, tm, d), row),
        input_output_aliases={0: 0},
        compiler_params=_params(("parallel", "parallel")),
        name="moe_combine",
    )(x, g2, gates, y_sel)


def _route_kernel(ids_ref, upper_ref, dest_ref, counts_ref, run_ref, *, tile):
    p = pl.program_id(0)
    i = pl.program_id(1)
    tm = ids_ref.shape[0]
    ids_t = ids_ref[...].T
    expert = lax.broadcasted_iota(jnp.int32, (N_EXPERTS, tm), 0)
    chosen = [ids_t[k:k + 1, :] == expert for k in range(TOP_K)]
    picks = functools.reduce(jnp.add, [c.astype(F32) for c in chosen])
    tile_counts = jnp.sum(picks, axis=1, keepdims=True)

    @pl.when((p == 0) & (i == 0))
    def _():
        run_ref[...] = jnp.zeros_like(run_ref)

    @pl.when(p == 0)
    def _():
        run_ref[...] += tile_counts

    @pl.when((p == 1) & (i == 0))
    def _():
        counts = run_ref[...]
        counts_ref[...] = counts.astype(jnp.int32)
        padded = jnp.floor((counts + (tile - 1)) * (1.0 / tile)) * tile
        row = lax.broadcasted_iota(jnp.int32, counts.shape, 0)
        ends = padded
        for s in (1, 2, 4, 8, 16):
            ends = ends + jnp.where(row >= s, pltpu.roll(ends, s, 0), 0.0)
        run_ref[...] = ends - padded

    @pl.when(p == 1)
    def _():
        before = _dot(picks.astype(BF16), upper_ref[...])
        slot = before + run_ref[:, 0:1]
        rows = [jnp.sum(jnp.where(c, slot, 0.0), axis=0, keepdims=True) for c in chosen]
        rows += [jnp.zeros_like(rows[0])] * (dest_ref.shape[0] - TOP_K)
        dest_ref[...] = jnp.concatenate(rows, axis=0).astype(jnp.int32)
        run_ref[...] += tile_counts


def _route(ids):
    n = ids.shape[0]
    tm = int(np.gcd(n, 1024))
    n_tiles = n // tm
    upper = jnp.asarray(np.triu(np.ones((tm, tm), np.float32), 1), BF16)
    dest, counts = pl.pallas_call(
        functools.partial(_route_kernel, tile=EXPERT_TILE),
        out_shape=[jax.ShapeDtypeStruct((SUBLANES, n), jnp.int32),
                   jax.ShapeDtypeStruct((N_EXPERTS, LANES), jnp.int32)],
        grid=(2, n_tiles),
        in_specs=[pl.BlockSpec((tm, ROUTER_PAD), lambda p, i: (i, 0)),
                  pl.BlockSpec((tm, tm), lambda p, i: (0, 0))],
        out_specs=[pl.BlockSpec((SUBLANES, tm), lambda p, i: (0, i * p)),
                   pl.BlockSpec((N_EXPERTS, LANES), lambda p, i: (0, 0))],
        scratch_shapes=[pltpu.VMEM((N_EXPERTS, LANES), F32)],
        compiler_params=_params(("arbitrary", "arbitrary")),
        name="route",
    )(ids, upper)
    return dest, counts[:, 0]


def _row_gather(table, idx):
    info = plsc.get_sparse_core_info()
    n_cores, n_workers = info.num_cores, info.num_cores * info.num_subcores
    n_rows, width = idx.shape[0], table.shape[1]
    chunk_rows = SC_GATHER_ROWS
    per_worker = n_rows // n_workers
    n_chunks = per_worker // chunk_rows
    assert per_worker * n_workers == n_rows and n_chunks * chunk_rows == per_worker and n_chunks % 2 == 0
    mesh = plsc.VectorSubcoreMesh(core_axis_name="c", subcore_axis_name="s")

    @functools.partial(
        pl.kernel, mesh=mesh,
        out_type=jax.ShapeDtypeStruct((n_rows, width), table.dtype),
        scratch_types=[pltpu.VMEM((n_chunks, chunk_rows), jnp.int32), pltpu.VMEM((2, chunk_rows, width), table.dtype),
                       pltpu.SemaphoreType.DMA((2,)), pltpu.SemaphoreType.DMA((2,))],
    )
    def gather_kernel(table_hbm, idx_hbm, out_hbm, idx_v, rows_v, gather_sem, write_sem):
        worker = lax.axis_index("s") * n_cores + lax.axis_index("c")
        pltpu.sync_copy(idx_hbm.at[worker], idx_v)

        def gather(c, slot):
            return pltpu.make_async_copy(table_hbm.at[idx_v.at[c]], rows_v.at[slot], gather_sem.at[slot])

        def write(c, slot):
            first_row = worker * per_worker + c * chunk_rows
            return pltpu.make_async_copy(rows_v.at[slot], out_hbm.at[pl.ds(first_row, chunk_rows)], write_sem.at[slot])

        gather(0, 0).start()

        @pl.loop(0, n_chunks, step=2)
        def _(c0):
            for slot in (0, 1):
                c = c0 + slot
                gather(c, slot).wait()

                @pl.when(c >= 1)
                def _():
                    write(c - 1, 1 - slot).wait()

                @pl.when(c + 1 < n_chunks)
                def _():
                    gather(c + 1, 1 - slot).start()

                write(c, slot).start()

        write(n_chunks - 1, 1).wait()

    return gather_kernel(table, idx.reshape(n_workers, n_chunks, chunk_rows))


def _row_scatter(rows, dest, n_out):
    info = plsc.get_sparse_core_info()
    n_cores, n_workers = info.num_cores, info.num_cores * info.num_subcores
    n_choices, n_rows = dest.shape
    width = rows.shape[1]
    chunk_rows = SC_GATHER_ROWS
    per_worker = n_rows // (n_workers * chunk_rows)
    assert per_worker * n_workers * chunk_rows == n_rows == rows.shape[0]
    idx = dest.reshape(n_choices, n_workers, per_worker, chunk_rows).transpose(1, 0, 2, 3)
    mesh = plsc.VectorSubcoreMesh(core_axis_name="c", subcore_axis_name="s")

    @functools.partial(
        pl.kernel, mesh=mesh,
        out_type=jax.ShapeDtypeStruct((n_out, width), rows.dtype),
        scratch_types=[pltpu.VMEM((n_choices, per_worker, chunk_rows), jnp.int32),
                       pltpu.VMEM((chunk_rows, width), rows.dtype), pltpu.SemaphoreType.DMA],
    )
    def scatter_kernel(rows_hbm, idx_hbm, out_hbm, idx_v, rows_v, sem):
        worker = lax.axis_index("s") * n_cores + lax.axis_index("c")
        pltpu.sync_copy(idx_hbm.at[worker], idx_v)

        @pl.loop(0, per_worker)
        def _(c):
            first_row = (worker * per_worker + c) * chunk_rows
            pltpu.sync_copy(rows_hbm.at[pl.ds(first_row, chunk_rows)], rows_v)
            copies = [pltpu.async_copy(rows_v, out_hbm.at[idx_v.at[k, c]], sem) for k in range(n_choices)]
            for cp in copies:
                cp.wait()

    return scatter_kernel(rows, idx)


def _moe(h2, ids, layer, w_exp):
    n = h2.shape[0]
    nk = n * TOP_K
    tm = EXPERT_TILE
    dest, counts = _route(ids)
    dest = dest[:TOP_K]
    padded = (counts + tm - 1) // tm * tm
    pends = jnp.cumsum(padded)
    n_blocks = -(-nk // tm) + N_EXPERTS
    n_slots = n_blocks * tm
    block_start = jnp.arange(n_blocks, dtype=jnp.int32) * tm
    block_e = jnp.minimum(jnp.sum((pends[None] <= block_start[:, None]).astype(jnp.int32), axis=1), N_EXPERTS - 1)
    is_e = (block_e[:, None] == jnp.arange(N_EXPERTS, dtype=jnp.int32)[None]).astype(jnp.int32)
    block_rows = jnp.clip(jnp.sum(is_e * (pends - padded + counts)[None], axis=1) - block_start, 0, tm)
    n_used = (pends[-1] // tm).astype(jnp.int32).reshape(1)
    e_ids = jnp.arange(N_EXPERTS, dtype=jnp.int32)
    later = (e_ids[None] > e_ids[:, None]) & (counts[None] > 0)
    next_nonempty = jnp.min(jnp.where(later, e_ids[None], N_EXPERTS), axis=1)
    next_nonempty = jnp.where(next_nonempty == N_EXPERTS, -1, next_nonempty)
    next_e = jnp.sum(is_e * next_nonempty[None], axis=1)
    xb = _row_scatter(h2, dest, n_slots)
    return _experts(xb, block_e, next_e, block_rows, n_used, layer, *w_exp), dest


def _gather_choices(yb, dest, lo, hi):
    return _row_gather(yb, dest[:, lo:hi].reshape(-1)).reshape(TOP_K, hi - lo, -1)


def _split_batch(b, t, lead):
    unit = 2 * SC_GATHER_ROWS * SC_WORKERS // TOP_K
    fits = [bs for bs in range(1, b) if (lead + bs * t) % unit == 0 and ((b - bs) * t) % unit == 0]
    return min(fits, key=lambda bs: abs(lead + bs * t - (b - bs) * t)) if fits else b


def _mod_parts(mod_l, b):
    d = mod_l.shape[-1] // 6
    lat = [mod_l[:b, k * d:(k + 1) * d].reshape(b, 1, d) for k in range(6)]
    ctx = [jnp.broadcast_to(mod_l[b, k * d:(k + 1) * d].reshape(1, 1, d), (b, 1, d)) for k in range(6)]
    return lat, ctx


def kernel(x, c, ctx, c_ctx, ada_w, ada_b, norm1_g, norm2_g, ev_w_in, ev_w_out, ev_q_gain, ev_k_gain, ev_rpb, ev_conv_w, ev_conv_b, od_w_in, od_w_out, od_conv_w, od_conv_b, od_fwd_wa, od_fwd_ba, od_fwd_wx, od_fwd_bx, od_fwd_lam, od_bwd_wa, od_bwd_ba, od_bwd_wx, od_bwd_bx, od_bwd_lam, router_w, router_b, exp_w_gate, exp_b_gate, exp_w_up, exp_b_up, exp_w_down, exp_b_down):
    b, t, d = x.shape
    l = ctx.shape[1]
    assert ada_w.shape[0] == DEPTH == 2 and t % GRID_W == 0 and t // GRID_W >= WIN_H

    n_rows_c = -(-(b + 1) // SUBLANES) * SUBLANES
    cvec = jnp.zeros((n_rows_c, d), F32).at[:b].set(c).at[b].set(c_ctx)
    mod = _ada_mod(cvec, ada_w, ada_b)

    def router(layer):
        w_r = jnp.zeros((d, ROUTER_PAD), F32).at[:, :N_EXPERTS].set(router_w[layer]).astype(BF16)
        b_r = jnp.zeros((1, ROUTER_PAD), F32).at[0, :N_EXPERTS].set(router_b[layer])
        return w_r, b_r

    w_exp = (exp_w_gate, exp_b_gate, exp_w_up, exp_b_up, exp_w_down, exp_b_down)

    (sh1, sc1, g1, sh2, sc2, g2), (csh1, csc1, cg1, csh2, csc2, cg2) = _mod_parts(mod[0], b)
    n1 = norm1_g[0].reshape(1, d)
    n2 = norm2_g[0].reshape(1, d)
    w_in = ev_w_in[0].astype(BF16)
    w_out = ev_w_out[0].astype(BF16)
    q_scale = NA_HEAD_DIM ** -0.5 * LOG2_E
    head_gain = jnp.stack([jnp.tile(ev_q_gain[0] * q_scale, NA_HEADS), jnp.tile(ev_k_gain[0], NA_HEADS)])
    head_gain = head_gain.reshape(2, 1, NA_WIDTH).astype(F32)
    proj = _inproj(x, n1, sh1, sc1, w_in, head_gain)
    proj_c = _inproj(ctx, n1, csh1, csc1, w_in, head_gain)
    kh, r0, row_type, patterns = _na_tables(t // GRID_W)
    bias_tab = _na_bias_table(ev_rpb[0], patterns)
    o_a = _neighbourhood_attention(proj, proj_c, bias_tab, r0, row_type, kh)
    oc_a = _context_attention(proj_c)
    w_r, b_r = router(0)
    x1, h2, ids, gates = _even_out(o_a, proj, ev_conv_w[0], ev_conv_b[0], w_out, x, g1, n2, sh2, sc2, w_r, b_r)
    c1, h2c, ids_c, gates_c = _even_out(oc_a, proj_c, ev_conv_w[0], ev_conv_b[0], w_out, ctx, cg1, n2, csh2, csc2,
                                        w_r, b_r)
    tokens = jnp.concatenate([h2c.reshape(b * l, d // 2), h2.reshape(b * t, d // 2)], axis=0)
    ids_all = jnp.concatenate([ids_c.reshape(b * l, ROUTER_PAD), ids.reshape(b * t, ROUTER_PAD)], axis=0)
    yb, dest = _moe(tokens, ids_all, 0, w_exp)
    bs = _split_batch(b, t, b * l)
    n_first = b * l + bs * t
    y_first = _gather_choices(yb, dest, 0, n_first)
    hctx = _combine(c1, cg2, gates_c, y_first, 0)
    x = _combine(x1, g2, gates, y_first, b * l, (0, bs))
    if bs < b:
        x = _combine(x, g2, gates, _gather_choices(yb, dest, n_first, b * (l + t)), 0, (bs, b))

    (sh1, sc1, g1, sh2, sc2, g2), (csh1, csc1, _, _, _, _) = _mod_parts(mod[1], b)
    n1 = norm1_g[1].reshape(1, d)
    n2 = norm2_g[1].reshape(1, d)
    w_in = od_w_in[0].astype(BF16)
    width = w_in.shape[1] // 2
    proj = _inproj_conv(x, n1, sh1, sc1, w_in, od_conv_w[0], od_conv_b[0], width)
    u_ctx = _inproj_conv(hctx, n1, csh1, csc1, w_in[:, width:], od_conv_w[0], od_conv_b[0], 0)
    h_dir = []
    for reverse, (wa, ba, wx, bx, lam) in ((False, (od_fwd_wa, od_fwd_ba, od_fwd_wx, od_fwd_bx, od_fwd_lam)),
                                           (True, (od_bwd_wa, od_bwd_ba, od_bwd_wx, od_bwd_bx, od_bwd_lam))):
        w_cat = jnp.concatenate([wa[0], wx[0]], axis=-1).astype(BF16)
        h_dir.append(_lru_scan(proj, u_ctx, w_cat, ba[0], bx[0], lam[0], reverse))
    w_r, b_r = router(1)
    x1, h2, ids, gates = _odd_out(h_dir[0], h_dir[1], proj, od_w_out[0].astype(BF16), x, g1, n2, sh2, sc2, w_r, b_r)
    yb, dest = _moe(h2.reshape(b * t, d // 2), ids.reshape(b * t, ROUTER_PAD), 1, w_exp)
    bs = _split_batch(b, t, 0)
    x = _combine(x1, g2, gates, _gather_choices(yb, dest, 0, bs * t), 0, (0, bs))
    if bs < b:
        x = _combine(x, g2, gates, _gather_choices(yb, dest, bs * t, b * t), 0, (bs, b))
    return x
```

```python
import functools

import numpy as np
import jax
import jax.numpy as jnp
from jax import lax
from jax.experimental import pallas as pl
from jax.experimental.pallas import tpu as pltpu
from jax.experimental.pallas import tpu_sc as plsc

DEPTH = 2
GRID_W = 64
EPS = 1e-6
NEG_INF = -1e30
LOG2_E = 1.4426950408889634
NA_HEADS = 8
NA_HEAD_DIM = 64
NA_WIDTH = NA_HEADS * NA_HEAD_DIM
HEAD_PAIRS = NA_HEADS // 2
WIN_H = 8
WIN_W = 16
SC_CONV = 3
LRU_BLOCKS = 4
LRU_CONV = 4
LRU_C = 8.0
N_EXPERTS = 32
TOP_K = 4
SWIGLU_LIMIT = 7.0
SWIGLU_ALPHA = 1.702

LANES = 128
SUBLANES = 8
HALO = 16
ROUTER_PAD = LANES
NA_ROWS_PER_STEP = 4
LRU_SUB_TILES = 4
OUT_TILE = 1024
OUT_SUB_TILE = 512
EXPERT_TILE = 512
SC_GATHER_ROWS = 64
SC_WORKERS = 32
VMEM_LIMIT = 56 * 1024 * 1024

F32 = jnp.float32
BF16 = jnp.bfloat16


def _params(sem, vmem=VMEM_LIMIT):
    return pltpu.CompilerParams(dimension_semantics=sem, vmem_limit_bytes=vmem)


def _dot(a, b):
    return jnp.dot(a, b, preferred_element_type=F32)


def _dot_nt(a, b):
    return lax.dot_general(a, b, (((1,), (1,)), ((), ())), preferred_element_type=F32)


def _pack_rows(v):
    w = v.shape[-1] // 2
    lo = lax.bitcast_convert_type(v[:, :w].astype(BF16).astype(F32), jnp.int32)
    hi = lax.bitcast_convert_type(v[:, w:].astype(BF16).astype(F32), jnp.int32)
    return lax.shift_right_logical(lo, 16) | (hi & jnp.int32(-65536))


def _unpack_rows(p):
    lo = lax.bitcast_convert_type(lax.shift_left(p, 16), F32)
    hi = lax.bitcast_convert_type(p & jnp.int32(-65536), F32)
    return lo, hi


def _rms_mod(x, g, shift, scale):
    ms = jnp.mean(x * x, axis=-1, keepdims=True)
    y = x * lax.rsqrt(ms + EPS) * g
    return y * (1.0 + scale) + shift


def _ada_kernel(c_ref, w_ref, b_ref, o_ref):
    c = c_ref[...]
    s = (c * jax.nn.sigmoid(c)).astype(BF16)
    o_ref[0] = _dot(s, w_ref[0].astype(BF16)) + b_ref[0]


def _ada_mod(cvec, ada_w, ada_b):
    depth, d, n = ada_w.shape
    r = cvec.shape[0]
    tn = 1536
    return pl.pallas_call(
        _ada_kernel,
        out_shape=jax.ShapeDtypeStruct((depth, r, n), F32),
        grid=(depth, n // tn),
        in_specs=[
            pl.BlockSpec((r, d), lambda l, j: (0, 0)),
            pl.BlockSpec((1, d, tn), lambda l, j: (l, 0, j)),
            pl.BlockSpec((1, 1, tn), lambda l, j: (l, 0, j)),
        ],
        out_specs=pl.BlockSpec((1, r, tn), lambda l, j: (l, 0, j)),
        compiler_params=_params(("parallel", "parallel")),
        name="ada_mod",
    )(cvec, ada_w, ada_b.reshape(depth, 1, n))


def _inproj_kernel(x_ref, g_ref, sh_ref, sc_ref, w_ref, hg_ref, ones_ref, o_ref, *, n_tiles, tn, n_headnorm):
    h = _rms_mod(x_ref[0], g_ref[...], sh_ref[0], sc_ref[0]).astype(BF16)
    for j in range(n_tiles):
        y = _dot(h, w_ref[:, j * tn:(j + 1) * tn])
        if j < n_headnorm:
            ms = _dot((y * y).astype(BF16), ones_ref[...]) * (1.0 / NA_HEAD_DIM)
            y = y * lax.rsqrt(ms + EPS) * hg_ref[j]
        o_ref[0, :, j * tn:(j + 1) * tn] = y.astype(o_ref.dtype)


def _inproj(x, g, shift, scale, w, head_gain=None):
    b, t, d = x.shape
    n = w.shape[1]
    tn = NA_WIDTH
    tm = min(t, 512)
    n_headnorm = 0 if head_gain is None else head_gain.shape[0]
    if head_gain is None:
        head_gain = jnp.ones((1, 1, tn), F32)
    hid = np.arange(tn) // NA_HEAD_DIM
    ones_bd = jnp.asarray((hid[:, None] == hid[None, :]), BF16)
    kern = functools.partial(_inproj_kernel, n_tiles=n // tn, tn=tn, n_headnorm=n_headnorm)
    return pl.pallas_call(
        kern,
        out_shape=jax.ShapeDtypeStruct((b, t, n), BF16),
        grid=(b, t // tm),
        in_specs=[
            pl.BlockSpec((1, tm, d), lambda bi, i: (bi, i, 0)),
            pl.BlockSpec((1, d), lambda bi, i: (0, 0)),
            pl.BlockSpec((1, 1, d), lambda bi, i: (bi, 0, 0)),
            pl.BlockSpec((1, 1, d), lambda bi, i: (bi, 0, 0)),
            pl.BlockSpec((d, n), lambda bi, i: (0, 0)),
            pl.BlockSpec(head_gain.shape, lambda bi, i: (0, 0, 0)),
            pl.BlockSpec((tn, tn), lambda bi, i: (0, 0)),
        ],
        out_specs=pl.BlockSpec((1, tm, n), lambda bi, i: (bi, i, 0)),
        compiler_params=_params(("parallel", "parallel")),
        name="inproj",
    )(x, g, shift, scale, w, head_gain, ones_bd)


def _inproj_conv_kernel(x_ref, xp_ref, xn_ref, g_ref, sh_ref, sc_ref, w_ref, cw_ref, cb_ref, o_ref, *,
                        n_tiles, n_plain, tn, tm):
    i = pl.program_id(1)
    edge = SUBLANES
    x_ext = jnp.concatenate([xp_ref[0], x_ref[0], xn_ref[0]], axis=0)
    h = _rms_mod(x_ext, g_ref[...], sh_ref[0], sc_ref[0]).astype(BF16)
    keep_prev = jnp.where(i > 0, 1.0, 0.0)
    keep_next = jnp.where(i < pl.num_programs(1) - 1, 1.0, 0.0)
    for j in list(range(n_plain, n_tiles)) + list(range(n_plain)):
        y = _dot(h, w_ref[:, j * tn:(j + 1) * tn])
        if j < n_plain:
            o_ref[0, :, j * tn:(j + 1) * tn] = y[edge:edge + tm].astype(o_ref.dtype)
            continue
        u = jnp.concatenate([y[:edge] * keep_prev, y[edge:edge + tm], y[edge + tm:] * keep_next], axis=0)
        cw = cw_ref[:, (j - n_plain) * tn:(j - n_plain + 1) * tn]
        left = (LRU_CONV - 1) // 2
        uc = cb_ref[:, (j - n_plain) * tn:(j - n_plain + 1) * tn]
        n_ext = u.shape[0]
        for k in range(LRU_CONV):
            shifted = u if k == left else pltpu.roll(u, (left - k) % n_ext, 0)
            uc = uc + shifted[edge:edge + tm] * cw[k:k + 1]
        o_ref[0, :, j * tn:(j + 1) * tn] = uc.astype(o_ref.dtype)


def _inproj_conv(x, g, shift, scale, w, conv_w, conv_b, n_plain_cols):
    b, t, d = x.shape
    n = w.shape[1]
    tn = NA_WIDTH
    tm = min(t, 512)
    hb = tm // SUBLANES
    n_hblocks = t // SUBLANES
    kern = functools.partial(_inproj_conv_kernel, n_tiles=n // tn, n_plain=n_plain_cols // tn, tn=tn, tm=tm)
    const = lambda bi, i: (0, 0)
    return pl.pallas_call(
        kern,
        out_shape=jax.ShapeDtypeStruct((b, t, n), BF16),
        grid=(b, t // tm),
        in_specs=[
            pl.BlockSpec((1, tm, d), lambda bi, i: (bi, i, 0)),
            pl.BlockSpec((1, SUBLANES, d), lambda bi, i: (bi, jnp.maximum(i * hb - 1, 0), 0)),
            pl.BlockSpec((1, SUBLANES, d), lambda bi, i: (bi, jnp.minimum((i + 1) * hb, n_hblocks - 1), 0)),
            pl.BlockSpec((1, d), const),
            pl.BlockSpec((1, 1, d), lambda bi, i: (bi, 0, 0)),
            pl.BlockSpec((1, 1, d), lambda bi, i: (bi, 0, 0)),
            pl.BlockSpec((d, n), const),
            pl.BlockSpec(conv_w.shape, const),
            pl.BlockSpec((1, conv_w.shape[1]), const),
        ],
        out_specs=pl.BlockSpec((1, tm, n), lambda bi, i: (bi, i, 0)),
        compiler_params=_params(("parallel", "parallel")),
        name="inproj_conv",
    )(x, x, x, g, shift, scale, w, conv_w, conv_b.reshape(1, -1))


def _na_tables(rows):
    kh = min(WIN_H, rows)
    r = np.arange(rows)
    r0 = np.clip(r - kh // 2, 0, rows - kh)
    dr = r0[:, None] + np.arange(kh)[None] - r[:, None] + WIN_H - 1
    patterns, row_type = np.unique(dr, axis=0, return_inverse=True)
    return kh, r0.astype(np.int32), row_type.reshape(-1).astype(np.int32), patterns


def _na_bias_table(rpb, patterns):
    qc = np.arange(GRID_W)
    kc = np.arange(GRID_W)
    c0 = np.clip(qc - WIN_W // 2, 0, GRID_W - WIN_W)[:, None]
    valid = (kc[None] >= c0) & (kc[None] < c0 + WIN_W)
    dc = np.clip(kc[None] - qc[:, None] + WIN_W - 1, 0, 2 * WIN_W - 2)
    n_pat, kh = patterns.shape
    onehot_dc = jnp.asarray(dc[None] == np.arange(2 * WIN_W - 1)[:, None, None], F32)
    tab = jnp.einsum('hpic,cqk->hpiqk', rpb.astype(F32)[:, patterns], onehot_dc,
                     precision=lax.Precision.HIGHEST)
    tab = jnp.where(valid[None, None, None], tab * LOG2_E, NEG_INF)
    tab = tab.reshape(HEAD_PAIRS, 2, n_pat, kh, GRID_W, GRID_W)
    tab = tab.transpose(2, 0, 1, 4, 3, 5)
    return tab.reshape(n_pat, HEAD_PAIRS, 2 * GRID_W, kh * GRID_W)


def _pair_attention(q, keys, values, biases):
    m = q.shape[0]
    qq = _stack_heads(q)
    scores = []
    for k, bias in zip(keys, biases):
        s = _dot_nt(qq, k)
        scores.append(s if bias is None else s + bias)
    s = jnp.concatenate(scores, axis=1)
    e = jnp.exp2(s - jnp.max(s, axis=-1, keepdims=True))
    denom = jnp.sum(e, axis=-1, keepdims=True)
    e = e.astype(BF16)
    o, start = 0.0, 0
    for v in values:
        o = o + _dot(e[:, start:start + v.shape[0]], v)
        start += v.shape[0]
    o = o * (1.0 / denom)
    lane_o = lax.broadcasted_iota(jnp.int32, (m, LANES), 1)
    return jnp.where(lane_o < NA_HEAD_DIM, o[:m], o[m:])


def _stack_heads(q):
    lane = lax.broadcasted_iota(jnp.int32, q.shape, 1)
    zero = jnp.zeros_like(q)
    return jnp.concatenate([jnp.where(lane < NA_HEAD_DIM, q, zero), jnp.where(lane >= NA_HEAD_DIM, q, zero)], axis=0)


def _na_kernel(r0_ref, type_ref, q_ref, k_ref, v_ref, kc_ref, vc_ref, bias_ref, o_ref, s_ref, p_ref, *,
               kh, rows_per_step):
    n_lat = kh * GRID_W
    tiles = [(j, p) for j in range(rows_per_step) for p in range(HEAD_PAIRS)]
    window = []
    for j in range(rows_per_step):
        r = pl.program_id(1) * rows_per_step + j
        window.append((pl.multiple_of(r0_ref[r] * GRID_W, GRID_W), type_ref[r]))

    for idx, (j, p) in enumerate(tiles):
        start, rtype = window[j]
        cols = slice(p * LANES, (p + 1) * LANES)
        qq = _stack_heads(q_ref[0, j * GRID_W:(j + 1) * GRID_W, cols])
        s_ref[idx, :, :n_lat] = _dot_nt(qq, k_ref[0, pl.ds(start, n_lat), cols]) + bias_ref[rtype, p]
        s_ref[idx, :, n_lat:] = _dot_nt(qq, kc_ref[0, :, cols])

    denoms = []
    for idx in range(len(tiles)):
        s = s_ref[idx]
        e = jnp.exp2(s - jnp.max(s, axis=-1, keepdims=True))
        denoms.append(jnp.sum(e, axis=-1, keepdims=True))
        p_ref[idx] = e.astype(BF16)

    lane = lax.broadcasted_iota(jnp.int32, (GRID_W, LANES), 1)
    for idx, (j, p) in enumerate(tiles):
        start, _ = window[j]
        cols = slice(p * LANES, (p + 1) * LANES)
        o = _dot(p_ref[idx, :, :n_lat], v_ref[0, pl.ds(start, n_lat), cols]) + _dot(p_ref[idx, :, n_lat:],
                                                                                 vc_ref[0, :, cols])
        o = o * (1.0 / denoms[idx])
        o = jnp.where(lane < NA_HEAD_DIM, o[:GRID_W], o[GRID_W:])
        o_ref[0, j * GRID_W:(j + 1) * GRID_W, cols] = o.astype(o_ref.dtype)


def _neighbourhood_attention(proj, proj_c, bias_tab, r0, row_type, kh):
    b, t, _ = proj.shape
    l = proj_c.shape[1]
    rows = t // GRID_W
    w = NA_WIDTH
    rps = int(np.gcd(rows, NA_ROWS_PER_STEP))
    q_rows = rps * GRID_W
    grid_spec = pltpu.PrefetchScalarGridSpec(
        num_scalar_prefetch=2,
        grid=(b, rows // rps),
        in_specs=[
            pl.BlockSpec((1, q_rows, w), lambda bi, r, *_: (bi, r, 0)),
            pl.BlockSpec((1, t, w), lambda bi, r, *_: (bi, 0, 1)),
            pl.BlockSpec((1, t, w), lambda bi, r, *_: (bi, 0, 2)),
            pl.BlockSpec((1, l, w), lambda bi, r, *_: (bi, 0, 1)),
            pl.BlockSpec((1, l, w), lambda bi, r, *_: (bi, 0, 2)),
            pl.BlockSpec(bias_tab.shape, lambda bi, r, *_: (0, 0, 0, 0)),
        ],
        out_specs=pl.BlockSpec((1, q_rows, w), lambda bi, r, *_: (bi, r, 0)),
        scratch_shapes=[pltpu.VMEM((rps * HEAD_PAIRS, 2 * GRID_W, kh * GRID_W + l), F32),
                        pltpu.VMEM((rps * HEAD_PAIRS, 2 * GRID_W, kh * GRID_W + l), BF16)],
    )
    return pl.pallas_call(
        functools.partial(_na_kernel, kh=kh, rows_per_step=rps),
        out_shape=jax.ShapeDtypeStruct((b, t, w), BF16),
        grid_spec=grid_spec,
        compiler_params=_params(("parallel", "arbitrary")),
        name="na_attention",
    )(jnp.asarray(r0), jnp.asarray(row_type), proj, proj, proj, proj_c, proj_c, bias_tab)


def _ctx_attn_kernel(q_ref, k_ref, v_ref, o_ref):
    for p in range(HEAD_PAIRS):
        cols = slice(p * LANES, (p + 1) * LANES)
        o = _pair_attention(q_ref[0, :, cols], [k_ref[0, :, cols]], [v_ref[0, :, cols]], [None])
        o_ref[0, :, cols] = o.astype(o_ref.dtype)


def _context_attention(proj_c):
    b, l, _ = proj_c.shape
    w = NA_WIDTH
    return pl.pallas_call(
        _ctx_attn_kernel,
        out_shape=jax.ShapeDtypeStruct((b, l, w), BF16),
        grid=(b,),
        in_specs=[pl.BlockSpec((1, l, w), lambda bi, j=j: (bi, 0, j)) for j in range(3)],
        out_specs=pl.BlockSpec((1, l, w), lambda bi: (bi, 0, 0)),
        compiler_params=_params(("parallel",)),
        name="ctx_attention",
    )(proj_c, proj_c, proj_c)


def _top4(logits):
    lane = lax.broadcasted_iota(jnp.int32, logits.shape, 1)
    cur = jnp.where(lane < N_EXPERTS, logits, -jnp.inf)
    vals, idxs = [], []
    for _ in range(TOP_K):
        m = jnp.max(cur, axis=-1, keepdims=True)
        first = jnp.min(jnp.where(cur == m, lane, ROUTER_PAD).astype(F32), axis=-1, keepdims=True)
        idx = first.astype(jnp.int32)
        vals.append(m)
        idxs.append(idx)
        cur = jnp.where(lane == idx, -jnp.inf, cur)
    exps = [jnp.exp(v - vals[0]) for v in vals]
    inv = 1.0 / functools.reduce(jnp.add, exps)
    ids = jnp.zeros(logits.shape, jnp.int32)
    gates = jnp.zeros(logits.shape, F32)
    for k in range(TOP_K):
        ids = jnp.where(lane == k, idxs[k], ids)
        gates = jnp.where(lane == k, exps[k] * inv, gates)
    return ids, gates


def _layer_tail(y, rows, x_ref, g1_ref, n2_ref, sh2_ref, sc2_ref, wr_ref, br_ref, xo_ref, h2_ref, ids_ref, gates_ref):
    x_new = x_ref[0, rows] + g1_ref[0] * y
    xo_ref[0, rows] = x_new
    h2 = _rms_mod(x_new, n2_ref[...], sh2_ref[0], sc2_ref[0])
    h2_ref[0, rows] = _pack_rows(h2)
    logits = _dot(h2.astype(BF16), wr_ref[...]) + br_ref[...]
    ids, gates = _top4(logits)
    ids_ref[0, rows] = ids
    gates_ref[0, rows] = gates


def _sub_tiles(n_rows):
    sub = min(OUT_SUB_TILE, n_rows)
    return [slice(s, s + sub) for s in range(0, n_rows, sub)]


def _tail_specs(b, t, d, tm):
    row = lambda bi, i: (bi, i, 0)
    per_b = lambda bi, i: (bi, 0, 0)
    const = lambda bi, i: (0, 0)
    in_specs = [
        pl.BlockSpec((1, tm, d), row),
        pl.BlockSpec((1, 1, d), per_b),
        pl.BlockSpec((1, d), const),
        pl.BlockSpec((1, 1, d), per_b),
        pl.BlockSpec((1, 1, d), per_b),
        pl.BlockSpec((d, ROUTER_PAD), const),
        pl.BlockSpec((1, ROUTER_PAD), const),
    ]
    out_specs = [
        pl.BlockSpec((1, tm, d), row),
        pl.BlockSpec((1, tm, d // 2), row),
        pl.BlockSpec((1, tm, ROUTER_PAD), row),
        pl.BlockSpec((1, tm, ROUTER_PAD), row),
    ]
    out_shape = [
        jax.ShapeDtypeStruct((b, t, d), F32),
        jax.ShapeDtypeStruct((b, t, d // 2), jnp.int32),
        jax.ShapeDtypeStruct((b, t, ROUTER_PAD), jnp.int32),
        jax.ShapeDtypeStruct((b, t, ROUTER_PAD), F32),
    ]
    return in_specs, out_specs, out_shape


def _halo_fix(rolled, at_row, halo_row, present):
    n = rolled.shape[0]
    first = at_row < SUBLANES
    assert first or at_row >= n - SUBLANES
    slab = rolled[:SUBLANES] if first else rolled[n - SUBLANES:]
    sub = lax.broadcasted_iota(jnp.int32, slab.shape, 0)
    fill = jnp.where(present, halo_row, jnp.zeros_like(halo_row))
    slab = jnp.where(sub == at_row % SUBLANES, fill, slab)
    return jnp.concatenate([slab, rolled[SUBLANES:]] if first else [rolled[:n - SUBLANES], slab], axis=0)


def _even_out_kernel(oa_ref, bg_ref, cg_ref, xin_ref, cgp_ref, xinp_ref, cgn_ref, xinn_ref, cw_ref, cb_ref,
                     wa_ref, wb_ref, *tail_refs, tm):
    i = pl.program_id(1)
    has_prev = i > 0
    has_next = i < pl.num_programs(1) - 1
    u = cg_ref[0].astype(F32) * xin_ref[0].astype(F32)
    u_prev = (cgp_ref[0].astype(F32) * xinp_ref[0].astype(F32))[HALO - 1:HALO]
    u_next = (cgn_ref[0].astype(F32) * xinn_ref[0].astype(F32))[0:1]
    u_m1 = _halo_fix(pltpu.roll(u, 1, 0), 0, u_prev, has_prev)
    u_p1 = _halo_fix(pltpu.roll(u, tm - 1, 0), tm - 1, u_next, has_next)
    cw = cw_ref[...]
    conv = u_m1 * cw[0:1] + u * cw[1:2] + u_p1 * cw[2:3] + cb_ref[...]
    o_b = (bg_ref[0].astype(F32) * conv).astype(BF16)
    for rows in _sub_tiles(tm):
        y = _dot(oa_ref[0, rows], wa_ref[...]) + _dot(o_b[rows], wb_ref[...])
        _layer_tail(y, rows, *tail_refs)


def _even_out(o_a, proj, conv_w, conv_b, w_out, x, g1, n2, sh2, sc2, w_r, b_r):
    b, t, d = x.shape
    w = NA_WIDTH
    tm = min(t, OUT_TILE)
    hb = tm // HALO
    n_hblocks = t // HALO
    row = lambda bi, i: (bi, i, 0)
    const = lambda bi, i: (0, 0)
    prev = lambda col: (lambda bi, i: (bi, jnp.maximum(i * hb - 1, 0), col))
    nxt = lambda col: (lambda bi, i: (bi, jnp.minimum((i + 1) * hb, n_hblocks - 1), col))
    tail_in, out_specs, out_shape = _tail_specs(b, t, d, tm)
    in_specs = [
        pl.BlockSpec((1, tm, w), row),
        pl.BlockSpec((1, tm, w), lambda bi, i: (bi, i, 3)),
        pl.BlockSpec((1, tm, w), lambda bi, i: (bi, i, 4)),
        pl.BlockSpec((1, tm, w), lambda bi, i: (bi, i, 5)),
        pl.BlockSpec((1, HALO, w), prev(4)),
        pl.BlockSpec((1, HALO, w), prev(5)),
        pl.BlockSpec((1, HALO, w), nxt(4)),
        pl.BlockSpec((1, HALO, w), nxt(5)),
        pl.BlockSpec((SC_CONV, w), const),
        pl.BlockSpec((1, w), const),
        pl.BlockSpec((w, d), const),
        pl.BlockSpec((w, d), const),
    ] + tail_in
    return pl.pallas_call(
        functools.partial(_even_out_kernel, tm=tm),
        out_shape=out_shape,
        grid=(b, t // tm),
        in_specs=in_specs,
        out_specs=out_specs,
        compiler_params=_params(("parallel", "parallel")),
        name="even_out",
    )(o_a, proj, proj, proj, proj, proj, proj, proj, conv_w, conv_b.reshape(1, w),
      w_out[:w], w_out[w:], x, g1, n2, sh2, sc2, w_r, b_r)


def _log_sigmoid(x):
    return jnp.minimum(x, 0.0) - jnp.log1p(jnp.exp(-jnp.abs(x)))


def _sigmoid(x):
    return 0.5 + 0.5 * jnp.tanh(0.5 * x)


def _lru_sub_tile(ucb_time, state, perm_ref, w_ref, ba_ref, bx_ref, lam_ref, want_hidden, reverse):
    sub, width = ucb_time.shape
    blk = width // LRU_BLOCKS
    steps = sub // SUBLANES
    uc = _dot(perm_ref[0], ucb_time)
    ucb = uc.astype(BF16)
    za, zx = [], []
    for h in range(LRU_BLOCKS):
        z = _dot(ucb[:, h * blk:(h + 1) * blk], w_ref[h])
        za.append(z[:, :blk])
        zx.append(z[:, blk:])
    r = _sigmoid(jnp.concatenate(za, axis=1) + ba_ref[...])
    gate_i = _sigmoid(jnp.concatenate(zx, axis=1) + bx_ref[...])
    log_a = (LRU_C * _log_sigmoid(lam_ref[...])) * r
    a = jnp.exp(log_a)
    th = jnp.tanh(log_a)
    num = -2.0 * th
    mult = jnp.where(num > 0.0, num * lax.rsqrt(num * (1.0 - th)), 0.0)
    bcoef = mult * gate_i * uc

    grp = lambda v, j: v[j * SUBLANES:(j + 1) * SUBLANES]
    prods, local = [grp(a, 0)], [grp(bcoef, 0)]
    for j in range(1, steps):
        aj = grp(a, j)
        local.append(aj * local[-1] + grp(bcoef, j))
        prods.append(aj * prods[-1])

    carry_in = [None] * SUBLANES
    for s in (range(SUBLANES - 1, -1, -1) if reverse else range(SUBLANES)):
        carry_in[s] = state
        state = prods[-1][s:s + 1] * state + local[-1][s:s + 1]
    if not want_hidden:
        return state, None
    start = jnp.concatenate(carry_in, axis=0)
    hidden = jnp.concatenate([prods[j] * start + local[j] for j in range(steps)], axis=0)
    return state, _dot(perm_ref[1], hidden.astype(BF16)).astype(BF16)


def _lru_tile(u_ref, perm_ref, w_ref, ba_ref, bx_ref, lam_ref, carry_ref, o_ref, *, reverse):
    sub = perm_ref.shape[1]
    n_sub = u_ref.shape[1] // sub
    state = carry_ref[0:1, :]
    for k in (range(n_sub - 1, -1, -1) if reverse else range(n_sub)):
        rows = slice(k * sub, (k + 1) * sub)
        state, hidden = _lru_sub_tile(u_ref[0, rows], state, perm_ref, w_ref, ba_ref, bx_ref, lam_ref,
                                      o_ref is not None, reverse)
        if o_ref is not None:
            o_ref[0, rows] = hidden
    carry_ref[...] = jnp.broadcast_to(state, carry_ref.shape)


def _lru_kernel(uc_ref, ul_ref, perm_ref, w_ref, ba_ref, bx_ref, lam_ref, o_ref, carry_ref, *, n_ctx_tiles, reverse):
    j = pl.program_id(1)
    shared = (perm_ref, w_ref, ba_ref, bx_ref, lam_ref, carry_ref)

    @pl.when(j == 0)
    def _():
        carry_ref[...] = jnp.zeros_like(carry_ref)

    @pl.when(j < n_ctx_tiles)
    def _():
        _lru_tile(uc_ref, *shared, None, reverse=reverse)

    @pl.when(j >= n_ctx_tiles)
    def _():
        _lru_tile(ul_ref, *shared, o_ref, reverse=reverse)


def _lru_scan(proj, u_ctx, w_cat, ba, bx, lam, reverse):
    b, t, _ = proj.shape
    l, width = u_ctx.shape[1], u_ctx.shape[2]
    sub = min(256, l, t)
    tc = sub * min(LRU_SUB_TILES, l // sub)
    tl = sub * min(LRU_SUB_TILES, t // sub)
    n_c, n_l = l // tc, t // tl

    def pos_of(step, n):
        step = jnp.clip(step, 0, n - 1)
        return (n - 1 - step) if reverse else step

    def tile_map(off, n, col):
        return lambda bi, j: (bi, pos_of(j - off, n), col)

    const2 = lambda bi, j: (0, 0)
    in_specs = [
        pl.BlockSpec((1, tc, width), tile_map(0, n_c, 0)),
        pl.BlockSpec((1, tl, width), tile_map(n_c, n_l, 1)),
        pl.BlockSpec((2, sub, sub), lambda bi, j: (0, 0, 0)),
        pl.BlockSpec(w_cat.shape, lambda bi, j: (0, 0, 0)),
        pl.BlockSpec((1, width), const2),
        pl.BlockSpec((1, width), const2),
        pl.BlockSpec((1, width), const2),
    ]
    steps = sub // SUBLANES
    step, block = np.divmod(np.arange(sub), SUBLANES)
    time_of_row = block * steps + (steps - 1 - step if reverse else step)
    perm = (time_of_row[:, None] == np.arange(sub)[None]).astype(np.float32)
    perms = jnp.asarray(np.stack([perm, perm.T]), BF16)
    kern = functools.partial(_lru_kernel, n_ctx_tiles=n_c, reverse=reverse)
    return pl.pallas_call(
        kern,
        out_shape=jax.ShapeDtypeStruct((b, t, width), BF16),
        grid=(b, n_c + n_l),
        in_specs=in_specs,
        out_specs=pl.BlockSpec((1, tl, width), tile_map(n_c, n_l, 0)),
        scratch_shapes=[pltpu.VMEM((SUBLANES, width), F32)],
        compiler_params=_params(("parallel", "arbitrary")),
        name="lru_scan_bwd" if reverse else "lru_scan_fwd",
    )(u_ctx, proj, perms, w_cat, ba.reshape(1, width), bx.reshape(1, width), lam.reshape(1, width))


def _odd_out_kernel(hf_ref, hb_ref, gate_ref, w_ref, *tail_refs):
    for rows in _sub_tiles(hf_ref.shape[1]):
        hsum = hf_ref[0, rows].astype(F32) + hb_ref[0, rows].astype(F32)
        z = hsum * jax.nn.gelu(gate_ref[0, rows].astype(F32), approximate=True)
        y = _dot(z.astype(BF16), w_ref[...])
        _layer_tail(y, rows, *tail_refs)


def _odd_out(h_f, h_b, proj, w_out, x, g1, n2, sh2, sc2, w_r, b_r):
    b, t, d = x.shape
    width = h_f.shape[-1]
    tm = min(t, OUT_TILE)
    row = lambda bi, i: (bi, i, 0)
    tail_in, out_specs, out_shape = _tail_specs(b, t, d, tm)
    in_specs = [
        pl.BlockSpec((1, tm, width), row),
        pl.BlockSpec((1, tm, width), row),
        pl.BlockSpec((1, tm, width), row),
        pl.BlockSpec((width, d), lambda bi, i: (0, 0)),
    ] + tail_in
    return pl.pallas_call(
        _odd_out_kernel,
        out_shape=out_shape,
        grid=(b, t // tm),
        in_specs=in_specs,
        out_specs=out_specs,
        compiler_params=_params(("parallel", "parallel")),
        name="odd_out",
    )(h_f, h_b, proj, w_out, x, g1, n2, sh2, sc2, w_r, b_r)


def _expert_kernel(be_ref, next_ref, rows_ref, nb_ref, x_ref, wg_hbm, bg_ref, wu_hbm, bu_ref, wd_hbm, bd_ref,
                   o_ref, w_f32, wg_bf, wu_bf, wd_bf, h_bf, sem, *, layer, chunk):
    i = pl.program_id(0)
    expert = be_ref[i]
    used = i < nb_ref[0]

    def weight_copies(e):
        return [pltpu.make_async_copy(src.at[layer, e], w_f32.at[j], sem.at[j])
                for j, src in enumerate((wg_hbm, wu_hbm, wd_hbm))]

    @pl.when(i == 0)
    def _():
        for cp in weight_copies(expert):
            cp.start()

    @pl.when(used & ((i == 0) | (expert != be_ref[jnp.maximum(i - 1, 0)])))
    def _():
        for cp in weight_copies(expert):
            cp.wait()
        wg_bf[...] = w_f32[0].astype(BF16)
        wu_bf[...] = w_f32[1].astype(BF16)
        wd_bf[...] = w_f32[2].astype(BF16)

        @pl.when(next_ref[i] >= 0)
        def _():
            for cp in weight_copies(next_ref[i]):
                cp.start()

    @pl.when(used)
    def _():
        xp = x_ref[...]
        row = lax.broadcasted_iota(jnp.int32, xp.shape, 0)
        xp = jnp.where(row < rows_ref[i], xp, 0)
        x = jnp.concatenate(_unpack_rows(xp), axis=1).astype(BF16)
        for c in range(wg_bf.shape[1] // chunk):
            cs = slice(c * chunk, (c + 1) * chunk)
            g = jnp.minimum(_dot(x, wg_bf[:, cs]) + bg_ref[0, 0, :, cs], SWIGLU_LIMIT)
            u = jnp.clip(_dot(x, wu_bf[:, cs]) + bu_ref[0, 0, :, cs], -SWIGLU_LIMIT, SWIGLU_LIMIT)
            h_bf[:, cs] = (g * jax.nn.sigmoid(SWIGLU_ALPHA * g) * (u + 1.0)).astype(BF16)
        o_ref[...] = _pack_rows(_dot(h_bf[...], wd_bf[...]) + bd_ref[0, 0])

    @pl.when(jnp.logical_not(used))
    def _():
        o_ref[...] = jnp.zeros_like(o_ref)


def _experts(xb, block_e, next_e, block_rows, n_used, layer, wg, bg, wu, bu, wd, bd):
    n_slots = xb.shape[0]
    depth, n_e, d, d_exp = wg.shape
    assert d == d_exp
    tm = EXPERT_TILE
    n_blocks = n_slots // tm
    xmap = lambda i, be, ne, br, nb: (jnp.minimum(i, nb[0] - 1), 0)
    bmap = lambda i, be, ne, br, nb: (layer, be[i], 0, 0)
    hbm = pl.BlockSpec(memory_space=pl.ANY)
    grid_spec = pltpu.PrefetchScalarGridSpec(
        num_scalar_prefetch=4,
        grid=(n_blocks,),
        in_specs=[
            pl.BlockSpec((tm, d // 2), xmap),
            hbm,
            pl.BlockSpec((1, 1, 1, d_exp), bmap),
            hbm,
            pl.BlockSpec((1, 1, 1, d_exp), bmap),
            hbm,
            pl.BlockSpec((1, 1, 1, d), bmap),
        ],
        out_specs=pl.BlockSpec((tm, d // 2), lambda i, be, ne, br, nb: (i, 0)),
        scratch_shapes=[pltpu.VMEM((3, d, d_exp), F32), pltpu.VMEM((d, d_exp), BF16), pltpu.VMEM((d, d_exp), BF16),
                        pltpu.VMEM((d_exp, d), BF16), pltpu.VMEM((tm, d_exp), BF16), pltpu.SemaphoreType.DMA((3,))],
    )
    return pl.pallas_call(
        functools.partial(_expert_kernel, layer=layer, chunk=256),
        out_shape=jax.ShapeDtypeStruct((n_slots, d // 2), jnp.int32),
        grid_spec=grid_spec,
        compiler_params=_params(("arbitrary",)),
        name="experts",
    )(block_e, next_e, block_rows, n_used, xb, wg, bg.reshape(depth, n_e, 1, d_exp), wu,
      bu.reshape(depth, n_e, 1, d_exp), wd, bd.reshape(depth, n_e, 1, d))


def _combine_kernel(x_ref, g2_ref, gates_ref, y_ref, o_ref):
    gates = gates_ref[0]
    half = y_ref.shape[-1]
    acc_lo = jnp.zeros((x_ref.shape[1], half), F32)
    acc_hi = jnp.zeros((x_ref.shape[1], half), F32)
    for k in range(TOP_K):
        lo, hi = _unpack_rows(y_ref[k])
        acc_lo = acc_lo + gates[:, k:k + 1] * lo
        acc_hi = acc_hi + gates[:, k:k + 1] * hi
    o_ref[0] = x_ref[0] + g2_ref[0] * jnp.concatenate([acc_lo, acc_hi], axis=1)


def _combine(x, g2, gates, y_sel, tok_offset, batches=None):
    b, t, d = x.shape
    lo, hi = batches or (0, b)
    tm = int(np.gcd(min(t, 512), tok_offset)) if tok_offset else min(t, 512)
    n_t = t // tm
    off = tok_offset // tm
    row = lambda bi, i: (bi + lo, i, 0)
    return pl.pallas_call(
        _combine_kernel,
        out_shape=jax.ShapeDtypeStruct((b, t, d), F32),
        grid=(hi - lo, n_t),
        in_specs=[
            pl.BlockSpec((1, tm, d), row),
            pl.BlockSpec((1, 1, d), lambda bi, i: (bi + lo, 0, 0)),
            pl.BlockSpec((1, tm, ROUTER_PAD), row),
            pl.BlockSpec((TOP_K, tm, d // 2), lambda bi, i: (0, off + bi * n_t + i, 0)),
        ],
        out_specs=pl.BlockSpec((1, tm, d), row),
        input_output_aliases={0: 0},
        compiler_params=_params(("parallel", "parallel")),
        name="moe_combine",
    )(x, g2, gates, y_sel)


def _route_kernel(ids_ref, upper_ref, dest_ref, counts_ref, run_ref, *, tile):
    p = pl.program_id(0)
    i = pl.program_id(1)
    tm = ids_ref.shape[0]
    ids_t = ids_ref[...].T
    expert = lax.broadcasted_iota(jnp.int32, (N_EXPERTS, tm), 0)
    chosen = [ids_t[k:k + 1, :] == expert for k in range(TOP_K)]
    picks = functools.reduce(jnp.add, [c.astype(F32) for c in chosen])
    tile_counts = jnp.sum(picks, axis=1, keepdims=True)

    @pl.when((p == 0) & (i == 0))
    def _():
        run_ref[...] = jnp.zeros_like(run_ref)

    @pl.when(p == 0)
    def _():
        run_ref[...] += tile_counts

    @pl.when((p == 1) & (i == 0))
    def _():
        counts = run_ref[...]
        counts_ref[...] = counts.astype(jnp.int32)
        padded = jnp.floor((counts + (tile - 1)) * (1.0 / tile)) * tile
        row = lax.broadcasted_iota(jnp.int32, counts.shape, 0)
        ends = padded
        for s in (1, 2, 4, 8, 16):
            ends = ends + jnp.where(row >= s, pltpu.roll(ends, s, 0), 0.0)
        run_ref[...] = ends - padded

    @pl.when(p == 1)
    def _():
        before = _dot(picks.astype(BF16), upper_ref[...])
        slot = before + run_ref[:, 0:1]
        rows = [jnp.sum(jnp.where(c, slot, 0.0), axis=0, keepdims=True) for c in chosen]
        rows += [jnp.zeros_like(rows[0])] * (dest_ref.shape[0] - TOP_K)
        dest_ref[...] = jnp.concatenate(rows, axis=0).astype(jnp.int32)
        run_ref[...] += tile_counts


def _route(ids):
    n = ids.shape[0]
    tm = int(np.gcd(n, 1024))
    n_tiles = n // tm
    upper = jnp.asarray(np.triu(np.ones((tm, tm), np.float32), 1), BF16)
    dest, counts = pl.pallas_call(
        functools.partial(_route_kernel, tile=EXPERT_TILE),
        out_shape=[jax.ShapeDtypeStruct((SUBLANES, n), jnp.int32),
                   jax.ShapeDtypeStruct((N_EXPERTS, LANES), jnp.int32)],
        grid=(2, n_tiles),
        in_specs=[pl.BlockSpec((tm, ROUTER_PAD), lambda p, i: (i, 0)),
                  pl.BlockSpec((tm, tm), lambda p, i: (0, 0))],
        out_specs=[pl.BlockSpec((SUBLANES, tm), lambda p, i: (0, i * p)),
                   pl.BlockSpec((N_EXPERTS, LANES), lambda p, i: (0, 0))],
        scratch_shapes=[pltpu.VMEM((N_EXPERTS, LANES), F32)],
        compiler_params=_params(("arbitrary", "arbitrary")),
        name="route",
    )(ids, upper)
    return dest, counts[:, 0]


def _row_gather(table, idx):
    info = plsc.get_sparse_core_info()
    n_cores, n_workers = info.num_cores, info.num_cores * info.num_subcores
    n_rows, width = idx.shape[0], table.shape[1]
    chunk_rows = SC_GATHER_ROWS
    per_worker = n_rows // n_workers
    n_chunks = per_worker // chunk_rows
    assert per_worker * n_workers == n_rows and n_chunks * chunk_rows == per_worker and n_chunks % 2 == 0
    mesh = plsc.VectorSubcoreMesh(core_axis_name="c", subcore_axis_name="s")

    @functools.partial(
        pl.kernel, mesh=mesh,
        out_type=jax.ShapeDtypeStruct((n_rows, width), table.dtype),
        scratch_types=[pltpu.VMEM((n_chunks, chunk_rows), jnp.int32), pltpu.VMEM((2, chunk_rows, width), table.dtype),
                       pltpu.SemaphoreType.DMA((2,)), pltpu.SemaphoreType.DMA((2,))],
    )
    def gather_kernel(table_hbm, idx_hbm, out_hbm, idx_v, rows_v, gather_sem, write_sem):
        worker = lax.axis_index("s") * n_cores + lax.axis_index("c")
        pltpu.sync_copy(idx_hbm.at[worker], idx_v)

        def gather(c, slot):
            return pltpu.make_async_copy(table_hbm.at[idx_v.at[c]], rows_v.at[slot], gather_sem.at[slot])

        def write(c, slot):
            first_row = worker * per_worker + c * chunk_rows
            return pltpu.make_async_copy(rows_v.at[slot], out_hbm.at[pl.ds(first_row, chunk_rows)], write_sem.at[slot])

        gather(0, 0).start()

        @pl.loop(0, n_chunks, step=2)
        def _(c0):
            for slot in (0, 1):
                c = c0 + slot
                gather(c, slot).wait()

                @pl.when(c >= 1)
                def _():
                    write(c - 1, 1 - slot).wait()

                @pl.when(c + 1 < n_chunks)
                def _():
                    gather(c + 1, 1 - slot).start()

                write(c, slot).start()

        write(n_chunks - 1, 1).wait()

    return gather_kernel(table, idx.reshape(n_workers, n_chunks, chunk_rows))


def _row_scatter(rows, dest, n_out):
    info = plsc.get_sparse_core_info()
    n_cores, n_workers = info.num_cores, info.num_cores * info.num_subcores
    n_choices, n_rows = dest.shape
    width = rows.shape[1]
    chunk_rows = SC_GATHER_ROWS
    per_worker = n_rows // (n_workers * chunk_rows)
    assert per_worker * n_workers * chunk_rows == n_rows == rows.shape[0]
    idx = dest.reshape(n_choices, n_workers, per_worker, chunk_rows).transpose(1, 0, 2, 3)
    mesh = plsc.VectorSubcoreMesh(core_axis_name="c", subcore_axis_name="s")

    @functools.partial(
        pl.kernel, mesh=mesh,
        out_type=jax.ShapeDtypeStruct((n_out, width), rows.dtype),
        scratch_types=[pltpu.VMEM((n_choices, per_worker, chunk_rows), jnp.int32),
                       pltpu.VMEM((chunk_rows, width), rows.dtype), pltpu.SemaphoreType.DMA],
    )
    def scatter_kernel(rows_hbm, idx_hbm, out_hbm, idx_v, rows_v, sem):
        worker = lax.axis_index("s") * n_cores + lax.axis_index("c")
        pltpu.sync_copy(idx_hbm.at[worker], idx_v)

        @pl.loop(0, per_worker)
        def _(c):
            first_row = (worker * per_worker + c) * chunk_rows
            pltpu.sync_copy(rows_hbm.at[pl.ds(first_row, chunk_rows)], rows_v)
            copies = [pltpu.async_copy(rows_v, out_hbm.at[idx_v.at[k, c]], sem) for k in range(n_choices)]
            for cp in copies:
                cp.wait()

    return scatter_kernel(rows, idx)


def _moe(h2, ids, layer, w_exp):
    n = h2.shape[0]
    nk = n * TOP_K
    tm = EXPERT_TILE
    dest, counts = _route(ids)
    dest = dest[:TOP_K]
    padded = (counts + tm - 1) // tm * tm
    pends = jnp.cumsum(padded)
    n_blocks = -(-nk // tm) + N_EXPERTS
    n_slots = n_blocks * tm
    block_start = jnp.arange(n_blocks, dtype=jnp.int32) * tm
    block_e = jnp.minimum(jnp.sum((pends[None] <= block_start[:, None]).astype(jnp.int32), axis=1), N_EXPERTS - 1)
    is_e = (block_e[:, None] == jnp.arange(N_EXPERTS, dtype=jnp.int32)[None]).astype(jnp.int32)
    block_rows = jnp.clip(jnp.sum(is_e * (pends - padded + counts)[None], axis=1) - block_start, 0, tm)
    n_used = (pends[-1] // tm).astype(jnp.int32).reshape(1)
    e_ids = jnp.arange(N_EXPERTS, dtype=jnp.int32)
    later = (e_ids[None] > e_ids[:, None]) & (counts[None] > 0)
    next_nonempty = jnp.min(jnp.where(later, e_ids[None], N_EXPERTS), axis=1)
    next_nonempty = jnp.where(next_nonempty == N_EXPERTS, -1, next_nonempty)
    next_e = jnp.sum(is_e * next_nonempty[None], axis=1)
    xb = _row_scatter(h2, dest, n_slots)
    return _experts(xb, block_e, next_e, block_rows, n_used, layer, *w_exp), dest


def _gather_choices(yb, dest, lo, hi):
    return _row_gather(yb, dest[:, lo:hi].reshape(-1)).reshape(TOP_K, hi - lo, -1)


def _split_batch(b, t, lead):
    unit = 2 * SC_GATHER_ROWS * SC_WORKERS // TOP_K
    fits = [bs for bs in range(1, b) if (lead + bs * t) % unit == 0 and ((b - bs) * t) % unit == 0]
    return min(fits, key=lambda bs: abs(lead + bs * t - (b - bs) * t)) if fits else b


def _mod_parts(mod_l, b):
    d = mod_l.shape[-1] // 6
    lat = [mod_l[:b, k * d:(k + 1) * d].reshape(b, 1, d) for k in range(6)]
    ctx = [jnp.broadcast_to(mod_l[b, k * d:(k + 1) * d].reshape(1, 1, d), (b, 1, d)) for k in range(6)]
    return lat, ctx


def kernel(x, c, ctx, c_ctx, ada_w, ada_b, norm1_g, norm2_g, ev_w_in, ev_w_out, ev_q_gain, ev_k_gain, ev_rpb, ev_conv_w, ev_conv_b, od_w_in, od_w_out, od_conv_w, od_conv_b, od_fwd_wa, od_fwd_ba, od_fwd_wx, od_fwd_bx, od_fwd_lam, od_bwd_wa, od_bwd_ba, od_bwd_wx, od_bwd_bx, od_bwd_lam, router_w, router_b, exp_w_gate, exp_b_gate, exp_w_up, exp_b_up, exp_w_down, exp_b_down):
    b, t, d = x.shape
    l = ctx.shape[1]
    assert ada_w.shape[0] == DEPTH == 2 and t % GRID_W == 0 and t // GRID_W >= WIN_H

    n_rows_c = -(-(b + 1) // SUBLANES) * SUBLANES
    cvec = jnp.zeros((n_rows_c, d), F32).at[:b].set(c).at[b].set(c_ctx)
    mod = _ada_mod(cvec, ada_w, ada_b)

    def router(layer):
        w_r = jnp.zeros((d, ROUTER_PAD), F32).at[:, :N_EXPERTS].set(router_w[layer]).astype(BF16)
        b_r = jnp.zeros((1, ROUTER_PAD), F32).at[0, :N_EXPERTS].set(router_b[layer])
        return w_r, b_r

    w_exp = (exp_w_gate, exp_b_gate, exp_w_up, exp_b_up, exp_w_down, exp_b_down)

    (sh1, sc1, g1, sh2, sc2, g2), (csh1, csc1, cg1, csh2, csc2, cg2) = _mod_parts(mod[0], b)
    n1 = norm1_g[0].reshape(1, d)
    n2 = norm2_g[0].reshape(1, d)
    w_in = ev_w_in[0].astype(BF16)
    w_out = ev_w_out[0].astype(BF16)
    q_scale = NA_HEAD_DIM ** -0.5 * LOG2_E
    head_gain = jnp.stack([jnp.tile(ev_q_gain[0] * q_scale, NA_HEADS), jnp.tile(ev_k_gain[0], NA_HEADS)])
    head_gain = head_gain.reshape(2, 1, NA_WIDTH).astype(F32)
    proj = _inproj(x, n1, sh1, sc1, w_in, head_gain)
    proj_c = _inproj(ctx, n1, csh1, csc1, w_in, head_gain)
    kh, r0, row_type, patterns = _na_tables(t // GRID_W)
    bias_tab = _na_bias_table(ev_rpb[0], patterns)
    o_a = _neighbourhood_attention(proj, proj_c, bias_tab, r0, row_type, kh)
    oc_a = _context_attention(proj_c)
    w_r, b_r = router(0)
    x1, h2, ids, gates = _even_out(o_a, proj, ev_conv_w[0], ev_conv_b[0], w_out, x, g1, n2, sh2, sc2, w_r, b_r)
    c1, h2c, ids_c, gates_c = _even_out(oc_a, proj_c, ev_conv_w[0], ev_conv_b[0], w_out, ctx, cg1, n2, csh2, csc2,
                                        w_r, b_r)
    tokens = jnp.concatenate([h2c.reshape(b * l, d // 2), h2.reshape(b * t, d // 2)], axis=0)
    ids_all = jnp.concatenate([ids_c.reshape(b * l, ROUTER_PAD), ids.reshape(b * t, ROUTER_PAD)], axis=0)
    yb, dest = _moe(tokens, ids_all, 0, w_exp)
    bs = _split_batch(b, t, b * l)
    n_first = b * l + bs * t
    y_first = _gather_choices(yb, dest, 0, n_first)
    hctx = _combine(c1, cg2, gates_c, y_first, 0)
    x = _combine(x1, g2, gates, y_first, b * l, (0, bs))
    if bs < b:
        x = _combine(x, g2, gates, _gather_choices(yb, dest, n_first, b * (l + t)), 0, (bs, b))

    (sh1, sc1, g1, sh2, sc2, g2), (csh1, csc1, _, _, _, _) = _mod_parts(mod[1], b)
    n1 = norm1_g[1].reshape(1, d)
    n2 = norm2_g[1].reshape(1, d)
    w_in = od_w_in[0].astype(BF16)
    width = w_in.shape[1] // 2
    proj = _inproj_conv(x, n1, sh1, sc1, w_in, od_conv_w[0], od_conv_b[0], width)
    u_ctx = _inproj_conv(hctx, n1, csh1, csc1, w_in[:, width:], od_conv_w[0], od_conv_b[0], 0)
    h_dir = []
    for reverse, (wa, ba, wx, bx, lam) in ((False, (od_fwd_wa, od_fwd_ba, od_fwd_wx, od_fwd_bx, od_fwd_lam)),
                                           (True, (od_bwd_wa, od_bwd_ba, od_bwd_wx, od_bwd_bx, od_bwd_lam))):
        w_cat = jnp.concatenate([wa[0], wx[0]], axis=-1).astype(BF16)
        h_dir.append(_lru_scan(proj, u_ctx, w_cat, ba[0], bx[0], lam[0], reverse))
    w_r, b_r = router(1)
    x1, h2, ids, gates = _odd_out(h_dir[0], h_dir[1], proj, od_w_out[0].astype(BF16), x, g1, n2, sh2, sc2, w_r, b_r)
    yb, dest = _moe(h2.reshape(b * t, d // 2), ids.reshape(b * t, ROUTER_PAD), 1, w_exp)
    bs = _split_batch(b, t, 0)
    x = _combine(x1, g2, gates, _gather_choices(yb, dest, 0, bs * t), 0, (0, bs))
    if bs < b:
        x = _combine(x, g2, gates, _gather_choices(yb, dest, bs * t, b * t), 0, (bs, b))
    return x
```

```python
import functools

import numpy as np
import jax
import jax.numpy as jnp
from jax import lax
from jax.experimental import pallas as pl
from jax.experimental.pallas import tpu as pltpu
from jax.experimental.pallas import tpu_sc as plsc

DEPTH = 2
GRID_W = 64
EPS = 1e-6
NEG_INF = -1e30
LOG2_E = 1.4426950408889634
NA_HEADS = 8
NA_HEAD_DIM = 64
NA_WIDTH = NA_HEADS * NA_HEAD_DIM
HEAD_PAIRS = NA_HEADS // 2
WIN_H = 8
WIN_W = 16
SC_CONV = 3
LRU_BLOCKS = 4
LRU_CONV = 4
LRU_C = 8.0
N_EXPERTS = 32
TOP_K = 4
SWIGLU_LIMIT = 7.0
SWIGLU_ALPHA = 1.702

LANES = 128
SUBLANES = 8
HALO = 16
ROUTER_PAD = LANES
NA_ROWS_PER_STEP = 4
LRU_SUB_TILES = 4
OUT_TILE = 1024
OUT_SUB_TILE = 512
EXPERT_TILE = 512
SC_GATHER_ROWS = 64
VMEM_LIMIT = 56 * 1024 * 1024

F32 = jnp.float32
BF16 = jnp.bfloat16


def _params(sem, vmem=VMEM_LIMIT):
    return pltpu.CompilerParams(dimension_semantics=sem, vmem_limit_bytes=vmem)


def _dot(a, b):
    return jnp.dot(a, b, preferred_element_type=F32)


def _dot_nt(a, b):
    return lax.dot_general(a, b, (((1,), (1,)), ((), ())), preferred_element_type=F32)


def _pack_rows(v):
    w = v.shape[-1] // 2
    lo = lax.bitcast_convert_type(v[:, :w].astype(BF16).astype(F32), jnp.int32)
    hi = lax.bitcast_convert_type(v[:, w:].astype(BF16).astype(F32), jnp.int32)
    return lax.shift_right_logical(lo, 16) | (hi & jnp.int32(-65536))


def _unpack_rows(p):
    lo = lax.bitcast_convert_type(lax.shift_left(p, 16), F32)
    hi = lax.bitcast_convert_type(p & jnp.int32(-65536), F32)
    return lo, hi


def _rms_mod(x, g, shift, scale):
    ms = jnp.mean(x * x, axis=-1, keepdims=True)
    y = x * lax.rsqrt(ms + EPS) * g
    return y * (1.0 + scale) + shift


def _ada_kernel(c_ref, w_ref, b_ref, o_ref):
    c = c_ref[...]
    s = (c * jax.nn.sigmoid(c)).astype(BF16)
    o_ref[0] = _dot(s, w_ref[0].astype(BF16)) + b_ref[0]


def _ada_mod(cvec, ada_w, ada_b):
    depth, d, n = ada_w.shape
    r = cvec.shape[0]
    tn = 1536
    return pl.pallas_call(
        _ada_kernel,
        out_shape=jax.ShapeDtypeStruct((depth, r, n), F32),
        grid=(depth, n // tn),
        in_specs=[
            pl.BlockSpec((r, d), lambda l, j: (0, 0)),
            pl.BlockSpec((1, d, tn), lambda l, j: (l, 0, j)),
            pl.BlockSpec((1, 1, tn), lambda l, j: (l, 0, j)),
        ],
        out_specs=pl.BlockSpec((1, r, tn), lambda l, j: (l, 0, j)),
        compiler_params=_params(("parallel", "parallel")),
        name="ada_mod",
    )(cvec, ada_w, ada_b.reshape(depth, 1, n))


def _inproj_kernel(x_ref, g_ref, sh_ref, sc_ref, w_ref, hg_ref, ones_ref, o_ref, *, n_tiles, tn, n_headnorm):
    h = _rms_mod(x_ref[0], g_ref[...], sh_ref[0], sc_ref[0]).astype(BF16)
    for j in range(n_tiles):
        y = _dot(h, w_ref[:, j * tn:(j + 1) * tn])
        if j < n_headnorm:
            ms = _dot((y * y).astype(BF16), ones_ref[...]) * (1.0 / NA_HEAD_DIM)
            y = y * lax.rsqrt(ms + EPS) * hg_ref[j]
        o_ref[0, :, j * tn:(j + 1) * tn] = y.astype(o_ref.dtype)


def _inproj(x, g, shift, scale, w, head_gain=None):
    b, t, d = x.shape
    n = w.shape[1]
    tn = NA_WIDTH
    tm = min(t, 512)
    n_headnorm = 0 if head_gain is None else head_gain.shape[0]
    if head_gain is None:
        head_gain = jnp.ones((1, 1, tn), F32)
    hid = np.arange(tn) // NA_HEAD_DIM
    ones_bd = jnp.asarray((hid[:, None] == hid[None, :]), BF16)
    kern = functools.partial(_inproj_kernel, n_tiles=n // tn, tn=tn, n_headnorm=n_headnorm)
    return pl.pallas_call(
        kern,
        out_shape=jax.ShapeDtypeStruct((b, t, n), BF16),
        grid=(b, t // tm),
        in_specs=[
            pl.BlockSpec((1, tm, d), lambda bi, i: (bi, i, 0)),
            pl.BlockSpec((1, d), lambda bi, i: (0, 0)),
            pl.BlockSpec((1, 1, d), lambda bi, i: (bi, 0, 0)),
            pl.BlockSpec((1, 1, d), lambda bi, i: (bi, 0, 0)),
            pl.BlockSpec((d, n), lambda bi, i: (0, 0)),
            pl.BlockSpec(head_gain.shape, lambda bi, i: (0, 0, 0)),
            pl.BlockSpec((tn, tn), lambda bi, i: (0, 0)),
        ],
        out_specs=pl.BlockSpec((1, tm, n), lambda bi, i: (bi, i, 0)),
        compiler_params=_params(("parallel", "parallel")),
        name="inproj",
    )(x, g, shift, scale, w, head_gain, ones_bd)


def _inproj_conv_kernel(x_ref, xp_ref, xn_ref, g_ref, sh_ref, sc_ref, w_ref, cw_ref, cb_ref, o_ref, *,
                        n_tiles, n_plain, tn, tm, sub):
    i = pl.program_id(1)
    edge = SUBLANES
    x_ext = jnp.concatenate([xp_ref[0], x_ref[0], xn_ref[0]], axis=0)
    h = _rms_mod(x_ext, g_ref[...], sh_ref[0], sc_ref[0]).astype(BF16)
    first_step = i == 0
    last_step = i == pl.num_programs(1) - 1
    left = (LRU_CONV - 1) // 2
    n_sub = tm // sub
    for s in range(n_sub):
        h_s = h[s * sub:(s + 1) * sub + 2 * edge]
        rows = slice(s * sub, (s + 1) * sub)
        keep_prev = jnp.where(first_step, 0.0, 1.0) if s == 0 else 1.0
        keep_next = jnp.where(last_step, 0.0, 1.0) if s == n_sub - 1 else 1.0
        for j in range(n_tiles):
            y = _dot(h_s, w_ref[:, j * tn:(j + 1) * tn])
            if j < n_plain:
                o_ref[0, rows, j * tn:(j + 1) * tn] = y[edge:edge + sub].astype(o_ref.dtype)
                continue
            u = jnp.concatenate([y[:edge] * keep_prev, y[edge:edge + sub], y[edge + sub:] * keep_next], axis=0)
            cw = cw_ref[:, (j - n_plain) * tn:(j - n_plain + 1) * tn]
            uc = cb_ref[:, (j - n_plain) * tn:(j - n_plain + 1) * tn]
            n_ext = u.shape[0]
            for k in range(LRU_CONV):
                shifted = u if k == left else pltpu.roll(u, (left - k) % n_ext, 0)
                uc = uc + shifted[edge:edge + sub] * cw[k:k + 1]
            o_ref[0, rows, j * tn:(j + 1) * tn] = uc.astype(o_ref.dtype)


def _inproj_conv(x, g, shift, scale, w, conv_w, conv_b, n_plain_cols):
    b, t, d = x.shape
    n = w.shape[1]
    tn = NA_WIDTH
    tm = min(t, OUT_TILE)
    hb = tm // SUBLANES
    n_hblocks = t // SUBLANES
    kern = functools.partial(_inproj_conv_kernel, n_tiles=n // tn, n_plain=n_plain_cols // tn, tn=tn, tm=tm,
                             sub=min(tm, OUT_SUB_TILE))
    const = lambda bi, i: (0, 0)
    return pl.pallas_call(
        kern,
        out_shape=jax.ShapeDtypeStruct((b, t, n), BF16),
        grid=(b, t // tm),
        in_specs=[
            pl.BlockSpec((1, tm, d), lambda bi, i: (bi, i, 0)),
            pl.BlockSpec((1, SUBLANES, d), lambda bi, i: (bi, jnp.maximum(i * hb - 1, 0), 0)),
            pl.BlockSpec((1, SUBLANES, d), lambda bi, i: (bi, jnp.minimum((i + 1) * hb, n_hblocks - 1), 0)),
            pl.BlockSpec((1, d), const),
            pl.BlockSpec((1, 1, d), lambda bi, i: (bi, 0, 0)),
            pl.BlockSpec((1, 1, d), lambda bi, i: (bi, 0, 0)),
            pl.BlockSpec((d, n), const),
            pl.BlockSpec(conv_w.shape, const),
            pl.BlockSpec((1, conv_w.shape[1]), const),
        ],
        out_specs=pl.BlockSpec((1, tm, n), lambda bi, i: (bi, i, 0)),
        compiler_params=_params(("parallel", "parallel")),
        name="inproj_conv",
    )(x, x, x, g, shift, scale, w, conv_w, conv_b.reshape(1, -1))


def _na_tables(rows):
    kh = min(WIN_H, rows)
    r = np.arange(rows)
    r0 = np.clip(r - kh // 2, 0, rows - kh)
    dr = r0[:, None] + np.arange(kh)[None] - r[:, None] + WIN_H - 1
    patterns, row_type = np.unique(dr, axis=0, return_inverse=True)
    return kh, r0.astype(np.int32), row_type.reshape(-1).astype(np.int32), patterns


def _na_bias_table(rpb, patterns):
    qc = np.arange(GRID_W)
    kc = np.arange(GRID_W)
    c0 = np.clip(qc - WIN_W // 2, 0, GRID_W - WIN_W)[:, None]
    valid = (kc[None] >= c0) & (kc[None] < c0 + WIN_W)
    dc = np.clip(kc[None] - qc[:, None] + WIN_W - 1, 0, 2 * WIN_W - 2)
    n_pat, kh = patterns.shape
    onehot_dc = jnp.asarray(dc[None] == np.arange(2 * WIN_W - 1)[:, None, None], F32)
    tab = jnp.einsum('hpic,cqk->hpiqk', rpb.astype(F32)[:, patterns], onehot_dc,
                     precision=lax.Precision.HIGHEST)
    tab = jnp.where(valid[None, None, None], tab * LOG2_E, NEG_INF)
    tab = tab.reshape(HEAD_PAIRS, 2, n_pat, kh, GRID_W, GRID_W)
    tab = tab.transpose(2, 0, 1, 4, 3, 5)
    return tab.reshape(n_pat, HEAD_PAIRS, 2 * GRID_W, kh * GRID_W)


def _pair_attention(q, keys, values, biases):
    m = q.shape[0]
    qq = _stack_heads(q)
    scores = []
    for k, bias in zip(keys, biases):
        s = _dot_nt(qq, k)
        scores.append(s if bias is None else s + bias)
    s = jnp.concatenate(scores, axis=1)
    e = jnp.exp2(s - jnp.max(s, axis=-1, keepdims=True))
    denom = jnp.sum(e, axis=-1, keepdims=True)
    e = e.astype(BF16)
    o, start = 0.0, 0
    for v in values:
        o = o + _dot(e[:, start:start + v.shape[0]], v)
        start += v.shape[0]
    o = o * (1.0 / denom)
    lane_o = lax.broadcasted_iota(jnp.int32, (m, LANES), 1)
    return jnp.where(lane_o < NA_HEAD_DIM, o[:m], o[m:])


def _stack_heads(q):
    lane = lax.broadcasted_iota(jnp.int32, q.shape, 1)
    zero = jnp.zeros_like(q)
    return jnp.concatenate([jnp.where(lane < NA_HEAD_DIM, q, zero), jnp.where(lane >= NA_HEAD_DIM, q, zero)], axis=0)


def _na_kernel(r0_ref, type_ref, q_ref, k_ref, v_ref, kc_ref, vc_ref, bias_ref, o_ref, s_ref, p_ref, *,
               kh, rows_per_step):
    n_lat = kh * GRID_W
    tiles = [(j, p) for j in range(rows_per_step) for p in range(HEAD_PAIRS)]
    window = []
    for j in range(rows_per_step):
        r = pl.program_id(1) * rows_per_step + j
        window.append((pl.multiple_of(r0_ref[r] * GRID_W, GRID_W), type_ref[r]))

    for idx, (j, p) in enumerate(tiles):
        start, rtype = window[j]
        cols = slice(p * LANES, (p + 1) * LANES)
        qq = _stack_heads(q_ref[0, j * GRID_W:(j + 1) * GRID_W, cols])
        s_ref[idx, :, :n_lat] = _dot_nt(qq, k_ref[0, pl.ds(start, n_lat), cols]) + bias_ref[rtype, p]
        s_ref[idx, :, n_lat:] = _dot_nt(qq, kc_ref[0, :, cols])

    denoms = []
    for idx in range(len(tiles)):
        s = s_ref[idx]
        e = jnp.exp2(s - jnp.max(s, axis=-1, keepdims=True))
        denoms.append(jnp.sum(e, axis=-1, keepdims=True))
        p_ref[idx] = e.astype(BF16)

    lane = lax.broadcasted_iota(jnp.int32, (GRID_W, LANES), 1)
    for idx, (j, p) in enumerate(tiles):
        start, _ = window[j]
        cols = slice(p * LANES, (p + 1) * LANES)
        o = _dot(p_ref[idx, :, :n_lat], v_ref[0, pl.ds(start, n_lat), cols]) + _dot(p_ref[idx, :, n_lat:],
                                                                                 vc_ref[0, :, cols])
        o = o * (1.0 / denoms[idx])
        o = jnp.where(lane < NA_HEAD_DIM, o[:GRID_W], o[GRID_W:])
        o_ref[0, j * GRID_W:(j + 1) * GRID_W, cols] = o.astype(o_ref.dtype)


def _neighbourhood_attention(proj, proj_c, bias_tab, r0, row_type, kh):
    b, t, _ = proj.shape
    l = proj_c.shape[1]
    rows = t // GRID_W
    w = NA_WIDTH
    rps = int(np.gcd(rows, NA_ROWS_PER_STEP))
    q_rows = rps * GRID_W
    grid_spec = pltpu.PrefetchScalarGridSpec(
        num_scalar_prefetch=2,
        grid=(b, rows // rps),
        in_specs=[
            pl.BlockSpec((1, q_rows, w), lambda bi, r, *_: (bi, r, 0)),
            pl.BlockSpec((1, t, w), lambda bi, r, *_: (bi, 0, 1)),
            pl.BlockSpec((1, t, w), lambda bi, r, *_: (bi, 0, 2)),
            pl.BlockSpec((1, l, w), lambda bi, r, *_: (bi, 0, 1)),
            pl.BlockSpec((1, l, w), lambda bi, r, *_: (bi, 0, 2)),
            pl.BlockSpec(bias_tab.shape, lambda bi, r, *_: (0, 0, 0, 0)),
        ],
        out_specs=pl.BlockSpec((1, q_rows, w), lambda bi, r, *_: (bi, r, 0)),
        scratch_shapes=[pltpu.VMEM((rps * HEAD_PAIRS, 2 * GRID_W, kh * GRID_W + l), F32),
                        pltpu.VMEM((rps * HEAD_PAIRS, 2 * GRID_W, kh * GRID_W + l), BF16)],
    )
    return pl.pallas_call(
        functools.partial(_na_kernel, kh=kh, rows_per_step=rps),
        out_shape=jax.ShapeDtypeStruct((b, t, w), BF16),
        grid_spec=grid_spec,
        compiler_params=_params(("parallel", "arbitrary")),
        name="na_attention",
    )(jnp.asarray(r0), jnp.asarray(row_type), proj, proj, proj, proj_c, proj_c, bias_tab)


def _ctx_attn_kernel(q_ref, k_ref, v_ref, o_ref):
    for p in range(HEAD_PAIRS):
        cols = slice(p * LANES, (p + 1) * LANES)
        o = _pair_attention(q_ref[0, :, cols], [k_ref[0, :, cols]], [v_ref[0, :, cols]], [None])
        o_ref[0, :, cols] = o.astype(o_ref.dtype)


def _context_attention(proj_c):
    b, l, _ = proj_c.shape
    w = NA_WIDTH
    return pl.pallas_call(
        _ctx_attn_kernel,
        out_shape=jax.ShapeDtypeStruct((b, l, w), BF16),
        grid=(b,),
        in_specs=[pl.BlockSpec((1, l, w), lambda bi, j=j: (bi, 0, j)) for j in range(3)],
        out_specs=pl.BlockSpec((1, l, w), lambda bi: (bi, 0, 0)),
        compiler_params=_params(("parallel",)),
        name="ctx_attention",
    )(proj_c, proj_c, proj_c)


def _top4(logits):
    lane = lax.broadcasted_iota(jnp.int32, logits.shape, 1)
    cur = jnp.where(lane < N_EXPERTS, logits, -jnp.inf)
    vals, idxs = [], []
    for _ in range(TOP_K):
        m = jnp.max(cur, axis=-1, keepdims=True)
        first = jnp.min(jnp.where(cur == m, lane, ROUTER_PAD).astype(F32), axis=-1, keepdims=True)
        idx = first.astype(jnp.int32)
        vals.append(m)
        idxs.append(idx)
        cur = jnp.where(lane == idx, -jnp.inf, cur)
    exps = [jnp.exp(v - vals[0]) for v in vals]
    inv = 1.0 / functools.reduce(jnp.add, exps)
    ids = jnp.zeros(logits.shape, jnp.int32)
    gates = jnp.zeros(logits.shape, F32)
    for k in range(TOP_K):
        ids = jnp.where(lane == k, idxs[k], ids)
        gates = jnp.where(lane == k, exps[k] * inv, gates)
    return ids, gates


def _layer_tail(y, rows, x_ref, g1_ref, n2_ref, sh2_ref, sc2_ref, wr_ref, br_ref, xo_ref, h2_ref, ids_ref, gates_ref):
    x_new = x_ref[0, rows] + g1_ref[0] * y
    xo_ref[0, rows] = x_new
    h2 = _rms_mod(x_new, n2_ref[...], sh2_ref[0], sc2_ref[0])
    h2_ref[0, rows] = _pack_rows(h2)
    logits = _dot(h2.astype(BF16), wr_ref[...]) + br_ref[...]
    ids, gates = _top4(logits)
    ids_ref[0, rows] = ids
    gates_ref[0, rows] = gates


def _sub_tiles(n_rows):
    sub = min(OUT_SUB_TILE, n_rows)
    return [slice(s, s + sub) for s in range(0, n_rows, sub)]


def _tail_specs(b, t, d, tm):
    row = lambda bi, i: (bi, i, 0)
    per_b = lambda bi, i: (bi, 0, 0)
    const = lambda bi, i: (0, 0)
    in_specs = [
        pl.BlockSpec((1, tm, d), row),
        pl.BlockSpec((1, 1, d), per_b),
        pl.BlockSpec((1, d), const),
        pl.BlockSpec((1, 1, d), per_b),
        pl.BlockSpec((1, 1, d), per_b),
        pl.BlockSpec((d, ROUTER_PAD), const),
        pl.BlockSpec((1, ROUTER_PAD), const),
    ]
    out_specs = [
        pl.BlockSpec((1, tm, d), row),
        pl.BlockSpec((1, tm, d // 2), row),
        pl.BlockSpec((1, tm, ROUTER_PAD), row),
        pl.BlockSpec((1, tm, ROUTER_PAD), row),
    ]
    out_shape = [
        jax.ShapeDtypeStruct((b, t, d), F32),
        jax.ShapeDtypeStruct((b, t, d // 2), jnp.int32),
        jax.ShapeDtypeStruct((b, t, ROUTER_PAD), jnp.int32),
        jax.ShapeDtypeStruct((b, t, ROUTER_PAD), F32),
    ]
    return in_specs, out_specs, out_shape


def _halo_fix(rolled, at_row, halo_row, present):
    n = rolled.shape[0]
    first = at_row < SUBLANES
    assert first or at_row >= n - SUBLANES
    slab = rolled[:SUBLANES] if first else rolled[n - SUBLANES:]
    sub = lax.broadcasted_iota(jnp.int32, slab.shape, 0)
    fill = jnp.where(present, halo_row, jnp.zeros_like(halo_row))
    slab = jnp.where(sub == at_row % SUBLANES, fill, slab)
    return jnp.concatenate([slab, rolled[SUBLANES:]] if first else [rolled[:n - SUBLANES], slab], axis=0)


def _even_out_kernel(oa_ref, bg_ref, cg_ref, xin_ref, cgp_ref, xinp_ref, cgn_ref, xinn_ref, cw_ref, cb_ref,
                     wa_ref, wb_ref, *tail_refs, tm):
    i = pl.program_id(1)
    has_prev = i > 0
    has_next = i < pl.num_programs(1) - 1
    u = cg_ref[0].astype(F32) * xin_ref[0].astype(F32)
    u_prev = (cgp_ref[0].astype(F32) * xinp_ref[0].astype(F32))[HALO - 1:HALO]
    u_next = (cgn_ref[0].astype(F32) * xinn_ref[0].astype(F32))[0:1]
    u_m1 = _halo_fix(pltpu.roll(u, 1, 0), 0, u_prev, has_prev)
    u_p1 = _halo_fix(pltpu.roll(u, tm - 1, 0), tm - 1, u_next, has_next)
    cw = cw_ref[...]
    conv = u_m1 * cw[0:1] + u * cw[1:2] + u_p1 * cw[2:3] + cb_ref[...]
    o_b = (bg_ref[0].astype(F32) * conv).astype(BF16)
    for rows in _sub_tiles(tm):
        y = _dot(oa_ref[0, rows], wa_ref[...]) + _dot(o_b[rows], wb_ref[...])
        _layer_tail(y, rows, *tail_refs)


def _even_out(o_a, proj, conv_w, conv_b, w_out, x, g1, n2, sh2, sc2, w_r, b_r):
    b, t, d = x.shape
    w = NA_WIDTH
    tm = min(t, OUT_TILE)
    hb = tm // HALO
    n_hblocks = t // HALO
    row = lambda bi, i: (bi, i, 0)
    const = lambda bi, i: (0, 0)
    prev = lambda col: (lambda bi, i: (bi, jnp.maximum(i * hb - 1, 0), col))
    nxt = lambda col: (lambda bi, i: (bi, jnp.minimum((i + 1) * hb, n_hblocks - 1), col))
    tail_in, out_specs, out_shape = _tail_specs(b, t, d, tm)
    in_specs = [
        pl.BlockSpec((1, tm, w), row),
        pl.BlockSpec((1, tm, w), lambda bi, i: (bi, i, 3)),
        pl.BlockSpec((1, tm, w), lambda bi, i: (bi, i, 4)),
        pl.BlockSpec((1, tm, w), lambda bi, i: (bi, i, 5)),
        pl.BlockSpec((1, HALO, w), prev(4)),
        pl.BlockSpec((1, HALO, w), prev(5)),
        pl.BlockSpec((1, HALO, w), nxt(4)),
        pl.BlockSpec((1, HALO, w), nxt(5)),
        pl.BlockSpec((SC_CONV, w), const),
        pl.BlockSpec((1, w), const),
        pl.BlockSpec((w, d), const),
        pl.BlockSpec((w, d), const),
    ] + tail_in
    return pl.pallas_call(
        functools.partial(_even_out_kernel, tm=tm),
        out_shape=out_shape,
        grid=(b, t // tm),
        in_specs=in_specs,
        out_specs=out_specs,
        compiler_params=_params(("parallel", "parallel")),
        name="even_out",
    )(o_a, proj, proj, proj, proj, proj, proj, proj, conv_w, conv_b.reshape(1, w),
      w_out[:w], w_out[w:], x, g1, n2, sh2, sc2, w_r, b_r)


def _log_sigmoid(x):
    return jnp.minimum(x, 0.0) - jnp.log1p(jnp.exp(-jnp.abs(x)))


def _sigmoid(x):
    return 0.5 + 0.5 * jnp.tanh(0.5 * x)


def _lru_sub_tile(ucb_time, state, perm_ref, w_ref, ba_ref, bx_ref, lam_ref, want_hidden, reverse):
    sub, width = ucb_time.shape
    blk = width // LRU_BLOCKS
    steps = sub // SUBLANES
    uc = _dot(perm_ref[0], ucb_time)
    ucb = uc.astype(BF16)
    za, zx = [], []
    for h in range(LRU_BLOCKS):
        z = _dot(ucb[:, h * blk:(h + 1) * blk], w_ref[h])
        za.append(z[:, :blk])
        zx.append(z[:, blk:])
    r = _sigmoid(jnp.concatenate(za, axis=1) + ba_ref[...])
    gate_i = _sigmoid(jnp.concatenate(zx, axis=1) + bx_ref[...])
    log_a = (LRU_C * _log_sigmoid(lam_ref[...])) * r
    a = jnp.exp(log_a)
    th = jnp.tanh(log_a)
    num = -2.0 * th
    mult = jnp.where(num > 0.0, num * lax.rsqrt(num * (1.0 - th)), 0.0)
    bcoef = mult * gate_i * uc

    grp = lambda v, j: v[j * SUBLANES:(j + 1) * SUBLANES]
    prods, local = [grp(a, 0)], [grp(bcoef, 0)]
    for j in range(1, steps):
        aj = grp(a, j)
        local.append(aj * local[-1] + grp(bcoef, j))
        prods.append(aj * prods[-1])

    carry_in = [None] * SUBLANES
    for s in (range(SUBLANES - 1, -1, -1) if reverse else range(SUBLANES)):
        carry_in[s] = state
        state = prods[-1][s:s + 1] * state + local[-1][s:s + 1]
    if not want_hidden:
        return state, None
    start = jnp.concatenate(carry_in, axis=0)
    hidden = jnp.concatenate([prods[j] * start + local[j] for j in range(steps)], axis=0)
    return state, _dot(perm_ref[1], hidden.astype(BF16)).astype(BF16)


def _lru_tile(u_ref, perm_ref, w_ref, ba_ref, bx_ref, lam_ref, carry_ref, o_ref, *, reverse):
    sub = perm_ref.shape[1]
    n_sub = u_ref.shape[1] // sub
    state = carry_ref[0:1, :]
    for k in (range(n_sub - 1, -1, -1) if reverse else range(n_sub)):
        rows = slice(k * sub, (k + 1) * sub)
        state, hidden = _lru_sub_tile(u_ref[0, rows], state, perm_ref, w_ref, ba_ref, bx_ref, lam_ref,
                                      o_ref is not None, reverse)
        if o_ref is not None:
            o_ref[0, rows] = hidden
    carry_ref[...] = jnp.broadcast_to(state, carry_ref.shape)


def _lru_kernel(uc_ref, ul_ref, perm_ref, w_ref, ba_ref, bx_ref, lam_ref, o_ref, carry_ref, *, n_ctx_tiles, reverse):
    j = pl.program_id(1)
    shared = (perm_ref, w_ref, ba_ref, bx_ref, lam_ref, carry_ref)

    @pl.when(j == 0)
    def _():
        carry_ref[...] = jnp.zeros_like(carry_ref)

    @pl.when(j < n_ctx_tiles)
    def _():
        _lru_tile(uc_ref, *shared, None, reverse=reverse)

    @pl.when(j >= n_ctx_tiles)
    def _():
        _lru_tile(ul_ref, *shared, o_ref, reverse=reverse)


def _lru_scan(proj, u_ctx, w_cat, ba, bx, lam, reverse):
    b, t, _ = proj.shape
    l, width = u_ctx.shape[1], u_ctx.shape[2]
    sub = min(256, l, t)
    tc = sub * min(LRU_SUB_TILES, l // sub)
    tl = sub * min(LRU_SUB_TILES, t // sub)
    n_c, n_l = l // tc, t // tl

    def pos_of(step, n):
        step = jnp.clip(step, 0, n - 1)
        return (n - 1 - step) if reverse else step

    def tile_map(off, n, col):
        return lambda bi, j: (bi, pos_of(j - off, n), col)

    const2 = lambda bi, j: (0, 0)
    in_specs = [
        pl.BlockSpec((1, tc, width), tile_map(0, n_c, 0)),
        pl.BlockSpec((1, tl, width), tile_map(n_c, n_l, 1)),
        pl.BlockSpec((2, sub, sub), lambda bi, j: (0, 0, 0)),
        pl.BlockSpec(w_cat.shape, lambda bi, j: (0, 0, 0)),
        pl.BlockSpec((1, width), const2),
        pl.BlockSpec((1, width), const2),
        pl.BlockSpec((1, width), const2),
    ]
    steps = sub // SUBLANES
    step, block = np.divmod(np.arange(sub), SUBLANES)
    time_of_row = block * steps + (steps - 1 - step if reverse else step)
    perm = (time_of_row[:, None] == np.arange(sub)[None]).astype(np.float32)
    perms = jnp.asarray(np.stack([perm, perm.T]), BF16)
    kern = functools.partial(_lru_kernel, n_ctx_tiles=n_c, reverse=reverse)
    return pl.pallas_call(
        kern,
        out_shape=jax.ShapeDtypeStruct((b, t, width), BF16),
        grid=(b, n_c + n_l),
        in_specs=in_specs,
        out_specs=pl.BlockSpec((1, tl, width), tile_map(n_c, n_l, 0)),
        scratch_shapes=[pltpu.VMEM((SUBLANES, width), F32)],
        compiler_params=_params(("parallel", "arbitrary")),
        name="lru_scan_bwd" if reverse else "lru_scan_fwd",
    )(u_ctx, proj, perms, w_cat, ba.reshape(1, width), bx.reshape(1, width), lam.reshape(1, width))


def _odd_out_kernel(hf_ref, hb_ref, gate_ref, w_ref, *tail_refs):
    for rows in _sub_tiles(hf_ref.shape[1]):
        hsum = hf_ref[0, rows].astype(F32) + hb_ref[0, rows].astype(F32)
        z = hsum * jax.nn.gelu(gate_ref[0, rows].astype(F32), approximate=True)
        y = _dot(z.astype(BF16), w_ref[...])
        _layer_tail(y, rows, *tail_refs)


def _odd_out(h_f, h_b, proj, w_out, x, g1, n2, sh2, sc2, w_r, b_r):
    b, t, d = x.shape
    width = h_f.shape[-1]
    tm = min(t, OUT_TILE)
    row = lambda bi, i: (bi, i, 0)
    tail_in, out_specs, out_shape = _tail_specs(b, t, d, tm)
    in_specs = [
        pl.BlockSpec((1, tm, width), row),
        pl.BlockSpec((1, tm, width), row),
        pl.BlockSpec((1, tm, width), row),
        pl.BlockSpec((width, d), lambda bi, i: (0, 0)),
    ] + tail_in
    return pl.pallas_call(
        _odd_out_kernel,
        out_shape=out_shape,
        grid=(b, t // tm),
        in_specs=in_specs,
        out_specs=out_specs,
        compiler_params=_params(("parallel", "parallel")),
        name="odd_out",
    )(h_f, h_b, proj, w_out, x, g1, n2, sh2, sc2, w_r, b_r)


def _expert_kernel(be_ref, next_ref, rows_ref, nb_ref, x_ref, wg_hbm, bg_ref, wu_hbm, bu_ref, wd_hbm, bd_ref,
                   o_ref, w_f32, wg_bf, wu_bf, wd_bf, h_bf, sem, *, layer, chunk):
    i = pl.program_id(0)
    expert = be_ref[i]
    used = i < nb_ref[0]

    def weight_copies(e):
        return [pltpu.make_async_copy(src.at[layer, e], w_f32.at[j], sem.at[j])
                for j, src in enumerate((wg_hbm, wu_hbm, wd_hbm))]

    @pl.when(i == 0)
    def _():
        for cp in weight_copies(expert):
            cp.start()

    @pl.when(used & ((i == 0) | (expert != be_ref[jnp.maximum(i - 1, 0)])))
    def _():
        for cp in weight_copies(expert):
            cp.wait()
        wg_bf[...] = w_f32[0].astype(BF16)
        wu_bf[...] = w_f32[1].astype(BF16)
        wd_bf[...] = w_f32[2].astype(BF16)

        @pl.when(next_ref[i] >= 0)
        def _():
            for cp in weight_copies(next_ref[i]):
                cp.start()

    @pl.when(used)
    def _():
        xp = x_ref[...]
        row = lax.broadcasted_iota(jnp.int32, xp.shape, 0)
        xp = jnp.where(row < rows_ref[i], xp, 0)
        x = jnp.concatenate(_unpack_rows(xp), axis=1).astype(BF16)
        for c in range(wg_bf.shape[1] // chunk):
            cs = slice(c * chunk, (c + 1) * chunk)
            g = jnp.minimum(_dot(x, wg_bf[:, cs]) + bg_ref[0, 0, :, cs], SWIGLU_LIMIT)
            u = jnp.clip(_dot(x, wu_bf[:, cs]) + bu_ref[0, 0, :, cs], -SWIGLU_LIMIT, SWIGLU_LIMIT)
            h_bf[:, cs] = (g * jax.nn.sigmoid(SWIGLU_ALPHA * g) * (u + 1.0)).astype(BF16)
        o_ref[...] = _pack_rows(_dot(h_bf[...], wd_bf[...]) + bd_ref[0, 0])

    @pl.when(jnp.logical_not(used))
    def _():
        o_ref[...] = jnp.zeros_like(o_ref)


def _experts(xb, block_e, next_e, block_rows, n_used, layer, wg, bg, wu, bu, wd, bd):
    n_slots = xb.shape[0]
    depth, n_e, d, d_exp = wg.shape
    assert d == d_exp
    tm = EXPERT_TILE
    n_blocks = n_slots // tm
    xmap = lambda i, be, ne, br, nb: (jnp.minimum(i, nb[0] - 1), 0)
    bmap = lambda i, be, ne, br, nb: (layer, be[i], 0, 0)
    hbm = pl.BlockSpec(memory_space=pl.ANY)
    grid_spec = pltpu.PrefetchScalarGridSpec(
        num_scalar_prefetch=4,
        grid=(n_blocks,),
        in_specs=[
            pl.BlockSpec((tm, d // 2), xmap),
            hbm,
            pl.BlockSpec((1, 1, 1, d_exp), bmap),
            hbm,
            pl.BlockSpec((1, 1, 1, d_exp), bmap),
            hbm,
            pl.BlockSpec((1, 1, 1, d), bmap),
        ],
        out_specs=pl.BlockSpec((tm, d // 2), lambda i, be, ne, br, nb: (i, 0)),
        scratch_shapes=[pltpu.VMEM((3, d, d_exp), F32), pltpu.VMEM((d, d_exp), BF16), pltpu.VMEM((d, d_exp), BF16),
                        pltpu.VMEM((d_exp, d), BF16), pltpu.VMEM((tm, d_exp), BF16), pltpu.SemaphoreType.DMA((3,))],
    )
    return pl.pallas_call(
        functools.partial(_expert_kernel, layer=layer, chunk=256),
        out_shape=jax.ShapeDtypeStruct((n_slots, d // 2), jnp.int32),
        grid_spec=grid_spec,
        compiler_params=_params(("arbitrary",)),
        name="experts",
    )(block_e, next_e, block_rows, n_used, xb, wg, bg.reshape(depth, n_e, 1, d_exp), wu,
      bu.reshape(depth, n_e, 1, d_exp), wd, bd.reshape(depth, n_e, 1, d))


def _combine_kernel(x_ref, g2_ref, gates_ref, y_ref, o_ref):
    gates = gates_ref[0]
    half = y_ref.shape[-1]
    acc_lo = jnp.zeros((x_ref.shape[1], half), F32)
    acc_hi = jnp.zeros((x_ref.shape[1], half), F32)
    for k in range(TOP_K):
        lo, hi = _unpack_rows(y_ref[k])
        acc_lo = acc_lo + gates[:, k:k + 1] * lo
        acc_hi = acc_hi + gates[:, k:k + 1] * hi
    o_ref[0] = x_ref[0] + g2_ref[0] * jnp.concatenate([acc_lo, acc_hi], axis=1)


def _combine(x, g2, gates, y_sel, tok_offset):
    b, t, d = x.shape
    tm = int(np.gcd(min(t, 512), tok_offset)) if tok_offset else min(t, 512)
    n_t = t // tm
    off = tok_offset // tm
    row = lambda bi, i: (bi, i, 0)
    return pl.pallas_call(
        _combine_kernel,
        out_shape=jax.ShapeDtypeStruct((b, t, d), F32),
        grid=(b, n_t),
        in_specs=[
            pl.BlockSpec((1, tm, d), row),
            pl.BlockSpec((1, 1, d), lambda bi, i: (bi, 0, 0)),
            pl.BlockSpec((1, tm, ROUTER_PAD), row),
            pl.BlockSpec((TOP_K, tm, d // 2), lambda bi, i: (0, off + bi * n_t + i, 0)),
        ],
        out_specs=pl.BlockSpec((1, tm, d), row),
        input_output_aliases={0: 0},
        compiler_params=_params(("parallel", "parallel")),
        name="moe_combine",
    )(x, g2, gates, y_sel)


def _route_kernel(ids_ref, upper_ref, dest_ref, counts_ref, run_ref, *, tile):
    p = pl.program_id(0)
    i = pl.program_id(1)
    tm = ids_ref.shape[0]
    ids_t = ids_ref[...].T
    expert = lax.broadcasted_iota(jnp.int32, (N_EXPERTS, tm), 0)
    chosen = [ids_t[k:k + 1, :] == expert for k in range(TOP_K)]
    picks = functools.reduce(jnp.add, [c.astype(F32) for c in chosen])
    tile_counts = jnp.sum(picks, axis=1, keepdims=True)

    @pl.when((p == 0) & (i == 0))
    def _():
        run_ref[...] = jnp.zeros_like(run_ref)

    @pl.when(p == 0)
    def _():
        run_ref[...] += tile_counts

    @pl.when((p == 1) & (i == 0))
    def _():
        counts = run_ref[...]
        counts_ref[...] = counts.astype(jnp.int32)
        padded = jnp.floor((counts + (tile - 1)) * (1.0 / tile)) * tile
        row = lax.broadcasted_iota(jnp.int32, counts.shape, 0)
        ends = padded
        for s in (1, 2, 4, 8, 16):
            ends = ends + jnp.where(row >= s, pltpu.roll(ends, s, 0), 0.0)
        run_ref[...] = ends - padded

    @pl.when(p == 1)
    def _():
        before = _dot(picks.astype(BF16), upper_ref[...])
        slot = before + run_ref[:, 0:1]
        rows = [jnp.sum(jnp.where(c, slot, 0.0), axis=0, keepdims=True) for c in chosen]
        rows += [jnp.zeros_like(rows[0])] * (dest_ref.shape[0] - TOP_K)
        dest_ref[...] = jnp.concatenate(rows, axis=0).astype(jnp.int32)
        run_ref[...] += tile_counts


def _route(ids):
    n = ids.shape[0]
    tm = int(np.gcd(n, 1024))
    n_tiles = n // tm
    upper = jnp.asarray(np.triu(np.ones((tm, tm), np.float32), 1), BF16)
    dest, counts = pl.pallas_call(
        functools.partial(_route_kernel, tile=EXPERT_TILE),
        out_shape=[jax.ShapeDtypeStruct((SUBLANES, n), jnp.int32),
                   jax.ShapeDtypeStruct((N_EXPERTS, LANES), jnp.int32)],
        grid=(2, n_tiles),
        in_specs=[pl.BlockSpec((tm, ROUTER_PAD), lambda p, i: (i, 0)),
                  pl.BlockSpec((tm, tm), lambda p, i: (0, 0))],
        out_specs=[pl.BlockSpec((SUBLANES, tm), lambda p, i: (0, i * p)),
                   pl.BlockSpec((N_EXPERTS, LANES), lambda p, i: (0, 0))],
        scratch_shapes=[pltpu.VMEM((N_EXPERTS, LANES), F32)],
        compiler_params=_params(("arbitrary", "arbitrary")),
        name="route",
    )(ids, upper)
    return dest, counts[:, 0]


def _row_gather(table, idx):
    info = plsc.get_sparse_core_info()
    n_cores, n_workers = info.num_cores, info.num_cores * info.num_subcores
    n_rows, width = idx.shape[0], table.shape[1]
    chunk_rows = SC_GATHER_ROWS
    per_worker = n_rows // n_workers
    n_chunks = per_worker // chunk_rows
    assert per_worker * n_workers == n_rows and n_chunks * chunk_rows == per_worker and n_chunks % 2 == 0
    mesh = plsc.VectorSubcoreMesh(core_axis_name="c", subcore_axis_name="s")

    @functools.partial(
        pl.kernel, mesh=mesh,
        out_type=jax.ShapeDtypeStruct((n_rows, width), table.dtype),
        scratch_types=[pltpu.VMEM((n_chunks, chunk_rows), jnp.int32), pltpu.VMEM((2, chunk_rows, width), table.dtype),
                       pltpu.SemaphoreType.DMA((2,)), pltpu.SemaphoreType.DMA((2,))],
    )
    def gather_kernel(table_hbm, idx_hbm, out_hbm, idx_v, rows_v, gather_sem, write_sem):
        worker = lax.axis_index("s") * n_cores + lax.axis_index("c")
        pltpu.sync_copy(idx_hbm.at[worker], idx_v)

        def gather(c, slot):
            return pltpu.make_async_copy(table_hbm.at[idx_v.at[c]], rows_v.at[slot], gather_sem.at[slot])

        def write(c, slot):
            first_row = worker * per_worker + c * chunk_rows
            return pltpu.make_async_copy(rows_v.at[slot], out_hbm.at[pl.ds(first_row, chunk_rows)], write_sem.at[slot])

        gather(0, 0).start()

        @pl.loop(0, n_chunks, step=2)
        def _(c0):
            for slot in (0, 1):
                c = c0 + slot
                gather(c, slot).wait()

                @pl.when(c >= 1)
                def _():
                    write(c - 1, 1 - slot).wait()

                @pl.when(c + 1 < n_chunks)
                def _():
                    gather(c + 1, 1 - slot).start()

                write(c, slot).start()

        write(n_chunks - 1, 1).wait()

    return gather_kernel(table, idx.reshape(n_workers, n_chunks, chunk_rows))


def _row_scatter(rows, dest, n_out):
    info = plsc.get_sparse_core_info()
    n_cores, n_workers = info.num_cores, info.num_cores * info.num_subcores
    n_choices, n_rows = dest.shape
    width = rows.shape[1]
    chunk_rows = SC_GATHER_ROWS
    per_worker = n_rows // (n_workers * chunk_rows)
    assert per_worker * n_workers * chunk_rows == n_rows == rows.shape[0]
    idx = dest.reshape(n_choices, n_workers, per_worker, chunk_rows).transpose(1, 0, 2, 3)
    mesh = plsc.VectorSubcoreMesh(core_axis_name="c", subcore_axis_name="s")

    @functools.partial(
        pl.kernel, mesh=mesh,
        out_type=jax.ShapeDtypeStruct((n_out, width), rows.dtype),
        scratch_types=[pltpu.VMEM((n_choices, per_worker, chunk_rows), jnp.int32),
                       pltpu.VMEM((chunk_rows, width), rows.dtype), pltpu.SemaphoreType.DMA],
    )
    def scatter_kernel(rows_hbm, idx_hbm, out_hbm, idx_v, rows_v, sem):
        worker = lax.axis_index("s") * n_cores + lax.axis_index("c")
        pltpu.sync_copy(idx_hbm.at[worker], idx_v)

        @pl.loop(0, per_worker)
        def _(c):
            first_row = (worker * per_worker + c) * chunk_rows
            pltpu.sync_copy(rows_hbm.at[pl.ds(first_row, chunk_rows)], rows_v)
            copies = [pltpu.async_copy(rows_v, out_hbm.at[idx_v.at[k, c]], sem) for k in range(n_choices)]
            for cp in copies:
                cp.wait()

    return scatter_kernel(rows, idx)


def _moe(h2, ids, layer, w_exp):
    n = h2.shape[0]
    nk = n * TOP_K
    tm = EXPERT_TILE
    dest, counts = _route(ids)
    dest = dest[:TOP_K]
    padded = (counts + tm - 1) // tm * tm
    pends = jnp.cumsum(padded)
    n_blocks = -(-nk // tm) + N_EXPERTS
    n_slots = n_blocks * tm
    block_start = jnp.arange(n_blocks, dtype=jnp.int32) * tm
    block_e = jnp.minimum(jnp.sum((pends[None] <= block_start[:, None]).astype(jnp.int32), axis=1), N_EXPERTS - 1)
    is_e = (block_e[:, None] == jnp.arange(N_EXPERTS, dtype=jnp.int32)[None]).astype(jnp.int32)
    block_rows = jnp.clip(jnp.sum(is_e * (pends - padded + counts)[None], axis=1) - block_start, 0, tm)
    n_used = (pends[-1] // tm).astype(jnp.int32).reshape(1)
    e_ids = jnp.arange(N_EXPERTS, dtype=jnp.int32)
    later = (e_ids[None] > e_ids[:, None]) & (counts[None] > 0)
    next_nonempty = jnp.min(jnp.where(later, e_ids[None], N_EXPERTS), axis=1)
    next_nonempty = jnp.where(next_nonempty == N_EXPERTS, -1, next_nonempty)
    next_e = jnp.sum(is_e * next_nonempty[None], axis=1)
    xb = _row_scatter(h2, dest, n_slots)
    return _experts(xb, block_e, next_e, block_rows, n_used, layer, *w_exp), dest


def _gather_choices(yb, dest, lo, hi):
    return _row_gather(yb, dest[:, lo:hi].reshape(-1)).reshape(TOP_K, hi - lo, -1)


def _mod_parts(mod_l, b):
    d = mod_l.shape[-1] // 6
    lat = [mod_l[:b, k * d:(k + 1) * d].reshape(b, 1, d) for k in range(6)]
    ctx = [jnp.broadcast_to(mod_l[b, k * d:(k + 1) * d].reshape(1, 1, d), (b, 1, d)) for k in range(6)]
    return lat, ctx


def kernel(x, c, ctx, c_ctx, ada_w, ada_b, norm1_g, norm2_g, ev_w_in, ev_w_out, ev_q_gain, ev_k_gain, ev_rpb, ev_conv_w, ev_conv_b, od_w_in, od_w_out, od_conv_w, od_conv_b, od_fwd_wa, od_fwd_ba, od_fwd_wx, od_fwd_bx, od_fwd_lam, od_bwd_wa, od_bwd_ba, od_bwd_wx, od_bwd_bx, od_bwd_lam, router_w, router_b, exp_w_gate, exp_b_gate, exp_w_up, exp_b_up, exp_w_down, exp_b_down):
    b, t, d = x.shape
    l = ctx.shape[1]
    assert ada_w.shape[0] == DEPTH == 2 and t % GRID_W == 0 and t // GRID_W >= WIN_H

    n_rows_c = -(-(b + 1) // SUBLANES) * SUBLANES
    cvec = jnp.zeros((n_rows_c, d), F32).at[:b].set(c).at[b].set(c_ctx)
    mod = _ada_mod(cvec, ada_w, ada_b)

    def router(layer):
        w_r = jnp.zeros((d, ROUTER_PAD), F32).at[:, :N_EXPERTS].set(router_w[layer]).astype(BF16)
        b_r = jnp.zeros((1, ROUTER_PAD), F32).at[0, :N_EXPERTS].set(router_b[layer])
        return w_r, b_r

    w_exp = (exp_w_gate, exp_b_gate, exp_w_up, exp_b_up, exp_w_down, exp_b_down)

    (sh1, sc1, g1, sh2, sc2, g2), (csh1, csc1, cg1, csh2, csc2, cg2) = _mod_parts(mod[0], b)
    n1 = norm1_g[0].reshape(1, d)
    n2 = norm2_g[0].reshape(1, d)
    w_in = ev_w_in[0].astype(BF16)
    w_out = ev_w_out[0].astype(BF16)
    q_scale = NA_HEAD_DIM ** -0.5 * LOG2_E
    head_gain = jnp.stack([jnp.tile(ev_q_gain[0] * q_scale, NA_HEADS), jnp.tile(ev_k_gain[0], NA_HEADS)])
    head_gain = head_gain.reshape(2, 1, NA_WIDTH).astype(F32)
    proj = _inproj(x, n1, sh1, sc1, w_in, head_gain)
    proj_c = _inproj(ctx, n1, csh1, csc1, w_in, head_gain)
    kh, r0, row_type, patterns = _na_tables(t // GRID_W)
    bias_tab = _na_bias_table(ev_rpb[0], patterns)
    o_a = _neighbourhood_attention(proj, proj_c, bias_tab, r0, row_type, kh)
    oc_a = _context_attention(proj_c)
    w_r, b_r = router(0)
    x1, h2, ids, gates = _even_out(o_a, proj, ev_conv_w[0], ev_conv_b[0], w_out, x, g1, n2, sh2, sc2, w_r, b_r)
    c1, h2c, ids_c, gates_c = _even_out(oc_a, proj_c, ev_conv_w[0], ev_conv_b[0], w_out, ctx, cg1, n2, csh2, csc2,
                                        w_r, b_r)
    tokens = jnp.concatenate([h2c.reshape(b * l, d // 2), h2.reshape(b * t, d // 2)], axis=0)
    ids_all = jnp.concatenate([ids_c.reshape(b * l, ROUTER_PAD), ids.reshape(b * t, ROUTER_PAD)], axis=0)
    yb, dest = _moe(tokens, ids_all, 0, w_exp)
    y_sel = _gather_choices(yb, dest, 0, b * (l + t))
    hctx = _combine(c1, cg2, gates_c, y_sel, 0)
    x = _combine(x1, g2, gates, y_sel, b * l)

    (sh1, sc1, g1, sh2, sc2, g2), (csh1, csc1, _, _, _, _) = _mod_parts(mod[1], b)
    n1 = norm1_g[1].reshape(1, d)
    n2 = norm2_g[1].reshape(1, d)
    w_in = od_w_in[0].astype(BF16)
    width = w_in.shape[1] // 2
    proj = _inproj_conv(x, n1, sh1, sc1, w_in, od_conv_w[0], od_conv_b[0], width)
    u_ctx = _inproj_conv(hctx, n1, csh1, csc1, w_in[:, width:], od_conv_w[0], od_conv_b[0], 0)
    h_dir = []
    for reverse, (wa, ba, wx, bx, lam) in ((False, (od_fwd_wa, od_fwd_ba, od_fwd_wx, od_fwd_bx, od_fwd_lam)),
                                           (True, (od_bwd_wa, od_bwd_ba, od_bwd_wx, od_bwd_bx, od_bwd_lam))):
        w_cat = jnp.concatenate([wa[0], wx[0]], axis=-1).astype(BF16)
        h_dir.append(_lru_scan(proj, u_ctx, w_cat, ba[0], bx[0], lam[0], reverse))
    w_r, b_r = router(1)
    x1, h2, ids, gates = _odd_out(h_dir[0], h_dir[1], proj, od_w_out[0].astype(BF16), x, g1, n2, sh2, sc2, w_r, b_r)
    yb, dest = _moe(h2.reshape(b * t, d // 2), ids.reshape(b * t, ROUTER_PAD), 1, w_exp)
    return _combine(x1, g2, gates, _gather_choices(yb, dest, 0, b * t), 0)
```

```python
import functools

import numpy as np
import jax
import jax.numpy as jnp
from jax import lax
from jax.experimental import pallas as pl
from jax.experimental.pallas import tpu as pltpu
from jax.experimental.pallas import tpu_sc as plsc

DEPTH = 2
GRID_W = 64
EPS = 1e-6
NEG_INF = -1e30
LOG2_E = 1.4426950408889634
NA_HEADS = 8
NA_HEAD_DIM = 64
NA_WIDTH = NA_HEADS * NA_HEAD_DIM
HEAD_PAIRS = NA_HEADS // 2
WIN_H = 8
WIN_W = 16
SC_CONV = 3
LRU_BLOCKS = 4
LRU_CONV = 4
LRU_C = 8.0
N_EXPERTS = 32
TOP_K = 4
SWIGLU_LIMIT = 7.0
SWIGLU_ALPHA = 1.702

LANES = 128
SUBLANES = 8
HALO = 16
ROUTER_PAD = LANES
NA_ROWS_PER_STEP = 4
LRU_SUB_TILES = 4
OUT_TILE = 1024
OUT_SUB_TILE = 512
EXPERT_TILE = 512
SC_GATHER_ROWS = 64
VMEM_LIMIT = 56 * 1024 * 1024

F32 = jnp.float32
BF16 = jnp.bfloat16


def _params(sem, vmem=VMEM_LIMIT):
    return pltpu.CompilerParams(dimension_semantics=sem, vmem_limit_bytes=vmem)


def _dot(a, b):
    return jnp.dot(a, b, preferred_element_type=F32)


def _dot_nt(a, b):
    return lax.dot_general(a, b, (((1,), (1,)), ((), ())), preferred_element_type=F32)


def _pack_rows(v):
    w = v.shape[-1] // 2
    lo = lax.bitcast_convert_type(v[:, :w].astype(BF16).astype(F32), jnp.int32)
    hi = lax.bitcast_convert_type(v[:, w:].astype(BF16).astype(F32), jnp.int32)
    return lax.shift_right_logical(lo, 16) | (hi & jnp.int32(-65536))


def _unpack_rows(p):
    lo = lax.bitcast_convert_type(lax.shift_left(p, 16), F32)
    hi = lax.bitcast_convert_type(p & jnp.int32(-65536), F32)
    return lo, hi


def _rms_mod(x, g, shift, scale):
    ms = jnp.mean(x * x, axis=-1, keepdims=True)
    y = x * lax.rsqrt(ms + EPS) * g
    return y * (1.0 + scale) + shift


def _ada_kernel(c_ref, w_ref, b_ref, o_ref):
    c = c_ref[...]
    s = (c * jax.nn.sigmoid(c)).astype(BF16)
    o_ref[0] = _dot(s, w_ref[0].astype(BF16)) + b_ref[0]


def _ada_mod(cvec, ada_w, ada_b):
    depth, d, n = ada_w.shape
    r = cvec.shape[0]
    tn = 1536
    return pl.pallas_call(
        _ada_kernel,
        out_shape=jax.ShapeDtypeStruct((depth, r, n), F32),
        grid=(depth, n // tn),
        in_specs=[
            pl.BlockSpec((r, d), lambda l, j: (0, 0)),
            pl.BlockSpec((1, d, tn), lambda l, j: (l, 0, j)),
            pl.BlockSpec((1, 1, tn), lambda l, j: (l, 0, j)),
        ],
        out_specs=pl.BlockSpec((1, r, tn), lambda l, j: (l, 0, j)),
        compiler_params=_params(("parallel", "parallel")),
        name="ada_mod",
    )(cvec, ada_w, ada_b.reshape(depth, 1, n))


def _inproj_kernel(x_ref, g_ref, sh_ref, sc_ref, w_ref, hg_ref, ones_ref, o_ref, *, n_tiles, tn, n_headnorm):
    h = _rms_mod(x_ref[0], g_ref[...], sh_ref[0], sc_ref[0]).astype(BF16)
    for j in range(n_tiles):
        y = _dot(h, w_ref[:, j * tn:(j + 1) * tn])
        if j < n_headnorm:
            ms = _dot((y * y).astype(BF16), ones_ref[...]) * (1.0 / NA_HEAD_DIM)
            y = y * lax.rsqrt(ms + EPS) * hg_ref[j]
        o_ref[0, :, j * tn:(j + 1) * tn] = y.astype(o_ref.dtype)


def _inproj(x, g, shift, scale, w, head_gain=None):
    b, t, d = x.shape
    n = w.shape[1]
    tn = NA_WIDTH
    tm = min(t, 512)
    n_headnorm = 0 if head_gain is None else head_gain.shape[0]
    if head_gain is None:
        head_gain = jnp.ones((1, 1, tn), F32)
    hid = np.arange(tn) // NA_HEAD_DIM
    ones_bd = jnp.asarray((hid[:, None] == hid[None, :]), BF16)
    kern = functools.partial(_inproj_kernel, n_tiles=n // tn, tn=tn, n_headnorm=n_headnorm)
    return pl.pallas_call(
        kern,
        out_shape=jax.ShapeDtypeStruct((b, t, n), BF16),
        grid=(b, t // tm),
        in_specs=[
            pl.BlockSpec((1, tm, d), lambda bi, i: (bi, i, 0)),
            pl.BlockSpec((1, d), lambda bi, i: (0, 0)),
            pl.BlockSpec((1, 1, d), lambda bi, i: (bi, 0, 0)),
            pl.BlockSpec((1, 1, d), lambda bi, i: (bi, 0, 0)),
            pl.BlockSpec((d, n), lambda bi, i: (0, 0)),
            pl.BlockSpec(head_gain.shape, lambda bi, i: (0, 0, 0)),
            pl.BlockSpec((tn, tn), lambda bi, i: (0, 0)),
        ],
        out_specs=pl.BlockSpec((1, tm, n), lambda bi, i: (bi, i, 0)),
        compiler_params=_params(("parallel", "parallel")),
        name="inproj",
    )(x, g, shift, scale, w, head_gain, ones_bd)


def _inproj_conv_kernel(x_ref, xp_ref, xn_ref, g_ref, sh_ref, sc_ref, w_ref, cw_ref, cb_ref, o_ref, *,
                        n_tiles, n_plain, tn, tm, sub):
    i = pl.program_id(1)
    edge = SUBLANES
    x_ext = jnp.concatenate([xp_ref[0], x_ref[0], xn_ref[0]], axis=0)
    h = _rms_mod(x_ext, g_ref[...], sh_ref[0], sc_ref[0]).astype(BF16)
    first_step = i == 0
    last_step = i == pl.num_programs(1) - 1
    left = (LRU_CONV - 1) // 2
    n_sub = tm // sub
    for s in range(n_sub):
        h_s = h[s * sub:(s + 1) * sub + 2 * edge]
        rows = slice(s * sub, (s + 1) * sub)
        keep_prev = jnp.where(first_step, 0.0, 1.0) if s == 0 else 1.0
        keep_next = jnp.where(last_step, 0.0, 1.0) if s == n_sub - 1 else 1.0
        for j in range(n_tiles):
            y = _dot(h_s, w_ref[:, j * tn:(j + 1) * tn])
            if j < n_plain:
                o_ref[0, rows, j * tn:(j + 1) * tn] = y[edge:edge + sub].astype(o_ref.dtype)
                continue
            u = jnp.concatenate([y[:edge] * keep_prev, y[edge:edge + sub], y[edge + sub:] * keep_next], axis=0)
            cw = cw_ref[:, (j - n_plain) * tn:(j - n_plain + 1) * tn]
            uc = cb_ref[:, (j - n_plain) * tn:(j - n_plain + 1) * tn]
            n_ext = u.shape[0]
            for k in range(LRU_CONV):
                shifted = u if k == left else pltpu.roll(u, (left - k) % n_ext, 0)
                uc = uc + shifted[edge:edge + sub] * cw[k:k + 1]
            o_ref[0, rows, j * tn:(j + 1) * tn] = uc.astype(o_ref.dtype)


def _inproj_conv(x, g, shift, scale, w, conv_w, conv_b, n_plain_cols):
    b, t, d = x.shape
    n = w.shape[1]
    tn = NA_WIDTH
    tm = min(t, OUT_TILE)
    hb = tm // SUBLANES
    n_hblocks = t // SUBLANES
    kern = functools.partial(_inproj_conv_kernel, n_tiles=n // tn, n_plain=n_plain_cols // tn, tn=tn, tm=tm,
                             sub=min(tm, OUT_SUB_TILE))
    const = lambda bi, i: (0, 0)
    return pl.pallas_call(
        kern,
        out_shape=jax.ShapeDtypeStruct((b, t, n), BF16),
        grid=(b, t // tm),
        in_specs=[
            pl.BlockSpec((1, tm, d), lambda bi, i: (bi, i, 0)),
            pl.BlockSpec((1, SUBLANES, d), lambda bi, i: (bi, jnp.maximum(i * hb - 1, 0), 0)),
            pl.BlockSpec((1, SUBLANES, d), lambda bi, i: (bi, jnp.minimum((i + 1) * hb, n_hblocks - 1), 0)),
            pl.BlockSpec((1, d), const),
            pl.BlockSpec((1, 1, d), lambda bi, i: (bi, 0, 0)),
            pl.BlockSpec((1, 1, d), lambda bi, i: (bi, 0, 0)),
            pl.BlockSpec((d, n), const),
            pl.BlockSpec(conv_w.shape, const),
            pl.BlockSpec((1, conv_w.shape[1]), const),
        ],
        out_specs=pl.BlockSpec((1, tm, n), lambda bi, i: (bi, i, 0)),
        compiler_params=_params(("parallel", "parallel")),
        name="inproj_conv",
    )(x, x, x, g, shift, scale, w, conv_w, conv_b.reshape(1, -1))


def _na_tables(rows):
    kh = min(WIN_H, rows)
    r = np.arange(rows)
    r0 = np.clip(r - kh // 2, 0, rows - kh)
    dr = r0[:, None] + np.arange(kh)[None] - r[:, None] + WIN_H - 1
    patterns, row_type = np.unique(dr, axis=0, return_inverse=True)
    return kh, r0.astype(np.int32), row_type.reshape(-1).astype(np.int32), patterns


def _na_bias_table(rpb, patterns):
    qc = np.arange(GRID_W)
    kc = np.arange(GRID_W)
    c0 = np.clip(qc - WIN_W // 2, 0, GRID_W - WIN_W)[:, None]
    valid = (kc[None] >= c0) & (kc[None] < c0 + WIN_W)
    dc = np.clip(kc[None] - qc[:, None] + WIN_W - 1, 0, 2 * WIN_W - 2)
    n_pat, kh = patterns.shape
    onehot_dc = jnp.asarray(dc[None] == np.arange(2 * WIN_W - 1)[:, None, None], F32)
    tab = jnp.einsum('hpic,cqk->hpiqk', rpb.astype(F32)[:, patterns], onehot_dc,
                     precision=lax.Precision.HIGHEST)
    tab = jnp.where(valid[None, None, None], tab * LOG2_E, NEG_INF)
    tab = tab.reshape(HEAD_PAIRS, 2, n_pat, kh, GRID_W, GRID_W)
    tab = tab.transpose(2, 0, 1, 4, 3, 5)
    return tab.reshape(n_pat, HEAD_PAIRS, 2 * GRID_W, kh * GRID_W)


def _pair_attention(q, keys, values, biases):
    m = q.shape[0]
    qq = _stack_heads(q)
    scores = []
    for k, bias in zip(keys, biases):
        s = _dot_nt(qq, k)
        scores.append(s if bias is None else s + bias)
    s = jnp.concatenate(scores, axis=1)
    e = jnp.exp2(s - jnp.max(s, axis=-1, keepdims=True))
    denom = jnp.sum(e, axis=-1, keepdims=True)
    e = e.astype(BF16)
    o, start = 0.0, 0
    for v in values:
        o = o + _dot(e[:, start:start + v.shape[0]], v)
        start += v.shape[0]
    o = o * (1.0 / denom)
    lane_o = lax.broadcasted_iota(jnp.int32, (m, LANES), 1)
    return jnp.where(lane_o < NA_HEAD_DIM, o[:m], o[m:])


def _stack_heads(q):
    lane = lax.broadcasted_iota(jnp.int32, q.shape, 1)
    zero = jnp.zeros_like(q)
    return jnp.concatenate([jnp.where(lane < NA_HEAD_DIM, q, zero), jnp.where(lane >= NA_HEAD_DIM, q, zero)], axis=0)


def _na_kernel(r0_ref, type_ref, q_ref, k_ref, v_ref, kc_ref, vc_ref, bias_ref, o_ref, s_ref, p_ref, *,
               kh, rows_per_step):
    n_lat = kh * GRID_W
    tiles = [(j, p) for j in range(rows_per_step) for p in range(HEAD_PAIRS)]
    window = []
    for j in range(rows_per_step):
        r = pl.program_id(1) * rows_per_step + j
        window.append((pl.multiple_of(r0_ref[r] * GRID_W, GRID_W), type_ref[r]))

    for idx, (j, p) in enumerate(tiles):
        start, rtype = window[j]
        cols = slice(p * LANES, (p + 1) * LANES)
        qq = _stack_heads(q_ref[0, j * GRID_W:(j + 1) * GRID_W, cols])
        s_ref[idx, :, :n_lat] = _dot_nt(qq, k_ref[0, pl.ds(start, n_lat), cols]) + bias_ref[rtype, p]
        s_ref[idx, :, n_lat:] = _dot_nt(qq, kc_ref[0, :, cols])

    denoms = []
    for idx in range(len(tiles)):
        s = s_ref[idx]
        e = jnp.exp2(s - jnp.max(s, axis=-1, keepdims=True))
        denoms.append(jnp.sum(e, axis=-1, keepdims=True))
        p_ref[idx] = e.astype(BF16)

    lane = lax.broadcasted_iota(jnp.int32, (GRID_W, LANES), 1)
    for idx, (j, p) in enumerate(tiles):
        start, _ = window[j]
        cols = slice(p * LANES, (p + 1) * LANES)
        o = _dot(p_ref[idx, :, :n_lat], v_ref[0, pl.ds(start, n_lat), cols]) + _dot(p_ref[idx, :, n_lat:],
                                                                                 vc_ref[0, :, cols])
        o = o * (1.0 / denoms[idx])
        o = jnp.where(lane < NA_HEAD_DIM, o[:GRID_W], o[GRID_W:])
        o_ref[0, j * GRID_W:(j + 1) * GRID_W, cols] = o.astype(o_ref.dtype)


def _neighbourhood_attention(proj, proj_c, bias_tab, r0, row_type, kh):
    b, t, _ = proj.shape
    l = proj_c.shape[1]
    rows = t // GRID_W
    w = NA_WIDTH
    rps = int(np.gcd(rows, NA_ROWS_PER_STEP))
    q_rows = rps * GRID_W
    grid_spec = pltpu.PrefetchScalarGridSpec(
        num_scalar_prefetch=2,
        grid=(b, rows // rps),
        in_specs=[
            pl.BlockSpec((1, q_rows, w), lambda bi, r, *_: (bi, r, 0)),
            pl.BlockSpec((1, t, w), lambda bi, r, *_: (bi, 0, 1)),
            pl.BlockSpec((1, t, w), lambda bi, r, *_: (bi, 0, 2)),
            pl.BlockSpec((1, l, w), lambda bi, r, *_: (bi, 0, 1)),
            pl.BlockSpec((1, l, w), lambda bi, r, *_: (bi, 0, 2)),
            pl.BlockSpec(bias_tab.shape, lambda bi, r, *_: (0, 0, 0, 0)),
        ],
        out_specs=pl.BlockSpec((1, q_rows, w), lambda bi, r, *_: (bi, r, 0)),
        scratch_shapes=[pltpu.VMEM((rps * HEAD_PAIRS, 2 * GRID_W, kh * GRID_W + l), F32),
                        pltpu.VMEM((rps * HEAD_PAIRS, 2 * GRID_W, kh * GRID_W + l), BF16)],
    )
    return pl.pallas_call(
        functools.partial(_na_kernel, kh=kh, rows_per_step=rps),
        out_shape=jax.ShapeDtypeStruct((b, t, w), BF16),
        grid_spec=grid_spec,
        compiler_params=_params(("parallel", "arbitrary")),
        name="na_attention",
    )(jnp.asarray(r0), jnp.asarray(row_type), proj, proj, proj, proj_c, proj_c, bias_tab)


def _ctx_attn_kernel(q_ref, k_ref, v_ref, o_ref):
    for p in range(HEAD_PAIRS):
        cols = slice(p * LANES, (p + 1) * LANES)
        o = _pair_attention(q_ref[0, :, cols], [k_ref[0, :, cols]], [v_ref[0, :, cols]], [None])
        o_ref[0, :, cols] = o.astype(o_ref.dtype)


def _context_attention(proj_c):
    b, l, _ = proj_c.shape
    w = NA_WIDTH
    return pl.pallas_call(
        _ctx_attn_kernel,
        out_shape=jax.ShapeDtypeStruct((b, l, w), BF16),
        grid=(b,),
        in_specs=[pl.BlockSpec((1, l, w), lambda bi, j=j: (bi, 0, j)) for j in range(3)],
        out_specs=pl.BlockSpec((1, l, w), lambda bi: (bi, 0, 0)),
        compiler_params=_params(("parallel",)),
        name="ctx_attention",
    )(proj_c, proj_c, proj_c)


def _top4(logits):
    lane = lax.broadcasted_iota(jnp.int32, logits.shape, 1)
    cur = jnp.where(lane < N_EXPERTS, logits, -jnp.inf)
    vals, idxs = [], []
    for _ in range(TOP_K):
        m = jnp.max(cur, axis=-1, keepdims=True)
        first = jnp.min(jnp.where(cur == m, lane, ROUTER_PAD).astype(F32), axis=-1, keepdims=True)
        idx = first.astype(jnp.int32)
        vals.append(m)
        idxs.append(idx)
        cur = jnp.where(lane == idx, -jnp.inf, cur)
    exps = [jnp.exp(v - vals[0]) for v in vals]
    inv = 1.0 / functools.reduce(jnp.add, exps)
    ids = jnp.zeros(logits.shape, jnp.int32)
    gates = jnp.zeros(logits.shape, F32)
    for k in range(TOP_K):
        ids = jnp.where(lane == k, idxs[k], ids)
        gates = jnp.where(lane == k, exps[k] * inv, gates)
    return ids, gates


def _layer_tail(y, rows, x_ref, g1_ref, n2_ref, sh2_ref, sc2_ref, wr_ref, br_ref, *rest):
    xo_ref, h2_ref, ids_ref, gates_ref = rest[-4:]
    x_new = x_ref[0, rows] + g1_ref[0] * y
    xo_ref[0, rows] = x_new
    h2 = _rms_mod(x_new, n2_ref[...], sh2_ref[0], sc2_ref[0])
    h2_ref[rows] = _pack_rows(h2)
    logits = _dot(h2.astype(BF16), wr_ref[...]) + br_ref[...]
    ids, gates = _top4(logits)
    ids_ref[rows] = ids
    gates_ref[0, rows] = gates


def _sub_tiles(n_rows):
    sub = min(OUT_SUB_TILE, n_rows)
    return [slice(s, s + sub) for s in range(0, n_rows, sub)]


def _tail_specs(b, t, d, tm, n_inputs_before, tokens, shared):
    n_total, offset = tokens
    n_t = t // tm
    assert offset % tm == 0
    row = lambda bi, i: (bi, i, 0)
    flat = lambda bi, i: (offset // tm + bi * n_t + i, 0)
    per_b = lambda bi, i: (bi, 0, 0)
    const = lambda bi, i: (0, 0)
    extra, aliases = (), {}
    if shared is not None:
        extra = tuple(shared)
        aliases = {n_inputs_before + 7: 1, n_inputs_before + 8: 2}
    in_specs = [
        pl.BlockSpec((1, tm, d), row),
        pl.BlockSpec((1, 1, d), per_b),
        pl.BlockSpec((1, d), const),
        pl.BlockSpec((1, 1, d), per_b),
        pl.BlockSpec((1, 1, d), per_b),
        pl.BlockSpec((d, ROUTER_PAD), const),
        pl.BlockSpec((1, ROUTER_PAD), const),
    ] + [pl.BlockSpec(memory_space=pl.ANY)] * len(extra)
    out_specs = [
        pl.BlockSpec((1, tm, d), row),
        pl.BlockSpec((tm, d // 2), flat),
        pl.BlockSpec((tm, ROUTER_PAD), flat),
        pl.BlockSpec((1, tm, ROUTER_PAD), row),
    ]
    out_shape = [
        jax.ShapeDtypeStruct((b, t, d), F32),
        jax.ShapeDtypeStruct((n_total, d // 2), jnp.int32),
        jax.ShapeDtypeStruct((n_total, ROUTER_PAD), jnp.int32),
        jax.ShapeDtypeStruct((b, t, ROUTER_PAD), F32),
    ]
    return in_specs, out_specs, out_shape, extra, aliases


def _halo_fix(rolled, at_row, halo_row, present):
    n = rolled.shape[0]
    first = at_row < SUBLANES
    assert first or at_row >= n - SUBLANES
    slab = rolled[:SUBLANES] if first else rolled[n - SUBLANES:]
    sub = lax.broadcasted_iota(jnp.int32, slab.shape, 0)
    fill = jnp.where(present, halo_row, jnp.zeros_like(halo_row))
    slab = jnp.where(sub == at_row % SUBLANES, fill, slab)
    return jnp.concatenate([slab, rolled[SUBLANES:]] if first else [rolled[:n - SUBLANES], slab], axis=0)


def _even_out_kernel(oa_ref, bg_ref, cg_ref, xin_ref, cgp_ref, xinp_ref, cgn_ref, xinn_ref, cw_ref, cb_ref,
                     wa_ref, wb_ref, *tail_refs, tm):
    i = pl.program_id(1)
    has_prev = i > 0
    has_next = i < pl.num_programs(1) - 1
    u = cg_ref[0].astype(F32) * xin_ref[0].astype(F32)
    u_prev = (cgp_ref[0].astype(F32) * xinp_ref[0].astype(F32))[HALO - 1:HALO]
    u_next = (cgn_ref[0].astype(F32) * xinn_ref[0].astype(F32))[0:1]
    u_m1 = _halo_fix(pltpu.roll(u, 1, 0), 0, u_prev, has_prev)
    u_p1 = _halo_fix(pltpu.roll(u, tm - 1, 0), tm - 1, u_next, has_next)
    cw = cw_ref[...]
    conv = u_m1 * cw[0:1] + u * cw[1:2] + u_p1 * cw[2:3] + cb_ref[...]
    o_b = (bg_ref[0].astype(F32) * conv).astype(BF16)
    for rows in _sub_tiles(tm):
        y = _dot(oa_ref[0, rows], wa_ref[...]) + _dot(o_b[rows], wb_ref[...])
        _layer_tail(y, rows, *tail_refs)


def _even_out(o_a, proj, conv_w, conv_b, w_out, x, g1, n2, sh2, sc2, w_r, b_r, tokens, shared=None):
    b, t, d = x.shape
    w = NA_WIDTH
    tm = min(t, OUT_TILE)
    hb = tm // HALO
    n_hblocks = t // HALO
    row = lambda bi, i: (bi, i, 0)
    const = lambda bi, i: (0, 0)
    prev = lambda col: (lambda bi, i: (bi, jnp.maximum(i * hb - 1, 0), col))
    nxt = lambda col: (lambda bi, i: (bi, jnp.minimum((i + 1) * hb, n_hblocks - 1), col))
    tail_in, out_specs, out_shape, extra, aliases = _tail_specs(b, t, d, tm, 12, tokens, shared)
    in_specs = [
        pl.BlockSpec((1, tm, w), row),
        pl.BlockSpec((1, tm, w), lambda bi, i: (bi, i, 3)),
        pl.BlockSpec((1, tm, w), lambda bi, i: (bi, i, 4)),
        pl.BlockSpec((1, tm, w), lambda bi, i: (bi, i, 5)),
        pl.BlockSpec((1, HALO, w), prev(4)),
        pl.BlockSpec((1, HALO, w), prev(5)),
        pl.BlockSpec((1, HALO, w), nxt(4)),
        pl.BlockSpec((1, HALO, w), nxt(5)),
        pl.BlockSpec((SC_CONV, w), const),
        pl.BlockSpec((1, w), const),
        pl.BlockSpec((w, d), const),
        pl.BlockSpec((w, d), const),
    ] + tail_in
    return pl.pallas_call(
        functools.partial(_even_out_kernel, tm=tm),
        out_shape=out_shape,
        grid=(b, t // tm),
        in_specs=in_specs,
        out_specs=out_specs,
        input_output_aliases=aliases,
        compiler_params=_params(("parallel", "parallel")),
        name="even_out",
    )(o_a, proj, proj, proj, proj, proj, proj, proj, conv_w, conv_b.reshape(1, w),
      w_out[:w], w_out[w:], x, g1, n2, sh2, sc2, w_r, b_r, *extra)


def _log_sigmoid(x):
    return jnp.minimum(x, 0.0) - jnp.log1p(jnp.exp(-jnp.abs(x)))


def _sigmoid(x):
    return 0.5 + 0.5 * jnp.tanh(0.5 * x)


def _lru_sub_tile(ucb_time, state, perm_ref, w_ref, ba_ref, bx_ref, lam_ref, want_hidden, reverse):
    sub, width = ucb_time.shape
    blk = width // LRU_BLOCKS
    steps = sub // SUBLANES
    uc = _dot(perm_ref[0], ucb_time)
    ucb = uc.astype(BF16)
    za, zx = [], []
    for h in range(LRU_BLOCKS):
        z = _dot(ucb[:, h * blk:(h + 1) * blk], w_ref[h])
        za.append(z[:, :blk])
        zx.append(z[:, blk:])
    r = _sigmoid(jnp.concatenate(za, axis=1) + ba_ref[...])
    gate_i = _sigmoid(jnp.concatenate(zx, axis=1) + bx_ref[...])
    log_a = (LRU_C * _log_sigmoid(lam_ref[...])) * r
    a = jnp.exp(log_a)
    th = jnp.tanh(log_a)
    num = -2.0 * th
    mult = jnp.where(num > 0.0, num * lax.rsqrt(num * (1.0 - th)), 0.0)
    bcoef = mult * gate_i * uc

    grp = lambda v, j: v[j * SUBLANES:(j + 1) * SUBLANES]
    prods, local = [grp(a, 0)], [grp(bcoef, 0)]
    for j in range(1, steps):
        aj = grp(a, j)
        local.append(aj * local[-1] + grp(bcoef, j))
        prods.append(aj * prods[-1])

    carry_in = [None] * SUBLANES
    for s in (range(SUBLANES - 1, -1, -1) if reverse else range(SUBLANES)):
        carry_in[s] = state
        state = prods[-1][s:s + 1] * state + local[-1][s:s + 1]
    if not want_hidden:
        return state, None
    start = jnp.concatenate(carry_in, axis=0)
    hidden = jnp.concatenate([prods[j] * start + local[j] for j in range(steps)], axis=0)
    return state, _dot(perm_ref[1], hidden.astype(BF16)).astype(BF16)


def _lru_tile(u_ref, perm_ref, w_ref, ba_ref, bx_ref, lam_ref, carry_ref, o_ref, *, reverse):
    sub = perm_ref.shape[1]
    n_sub = u_ref.shape[1] // sub
    state = carry_ref[0:1, :]
    for k in (range(n_sub - 1, -1, -1) if reverse else range(n_sub)):
        rows = slice(k * sub, (k + 1) * sub)
        state, hidden = _lru_sub_tile(u_ref[0, rows], state, perm_ref, w_ref, ba_ref, bx_ref, lam_ref,
                                      o_ref is not None, reverse)
        if o_ref is not None:
            o_ref[0, rows] = hidden
    carry_ref[...] = jnp.broadcast_to(state, carry_ref.shape)


def _lru_kernel(uc_ref, ul_ref, perm_ref, w_ref, ba_ref, bx_ref, lam_ref, o_ref, carry_ref, *, n_ctx_tiles, reverse):
    j = pl.program_id(1)
    shared = (perm_ref, w_ref, ba_ref, bx_ref, lam_ref, carry_ref)

    @pl.when(j == 0)
    def _():
        carry_ref[...] = jnp.zeros_like(carry_ref)

    @pl.when(j < n_ctx_tiles)
    def _():
        _lru_tile(uc_ref, *shared, None, reverse=reverse)

    @pl.when(j >= n_ctx_tiles)
    def _():
        _lru_tile(ul_ref, *shared, o_ref, reverse=reverse)


def _lru_scan(proj, u_ctx, w_cat, ba, bx, lam, reverse):
    b, t, _ = proj.shape
    l, width = u_ctx.shape[1], u_ctx.shape[2]
    sub = min(256, l, t)
    tc = sub * min(LRU_SUB_TILES, l // sub)
    tl = sub * min(LRU_SUB_TILES, t // sub)
    n_c, n_l = l // tc, t // tl

    def pos_of(step, n):
        step = jnp.clip(step, 0, n - 1)
        return (n - 1 - step) if reverse else step

    def tile_map(off, n, col):
        return lambda bi, j: (bi, pos_of(j - off, n), col)

    const2 = lambda bi, j: (0, 0)
    in_specs = [
        pl.BlockSpec((1, tc, width), tile_map(0, n_c, 0)),
        pl.BlockSpec((1, tl, width), tile_map(n_c, n_l, 1)),
        pl.BlockSpec((2, sub, sub), lambda bi, j: (0, 0, 0)),
        pl.BlockSpec(w_cat.shape, lambda bi, j: (0, 0, 0)),
        pl.BlockSpec((1, width), const2),
        pl.BlockSpec((1, width), const2),
        pl.BlockSpec((1, width), const2),
    ]
    steps = sub // SUBLANES
    step, block = np.divmod(np.arange(sub), SUBLANES)
    time_of_row = block * steps + (steps - 1 - step if reverse else step)
    perm = (time_of_row[:, None] == np.arange(sub)[None]).astype(np.float32)
    perms = jnp.asarray(np.stack([perm, perm.T]), BF16)
    kern = functools.partial(_lru_kernel, n_ctx_tiles=n_c, reverse=reverse)
    return pl.pallas_call(
        kern,
        out_shape=jax.ShapeDtypeStruct((b, t, width), BF16),
        grid=(b, n_c + n_l),
        in_specs=in_specs,
        out_specs=pl.BlockSpec((1, tl, width), tile_map(n_c, n_l, 0)),
        scratch_shapes=[pltpu.VMEM((SUBLANES, width), F32)],
        compiler_params=_params(("parallel", "arbitrary")),
        name="lru_scan_bwd" if reverse else "lru_scan_fwd",
    )(u_ctx, proj, perms, w_cat, ba.reshape(1, width), bx.reshape(1, width), lam.reshape(1, width))


def _odd_out_kernel(hf_ref, hb_ref, gate_ref, w_ref, *tail_refs):
    for rows in _sub_tiles(hf_ref.shape[1]):
        hsum = hf_ref[0, rows].astype(F32) + hb_ref[0, rows].astype(F32)
        z = hsum * jax.nn.gelu(gate_ref[0, rows].astype(F32), approximate=True)
        y = _dot(z.astype(BF16), w_ref[...])
        _layer_tail(y, rows, *tail_refs)


def _odd_out(h_f, h_b, proj, w_out, x, g1, n2, sh2, sc2, w_r, b_r):
    b, t, d = x.shape
    width = h_f.shape[-1]
    tm = min(t, OUT_TILE)
    row = lambda bi, i: (bi, i, 0)
    tail_in, out_specs, out_shape, _, _ = _tail_specs(b, t, d, tm, 4, (b * t, 0), None)
    in_specs = [
        pl.BlockSpec((1, tm, width), row),
        pl.BlockSpec((1, tm, width), row),
        pl.BlockSpec((1, tm, width), row),
        pl.BlockSpec((width, d), lambda bi, i: (0, 0)),
    ] + tail_in
    return pl.pallas_call(
        _odd_out_kernel,
        out_shape=out_shape,
        grid=(b, t // tm),
        in_specs=in_specs,
        out_specs=out_specs,
        compiler_params=_params(("parallel", "parallel")),
        name="odd_out",
    )(h_f, h_b, proj, w_out, x, g1, n2, sh2, sc2, w_r, b_r)


def _expert_kernel(be_ref, next_ref, rows_ref, nb_ref, x_ref, wg_hbm, bg_ref, wu_hbm, bu_ref, wd_hbm, bd_ref,
                   o_ref, w_f32, wg_bf, wu_bf, wd_bf, h_bf, sem, *, layer, chunk):
    i = pl.program_id(0)
    expert = be_ref[i]
    used = i < nb_ref[0]

    def weight_copies(e):
        return [pltpu.make_async_copy(src.at[layer, e], w_f32.at[j], sem.at[j])
                for j, src in enumerate((wg_hbm, wu_hbm, wd_hbm))]

    @pl.when(i == 0)
    def _():
        for cp in weight_copies(expert):
            cp.start()

    @pl.when(used & ((i == 0) | (expert != be_ref[jnp.maximum(i - 1, 0)])))
    def _():
        for cp in weight_copies(expert):
            cp.wait()
        wg_bf[...] = w_f32[0].astype(BF16)
        wu_bf[...] = w_f32[1].astype(BF16)
        wd_bf[...] = w_f32[2].astype(BF16)

        @pl.when(next_ref[i] >= 0)
        def _():
            for cp in weight_copies(next_ref[i]):
                cp.start()

    @pl.when(used)
    def _():
        xp = x_ref[...]
        row = lax.broadcasted_iota(jnp.int32, xp.shape, 0)
        xp = jnp.where(row < rows_ref[i], xp, 0)
        x = jnp.concatenate(_unpack_rows(xp), axis=1).astype(BF16)
        for c in range(wg_bf.shape[1] // chunk):
            cs = slice(c * chunk, (c + 1) * chunk)
            g = jnp.minimum(_dot(x, wg_bf[:, cs]) + bg_ref[0, 0, :, cs], SWIGLU_LIMIT)
            u = jnp.clip(_dot(x, wu_bf[:, cs]) + bu_ref[0, 0, :, cs], -SWIGLU_LIMIT, SWIGLU_LIMIT)
            h_bf[:, cs] = (g * jax.nn.sigmoid(SWIGLU_ALPHA * g) * (u + 1.0)).astype(BF16)
        o_ref[...] = _pack_rows(_dot(h_bf[...], wd_bf[...]) + bd_ref[0, 0])

    @pl.when(jnp.logical_not(used))
    def _():
        o_ref[...] = jnp.zeros_like(o_ref)


def _experts(xb, block_e, next_e, block_rows, n_used, layer, wg, bg, wu, bu, wd, bd):
    n_slots = xb.shape[0]
    depth, n_e, d, d_exp = wg.shape
    assert d == d_exp
    tm = EXPERT_TILE
    n_blocks = n_slots // tm
    xmap = lambda i, be, ne, br, nb: (jnp.minimum(i, nb[0] - 1), 0)
    bmap = lambda i, be, ne, br, nb: (layer, be[i], 0, 0)
    hbm = pl.BlockSpec(memory_space=pl.ANY)
    grid_spec = pltpu.PrefetchScalarGridSpec(
        num_scalar_prefetch=4,
        grid=(n_blocks,),
        in_specs=[
            pl.BlockSpec((tm, d // 2), xmap),
            hbm,
            pl.BlockSpec((1, 1, 1, d_exp), bmap),
            hbm,
            pl.BlockSpec((1, 1, 1, d_exp), bmap),
            hbm,
            pl.BlockSpec((1, 1, 1, d), bmap),
        ],
        out_specs=pl.BlockSpec((tm, d // 2), lambda i, be, ne, br, nb: (i, 0)),
        scratch_shapes=[pltpu.VMEM((3, d, d_exp), F32), pltpu.VMEM((d, d_exp), BF16), pltpu.VMEM((d, d_exp), BF16),
                        pltpu.VMEM((d_exp, d), BF16), pltpu.VMEM((tm, d_exp), BF16), pltpu.SemaphoreType.DMA((3,))],
    )
    return pl.pallas_call(
        functools.partial(_expert_kernel, layer=layer, chunk=256),
        out_shape=jax.ShapeDtypeStruct((n_slots, d // 2), jnp.int32),
        grid_spec=grid_spec,
        compiler_params=_params(("arbitrary",)),
        name="experts",
    )(block_e, next_e, block_rows, n_used, xb, wg, bg.reshape(depth, n_e, 1, d_exp), wu,
      bu.reshape(depth, n_e, 1, d_exp), wd, bd.reshape(depth, n_e, 1, d))


def _combine_kernel(x_ref, g2_ref, gates_ref, y_ref, o_ref):
    gates = gates_ref[0]
    half = y_ref.shape[-1]
    acc_lo = jnp.zeros((x_ref.shape[1], half), F32)
    acc_hi = jnp.zeros((x_ref.shape[1], half), F32)
    for k in range(TOP_K):
        lo, hi = _unpack_rows(y_ref[k])
        acc_lo = acc_lo + gates[:, k:k + 1] * lo
        acc_hi = acc_hi + gates[:, k:k + 1] * hi
    o_ref[0] = x_ref[0] + g2_ref[0] * jnp.concatenate([acc_lo, acc_hi], axis=1)


def _combine(x, g2, gates, y_sel, tok_offset):
    b, t, d = x.shape
    tm = int(np.gcd(min(t, 512), tok_offset)) if tok_offset else min(t, 512)
    n_t = t // tm
    off = tok_offset // tm
    row = lambda bi, i: (bi, i, 0)
    return pl.pallas_call(
        _combine_kernel,
        out_shape=jax.ShapeDtypeStruct((b, t, d), F32),
        grid=(b, n_t),
        in_specs=[
            pl.BlockSpec((1, tm, d), row),
            pl.BlockSpec((1, 1, d), lambda bi, i: (bi, 0, 0)),
            pl.BlockSpec((1, tm, ROUTER_PAD), row),
            pl.BlockSpec((TOP_K, tm, d // 2), lambda bi, i: (0, off + bi * n_t + i, 0)),
        ],
        out_specs=pl.BlockSpec((1, tm, d), row),
        input_output_aliases={0: 0},
        compiler_params=_params(("parallel", "parallel")),
        name="moe_combine",
    )(x, g2, gates, y_sel)


def _route_kernel(ids_ref, upper_ref, dest_ref, counts_ref, run_ref, *, tile):
    p = pl.program_id(0)
    i = pl.program_id(1)
    tm = ids_ref.shape[0]
    ids_t = ids_ref[...].T
    expert = lax.broadcasted_iota(jnp.int32, (N_EXPERTS, tm), 0)
    chosen = [ids_t[k:k + 1, :] == expert for k in range(TOP_K)]
    picks = functools.reduce(jnp.add, [c.astype(F32) for c in chosen])
    tile_counts = jnp.sum(picks, axis=1, keepdims=True)

    @pl.when((p == 0) & (i == 0))
    def _():
        run_ref[...] = jnp.zeros_like(run_ref)

    @pl.when(p == 0)
    def _():
        run_ref[...] += tile_counts

    @pl.when((p == 1) & (i == 0))
    def _():
        counts = run_ref[...]
        counts_ref[...] = counts.astype(jnp.int32)
        padded = jnp.floor((counts + (tile - 1)) * (1.0 / tile)) * tile
        row = lax.broadcasted_iota(jnp.int32, counts.shape, 0)
        ends = padded
        for s in (1, 2, 4, 8, 16):
            ends = ends + jnp.where(row >= s, pltpu.roll(ends, s, 0), 0.0)
        run_ref[...] = ends - padded

    @pl.when(p == 1)
    def _():
        before = _dot(picks.astype(BF16), upper_ref[...])
        slot = before + run_ref[:, 0:1]
        rows = [jnp.sum(jnp.where(c, slot, 0.0), axis=0, keepdims=True) for c in chosen]
        rows += [jnp.zeros_like(rows[0])] * (dest_ref.shape[0] - TOP_K)
        dest_ref[...] = jnp.concatenate(rows, axis=0).astype(jnp.int32)
        run_ref[...] += tile_counts


def _route(ids):
    n = ids.shape[0]
    tm = int(np.gcd(n, 1024))
    n_tiles = n // tm
    upper = jnp.asarray(np.triu(np.ones((tm, tm), np.float32), 1), BF16)
    dest, counts = pl.pallas_call(
        functools.partial(_route_kernel, tile=EXPERT_TILE),
        out_shape=[jax.ShapeDtypeStruct((SUBLANES, n), jnp.int32),
                   jax.ShapeDtypeStruct((N_EXPERTS, LANES), jnp.int32)],
        grid=(2, n_tiles),
        in_specs=[pl.BlockSpec((tm, ROUTER_PAD), lambda p, i: (i, 0)),
                  pl.BlockSpec((tm, tm), lambda p, i: (0, 0))],
        out_specs=[pl.BlockSpec((SUBLANES, tm), lambda p, i: (0, i * p)),
                   pl.BlockSpec((N_EXPERTS, LANES), lambda p, i: (0, 0))],
        scratch_shapes=[pltpu.VMEM((N_EXPERTS, LANES), F32)],
        compiler_params=_params(("arbitrary", "arbitrary")),
        name="route",
    )(ids, upper)
    return dest, counts[:, 0]


def _row_gather(table, idx):
    info = plsc.get_sparse_core_info()
    n_cores, n_workers = info.num_cores, info.num_cores * info.num_subcores
    n_rows, width = idx.shape[0], table.shape[1]
    chunk_rows = SC_GATHER_ROWS
    per_worker = n_rows // n_workers
    n_chunks = per_worker // chunk_rows
    assert per_worker * n_workers == n_rows and n_chunks * chunk_rows == per_worker and n_chunks % 2 == 0
    mesh = plsc.VectorSubcoreMesh(core_axis_name="c", subcore_axis_name="s")

    @functools.partial(
        pl.kernel, mesh=mesh,
        out_type=jax.ShapeDtypeStruct((n_rows, width), table.dtype),
        scratch_types=[pltpu.VMEM((n_chunks, chunk_rows), jnp.int32), pltpu.VMEM((2, chunk_rows, width), table.dtype),
                       pltpu.SemaphoreType.DMA((2,)), pltpu.SemaphoreType.DMA((2,))],
    )
    def gather_kernel(table_hbm, idx_hbm, out_hbm, idx_v, rows_v, gather_sem, write_sem):
        worker = lax.axis_index("s") * n_cores + lax.axis_index("c")
        pltpu.sync_copy(idx_hbm.at[worker], idx_v)

        def gather(c, slot):
            return pltpu.make_async_copy(table_hbm.at[idx_v.at[c]], rows_v.at[slot], gather_sem.at[slot])

        def write(c, slot):
            first_row = worker * per_worker + c * chunk_rows
            return pltpu.make_async_copy(rows_v.at[slot], out_hbm.at[pl.ds(first_row, chunk_rows)], write_sem.at[slot])

        gather(0, 0).start()

        @pl.loop(0, n_chunks, step=2)
        def _(c0):
            for slot in (0, 1):
                c = c0 + slot
                gather(c, slot).wait()

                @pl.when(c >= 1)
                def _():
                    write(c - 1, 1 - slot).wait()

                @pl.when(c + 1 < n_chunks)
                def _():
                    gather(c + 1, 1 - slot).start()

                write(c, slot).start()

        write(n_chunks - 1, 1).wait()

    return gather_kernel(table, idx.reshape(n_workers, n_chunks, chunk_rows))


def _row_scatter(rows, dest, n_out):
    info = plsc.get_sparse_core_info()
    n_cores, n_workers = info.num_cores, info.num_cores * info.num_subcores
    n_choices, n_rows = dest.shape
    width = rows.shape[1]
    chunk_rows = SC_GATHER_ROWS
    per_worker = n_rows // (n_workers * chunk_rows)
    assert per_worker * n_workers * chunk_rows == n_rows == rows.shape[0]
    idx = dest.reshape(n_choices, n_workers, per_worker, chunk_rows).transpose(1, 0, 2, 3)
    mesh = plsc.VectorSubcoreMesh(core_axis_name="c", subcore_axis_name="s")

    @functools.partial(
        pl.kernel, mesh=mesh,
        out_type=jax.ShapeDtypeStruct((n_out, width), rows.dtype),
        scratch_types=[pltpu.VMEM((n_choices, per_worker, chunk_rows), jnp.int32),
                       pltpu.VMEM((chunk_rows, width), rows.dtype), pltpu.SemaphoreType.DMA],
    )
    def scatter_kernel(rows_hbm, idx_hbm, out_hbm, idx_v, rows_v, sem):
        worker = lax.axis_index("s") * n_cores + lax.axis_index("c")
        pltpu.sync_copy(idx_hbm.at[worker], idx_v)

        @pl.loop(0, per_worker)
        def _(c):
            first_row = (worker * per_worker + c) * chunk_rows
            pltpu.sync_copy(rows_hbm.at[pl.ds(first_row, chunk_rows)], rows_v)
            copies = [pltpu.async_copy(rows_v, out_hbm.at[idx_v.at[k, c]], sem) for k in range(n_choices)]
            for cp in copies:
                cp.wait()

    return scatter_kernel(rows, idx)


def _moe(h2, ids, layer, w_exp):
    n = h2.shape[0]
    nk = n * TOP_K
    tm = EXPERT_TILE
    dest, counts = _route(ids)
    dest = dest[:TOP_K]
    padded = (counts + tm - 1) // tm * tm
    pends = jnp.cumsum(padded)
    n_blocks = -(-nk // tm) + N_EXPERTS
    n_slots = n_blocks * tm
    block_start = jnp.arange(n_blocks, dtype=jnp.int32) * tm
    block_e = jnp.minimum(jnp.sum((pends[None] <= block_start[:, None]).astype(jnp.int32), axis=1), N_EXPERTS - 1)
    is_e = (block_e[:, None] == jnp.arange(N_EXPERTS, dtype=jnp.int32)[None]).astype(jnp.int32)
    block_rows = jnp.clip(jnp.sum(is_e * (pends - padded + counts)[None], axis=1) - block_start, 0, tm)
    n_used = (pends[-1] // tm).astype(jnp.int32).reshape(1)
    e_ids = jnp.arange(N_EXPERTS, dtype=jnp.int32)
    later = (e_ids[None] > e_ids[:, None]) & (counts[None] > 0)
    next_nonempty = jnp.min(jnp.where(later, e_ids[None], N_EXPERTS), axis=1)
    next_nonempty = jnp.where(next_nonempty == N_EXPERTS, -1, next_nonempty)
    next_e = jnp.sum(is_e * next_nonempty[None], axis=1)
    xb = _row_scatter(h2, dest, n_slots)
    return _experts(xb, block_e, next_e, block_rows, n_used, layer, *w_exp), dest


def _gather_choices(yb, dest, lo, hi):
    return _row_gather(yb, dest[:, lo:hi].reshape(-1)).reshape(TOP_K, hi - lo, -1)


def _mod_parts(mod_l, b):
    d = mod_l.shape[-1] // 6
    lat = [mod_l[:b, k * d:(k + 1) * d].reshape(b, 1, d) for k in range(6)]
    ctx = [jnp.broadcast_to(mod_l[b, k * d:(k + 1) * d].reshape(1, 1, d), (b, 1, d)) for k in range(6)]
    return lat, ctx


def kernel(x, c, ctx, c_ctx, ada_w, ada_b, norm1_g, norm2_g, ev_w_in, ev_w_out, ev_q_gain, ev_k_gain, ev_rpb, ev_conv_w, ev_conv_b, od_w_in, od_w_out, od_conv_w, od_conv_b, od_fwd_wa, od_fwd_ba, od_fwd_wx, od_fwd_bx, od_fwd_lam, od_bwd_wa, od_bwd_ba, od_bwd_wx, od_bwd_bx, od_bwd_lam, router_w, router_b, exp_w_gate, exp_b_gate, exp_w_up, exp_b_up, exp_w_down, exp_b_down):
    b, t, d = x.shape
    l = ctx.shape[1]
    assert ada_w.shape[0] == DEPTH == 2 and t % GRID_W == 0 and t // GRID_W >= WIN_H

    n_rows_c = -(-(b + 1) // SUBLANES) * SUBLANES
    cvec = jnp.zeros((n_rows_c, d), F32).at[:b].set(c).at[b].set(c_ctx)
    mod = _ada_mod(cvec, ada_w, ada_b)

    def router(layer):
        w_r = jnp.zeros((d, ROUTER_PAD), F32).at[:, :N_EXPERTS].set(router_w[layer]).astype(BF16)
        b_r = jnp.zeros((1, ROUTER_PAD), F32).at[0, :N_EXPERTS].set(router_b[layer])
        return w_r, b_r

    w_exp = (exp_w_gate, exp_b_gate, exp_w_up, exp_b_up, exp_w_down, exp_b_down)

    (sh1, sc1, g1, sh2, sc2, g2), (csh1, csc1, cg1, csh2, csc2, cg2) = _mod_parts(mod[0], b)
    n1 = norm1_g[0].reshape(1, d)
    n2 = norm2_g[0].reshape(1, d)
    w_in = ev_w_in[0].astype(BF16)
    w_out = ev_w_out[0].astype(BF16)
    q_scale = NA_HEAD_DIM ** -0.5 * LOG2_E
    head_gain = jnp.stack([jnp.tile(ev_q_gain[0] * q_scale, NA_HEADS), jnp.tile(ev_k_gain[0], NA_HEADS)])
    head_gain = head_gain.reshape(2, 1, NA_WIDTH).astype(F32)
    proj = _inproj(x, n1, sh1, sc1, w_in, head_gain)
    proj_c = _inproj(ctx, n1, csh1, csc1, w_in, head_gain)
    kh, r0, row_type, patterns = _na_tables(t // GRID_W)
    bias_tab = _na_bias_table(ev_rpb[0], patterns)
    o_a = _neighbourhood_attention(proj, proj_c, bias_tab, r0, row_type, kh)
    oc_a = _context_attention(proj_c)
    w_r, b_r = router(0)
    n_tok = b * (l + t)
    c1, tokens, ids_all, gates_c = _even_out(oc_a, proj_c, ev_conv_w[0], ev_conv_b[0], w_out, ctx, cg1, n2, csh2,
                                             csc2, w_r, b_r, (n_tok, 0))
    x1, tokens, ids_all, gates = _even_out(o_a, proj, ev_conv_w[0], ev_conv_b[0], w_out, x, g1, n2, sh2, sc2, w_r,
                                           b_r, (n_tok, b * l), (tokens, ids_all))
    yb, dest = _moe(tokens, ids_all, 0, w_exp)
    y_sel = _gather_choices(yb, dest, 0, b * (l + t))
    hctx = _combine(c1, cg2, gates_c, y_sel, 0)
    x = _combine(x1, g2, gates, y_sel, b * l)

    (sh1, sc1, g1, sh2, sc2, g2), (csh1, csc1, _, _, _, _) = _mod_parts(mod[1], b)
    n1 = norm1_g[1].reshape(1, d)
    n2 = norm2_g[1].reshape(1, d)
    w_in = od_w_in[0].astype(BF16)
    width = w_in.shape[1] // 2
    proj = _inproj_conv(x, n1, sh1, sc1, w_in, od_conv_w[0], od_conv_b[0], width)
    u_ctx = _inproj_conv(hctx, n1, csh1, csc1, w_in[:, width:], od_conv_w[0], od_conv_b[0], 0)
    h_dir = []
    for reverse, (wa, ba, wx, bx, lam) in ((False, (od_fwd_wa, od_fwd_ba, od_fwd_wx, od_fwd_bx, od_fwd_lam)),
                                           (True, (od_bwd_wa, od_bwd_ba, od_bwd_wx, od_bwd_bx, od_bwd_lam))):
        w_cat = jnp.concatenate([wa[0], wx[0]], axis=-1).astype(BF16)
        h_dir.append(_lru_scan(proj, u_ctx, w_cat, ba[0], bx[0], lam[0], reverse))
    w_r, b_r = router(1)
    x1, h2, ids, gates = _odd_out(h_dir[0], h_dir[1], proj, od_w_out[0].astype(BF16), x, g1, n2, sh2, sc2, w_r, b_r)
    yb, dest = _moe(h2, ids, 1, w_exp)
    return _combine(x1, g2, gates, _gather_choices(yb, dest, 0, b * t), 0)
```

```python
import functools

import numpy as np
import jax
import jax.numpy as jnp
from jax import lax
from jax.experimental import pallas as pl
from jax.experimental.pallas import tpu as pltpu
from jax.experimental.pallas import tpu_sc as plsc

DEPTH = 2
GRID_W = 64
EPS = 1e-6
NEG_INF = -1e30
LOG2_E = 1.4426950408889634
NA_HEADS = 8
NA_HEAD_DIM = 64
NA_WIDTH = NA_HEADS * NA_HEAD_DIM
HEAD_PAIRS = NA_HEADS // 2
WIN_H = 8
WIN_W = 16
SC_CONV = 3
LRU_BLOCKS = 4
LRU_CONV = 4
LRU_C = 8.0
N_EXPERTS = 32
TOP_K = 4
SWIGLU_LIMIT = 7.0
SWIGLU_ALPHA = 1.702

LANES = 128
SUBLANES = 8
HALO = 16
ROUTER_PAD = LANES
NA_ROWS_PER_STEP = 4
LRU_SUB_TILES = 4
OUT_TILE = 1024
OUT_SUB_TILE = 512
EXPERT_TILE = 512
SC_GATHER_ROWS = 64
VMEM_LIMIT = 56 * 1024 * 1024

F32 = jnp.float32
BF16 = jnp.bfloat16


def _params(sem, vmem=VMEM_LIMIT):
    return pltpu.CompilerParams(dimension_semantics=sem, vmem_limit_bytes=vmem)


def _dot(a, b):
    return jnp.dot(a, b, preferred_element_type=F32)


def _dot_nt(a, b):
    return lax.dot_general(a, b, (((1,), (1,)), ((), ())), preferred_element_type=F32)


def _pack_rows(v):
    w = v.shape[-1] // 2
    lo = lax.bitcast_convert_type(v[:, :w].astype(BF16).astype(F32), jnp.int32)
    hi = lax.bitcast_convert_type(v[:, w:].astype(BF16).astype(F32), jnp.int32)
    return lax.shift_right_logical(lo, 16) | (hi & jnp.int32(-65536))


def _unpack_rows(p):
    lo = lax.bitcast_convert_type(lax.shift_left(p, 16), F32)
    hi = lax.bitcast_convert_type(p & jnp.int32(-65536), F32)
    return lo, hi


def _rms_mod(x, g, shift, scale):
    ms = jnp.mean(x * x, axis=-1, keepdims=True)
    y = x * lax.rsqrt(ms + EPS) * g
    return y * (1.0 + scale) + shift


def _ada_kernel(c_ref, w_ref, b_ref, o_ref):
    c = c_ref[...]
    s = (c * jax.nn.sigmoid(c)).astype(BF16)
    o_ref[0] = _dot(s, w_ref[0].astype(BF16)) + b_ref[0]


def _ada_mod(cvec, ada_w, ada_b):
    depth, d, n = ada_w.shape
    r = cvec.shape[0]
    tn = 1536
    return pl.pallas_call(
        _ada_kernel,
        out_shape=jax.ShapeDtypeStruct((depth, r, n), F32),
        grid=(depth, n // tn),
        in_specs=[
            pl.BlockSpec((r, d), lambda l, j: (0, 0)),
            pl.BlockSpec((1, d, tn), lambda l, j: (l, 0, j)),
            pl.BlockSpec((1, 1, tn), lambda l, j: (l, 0, j)),
        ],
        out_specs=pl.BlockSpec((1, r, tn), lambda l, j: (l, 0, j)),
        compiler_params=_params(("parallel", "parallel")),
        name="ada_mod",
    )(cvec, ada_w, ada_b.reshape(depth, 1, n))


def _inproj_kernel(x_ref, g_ref, sh_ref, sc_ref, w_ref, hg_ref, ones_ref, o_ref, *, n_tiles, tn, n_headnorm):
    h = _rms_mod(x_ref[0], g_ref[...], sh_ref[0], sc_ref[0]).astype(BF16)
    for j in range(n_tiles):
        y = _dot(h, w_ref[:, j * tn:(j + 1) * tn])
        if j < n_headnorm:
            ms = _dot((y * y).astype(BF16), ones_ref[...]) * (1.0 / NA_HEAD_DIM)
            y = y * lax.rsqrt(ms + EPS) * hg_ref[j]
        o_ref[0, :, j * tn:(j + 1) * tn] = y.astype(o_ref.dtype)


def _inproj(x, g, shift, scale, w, head_gain=None):
    b, t, d = x.shape
    n = w.shape[1]
    tn = NA_WIDTH
    tm = min(t, 512)
    n_headnorm = 0 if head_gain is None else head_gain.shape[0]
    if head_gain is None:
        head_gain = jnp.ones((1, 1, tn), F32)
    hid = np.arange(tn) // NA_HEAD_DIM
    ones_bd = jnp.asarray((hid[:, None] == hid[None, :]), BF16)
    kern = functools.partial(_inproj_kernel, n_tiles=n // tn, tn=tn, n_headnorm=n_headnorm)
    return pl.pallas_call(
        kern,
        out_shape=jax.ShapeDtypeStruct((b, t, n), BF16),
        grid=(b, t // tm),
        in_specs=[
            pl.BlockSpec((1, tm, d), lambda bi, i: (bi, i, 0)),
            pl.BlockSpec((1, d), lambda bi, i: (0, 0)),
            pl.BlockSpec((1, 1, d), lambda bi, i: (bi, 0, 0)),
            pl.BlockSpec((1, 1, d), lambda bi, i: (bi, 0, 0)),
            pl.BlockSpec((d, n), lambda bi, i: (0, 0)),
            pl.BlockSpec(head_gain.shape, lambda bi, i: (0, 0, 0)),
            pl.BlockSpec((tn, tn), lambda bi, i: (0, 0)),
        ],
        out_specs=pl.BlockSpec((1, tm, n), lambda bi, i: (bi, i, 0)),
        compiler_params=_params(("parallel", "parallel")),
        name="inproj",
    )(x, g, shift, scale, w, head_gain, ones_bd)


def _inproj_conv_kernel(x_ref, xp_ref, xn_ref, g_ref, sh_ref, sc_ref, w_ref, cw_ref, cb_ref, o_ref, *,
                        n_tiles, n_plain, tn, tm, sub):
    i = pl.program_id(1)
    edge = SUBLANES
    x_ext = jnp.concatenate([xp_ref[0], x_ref[0], xn_ref[0]], axis=0)
    h = _rms_mod(x_ext, g_ref[...], sh_ref[0], sc_ref[0]).astype(BF16)
    first_step = i == 0
    last_step = i == pl.num_programs(1) - 1
    left = (LRU_CONV - 1) // 2
    n_sub = tm // sub
    for s in range(n_sub):
        h_s = h[s * sub:(s + 1) * sub + 2 * edge]
        rows = slice(s * sub, (s + 1) * sub)
        keep_prev = jnp.where(first_step, 0.0, 1.0) if s == 0 else 1.0
        keep_next = jnp.where(last_step, 0.0, 1.0) if s == n_sub - 1 else 1.0
        for j in range(n_tiles):
            y = _dot(h_s, w_ref[:, j * tn:(j + 1) * tn])
            if j < n_plain:
                o_ref[0, rows, j * tn:(j + 1) * tn] = y[edge:edge + sub].astype(o_ref.dtype)
                continue
            u = jnp.concatenate([y[:edge] * keep_prev, y[edge:edge + sub], y[edge + sub:] * keep_next], axis=0)
            cw = cw_ref[:, (j - n_plain) * tn:(j - n_plain + 1) * tn]
            uc = cb_ref[:, (j - n_plain) * tn:(j - n_plain + 1) * tn]
            n_ext = u.shape[0]
            for k in range(LRU_CONV):
                shifted = u if k == left else pltpu.roll(u, (left - k) % n_ext, 0)
                uc = uc + shifted[edge:edge + sub] * cw[k:k + 1]
            o_ref[0, rows, j * tn:(j + 1) * tn] = uc.astype(o_ref.dtype)


def _inproj_conv(x, g, shift, scale, w, conv_w, conv_b, n_plain_cols):
    b, t, d = x.shape
    n = w.shape[1]
    tn = NA_WIDTH
    tm = min(t, OUT_TILE)
    hb = tm // SUBLANES
    n_hblocks = t // SUBLANES
    kern = functools.partial(_inproj_conv_kernel, n_tiles=n // tn, n_plain=n_plain_cols // tn, tn=tn, tm=tm,
                             sub=min(tm, OUT_SUB_TILE))
    const = lambda bi, i: (0, 0)
    return pl.pallas_call(
        kern,
        out_shape=jax.ShapeDtypeStruct((b, t, n), BF16),
        grid=(b, t // tm),
        in_specs=[
            pl.BlockSpec((1, tm, d), lambda bi, i: (bi, i, 0)),
            pl.BlockSpec((1, SUBLANES, d), lambda bi, i: (bi, jnp.maximum(i * hb - 1, 0), 0)),
            pl.BlockSpec((1, SUBLANES, d), lambda bi, i: (bi, jnp.minimum((i + 1) * hb, n_hblocks - 1), 0)),
            pl.BlockSpec((1, d), const),
            pl.BlockSpec((1, 1, d), lambda bi, i: (bi, 0, 0)),
            pl.BlockSpec((1, 1, d), lambda bi, i: (bi, 0, 0)),
            pl.BlockSpec((d, n), const),
            pl.BlockSpec(conv_w.shape, const),
            pl.BlockSpec((1, conv_w.shape[1]), const),
        ],
        out_specs=pl.BlockSpec((1, tm, n), lambda bi, i: (bi, i, 0)),
        compiler_params=_params(("parallel", "parallel")),
        name="inproj_conv",
    )(x, x, x, g, shift, scale, w, conv_w, conv_b.reshape(1, -1))


def _na_tables(rows):
    kh = min(WIN_H, rows)
    r = np.arange(rows)
    r0 = np.clip(r - kh // 2, 0, rows - kh)
    dr = r0[:, None] + np.arange(kh)[None] - r[:, None] + WIN_H - 1
    patterns, row_type = np.unique(dr, axis=0, return_inverse=True)
    return kh, r0.astype(np.int32), row_type.reshape(-1).astype(np.int32), patterns


def _na_bias_table(rpb, patterns):
    qc = np.arange(GRID_W)
    kc = np.arange(GRID_W)
    c0 = np.clip(qc - WIN_W // 2, 0, GRID_W - WIN_W)[:, None]
    valid = (kc[None] >= c0) & (kc[None] < c0 + WIN_W)
    dc = np.clip(kc[None] - qc[:, None] + WIN_W - 1, 0, 2 * WIN_W - 2)
    n_pat, kh = patterns.shape
    onehot_dc = jnp.asarray(dc[None] == np.arange(2 * WIN_W - 1)[:, None, None], F32)
    tab = jnp.einsum('hpic,cqk->hpiqk', rpb.astype(F32)[:, patterns], onehot_dc,
                     precision=lax.Precision.HIGHEST)
    tab = jnp.where(valid[None, None, None], tab * LOG2_E, NEG_INF)
    tab = tab.reshape(HEAD_PAIRS, 2, n_pat, kh, GRID_W, GRID_W)
    tab = tab.transpose(2, 0, 1, 4, 3, 5)
    return tab.reshape(n_pat, HEAD_PAIRS, 2 * GRID_W, kh * GRID_W)


def _pair_attention(q, keys, values, biases):
    m = q.shape[0]
    qq = _stack_heads(q)
    scores = []
    for k, bias in zip(keys, biases):
        s = _dot_nt(qq, k)
        scores.append(s if bias is None else s + bias)
    s = jnp.concatenate(scores, axis=1)
    e = jnp.exp2(s - jnp.max(s, axis=-1, keepdims=True))
    denom = jnp.sum(e, axis=-1, keepdims=True)
    e = e.astype(BF16)
    o, start = 0.0, 0
    for v in values:
        o = o + _dot(e[:, start:start + v.shape[0]], v)
        start += v.shape[0]
    o = o * (1.0 / denom)
    lane_o = lax.broadcasted_iota(jnp.int32, (m, LANES), 1)
    return jnp.where(lane_o < NA_HEAD_DIM, o[:m], o[m:])


def _stack_heads(q):
    lane = lax.broadcasted_iota(jnp.int32, q.shape, 1)
    zero = jnp.zeros_like(q)
    return jnp.concatenate([jnp.where(lane < NA_HEAD_DIM, q, zero), jnp.where(lane >= NA_HEAD_DIM, q, zero)], axis=0)


def _na_kernel(r0_ref, type_ref, q_ref, k_ref, v_ref, kc_ref, vc_ref, bias_ref, o_ref, s_ref, p_ref, *,
               kh, rows_per_step):
    n_lat = kh * GRID_W
    tiles = [(j, p) for j in range(rows_per_step) for p in range(HEAD_PAIRS)]
    window = []
    for j in range(rows_per_step):
        r = pl.program_id(1) * rows_per_step + j
        window.append((pl.multiple_of(r0_ref[r] * GRID_W, GRID_W), type_ref[r]))

    for idx, (j, p) in enumerate(tiles):
        start, rtype = window[j]
        cols = slice(p * LANES, (p + 1) * LANES)
        qq = _stack_heads(q_ref[0, j * GRID_W:(j + 1) * GRID_W, cols])
        s_ref[idx, :, :n_lat] = _dot_nt(qq, k_ref[0, pl.ds(start, n_lat), cols]) + bias_ref[rtype, p]
        s_ref[idx, :, n_lat:] = _dot_nt(qq, kc_ref[0, :, cols])

    denoms = []
    for idx in range(len(tiles)):
        s = s_ref[idx]
        e = jnp.exp2(s - jnp.max(s, axis=-1, keepdims=True))
        denoms.append(jnp.sum(e, axis=-1, keepdims=True))
        p_ref[idx] = e.astype(BF16)

    lane = lax.broadcasted_iota(jnp.int32, (GRID_W, LANES), 1)
    for idx, (j, p) in enumerate(tiles):
        start, _ = window[j]
        cols = slice(p * LANES, (p + 1) * LANES)
        o = _dot(p_ref[idx, :, :n_lat], v_ref[0, pl.ds(start, n_lat), cols]) + _dot(p_ref[idx, :, n_lat:],
                                                                                 vc_ref[0, :, cols])
        o = o * (1.0 / denoms[idx])
        o = jnp.where(lane < NA_HEAD_DIM, o[:GRID_W], o[GRID_W:])
        o_ref[0, j * GRID_W:(j + 1) * GRID_W, cols] = o.astype(o_ref.dtype)


def _neighbourhood_attention(proj, proj_c, bias_tab, r0, row_type, kh):
    b, t, _ = proj.shape
    l = proj_c.shape[1]
    rows = t // GRID_W
    w = NA_WIDTH
    rps = int(np.gcd(rows, NA_ROWS_PER_STEP))
    q_rows = rps * GRID_W
    grid_spec = pltpu.PrefetchScalarGridSpec(
        num_scalar_prefetch=2,
        grid=(b, rows // rps),
        in_specs=[
            pl.BlockSpec((1, q_rows, w), lambda bi, r, *_: (bi, r, 0)),
            pl.BlockSpec((1, t, w), lambda bi, r, *_: (bi, 0, 1)),
            pl.BlockSpec((1, t, w), lambda bi, r, *_: (bi, 0, 2)),
            pl.BlockSpec((1, l, w), lambda bi, r, *_: (bi, 0, 1)),
            pl.BlockSpec((1, l, w), lambda bi, r, *_: (bi, 0, 2)),
            pl.BlockSpec(bias_tab.shape, lambda bi, r, *_: (0, 0, 0, 0)),
        ],
        out_specs=pl.BlockSpec((1, q_rows, w), lambda bi, r, *_: (bi, r, 0)),
        scratch_shapes=[pltpu.VMEM((rps * HEAD_PAIRS, 2 * GRID_W, kh * GRID_W + l), F32),
                        pltpu.VMEM((rps * HEAD_PAIRS, 2 * GRID_W, kh * GRID_W + l), BF16)],
    )
    return pl.pallas_call(
        functools.partial(_na_kernel, kh=kh, rows_per_step=rps),
        out_shape=jax.ShapeDtypeStruct((b, t, w), BF16),
        grid_spec=grid_spec,
        compiler_params=_params(("parallel", "arbitrary")),
        name="na_attention",
    )(jnp.asarray(r0), jnp.asarray(row_type), proj, proj, proj, proj_c, proj_c, bias_tab)


def _ctx_attn_kernel(q_ref, k_ref, v_ref, o_ref):
    for p in range(HEAD_PAIRS):
        cols = slice(p * LANES, (p + 1) * LANES)
        o = _pair_attention(q_ref[0, :, cols], [k_ref[0, :, cols]], [v_ref[0, :, cols]], [None])
        o_ref[0, :, cols] = o.astype(o_ref.dtype)


def _context_attention(proj_c):
    b, l, _ = proj_c.shape
    w = NA_WIDTH
    return pl.pallas_call(
        _ctx_attn_kernel,
        out_shape=jax.ShapeDtypeStruct((b, l, w), BF16),
        grid=(b,),
        in_specs=[pl.BlockSpec((1, l, w), lambda bi, j=j: (bi, 0, j)) for j in range(3)],
        out_specs=pl.BlockSpec((1, l, w), lambda bi: (bi, 0, 0)),
        compiler_params=_params(("parallel",)),
        name="ctx_attention",
    )(proj_c, proj_c, proj_c)


def _top4(logits):
    lane = lax.broadcasted_iota(jnp.int32, logits.shape, 1)
    cur = jnp.where(lane < N_EXPERTS, logits, -jnp.inf)
    vals, idxs = [], []
    for _ in range(TOP_K):
        m = jnp.max(cur, axis=-1, keepdims=True)
        first = jnp.min(jnp.where(cur == m, lane, ROUTER_PAD).astype(F32), axis=-1, keepdims=True)
        idx = first.astype(jnp.int32)
        vals.append(m)
        idxs.append(idx)
        cur = jnp.where(lane == idx, -jnp.inf, cur)
    exps = [jnp.exp(v - vals[0]) for v in vals]
    inv = 1.0 / functools.reduce(jnp.add, exps)
    ids = jnp.zeros(logits.shape, jnp.int32)
    gates = jnp.zeros(logits.shape, F32)
    for k in range(TOP_K):
        ids = jnp.where(lane == k, idxs[k], ids)
        gates = jnp.where(lane == k, exps[k] * inv, gates)
    return ids, gates


def _layer_tail(y, rows, x_ref, g1_ref, n2_ref, sh2_ref, sc2_ref, wr_ref, br_ref, *rest):
    xo_ref, h2_ref, ids_ref, gates_ref = rest[-4:]
    x_new = x_ref[0, rows] + g1_ref[0] * y
    xo_ref[0, rows] = x_new
    h2 = _rms_mod(x_new, n2_ref[...], sh2_ref[0], sc2_ref[0])
    h2_ref[rows] = _pack_rows(h2)
    logits = _dot(h2.astype(BF16), wr_ref[...]) + br_ref[...]
    ids, gates = _top4(logits)
    ids_ref[rows] = ids
    gates_ref[0, rows] = gates


def _sub_tiles(n_rows):
    sub = min(OUT_SUB_TILE, n_rows)
    return [slice(s, s + sub) for s in range(0, n_rows, sub)]


def _tail_specs(b, t, d, tm, n_inputs_before, tokens, shared):
    n_total, offset = tokens
    n_t = t // tm
    assert offset % tm == 0
    row = lambda bi, i: (bi, i, 0)
    flat = lambda bi, i: (offset // tm + bi * n_t + i, 0)
    per_b = lambda bi, i: (bi, 0, 0)
    const = lambda bi, i: (0, 0)
    extra, aliases = (), {}
    if shared is not None:
        extra = tuple(shared)
        aliases = {n_inputs_before + 7: 1, n_inputs_before + 8: 2}
    in_specs = [
        pl.BlockSpec((1, tm, d), row),
        pl.BlockSpec((1, 1, d), per_b),
        pl.BlockSpec((1, d), const),
        pl.BlockSpec((1, 1, d), per_b),
        pl.BlockSpec((1, 1, d), per_b),
        pl.BlockSpec((d, ROUTER_PAD), const),
        pl.BlockSpec((1, ROUTER_PAD), const),
    ] + [pl.BlockSpec(memory_space=pl.ANY)] * len(extra)
    out_specs = [
        pl.BlockSpec((1, tm, d), row),
        pl.BlockSpec((tm, d // 2), flat),
        pl.BlockSpec((tm, ROUTER_PAD), flat),
        pl.BlockSpec((1, tm, ROUTER_PAD), row),
    ]
    out_shape = [
        jax.ShapeDtypeStruct((b, t, d), F32),
        jax.ShapeDtypeStruct((n_total, d // 2), jnp.int32),
        jax.ShapeDtypeStruct((n_total, ROUTER_PAD), jnp.int32),
        jax.ShapeDtypeStruct((b, t, ROUTER_PAD), F32),
    ]
    return in_specs, out_specs, out_shape, extra, aliases


def _halo_fix(rolled, at_row, halo_row, present):
    n = rolled.shape[0]
    first = at_row < SUBLANES
    assert first or at_row >= n - SUBLANES
    slab = rolled[:SUBLANES] if first else rolled[n - SUBLANES:]
    sub = lax.broadcasted_iota(jnp.int32, slab.shape, 0)
    fill = jnp.where(present, halo_row, jnp.zeros_like(halo_row))
    slab = jnp.where(sub == at_row % SUBLANES, fill, slab)
    return jnp.concatenate([slab, rolled[SUBLANES:]] if first else [rolled[:n - SUBLANES], slab], axis=0)


def _even_out_kernel(oa_ref, bg_ref, cg_ref, xin_ref, cgp_ref, xinp_ref, cgn_ref, xinn_ref, cw_ref, cb_ref,
                     wa_ref, wb_ref, *tail_refs, tm):
    i = pl.program_id(1)
    has_prev = i > 0
    has_next = i < pl.num_programs(1) - 1
    u = cg_ref[0].astype(F32) * xin_ref[0].astype(F32)
    u_prev = (cgp_ref[0].astype(F32) * xinp_ref[0].astype(F32))[HALO - 1:HALO]
    u_next = (cgn_ref[0].astype(F32) * xinn_ref[0].astype(F32))[0:1]
    u_m1 = _halo_fix(pltpu.roll(u, 1, 0), 0, u_prev, has_prev)
    u_p1 = _halo_fix(pltpu.roll(u, tm - 1, 0), tm - 1, u_next, has_next)
    cw = cw_ref[...]
    conv = u_m1 * cw[0:1] + u * cw[1:2] + u_p1 * cw[2:3] + cb_ref[...]
    o_b = (bg_ref[0].astype(F32) * conv).astype(BF16)
    for rows in _sub_tiles(tm):
        y = _dot(oa_ref[0, rows], wa_ref[...]) + _dot(o_b[rows], wb_ref[...])
        _layer_tail(y, rows, *tail_refs)


def _even_out(o_a, proj, conv_w, conv_b, w_out, x, g1, n2, sh2, sc2, w_r, b_r, tokens, shared=None):
    b, t, d = x.shape
    w = NA_WIDTH
    tm = min(t, OUT_TILE)
    hb = tm // HALO
    n_hblocks = t // HALO
    row = lambda bi, i: (bi, i, 0)
    const = lambda bi, i: (0, 0)
    prev = lambda col: (lambda bi, i: (bi, jnp.maximum(i * hb - 1, 0), col))
    nxt = lambda col: (lambda bi, i: (bi, jnp.minimum((i + 1) * hb, n_hblocks - 1), col))
    tail_in, out_specs, out_shape, extra, aliases = _tail_specs(b, t, d, tm, 12, tokens, shared)
    in_specs = [
        pl.BlockSpec((1, tm, w), row),
        pl.BlockSpec((1, tm, w), lambda bi, i: (bi, i, 3)),
        pl.BlockSpec((1, tm, w), lambda bi, i: (bi, i, 4)),
        pl.BlockSpec((1, tm, w), lambda bi, i: (bi, i, 5)),
        pl.BlockSpec((1, HALO, w), prev(4)),
        pl.BlockSpec((1, HALO, w), prev(5)),
        pl.BlockSpec((1, HALO, w), nxt(4)),
        pl.BlockSpec((1, HALO, w), nxt(5)),
        pl.BlockSpec((SC_CONV, w), const),
        pl.BlockSpec((1, w), const),
        pl.BlockSpec((w, d), const),
        pl.BlockSpec((w, d), const),
    ] + tail_in
    return pl.pallas_call(
        functools.partial(_even_out_kernel, tm=tm),
        out_shape=out_shape,
        grid=(b, t // tm),
        in_specs=in_specs,
        out_specs=out_specs,
        input_output_aliases=aliases,
        compiler_params=_params(("parallel", "parallel")),
        name="even_out",
    )(o_a, proj, proj, proj, proj, proj, proj, proj, conv_w, conv_b.reshape(1, w),
      w_out[:w], w_out[w:], x, g1, n2, sh2, sc2, w_r, b_r, *extra)


def _log_sigmoid(x):
    return jnp.minimum(x, 0.0) - jnp.log1p(jnp.exp(-jnp.abs(x)))


def _sigmoid(x):
    return 0.5 + 0.5 * jnp.tanh(0.5 * x)


def _lru_sub_tile(ucb_time, state, perm_ref, w_ref, ba_ref, bx_ref, lam_ref, want_hidden, reverse):
    sub, width = ucb_time.shape
    blk = width // LRU_BLOCKS
    steps = sub // SUBLANES
    uc = _dot(perm_ref[0], ucb_time)
    ucb = uc.astype(BF16)
    za, zx = [], []
    for h in range(LRU_BLOCKS):
        z = _dot(ucb[:, h * blk:(h + 1) * blk], w_ref[h])
        za.append(z[:, :blk])
        zx.append(z[:, blk:])
    r = _sigmoid(jnp.concatenate(za, axis=1) + ba_ref[...])
    gate_i = _sigmoid(jnp.concatenate(zx, axis=1) + bx_ref[...])
    log_a = (LRU_C * _log_sigmoid(lam_ref[...])) * r
    a = jnp.exp(log_a)
    th = jnp.tanh(log_a)
    num = -2.0 * th
    mult = jnp.where(num > 0.0, num * lax.rsqrt(num * (1.0 - th)), 0.0)
    bcoef = mult * gate_i * uc

    grp = lambda v, j: v[j * SUBLANES:(j + 1) * SUBLANES]
    prods, local = [grp(a, 0)], [grp(bcoef, 0)]
    for j in range(1, steps):
        aj = grp(a, j)
        local.append(aj * local[-1] + grp(bcoef, j))
        prods.append(aj * prods[-1])

    carry_in = [None] * SUBLANES
    for s in (range(SUBLANES - 1, -1, -1) if reverse else range(SUBLANES)):
        carry_in[s] = state
        state = prods[-1][s:s + 1] * state + local[-1][s:s + 1]
    if not want_hidden:
        return state, None
    start = jnp.concatenate(carry_in, axis=0)
    hidden = jnp.concatenate([prods[j] * start + local[j] for j in range(steps)], axis=0)
    return state, _dot(perm_ref[1], hidden.astype(BF16)).astype(BF16)


def _lru_tile(u_ref, perm_ref, w_ref, ba_ref, bx_ref, lam_ref, carry_ref, o_ref, *, reverse):
    sub = perm_ref.shape[1]
    n_sub = u_ref.shape[1] // sub
    state = carry_ref[0:1, :]
    for k in (range(n_sub - 1, -1, -1) if reverse else range(n_sub)):
        rows = slice(k * sub, (k + 1) * sub)
        state, hidden = _lru_sub_tile(u_ref[0, rows], state, perm_ref, w_ref, ba_ref, bx_ref, lam_ref,
                                      o_ref is not None, reverse)
        if o_ref is not None:
            o_ref[0, rows] = hidden
    carry_ref[...] = jnp.broadcast_to(state, carry_ref.shape)


def _lru_kernel(uc_ref, ul_ref, perm_ref, w_ref, ba_ref, bx_ref, lam_ref, o_ref, carry_ref, *, n_ctx_tiles, reverse):
    j = pl.program_id(1)
    shared = (perm_ref, w_ref, ba_ref, bx_ref, lam_ref, carry_ref)

    @pl.when(j == 0)
    def _():
        carry_ref[...] = jnp.zeros_like(carry_ref)

    @pl.when(j < n_ctx_tiles)
    def _():
        _lru_tile(uc_ref, *shared, None, reverse=reverse)

    @pl.when(j >= n_ctx_tiles)
    def _():
        _lru_tile(ul_ref, *shared, o_ref, reverse=reverse)


def _lru_scan(proj, u_ctx, w_cat, ba, bx, lam, reverse):
    b, t, _ = proj.shape
    l, width = u_ctx.shape[1], u_ctx.shape[2]
    sub = min(256, l, t)
    tc = sub * min(LRU_SUB_TILES, l // sub)
    tl = sub * min(LRU_SUB_TILES, t // sub)
    n_c, n_l = l // tc, t // tl

    def pos_of(step, n):
        step = jnp.clip(step, 0, n - 1)
        return (n - 1 - step) if reverse else step

    def tile_map(off, n, col):
        return lambda bi, j: (bi, pos_of(j - off, n), col)

    const2 = lambda bi, j: (0, 0)
    in_specs = [
        pl.BlockSpec((1, tc, width), tile_map(0, n_c, 0)),
        pl.BlockSpec((1, tl, width), tile_map(n_c, n_l, 1)),
        pl.BlockSpec((2, sub, sub), lambda bi, j: (0, 0, 0)),
        pl.BlockSpec(w_cat.shape, lambda bi, j: (0, 0, 0)),
        pl.BlockSpec((1, width), const2),
        pl.BlockSpec((1, width), const2),
        pl.BlockSpec((1, width), const2),
    ]
    steps = sub // SUBLANES
    step, block = np.divmod(np.arange(sub), SUBLANES)
    time_of_row = block * steps + (steps - 1 - step if reverse else step)
    perm = (time_of_row[:, None] == np.arange(sub)[None]).astype(np.float32)
    perms = jnp.asarray(np.stack([perm, perm.T]), BF16)
    kern = functools.partial(_lru_kernel, n_ctx_tiles=n_c, reverse=reverse)
    return pl.pallas_call(
        kern,
        out_shape=jax.ShapeDtypeStruct((b, t, width), BF16),
        grid=(b, n_c + n_l),
        in_specs=in_specs,
        out_specs=pl.BlockSpec((1, tl, width), tile_map(n_c, n_l, 0)),
        scratch_shapes=[pltpu.VMEM((SUBLANES, width), F32)],
        compiler_params=_params(("parallel", "arbitrary")),
        name="lru_scan_bwd" if reverse else "lru_scan_fwd",
    )(u_ctx, proj, perms, w_cat, ba.reshape(1, width), bx.reshape(1, width), lam.reshape(1, width))


def _odd_out_kernel(hf_ref, hb_ref, gate_ref, w_ref, *tail_refs):
    for rows in _sub_tiles(hf_ref.shape[1]):
        hsum = hf_ref[0, rows].astype(F32) + hb_ref[0, rows].astype(F32)
        z = hsum * jax.nn.gelu(gate_ref[0, rows].astype(F32), approximate=True)
        y = _dot(z.astype(BF16), w_ref[...])
        _layer_tail(y, rows, *tail_refs)


def _odd_out(h_f, h_b, proj, w_out, x, g1, n2, sh2, sc2, w_r, b_r):
    b, t, d = x.shape
    width = h_f.shape[-1]
    tm = min(t, OUT_TILE)
    row = lambda bi, i: (bi, i, 0)
    tail_in, out_specs, out_shape, _, _ = _tail_specs(b, t, d, tm, 4, (b * t, 0), None)
    in_specs = [
        pl.BlockSpec((1, tm, width), row),
        pl.BlockSpec((1, tm, width), row),
        pl.BlockSpec((1, tm, width), row),
        pl.BlockSpec((width, d), lambda bi, i: (0, 0)),
    ] + tail_in
    return pl.pallas_call(
        _odd_out_kernel,
        out_shape=out_shape,
        grid=(b, t // tm),
        in_specs=in_specs,
        out_specs=out_specs,
        compiler_params=_params(("parallel", "parallel")),
        name="odd_out",
    )(h_f, h_b, proj, w_out, x, g1, n2, sh2, sc2, w_r, b_r)


def _expert_kernel(be_ref, next_ref, rows_ref, nb_ref, x_ref, wg_hbm, bg_ref, wu_hbm, bu_ref, wd_hbm, bd_ref,
                   o_ref, w_f32, wg_bf, wu_bf, wd_bf, h_bf, sem, *, layer, chunk):
    i = pl.program_id(0)
    expert = be_ref[i]
    used = i < nb_ref[0]

    def weight_copies(e):
        return [pltpu.make_async_copy(src.at[layer, e], w_f32.at[j], sem.at[j])
                for j, src in enumerate((wg_hbm, wu_hbm, wd_hbm))]

    @pl.when(i == 0)
    def _():
        for cp in weight_copies(expert):
            cp.start()

    @pl.when(used & ((i == 0) | (expert != be_ref[jnp.maximum(i - 1, 0)])))
    def _():
        for cp in weight_copies(expert):
            cp.wait()
        wg_bf[...] = w_f32[0].astype(BF16)
        wu_bf[...] = w_f32[1].astype(BF16)
        wd_bf[...] = w_f32[2].astype(BF16)

        @pl.when(next_ref[i] >= 0)
        def _():
            for cp in weight_copies(next_ref[i]):
                cp.start()

    @pl.when(used)
    def _():
        xp = x_ref[...]
        row = lax.broadcasted_iota(jnp.int32, xp.shape, 0)
        xp = jnp.where(row < rows_ref[i], xp, 0)
        x = jnp.concatenate(_unpack_rows(xp), axis=1).astype(BF16)
        for c in range(wg_bf.shape[1] // chunk):
            cs = slice(c * chunk, (c + 1) * chunk)
            g = jnp.minimum(_dot(x, wg_bf[:, cs]) + bg_ref[0, 0, :, cs], SWIGLU_LIMIT)
            u = jnp.clip(_dot(x, wu_bf[:, cs]) + bu_ref[0, 0, :, cs], -SWIGLU_LIMIT, SWIGLU_LIMIT)
            h_bf[:, cs] = (g * jax.nn.sigmoid(SWIGLU_ALPHA * g) * (u + 1.0)).astype(BF16)
        o_ref[...] = _pack_rows(_dot(h_bf[...], wd_bf[...]) + bd_ref[0, 0])

    @pl.when(jnp.logical_not(used))
    def _():
        o_ref[...] = jnp.zeros_like(o_ref)


def _experts(xb, block_e, next_e, block_rows, n_used, layer, wg, bg, wu, bu, wd, bd):
    n_slots = xb.shape[0]
    depth, n_e, d, d_exp = wg.shape
    assert d == d_exp
    tm = EXPERT_TILE
    n_blocks = n_slots // tm
    xmap = lambda i, be, ne, br, nb: (jnp.minimum(i, nb[0] - 1), 0)
    bmap = lambda i, be, ne, br, nb: (layer, be[i], 0, 0)
    hbm = pl.BlockSpec(memory_space=pl.ANY)
    grid_spec = pltpu.PrefetchScalarGridSpec(
        num_scalar_prefetch=4,
        grid=(n_blocks,),
        in_specs=[
            pl.BlockSpec((tm, d // 2), xmap),
            hbm,
            pl.BlockSpec((1, 1, 1, d_exp), bmap),
            hbm,
            pl.BlockSpec((1, 1, 1, d_exp), bmap),
            hbm,
            pl.BlockSpec((1, 1, 1, d), bmap),
        ],
        out_specs=pl.BlockSpec((tm, d // 2), lambda i, be, ne, br, nb: (i, 0)),
        scratch_shapes=[pltpu.VMEM((3, d, d_exp), F32), pltpu.VMEM((d, d_exp), BF16), pltpu.VMEM((d, d_exp), BF16),
                        pltpu.VMEM((d_exp, d), BF16), pltpu.VMEM((tm, d_exp), BF16), pltpu.SemaphoreType.DMA((3,))],
    )
    return pl.pallas_call(
        functools.partial(_expert_kernel, layer=layer, chunk=256),
        out_shape=jax.ShapeDtypeStruct((n_slots, d // 2), jnp.int32),
        grid_spec=grid_spec,
        compiler_params=_params(("arbitrary",)),
        name="experts",
    )(block_e, next_e, block_rows, n_used, xb, wg, bg.reshape(depth, n_e, 1, d_exp), wu,
      bu.reshape(depth, n_e, 1, d_exp), wd, bd.reshape(depth, n_e, 1, d))


def _combine_kernel(x_ref, g2_ref, gates_ref, y_ref, o_ref):
    gates = gates_ref[0]
    half = y_ref.shape[-1]
    acc_lo = jnp.zeros((x_ref.shape[1], half), F32)
    acc_hi = jnp.zeros((x_ref.shape[1], half), F32)
    for k in range(TOP_K):
        lo, hi = _unpack_rows(y_ref[k])
        acc_lo = acc_lo + gates[:, k:k + 1] * lo
        acc_hi = acc_hi + gates[:, k:k + 1] * hi
    o_ref[0] = x_ref[0] + g2_ref[0] * jnp.concatenate([acc_lo, acc_hi], axis=1)


def _combine(x, g2, gates, y_sel, tok_offset):
    b, t, d = x.shape
    tm = int(np.gcd(min(t, 512), tok_offset)) if tok_offset else min(t, 512)
    n_t = t // tm
    off = tok_offset // tm
    row = lambda bi, i: (bi, i, 0)
    return pl.pallas_call(
        _combine_kernel,
        out_shape=jax.ShapeDtypeStruct((b, t, d), F32),
        grid=(b, n_t),
        in_specs=[
            pl.BlockSpec((1, tm, d), row),
            pl.BlockSpec((1, 1, d), lambda bi, i: (bi, 0, 0)),
            pl.BlockSpec((1, tm, ROUTER_PAD), row),
            pl.BlockSpec((TOP_K, tm, d // 2), lambda bi, i: (0, off + bi * n_t + i, 0)),
        ],
        out_specs=pl.BlockSpec((1, tm, d), row),
        input_output_aliases={0: 0},
        compiler_params=_params(("parallel", "parallel")),
        name="moe_combine",
    )(x, g2, gates, y_sel)


def _route_kernel(ids_ref, upper_ref, dest_ref, counts_ref, run_ref, *, tile):
    p = pl.program_id(0)
    i = pl.program_id(1)
    tm = ids_ref.shape[0]
    ids_t = ids_ref[...].T
    expert = lax.broadcasted_iota(jnp.int32, (N_EXPERTS, tm), 0)
    chosen = [ids_t[k:k + 1, :] == expert for k in range(TOP_K)]
    picks = functools.reduce(jnp.add, [c.astype(F32) for c in chosen])
    tile_counts = jnp.sum(picks, axis=1, keepdims=True)

    @pl.when((p == 0) & (i == 0))
    def _():
        run_ref[...] = jnp.zeros_like(run_ref)

    @pl.when(p == 0)
    def _():
        run_ref[...] += tile_counts

    @pl.when((p == 1) & (i == 0))
    def _():
        counts = run_ref[...]
        counts_ref[...] = counts.astype(jnp.int32)
        padded = jnp.floor((counts + (tile - 1)) * (1.0 / tile)) * tile
        row = lax.broadcasted_iota(jnp.int32, counts.shape, 0)
        ends = padded
        for s in (1, 2, 4, 8, 16):
            ends = ends + jnp.where(row >= s, pltpu.roll(ends, s, 0), 0.0)
        run_ref[...] = ends - padded

    @pl.when(p == 1)
    def _():
        before = _dot(picks.astype(BF16), upper_ref[...])
        slot = before + run_ref[:, 0:1]
        rows = [jnp.sum(jnp.where(c, slot, 0.0), axis=0, keepdims=True) for c in chosen]
        rows += [jnp.zeros_like(rows[0])] * (dest_ref.shape[0] - TOP_K)
        dest_ref[...] = jnp.concatenate(rows, axis=0).astype(jnp.int32)
        run_ref[...] += tile_counts


def _route(ids):
    n = ids.shape[0]
    tm = int(np.gcd(n, 1024))
    n_tiles = n // tm
    upper = jnp.asarray(np.triu(np.ones((tm, tm), np.float32), 1), BF16)
    dest, counts = pl.pallas_call(
        functools.partial(_route_kernel, tile=EXPERT_TILE),
        out_shape=[jax.ShapeDtypeStruct((SUBLANES, n), jnp.int32),
                   jax.ShapeDtypeStruct((N_EXPERTS, LANES), jnp.int32)],
        grid=(2, n_tiles),
        in_specs=[pl.BlockSpec((tm, ROUTER_PAD), lambda p, i: (i, 0)),
                  pl.BlockSpec((tm, tm), lambda p, i: (0, 0))],
        out_specs=[pl.BlockSpec((SUBLANES, tm), lambda p, i: (0, i * p)),
                   pl.BlockSpec((N_EXPERTS, LANES), lambda p, i: (0, 0))],
        scratch_shapes=[pltpu.VMEM((N_EXPERTS, LANES), F32)],
        compiler_params=_params(("arbitrary", "arbitrary")),
        name="route",
    )(ids, upper)
    return dest, counts[:, 0]


def _row_gather(table, idx):
    info = plsc.get_sparse_core_info()
    n_cores, n_workers = info.num_cores, info.num_cores * info.num_subcores
    n_rows, width = idx.shape[0], table.shape[1]
    chunk_rows = SC_GATHER_ROWS
    per_worker = n_rows // n_workers
    n_chunks = per_worker // chunk_rows
    assert per_worker * n_workers == n_rows and n_chunks * chunk_rows == per_worker and n_chunks % 2 == 0
    mesh = plsc.VectorSubcoreMesh(core_axis_name="c", subcore_axis_name="s")

    @functools.partial(
        pl.kernel, mesh=mesh,
        out_type=jax.ShapeDtypeStruct((n_rows, width), table.dtype),
        scratch_types=[pltpu.VMEM((n_chunks, chunk_rows), jnp.int32), pltpu.VMEM((2, chunk_rows, width), table.dtype),
                       pltpu.SemaphoreType.DMA((2,)), pltpu.SemaphoreType.DMA((2,))],
    )
    def gather_kernel(table_hbm, idx_hbm, out_hbm, idx_v, rows_v, gather_sem, write_sem):
        worker = lax.axis_index("s") * n_cores + lax.axis_index("c")
        pltpu.sync_copy(idx_hbm.at[worker], idx_v)

        def gather(c, slot):
            return pltpu.make_async_copy(table_hbm.at[idx_v.at[c]], rows_v.at[slot], gather_sem.at[slot])

        def write(c, slot):
            first_row = worker * per_worker + c * chunk_rows
            return pltpu.make_async_copy(rows_v.at[slot], out_hbm.at[pl.ds(first_row, chunk_rows)], write_sem.at[slot])

        gather(0, 0).start()

        @pl.loop(0, n_chunks, step=2)
        def _(c0):
            for slot in (0, 1):
                c = c0 + slot
                gather(c, slot).wait()

                @pl.when(c >= 1)
                def _():
                    write(c - 1, 1 - slot).wait()

                @pl.when(c + 1 < n_chunks)
                def _():
                    gather(c + 1, 1 - slot).start()

                write(c, slot).start()

        write(n_chunks - 1, 1).wait()

    return gather_kernel(table, idx.reshape(n_workers, n_chunks, chunk_rows))


def _row_scatter(rows, dest, n_out):
    info = plsc.get_sparse_core_info()
    n_cores, n_workers = info.num_cores, info.num_cores * info.num_subcores
    n_choices, n_rows = dest.shape
    width = rows.shape[1]
    chunk_rows = SC_GATHER_ROWS
    per_worker = n_rows // (n_workers * chunk_rows)
    assert per_worker * n_workers * chunk_rows == n_rows == rows.shape[0]
    idx = dest.reshape(n_choices, n_workers, per_worker, chunk_rows).transpose(1, 0, 2, 3)
    mesh = plsc.VectorSubcoreMesh(core_axis_name="c", subcore_axis_name="s")

    @functools.partial(
        pl.kernel, mesh=mesh,
        out_type=jax.ShapeDtypeStruct((n_out, width), rows.dtype),
        scratch_types=[pltpu.VMEM((n_choices, per_worker, chunk_rows), jnp.int32),
                       pltpu.VMEM((chunk_rows, width), rows.dtype), pltpu.SemaphoreType.DMA],
    )
    def scatter_kernel(rows_hbm, idx_hbm, out_hbm, idx_v, rows_v, sem):
        worker = lax.axis_index("s") * n_cores + lax.axis_index("c")
        pltpu.sync_copy(idx_hbm.at[worker], idx_v)

        @pl.loop(0, per_worker)
        def _(c):
            first_row = (worker * per_worker + c) * chunk_rows
            pltpu.sync_copy(rows_hbm.at[pl.ds(first_row, chunk_rows)], rows_v)
            copies = [pltpu.async_copy(rows_v, out_hbm.at[idx_v.at[k, c]], sem) for k in range(n_choices)]
            for cp in copies:
                cp.wait()

    return scatter_kernel(rows, idx)


def _moe(h2, ids, layer, w_exp):
    n = h2.shape[0]
    nk = n * TOP_K
    tm = EXPERT_TILE
    dest, counts = _route(ids)
    dest = dest[:TOP_K]
    padded = (counts + tm - 1) // tm * tm
    pends = jnp.cumsum(padded)
    n_blocks = -(-nk // tm) + N_EXPERTS
    n_slots = n_blocks * tm
    block_start = jnp.arange(n_blocks, dtype=jnp.int32) * tm
    block_e = jnp.minimum(jnp.sum((pends[None] <= block_start[:, None]).astype(jnp.int32), axis=1), N_EXPERTS - 1)
    is_e = (block_e[:, None] == jnp.arange(N_EXPERTS, dtype=jnp.int32)[None]).astype(jnp.int32)
    block_rows = jnp.clip(jnp.sum(is_e * (pends - padded + counts)[None], axis=1) - block_start, 0, tm)
    n_used = (pends[-1] // tm).astype(jnp.int32).reshape(1)
    e_ids = jnp.arange(N_EXPERTS, dtype=jnp.int32)
    later = (e_ids[None] > e_ids[:, None]) & (counts[None] > 0)
    next_nonempty = jnp.min(jnp.where(later, e_ids[None], N_EXPERTS), axis=1)
    next_nonempty = jnp.where(next_nonempty == N_EXPERTS, -1, next_nonempty)
    next_e = jnp.sum(is_e * next_nonempty[None], axis=1)
    xb = _row_scatter(h2, dest, n_slots)
    return _experts(xb, block_e, next_e, block_rows, n_used, layer, *w_exp), dest


def _gather_choices(yb, dest, lo, hi):
    return _row_gather(yb, dest[:, lo:hi].reshape(-1)).reshape(TOP_K, hi - lo, -1)


def _mod_parts(mod_l, b):
    d = mod_l.shape[-1] // 6
    lat = [mod_l[:b, k * d:(k + 1) * d].reshape(b, 1, d) for k in range(6)]
    ctx = [jnp.broadcast_to(mod_l[b, k * d:(k + 1) * d].reshape(1, 1, d), (b, 1, d)) for k in range(6)]
    return lat, ctx


def kernel(x, c, ctx, c_ctx, ada_w, ada_b, norm1_g, norm2_g, ev_w_in, ev_w_out, ev_q_gain, ev_k_gain, ev_rpb, ev_conv_w, ev_conv_b, od_w_in, od_w_out, od_conv_w, od_conv_b, od_fwd_wa, od_fwd_ba, od_fwd_wx, od_fwd_bx, od_fwd_lam, od_bwd_wa, od_bwd_ba, od_bwd_wx, od_bwd_bx, od_bwd_lam, router_w, router_b, exp_w_gate, exp_b_gate, exp_w_up, exp_b_up, exp_w_down, exp_b_down):
    b, t, d = x.shape
    l = ctx.shape[1]
    assert ada_w.shape[0] == DEPTH == 2 and t % GRID_W == 0 and t // GRID_W >= WIN_H

    n_rows_c = -(-(b + 1) // SUBLANES) * SUBLANES
    cvec = jnp.zeros((n_rows_c, d), F32).at[:b].set(c).at[b].set(c_ctx)
    mod = _ada_mod(cvec, ada_w, ada_b)

    def router(layer):
        w_r = jnp.zeros((d, ROUTER_PAD), F32).at[:, :N_EXPERTS].set(router_w[layer]).astype(BF16)
        b_r = jnp.zeros((1, ROUTER_PAD), F32).at[0, :N_EXPERTS].set(router_b[layer])
        return w_r, b_r

    w_exp = (exp_w_gate, exp_b_gate, exp_w_up, exp_b_up, exp_w_down, exp_b_down)

    (sh1, sc1, g1, sh2, sc2, g2), (csh1, csc1, cg1, csh2, csc2, cg2) = _mod_parts(mod[0], b)
    n1 = norm1_g[0].reshape(1, d)
    n2 = norm2_g[0].reshape(1, d)
    w_in = ev_w_in[0].astype(BF16)
    w_out = ev_w_out[0].astype(BF16)
    q_scale = NA_HEAD_DIM ** -0.5 * LOG2_E
    head_gain = jnp.stack([jnp.tile(ev_q_gain[0] * q_scale, NA_HEADS), jnp.tile(ev_k_gain[0], NA_HEADS)])
    head_gain = head_gain.reshape(2, 1, NA_WIDTH).astype(F32)
    proj = _inproj(x, n1, sh1, sc1, w_in, head_gain)
    proj_c = _inproj(ctx, n1, csh1, csc1, w_in, head_gain)
    kh, r0, row_type, patterns = _na_tables(t // GRID_W)
    bias_tab = _na_bias_table(ev_rpb[0], patterns)
    o_a = _neighbourhood_attention(proj, proj_c, bias_tab, r0, row_type, kh)
    oc_a = _context_attention(proj_c)
    w_r, b_r = router(0)
    n_tok = b * (l + t)
    fresh = (jnp.zeros((n_tok, d // 2), jnp.int32), jnp.zeros((n_tok, ROUTER_PAD), jnp.int32))
    c1, tokens, ids_all, gates_c = _even_out(oc_a, proj_c, ev_conv_w[0], ev_conv_b[0], w_out, ctx, cg1, n2, csh2,
                                             csc2, w_r, b_r, (n_tok, 0), fresh)
    x1, tokens, ids_all, gates = _even_out(o_a, proj, ev_conv_w[0], ev_conv_b[0], w_out, x, g1, n2, sh2, sc2, w_r,
                                           b_r, (n_tok, b * l), (tokens, ids_all))
    yb, dest = _moe(tokens, ids_all, 0, w_exp)
    y_sel = _gather_choices(yb, dest, 0, b * (l + t))
    hctx = _combine(c1, cg2, gates_c, y_sel, 0)
    x = _combine(x1, g2, gates, y_sel, b * l)

    (sh1, sc1, g1, sh2, sc2, g2), (csh1, csc1, _, _, _, _) = _mod_parts(mod[1], b)
    n1 = norm1_g[1].reshape(1, d)
    n2 = norm2_g[1].reshape(1, d)
    w_in = od_w_in[0].astype(BF16)
    width = w_in.shape[1] // 2
    proj = _inproj_conv(x, n1, sh1, sc1, w_in, od_conv_w[0], od_conv_b[0], width)
    u_ctx = _inproj_conv(hctx, n1, csh1, csc1, w_in[:, width:], od_conv_w[0], od_conv_b[0], 0)
    h_dir = []
    for reverse, (wa, ba, wx, bx, lam) in ((False, (od_fwd_wa, od_fwd_ba, od_fwd_wx, od_fwd_bx, od_fwd_lam)),
                                           (True, (od_bwd_wa, od_bwd_ba, od_bwd_wx, od_bwd_bx, od_bwd_lam))):
        w_cat = jnp.concatenate([wa[0], wx[0]], axis=-1).astype(BF16)
        h_dir.append(_lru_scan(proj, u_ctx, w_cat, ba[0], bx[0], lam[0], reverse))
    w_r, b_r = router(1)
    x1, h2, ids, gates = _odd_out(h_dir[0], h_dir[1], proj, od_w_out[0].astype(BF16), x, g1, n2, sh2, sc2, w_r, b_r)
    yb, dest = _moe(h2, ids, 1, w_exp)
    return _combine(x1, g2, gates, _gather_choices(yb, dest, 0, b * t), 0)
```

```python
import functools

import numpy as np
import jax
import jax.numpy as jnp
from jax import lax
from jax.experimental import pallas as pl
from jax.experimental.pallas import tpu as pltpu
from jax.experimental.pallas import tpu_sc as plsc

DEPTH = 2
GRID_W = 64
EPS = 1e-6
NEG_INF = -1e30
LOG2_E = 1.4426950408889634
NA_HEADS = 8
NA_HEAD_DIM = 64
NA_WIDTH = NA_HEADS * NA_HEAD_DIM
HEAD_PAIRS = NA_HEADS // 2
WIN_H = 8
WIN_W = 16
SC_CONV = 3
LRU_BLOCKS = 4
LRU_CONV = 4
LRU_C = 8.0
N_EXPERTS = 32
TOP_K = 4
SWIGLU_LIMIT = 7.0
SWIGLU_ALPHA = 1.702

LANES = 128
SUBLANES = 8
HALO = 16
ROUTER_PAD = LANES
NA_ROWS_PER_STEP = 4
LRU_SUB_TILES = 4
OUT_TILE = 1024
OUT_SUB_TILE = 512
EXPERT_TILE = 512
SC_GATHER_ROWS = 64
VMEM_LIMIT = 56 * 1024 * 1024

F32 = jnp.float32
BF16 = jnp.bfloat16


def _params(sem, vmem=VMEM_LIMIT):
    return pltpu.CompilerParams(dimension_semantics=sem, vmem_limit_bytes=vmem)


def _dot(a, b):
    return jnp.dot(a, b, preferred_element_type=F32)


def _dot_nt(a, b):
    return lax.dot_general(a, b, (((1,), (1,)), ((), ())), preferred_element_type=F32)


def _pack_rows(v):
    w = v.shape[-1] // 2
    lo = lax.bitcast_convert_type(v[:, :w].astype(BF16).astype(F32), jnp.int32)
    hi = lax.bitcast_convert_type(v[:, w:].astype(BF16).astype(F32), jnp.int32)
    return lax.shift_right_logical(lo, 16) | (hi & jnp.int32(-65536))


def _unpack_rows(p):
    lo = lax.bitcast_convert_type(lax.shift_left(p, 16), F32)
    hi = lax.bitcast_convert_type(p & jnp.int32(-65536), F32)
    return lo, hi


def _rms_mod(x, g, shift, scale):
    ms = jnp.mean(x * x, axis=-1, keepdims=True)
    y = x * lax.rsqrt(ms + EPS) * g
    return y * (1.0 + scale) + shift


def _ada_kernel(c_ref, w_ref, b_ref, o_ref):
    c = c_ref[...]
    s = (c * jax.nn.sigmoid(c)).astype(BF16)
    o_ref[0] = _dot(s, w_ref[0].astype(BF16)) + b_ref[0]


def _ada_mod(cvec, ada_w, ada_b):
    depth, d, n = ada_w.shape
    r = cvec.shape[0]
    tn = 1536
    return pl.pallas_call(
        _ada_kernel,
        out_shape=jax.ShapeDtypeStruct((depth, r, n), F32),
        grid=(depth, n // tn),
        in_specs=[
            pl.BlockSpec((r, d), lambda l, j: (0, 0)),
            pl.BlockSpec((1, d, tn), lambda l, j: (l, 0, j)),
            pl.BlockSpec((1, 1, tn), lambda l, j: (l, 0, j)),
        ],
        out_specs=pl.BlockSpec((1, r, tn), lambda l, j: (l, 0, j)),
        compiler_params=_params(("parallel", "parallel")),
        name="ada_mod",
    )(cvec, ada_w, ada_b.reshape(depth, 1, n))


def _inproj_kernel(x_ref, g_ref, sh_ref, sc_ref, w_ref, hg_ref, ones_ref, o_ref, *, n_tiles, tn, n_headnorm):
    h = _rms_mod(x_ref[0], g_ref[...], sh_ref[0], sc_ref[0]).astype(BF16)
    for j in range(n_tiles):
        y = _dot(h, w_ref[:, j * tn:(j + 1) * tn])
        if j < n_headnorm:
            ms = _dot((y * y).astype(BF16), ones_ref[...]) * (1.0 / NA_HEAD_DIM)
            y = y * lax.rsqrt(ms + EPS) * hg_ref[j]
        o_ref[0, :, j * tn:(j + 1) * tn] = y.astype(o_ref.dtype)


def _inproj(x, g, shift, scale, w, head_gain=None):
    b, t, d = x.shape
    n = w.shape[1]
    tn = NA_WIDTH
    tm = min(t, 512)
    n_headnorm = 0 if head_gain is None else head_gain.shape[0]
    if head_gain is None:
        head_gain = jnp.ones((1, 1, tn), F32)
    hid = np.arange(tn) // NA_HEAD_DIM
    ones_bd = jnp.asarray((hid[:, None] == hid[None, :]), BF16)
    kern = functools.partial(_inproj_kernel, n_tiles=n // tn, tn=tn, n_headnorm=n_headnorm)
    return pl.pallas_call(
        kern,
        out_shape=jax.ShapeDtypeStruct((b, t, n), BF16),
        grid=(b, t // tm),
        in_specs=[
            pl.BlockSpec((1, tm, d), lambda bi, i: (bi, i, 0)),
            pl.BlockSpec((1, d), lambda bi, i: (0, 0)),
            pl.BlockSpec((1, 1, d), lambda bi, i: (bi, 0, 0)),
            pl.BlockSpec((1, 1, d), lambda bi, i: (bi, 0, 0)),
            pl.BlockSpec((d, n), lambda bi, i: (0, 0)),
            pl.BlockSpec(head_gain.shape, lambda bi, i: (0, 0, 0)),
            pl.BlockSpec((tn, tn), lambda bi, i: (0, 0)),
        ],
        out_specs=pl.BlockSpec((1, tm, n), lambda bi, i: (bi, i, 0)),
        compiler_params=_params(("parallel", "parallel")),
        name="inproj",
    )(x, g, shift, scale, w, head_gain, ones_bd)


def _inproj_conv_kernel(x_ref, xp_ref, xn_ref, g_ref, sh_ref, sc_ref, w_ref, cw_ref, cb_ref, o_ref, *,
                        n_tiles, n_plain, tn, tm, sub):
    i = pl.program_id(1)
    edge = SUBLANES
    x_ext = jnp.concatenate([xp_ref[0], x_ref[0], xn_ref[0]], axis=0)
    h = _rms_mod(x_ext, g_ref[...], sh_ref[0], sc_ref[0]).astype(BF16)
    first_step = i == 0
    last_step = i == pl.num_programs(1) - 1
    left = (LRU_CONV - 1) // 2
    n_sub = tm // sub
    for s in range(n_sub):
        h_s = h[s * sub:(s + 1) * sub + 2 * edge]
        rows = slice(s * sub, (s + 1) * sub)
        keep_prev = jnp.where(first_step, 0.0, 1.0) if s == 0 else 1.0
        keep_next = jnp.where(last_step, 0.0, 1.0) if s == n_sub - 1 else 1.0
        for j in range(n_tiles):
            y = _dot(h_s, w_ref[:, j * tn:(j + 1) * tn])
            if j < n_plain:
                o_ref[0, rows, j * tn:(j + 1) * tn] = y[edge:edge + sub].astype(o_ref.dtype)
                continue
            u = jnp.concatenate([y[:edge] * keep_prev, y[edge:edge + sub], y[edge + sub:] * keep_next], axis=0)
            cw = cw_ref[:, (j - n_plain) * tn:(j - n_plain + 1) * tn]
            uc = cb_ref[:, (j - n_plain) * tn:(j - n_plain + 1) * tn]
            n_ext = u.shape[0]
            for k in range(LRU_CONV):
                shifted = u if k == left else pltpu.roll(u, (left - k) % n_ext, 0)
                uc = uc + shifted[edge:edge + sub] * cw[k:k + 1]
            o_ref[0, rows, j * tn:(j + 1) * tn] = uc.astype(o_ref.dtype)


def _inproj_conv(x, g, shift, scale, w, conv_w, conv_b, n_plain_cols):
    b, t, d = x.shape
    n = w.shape[1]
    tn = NA_WIDTH
    tm = min(t, OUT_TILE)
    hb = tm // SUBLANES
    n_hblocks = t // SUBLANES
    kern = functools.partial(_inproj_conv_kernel, n_tiles=n // tn, n_plain=n_plain_cols // tn, tn=tn, tm=tm,
                             sub=min(tm, OUT_SUB_TILE))
    const = lambda bi, i: (0, 0)
    return pl.pallas_call(
        kern,
        out_shape=jax.ShapeDtypeStruct((b, t, n), BF16),
        grid=(b, t // tm),
        in_specs=[
            pl.BlockSpec((1, tm, d), lambda bi, i: (bi, i, 0)),
            pl.BlockSpec((1, SUBLANES, d), lambda bi, i: (bi, jnp.maximum(i * hb - 1, 0), 0)),
            pl.BlockSpec((1, SUBLANES, d), lambda bi, i: (bi, jnp.minimum((i + 1) * hb, n_hblocks - 1), 0)),
            pl.BlockSpec((1, d), const),
            pl.BlockSpec((1, 1, d), lambda bi, i: (bi, 0, 0)),
            pl.BlockSpec((1, 1, d), lambda bi, i: (bi, 0, 0)),
            pl.BlockSpec((d, n), const),
            pl.BlockSpec(conv_w.shape, const),
            pl.BlockSpec((1, conv_w.shape[1]), const),
        ],
        out_specs=pl.BlockSpec((1, tm, n), lambda bi, i: (bi, i, 0)),
        compiler_params=_params(("parallel", "parallel")),
        name="inproj_conv",
    )(x, x, x, g, shift, scale, w, conv_w, conv_b.reshape(1, -1))


def _na_tables(rows):
    kh = min(WIN_H, rows)
    r = np.arange(rows)
    r0 = np.clip(r - kh // 2, 0, rows - kh)
    dr = r0[:, None] + np.arange(kh)[None] - r[:, None] + WIN_H - 1
    patterns, row_type = np.unique(dr, axis=0, return_inverse=True)
    return kh, r0.astype(np.int32), row_type.reshape(-1).astype(np.int32), patterns


def _na_bias_table(rpb, patterns):
    qc = np.arange(GRID_W)
    kc = np.arange(GRID_W)
    c0 = np.clip(qc - WIN_W // 2, 0, GRID_W - WIN_W)[:, None]
    valid = (kc[None] >= c0) & (kc[None] < c0 + WIN_W)
    dc = np.clip(kc[None] - qc[:, None] + WIN_W - 1, 0, 2 * WIN_W - 2)
    n_pat, kh = patterns.shape
    onehot_dc = jnp.asarray(dc[None] == np.arange(2 * WIN_W - 1)[:, None, None], F32)
    tab = jnp.einsum('hpic,cqk->hpiqk', rpb.astype(F32)[:, patterns], onehot_dc,
                     precision=lax.Precision.HIGHEST)
    tab = jnp.where(valid[None, None, None], tab * LOG2_E, NEG_INF)
    tab = tab.reshape(HEAD_PAIRS, 2, n_pat, kh, GRID_W, GRID_W)
    tab = tab.transpose(2, 0, 1, 4, 3, 5)
    return tab.reshape(n_pat, HEAD_PAIRS, 2 * GRID_W, kh * GRID_W)


def _pair_attention(q, keys, values, biases):
    m = q.shape[0]
    qq = _stack_heads(q)
    scores = []
    for k, bias in zip(keys, biases):
        s = _dot_nt(qq, k)
        scores.append(s if bias is None else s + bias)
    s = jnp.concatenate(scores, axis=1)
    e = jnp.exp2(s - jnp.max(s, axis=-1, keepdims=True))
    denom = jnp.sum(e, axis=-1, keepdims=True)
    e = e.astype(BF16)
    o, start = 0.0, 0
    for v in values:
        o = o + _dot(e[:, start:start + v.shape[0]], v)
        start += v.shape[0]
    o = o * (1.0 / denom)
    lane_o = lax.broadcasted_iota(jnp.int32, (m, LANES), 1)
    return jnp.where(lane_o < NA_HEAD_DIM, o[:m], o[m:])


def _stack_heads(q):
    lane = lax.broadcasted_iota(jnp.int32, q.shape, 1)
    zero = jnp.zeros_like(q)
    return jnp.concatenate([jnp.where(lane < NA_HEAD_DIM, q, zero), jnp.where(lane >= NA_HEAD_DIM, q, zero)], axis=0)


def _na_kernel(r0_ref, type_ref, q_ref, k_ref, v_ref, kc_ref, vc_ref, bias_ref, o_ref, s_ref, p_ref, *,
               kh, rows_per_step):
    n_lat = kh * GRID_W
    tiles = [(j, p) for j in range(rows_per_step) for p in range(HEAD_PAIRS)]
    window = []
    for j in range(rows_per_step):
        r = pl.program_id(1) * rows_per_step + j
        window.append((pl.multiple_of(r0_ref[r] * GRID_W, GRID_W), type_ref[r]))

    for idx, (j, p) in enumerate(tiles):
        start, rtype = window[j]
        cols = slice(p * LANES, (p + 1) * LANES)
        qq = _stack_heads(q_ref[0, j * GRID_W:(j + 1) * GRID_W, cols])
        s_ref[idx, :, :n_lat] = _dot_nt(qq, k_ref[0, pl.ds(start, n_lat), cols]) + bias_ref[rtype, p]
        s_ref[idx, :, n_lat:] = _dot_nt(qq, kc_ref[0, :, cols])

    denoms = []
    for idx in range(len(tiles)):
        s = s_ref[idx]
        e = jnp.exp2(s - jnp.max(s, axis=-1, keepdims=True))
        denoms.append(jnp.sum(e, axis=-1, keepdims=True))
        p_ref[idx] = e.astype(BF16)

    lane = lax.broadcasted_iota(jnp.int32, (GRID_W, LANES), 1)
    for idx, (j, p) in enumerate(tiles):
        start, _ = window[j]
        cols = slice(p * LANES, (p + 1) * LANES)
        o = _dot(p_ref[idx, :, :n_lat], v_ref[0, pl.ds(start, n_lat), cols]) + _dot(p_ref[idx, :, n_lat:],
                                                                                 vc_ref[0, :, cols])
        o = o * (1.0 / denoms[idx])
        o = jnp.where(lane < NA_HEAD_DIM, o[:GRID_W], o[GRID_W:])
        o_ref[0, j * GRID_W:(j + 1) * GRID_W, cols] = o.astype(o_ref.dtype)


def _neighbourhood_attention(proj, proj_c, bias_tab, r0, row_type, kh):
    b, t, _ = proj.shape
    l = proj_c.shape[1]
    rows = t // GRID_W
    w = NA_WIDTH
    rps = int(np.gcd(rows, NA_ROWS_PER_STEP))
    q_rows = rps * GRID_W
    grid_spec = pltpu.PrefetchScalarGridSpec(
        num_scalar_prefetch=2,
        grid=(b, rows // rps),
        in_specs=[
            pl.BlockSpec((1, q_rows, w), lambda bi, r, *_: (bi, r, 0)),
            pl.BlockSpec((1, t, w), lambda bi, r, *_: (bi, 0, 1)),
            pl.BlockSpec((1, t, w), lambda bi, r, *_: (bi, 0, 2)),
            pl.BlockSpec((1, l, w), lambda bi, r, *_: (bi, 0, 1)),
            pl.BlockSpec((1, l, w), lambda bi, r, *_: (bi, 0, 2)),
            pl.BlockSpec(bias_tab.shape, lambda bi, r, *_: (0, 0, 0, 0)),
        ],
        out_specs=pl.BlockSpec((1, q_rows, w), lambda bi, r, *_: (bi, r, 0)),
        scratch_shapes=[pltpu.VMEM((rps * HEAD_PAIRS, 2 * GRID_W, kh * GRID_W + l), F32),
                        pltpu.VMEM((rps * HEAD_PAIRS, 2 * GRID_W, kh * GRID_W + l), BF16)],
    )
    return pl.pallas_call(
        functools.partial(_na_kernel, kh=kh, rows_per_step=rps),
        out_shape=jax.ShapeDtypeStruct((b, t, w), BF16),
        grid_spec=grid_spec,
        compiler_params=_params(("parallel", "arbitrary")),
        name="na_attention",
    )(jnp.asarray(r0), jnp.asarray(row_type), proj, proj, proj, proj_c, proj_c, bias_tab)


def _ctx_attn_kernel(q_ref, k_ref, v_ref, o_ref):
    for p in range(HEAD_PAIRS):
        cols = slice(p * LANES, (p + 1) * LANES)
        o = _pair_attention(q_ref[0, :, cols], [k_ref[0, :, cols]], [v_ref[0, :, cols]], [None])
        o_ref[0, :, cols] = o.astype(o_ref.dtype)


def _context_attention(proj_c):
    b, l, _ = proj_c.shape
    w = NA_WIDTH
    return pl.pallas_call(
        _ctx_attn_kernel,
        out_shape=jax.ShapeDtypeStruct((b, l, w), BF16),
        grid=(b,),
        in_specs=[pl.BlockSpec((1, l, w), lambda bi, j=j: (bi, 0, j)) for j in range(3)],
        out_specs=pl.BlockSpec((1, l, w), lambda bi: (bi, 0, 0)),
        compiler_params=_params(("parallel",)),
        name="ctx_attention",
    )(proj_c, proj_c, proj_c)


def _top4(logits):
    lane = lax.broadcasted_iota(jnp.int32, logits.shape, 1)
    cur = jnp.where(lane < N_EXPERTS, logits, -jnp.inf)
    vals, idxs = [], []
    for _ in range(TOP_K):
        m = jnp.max(cur, axis=-1, keepdims=True)
        first = jnp.min(jnp.where(cur == m, lane, ROUTER_PAD).astype(F32), axis=-1, keepdims=True)
        idx = first.astype(jnp.int32)
        vals.append(m)
        idxs.append(idx)
        cur = jnp.where(lane == idx, -jnp.inf, cur)
    exps = [jnp.exp(v - vals[0]) for v in vals]
    inv = 1.0 / functools.reduce(jnp.add, exps)
    ids = jnp.zeros(logits.shape, jnp.int32)
    gates = jnp.zeros(logits.shape, F32)
    for k in range(TOP_K):
        ids = jnp.where(lane == k, idxs[k], ids)
        gates = jnp.where(lane == k, exps[k] * inv, gates)
    return ids, gates


def _layer_tail(y, rows, x_ref, g1_ref, n2_ref, sh2_ref, sc2_ref, wr_ref, br_ref, *rest):
    xo_ref, h2_ref, ids_ref, gates_ref = rest[-4:]
    x_new = x_ref[0, rows] + g1_ref[0] * y
    xo_ref[0, rows] = x_new
    h2 = _rms_mod(x_new, n2_ref[...], sh2_ref[0], sc2_ref[0])
    h2_ref[rows] = _pack_rows(h2)
    logits = _dot(h2.astype(BF16), wr_ref[...]) + br_ref[...]
    ids, gates = _top4(logits)
    ids_ref[rows] = ids
    gates_ref[0, rows] = gates


def _sub_tiles(n_rows):
    sub = min(OUT_SUB_TILE, n_rows)
    return [slice(s, s + sub) for s in range(0, n_rows, sub)]


def _tail_specs(b, t, d, tm, n_inputs_before, tokens, shared):
    n_total, offset = tokens
    n_t = t // tm
    assert offset % tm == 0
    row = lambda bi, i: (bi, i, 0)
    flat = lambda bi, i: (offset // tm + bi * n_t + i, 0)
    per_b = lambda bi, i: (bi, 0, 0)
    const = lambda bi, i: (0, 0)
    extra, aliases = (), {}
    if shared is not None:
        extra = tuple(shared)
        aliases = {n_inputs_before + 7: 1, n_inputs_before + 8: 2}
    in_specs = [
        pl.BlockSpec((1, tm, d), row),
        pl.BlockSpec((1, 1, d), per_b),
        pl.BlockSpec((1, d), const),
        pl.BlockSpec((1, 1, d), per_b),
        pl.BlockSpec((1, 1, d), per_b),
        pl.BlockSpec((d, ROUTER_PAD), const),
        pl.BlockSpec((1, ROUTER_PAD), const),
    ] + [pl.BlockSpec(memory_space=pl.ANY)] * len(extra)
    out_specs = [
        pl.BlockSpec((1, tm, d), row),
        pl.BlockSpec((tm, d // 2), flat),
        pl.BlockSpec((tm, ROUTER_PAD), flat),
        pl.BlockSpec((1, tm, ROUTER_PAD), row),
    ]
    out_shape = [
        jax.ShapeDtypeStruct((b, t, d), F32),
        jax.ShapeDtypeStruct((n_total, d // 2), jnp.int32),
        jax.ShapeDtypeStruct((n_total, ROUTER_PAD), jnp.int32),
        jax.ShapeDtypeStruct((b, t, ROUTER_PAD), F32),
    ]
    return in_specs, out_specs, out_shape, extra, aliases


def _halo_fix(rolled, at_row, halo_row, present):
    n = rolled.shape[0]
    first = at_row < SUBLANES
    assert first or at_row >= n - SUBLANES
    slab = rolled[:SUBLANES] if first else rolled[n - SUBLANES:]
    sub = lax.broadcasted_iota(jnp.int32, slab.shape, 0)
    fill = jnp.where(present, halo_row, jnp.zeros_like(halo_row))
    slab = jnp.where(sub == at_row % SUBLANES, fill, slab)
    return jnp.concatenate([slab, rolled[SUBLANES:]] if first else [rolled[:n - SUBLANES], slab], axis=0)


def _even_out_kernel(oa_ref, bg_ref, cg_ref, xin_ref, cgp_ref, xinp_ref, cgn_ref, xinn_ref, cw_ref, cb_ref,
                     wa_ref, wb_ref, *tail_refs, tm):
    i = pl.program_id(1)
    has_prev = i > 0
    has_next = i < pl.num_programs(1) - 1
    u = cg_ref[0].astype(F32) * xin_ref[0].astype(F32)
    u_prev = (cgp_ref[0].astype(F32) * xinp_ref[0].astype(F32))[HALO - 1:HALO]
    u_next = (cgn_ref[0].astype(F32) * xinn_ref[0].astype(F32))[0:1]
    u_m1 = _halo_fix(pltpu.roll(u, 1, 0), 0, u_prev, has_prev)
    u_p1 = _halo_fix(pltpu.roll(u, tm - 1, 0), tm - 1, u_next, has_next)
    cw = cw_ref[...]
    conv = u_m1 * cw[0:1] + u * cw[1:2] + u_p1 * cw[2:3] + cb_ref[...]
    o_b = (bg_ref[0].astype(F32) * conv).astype(BF16)
    for rows in _sub_tiles(tm):
        y = _dot(oa_ref[0, rows], wa_ref[...]) + _dot(o_b[rows], wb_ref[...])
        _layer_tail(y, rows, *tail_refs)


def _even_out(o_a, proj, conv_w, conv_b, w_out, x, g1, n2, sh2, sc2, w_r, b_r, tokens, shared=None):
    b, t, d = x.shape
    w = NA_WIDTH
    tm = min(t, OUT_TILE)
    hb = tm // HALO
    n_hblocks = t // HALO
    row = lambda bi, i: (bi, i, 0)
    const = lambda bi, i: (0, 0)
    prev = lambda col: (lambda bi, i: (bi, jnp.maximum(i * hb - 1, 0), col))
    nxt = lambda col: (lambda bi, i: (bi, jnp.minimum((i + 1) * hb, n_hblocks - 1), col))
    tail_in, out_specs, out_shape, extra, aliases = _tail_specs(b, t, d, tm, 12, tokens, shared)
    in_specs = [
        pl.BlockSpec((1, tm, w), row),
        pl.BlockSpec((1, tm, w), lambda bi, i: (bi, i, 3)),
        pl.BlockSpec((1, tm, w), lambda bi, i: (bi, i, 4)),
        pl.BlockSpec((1, tm, w), lambda bi, i: (bi, i, 5)),
        pl.BlockSpec((1, HALO, w), prev(4)),
        pl.BlockSpec((1, HALO, w), prev(5)),
        pl.BlockSpec((1, HALO, w), nxt(4)),
        pl.BlockSpec((1, HALO, w), nxt(5)),
        pl.BlockSpec((SC_CONV, w), const),
        pl.BlockSpec((1, w), const),
        pl.BlockSpec((w, d), const),
        pl.BlockSpec((w, d), const),
    ] + tail_in
    return pl.pallas_call(
        functools.partial(_even_out_kernel, tm=tm),
        out_shape=out_shape,
        grid=(b, t // tm),
        in_specs=in_specs,
        out_specs=out_specs,
        input_output_aliases=aliases,
        compiler_params=_params(("parallel", "parallel")),
        name="even_out",
    )(o_a, proj, proj, proj, proj, proj, proj, proj, conv_w, conv_b.reshape(1, w),
      w_out[:w], w_out[w:], x, g1, n2, sh2, sc2, w_r, b_r, *extra)


def _log_sigmoid(x):
    return jnp.minimum(x, 0.0) - jnp.log1p(jnp.exp(-jnp.abs(x)))


def _sigmoid(x):
    return 0.5 + 0.5 * jnp.tanh(0.5 * x)


def _lru_sub_tile(ucb_time, state, perm_ref, w_ref, ba_ref, bx_ref, lam_ref, want_hidden, reverse):
    sub, width = ucb_time.shape
    blk = width // LRU_BLOCKS
    steps = sub // SUBLANES
    uc = _dot(perm_ref[0], ucb_time)
    ucb = uc.astype(BF16)
    za, zx = [], []
    for h in range(LRU_BLOCKS):
        z = _dot(ucb[:, h * blk:(h + 1) * blk], w_ref[h])
        za.append(z[:, :blk])
        zx.append(z[:, blk:])
    r = _sigmoid(jnp.concatenate(za, axis=1) + ba_ref[...])
    gate_i = _sigmoid(jnp.concatenate(zx, axis=1) + bx_ref[...])
    log_a = (LRU_C * _log_sigmoid(lam_ref[...])) * r
    a = jnp.exp(log_a)
    th = jnp.tanh(log_a)
    num = -2.0 * th
    mult = jnp.where(num > 0.0, num * lax.rsqrt(num * (1.0 - th)), 0.0)
    bcoef = mult * gate_i * uc

    grp = lambda v, j: v[j * SUBLANES:(j + 1) * SUBLANES]
    prods, local = [grp(a, 0)], [grp(bcoef, 0)]
    for j in range(1, steps):
        aj = grp(a, j)
        local.append(aj * local[-1] + grp(bcoef, j))
        prods.append(aj * prods[-1])

    carry_in = [None] * SUBLANES
    for s in (range(SUBLANES - 1, -1, -1) if reverse else range(SUBLANES)):
        carry_in[s] = state
        state = prods[-1][s:s + 1] * state + local[-1][s:s + 1]
    if not want_hidden:
        return state, None
    start = jnp.concatenate(carry_in, axis=0)
    hidden = jnp.concatenate([prods[j] * start + local[j] for j in range(steps)], axis=0)
    return state, _dot(perm_ref[1], hidden.astype(BF16)).astype(BF16)


def _lru_tile(u_ref, perm_ref, w_ref, ba_ref, bx_ref, lam_ref, carry_ref, o_ref, *, reverse):
    sub = perm_ref.shape[1]
    n_sub = u_ref.shape[1] // sub
    state = carry_ref[0:1, :]
    for k in (range(n_sub - 1, -1, -1) if reverse else range(n_sub)):
        rows = slice(k * sub, (k + 1) * sub)
        state, hidden = _lru_sub_tile(u_ref[0, rows], state, perm_ref, w_ref, ba_ref, bx_ref, lam_ref,
                                      o_ref is not None, reverse)
        if o_ref is not None:
            o_ref[0, rows] = hidden
    carry_ref[...] = jnp.broadcast_to(state, carry_ref.shape)


def _lru_kernel(uc_ref, ul_ref, perm_ref, w_ref, ba_ref, bx_ref, lam_ref, o_ref, carry_ref, *, n_ctx_tiles, reverse):
    j = pl.program_id(1)
    shared = (perm_ref, w_ref, ba_ref, bx_ref, lam_ref, carry_ref)

    @pl.when(j == 0)
    def _():
        carry_ref[...] = jnp.zeros_like(carry_ref)

    @pl.when(j < n_ctx_tiles)
    def _():
        _lru_tile(uc_ref, *shared, None, reverse=reverse)

    @pl.when(j >= n_ctx_tiles)
    def _():
        _lru_tile(ul_ref, *shared, o_ref, reverse=reverse)


def _lru_scan(proj, u_ctx, w_cat, ba, bx, lam, reverse):
    b, t, _ = proj.shape
    l, width = u_ctx.shape[1], u_ctx.shape[2]
    sub = min(256, l, t)
    tc = sub * min(LRU_SUB_TILES, l // sub)
    tl = sub * min(LRU_SUB_TILES, t // sub)
    n_c, n_l = l // tc, t // tl

    def pos_of(step, n):
        step = jnp.clip(step, 0, n - 1)
        return (n - 1 - step) if reverse else step

    def tile_map(off, n, col):
        return lambda bi, j: (bi, pos_of(j - off, n), col)

    const2 = lambda bi, j: (0, 0)
    in_specs = [
        pl.BlockSpec((1, tc, width), tile_map(0, n_c, 0)),
        pl.BlockSpec((1, tl, width), tile_map(n_c, n_l, 1)),
        pl.BlockSpec((2, sub, sub), lambda bi, j: (0, 0, 0)),
        pl.BlockSpec(w_cat.shape, lambda bi, j: (0, 0, 0)),
        pl.BlockSpec((1, width), const2),
        pl.BlockSpec((1, width), const2),
        pl.BlockSpec((1, width), const2),
    ]
    steps = sub // SUBLANES
    step, block = np.divmod(np.arange(sub), SUBLANES)
    time_of_row = block * steps + (steps - 1 - step if reverse else step)
    perm = (time_of_row[:, None] == np.arange(sub)[None]).astype(np.float32)
    perms = jnp.asarray(np.stack([perm, perm.T]), BF16)
    kern = functools.partial(_lru_kernel, n_ctx_tiles=n_c, reverse=reverse)
    return pl.pallas_call(
        kern,
        out_shape=jax.ShapeDtypeStruct((b, t, width), BF16),
        grid=(b, n_c + n_l),
        in_specs=in_specs,
        out_specs=pl.BlockSpec((1, tl, width), tile_map(n_c, n_l, 0)),
        scratch_shapes=[pltpu.VMEM((SUBLANES, width), F32)],
        compiler_params=_params(("parallel", "arbitrary")),
        name="lru_scan_bwd" if reverse else "lru_scan_fwd",
    )(u_ctx, proj, perms, w_cat, ba.reshape(1, width), bx.reshape(1, width), lam.reshape(1, width))


def _odd_out_kernel(hf_ref, hb_ref, gate_ref, w_ref, *tail_refs):
    for rows in _sub_tiles(hf_ref.shape[1]):
        hsum = hf_ref[0, rows].astype(F32) + hb_ref[0, rows].astype(F32)
        z = hsum * jax.nn.gelu(gate_ref[0, rows].astype(F32), approximate=True)
        y = _dot(z.astype(BF16), w_ref[...])
        _layer_tail(y, rows, *tail_refs)


def _odd_out(h_f, h_b, proj, w_out, x, g1, n2, sh2, sc2, w_r, b_r):
    b, t, d = x.shape
    width = h_f.shape[-1]
    tm = min(t, OUT_TILE)
    row = lambda bi, i: (bi, i, 0)
    tail_in, out_specs, out_shape, _, _ = _tail_specs(b, t, d, tm, 4, (b * t, 0), None)
    in_specs = [
        pl.BlockSpec((1, tm, width), row),
        pl.BlockSpec((1, tm, width), row),
        pl.BlockSpec((1, tm, width), row),
        pl.BlockSpec((width, d), lambda bi, i: (0, 0)),
    ] + tail_in
    return pl.pallas_call(
        _odd_out_kernel,
        out_shape=out_shape,
        grid=(b, t // tm),
        in_specs=in_specs,
        out_specs=out_specs,
        compiler_params=_params(("parallel", "parallel")),
        name="odd_out",
    )(h_f, h_b, proj, w_out, x, g1, n2, sh2, sc2, w_r, b_r)


def _expert_kernel(be_ref, next_ref, rows_ref, nb_ref, x_ref, wg_hbm, bg_ref, wu_hbm, bu_ref, wd_hbm, bd_ref,
                   o_ref, w_f32, wg_bf, wu_bf, wd_bf, h_bf, sem, *, layer, chunk):
    i = pl.program_id(0)
    expert = be_ref[i]
    used = i < nb_ref[0]

    def weight_copies(e):
        return [pltpu.make_async_copy(src.at[layer, e], w_f32.at[j], sem.at[j])
                for j, src in enumerate((wg_hbm, wu_hbm, wd_hbm))]

    @pl.when(i == 0)
    def _():
        for cp in weight_copies(expert):
            cp.start()

    @pl.when(used & ((i == 0) | (expert != be_ref[jnp.maximum(i - 1, 0)])))
    def _():
        for cp in weight_copies(expert):
            cp.wait()
        wg_bf[...] = w_f32[0].astype(BF16)
        wu_bf[...] = w_f32[1].astype(BF16)
        wd_bf[...] = w_f32[2].astype(BF16)

        @pl.when(next_ref[i] >= 0)
        def _():
            for cp in weight_copies(next_ref[i]):
                cp.start(priority=1)

    @pl.when(used)
    def _():
        xp = x_ref[...]
        row = lax.broadcasted_iota(jnp.int32, xp.shape, 0)
        xp = jnp.where(row < rows_ref[i], xp, 0)
        x = jnp.concatenate(_unpack_rows(xp), axis=1).astype(BF16)
        for c in range(wg_bf.shape[1] // chunk):
            cs = slice(c * chunk, (c + 1) * chunk)
            g = jnp.minimum(_dot(x, wg_bf[:, cs]) + bg_ref[0, 0, :, cs], SWIGLU_LIMIT)
            u = jnp.clip(_dot(x, wu_bf[:, cs]) + bu_ref[0, 0, :, cs], -SWIGLU_LIMIT, SWIGLU_LIMIT)
            h_bf[:, cs] = (g * jax.nn.sigmoid(SWIGLU_ALPHA * g) * (u + 1.0)).astype(BF16)
        o_ref[...] = _pack_rows(_dot(h_bf[...], wd_bf[...]) + bd_ref[0, 0])

    @pl.when(jnp.logical_not(used))
    def _():
        o_ref[...] = jnp.zeros_like(o_ref)


def _experts(xb, block_e, next_e, block_rows, n_used, layer, wg, bg, wu, bu, wd, bd):
    n_slots = xb.shape[0]
    depth, n_e, d, d_exp = wg.shape
    assert d == d_exp
    tm = EXPERT_TILE
    n_blocks = n_slots // tm
    xmap = lambda i, be, ne, br, nb: (jnp.minimum(i, nb[0] - 1), 0)
    bmap = lambda i, be, ne, br, nb: (layer, be[i], 0, 0)
    hbm = pl.BlockSpec(memory_space=pl.ANY)
    grid_spec = pltpu.PrefetchScalarGridSpec(
        num_scalar_prefetch=4,
        grid=(n_blocks,),
        in_specs=[
            pl.BlockSpec((tm, d // 2), xmap),
            hbm,
            pl.BlockSpec((1, 1, 1, d_exp), bmap),
            hbm,
            pl.BlockSpec((1, 1, 1, d_exp), bmap),
            hbm,
            pl.BlockSpec((1, 1, 1, d), bmap),
        ],
        out_specs=pl.BlockSpec((tm, d // 2), lambda i, be, ne, br, nb: (i, 0)),
        scratch_shapes=[pltpu.VMEM((3, d, d_exp), F32), pltpu.VMEM((d, d_exp), BF16), pltpu.VMEM((d, d_exp), BF16),
                        pltpu.VMEM((d_exp, d), BF16), pltpu.VMEM((tm, d_exp), BF16), pltpu.SemaphoreType.DMA((3,))],
    )
    return pl.pallas_call(
        functools.partial(_expert_kernel, layer=layer, chunk=256),
        out_shape=jax.ShapeDtypeStruct((n_slots, d // 2), jnp.int32),
        grid_spec=grid_spec,
        compiler_params=_params(("arbitrary",)),
        name="experts",
    )(block_e, next_e, block_rows, n_used, xb, wg, bg.reshape(depth, n_e, 1, d_exp), wu,
      bu.reshape(depth, n_e, 1, d_exp), wd, bd.reshape(depth, n_e, 1, d))


def _combine_kernel(x_ref, g2_ref, gates_ref, y_ref, o_ref):
    gates = gates_ref[0]
    half = y_ref.shape[-1]
    acc_lo = jnp.zeros((x_ref.shape[1], half), F32)
    acc_hi = jnp.zeros((x_ref.shape[1], half), F32)
    for k in range(TOP_K):
        lo, hi = _unpack_rows(y_ref[k])
        acc_lo = acc_lo + gates[:, k:k + 1] * lo
        acc_hi = acc_hi + gates[:, k:k + 1] * hi
    o_ref[0] = x_ref[0] + g2_ref[0] * jnp.concatenate([acc_lo, acc_hi], axis=1)


def _combine(x, g2, gates, y_sel, tok_offset):
    b, t, d = x.shape
    tm = int(np.gcd(min(t, 512), tok_offset)) if tok_offset else min(t, 512)
    n_t = t // tm
    off = tok_offset // tm
    row = lambda bi, i: (bi, i, 0)
    return pl.pallas_call(
        _combine_kernel,
        out_shape=jax.ShapeDtypeStruct((b, t, d), F32),
        grid=(b, n_t),
        in_specs=[
            pl.BlockSpec((1, tm, d), row),
            pl.BlockSpec((1, 1, d), lambda bi, i: (bi, 0, 0)),
            pl.BlockSpec((1, tm, ROUTER_PAD), row),
            pl.BlockSpec((TOP_K, tm, d // 2), lambda bi, i: (0, off + bi * n_t + i, 0)),
        ],
        out_specs=pl.BlockSpec((1, tm, d), row),
        input_output_aliases={0: 0},
        compiler_params=_params(("parallel", "parallel")),
        name="moe_combine",
    )(x, g2, gates, y_sel)


def _route_kernel(ids_ref, upper_ref, dest_ref, counts_ref, run_ref, *, tile):
    p = pl.program_id(0)
    i = pl.program_id(1)
    tm = ids_ref.shape[0]
    ids_t = ids_ref[...].T
    expert = lax.broadcasted_iota(jnp.int32, (N_EXPERTS, tm), 0)
    chosen = [ids_t[k:k + 1, :] == expert for k in range(TOP_K)]
    picks = functools.reduce(jnp.add, [c.astype(F32) for c in chosen])
    tile_counts = jnp.sum(picks, axis=1, keepdims=True)

    @pl.when((p == 0) & (i == 0))
    def _():
        run_ref[...] = jnp.zeros_like(run_ref)

    @pl.when(p == 0)
    def _():
        run_ref[...] += tile_counts

    @pl.when((p == 1) & (i == 0))
    def _():
        counts = run_ref[...]
        counts_ref[...] = counts.astype(jnp.int32)
        padded = jnp.floor((counts + (tile - 1)) * (1.0 / tile)) * tile
        row = lax.broadcasted_iota(jnp.int32, counts.shape, 0)
        ends = padded
        for s in (1, 2, 4, 8, 16):
            ends = ends + jnp.where(row >= s, pltpu.roll(ends, s, 0), 0.0)
        run_ref[...] = ends - padded

    @pl.when(p == 1)
    def _():
        before = _dot(picks.astype(BF16), upper_ref[...])
        slot = before + run_ref[:, 0:1]
        rows = [jnp.sum(jnp.where(c, slot, 0.0), axis=0, keepdims=True) for c in chosen]
        rows += [jnp.zeros_like(rows[0])] * (dest_ref.shape[0] - TOP_K)
        dest_ref[...] = jnp.concatenate(rows, axis=0).astype(jnp.int32)
        run_ref[...] += tile_counts


def _route(ids):
    n = ids.shape[0]
    tm = int(np.gcd(n, 1024))
    n_tiles = n // tm
    upper = jnp.asarray(np.triu(np.ones((tm, tm), np.float32), 1), BF16)
    dest, counts = pl.pallas_call(
        functools.partial(_route_kernel, tile=EXPERT_TILE),
        out_shape=[jax.ShapeDtypeStruct((SUBLANES, n), jnp.int32),
                   jax.ShapeDtypeStruct((N_EXPERTS, LANES), jnp.int32)],
        grid=(2, n_tiles),
        in_specs=[pl.BlockSpec((tm, ROUTER_PAD), lambda p, i: (i, 0)),
                  pl.BlockSpec((tm, tm), lambda p, i: (0, 0))],
        out_specs=[pl.BlockSpec((SUBLANES, tm), lambda p, i: (0, i * p)),
                   pl.BlockSpec((N_EXPERTS, LANES), lambda p, i: (0, 0))],
        scratch_shapes=[pltpu.VMEM((N_EXPERTS, LANES), F32)],
        compiler_params=_params(("arbitrary", "arbitrary")),
        name="route",
    )(ids, upper)
    return dest, counts[:, 0]


def _row_gather(table, idx):
    info = plsc.get_sparse_core_info()
    n_cores, n_workers = info.num_cores, info.num_cores * info.num_subcores
    n_rows, width = idx.shape[0], table.shape[1]
    chunk_rows = SC_GATHER_ROWS
    per_worker = n_rows // n_workers
    n_chunks = per_worker // chunk_rows
    assert per_worker * n_workers == n_rows and n_chunks * chunk_rows == per_worker and n_chunks % 2 == 0
    mesh = plsc.VectorSubcoreMesh(core_axis_name="c", subcore_axis_name="s")

    @functools.partial(
        pl.kernel, mesh=mesh,
        out_type=jax.ShapeDtypeStruct((n_rows, width), table.dtype),
        scratch_types=[pltpu.VMEM((n_chunks, chunk_rows), jnp.int32), pltpu.VMEM((2, chunk_rows, width), table.dtype),
                       pltpu.SemaphoreType.DMA((2,)), pltpu.SemaphoreType.DMA((2,))],
    )
    def gather_kernel(table_hbm, idx_hbm, out_hbm, idx_v, rows_v, gather_sem, write_sem):
        worker = lax.axis_index("s") * n_cores + lax.axis_index("c")
        pltpu.sync_copy(idx_hbm.at[worker], idx_v)

        def gather(c, slot):
            return pltpu.make_async_copy(table_hbm.at[idx_v.at[c]], rows_v.at[slot], gather_sem.at[slot])

        def write(c, slot):
            first_row = worker * per_worker + c * chunk_rows
            return pltpu.make_async_copy(rows_v.at[slot], out_hbm.at[pl.ds(first_row, chunk_rows)], write_sem.at[slot])

        gather(0, 0).start()

        @pl.loop(0, n_chunks, step=2)
        def _(c0):
            for slot in (0, 1):
                c = c0 + slot
                gather(c, slot).wait()

                @pl.when(c >= 1)
                def _():
                    write(c - 1, 1 - slot).wait()

                @pl.when(c + 1 < n_chunks)
                def _():
                    gather(c + 1, 1 - slot).start()

                write(c, slot).start()

        write(n_chunks - 1, 1).wait()

    return gather_kernel(table, idx.reshape(n_workers, n_chunks, chunk_rows))


def _row_scatter(rows, dest, n_out):
    info = plsc.get_sparse_core_info()
    n_cores, n_workers = info.num_cores, info.num_cores * info.num_subcores
    n_choices, n_rows = dest.shape
    width = rows.shape[1]
    chunk_rows = SC_GATHER_ROWS
    per_worker = n_rows // (n_workers * chunk_rows)
    assert per_worker * n_workers * chunk_rows == n_rows == rows.shape[0] and per_worker % 2 == 0
    idx = dest.reshape(n_choices, n_workers, per_worker, chunk_rows).transpose(1, 0, 2, 3)
    mesh = plsc.VectorSubcoreMesh(core_axis_name="c", subcore_axis_name="s")

    @functools.partial(
        pl.kernel, mesh=mesh,
        out_type=jax.ShapeDtypeStruct((n_out, width), rows.dtype),
        scratch_types=[pltpu.VMEM((n_choices, per_worker, chunk_rows), jnp.int32),
                       pltpu.VMEM((2, chunk_rows, width), rows.dtype), pltpu.SemaphoreType.DMA((2,)),
                       pltpu.SemaphoreType.DMA],
    )
    def scatter_kernel(rows_hbm, idx_hbm, out_hbm, idx_v, rows_v, load_sem, scatter_sem):
        worker = lax.axis_index("s") * n_cores + lax.axis_index("c")
        pltpu.sync_copy(idx_hbm.at[worker], idx_v)

        def load(c, slot):
            first_row = (worker * per_worker + c) * chunk_rows
            return pltpu.make_async_copy(rows_hbm.at[pl.ds(first_row, chunk_rows)], rows_v.at[slot],
                                         load_sem.at[slot])

        load(0, 0).start()

        @pl.loop(0, per_worker, step=2)
        def _(c0):
            for slot in (0, 1):
                c = c0 + slot
                load(c, slot).wait()

                @pl.when(c + 1 < per_worker)
                def _():
                    load(c + 1, 1 - slot).start()

                copies = [pltpu.async_copy(rows_v.at[slot], out_hbm.at[idx_v.at[k, c]], scatter_sem)
                          for k in range(n_choices)]
                for cp in copies:
                    cp.wait()

    return scatter_kernel(rows, idx)


def _moe(h2, ids, layer, w_exp):
    n = h2.shape[0]
    nk = n * TOP_K
    tm = EXPERT_TILE
    dest, counts = _route(ids)
    dest = dest[:TOP_K]
    padded = (counts + tm - 1) // tm * tm
    pends = jnp.cumsum(padded)
    n_blocks = -(-nk // tm) + N_EXPERTS
    n_slots = n_blocks * tm
    block_start = jnp.arange(n_blocks, dtype=jnp.int32) * tm
    block_e = jnp.minimum(jnp.sum((pends[None] <= block_start[:, None]).astype(jnp.int32), axis=1), N_EXPERTS - 1)
    is_e = (block_e[:, None] == jnp.arange(N_EXPERTS, dtype=jnp.int32)[None]).astype(jnp.int32)
    block_rows = jnp.clip(jnp.sum(is_e * (pends - padded + counts)[None], axis=1) - block_start, 0, tm)
    n_used = (pends[-1] // tm).astype(jnp.int32).reshape(1)
    e_ids = jnp.arange(N_EXPERTS, dtype=jnp.int32)
    later = (e_ids[None] > e_ids[:, None]) & (counts[None] > 0)
    next_nonempty = jnp.min(jnp.where(later, e_ids[None], N_EXPERTS), axis=1)
    next_nonempty = jnp.where(next_nonempty == N_EXPERTS, -1, next_nonempty)
    next_e = jnp.sum(is_e * next_nonempty[None], axis=1)
    xb = _row_scatter(h2, dest, n_slots)
    return _experts(xb, block_e, next_e, block_rows, n_used, layer, *w_exp), dest


def _gather_choices(yb, dest, lo, hi):
    return _row_gather(yb, dest[:, lo:hi].reshape(-1)).reshape(TOP_K, hi - lo, -1)


def _mod_parts(mod_l, b):
    d = mod_l.shape[-1] // 6
    lat = [mod_l[:b, k * d:(k + 1) * d].reshape(b, 1, d) for k in range(6)]
    ctx = [jnp.broadcast_to(mod_l[b, k * d:(k + 1) * d].reshape(1, 1, d), (b, 1, d)) for k in range(6)]
    return lat, ctx


def kernel(x, c, ctx, c_ctx, ada_w, ada_b, norm1_g, norm2_g, ev_w_in, ev_w_out, ev_q_gain, ev_k_gain, ev_rpb, ev_conv_w, ev_conv_b, od_w_in, od_w_out, od_conv_w, od_conv_b, od_fwd_wa, od_fwd_ba, od_fwd_wx, od_fwd_bx, od_fwd_lam, od_bwd_wa, od_bwd_ba, od_bwd_wx, od_bwd_bx, od_bwd_lam, router_w, router_b, exp_w_gate, exp_b_gate, exp_w_up, exp_b_up, exp_w_down, exp_b_down):
    b, t, d = x.shape
    l = ctx.shape[1]
    assert ada_w.shape[0] == DEPTH == 2 and t % GRID_W == 0 and t // GRID_W >= WIN_H

    n_rows_c = -(-(b + 1) // SUBLANES) * SUBLANES
    cvec = jnp.zeros((n_rows_c, d), F32).at[:b].set(c).at[b].set(c_ctx)
    mod = _ada_mod(cvec, ada_w, ada_b)

    def router(layer):
        w_r = jnp.zeros((d, ROUTER_PAD), F32).at[:, :N_EXPERTS].set(router_w[layer]).astype(BF16)
        b_r = jnp.zeros((1, ROUTER_PAD), F32).at[0, :N_EXPERTS].set(router_b[layer])
        return w_r, b_r

    w_exp = (exp_w_gate, exp_b_gate, exp_w_up, exp_b_up, exp_w_down, exp_b_down)

    (sh1, sc1, g1, sh2, sc2, g2), (csh1, csc1, cg1, csh2, csc2, cg2) = _mod_parts(mod[0], b)
    n1 = norm1_g[0].reshape(1, d)
    n2 = norm2_g[0].reshape(1, d)
    w_in = ev_w_in[0].astype(BF16)
    w_out = ev_w_out[0].astype(BF16)
    q_scale = NA_HEAD_DIM ** -0.5 * LOG2_E
    head_gain = jnp.stack([jnp.tile(ev_q_gain[0] * q_scale, NA_HEADS), jnp.tile(ev_k_gain[0], NA_HEADS)])
    head_gain = head_gain.reshape(2, 1, NA_WIDTH).astype(F32)
    proj = _inproj(x, n1, sh1, sc1, w_in, head_gain)
    proj_c = _inproj(ctx, n1, csh1, csc1, w_in, head_gain)
    kh, r0, row_type, patterns = _na_tables(t // GRID_W)
    bias_tab = _na_bias_table(ev_rpb[0], patterns)
    o_a = _neighbourhood_attention(proj, proj_c, bias_tab, r0, row_type, kh)
    oc_a = _context_attention(proj_c)
    w_r, b_r = router(0)
    n_tok = b * (l + t)
    fresh = (jnp.zeros((n_tok, d // 2), jnp.int32), jnp.zeros((n_tok, ROUTER_PAD), jnp.int32))
    c1, tokens, ids_all, gates_c = _even_out(oc_a, proj_c, ev_conv_w[0], ev_conv_b[0], w_out, ctx, cg1, n2, csh2,
                                             csc2, w_r, b_r, (n_tok, 0), fresh)
    x1, tokens, ids_all, gates = _even_out(o_a, proj, ev_conv_w[0], ev_conv_b[0], w_out, x, g1, n2, sh2, sc2, w_r,
                                           b_r, (n_tok, b * l), (tokens, ids_all))
    yb, dest = _moe(tokens, ids_all, 0, w_exp)
    y_sel = _gather_choices(yb, dest, 0, b * (l + t))
    hctx = _combine(c1, cg2, gates_c, y_sel, 0)
    x = _combine(x1, g2, gates, y_sel, b * l)

    (sh1, sc1, g1, sh2, sc2, g2), (csh1, csc1, _, _, _, _) = _mod_parts(mod[1], b)
    n1 = norm1_g[1].reshape(1, d)
    n2 = norm2_g[1].reshape(1, d)
    w_in = od_w_in[0].astype(BF16)
    width = w_in.shape[1] // 2
    proj = _inproj_conv(x, n1, sh1, sc1, w_in, od_conv_w[0], od_conv_b[0], width)
    u_ctx = _inproj_conv(hctx, n1, csh1, csc1, w_in[:, width:], od_conv_w[0], od_conv_b[0], 0)
    h_dir = []
    for reverse, (wa, ba, wx, bx, lam) in ((False, (od_fwd_wa, od_fwd_ba, od_fwd_wx, od_fwd_bx, od_fwd_lam)),
                                           (True, (od_bwd_wa, od_bwd_ba, od_bwd_wx, od_bwd_bx, od_bwd_lam))):
        w_cat = jnp.concatenate([wa[0], wx[0]], axis=-1).astype(BF16)
        h_dir.append(_lru_scan(proj, u_ctx, w_cat, ba[0], bx[0], lam[0], reverse))
    w_r, b_r = router(1)
    x1, h2, ids, gates = _odd_out(h_dir[0], h_dir[1], proj, od_w_out[0].astype(BF16), x, g1, n2, sh2, sc2, w_r, b_r)
    yb, dest = _moe(h2, ids, 1, w_exp)
    return _combine(x1, g2, gates, _gather_choices(yb, dest, 0, b * t), 0)
```

```python
import functools

import numpy as np
import jax
import jax.numpy as jnp
from jax import lax
from jax.experimental import pallas as pl
from jax.experimental.pallas import tpu as pltpu
from jax.experimental.pallas import tpu_sc as plsc

DEPTH = 2
GRID_W = 64
EPS = 1e-6
NEG_INF = -1e30
LOG2_E = 1.4426950408889634
NA_HEADS = 8
NA_HEAD_DIM = 64
NA_WIDTH = NA_HEADS * NA_HEAD_DIM
HEAD_PAIRS = NA_HEADS // 2
WIN_H = 8
WIN_W = 16
SC_CONV = 3
LRU_BLOCKS = 4
LRU_CONV = 4
LRU_C = 8.0
N_EXPERTS = 32
TOP_K = 4
SWIGLU_LIMIT = 7.0
SWIGLU_ALPHA = 1.702

LANES = 128
SUBLANES = 8
HALO = 16
ROUTER_PAD = LANES
NA_ROWS_PER_STEP = 4
LRU_SUB_TILES = 4
OUT_TILE = 1024
OUT_SUB_TILE = 512
EXPERT_TILE = 2048
EXPERT_SUB = 512
SC_GATHER_ROWS = 64
VMEM_LIMIT = 56 * 1024 * 1024

F32 = jnp.float32
BF16 = jnp.bfloat16


def _params(sem, vmem=VMEM_LIMIT):
    return pltpu.CompilerParams(dimension_semantics=sem, vmem_limit_bytes=vmem)


def _dot(a, b):
    return jnp.dot(a, b, preferred_element_type=F32)


def _dot_nt(a, b):
    return lax.dot_general(a, b, (((1,), (1,)), ((), ())), preferred_element_type=F32)


def _pack_rows(v):
    w = v.shape[-1] // 2
    lo = lax.bitcast_convert_type(v[:, :w].astype(BF16).astype(F32), jnp.int32)
    hi = lax.bitcast_convert_type(v[:, w:].astype(BF16).astype(F32), jnp.int32)
    return lax.shift_right_logical(lo, 16) | (hi & jnp.int32(-65536))


def _unpack_rows(p):
    lo = lax.bitcast_convert_type(lax.shift_left(p, 16), F32)
    hi = lax.bitcast_convert_type(p & jnp.int32(-65536), F32)
    return lo, hi


def _rms_mod(x, g, shift, scale):
    ms = jnp.mean(x * x, axis=-1, keepdims=True)
    y = x * lax.rsqrt(ms + EPS) * g
    return y * (1.0 + scale) + shift


def _ada_kernel(c_ref, w_ref, b_ref, o_ref):
    c = c_ref[...]
    s = (c * jax.nn.sigmoid(c)).astype(BF16)
    o_ref[0] = _dot(s, w_ref[0].astype(BF16)) + b_ref[0]


def _ada_mod(cvec, ada_w, ada_b):
    depth, d, n = ada_w.shape
    r = cvec.shape[0]
    tn = 1536
    return pl.pallas_call(
        _ada_kernel,
        out_shape=jax.ShapeDtypeStruct((depth, r, n), F32),
        grid=(depth, n // tn),
        in_specs=[
            pl.BlockSpec((r, d), lambda l, j: (0, 0)),
            pl.BlockSpec((1, d, tn), lambda l, j: (l, 0, j)),
            pl.BlockSpec((1, 1, tn), lambda l, j: (l, 0, j)),
        ],
        out_specs=pl.BlockSpec((1, r, tn), lambda l, j: (l, 0, j)),
        compiler_params=_params(("parallel", "parallel")),
        name="ada_mod",
    )(cvec, ada_w, ada_b.reshape(depth, 1, n))


def _inproj_kernel(x_ref, g_ref, sh_ref, sc_ref, w_ref, hg_ref, ones_ref, o_ref, *, n_tiles, tn, n_headnorm):
    h = _rms_mod(x_ref[0], g_ref[...], sh_ref[0], sc_ref[0]).astype(BF16)
    for j in range(n_tiles):
        y = _dot(h, w_ref[:, j * tn:(j + 1) * tn])
        if j < n_headnorm:
            ms = _dot((y * y).astype(BF16), ones_ref[...]) * (1.0 / NA_HEAD_DIM)
            y = y * lax.rsqrt(ms + EPS) * hg_ref[j]
        o_ref[0, :, j * tn:(j + 1) * tn] = y.astype(o_ref.dtype)


def _inproj(x, g, shift, scale, w, head_gain=None):
    b, t, d = x.shape
    n = w.shape[1]
    tn = NA_WIDTH
    tm = min(t, 512)
    n_headnorm = 0 if head_gain is None else head_gain.shape[0]
    if head_gain is None:
        head_gain = jnp.ones((1, 1, tn), F32)
    hid = np.arange(tn) // NA_HEAD_DIM
    ones_bd = jnp.asarray((hid[:, None] == hid[None, :]), BF16)
    kern = functools.partial(_inproj_kernel, n_tiles=n // tn, tn=tn, n_headnorm=n_headnorm)
    return pl.pallas_call(
        kern,
        out_shape=jax.ShapeDtypeStruct((b, t, n), BF16),
        grid=(b, t // tm),
        in_specs=[
            pl.BlockSpec((1, tm, d), lambda bi, i: (bi, i, 0)),
            pl.BlockSpec((1, d), lambda bi, i: (0, 0)),
            pl.BlockSpec((1, 1, d), lambda bi, i: (bi, 0, 0)),
            pl.BlockSpec((1, 1, d), lambda bi, i: (bi, 0, 0)),
            pl.BlockSpec((d, n), lambda bi, i: (0, 0)),
            pl.BlockSpec(head_gain.shape, lambda bi, i: (0, 0, 0)),
            pl.BlockSpec((tn, tn), lambda bi, i: (0, 0)),
        ],
        out_specs=pl.BlockSpec((1, tm, n), lambda bi, i: (bi, i, 0)),
        compiler_params=_params(("parallel", "parallel")),
        name="inproj",
    )(x, g, shift, scale, w, head_gain, ones_bd)


def _inproj_conv_kernel(x_ref, xp_ref, xn_ref, g_ref, sh_ref, sc_ref, w_ref, cw_ref, cb_ref, o_ref, *,
                        n_tiles, n_plain, tn, tm, sub):
    i = pl.program_id(1)
    edge = SUBLANES
    x_ext = jnp.concatenate([xp_ref[0], x_ref[0], xn_ref[0]], axis=0)
    h = _rms_mod(x_ext, g_ref[...], sh_ref[0], sc_ref[0]).astype(BF16)
    first_step = i == 0
    last_step = i == pl.num_programs(1) - 1
    left = (LRU_CONV - 1) // 2
    n_sub = tm // sub
    for s in range(n_sub):
        h_s = h[s * sub:(s + 1) * sub + 2 * edge]
        rows = slice(s * sub, (s + 1) * sub)
        keep_prev = jnp.where(first_step, 0.0, 1.0) if s == 0 else 1.0
        keep_next = jnp.where(last_step, 0.0, 1.0) if s == n_sub - 1 else 1.0
        for j in range(n_tiles):
            y = _dot(h_s, w_ref[:, j * tn:(j + 1) * tn])
            if j < n_plain:
                o_ref[0, rows, j * tn:(j + 1) * tn] = y[edge:edge + sub].astype(o_ref.dtype)
                continue
            u = jnp.concatenate([y[:edge] * keep_prev, y[edge:edge + sub], y[edge + sub:] * keep_next], axis=0)
            cw = cw_ref[:, (j - n_plain) * tn:(j - n_plain + 1) * tn]
            uc = cb_ref[:, (j - n_plain) * tn:(j - n_plain + 1) * tn]
            n_ext = u.shape[0]
            for k in range(LRU_CONV):
                shifted = u if k == left else pltpu.roll(u, (left - k) % n_ext, 0)
                uc = uc + shifted[edge:edge + sub] * cw[k:k + 1]
            o_ref[0, rows, j * tn:(j + 1) * tn] = uc.astype(o_ref.dtype)


def _inproj_conv(x, g, shift, scale, w, conv_w, conv_b, n_plain_cols):
    b, t, d = x.shape
    n = w.shape[1]
    tn = NA_WIDTH
    tm = min(t, OUT_TILE)
    hb = tm // SUBLANES
    n_hblocks = t // SUBLANES
    kern = functools.partial(_inproj_conv_kernel, n_tiles=n // tn, n_plain=n_plain_cols // tn, tn=tn, tm=tm,
                             sub=min(tm, OUT_SUB_TILE))
    const = lambda bi, i: (0, 0)
    return pl.pallas_call(
        kern,
        out_shape=jax.ShapeDtypeStruct((b, t, n), BF16),
        grid=(b, t // tm),
        in_specs=[
            pl.BlockSpec((1, tm, d), lambda bi, i: (bi, i, 0)),
            pl.BlockSpec((1, SUBLANES, d), lambda bi, i: (bi, jnp.maximum(i * hb - 1, 0), 0)),
            pl.BlockSpec((1, SUBLANES, d), lambda bi, i: (bi, jnp.minimum((i + 1) * hb, n_hblocks - 1), 0)),
            pl.BlockSpec((1, d), const),
            pl.BlockSpec((1, 1, d), lambda bi, i: (bi, 0, 0)),
            pl.BlockSpec((1, 1, d), lambda bi, i: (bi, 0, 0)),
            pl.BlockSpec((d, n), const),
            pl.BlockSpec(conv_w.shape, const),
            pl.BlockSpec((1, conv_w.shape[1]), const),
        ],
        out_specs=pl.BlockSpec((1, tm, n), lambda bi, i: (bi, i, 0)),
        compiler_params=_params(("parallel", "parallel")),
        name="inproj_conv",
    )(x, x, x, g, shift, scale, w, conv_w, conv_b.reshape(1, -1))


def _na_tables(rows):
    kh = min(WIN_H, rows)
    r = np.arange(rows)
    r0 = np.clip(r - kh // 2, 0, rows - kh)
    dr = r0[:, None] + np.arange(kh)[None] - r[:, None] + WIN_H - 1
    patterns, row_type = np.unique(dr, axis=0, return_inverse=True)
    return kh, r0.astype(np.int32), row_type.reshape(-1).astype(np.int32), patterns


def _na_bias_table(rpb, patterns):
    qc = np.arange(GRID_W)
    kc = np.arange(GRID_W)
    c0 = np.clip(qc - WIN_W // 2, 0, GRID_W - WIN_W)[:, None]
    valid = (kc[None] >= c0) & (kc[None] < c0 + WIN_W)
    dc = np.clip(kc[None] - qc[:, None] + WIN_W - 1, 0, 2 * WIN_W - 2)
    n_pat, kh = patterns.shape
    onehot_dc = jnp.asarray(dc[None] == np.arange(2 * WIN_W - 1)[:, None, None], F32)
    tab = jnp.einsum('hpic,cqk->hpiqk', rpb.astype(F32)[:, patterns], onehot_dc,
                     precision=lax.Precision.HIGHEST)
    tab = jnp.where(valid[None, None, None], tab * LOG2_E, NEG_INF)
    tab = tab.reshape(HEAD_PAIRS, 2, n_pat, kh, GRID_W, GRID_W)
    tab = tab.transpose(2, 0, 1, 4, 3, 5)
    return tab.reshape(n_pat, HEAD_PAIRS, 2 * GRID_W, kh * GRID_W)


def _pair_attention(q, keys, values, biases):
    m = q.shape[0]
    qq = _stack_heads(q)
    scores = []
    for k, bias in zip(keys, biases):
        s = _dot_nt(qq, k)
        scores.append(s if bias is None else s + bias)
    s = jnp.concatenate(scores, axis=1)
    e = jnp.exp2(s - jnp.max(s, axis=-1, keepdims=True))
    denom = jnp.sum(e, axis=-1, keepdims=True)
    e = e.astype(BF16)
    o, start = 0.0, 0
    for v in values:
        o = o + _dot(e[:, start:start + v.shape[0]], v)
        start += v.shape[0]
    o = o * (1.0 / denom)
    lane_o = lax.broadcasted_iota(jnp.int32, (m, LANES), 1)
    return jnp.where(lane_o < NA_HEAD_DIM, o[:m], o[m:])


def _stack_heads(q):
    lane = lax.broadcasted_iota(jnp.int32, q.shape, 1)
    zero = jnp.zeros_like(q)
    return jnp.concatenate([jnp.where(lane < NA_HEAD_DIM, q, zero), jnp.where(lane >= NA_HEAD_DIM, q, zero)], axis=0)


def _na_kernel(r0_ref, type_ref, q_ref, k_ref, v_ref, kc_ref, vc_ref, bias_ref, o_ref, s_ref, p_ref, *,
               kh, rows_per_step):
    n_lat = kh * GRID_W
    tiles = [(j, p) for j in range(rows_per_step) for p in range(HEAD_PAIRS)]
    window = []
    for j in range(rows_per_step):
        r = pl.program_id(1) * rows_per_step + j
        window.append((pl.multiple_of(r0_ref[r] * GRID_W, GRID_W), type_ref[r]))

    for idx, (j, p) in enumerate(tiles):
        start, rtype = window[j]
        cols = slice(p * LANES, (p + 1) * LANES)
        qq = _stack_heads(q_ref[0, j * GRID_W:(j + 1) * GRID_W, cols])
        s_ref[idx, :, :n_lat] = _dot_nt(qq, k_ref[0, pl.ds(start, n_lat), cols]) + bias_ref[rtype, p]
        s_ref[idx, :, n_lat:] = _dot_nt(qq, kc_ref[0, :, cols])

    denoms = []
    for idx in range(len(tiles)):
        s = s_ref[idx]
        e = jnp.exp2(s - jnp.max(s, axis=-1, keepdims=True))
        denoms.append(jnp.sum(e, axis=-1, keepdims=True))
        p_ref[idx] = e.astype(BF16)

    lane = lax.broadcasted_iota(jnp.int32, (GRID_W, LANES), 1)
    for idx, (j, p) in enumerate(tiles):
        start, _ = window[j]
        cols = slice(p * LANES, (p + 1) * LANES)
        o = _dot(p_ref[idx, :, :n_lat], v_ref[0, pl.ds(start, n_lat), cols]) + _dot(p_ref[idx, :, n_lat:],
                                                                                 vc_ref[0, :, cols])
        o = o * (1.0 / denoms[idx])
        o = jnp.where(lane < NA_HEAD_DIM, o[:GRID_W], o[GRID_W:])
        o_ref[0, j * GRID_W:(j + 1) * GRID_W, cols] = o.astype(o_ref.dtype)


def _neighbourhood_attention(proj, proj_c, bias_tab, r0, row_type, kh):
    b, t, _ = proj.shape
    l = proj_c.shape[1]
    rows = t // GRID_W
    w = NA_WIDTH
    rps = int(np.gcd(rows, NA_ROWS_PER_STEP))
    q_rows = rps * GRID_W
    grid_spec = pltpu.PrefetchScalarGridSpec(
        num_scalar_prefetch=2,
        grid=(b, rows // rps),
        in_specs=[
            pl.BlockSpec((1, q_rows, w), lambda bi, r, *_: (bi, r, 0)),
            pl.BlockSpec((1, t, w), lambda bi, r, *_: (bi, 0, 1)),
            pl.BlockSpec((1, t, w), lambda bi, r, *_: (bi, 0, 2)),
            pl.BlockSpec((1, l, w), lambda bi, r, *_: (bi, 0, 1)),
            pl.BlockSpec((1, l, w), lambda bi, r, *_: (bi, 0, 2)),
            pl.BlockSpec(bias_tab.shape, lambda bi, r, *_: (0, 0, 0, 0)),
        ],
        out_specs=pl.BlockSpec((1, q_rows, w), lambda bi, r, *_: (bi, r, 0)),
        scratch_shapes=[pltpu.VMEM((rps * HEAD_PAIRS, 2 * GRID_W, kh * GRID_W + l), F32),
                        pltpu.VMEM((rps * HEAD_PAIRS, 2 * GRID_W, kh * GRID_W + l), BF16)],
    )
    return pl.pallas_call(
        functools.partial(_na_kernel, kh=kh, rows_per_step=rps),
        out_shape=jax.ShapeDtypeStruct((b, t, w), BF16),
        grid_spec=grid_spec,
        compiler_params=_params(("parallel", "arbitrary")),
        name="na_attention",
    )(jnp.asarray(r0), jnp.asarray(row_type), proj, proj, proj, proj_c, proj_c, bias_tab)


def _ctx_attn_kernel(q_ref, k_ref, v_ref, o_ref):
    for p in range(HEAD_PAIRS):
        cols = slice(p * LANES, (p + 1) * LANES)
        o = _pair_attention(q_ref[0, :, cols], [k_ref[0, :, cols]], [v_ref[0, :, cols]], [None])
        o_ref[0, :, cols] = o.astype(o_ref.dtype)


def _context_attention(proj_c):
    b, l, _ = proj_c.shape
    w = NA_WIDTH
    return pl.pallas_call(
        _ctx_attn_kernel,
        out_shape=jax.ShapeDtypeStruct((b, l, w), BF16),
        grid=(b,),
        in_specs=[pl.BlockSpec((1, l, w), lambda bi, j=j: (bi, 0, j)) for j in range(3)],
        out_specs=pl.BlockSpec((1, l, w), lambda bi: (bi, 0, 0)),
        compiler_params=_params(("parallel",)),
        name="ctx_attention",
    )(proj_c, proj_c, proj_c)


def _top4(logits):
    lane = lax.broadcasted_iota(jnp.int32, logits.shape, 1)
    cur = jnp.where(lane < N_EXPERTS, logits, -jnp.inf)
    vals, idxs = [], []
    for _ in range(TOP_K):
        m = jnp.max(cur, axis=-1, keepdims=True)
        first = jnp.min(jnp.where(cur == m, lane, ROUTER_PAD).astype(F32), axis=-1, keepdims=True)
        idx = first.astype(jnp.int32)
        vals.append(m)
        idxs.append(idx)
        cur = jnp.where(lane == idx, -jnp.inf, cur)
    exps = [jnp.exp(v - vals[0]) for v in vals]
    inv = 1.0 / functools.reduce(jnp.add, exps)
    ids = jnp.zeros(logits.shape, jnp.int32)
    gates = jnp.zeros(logits.shape, F32)
    for k in range(TOP_K):
        ids = jnp.where(lane == k, idxs[k], ids)
        gates = jnp.where(lane == k, exps[k] * inv, gates)
    return ids, gates


def _layer_tail(y, rows, x_ref, g1_ref, n2_ref, sh2_ref, sc2_ref, wr_ref, br_ref, *rest):
    xo_ref, h2_ref, ids_ref, gates_ref = rest[-4:]
    x_new = x_ref[0, rows] + g1_ref[0] * y
    xo_ref[0, rows] = x_new
    h2 = _rms_mod(x_new, n2_ref[...], sh2_ref[0], sc2_ref[0])
    h2_ref[rows] = _pack_rows(h2)
    logits = _dot(h2.astype(BF16), wr_ref[...]) + br_ref[...]
    ids, gates = _top4(logits)
    ids_ref[rows] = ids
    gates_ref[0, rows] = gates


def _sub_tiles(n_rows):
    sub = min(OUT_SUB_TILE, n_rows)
    return [slice(s, s + sub) for s in range(0, n_rows, sub)]


def _tail_specs(b, t, d, tm, n_inputs_before, tokens, shared):
    n_total, offset = tokens
    n_t = t // tm
    assert offset % tm == 0
    row = lambda bi, i: (bi, i, 0)
    flat = lambda bi, i: (offset // tm + bi * n_t + i, 0)
    per_b = lambda bi, i: (bi, 0, 0)
    const = lambda bi, i: (0, 0)
    extra, aliases = (), {}
    if shared is not None:
        extra = tuple(shared)
        aliases = {n_inputs_before + 7: 1, n_inputs_before + 8: 2}
    in_specs = [
        pl.BlockSpec((1, tm, d), row),
        pl.BlockSpec((1, 1, d), per_b),
        pl.BlockSpec((1, d), const),
        pl.BlockSpec((1, 1, d), per_b),
        pl.BlockSpec((1, 1, d), per_b),
        pl.BlockSpec((d, ROUTER_PAD), const),
        pl.BlockSpec((1, ROUTER_PAD), const),
    ] + [pl.BlockSpec(memory_space=pl.ANY)] * len(extra)
    out_specs = [
        pl.BlockSpec((1, tm, d), row),
        pl.BlockSpec((tm, d // 2), flat),
        pl.BlockSpec((tm, ROUTER_PAD), flat),
        pl.BlockSpec((1, tm, ROUTER_PAD), row),
    ]
    out_shape = [
        jax.ShapeDtypeStruct((b, t, d), F32),
        jax.ShapeDtypeStruct((n_total, d // 2), jnp.int32),
        jax.ShapeDtypeStruct((n_total, ROUTER_PAD), jnp.int32),
        jax.ShapeDtypeStruct((b, t, ROUTER_PAD), F32),
    ]
    return in_specs, out_specs, out_shape, extra, aliases


def _halo_fix(rolled, at_row, halo_row, present):
    n = rolled.shape[0]
    first = at_row < SUBLANES
    assert first or at_row >= n - SUBLANES
    slab = rolled[:SUBLANES] if first else rolled[n - SUBLANES:]
    sub = lax.broadcasted_iota(jnp.int32, slab.shape, 0)
    fill = jnp.where(present, halo_row, jnp.zeros_like(halo_row))
    slab = jnp.where(sub == at_row % SUBLANES, fill, slab)
    return jnp.concatenate([slab, rolled[SUBLANES:]] if first else [rolled[:n - SUBLANES], slab], axis=0)


def _even_out_kernel(oa_ref, bg_ref, cg_ref, xin_ref, cgp_ref, xinp_ref, cgn_ref, xinn_ref, cw_ref, cb_ref,
                     wa_ref, wb_ref, *tail_refs, tm):
    i = pl.program_id(1)
    has_prev = i > 0
    has_next = i < pl.num_programs(1) - 1
    u = cg_ref[0].astype(F32) * xin_ref[0].astype(F32)
    u_prev = (cgp_ref[0].astype(F32) * xinp_ref[0].astype(F32))[HALO - 1:HALO]
    u_next = (cgn_ref[0].astype(F32) * xinn_ref[0].astype(F32))[0:1]
    u_m1 = _halo_fix(pltpu.roll(u, 1, 0), 0, u_prev, has_prev)
    u_p1 = _halo_fix(pltpu.roll(u, tm - 1, 0), tm - 1, u_next, has_next)
    cw = cw_ref[...]
    conv = u_m1 * cw[0:1] + u * cw[1:2] + u_p1 * cw[2:3] + cb_ref[...]
    o_b = (bg_ref[0].astype(F32) * conv).astype(BF16)
    for rows in _sub_tiles(tm):
        y = _dot(oa_ref[0, rows], wa_ref[...]) + _dot(o_b[rows], wb_ref[...])
        _layer_tail(y, rows, *tail_refs)


def _even_out(o_a, proj, conv_w, conv_b, w_out, x, g1, n2, sh2, sc2, w_r, b_r, tokens, shared=None):
    b, t, d = x.shape
    w = NA_WIDTH
    tm = min(t, OUT_TILE)
    hb = tm // HALO
    n_hblocks = t // HALO
    row = lambda bi, i: (bi, i, 0)
    const = lambda bi, i: (0, 0)
    prev = lambda col: (lambda bi, i: (bi, jnp.maximum(i * hb - 1, 0), col))
    nxt = lambda col: (lambda bi, i: (bi, jnp.minimum((i + 1) * hb, n_hblocks - 1), col))
    tail_in, out_specs, out_shape, extra, aliases = _tail_specs(b, t, d, tm, 12, tokens, shared)
    in_specs = [
        pl.BlockSpec((1, tm, w), row),
        pl.BlockSpec((1, tm, w), lambda bi, i: (bi, i, 3)),
        pl.BlockSpec((1, tm, w), lambda bi, i: (bi, i, 4)),
        pl.BlockSpec((1, tm, w), lambda bi, i: (bi, i, 5)),
        pl.BlockSpec((1, HALO, w), prev(4)),
        pl.BlockSpec((1, HALO, w), prev(5)),
        pl.BlockSpec((1, HALO, w), nxt(4)),
        pl.BlockSpec((1, HALO, w), nxt(5)),
        pl.BlockSpec((SC_CONV, w), const),
        pl.BlockSpec((1, w), const),
        pl.BlockSpec((w, d), const),
        pl.BlockSpec((w, d), const),
    ] + tail_in
    return pl.pallas_call(
        functools.partial(_even_out_kernel, tm=tm),
        out_shape=out_shape,
        grid=(b, t // tm),
        in_specs=in_specs,
        out_specs=out_specs,
        input_output_aliases=aliases,
        compiler_params=_params(("parallel", "parallel")),
        name="even_out",
    )(o_a, proj, proj, proj, proj, proj, proj, proj, conv_w, conv_b.reshape(1, w),
      w_out[:w], w_out[w:], x, g1, n2, sh2, sc2, w_r, b_r, *extra)


def _log_sigmoid(x):
    return jnp.minimum(x, 0.0) - jnp.log1p(jnp.exp(-jnp.abs(x)))


def _sigmoid(x):
    return 0.5 + 0.5 * jnp.tanh(0.5 * x)


def _lru_sub_tile(ucb_time, state, perm_ref, w_ref, ba_ref, bx_ref, lam_ref, want_hidden, reverse):
    sub, width = ucb_time.shape
    blk = width // LRU_BLOCKS
    steps = sub // SUBLANES
    uc = _dot(perm_ref[0], ucb_time)
    ucb = uc.astype(BF16)
    za, zx = [], []
    for h in range(LRU_BLOCKS):
        z = _dot(ucb[:, h * blk:(h + 1) * blk], w_ref[h])
        za.append(z[:, :blk])
        zx.append(z[:, blk:])
    r = _sigmoid(jnp.concatenate(za, axis=1) + ba_ref[...])
    gate_i = _sigmoid(jnp.concatenate(zx, axis=1) + bx_ref[...])
    log_a = (LRU_C * _log_sigmoid(lam_ref[...])) * r
    a = jnp.exp(log_a)
    th = jnp.tanh(log_a)
    num = -2.0 * th
    mult = jnp.where(num > 0.0, num * lax.rsqrt(num * (1.0 - th)), 0.0)
    bcoef = mult * gate_i * uc

    grp = lambda v, j: v[j * SUBLANES:(j + 1) * SUBLANES]
    prods, local = [grp(a, 0)], [grp(bcoef, 0)]
    for j in range(1, steps):
        aj = grp(a, j)
        local.append(aj * local[-1] + grp(bcoef, j))
        prods.append(aj * prods[-1])

    carry_in = [None] * SUBLANES
    for s in (range(SUBLANES - 1, -1, -1) if reverse else range(SUBLANES)):
        carry_in[s] = state
        state = prods[-1][s:s + 1] * state + local[-1][s:s + 1]
    if not want_hidden:
        return state, None
    start = jnp.concatenate(carry_in, axis=0)
    hidden = jnp.concatenate([prods[j] * start + local[j] for j in range(steps)], axis=0)
    return state, _dot(perm_ref[1], hidden.astype(BF16)).astype(BF16)


def _lru_tile(u_ref, perm_ref, w_ref, ba_ref, bx_ref, lam_ref, carry_ref, o_ref, *, reverse):
    sub = perm_ref.shape[1]
    n_sub = u_ref.shape[1] // sub
    state = carry_ref[0:1, :]
    for k in (range(n_sub - 1, -1, -1) if reverse else range(n_sub)):
        rows = slice(k * sub, (k + 1) * sub)
        state, hidden = _lru_sub_tile(u_ref[0, rows], state, perm_ref, w_ref, ba_ref, bx_ref, lam_ref,
                                      o_ref is not None, reverse)
        if o_ref is not None:
            o_ref[0, rows] = hidden
    carry_ref[...] = jnp.broadcast_to(state, carry_ref.shape)


def _lru_kernel(uc_ref, ul_ref, perm_ref, w_ref, ba_ref, bx_ref, lam_ref, o_ref, carry_ref, *, n_ctx_tiles, reverse):
    j = pl.program_id(1)
    shared = (perm_ref, w_ref, ba_ref, bx_ref, lam_ref, carry_ref)

    @pl.when(j == 0)
    def _():
        carry_ref[...] = jnp.zeros_like(carry_ref)

    @pl.when(j < n_ctx_tiles)
    def _():
        _lru_tile(uc_ref, *shared, None, reverse=reverse)

    @pl.when(j >= n_ctx_tiles)
    def _():
        _lru_tile(ul_ref, *shared, o_ref, reverse=reverse)


def _lru_scan(proj, u_ctx, w_cat, ba, bx, lam, reverse):
    b, t, _ = proj.shape
    l, width = u_ctx.shape[1], u_ctx.shape[2]
    sub = min(256, l, t)
    tc = sub * min(LRU_SUB_TILES, l // sub)
    tl = sub * min(LRU_SUB_TILES, t // sub)
    n_c, n_l = l // tc, t // tl

    def pos_of(step, n):
        step = jnp.clip(step, 0, n - 1)
        return (n - 1 - step) if reverse else step

    def tile_map(off, n, col):
        return lambda bi, j: (bi, pos_of(j - off, n), col)

    const2 = lambda bi, j: (0, 0)
    in_specs = [
        pl.BlockSpec((1, tc, width), tile_map(0, n_c, 0)),
        pl.BlockSpec((1, tl, width), tile_map(n_c, n_l, 1)),
        pl.BlockSpec((2, sub, sub), lambda bi, j: (0, 0, 0)),
        pl.BlockSpec(w_cat.shape, lambda bi, j: (0, 0, 0)),
        pl.BlockSpec((1, width), const2),
        pl.BlockSpec((1, width), const2),
        pl.BlockSpec((1, width), const2),
    ]
    steps = sub // SUBLANES
    step, block = np.divmod(np.arange(sub), SUBLANES)
    time_of_row = block * steps + (steps - 1 - step if reverse else step)
    perm = (time_of_row[:, None] == np.arange(sub)[None]).astype(np.float32)
    perms = jnp.asarray(np.stack([perm, perm.T]), BF16)
    kern = functools.partial(_lru_kernel, n_ctx_tiles=n_c, reverse=reverse)
    return pl.pallas_call(
        kern,
        out_shape=jax.ShapeDtypeStruct((b, t, width), BF16),
        grid=(b, n_c + n_l),
        in_specs=in_specs,
        out_specs=pl.BlockSpec((1, tl, width), tile_map(n_c, n_l, 0)),
        scratch_shapes=[pltpu.VMEM((SUBLANES, width), F32)],
        compiler_params=_params(("parallel", "arbitrary")),
        name="lru_scan_bwd" if reverse else "lru_scan_fwd",
    )(u_ctx, proj, perms, w_cat, ba.reshape(1, width), bx.reshape(1, width), lam.reshape(1, width))


def _odd_out_kernel(hf_ref, hb_ref, gate_ref, w_ref, *tail_refs):
    for rows in _sub_tiles(hf_ref.shape[1]):
        hsum = hf_ref[0, rows].astype(F32) + hb_ref[0, rows].astype(F32)
        z = hsum * jax.nn.gelu(gate_ref[0, rows].astype(F32), approximate=True)
        y = _dot(z.astype(BF16), w_ref[...])
        _layer_tail(y, rows, *tail_refs)


def _odd_out(h_f, h_b, proj, w_out, x, g1, n2, sh2, sc2, w_r, b_r):
    b, t, d = x.shape
    width = h_f.shape[-1]
    tm = min(t, OUT_TILE)
    row = lambda bi, i: (bi, i, 0)
    tail_in, out_specs, out_shape, _, _ = _tail_specs(b, t, d, tm, 4, (b * t, 0), None)
    in_specs = [
        pl.BlockSpec((1, tm, width), row),
        pl.BlockSpec((1, tm, width), row),
        pl.BlockSpec((1, tm, width), row),
        pl.BlockSpec((width, d), lambda bi, i: (0, 0)),
    ] + tail_in
    return pl.pallas_call(
        _odd_out_kernel,
        out_shape=out_shape,
        grid=(b, t // tm),
        in_specs=in_specs,
        out_specs=out_specs,
        compiler_params=_params(("parallel", "parallel")),
        name="odd_out",
    )(h_f, h_b, proj, w_out, x, g1, n2, sh2, sc2, w_r, b_r)


def _expert_kernel(be_ref, next_ref, rows_ref, nb_ref, x_ref, wg_hbm, bg_ref, wu_hbm, bu_ref, wd_hbm, bd_ref,
                   o_ref, w_f32, wg_bf, wu_bf, wd_bf, h_bf, sem, *, layer, chunk):
    i = pl.program_id(0)
    expert = be_ref[i]
    used = i < nb_ref[0]

    def weight_copies(e):
        return [pltpu.make_async_copy(src.at[layer, e], w_f32.at[j], sem.at[j])
                for j, src in enumerate((wg_hbm, wu_hbm, wd_hbm))]

    @pl.when(i == 0)
    def _():
        for cp in weight_copies(expert):
            cp.start()

    @pl.when(used & ((i == 0) | (expert != be_ref[jnp.maximum(i - 1, 0)])))
    def _():
        for cp in weight_copies(expert):
            cp.wait()
        wg_bf[...] = w_f32[0].astype(BF16)
        wu_bf[...] = w_f32[1].astype(BF16)
        wd_bf[...] = w_f32[2].astype(BF16)

        @pl.when(next_ref[i] >= 0)
        def _():
            for cp in weight_copies(next_ref[i]):
                cp.start()

    sub = h_bf.shape[0]
    n_sub = x_ref.shape[0] // sub
    n_rows = jnp.where(used, rows_ref[i], 0)
    n_live = (n_rows + sub - 1) // sub

    def live_sub_block(s, carry):
        rows = pl.ds(pl.multiple_of(s * sub, sub), sub)
        xp = x_ref[rows]
        row = lax.broadcasted_iota(jnp.int32, xp.shape, 0) + s * sub
        xp = jnp.where(row < n_rows, xp, 0)
        x = jnp.concatenate(_unpack_rows(xp), axis=1).astype(BF16)
        for c in range(wg_bf.shape[1] // chunk):
            cs = slice(c * chunk, (c + 1) * chunk)
            g = jnp.minimum(_dot(x, wg_bf[:, cs]) + bg_ref[0, 0, :, cs], SWIGLU_LIMIT)
            u = jnp.clip(_dot(x, wu_bf[:, cs]) + bu_ref[0, 0, :, cs], -SWIGLU_LIMIT, SWIGLU_LIMIT)
            h_bf[:, cs] = (g * jax.nn.sigmoid(SWIGLU_ALPHA * g) * (u + 1.0)).astype(BF16)
        o_ref[rows] = _pack_rows(_dot(h_bf[...], wd_bf[...]) + bd_ref[0, 0])
        return carry

    def empty_sub_block(s, carry):
        o_ref[pl.ds(pl.multiple_of(s * sub, sub), sub)] = jnp.zeros((sub, o_ref.shape[1]), o_ref.dtype)
        return carry

    lax.fori_loop(0, n_live, live_sub_block, 0)
    lax.fori_loop(n_live, n_sub, empty_sub_block, 0)


def _experts(xb, block_e, next_e, block_rows, n_used, layer, wg, bg, wu, bu, wd, bd):
    n_slots = xb.shape[0]
    depth, n_e, d, d_exp = wg.shape
    assert d == d_exp
    tm = EXPERT_TILE
    n_blocks = n_slots // tm
    xmap = lambda i, be, ne, br, nb: (jnp.minimum(i, nb[0] - 1), 0)
    bmap = lambda i, be, ne, br, nb: (layer, be[i], 0, 0)
    hbm = pl.BlockSpec(memory_space=pl.ANY)
    grid_spec = pltpu.PrefetchScalarGridSpec(
        num_scalar_prefetch=4,
        grid=(n_blocks,),
        in_specs=[
            pl.BlockSpec((tm, d // 2), xmap),
            hbm,
            pl.BlockSpec((1, 1, 1, d_exp), bmap),
            hbm,
            pl.BlockSpec((1, 1, 1, d_exp), bmap),
            hbm,
            pl.BlockSpec((1, 1, 1, d), bmap),
        ],
        out_specs=pl.BlockSpec((tm, d // 2), lambda i, be, ne, br, nb: (i, 0)),
        scratch_shapes=[pltpu.VMEM((3, d, d_exp), F32), pltpu.VMEM((d, d_exp), BF16), pltpu.VMEM((d, d_exp), BF16),
                        pltpu.VMEM((d_exp, d), BF16), pltpu.VMEM((EXPERT_SUB, d_exp), BF16),
                        pltpu.SemaphoreType.DMA((3,))],
    )
    return pl.pallas_call(
        functools.partial(_expert_kernel, layer=layer, chunk=256),
        out_shape=jax.ShapeDtypeStruct((n_slots, d // 2), jnp.int32),
        grid_spec=grid_spec,
        compiler_params=_params(("arbitrary",)),
        name="experts",
    )(block_e, next_e, block_rows, n_used, xb, wg, bg.reshape(depth, n_e, 1, d_exp), wu,
      bu.reshape(depth, n_e, 1, d_exp), wd, bd.reshape(depth, n_e, 1, d))


def _combine_kernel(x_ref, g2_ref, gates_ref, y_ref, o_ref):
    gates = gates_ref[0]
    half = y_ref.shape[-1]
    acc_lo = jnp.zeros((x_ref.shape[1], half), F32)
    acc_hi = jnp.zeros((x_ref.shape[1], half), F32)
    for k in range(TOP_K):
        lo, hi = _unpack_rows(y_ref[k])
        acc_lo = acc_lo + gates[:, k:k + 1] * lo
        acc_hi = acc_hi + gates[:, k:k + 1] * hi
    o_ref[0] = x_ref[0] + g2_ref[0] * jnp.concatenate([acc_lo, acc_hi], axis=1)


def _combine(x, g2, gates, y_sel, tok_offset):
    b, t, d = x.shape
    tm = int(np.gcd(min(t, 512), tok_offset)) if tok_offset else min(t, 512)
    n_t = t // tm
    off = tok_offset // tm
    row = lambda bi, i: (bi, i, 0)
    return pl.pallas_call(
        _combine_kernel,
        out_shape=jax.ShapeDtypeStruct((b, t, d), F32),
        grid=(b, n_t),
        in_specs=[
            pl.BlockSpec((1, tm, d), row),
            pl.BlockSpec((1, 1, d), lambda bi, i: (bi, 0, 0)),
            pl.BlockSpec((1, tm, ROUTER_PAD), row),
            pl.BlockSpec((TOP_K, tm, d // 2), lambda bi, i: (0, off + bi * n_t + i, 0)),
        ],
        out_specs=pl.BlockSpec((1, tm, d), row),
        input_output_aliases={0: 0},
        compiler_params=_params(("parallel", "parallel")),
        name="moe_combine",
    )(x, g2, gates, y_sel)


def _route_kernel(ids_ref, upper_ref, dest_ref, counts_ref, run_ref, *, tile):
    p = pl.program_id(0)
    i = pl.program_id(1)
    tm = ids_ref.shape[0]
    ids_t = ids_ref[...].T
    expert = lax.broadcasted_iota(jnp.int32, (N_EXPERTS, tm), 0)
    chosen = [ids_t[k:k + 1, :] == expert for k in range(TOP_K)]
    picks = functools.reduce(jnp.add, [c.astype(F32) for c in chosen])
    tile_counts = jnp.sum(picks, axis=1, keepdims=True)

    @pl.when((p == 0) & (i == 0))
    def _():
        run_ref[...] = jnp.zeros_like(run_ref)

    @pl.when(p == 0)
    def _():
        run_ref[...] += tile_counts

    @pl.when((p == 1) & (i == 0))
    def _():
        counts = run_ref[...]
        counts_ref[...] = counts.astype(jnp.int32)
        padded = jnp.floor((counts + (tile - 1)) * (1.0 / tile)) * tile
        row = lax.broadcasted_iota(jnp.int32, counts.shape, 0)
        ends = padded
        for s in (1, 2, 4, 8, 16):
            ends = ends + jnp.where(row >= s, pltpu.roll(ends, s, 0), 0.0)
        run_ref[...] = ends - padded

    @pl.when(p == 1)
    def _():
        before = _dot(picks.astype(BF16), upper_ref[...])
        slot = before + run_ref[:, 0:1]
        rows = [jnp.sum(jnp.where(c, slot, 0.0), axis=0, keepdims=True) for c in chosen]
        rows += [jnp.zeros_like(rows[0])] * (dest_ref.shape[0] - TOP_K)
        dest_ref[...] = jnp.concatenate(rows, axis=0).astype(jnp.int32)
        run_ref[...] += tile_counts


def _route(ids):
    n = ids.shape[0]
    tm = int(np.gcd(n, 1024))
    n_tiles = n // tm
    upper = jnp.asarray(np.triu(np.ones((tm, tm), np.float32), 1), BF16)
    dest, counts = pl.pallas_call(
        functools.partial(_route_kernel, tile=EXPERT_TILE),
        out_shape=[jax.ShapeDtypeStruct((SUBLANES, n), jnp.int32),
                   jax.ShapeDtypeStruct((N_EXPERTS, LANES), jnp.int32)],
        grid=(2, n_tiles),
        in_specs=[pl.BlockSpec((tm, ROUTER_PAD), lambda p, i: (i, 0)),
                  pl.BlockSpec((tm, tm), lambda p, i: (0, 0))],
        out_specs=[pl.BlockSpec((SUBLANES, tm), lambda p, i: (0, i * p)),
                   pl.BlockSpec((N_EXPERTS, LANES), lambda p, i: (0, 0))],
        scratch_shapes=[pltpu.VMEM((N_EXPERTS, LANES), F32)],
        compiler_params=_params(("arbitrary", "arbitrary")),
        name="route",
    )(ids, upper)
    return dest, counts[:, 0]


def _row_gather(table, idx):
    info = plsc.get_sparse_core_info()
    n_cores, n_workers = info.num_cores, info.num_cores * info.num_subcores
    n_rows, width = idx.shape[0], table.shape[1]
    chunk_rows = SC_GATHER_ROWS
    per_worker = n_rows // n_workers
    n_chunks = per_worker // chunk_rows
    assert per_worker * n_workers == n_rows and n_chunks * chunk_rows == per_worker and n_chunks % 2 == 0
    mesh = plsc.VectorSubcoreMesh(core_axis_name="c", subcore_axis_name="s")

    @functools.partial(
        pl.kernel, mesh=mesh,
        out_type=jax.ShapeDtypeStruct((n_rows, width), table.dtype),
        scratch_types=[pltpu.VMEM((n_chunks, chunk_rows), jnp.int32), pltpu.VMEM((2, chunk_rows, width), table.dtype),
                       pltpu.SemaphoreType.DMA((2,)), pltpu.SemaphoreType.DMA((2,))],
    )
    def gather_kernel(table_hbm, idx_hbm, out_hbm, idx_v, rows_v, gather_sem, write_sem):
        worker = lax.axis_index("s") * n_cores + lax.axis_index("c")
        pltpu.sync_copy(idx_hbm.at[worker], idx_v)

        def gather(c, slot):
            return pltpu.make_async_copy(table_hbm.at[idx_v.at[c]], rows_v.at[slot], gather_sem.at[slot])

        def write(c, slot):
            first_row = worker * per_worker + c * chunk_rows
            return pltpu.make_async_copy(rows_v.at[slot], out_hbm.at[pl.ds(first_row, chunk_rows)], write_sem.at[slot])

        gather(0, 0).start()

        @pl.loop(0, n_chunks, step=2)
        def _(c0):
            for slot in (0, 1):
                c = c0 + slot
                gather(c, slot).wait()

                @pl.when(c >= 1)
                def _():
                    write(c - 1, 1 - slot).wait()

                @pl.when(c + 1 < n_chunks)
                def _():
                    gather(c + 1, 1 - slot).start()

                write(c, slot).start()

        write(n_chunks - 1, 1).wait()

    return gather_kernel(table, idx.reshape(n_workers, n_chunks, chunk_rows))


def _row_scatter(rows, dest, n_out):
    info = plsc.get_sparse_core_info()
    n_cores, n_workers = info.num_cores, info.num_cores * info.num_subcores
    n_choices, n_rows = dest.shape
    width = rows.shape[1]
    chunk_rows = SC_GATHER_ROWS
    per_worker = n_rows // (n_workers * chunk_rows)
    assert per_worker * n_workers * chunk_rows == n_rows == rows.shape[0]
    idx = dest.reshape(n_choices, n_workers, per_worker, chunk_rows).transpose(1, 0, 2, 3)
    mesh = plsc.VectorSubcoreMesh(core_axis_name="c", subcore_axis_name="s")

    @functools.partial(
        pl.kernel, mesh=mesh,
        out_type=jax.ShapeDtypeStruct((n_out, width), rows.dtype),
        scratch_types=[pltpu.VMEM((n_choices, per_worker, chunk_rows), jnp.int32),
                       pltpu.VMEM((chunk_rows, width), rows.dtype), pltpu.SemaphoreType.DMA],
    )
    def scatter_kernel(rows_hbm, idx_hbm, out_hbm, idx_v, rows_v, sem):
        worker = lax.axis_index("s") * n_cores + lax.axis_index("c")
        pltpu.sync_copy(idx_hbm.at[worker], idx_v)

        @pl.loop(0, per_worker)
        def _(c):
            first_row = (worker * per_worker + c) * chunk_rows
            pltpu.sync_copy(rows_hbm.at[pl.ds(first_row, chunk_rows)], rows_v)
            copies = [pltpu.async_copy(rows_v, out_hbm.at[idx_v.at[k, c]], sem) for k in range(n_choices)]
            for cp in copies:
                cp.wait()

    return scatter_kernel(rows, idx)


def _moe(h2, ids, layer, w_exp):
    n = h2.shape[0]
    nk = n * TOP_K
    tm = EXPERT_TILE
    dest, counts = _route(ids)
    dest = dest[:TOP_K]
    padded = (counts + tm - 1) // tm * tm
    pends = jnp.cumsum(padded)
    n_blocks = -(-nk // tm) + N_EXPERTS
    n_slots = n_blocks * tm
    block_start = jnp.arange(n_blocks, dtype=jnp.int32) * tm
    block_e = jnp.minimum(jnp.sum((pends[None] <= block_start[:, None]).astype(jnp.int32), axis=1), N_EXPERTS - 1)
    is_e = (block_e[:, None] == jnp.arange(N_EXPERTS, dtype=jnp.int32)[None]).astype(jnp.int32)
    block_rows = jnp.clip(jnp.sum(is_e * (pends - padded + counts)[None], axis=1) - block_start, 0, tm)
    n_used = (pends[-1] // tm).astype(jnp.int32).reshape(1)
    e_ids = jnp.arange(N_EXPERTS, dtype=jnp.int32)
    later = (e_ids[None] > e_ids[:, None]) & (counts[None] > 0)
    next_nonempty = jnp.min(jnp.where(later, e_ids[None], N_EXPERTS), axis=1)
    next_nonempty = jnp.where(next_nonempty == N_EXPERTS, -1, next_nonempty)
    next_e = jnp.sum(is_e * next_nonempty[None], axis=1)
    xb = _row_scatter(h2, dest, n_slots)
    return _experts(xb, block_e, next_e, block_rows, n_used, layer, *w_exp), dest


def _gather_choices(yb, dest, lo, hi):
    return _row_gather(yb, dest[:, lo:hi].reshape(-1)).reshape(TOP_K, hi - lo, -1)


def _mod_parts(mod_l, b):
    d = mod_l.shape[-1] // 6
    lat = [mod_l[:b, k * d:(k + 1) * d].reshape(b, 1, d) for k in range(6)]
    ctx = [jnp.broadcast_to(mod_l[b, k * d:(k + 1) * d].reshape(1, 1, d), (b, 1, d)) for k in range(6)]
    return lat, ctx


def kernel(x, c, ctx, c_ctx, ada_w, ada_b, norm1_g, norm2_g, ev_w_in, ev_w_out, ev_q_gain, ev_k_gain, ev_rpb, ev_conv_w, ev_conv_b, od_w_in, od_w_out, od_conv_w, od_conv_b, od_fwd_wa, od_fwd_ba, od_fwd_wx, od_fwd_bx, od_fwd_lam, od_bwd_wa, od_bwd_ba, od_bwd_wx, od_bwd_bx, od_bwd_lam, router_w, router_b, exp_w_gate, exp_b_gate, exp_w_up, exp_b_up, exp_w_down, exp_b_down):
    b, t, d = x.shape
    l = ctx.shape[1]
    assert ada_w.shape[0] == DEPTH == 2 and t % GRID_W == 0 and t // GRID_W >= WIN_H

    n_rows_c = -(-(b + 1) // SUBLANES) * SUBLANES
    cvec = jnp.zeros((n_rows_c, d), F32).at[:b].set(c).at[b].set(c_ctx)
    mod = _ada_mod(cvec, ada_w, ada_b)

    def router(layer):
        w_r = jnp.zeros((d, ROUTER_PAD), F32).at[:, :N_EXPERTS].set(router_w[layer]).astype(BF16)
        b_r = jnp.zeros((1, ROUTER_PAD), F32).at[0, :N_EXPERTS].set(router_b[layer])
        return w_r, b_r

    w_exp = (exp_w_gate, exp_b_gate, exp_w_up, exp_b_up, exp_w_down, exp_b_down)

    (sh1, sc1, g1, sh2, sc2, g2), (csh1, csc1, cg1, csh2, csc2, cg2) = _mod_parts(mod[0], b)
    n1 = norm1_g[0].reshape(1, d)
    n2 = norm2_g[0].reshape(1, d)
    w_in = ev_w_in[0].astype(BF16)
    w_out = ev_w_out[0].astype(BF16)
    q_scale = NA_HEAD_DIM ** -0.5 * LOG2_E
    head_gain = jnp.stack([jnp.tile(ev_q_gain[0] * q_scale, NA_HEADS), jnp.tile(ev_k_gain[0], NA_HEADS)])
    head_gain = head_gain.reshape(2, 1, NA_WIDTH).astype(F32)
    proj = _inproj(x, n1, sh1, sc1, w_in, head_gain)
    proj_c = _inproj(ctx, n1, csh1, csc1, w_in, head_gain)
    kh, r0, row_type, patterns = _na_tables(t // GRID_W)
    bias_tab = _na_bias_table(ev_rpb[0], patterns)
    o_a = _neighbourhood_attention(proj, proj_c, bias_tab, r0, row_type, kh)
    oc_a = _context_attention(proj_c)
    w_r, b_r = router(0)
    n_tok = b * (l + t)
    fresh = (jnp.zeros((n_tok, d // 2), jnp.int32), jnp.zeros((n_tok, ROUTER_PAD), jnp.int32))
    c1, tokens, ids_all, gates_c = _even_out(oc_a, proj_c, ev_conv_w[0], ev_conv_b[0], w_out, ctx, cg1, n2, csh2,
                                             csc2, w_r, b_r, (n_tok, 0), fresh)
    x1, tokens, ids_all, gates = _even_out(o_a, proj, ev_conv_w[0], ev_conv_b[0], w_out, x, g1, n2, sh2, sc2, w_r,
                                           b_r, (n_tok, b * l), (tokens, ids_all))
    yb, dest = _moe(tokens, ids_all, 0, w_exp)
    y_sel = _gather_choices(yb, dest, 0, b * (l + t))
    hctx = _combine(c1, cg2, gates_c, y_sel, 0)
    x = _combine(x1, g2, gates, y_sel, b * l)

    (sh1, sc1, g1, sh2, sc2, g2), (csh1, csc1, _, _, _, _) = _mod_parts(mod[1], b)
    n1 = norm1_g[1].reshape(1, d)
    n2 = norm2_g[1].reshape(1, d)
    w_in = od_w_in[0].astype(BF16)
    width = w_in.shape[1] // 2
    proj = _inproj_conv(x, n1, sh1, sc1, w_in, od_conv_w[0], od_conv_b[0], width)
    u_ctx = _inproj_conv(hctx, n1, csh1, csc1, w_in[:, width:], od_conv_w[0], od_conv_b[0], 0)
    h_dir = []
    for reverse, (wa, ba, wx, bx, lam) in ((False, (od_fwd_wa, od_fwd_ba, od_fwd_wx, od_fwd_bx, od_fwd_lam)),
                                           (True, (od_bwd_wa, od_bwd_ba, od_bwd_wx, od_bwd_bx, od_bwd_lam))):
        w_cat = jnp.concatenate([wa[0], wx[0]], axis=-1).astype(BF16)
        h_dir.append(_lru_scan(proj, u_ctx, w_cat, ba[0], bx[0], lam[0], reverse))
    w_r, b_r = router(1)
    x1, h2, ids, gates = _odd_out(h_dir[0], h_dir[1], proj, od_w_out[0].astype(BF16), x, g1, n2, sh2, sc2, w_r, b_r)
    yb, dest = _moe(h2, ids, 1, w_exp)
    return _combine(x1, g2, gates, _gather_choices(yb, dest, 0, b * t), 0)
```

```python
import functools

import numpy as np
import jax
import jax.numpy as jnp
from jax import lax
from jax.experimental import pallas as pl
from jax.experimental.pallas import tpu as pltpu
from jax.experimental.pallas import tpu_sc as plsc

DEPTH = 2
GRID_W = 64
EPS = 1e-6
NEG_INF = -1e30
LOG2_E = 1.4426950408889634
NA_HEADS = 8
NA_HEAD_DIM = 64
NA_WIDTH = NA_HEADS * NA_HEAD_DIM
HEAD_PAIRS = NA_HEADS // 2
WIN_H = 8
WIN_W = 16
SC_CONV = 3
LRU_BLOCKS = 4
LRU_CONV = 4
LRU_C = 8.0
N_EXPERTS = 32
TOP_K = 4
SWIGLU_LIMIT = 7.0
SWIGLU_ALPHA = 1.702

LANES = 128
SUBLANES = 8
HALO = 16
ROUTER_PAD = LANES
NA_ROWS_PER_STEP = 4
LRU_SUB_TILES = 4
OUT_TILE = 1024
OUT_SUB_TILE = 512
ROUTE_TILE = 2048
EXPERT_TILE = 2048
EXPERT_SUB = 512
SC_GATHER_ROWS = 64
VMEM_LIMIT = 56 * 1024 * 1024

F32 = jnp.float32
BF16 = jnp.bfloat16


def _params(sem, vmem=VMEM_LIMIT):
    return pltpu.CompilerParams(dimension_semantics=sem, vmem_limit_bytes=vmem)


def _dot(a, b):
    return jnp.dot(a, b, preferred_element_type=F32)


def _dot_nt(a, b):
    return lax.dot_general(a, b, (((1,), (1,)), ((), ())), preferred_element_type=F32)


def _pack_rows(v):
    w = v.shape[-1] // 2
    lo = lax.bitcast_convert_type(v[:, :w].astype(BF16).astype(F32), jnp.int32)
    hi = lax.bitcast_convert_type(v[:, w:].astype(BF16).astype(F32), jnp.int32)
    return lax.shift_right_logical(lo, 16) | (hi & jnp.int32(-65536))


def _unpack_rows(p):
    lo = lax.bitcast_convert_type(lax.shift_left(p, 16), F32)
    hi = lax.bitcast_convert_type(p & jnp.int32(-65536), F32)
    return lo, hi


def _rms_mod(x, g, shift, scale):
    ms = jnp.mean(x * x, axis=-1, keepdims=True)
    return (x * lax.rsqrt(ms + EPS)) * (g * (1.0 + scale)) + shift


def _ada_kernel(c_ref, w_ref, b_ref, o_ref):
    c = c_ref[...]
    s = (c * jax.nn.sigmoid(c)).astype(BF16)
    o_ref[0] = _dot(s, w_ref[0].astype(BF16)) + b_ref[0]


def _ada_mod(cvec, ada_w, ada_b):
    depth, d, n = ada_w.shape
    r = cvec.shape[0]
    tn = 1536
    return pl.pallas_call(
        _ada_kernel,
        out_shape=jax.ShapeDtypeStruct((depth, r, n), F32),
        grid=(depth, n // tn),
        in_specs=[
            pl.BlockSpec((r, d), lambda l, j: (0, 0)),
            pl.BlockSpec((1, d, tn), lambda l, j: (l, 0, j)),
            pl.BlockSpec((1, 1, tn), lambda l, j: (l, 0, j)),
        ],
        out_specs=pl.BlockSpec((1, r, tn), lambda l, j: (l, 0, j)),
        compiler_params=_params(("parallel", "parallel")),
        name="ada_mod",
    )(cvec, ada_w, ada_b.reshape(depth, 1, n))


def _inproj_kernel(x_ref, g_ref, sh_ref, sc_ref, w_ref, hg_ref, ones_ref, o_ref, *, n_tiles, tn, n_headnorm):
    h = _rms_mod(x_ref[0], g_ref[...], sh_ref[0], sc_ref[0]).astype(BF16)
    for j in range(n_tiles):
        y = _dot(h, w_ref[:, j * tn:(j + 1) * tn])
        if j < n_headnorm:
            ms = _dot((y * y).astype(BF16), ones_ref[...]) * (1.0 / NA_HEAD_DIM)
            y = y * lax.rsqrt(ms + EPS) * hg_ref[j]
        o_ref[0, :, j * tn:(j + 1) * tn] = y.astype(o_ref.dtype)


def _inproj(x, g, shift, scale, w, head_gain=None):
    b, t, d = x.shape
    n = w.shape[1]
    tn = NA_WIDTH
    tm = min(t, 512)
    n_headnorm = 0 if head_gain is None else head_gain.shape[0]
    if head_gain is None:
        head_gain = jnp.ones((1, 1, tn), F32)
    hid = np.arange(tn) // NA_HEAD_DIM
    ones_bd = jnp.asarray((hid[:, None] == hid[None, :]), BF16)
    kern = functools.partial(_inproj_kernel, n_tiles=n // tn, tn=tn, n_headnorm=n_headnorm)
    return pl.pallas_call(
        kern,
        out_shape=jax.ShapeDtypeStruct((b, t, n), BF16),
        grid=(b, t // tm),
        in_specs=[
            pl.BlockSpec((1, tm, d), lambda bi, i: (bi, i, 0)),
            pl.BlockSpec((1, d), lambda bi, i: (0, 0)),
            pl.BlockSpec((1, 1, d), lambda bi, i: (bi, 0, 0)),
            pl.BlockSpec((1, 1, d), lambda bi, i: (bi, 0, 0)),
            pl.BlockSpec((d, n), lambda bi, i: (0, 0)),
            pl.BlockSpec(head_gain.shape, lambda bi, i: (0, 0, 0)),
            pl.BlockSpec((tn, tn), lambda bi, i: (0, 0)),
        ],
        out_specs=pl.BlockSpec((1, tm, n), lambda bi, i: (bi, i, 0)),
        compiler_params=_params(("parallel", "parallel")),
        name="inproj",
    )(x, g, shift, scale, w, head_gain, ones_bd)


def _inproj_conv_kernel(x_ref, xp_ref, xn_ref, g_ref, sh_ref, sc_ref, w_ref, cw_ref, cb_ref, o_ref, *,
                        n_tiles, n_plain, tn, tm, sub):
    i = pl.program_id(1)
    edge = SUBLANES
    x_ext = jnp.concatenate([xp_ref[0], x_ref[0], xn_ref[0]], axis=0)
    h = _rms_mod(x_ext, g_ref[...], sh_ref[0], sc_ref[0]).astype(BF16)
    first_step = i == 0
    last_step = i == pl.num_programs(1) - 1
    left = (LRU_CONV - 1) // 2
    n_sub = tm // sub
    for s in range(n_sub):
        h_s = h[s * sub:(s + 1) * sub + 2 * edge]
        rows = slice(s * sub, (s + 1) * sub)
        keep_prev = jnp.where(first_step, 0.0, 1.0) if s == 0 else 1.0
        keep_next = jnp.where(last_step, 0.0, 1.0) if s == n_sub - 1 else 1.0
        for j in range(n_tiles):
            y = _dot(h_s, w_ref[:, j * tn:(j + 1) * tn])
            if j < n_plain:
                o_ref[0, rows, j * tn:(j + 1) * tn] = y[edge:edge + sub].astype(o_ref.dtype)
                continue
            u = jnp.concatenate([y[:edge] * keep_prev, y[edge:edge + sub], y[edge + sub:] * keep_next], axis=0)
            cw = cw_ref[:, (j - n_plain) * tn:(j - n_plain + 1) * tn]
            uc = cb_ref[:, (j - n_plain) * tn:(j - n_plain + 1) * tn]
            n_ext = u.shape[0]
            for k in range(LRU_CONV):
                shifted = u if k == left else pltpu.roll(u, (left - k) % n_ext, 0)
                uc = uc + shifted[edge:edge + sub] * cw[k:k + 1]
            o_ref[0, rows, j * tn:(j + 1) * tn] = uc.astype(o_ref.dtype)


def _inproj_conv(x, g, shift, scale, w, conv_w, conv_b, n_plain_cols):
    b, t, d = x.shape
    n = w.shape[1]
    tn = NA_WIDTH
    tm = min(t, OUT_TILE)
    hb = tm // SUBLANES
    n_hblocks = t // SUBLANES
    kern = functools.partial(_inproj_conv_kernel, n_tiles=n // tn, n_plain=n_plain_cols // tn, tn=tn, tm=tm,
                             sub=min(tm, OUT_SUB_TILE))
    const = lambda bi, i: (0, 0)
    return pl.pallas_call(
        kern,
        out_shape=jax.ShapeDtypeStruct((b, t, n), BF16),
        grid=(b, t // tm),
        in_specs=[
            pl.BlockSpec((1, tm, d), lambda bi, i: (bi, i, 0)),
            pl.BlockSpec((1, SUBLANES, d), lambda bi, i: (bi, jnp.maximum(i * hb - 1, 0), 0)),
            pl.BlockSpec((1, SUBLANES, d), lambda bi, i: (bi, jnp.minimum((i + 1) * hb, n_hblocks - 1), 0)),
            pl.BlockSpec((1, d), const),
            pl.BlockSpec((1, 1, d), lambda bi, i: (bi, 0, 0)),
            pl.BlockSpec((1, 1, d), lambda bi, i: (bi, 0, 0)),
            pl.BlockSpec((d, n), const),
            pl.BlockSpec(conv_w.shape, const),
            pl.BlockSpec((1, conv_w.shape[1]), const),
        ],
        out_specs=pl.BlockSpec((1, tm, n), lambda bi, i: (bi, i, 0)),
        compiler_params=_params(("parallel", "parallel")),
        name="inproj_conv",
    )(x, x, x, g, shift, scale, w, conv_w, conv_b.reshape(1, -1))


def _na_tables(rows):
    kh = min(WIN_H, rows)
    r = np.arange(rows)
    r0 = np.clip(r - kh // 2, 0, rows - kh)
    dr = r0[:, None] + np.arange(kh)[None] - r[:, None] + WIN_H - 1
    patterns, row_type = np.unique(dr, axis=0, return_inverse=True)
    return kh, r0.astype(np.int32), row_type.reshape(-1).astype(np.int32), patterns


def _na_bias_table(rpb, patterns):
    qc = np.arange(GRID_W)
    kc = np.arange(GRID_W)
    c0 = np.clip(qc - WIN_W // 2, 0, GRID_W - WIN_W)[:, None]
    valid = (kc[None] >= c0) & (kc[None] < c0 + WIN_W)
    dc = np.clip(kc[None] - qc[:, None] + WIN_W - 1, 0, 2 * WIN_W - 2)
    n_pat, kh = patterns.shape
    onehot_dc = jnp.asarray(dc[None] == np.arange(2 * WIN_W - 1)[:, None, None], F32)
    tab = jnp.einsum('hpic,cqk->hpiqk', rpb.astype(F32)[:, patterns], onehot_dc,
                     precision=lax.Precision.HIGHEST)
    tab = jnp.where(valid[None, None, None], tab * LOG2_E, NEG_INF)
    tab = tab.reshape(HEAD_PAIRS, 2, n_pat, kh, GRID_W, GRID_W)
    tab = tab.transpose(2, 0, 1, 4, 3, 5)
    return tab.reshape(n_pat, HEAD_PAIRS, 2 * GRID_W, kh * GRID_W)


def _pair_attention(q, keys, values, biases):
    m = q.shape[0]
    qq = _stack_heads(q)
    scores = []
    for k, bias in zip(keys, biases):
        s = _dot_nt(qq, k)
        scores.append(s if bias is None else s + bias)
    s = jnp.concatenate(scores, axis=1)
    e = jnp.exp2(s - jnp.max(s, axis=-1, keepdims=True))
    denom = jnp.sum(e, axis=-1, keepdims=True)
    e = e.astype(BF16)
    o, start = 0.0, 0
    for v in values:
        o = o + _dot(e[:, start:start + v.shape[0]], v)
        start += v.shape[0]
    o = o * (1.0 / denom)
    lane_o = lax.broadcasted_iota(jnp.int32, (m, LANES), 1)
    return jnp.where(lane_o < NA_HEAD_DIM, o[:m], o[m:])


def _stack_heads(q):
    lane = lax.broadcasted_iota(jnp.int32, q.shape, 1)
    zero = jnp.zeros_like(q)
    return jnp.concatenate([jnp.where(lane < NA_HEAD_DIM, q, zero), jnp.where(lane >= NA_HEAD_DIM, q, zero)], axis=0)


def _na_kernel(r0_ref, type_ref, q_ref, k_ref, v_ref, kc_ref, vc_ref, bias_ref, o_ref, s_ref, p_ref, *,
               kh, rows_per_step):
    n_lat = kh * GRID_W
    tiles = [(j, p) for j in range(rows_per_step) for p in range(HEAD_PAIRS)]
    window = []
    for j in range(rows_per_step):
        r = pl.program_id(1) * rows_per_step + j
        window.append((pl.multiple_of(r0_ref[r] * GRID_W, GRID_W), type_ref[r]))

    for idx, (j, p) in enumerate(tiles):
        start, rtype = window[j]
        cols = slice(p * LANES, (p + 1) * LANES)
        qq = _stack_heads(q_ref[0, j * GRID_W:(j + 1) * GRID_W, cols])
        s_ref[idx, :, :n_lat] = _dot_nt(qq, k_ref[0, pl.ds(start, n_lat), cols]) + bias_ref[rtype, p]
        s_ref[idx, :, n_lat:] = _dot_nt(qq, kc_ref[0, :, cols])

    denoms = []
    for idx in range(len(tiles)):
        s = s_ref[idx]
        e = jnp.exp2(s - jnp.max(s, axis=-1, keepdims=True))
        denoms.append(jnp.sum(e, axis=-1, keepdims=True))
        p_ref[idx] = e.astype(BF16)

    lane = lax.broadcasted_iota(jnp.int32, (GRID_W, LANES), 1)
    for idx, (j, p) in enumerate(tiles):
        start, _ = window[j]
        cols = slice(p * LANES, (p + 1) * LANES)
        o = _dot(p_ref[idx, :, :n_lat], v_ref[0, pl.ds(start, n_lat), cols]) + _dot(p_ref[idx, :, n_lat:],
                                                                                 vc_ref[0, :, cols])
        o = o * (1.0 / denoms[idx])
        o = jnp.where(lane < NA_HEAD_DIM, o[:GRID_W], o[GRID_W:])
        o_ref[0, j * GRID_W:(j + 1) * GRID_W, cols] = o.astype(o_ref.dtype)


def _neighbourhood_attention(proj, proj_c, bias_tab, r0, row_type, kh):
    b, t, _ = proj.shape
    l = proj_c.shape[1]
    rows = t // GRID_W
    w = NA_WIDTH
    rps = int(np.gcd(rows, NA_ROWS_PER_STEP))
    q_rows = rps * GRID_W
    grid_spec = pltpu.PrefetchScalarGridSpec(
        num_scalar_prefetch=2,
        grid=(b, rows // rps),
        in_specs=[
            pl.BlockSpec((1, q_rows, w), lambda bi, r, *_: (bi, r, 0)),
            pl.BlockSpec((1, t, w), lambda bi, r, *_: (bi, 0, 1)),
            pl.BlockSpec((1, t, w), lambda bi, r, *_: (bi, 0, 2)),
            pl.BlockSpec((1, l, w), lambda bi, r, *_: (bi, 0, 1)),
            pl.BlockSpec((1, l, w), lambda bi, r, *_: (bi, 0, 2)),
            pl.BlockSpec(bias_tab.shape, lambda bi, r, *_: (0, 0, 0, 0)),
        ],
        out_specs=pl.BlockSpec((1, q_rows, w), lambda bi, r, *_: (bi, r, 0)),
        scratch_shapes=[pltpu.VMEM((rps * HEAD_PAIRS, 2 * GRID_W, kh * GRID_W + l), F32),
                        pltpu.VMEM((rps * HEAD_PAIRS, 2 * GRID_W, kh * GRID_W + l), BF16)],
    )
    return pl.pallas_call(
        functools.partial(_na_kernel, kh=kh, rows_per_step=rps),
        out_shape=jax.ShapeDtypeStruct((b, t, w), BF16),
        grid_spec=grid_spec,
        compiler_params=_params(("parallel", "arbitrary")),
        name="na_attention",
    )(jnp.asarray(r0), jnp.asarray(row_type), proj, proj, proj, proj_c, proj_c, bias_tab)


def _ctx_attn_kernel(q_ref, k_ref, v_ref, o_ref):
    for p in range(HEAD_PAIRS):
        cols = slice(p * LANES, (p + 1) * LANES)
        o = _pair_attention(q_ref[0, :, cols], [k_ref[0, :, cols]], [v_ref[0, :, cols]], [None])
        o_ref[0, :, cols] = o.astype(o_ref.dtype)


def _context_attention(proj_c):
    b, l, _ = proj_c.shape
    w = NA_WIDTH
    return pl.pallas_call(
        _ctx_attn_kernel,
        out_shape=jax.ShapeDtypeStruct((b, l, w), BF16),
        grid=(b,),
        in_specs=[pl.BlockSpec((1, l, w), lambda bi, j=j: (bi, 0, j)) for j in range(3)],
        out_specs=pl.BlockSpec((1, l, w), lambda bi: (bi, 0, 0)),
        compiler_params=_params(("parallel",)),
        name="ctx_attention",
    )(proj_c, proj_c, proj_c)


def _top4(logits):
    lane = lax.broadcasted_iota(jnp.int32, logits.shape, 1)
    cur = jnp.where(lane < N_EXPERTS, logits, -jnp.inf)
    vals, idxs = [], []
    for _ in range(TOP_K):
        m = jnp.max(cur, axis=-1, keepdims=True)
        first = jnp.min(jnp.where(cur == m, lane, ROUTER_PAD).astype(F32), axis=-1, keepdims=True)
        idx = first.astype(jnp.int32)
        vals.append(m)
        idxs.append(idx)
        cur = jnp.where(lane == idx, -jnp.inf, cur)
    exps = [jnp.exp(v - vals[0]) for v in vals]
    inv = 1.0 / functools.reduce(jnp.add, exps)
    ids = jnp.zeros(logits.shape, jnp.int32)
    gates = jnp.zeros(logits.shape, F32)
    for k in range(TOP_K):
        ids = jnp.where(lane == k, idxs[k], ids)
        gates = jnp.where(lane == k, exps[k] * inv, gates)
    return ids, gates


def _layer_tail(y, rows, x_ref, g1_ref, n2_ref, sh2_ref, sc2_ref, wr_ref, br_ref, *rest):
    xo_ref, h2_ref, ids_ref, gates_ref = rest[-4:]
    x_new = x_ref[0, rows] + g1_ref[0] * y
    xo_ref[0, rows] = x_new
    h2 = _rms_mod(x_new, n2_ref[...], sh2_ref[0], sc2_ref[0])
    h2_ref[rows] = _pack_rows(h2)
    logits = _dot(h2.astype(BF16), wr_ref[...]) + br_ref[...]
    ids, gates = _top4(logits)
    ids_ref[rows] = ids
    gates_ref[0, rows] = gates


def _sub_tiles(n_rows):
    sub = min(OUT_SUB_TILE, n_rows)
    return [slice(s, s + sub) for s in range(0, n_rows, sub)]


def _tail_specs(b, t, d, tm, n_inputs_before, tokens, shared):
    n_total, offset = tokens
    n_t = t // tm
    assert offset % tm == 0
    row = lambda bi, i: (bi, i, 0)
    flat = lambda bi, i: (offset // tm + bi * n_t + i, 0)
    per_b = lambda bi, i: (bi, 0, 0)
    const = lambda bi, i: (0, 0)
    extra, aliases = (), {}
    if shared is not None:
        extra = tuple(shared)
        aliases = {n_inputs_before + 7: 1, n_inputs_before + 8: 2}
    in_specs = [
        pl.BlockSpec((1, tm, d), row),
        pl.BlockSpec((1, 1, d), per_b),
        pl.BlockSpec((1, d), const),
        pl.BlockSpec((1, 1, d), per_b),
        pl.BlockSpec((1, 1, d), per_b),
        pl.BlockSpec((d, ROUTER_PAD), const),
        pl.BlockSpec((1, ROUTER_PAD), const),
    ] + [pl.BlockSpec(memory_space=pl.ANY)] * len(extra)
    out_specs = [
        pl.BlockSpec((1, tm, d), row),
        pl.BlockSpec((tm, d // 2), flat),
        pl.BlockSpec((tm, ROUTER_PAD), flat),
        pl.BlockSpec((1, tm, ROUTER_PAD), row),
    ]
    out_shape = [
        jax.ShapeDtypeStruct((b, t, d), F32),
        jax.ShapeDtypeStruct((n_total, d // 2), jnp.int32),
        jax.ShapeDtypeStruct((n_total, ROUTER_PAD), jnp.int32),
        jax.ShapeDtypeStruct((b, t, ROUTER_PAD), F32),
    ]
    return in_specs, out_specs, out_shape, extra, aliases


def _halo_fix(rolled, at_row, halo_row, present):
    n = rolled.shape[0]
    first = at_row < SUBLANES
    assert first or at_row >= n - SUBLANES
    slab = rolled[:SUBLANES] if first else rolled[n - SUBLANES:]
    sub = lax.broadcasted_iota(jnp.int32, slab.shape, 0)
    fill = jnp.where(present, halo_row, jnp.zeros_like(halo_row))
    slab = jnp.where(sub == at_row % SUBLANES, fill, slab)
    return jnp.concatenate([slab, rolled[SUBLANES:]] if first else [rolled[:n - SUBLANES], slab], axis=0)


def _even_out_kernel(oa_ref, bg_ref, cg_ref, xin_ref, cgp_ref, xinp_ref, cgn_ref, xinn_ref, cw_ref, cb_ref,
                     wa_ref, wb_ref, *tail_refs, tm):
    i = pl.program_id(1)
    has_prev = i > 0
    has_next = i < pl.num_programs(1) - 1
    u = cg_ref[0].astype(F32) * xin_ref[0].astype(F32)
    u_prev = (cgp_ref[0].astype(F32) * xinp_ref[0].astype(F32))[HALO - 1:HALO]
    u_next = (cgn_ref[0].astype(F32) * xinn_ref[0].astype(F32))[0:1]
    u_m1 = _halo_fix(pltpu.roll(u, 1, 0), 0, u_prev, has_prev)
    u_p1 = _halo_fix(pltpu.roll(u, tm - 1, 0), tm - 1, u_next, has_next)
    cw = cw_ref[...]
    conv = u_m1 * cw[0:1] + u * cw[1:2] + u_p1 * cw[2:3] + cb_ref[...]
    o_b = (bg_ref[0].astype(F32) * conv).astype(BF16)
    for rows in _sub_tiles(tm):
        y = _dot(oa_ref[0, rows], wa_ref[...]) + _dot(o_b[rows], wb_ref[...])
        _layer_tail(y, rows, *tail_refs)


def _even_out(o_a, proj, conv_w, conv_b, w_out, x, g1, n2, sh2, sc2, w_r, b_r, tokens, shared=None):
    b, t, d = x.shape
    w = NA_WIDTH
    tm = min(t, OUT_TILE)
    hb = tm // HALO
    n_hblocks = t // HALO
    row = lambda bi, i: (bi, i, 0)
    const = lambda bi, i: (0, 0)
    prev = lambda col: (lambda bi, i: (bi, jnp.maximum(i * hb - 1, 0), col))
    nxt = lambda col: (lambda bi, i: (bi, jnp.minimum((i + 1) * hb, n_hblocks - 1), col))
    tail_in, out_specs, out_shape, extra, aliases = _tail_specs(b, t, d, tm, 12, tokens, shared)
    in_specs = [
        pl.BlockSpec((1, tm, w), row),
        pl.BlockSpec((1, tm, w), lambda bi, i: (bi, i, 3)),
        pl.BlockSpec((1, tm, w), lambda bi, i: (bi, i, 4)),
        pl.BlockSpec((1, tm, w), lambda bi, i: (bi, i, 5)),
        pl.BlockSpec((1, HALO, w), prev(4)),
        pl.BlockSpec((1, HALO, w), prev(5)),
        pl.BlockSpec((1, HALO, w), nxt(4)),
        pl.BlockSpec((1, HALO, w), nxt(5)),
        pl.BlockSpec((SC_CONV, w), const),
        pl.BlockSpec((1, w), const),
        pl.BlockSpec((w, d), const),
        pl.BlockSpec((w, d), const),
    ] + tail_in
    return pl.pallas_call(
        functools.partial(_even_out_kernel, tm=tm),
        out_shape=out_shape,
        grid=(b, t // tm),
        in_specs=in_specs,
        out_specs=out_specs,
        input_output_aliases=aliases,
        compiler_params=_params(("parallel", "parallel")),
        name="even_out",
    )(o_a, proj, proj, proj, proj, proj, proj, proj, conv_w, conv_b.reshape(1, w),
      w_out[:w], w_out[w:], x, g1, n2, sh2, sc2, w_r, b_r, *extra)


def _log_sigmoid(x):
    return jnp.minimum(x, 0.0) - jnp.log1p(jnp.exp(-jnp.abs(x)))


def _sigmoid(x):
    return 0.5 + 0.5 * jnp.tanh(0.5 * x)


def _lru_sub_tile(ucb_time, state, perm_ref, w_ref, ba_ref, bx_ref, lam_ref, want_hidden, reverse):
    sub, width = ucb_time.shape
    blk = width // LRU_BLOCKS
    steps = sub // SUBLANES
    uc = _dot(perm_ref[0], ucb_time)
    ucb = uc.astype(BF16)
    za, zx = [], []
    for h in range(LRU_BLOCKS):
        z = _dot(ucb[:, h * blk:(h + 1) * blk], w_ref[h])
        za.append(z[:, :blk])
        zx.append(z[:, blk:])
    r = _sigmoid(jnp.concatenate(za, axis=1) + ba_ref[...])
    gate_i = _sigmoid(jnp.concatenate(zx, axis=1) + bx_ref[...])
    log_a = (LRU_C * _log_sigmoid(lam_ref[...])) * r
    a = jnp.exp(log_a)
    th = jnp.tanh(log_a)
    num = -2.0 * th
    mult = jnp.where(num > 0.0, num * lax.rsqrt(num * (1.0 - th)), 0.0)
    bcoef = mult * gate_i * uc

    grp = lambda v, j: v[j * SUBLANES:(j + 1) * SUBLANES]
    prods, local = [grp(a, 0)], [grp(bcoef, 0)]
    for j in range(1, steps):
        aj = grp(a, j)
        local.append(aj * local[-1] + grp(bcoef, j))
        prods.append(aj * prods[-1])

    carry_in = [None] * SUBLANES
    for s in (range(SUBLANES - 1, -1, -1) if reverse else range(SUBLANES)):
        carry_in[s] = state
        state = prods[-1][s:s + 1] * state + local[-1][s:s + 1]
    if not want_hidden:
        return state, None
    start = jnp.concatenate(carry_in, axis=0)
    hidden = jnp.concatenate([prods[j] * start + local[j] for j in range(steps)], axis=0)
    return state, _dot(perm_ref[1], hidden.astype(BF16)).astype(BF16)


def _lru_tile(u_ref, perm_ref, w_ref, ba_ref, bx_ref, lam_ref, carry_ref, o_ref, *, reverse):
    sub = perm_ref.shape[1]
    n_sub = u_ref.shape[1] // sub
    state = carry_ref[0:1, :]
    for k in (range(n_sub - 1, -1, -1) if reverse else range(n_sub)):
        rows = slice(k * sub, (k + 1) * sub)
        state, hidden = _lru_sub_tile(u_ref[0, rows], state, perm_ref, w_ref, ba_ref, bx_ref, lam_ref,
                                      o_ref is not None, reverse)
        if o_ref is not None:
            o_ref[0, rows] = hidden
    carry_ref[...] = jnp.broadcast_to(state, carry_ref.shape)


def _lru_kernel(uc_ref, ul_ref, perm_ref, w_ref, ba_ref, bx_ref, lam_ref, o_ref, carry_ref, *, n_ctx_tiles, reverse):
    j = pl.program_id(1)
    shared = (perm_ref, w_ref, ba_ref, bx_ref, lam_ref, carry_ref)

    @pl.when(j == 0)
    def _():
        carry_ref[...] = jnp.zeros_like(carry_ref)

    @pl.when(j < n_ctx_tiles)
    def _():
        _lru_tile(uc_ref, *shared, None, reverse=reverse)

    @pl.when(j >= n_ctx_tiles)
    def _():
        _lru_tile(ul_ref, *shared, o_ref, reverse=reverse)


def _lru_scan(proj, u_ctx, w_cat, ba, bx, lam, reverse):
    b, t, _ = proj.shape
    l, width = u_ctx.shape[1], u_ctx.shape[2]
    sub = min(256, l, t)
    tc = sub * min(LRU_SUB_TILES, l // sub)
    tl = sub * min(LRU_SUB_TILES, t // sub)
    n_c, n_l = l // tc, t // tl

    def pos_of(step, n):
        step = jnp.clip(step, 0, n - 1)
        return (n - 1 - step) if reverse else step

    def tile_map(off, n, col):
        return lambda bi, j: (bi, pos_of(j - off, n), col)

    const2 = lambda bi, j: (0, 0)
    in_specs = [
        pl.BlockSpec((1, tc, width), tile_map(0, n_c, 0)),
        pl.BlockSpec((1, tl, width), tile_map(n_c, n_l, 1)),
        pl.BlockSpec((2, sub, sub), lambda bi, j: (0, 0, 0)),
        pl.BlockSpec(w_cat.shape, lambda bi, j: (0, 0, 0)),
        pl.BlockSpec((1, width), const2),
        pl.BlockSpec((1, width), const2),
        pl.BlockSpec((1, width), const2),
    ]
    steps = sub // SUBLANES
    step, block = np.divmod(np.arange(sub), SUBLANES)
    time_of_row = block * steps + (steps - 1 - step if reverse else step)
    perm = (time_of_row[:, None] == np.arange(sub)[None]).astype(np.float32)
    perms = jnp.asarray(np.stack([perm, perm.T]), BF16)
    kern = functools.partial(_lru_kernel, n_ctx_tiles=n_c, reverse=reverse)
    return pl.pallas_call(
        kern,
        out_shape=jax.ShapeDtypeStruct((b, t, width), BF16),
        grid=(b, n_c + n_l),
        in_specs=in_specs,
        out_specs=pl.BlockSpec((1, tl, width), tile_map(n_c, n_l, 0)),
        scratch_shapes=[pltpu.VMEM((SUBLANES, width), F32)],
        compiler_params=_params(("parallel", "arbitrary")),
        name="lru_scan_bwd" if reverse else "lru_scan_fwd",
    )(u_ctx, proj, perms, w_cat, ba.reshape(1, width), bx.reshape(1, width), lam.reshape(1, width))


def _odd_out_kernel(hf_ref, hb_ref, gate_ref, w_ref, *tail_refs):
    for rows in _sub_tiles(hf_ref.shape[1]):
        hsum = hf_ref[0, rows].astype(F32) + hb_ref[0, rows].astype(F32)
        z = hsum * jax.nn.gelu(gate_ref[0, rows].astype(F32), approximate=True)
        y = _dot(z.astype(BF16), w_ref[...])
        _layer_tail(y, rows, *tail_refs)


def _odd_out(h_f, h_b, proj, w_out, x, g1, n2, sh2, sc2, w_r, b_r):
    b, t, d = x.shape
    width = h_f.shape[-1]
    tm = min(t, OUT_TILE)
    row = lambda bi, i: (bi, i, 0)
    tail_in, out_specs, out_shape, _, _ = _tail_specs(b, t, d, tm, 4, (b * t, 0), None)
    in_specs = [
        pl.BlockSpec((1, tm, width), row),
        pl.BlockSpec((1, tm, width), row),
        pl.BlockSpec((1, tm, width), row),
        pl.BlockSpec((width, d), lambda bi, i: (0, 0)),
    ] + tail_in
    return pl.pallas_call(
        _odd_out_kernel,
        out_shape=out_shape,
        grid=(b, t // tm),
        in_specs=in_specs,
        out_specs=out_specs,
        compiler_params=_params(("parallel", "parallel")),
        name="odd_out",
    )(h_f, h_b, proj, w_out, x, g1, n2, sh2, sc2, w_r, b_r)


def _expert_kernel(be_ref, next_ref, rows_ref, nb_ref, x_ref, wg_hbm, bg_ref, wu_hbm, bu_ref, wd_hbm, bd_ref,
                   o_ref, w_f32, wg_bf, wu_bf, wd_bf, h_bf, sem, *, layer, chunk):
    i = pl.program_id(0)
    expert = be_ref[i]
    used = i < nb_ref[0]

    def weight_copies(e):
        return [pltpu.make_async_copy(src.at[layer, e], w_f32.at[j], sem.at[j])
                for j, src in enumerate((wg_hbm, wu_hbm, wd_hbm))]

    @pl.when(i == 0)
    def _():
        for cp in weight_copies(expert):
            cp.start()

    @pl.when(used & ((i == 0) | (expert != be_ref[jnp.maximum(i - 1, 0)])))
    def _():
        for cp in weight_copies(expert):
            cp.wait()
        wg_bf[...] = w_f32[0].astype(BF16)
        wu_bf[...] = w_f32[1].astype(BF16)
        wd_bf[...] = w_f32[2].astype(BF16)

        @pl.when(next_ref[i] >= 0)
        def _():
            for cp in weight_copies(next_ref[i]):
                cp.start()

    sub = h_bf.shape[0]
    n_sub = x_ref.shape[0] // sub
    n_rows = jnp.where(used, rows_ref[i], 0)
    n_live = (n_rows + sub - 1) // sub

    def live_sub_block(s, carry):
        rows = pl.ds(pl.multiple_of(s * sub, sub), sub)
        xp = x_ref[rows]
        row = lax.broadcasted_iota(jnp.int32, xp.shape, 0) + s * sub
        xp = jnp.where(row < n_rows, xp, 0)
        x = jnp.concatenate(_unpack_rows(xp), axis=1).astype(BF16)
        for c in range(wg_bf.shape[1] // chunk):
            cs = slice(c * chunk, (c + 1) * chunk)
            g = jnp.minimum(_dot(x, wg_bf[:, cs]) + bg_ref[0, 0, :, cs], SWIGLU_LIMIT)
            u = jnp.clip(_dot(x, wu_bf[:, cs]) + bu_ref[0, 0, :, cs], -SWIGLU_LIMIT, SWIGLU_LIMIT)
            h_bf[:, cs] = (g * jax.nn.sigmoid(SWIGLU_ALPHA * g) * (u + 1.0)).astype(BF16)
        o_ref[rows] = _pack_rows(_dot(h_bf[...], wd_bf[...]) + bd_ref[0, 0])
        return carry

    def empty_sub_block(s, carry):
        o_ref[pl.ds(pl.multiple_of(s * sub, sub), sub)] = jnp.zeros((sub, o_ref.shape[1]), o_ref.dtype)
        return carry

    lax.fori_loop(0, n_live, live_sub_block, 0)
    lax.fori_loop(n_live, n_sub, empty_sub_block, 0)


def _experts(xb, block_e, next_e, block_rows, n_used, layer, wg, bg, wu, bu, wd, bd):
    n_slots = xb.shape[0]
    depth, n_e, d, d_exp = wg.shape
    assert d == d_exp
    tm = EXPERT_TILE
    n_blocks = n_slots // tm
    xmap = lambda i, be, ne, br, nb: (jnp.minimum(i, nb[0] - 1), 0)
    bmap = lambda i, be, ne, br, nb: (layer, be[i], 0, 0)
    hbm = pl.BlockSpec(memory_space=pl.ANY)
    grid_spec = pltpu.PrefetchScalarGridSpec(
        num_scalar_prefetch=4,
        grid=(n_blocks,),
        in_specs=[
            pl.BlockSpec((tm, d // 2), xmap),
            hbm,
            pl.BlockSpec((1, 1, 1, d_exp), bmap),
            hbm,
            pl.BlockSpec((1, 1, 1, d_exp), bmap),
            hbm,
            pl.BlockSpec((1, 1, 1, d), bmap),
        ],
        out_specs=pl.BlockSpec((tm, d // 2), lambda i, be, ne, br, nb: (i, 0)),
        scratch_shapes=[pltpu.VMEM((3, d, d_exp), F32), pltpu.VMEM((d, d_exp), BF16), pltpu.VMEM((d, d_exp), BF16),
                        pltpu.VMEM((d_exp, d), BF16), pltpu.VMEM((EXPERT_SUB, d_exp), BF16),
                        pltpu.SemaphoreType.DMA((3,))],
    )
    return pl.pallas_call(
        functools.partial(_expert_kernel, layer=layer, chunk=256),
        out_shape=jax.ShapeDtypeStruct((n_slots, d // 2), jnp.int32),
        grid_spec=grid_spec,
        compiler_params=_params(("arbitrary",)),
        name="experts",
    )(block_e, next_e, block_rows, n_used, xb, wg, bg.reshape(depth, n_e, 1, d_exp), wu,
      bu.reshape(depth, n_e, 1, d_exp), wd, bd.reshape(depth, n_e, 1, d))


def _combine_kernel(x_ref, g2_ref, gates_ref, y_ref, o_ref):
    gates = gates_ref[0]
    half = y_ref.shape[-1]
    acc_lo = jnp.zeros((x_ref.shape[1], half), F32)
    acc_hi = jnp.zeros((x_ref.shape[1], half), F32)
    for k in range(TOP_K):
        lo, hi = _unpack_rows(y_ref[k])
        acc_lo = acc_lo + gates[:, k:k + 1] * lo
        acc_hi = acc_hi + gates[:, k:k + 1] * hi
    o_ref[0] = x_ref[0] + g2_ref[0] * jnp.concatenate([acc_lo, acc_hi], axis=1)


def _combine(x, g2, gates, y_sel, tok_offset):
    b, t, d = x.shape
    tm = int(np.gcd(min(t, 512), tok_offset)) if tok_offset else min(t, 512)
    n_t = t // tm
    off = tok_offset // tm
    row = lambda bi, i: (bi, i, 0)
    return pl.pallas_call(
        _combine_kernel,
        out_shape=jax.ShapeDtypeStruct((b, t, d), F32),
        grid=(b, n_t),
        in_specs=[
            pl.BlockSpec((1, tm, d), row),
            pl.BlockSpec((1, 1, d), lambda bi, i: (bi, 0, 0)),
            pl.BlockSpec((1, tm, ROUTER_PAD), row),
            pl.BlockSpec((TOP_K, tm, d // 2), lambda bi, i: (0, off + bi * n_t + i, 0)),
        ],
        out_specs=pl.BlockSpec((1, tm, d), row),
        input_output_aliases={0: 0},
        compiler_params=_params(("parallel", "parallel")),
        name="moe_combine",
    )(x, g2, gates, y_sel)


def _route_kernel(ids_ref, upper_ref, dest_ref, counts_ref, run_ref, *, tile):
    p = pl.program_id(0)
    i = pl.program_id(1)
    tm = ids_ref.shape[0]
    ids_t = ids_ref[...].T
    expert = lax.broadcasted_iota(jnp.int32, (N_EXPERTS, tm), 0)
    chosen = [ids_t[k:k + 1, :] == expert for k in range(TOP_K)]
    picks = functools.reduce(jnp.add, [c.astype(F32) for c in chosen])
    tile_counts = jnp.sum(picks, axis=1, keepdims=True)

    @pl.when((p == 0) & (i == 0))
    def _():
        run_ref[...] = jnp.zeros_like(run_ref)

    @pl.when(p == 0)
    def _():
        run_ref[...] += tile_counts

    @pl.when((p == 1) & (i == 0))
    def _():
        counts = run_ref[...]
        counts_ref[...] = counts.astype(jnp.int32)
        padded = jnp.floor((counts + (tile - 1)) * (1.0 / tile)) * tile
        row = lax.broadcasted_iota(jnp.int32, counts.shape, 0)
        ends = padded
        for s in (1, 2, 4, 8, 16):
            ends = ends + jnp.where(row >= s, pltpu.roll(ends, s, 0), 0.0)
        run_ref[...] = ends - padded

    @pl.when(p == 1)
    def _():
        before = _dot(picks.astype(BF16), upper_ref[...])
        slot = before + run_ref[:, 0:1]
        rows = [jnp.sum(jnp.where(c, slot, 0.0), axis=0, keepdims=True) for c in chosen]
        rows += [jnp.zeros_like(rows[0])] * (dest_ref.shape[0] - TOP_K)
        dest_ref[...] = jnp.concatenate(rows, axis=0).astype(jnp.int32)
        run_ref[...] += tile_counts


def _route(ids):
    n = ids.shape[0]
    tm = int(np.gcd(n, ROUTE_TILE))
    n_tiles = n // tm
    upper = jnp.asarray(np.triu(np.ones((tm, tm), np.float32), 1), BF16)
    dest, counts = pl.pallas_call(
        functools.partial(_route_kernel, tile=EXPERT_TILE),
        out_shape=[jax.ShapeDtypeStruct((SUBLANES, n), jnp.int32),
                   jax.ShapeDtypeStruct((N_EXPERTS, LANES), jnp.int32)],
        grid=(2, n_tiles),
        in_specs=[pl.BlockSpec((tm, ROUTER_PAD), lambda p, i: (i, 0)),
                  pl.BlockSpec((tm, tm), lambda p, i: (0, 0))],
        out_specs=[pl.BlockSpec((SUBLANES, tm), lambda p, i: (0, i * p)),
                   pl.BlockSpec((N_EXPERTS, LANES), lambda p, i: (0, 0))],
        scratch_shapes=[pltpu.VMEM((N_EXPERTS, LANES), F32)],
        compiler_params=_params(("arbitrary", "arbitrary")),
        name="route",
    )(ids, upper)
    return dest, counts[:, 0]


def _row_gather(table, idx):
    info = plsc.get_sparse_core_info()
    n_cores, n_workers = info.num_cores, info.num_cores * info.num_subcores
    n_rows, width = idx.shape[0], table.shape[1]
    chunk_rows = SC_GATHER_ROWS
    per_worker = n_rows // n_workers
    n_chunks = per_worker // chunk_rows
    assert per_worker * n_workers == n_rows and n_chunks * chunk_rows == per_worker and n_chunks % 2 == 0
    mesh = plsc.VectorSubcoreMesh(core_axis_name="c", subcore_axis_name="s")

    @functools.partial(
        pl.kernel, mesh=mesh,
        out_type=jax.ShapeDtypeStruct((n_rows, width), table.dtype),
        scratch_types=[pltpu.VMEM((n_chunks, chunk_rows), jnp.int32), pltpu.VMEM((2, chunk_rows, width), table.dtype),
                       pltpu.SemaphoreType.DMA((2,)), pltpu.SemaphoreType.DMA((2,))],
    )
    def gather_kernel(table_hbm, idx_hbm, out_hbm, idx_v, rows_v, gather_sem, write_sem):
        worker = lax.axis_index("s") * n_cores + lax.axis_index("c")
        pltpu.sync_copy(idx_hbm.at[worker], idx_v)

        def gather(c, slot):
            return pltpu.make_async_copy(table_hbm.at[idx_v.at[c]], rows_v.at[slot], gather_sem.at[slot])

        def write(c, slot):
            first_row = worker * per_worker + c * chunk_rows
            return pltpu.make_async_copy(rows_v.at[slot], out_hbm.at[pl.ds(first_row, chunk_rows)], write_sem.at[slot])

        gather(0, 0).start()

        @pl.loop(0, n_chunks, step=2)
        def _(c0):
            for slot in (0, 1):
                c = c0 + slot
                gather(c, slot).wait()

                @pl.when(c >= 1)
                def _():
                    write(c - 1, 1 - slot).wait()

                @pl.when(c + 1 < n_chunks)
                def _():
                    gather(c + 1, 1 - slot).start()

                write(c, slot).start()

        write(n_chunks - 1, 1).wait()

    return gather_kernel(table, idx.reshape(n_workers, n_chunks, chunk_rows))


def _row_scatter(rows, dest, n_out):
    info = plsc.get_sparse_core_info()
    n_cores, n_workers = info.num_cores, info.num_cores * info.num_subcores
    n_choices, n_rows = dest.shape
    width = rows.shape[1]
    chunk_rows = SC_GATHER_ROWS
    per_worker = n_rows // (n_workers * chunk_rows)
    assert per_worker * n_workers * chunk_rows == n_rows == rows.shape[0]
    idx = dest.reshape(n_choices, n_workers, per_worker, chunk_rows).transpose(1, 0, 2, 3)
    mesh = plsc.VectorSubcoreMesh(core_axis_name="c", subcore_axis_name="s")

    @functools.partial(
        pl.kernel, mesh=mesh,
        out_type=jax.ShapeDtypeStruct((n_out, width), rows.dtype),
        scratch_types=[pltpu.VMEM((n_choices, per_worker, chunk_rows), jnp.int32),
                       pltpu.VMEM((chunk_rows, width), rows.dtype), pltpu.SemaphoreType.DMA],
    )
    def scatter_kernel(rows_hbm, idx_hbm, out_hbm, idx_v, rows_v, sem):
        worker = lax.axis_index("s") * n_cores + lax.axis_index("c")
        pltpu.sync_copy(idx_hbm.at[worker], idx_v)

        @pl.loop(0, per_worker)
        def _(c):
            first_row = (worker * per_worker + c) * chunk_rows
            pltpu.sync_copy(rows_hbm.at[pl.ds(first_row, chunk_rows)], rows_v)
            copies = [pltpu.async_copy(rows_v, out_hbm.at[idx_v.at[k, c]], sem) for k in range(n_choices)]
            for cp in copies:
                cp.wait()

    return scatter_kernel(rows, idx)


def _moe(h2, ids, layer, w_exp):
    n = h2.shape[0]
    nk = n * TOP_K
    tm = EXPERT_TILE
    dest, counts = _route(ids)
    dest = dest[:TOP_K]
    padded = (counts + tm - 1) // tm * tm
    pends = jnp.cumsum(padded)
    n_blocks = -(-nk // tm) + N_EXPERTS
    n_slots = n_blocks * tm
    block_start = jnp.arange(n_blocks, dtype=jnp.int32) * tm
    block_e = jnp.minimum(jnp.sum((pends[None] <= block_start[:, None]).astype(jnp.int32), axis=1), N_EXPERTS - 1)
    is_e = (block_e[:, None] == jnp.arange(N_EXPERTS, dtype=jnp.int32)[None]).astype(jnp.int32)
    block_rows = jnp.clip(jnp.sum(is_e * (pends - padded + counts)[None], axis=1) - block_start, 0, tm)
    n_used = (pends[-1] // tm).astype(jnp.int32).reshape(1)
    e_ids = jnp.arange(N_EXPERTS, dtype=jnp.int32)
    later = (e_ids[None] > e_ids[:, None]) & (counts[None] > 0)
    next_nonempty = jnp.min(jnp.where(later, e_ids[None], N_EXPERTS), axis=1)
    next_nonempty = jnp.where(next_nonempty == N_EXPERTS, -1, next_nonempty)
    next_e = jnp.sum(is_e * next_nonempty[None], axis=1)
    xb = _row_scatter(h2, dest, n_slots)
    return _experts(xb, block_e, next_e, block_rows, n_used, layer, *w_exp), dest


def _gather_choices(yb, dest, lo, hi):
    return _row_gather(yb, dest[:, lo:hi].reshape(-1)).reshape(TOP_K, hi - lo, -1)


def _mod_parts(mod_l, b):
    d = mod_l.shape[-1] // 6
    lat = [mod_l[:b, k * d:(k + 1) * d].reshape(b, 1, d) for k in range(6)]
    ctx = [jnp.broadcast_to(mod_l[b, k * d:(k + 1) * d].reshape(1, 1, d), (b, 1, d)) for k in range(6)]
    return lat, ctx


def kernel(x, c, ctx, c_ctx, ada_w, ada_b, norm1_g, norm2_g, ev_w_in, ev_w_out, ev_q_gain, ev_k_gain, ev_rpb, ev_conv_w, ev_conv_b, od_w_in, od_w_out, od_conv_w, od_conv_b, od_fwd_wa, od_fwd_ba, od_fwd_wx, od_fwd_bx, od_fwd_lam, od_bwd_wa, od_bwd_ba, od_bwd_wx, od_bwd_bx, od_bwd_lam, router_w, router_b, exp_w_gate, exp_b_gate, exp_w_up, exp_b_up, exp_w_down, exp_b_down):
    b, t, d = x.shape
    l = ctx.shape[1]
    assert ada_w.shape[0] == DEPTH == 2 and t % GRID_W == 0 and t // GRID_W >= WIN_H

    n_rows_c = -(-(b + 1) // SUBLANES) * SUBLANES
    cvec = jnp.zeros((n_rows_c, d), F32).at[:b].set(c).at[b].set(c_ctx)
    mod = _ada_mod(cvec, ada_w, ada_b)

    def router(layer):
        w_r = jnp.zeros((d, ROUTER_PAD), F32).at[:, :N_EXPERTS].set(router_w[layer]).astype(BF16)
        b_r = jnp.zeros((1, ROUTER_PAD), F32).at[0, :N_EXPERTS].set(router_b[layer])
        return w_r, b_r

    w_exp = (exp_w_gate, exp_b_gate, exp_w_up, exp_b_up, exp_w_down, exp_b_down)

    (sh1, sc1, g1, sh2, sc2, g2), (csh1, csc1, cg1, csh2, csc2, cg2) = _mod_parts(mod[0], b)
    n1 = norm1_g[0].reshape(1, d)
    n2 = norm2_g[0].reshape(1, d)
    w_in = ev_w_in[0].astype(BF16)
    w_out = ev_w_out[0].astype(BF16)
    q_scale = NA_HEAD_DIM ** -0.5 * LOG2_E
    head_gain = jnp.stack([jnp.tile(ev_q_gain[0] * q_scale, NA_HEADS), jnp.tile(ev_k_gain[0], NA_HEADS)])
    head_gain = head_gain.reshape(2, 1, NA_WIDTH).astype(F32)
    proj = _inproj(x, n1, sh1, sc1, w_in, head_gain)
    proj_c = _inproj(ctx, n1, csh1, csc1, w_in, head_gain)
    kh, r0, row_type, patterns = _na_tables(t // GRID_W)
    bias_tab = _na_bias_table(ev_rpb[0], patterns)
    o_a = _neighbourhood_attention(proj, proj_c, bias_tab, r0, row_type, kh)
    oc_a = _context_attention(proj_c)
    w_r, b_r = router(0)
    n_tok = b * (l + t)
    fresh = (jnp.zeros((n_tok, d // 2), jnp.int32), jnp.zeros((n_tok, ROUTER_PAD), jnp.int32))
    c1, tokens, ids_all, gates_c = _even_out(oc_a, proj_c, ev_conv_w[0], ev_conv_b[0], w_out, ctx, cg1, n2, csh2,
                                             csc2, w_r, b_r, (n_tok, 0), fresh)
    x1, tokens, ids_all, gates = _even_out(o_a, proj, ev_conv_w[0], ev_conv_b[0], w_out, x, g1, n2, sh2, sc2, w_r,
                                           b_r, (n_tok, b * l), (tokens, ids_all))
    yb, dest = _moe(tokens, ids_all, 0, w_exp)
    y_sel = _gather_choices(yb, dest, 0, b * (l + t))
    hctx = _combine(c1, cg2, gates_c, y_sel, 0)
    x = _combine(x1, g2, gates, y_sel, b * l)

    (sh1, sc1, g1, sh2, sc2, g2), (csh1, csc1, _, _, _, _) = _mod_parts(mod[1], b)
    n1 = norm1_g[1].reshape(1, d)
    n2 = norm2_g[1].reshape(1, d)
    w_in = od_w_in[0].astype(BF16)
    width = w_in.shape[1] // 2
    proj = _inproj_conv(x, n1, sh1, sc1, w_in, od_conv_w[0], od_conv_b[0], width)
    u_ctx = _inproj_conv(hctx, n1, csh1, csc1, w_in[:, width:], od_conv_w[0], od_conv_b[0], 0)
    h_dir = []
    for reverse, (wa, ba, wx, bx, lam) in ((False, (od_fwd_wa, od_fwd_ba, od_fwd_wx, od_fwd_bx, od_fwd_lam)),
                                           (True, (od_bwd_wa, od_bwd_ba, od_bwd_wx, od_bwd_bx, od_bwd_lam))):
        w_cat = jnp.concatenate([wa[0], wx[0]], axis=-1).astype(BF16)
        h_dir.append(_lru_scan(proj, u_ctx, w_cat, ba[0], bx[0], lam[0], reverse))
    w_r, b_r = router(1)
    x1, h2, ids, gates = _odd_out(h_dir[0], h_dir[1], proj, od_w_out[0].astype(BF16), x, g1, n2, sh2, sc2, w_r, b_r)
    yb, dest = _moe(h2, ids, 1, w_exp)
    return _combine(x1, g2, gates, _gather_choices(yb, dest, 0, b * t), 0)
```

```python
import functools

import numpy as np
import jax
import jax.numpy as jnp
from jax import lax
from jax.experimental import pallas as pl
from jax.experimental.pallas import tpu as pltpu
from jax.experimental.pallas import tpu_sc as plsc

DEPTH = 2
GRID_W = 64
EPS = 1e-6
NEG_INF = -1e30
LOG2_E = 1.4426950408889634
NA_HEADS = 8
NA_HEAD_DIM = 64
NA_WIDTH = NA_HEADS * NA_HEAD_DIM
HEAD_PAIRS = NA_HEADS // 2
WIN_H = 8
WIN_W = 16
SC_CONV = 3
LRU_BLOCKS = 4
LRU_CONV = 4
LRU_C = 8.0
N_EXPERTS = 32
TOP_K = 4
SWIGLU_LIMIT = 7.0
SWIGLU_ALPHA = 1.702

LANES = 128
SUBLANES = 8
HALO = 16
ROUTER_PAD = LANES
NA_ROWS_PER_STEP = 4
LRU_SUB_TILES = 4
OUT_TILE = 1024
OUT_SUB_TILE = 512
ROUTE_TILE = 2048
EXPERT_TILE = 2048
EXPERT_SUB = 512
SC_GATHER_ROWS = 64
VMEM_LIMIT = 56 * 1024 * 1024

F32 = jnp.float32
BF16 = jnp.bfloat16


def _params(sem, vmem=VMEM_LIMIT):
    return pltpu.CompilerParams(dimension_semantics=sem, vmem_limit_bytes=vmem)


def _dot(a, b):
    return jnp.dot(a, b, preferred_element_type=F32)


def _dot_nt(a, b):
    return lax.dot_general(a, b, (((1,), (1,)), ((), ())), preferred_element_type=F32)


def _pack_rows(v):
    w = v.shape[-1] // 2
    lo = lax.bitcast_convert_type(v[:, :w].astype(BF16).astype(F32), jnp.int32)
    hi = lax.bitcast_convert_type(v[:, w:].astype(BF16).astype(F32), jnp.int32)
    return lax.shift_right_logical(lo, 16) | (hi & jnp.int32(-65536))


def _unpack_rows(p):
    lo = lax.bitcast_convert_type(lax.shift_left(p, 16), F32)
    hi = lax.bitcast_convert_type(p & jnp.int32(-65536), F32)
    return lo, hi


def _rms_mod(x, g, shift, scale):
    ms = jnp.mean(x * x, axis=-1, keepdims=True)
    return (x * lax.rsqrt(ms + EPS)) * (g * (1.0 + scale)) + shift


def _ada_kernel(c_ref, w_ref, b_ref, o_ref):
    c = c_ref[...]
    s = (c * jax.nn.sigmoid(c)).astype(BF16)
    o_ref[0] = _dot(s, w_ref[0].astype(BF16)) + b_ref[0]


def _ada_mod(cvec, ada_w, ada_b):
    depth, d, n = ada_w.shape
    r = cvec.shape[0]
    tn = 1536
    return pl.pallas_call(
        _ada_kernel,
        out_shape=jax.ShapeDtypeStruct((depth, r, n), F32),
        grid=(depth, n // tn),
        in_specs=[
            pl.BlockSpec((r, d), lambda l, j: (0, 0)),
            pl.BlockSpec((1, d, tn), lambda l, j: (l, 0, j)),
            pl.BlockSpec((1, 1, tn), lambda l, j: (l, 0, j)),
        ],
        out_specs=pl.BlockSpec((1, r, tn), lambda l, j: (l, 0, j)),
        compiler_params=_params(("parallel", "parallel")),
        name="ada_mod",
    )(cvec, ada_w, ada_b.reshape(depth, 1, n))


def _inproj_kernel(x_ref, g_ref, sh_ref, sc_ref, w_ref, hg_ref, ones_ref, o_ref, *, n_tiles, tn, n_headnorm):
    h = _rms_mod(x_ref[0], g_ref[...], sh_ref[0], sc_ref[0]).astype(BF16)
    for j in range(n_tiles):
        y = _dot(h, w_ref[:, j * tn:(j + 1) * tn])
        if j < n_headnorm:
            ms = _dot((y * y).astype(BF16), ones_ref[...]) * (1.0 / NA_HEAD_DIM)
            y = y * lax.rsqrt(ms + EPS) * hg_ref[j]
        o_ref[0, :, j * tn:(j + 1) * tn] = y.astype(o_ref.dtype)


def _inproj(x, g, shift, scale, w, head_gain=None):
    b, t, d = x.shape
    n = w.shape[1]
    tn = NA_WIDTH
    tm = min(t, 512)
    n_headnorm = 0 if head_gain is None else head_gain.shape[0]
    if head_gain is None:
        head_gain = jnp.ones((1, 1, tn), F32)
    hid = np.arange(tn) // NA_HEAD_DIM
    ones_bd = jnp.asarray((hid[:, None] == hid[None, :]), BF16)
    kern = functools.partial(_inproj_kernel, n_tiles=n // tn, tn=tn, n_headnorm=n_headnorm)
    return pl.pallas_call(
        kern,
        out_shape=jax.ShapeDtypeStruct((b, t, n), BF16),
        grid=(b, t // tm),
        in_specs=[
            pl.BlockSpec((1, tm, d), lambda bi, i: (bi, i, 0)),
            pl.BlockSpec((1, d), lambda bi, i: (0, 0)),
            pl.BlockSpec((1, 1, d), lambda bi, i: (bi, 0, 0)),
            pl.BlockSpec((1, 1, d), lambda bi, i: (bi, 0, 0)),
            pl.BlockSpec((d, n), lambda bi, i: (0, 0)),
            pl.BlockSpec(head_gain.shape, lambda bi, i: (0, 0, 0)),
            pl.BlockSpec((tn, tn), lambda bi, i: (0, 0)),
        ],
        out_specs=pl.BlockSpec((1, tm, n), lambda bi, i: (bi, i, 0)),
        compiler_params=_params(("parallel", "parallel")),
        name="inproj",
    )(x, g, shift, scale, w, head_gain, ones_bd)


def _moe_residual(x, g2, gates, y):
    acc_lo = acc_hi = 0.0
    for k in range(TOP_K):
        lo, hi = _unpack_rows(y[k])
        acc_lo = acc_lo + gates[:, k:k + 1] * lo
        acc_hi = acc_hi + gates[:, k:k + 1] * hi
    return x + g2 * jnp.concatenate([acc_lo, acc_hi], axis=1)


def _combine_inproj_conv_kernel(x_ref, xp_ref, xn_ref, gt_ref, gtp_ref, gtn_ref, y_ref, yp_ref, yn_ref, g2_ref,
                                g_ref, sh_ref, sc_ref, w_ref, cw_ref, cb_ref, xo_ref, o_ref, *,
                                n_tiles, n_plain, tn, tm, sub):
    i = pl.program_id(1)
    edge = SUBLANES
    g2 = g2_ref[0]
    x_tile = _moe_residual(x_ref[0], g2, gt_ref[0], y_ref[...])
    xo_ref[0] = x_tile
    x_ext = jnp.concatenate([_moe_residual(xp_ref[0], g2, gtp_ref[0], yp_ref[...]), x_tile,
                             _moe_residual(xn_ref[0], g2, gtn_ref[0], yn_ref[...])], axis=0)
    h = _rms_mod(x_ext, g_ref[...], sh_ref[0], sc_ref[0]).astype(BF16)
    first_step = i == 0
    last_step = i == pl.num_programs(1) - 1
    left = (LRU_CONV - 1) // 2
    n_sub = tm // sub
    for s in range(n_sub):
        h_s = h[s * sub:(s + 1) * sub + 2 * edge]
        rows = slice(s * sub, (s + 1) * sub)
        keep_prev = jnp.where(first_step, 0.0, 1.0) if s == 0 else 1.0
        keep_next = jnp.where(last_step, 0.0, 1.0) if s == n_sub - 1 else 1.0
        for j in range(n_tiles):
            y = _dot(h_s, w_ref[:, j * tn:(j + 1) * tn])
            if j < n_plain:
                o_ref[0, rows, j * tn:(j + 1) * tn] = y[edge:edge + sub].astype(o_ref.dtype)
                continue
            u = jnp.concatenate([y[:edge] * keep_prev, y[edge:edge + sub], y[edge + sub:] * keep_next], axis=0)
            cw = cw_ref[:, (j - n_plain) * tn:(j - n_plain + 1) * tn]
            uc = cb_ref[:, (j - n_plain) * tn:(j - n_plain + 1) * tn]
            n_ext = u.shape[0]
            for k in range(LRU_CONV):
                shifted = u if k == left else pltpu.roll(u, (left - k) % n_ext, 0)
                uc = uc + shifted[edge:edge + sub] * cw[k:k + 1]
            o_ref[0, rows, j * tn:(j + 1) * tn] = uc.astype(o_ref.dtype)


def _combine_inproj_conv(x1, g2, gates, y_sel, tok_offset, g, shift, scale, w, conv_w, conv_b, n_plain_cols):
    b, t, d = x1.shape
    n = w.shape[1]
    tn = NA_WIDTH
    tm = min(t, OUT_TILE)
    hb = tm // SUBLANES
    n_hblocks = t // SUBLANES
    y_hblocks = y_sel.shape[1] // SUBLANES
    assert tok_offset % tm == 0
    kern = functools.partial(_combine_inproj_conv_kernel, n_tiles=n // tn, n_plain=n_plain_cols // tn, tn=tn, tm=tm,
                             sub=min(tm, OUT_SUB_TILE))
    const = lambda bi, i: (0, 0)
    tile = lambda bi, i: (bi, i, 0)
    prev = lambda bi, i: (bi, jnp.maximum(i * hb - 1, 0), 0)
    nxt = lambda bi, i: (bi, jnp.minimum((i + 1) * hb, n_hblocks - 1), 0)
    y_row = lambda bi, i: (tok_offset + bi * t + i * tm) // SUBLANES
    per_b = lambda bi, i: (bi, 0, 0)
    return pl.pallas_call(
        kern,
        out_shape=[jax.ShapeDtypeStruct((b, t, d), F32), jax.ShapeDtypeStruct((b, t, n), BF16)],
        grid=(b, t // tm),
        in_specs=[
            pl.BlockSpec((1, tm, d), tile),
            pl.BlockSpec((1, SUBLANES, d), prev),
            pl.BlockSpec((1, SUBLANES, d), nxt),
            pl.BlockSpec((1, tm, ROUTER_PAD), tile),
            pl.BlockSpec((1, SUBLANES, ROUTER_PAD), prev),
            pl.BlockSpec((1, SUBLANES, ROUTER_PAD), nxt),
            pl.BlockSpec((TOP_K, tm, d // 2), lambda bi, i: (0, y_row(bi, i) // hb, 0)),
            pl.BlockSpec((TOP_K, SUBLANES, d // 2), lambda bi, i: (0, jnp.maximum(y_row(bi, i) - 1, 0), 0)),
            pl.BlockSpec((TOP_K, SUBLANES, d // 2),
                         lambda bi, i: (0, jnp.minimum(y_row(bi, i) + hb, y_hblocks - 1), 0)),
            pl.BlockSpec((1, 1, d), per_b),
            pl.BlockSpec((1, d), const),
            pl.BlockSpec((1, 1, d), per_b),
            pl.BlockSpec((1, 1, d), per_b),
            pl.BlockSpec((d, n), const),
            pl.BlockSpec(conv_w.shape, const),
            pl.BlockSpec((1, conv_w.shape[1]), const),
        ],
        out_specs=[pl.BlockSpec((1, tm, d), tile), pl.BlockSpec((1, tm, n), tile)],
        compiler_params=_params(("parallel", "parallel")),
        name="combine_inproj_conv",
    )(x1, x1, x1, gates, gates, gates, y_sel, y_sel, y_sel, g2, g, shift, scale, w, conv_w, conv_b.reshape(1, -1))


def _na_tables(rows):
    kh = min(WIN_H, rows)
    r = np.arange(rows)
    r0 = np.clip(r - kh // 2, 0, rows - kh)
    dr = r0[:, None] + np.arange(kh)[None] - r[:, None] + WIN_H - 1
    patterns, row_type = np.unique(dr, axis=0, return_inverse=True)
    return kh, r0.astype(np.int32), row_type.reshape(-1).astype(np.int32), patterns


def _na_bias_table(rpb, patterns):
    qc = np.arange(GRID_W)
    kc = np.arange(GRID_W)
    c0 = np.clip(qc - WIN_W // 2, 0, GRID_W - WIN_W)[:, None]
    valid = (kc[None] >= c0) & (kc[None] < c0 + WIN_W)
    dc = np.clip(kc[None] - qc[:, None] + WIN_W - 1, 0, 2 * WIN_W - 2)
    n_pat, kh = patterns.shape
    onehot_dc = jnp.asarray(dc[None] == np.arange(2 * WIN_W - 1)[:, None, None], F32)
    tab = jnp.einsum('hpic,cqk->hpiqk', rpb.astype(F32)[:, patterns], onehot_dc,
                     precision=lax.Precision.HIGHEST)
    tab = jnp.where(valid[None, None, None], tab * LOG2_E, NEG_INF)
    tab = tab.reshape(HEAD_PAIRS, 2, n_pat, kh, GRID_W, GRID_W)
    tab = tab.transpose(2, 0, 1, 4, 3, 5)
    return tab.reshape(n_pat, HEAD_PAIRS, 2 * GRID_W, kh * GRID_W)


def _pair_attention(q, keys, values, biases):
    m = q.shape[0]
    qq = _stack_heads(q)
    scores = []
    for k, bias in zip(keys, biases):
        s = _dot_nt(qq, k)
        scores.append(s if bias is None else s + bias)
    s = jnp.concatenate(scores, axis=1)
    e = jnp.exp2(s - jnp.max(s, axis=-1, keepdims=True))
    denom = jnp.sum(e, axis=-1, keepdims=True)
    e = e.astype(BF16)
    o, start = 0.0, 0
    for v in values:
        o = o + _dot(e[:, start:start + v.shape[0]], v)
        start += v.shape[0]
    o = o * (1.0 / denom)
    lane_o = lax.broadcasted_iota(jnp.int32, (m, LANES), 1)
    return jnp.where(lane_o < NA_HEAD_DIM, o[:m], o[m:])


def _stack_heads(q):
    lane = lax.broadcasted_iota(jnp.int32, q.shape, 1)
    zero = jnp.zeros_like(q)
    return jnp.concatenate([jnp.where(lane < NA_HEAD_DIM, q, zero), jnp.where(lane >= NA_HEAD_DIM, q, zero)], axis=0)


def _na_kernel(r0_ref, type_ref, q_ref, k_ref, v_ref, kc_ref, vc_ref, bias_ref, o_ref, s_ref, p_ref, *,
               kh, rows_per_step):
    n_lat = kh * GRID_W
    tiles = [(j, p) for j in range(rows_per_step) for p in range(HEAD_PAIRS)]
    window = []
    for j in range(rows_per_step):
        r = pl.program_id(1) * rows_per_step + j
        window.append((pl.multiple_of(r0_ref[r] * GRID_W, GRID_W), type_ref[r]))

    for idx, (j, p) in enumerate(tiles):
        start, rtype = window[j]
        cols = slice(p * LANES, (p + 1) * LANES)
        qq = _stack_heads(q_ref[0, j * GRID_W:(j + 1) * GRID_W, cols])
        s_ref[idx, :, :n_lat] = _dot_nt(qq, k_ref[0, pl.ds(start, n_lat), cols]) + bias_ref[rtype, p]
        s_ref[idx, :, n_lat:] = _dot_nt(qq, kc_ref[0, :, cols])

    denoms = []
    for idx in range(len(tiles)):
        s = s_ref[idx]
        e = jnp.exp2(s - jnp.max(s, axis=-1, keepdims=True))
        denoms.append(jnp.sum(e, axis=-1, keepdims=True))
        p_ref[idx] = e.astype(BF16)

    lane = lax.broadcasted_iota(jnp.int32, (GRID_W, LANES), 1)
    for idx, (j, p) in enumerate(tiles):
        start, _ = window[j]
        cols = slice(p * LANES, (p + 1) * LANES)
        o = _dot(p_ref[idx, :, :n_lat], v_ref[0, pl.ds(start, n_lat), cols]) + _dot(p_ref[idx, :, n_lat:],
                                                                                 vc_ref[0, :, cols])
        o = o * (1.0 / denoms[idx])
        o = jnp.where(lane < NA_HEAD_DIM, o[:GRID_W], o[GRID_W:])
        o_ref[0, j * GRID_W:(j + 1) * GRID_W, cols] = o.astype(o_ref.dtype)


def _neighbourhood_attention(proj, proj_c, bias_tab, r0, row_type, kh):
    b, t, _ = proj.shape
    l = proj_c.shape[1]
    rows = t // GRID_W
    w = NA_WIDTH
    rps = int(np.gcd(rows, NA_ROWS_PER_STEP))
    q_rows = rps * GRID_W
    grid_spec = pltpu.PrefetchScalarGridSpec(
        num_scalar_prefetch=2,
        grid=(b, rows // rps),
        in_specs=[
            pl.BlockSpec((1, q_rows, w), lambda bi, r, *_: (bi, r, 0)),
            pl.BlockSpec((1, t, w), lambda bi, r, *_: (bi, 0, 1)),
            pl.BlockSpec((1, t, w), lambda bi, r, *_: (bi, 0, 2)),
            pl.BlockSpec((1, l, w), lambda bi, r, *_: (bi, 0, 1)),
            pl.BlockSpec((1, l, w), lambda bi, r, *_: (bi, 0, 2)),
            pl.BlockSpec(bias_tab.shape, lambda bi, r, *_: (0, 0, 0, 0)),
        ],
        out_specs=pl.BlockSpec((1, q_rows, w), lambda bi, r, *_: (bi, r, 0)),
        scratch_shapes=[pltpu.VMEM((rps * HEAD_PAIRS, 2 * GRID_W, kh * GRID_W + l), F32),
                        pltpu.VMEM((rps * HEAD_PAIRS, 2 * GRID_W, kh * GRID_W + l), BF16)],
    )
    return pl.pallas_call(
        functools.partial(_na_kernel, kh=kh, rows_per_step=rps),
        out_shape=jax.ShapeDtypeStruct((b, t, w), BF16),
        grid_spec=grid_spec,
        compiler_params=_params(("parallel", "arbitrary")),
        name="na_attention",
    )(jnp.asarray(r0), jnp.asarray(row_type), proj, proj, proj, proj_c, proj_c, bias_tab)


def _ctx_attn_kernel(q_ref, k_ref, v_ref, o_ref):
    for p in range(HEAD_PAIRS):
        cols = slice(p * LANES, (p + 1) * LANES)
        o = _pair_attention(q_ref[0, :, cols], [k_ref[0, :, cols]], [v_ref[0, :, cols]], [None])
        o_ref[0, :, cols] = o.astype(o_ref.dtype)


def _context_attention(proj_c):
    b, l, _ = proj_c.shape
    w = NA_WIDTH
    return pl.pallas_call(
        _ctx_attn_kernel,
        out_shape=jax.ShapeDtypeStruct((b, l, w), BF16),
        grid=(b,),
        in_specs=[pl.BlockSpec((1, l, w), lambda bi, j=j: (bi, 0, j)) for j in range(3)],
        out_specs=pl.BlockSpec((1, l, w), lambda bi: (bi, 0, 0)),
        compiler_params=_params(("parallel",)),
        name="ctx_attention",
    )(proj_c, proj_c, proj_c)


def _top4(logits):
    lane = lax.broadcasted_iota(jnp.int32, logits.shape, 1)
    cur = jnp.where(lane < N_EXPERTS, logits, -jnp.inf)
    vals, idxs = [], []
    for _ in range(TOP_K):
        m = jnp.max(cur, axis=-1, keepdims=True)
        first = jnp.min(jnp.where(cur == m, lane, ROUTER_PAD).astype(F32), axis=-1, keepdims=True)
        idx = first.astype(jnp.int32)
        vals.append(m)
        idxs.append(idx)
        cur = jnp.where(lane == idx, -jnp.inf, cur)
    exps = [jnp.exp(v - vals[0]) for v in vals]
    inv = 1.0 / functools.reduce(jnp.add, exps)
    ids = jnp.zeros(logits.shape, jnp.int32)
    gates = jnp.zeros(logits.shape, F32)
    for k in range(TOP_K):
        ids = jnp.where(lane == k, idxs[k], ids)
        gates = jnp.where(lane == k, exps[k] * inv, gates)
    return ids, gates


def _layer_tail(y, rows, x_ref, g1_ref, n2_ref, sh2_ref, sc2_ref, wr_ref, br_ref, *rest):
    xo_ref, h2_ref, ids_ref, gates_ref = rest[-4:]
    x_new = x_ref[0, rows] + g1_ref[0] * y
    xo_ref[0, rows] = x_new
    h2 = _rms_mod(x_new, n2_ref[...], sh2_ref[0], sc2_ref[0])
    h2_ref[rows] = _pack_rows(h2)
    logits = _dot(h2.astype(BF16), wr_ref[...]) + br_ref[...]
    ids, gates = _top4(logits)
    ids_ref[rows] = ids
    gates_ref[0, rows] = gates


def _sub_tiles(n_rows):
    sub = min(OUT_SUB_TILE, n_rows)
    return [slice(s, s + sub) for s in range(0, n_rows, sub)]


def _tail_specs(b, t, d, tm, n_inputs_before, tokens, shared):
    n_total, offset = tokens
    n_t = t // tm
    assert offset % tm == 0
    row = lambda bi, i: (bi, i, 0)
    flat = lambda bi, i: (offset // tm + bi * n_t + i, 0)
    per_b = lambda bi, i: (bi, 0, 0)
    const = lambda bi, i: (0, 0)
    extra, aliases = (), {}
    if shared is not None:
        extra = tuple(shared)
        aliases = {n_inputs_before + 7: 1, n_inputs_before + 8: 2}
    in_specs = [
        pl.BlockSpec((1, tm, d), row),
        pl.BlockSpec((1, 1, d), per_b),
        pl.BlockSpec((1, d), const),
        pl.BlockSpec((1, 1, d), per_b),
        pl.BlockSpec((1, 1, d), per_b),
        pl.BlockSpec((d, ROUTER_PAD), const),
        pl.BlockSpec((1, ROUTER_PAD), const),
    ] + [pl.BlockSpec(memory_space=pl.ANY)] * len(extra)
    out_specs = [
        pl.BlockSpec((1, tm, d), row),
        pl.BlockSpec((tm, d // 2), flat),
        pl.BlockSpec((tm, ROUTER_PAD), flat),
        pl.BlockSpec((1, tm, ROUTER_PAD), row),
    ]
    out_shape = [
        jax.ShapeDtypeStruct((b, t, d), F32),
        jax.ShapeDtypeStruct((n_total, d // 2), jnp.int32),
        jax.ShapeDtypeStruct((n_total, ROUTER_PAD), jnp.int32),
        jax.ShapeDtypeStruct((b, t, ROUTER_PAD), F32),
    ]
    return in_specs, out_specs, out_shape, extra, aliases


def _halo_fix(rolled, at_row, halo_row, present):
    n = rolled.shape[0]
    first = at_row < SUBLANES
    assert first or at_row >= n - SUBLANES
    slab = rolled[:SUBLANES] if first else rolled[n - SUBLANES:]
    sub = lax.broadcasted_iota(jnp.int32, slab.shape, 0)
    fill = jnp.where(present, halo_row, jnp.zeros_like(halo_row))
    slab = jnp.where(sub == at_row % SUBLANES, fill, slab)
    return jnp.concatenate([slab, rolled[SUBLANES:]] if first else [rolled[:n - SUBLANES], slab], axis=0)


def _even_out_kernel(oa_ref, bg_ref, cg_ref, xin_ref, cgp_ref, xinp_ref, cgn_ref, xinn_ref, cw_ref, cb_ref,
                     wa_ref, wb_ref, *tail_refs, tm):
    i = pl.program_id(1)
    has_prev = i > 0
    has_next = i < pl.num_programs(1) - 1
    u = cg_ref[0].astype(F32) * xin_ref[0].astype(F32)
    u_prev = (cgp_ref[0].astype(F32) * xinp_ref[0].astype(F32))[HALO - 1:HALO]
    u_next = (cgn_ref[0].astype(F32) * xinn_ref[0].astype(F32))[0:1]
    u_m1 = _halo_fix(pltpu.roll(u, 1, 0), 0, u_prev, has_prev)
    u_p1 = _halo_fix(pltpu.roll(u, tm - 1, 0), tm - 1, u_next, has_next)
    cw = cw_ref[...]
    conv = u_m1 * cw[0:1] + u * cw[1:2] + u_p1 * cw[2:3] + cb_ref[...]
    o_b = (bg_ref[0].astype(F32) * conv).astype(BF16)
    for rows in _sub_tiles(tm):
        y = _dot(oa_ref[0, rows], wa_ref[...]) + _dot(o_b[rows], wb_ref[...])
        _layer_tail(y, rows, *tail_refs)


def _even_out(o_a, proj, conv_w, conv_b, w_out, x, g1, n2, sh2, sc2, w_r, b_r, tokens, shared=None):
    b, t, d = x.shape
    w = NA_WIDTH
    tm = min(t, OUT_TILE)
    hb = tm // HALO
    n_hblocks = t // HALO
    row = lambda bi, i: (bi, i, 0)
    const = lambda bi, i: (0, 0)
    prev = lambda col: (lambda bi, i: (bi, jnp.maximum(i * hb - 1, 0), col))
    nxt = lambda col: (lambda bi, i: (bi, jnp.minimum((i + 1) * hb, n_hblocks - 1), col))
    tail_in, out_specs, out_shape, extra, aliases = _tail_specs(b, t, d, tm, 12, tokens, shared)
    in_specs = [
        pl.BlockSpec((1, tm, w), row),
        pl.BlockSpec((1, tm, w), lambda bi, i: (bi, i, 3)),
        pl.BlockSpec((1, tm, w), lambda bi, i: (bi, i, 4)),
        pl.BlockSpec((1, tm, w), lambda bi, i: (bi, i, 5)),
        pl.BlockSpec((1, HALO, w), prev(4)),
        pl.BlockSpec((1, HALO, w), prev(5)),
        pl.BlockSpec((1, HALO, w), nxt(4)),
        pl.BlockSpec((1, HALO, w), nxt(5)),
        pl.BlockSpec((SC_CONV, w), const),
        pl.BlockSpec((1, w), const),
        pl.BlockSpec((w, d), const),
        pl.BlockSpec((w, d), const),
    ] + tail_in
    return pl.pallas_call(
        functools.partial(_even_out_kernel, tm=tm),
        out_shape=out_shape,
        grid=(b, t // tm),
        in_specs=in_specs,
        out_specs=out_specs,
        input_output_aliases=aliases,
        compiler_params=_params(("parallel", "parallel")),
        name="even_out",
    )(o_a, proj, proj, proj, proj, proj, proj, proj, conv_w, conv_b.reshape(1, w),
      w_out[:w], w_out[w:], x, g1, n2, sh2, sc2, w_r, b_r, *extra)


def _log_sigmoid(x):
    return jnp.minimum(x, 0.0) - jnp.log1p(jnp.exp(-jnp.abs(x)))


def _sigmoid(x):
    return 0.5 + 0.5 * jnp.tanh(0.5 * x)


def _lru_sub_tile(ucb_time, state, perm_ref, w_ref, ba_ref, bx_ref, lam_ref, want_hidden, reverse):
    sub, width = ucb_time.shape
    blk = width // LRU_BLOCKS
    steps = sub // SUBLANES
    uc = _dot(perm_ref[0], ucb_time)
    ucb = uc.astype(BF16)
    za, zx = [], []
    for h in range(LRU_BLOCKS):
        z = _dot(ucb[:, h * blk:(h + 1) * blk], w_ref[h])
        za.append(z[:, :blk])
        zx.append(z[:, blk:])
    r = _sigmoid(jnp.concatenate(za, axis=1) + ba_ref[...])
    gate_i = _sigmoid(jnp.concatenate(zx, axis=1) + bx_ref[...])
    log_a = (LRU_C * _log_sigmoid(lam_ref[...])) * r
    a = jnp.exp(log_a)
    th = jnp.tanh(log_a)
    num = -2.0 * th
    mult = jnp.where(num > 0.0, num * lax.rsqrt(num * (1.0 - th)), 0.0)
    bcoef = mult * gate_i * uc

    grp = lambda v, j: v[j * SUBLANES:(j + 1) * SUBLANES]
    prods, local = [grp(a, 0)], [grp(bcoef, 0)]
    for j in range(1, steps):
        aj = grp(a, j)
        local.append(aj * local[-1] + grp(bcoef, j))
        prods.append(aj * prods[-1])

    carry_in = [None] * SUBLANES
    for s in (range(SUBLANES - 1, -1, -1) if reverse else range(SUBLANES)):
        carry_in[s] = state
        state = prods[-1][s:s + 1] * state + local[-1][s:s + 1]
    if not want_hidden:
        return state, None
    start = jnp.concatenate(carry_in, axis=0)
    hidden = jnp.concatenate([prods[j] * start + local[j] for j in range(steps)], axis=0)
    return state, _dot(perm_ref[1], hidden.astype(BF16)).astype(BF16)


def _lru_tile(u_ref, perm_ref, w_ref, ba_ref, bx_ref, lam_ref, carry_ref, o_ref, *, reverse):
    sub = perm_ref.shape[1]
    n_sub = u_ref.shape[1] // sub
    state = carry_ref[0:1, :]
    for k in (range(n_sub - 1, -1, -1) if reverse else range(n_sub)):
        rows = slice(k * sub, (k + 1) * sub)
        state, hidden = _lru_sub_tile(u_ref[0, rows], state, perm_ref, w_ref, ba_ref, bx_ref, lam_ref,
                                      o_ref is not None, reverse)
        if o_ref is not None:
            o_ref[0, rows] = hidden
    carry_ref[...] = jnp.broadcast_to(state, carry_ref.shape)


def _lru_kernel(uc_ref, ul_ref, perm_ref, w_ref, ba_ref, bx_ref, lam_ref, o_ref, carry_ref, *, n_ctx_tiles, reverse):
    j = pl.program_id(1)
    shared = (perm_ref, w_ref, ba_ref, bx_ref, lam_ref, carry_ref)

    @pl.when(j == 0)
    def _():
        carry_ref[...] = jnp.zeros_like(carry_ref)

    @pl.when(j < n_ctx_tiles)
    def _():
        _lru_tile(uc_ref, *shared, None, reverse=reverse)

    @pl.when(j >= n_ctx_tiles)
    def _():
        _lru_tile(ul_ref, *shared, o_ref, reverse=reverse)


def _lru_scan(proj, u_ctx, w_cat, ba, bx, lam, reverse):
    b, t, _ = proj.shape
    l, width = u_ctx.shape[1], u_ctx.shape[2]
    sub = min(256, l, t)
    tc = sub * min(LRU_SUB_TILES, l // sub)
    tl = sub * min(LRU_SUB_TILES, t // sub)
    n_c, n_l = l // tc, t // tl

    def pos_of(step, n):
        step = jnp.clip(step, 0, n - 1)
        return (n - 1 - step) if reverse else step

    def tile_map(off, n, col):
        return lambda bi, j: (bi, pos_of(j - off, n), col)

    const2 = lambda bi, j: (0, 0)
    in_specs = [
        pl.BlockSpec((1, tc, width), tile_map(0, n_c, 0)),
        pl.BlockSpec((1, tl, width), tile_map(n_c, n_l, 1)),
        pl.BlockSpec((2, sub, sub), lambda bi, j: (0, 0, 0)),
        pl.BlockSpec(w_cat.shape, lambda bi, j: (0, 0, 0)),
        pl.BlockSpec((1, width), const2),
        pl.BlockSpec((1, width), const2),
        pl.BlockSpec((1, width), const2),
    ]
    steps = sub // SUBLANES
    step, block = np.divmod(np.arange(sub), SUBLANES)
    time_of_row = block * steps + (steps - 1 - step if reverse else step)
    perm = (time_of_row[:, None] == np.arange(sub)[None]).astype(np.float32)
    perms = jnp.asarray(np.stack([perm, perm.T]), BF16)
    kern = functools.partial(_lru_kernel, n_ctx_tiles=n_c, reverse=reverse)
    return pl.pallas_call(
        kern,
        out_shape=jax.ShapeDtypeStruct((b, t, width), BF16),
        grid=(b, n_c + n_l),
        in_specs=in_specs,
        out_specs=pl.BlockSpec((1, tl, width), tile_map(n_c, n_l, 0)),
        scratch_shapes=[pltpu.VMEM((SUBLANES, width), F32)],
        compiler_params=_params(("parallel", "arbitrary")),
        name="lru_scan_bwd" if reverse else "lru_scan_fwd",
    )(u_ctx, proj, perms, w_cat, ba.reshape(1, width), bx.reshape(1, width), lam.reshape(1, width))


def _odd_out_kernel(hf_ref, hb_ref, gate_ref, w_ref, *tail_refs):
    for rows in _sub_tiles(hf_ref.shape[1]):
        hsum = hf_ref[0, rows].astype(F32) + hb_ref[0, rows].astype(F32)
        z = hsum * jax.nn.gelu(gate_ref[0, rows].astype(F32), approximate=True)
        y = _dot(z.astype(BF16), w_ref[...])
        _layer_tail(y, rows, *tail_refs)


def _odd_out(h_f, h_b, proj, w_out, x, g1, n2, sh2, sc2, w_r, b_r):
    b, t, d = x.shape
    width = h_f.shape[-1]
    tm = min(t, OUT_TILE)
    row = lambda bi, i: (bi, i, 0)
    tail_in, out_specs, out_shape, _, _ = _tail_specs(b, t, d, tm, 4, (b * t, 0), None)
    in_specs = [
        pl.BlockSpec((1, tm, width), row),
        pl.BlockSpec((1, tm, width), row),
        pl.BlockSpec((1, tm, width), row),
        pl.BlockSpec((width, d), lambda bi, i: (0, 0)),
    ] + tail_in
    return pl.pallas_call(
        _odd_out_kernel,
        out_shape=out_shape,
        grid=(b, t // tm),
        in_specs=in_specs,
        out_specs=out_specs,
        compiler_params=_params(("parallel", "parallel")),
        name="odd_out",
    )(h_f, h_b, proj, w_out, x, g1, n2, sh2, sc2, w_r, b_r)


def _expert_kernel(be_ref, next_ref, rows_ref, nb_ref, x_ref, wg_hbm, bg_ref, wu_hbm, bu_ref, wd_hbm, bd_ref,
                   o_ref, w_f32, wg_bf, wu_bf, wd_bf, h_bf, sem, *, layer, chunk):
    i = pl.program_id(0)
    expert = be_ref[i]
    used = i < nb_ref[0]

    def weight_copies(e):
        return [pltpu.make_async_copy(src.at[layer, e], w_f32.at[j], sem.at[j])
                for j, src in enumerate((wg_hbm, wu_hbm, wd_hbm))]

    @pl.when(i == 0)
    def _():
        for cp in weight_copies(expert):
            cp.start()

    @pl.when(used & ((i == 0) | (expert != be_ref[jnp.maximum(i - 1, 0)])))
    def _():
        for cp in weight_copies(expert):
            cp.wait()
        wg_bf[...] = w_f32[0].astype(BF16)
        wu_bf[...] = w_f32[1].astype(BF16)
        wd_bf[...] = w_f32[2].astype(BF16)

        @pl.when(next_ref[i] >= 0)
        def _():
            for cp in weight_copies(next_ref[i]):
                cp.start()

    sub = h_bf.shape[0]
    n_sub = x_ref.shape[0] // sub
    n_rows = jnp.where(used, rows_ref[i], 0)
    n_live = (n_rows + sub - 1) // sub

    def live_sub_block(s, carry):
        rows = pl.ds(pl.multiple_of(s * sub, sub), sub)
        xp = x_ref[rows]
        row = lax.broadcasted_iota(jnp.int32, xp.shape, 0) + s * sub
        xp = jnp.where(row < n_rows, xp, 0)
        x = jnp.concatenate(_unpack_rows(xp), axis=1).astype(BF16)
        for c in range(wg_bf.shape[1] // chunk):
            cs = slice(c * chunk, (c + 1) * chunk)
            g = jnp.minimum(_dot(x, wg_bf[:, cs]) + bg_ref[0, 0, :, cs], SWIGLU_LIMIT)
            u = jnp.clip(_dot(x, wu_bf[:, cs]) + bu_ref[0, 0, :, cs], -SWIGLU_LIMIT, SWIGLU_LIMIT)
            h_bf[:, cs] = (g * jax.nn.sigmoid(SWIGLU_ALPHA * g) * (u + 1.0)).astype(BF16)
        o_ref[rows] = _pack_rows(_dot(h_bf[...], wd_bf[...]) + bd_ref[0, 0])
        return carry

    def empty_sub_block(s, carry):
        o_ref[pl.ds(pl.multiple_of(s * sub, sub), sub)] = jnp.zeros((sub, o_ref.shape[1]), o_ref.dtype)
        return carry

    lax.fori_loop(0, n_live, live_sub_block, 0)
    lax.fori_loop(n_live, n_sub, empty_sub_block, 0)


def _experts(xb, block_e, next_e, block_rows, n_used, layer, wg, bg, wu, bu, wd, bd):
    n_slots = xb.shape[0]
    depth, n_e, d, d_exp = wg.shape
    assert d == d_exp
    tm = EXPERT_TILE
    n_blocks = n_slots // tm
    xmap = lambda i, be, ne, br, nb: (jnp.minimum(i, nb[0] - 1), 0)
    bmap = lambda i, be, ne, br, nb: (layer, be[i], 0, 0)
    hbm = pl.BlockSpec(memory_space=pl.ANY)
    grid_spec = pltpu.PrefetchScalarGridSpec(
        num_scalar_prefetch=4,
        grid=(n_blocks,),
        in_specs=[
            pl.BlockSpec((tm, d // 2), xmap),
            hbm,
            pl.BlockSpec((1, 1, 1, d_exp), bmap),
            hbm,
            pl.BlockSpec((1, 1, 1, d_exp), bmap),
            hbm,
            pl.BlockSpec((1, 1, 1, d), bmap),
        ],
        out_specs=pl.BlockSpec((tm, d // 2), lambda i, be, ne, br, nb: (i, 0)),
        scratch_shapes=[pltpu.VMEM((3, d, d_exp), F32), pltpu.VMEM((d, d_exp), BF16), pltpu.VMEM((d, d_exp), BF16),
                        pltpu.VMEM((d_exp, d), BF16), pltpu.VMEM((EXPERT_SUB, d_exp), BF16),
                        pltpu.SemaphoreType.DMA((3,))],
    )
    return pl.pallas_call(
        functools.partial(_expert_kernel, layer=layer, chunk=256),
        out_shape=jax.ShapeDtypeStruct((n_slots, d // 2), jnp.int32),
        grid_spec=grid_spec,
        compiler_params=_params(("arbitrary",)),
        name="experts",
    )(block_e, next_e, block_rows, n_used, xb, wg, bg.reshape(depth, n_e, 1, d_exp), wu,
      bu.reshape(depth, n_e, 1, d_exp), wd, bd.reshape(depth, n_e, 1, d))


def _combine_kernel(x_ref, g2_ref, gates_ref, y_ref, o_ref):
    o_ref[0] = _moe_residual(x_ref[0], g2_ref[0], gates_ref[0], y_ref[...])


def _combine(x, g2, gates, y_sel, tok_offset):
    b, t, d = x.shape
    tm = int(np.gcd(min(t, 512), tok_offset)) if tok_offset else min(t, 512)
    n_t = t // tm
    off = tok_offset // tm
    row = lambda bi, i: (bi, i, 0)
    return pl.pallas_call(
        _combine_kernel,
        out_shape=jax.ShapeDtypeStruct((b, t, d), F32),
        grid=(b, n_t),
        in_specs=[
            pl.BlockSpec((1, tm, d), row),
            pl.BlockSpec((1, 1, d), lambda bi, i: (bi, 0, 0)),
            pl.BlockSpec((1, tm, ROUTER_PAD), row),
            pl.BlockSpec((TOP_K, tm, d // 2), lambda bi, i: (0, off + bi * n_t + i, 0)),
        ],
        out_specs=pl.BlockSpec((1, tm, d), row),
        input_output_aliases={0: 0},
        compiler_params=_params(("parallel", "parallel")),
        name="moe_combine",
    )(x, g2, gates, y_sel)


def _route_kernel(ids_ref, upper_ref, dest_ref, counts_ref, run_ref, *, tile):
    p = pl.program_id(0)
    i = pl.program_id(1)
    tm = ids_ref.shape[0]
    ids_t = ids_ref[...].T
    expert = lax.broadcasted_iota(jnp.int32, (N_EXPERTS, tm), 0)
    chosen = [ids_t[k:k + 1, :] == expert for k in range(TOP_K)]
    picks = functools.reduce(jnp.add, [c.astype(F32) for c in chosen])
    tile_counts = jnp.sum(picks, axis=1, keepdims=True)

    @pl.when((p == 0) & (i == 0))
    def _():
        run_ref[...] = jnp.zeros_like(run_ref)

    @pl.when(p == 0)
    def _():
        run_ref[...] += tile_counts

    @pl.when((p == 1) & (i == 0))
    def _():
        counts = run_ref[...]
        counts_ref[...] = counts.astype(jnp.int32)
        padded = jnp.floor((counts + (tile - 1)) * (1.0 / tile)) * tile
        row = lax.broadcasted_iota(jnp.int32, counts.shape, 0)
        ends = padded
        for s in (1, 2, 4, 8, 16):
            ends = ends + jnp.where(row >= s, pltpu.roll(ends, s, 0), 0.0)
        run_ref[...] = ends - padded

    @pl.when(p == 1)
    def _():
        before = _dot(picks.astype(BF16), upper_ref[...])
        slot = before + run_ref[:, 0:1]
        rows = [jnp.sum(jnp.where(c, slot, 0.0), axis=0, keepdims=True) for c in chosen]
        rows += [jnp.zeros_like(rows[0])] * (dest_ref.shape[0] - TOP_K)
        dest_ref[...] = jnp.concatenate(rows, axis=0).astype(jnp.int32)
        run_ref[...] += tile_counts


def _route(ids):
    n = ids.shape[0]
    tm = int(np.gcd(n, ROUTE_TILE))
    n_tiles = n // tm
    upper = jnp.asarray(np.triu(np.ones((tm, tm), np.float32), 1), BF16)
    dest, counts = pl.pallas_call(
        functools.partial(_route_kernel, tile=EXPERT_TILE),
        out_shape=[jax.ShapeDtypeStruct((SUBLANES, n), jnp.int32),
                   jax.ShapeDtypeStruct((N_EXPERTS, LANES), jnp.int32)],
        grid=(2, n_tiles),
        in_specs=[pl.BlockSpec((tm, ROUTER_PAD), lambda p, i: (i, 0)),
                  pl.BlockSpec((tm, tm), lambda p, i: (0, 0))],
        out_specs=[pl.BlockSpec((SUBLANES, tm), lambda p, i: (0, i * p)),
                   pl.BlockSpec((N_EXPERTS, LANES), lambda p, i: (0, 0))],
        scratch_shapes=[pltpu.VMEM((N_EXPERTS, LANES), F32)],
        compiler_params=_params(("arbitrary", "arbitrary")),
        name="route",
    )(ids, upper)
    return dest, counts[:, 0]


def _row_gather(table, idx):
    info = plsc.get_sparse_core_info()
    n_cores, n_workers = info.num_cores, info.num_cores * info.num_subcores
    n_rows, width = idx.shape[0], table.shape[1]
    chunk_rows = SC_GATHER_ROWS
    per_worker = n_rows // n_workers
    n_chunks = per_worker // chunk_rows
    assert per_worker * n_workers == n_rows and n_chunks * chunk_rows == per_worker and n_chunks % 2 == 0
    mesh = plsc.VectorSubcoreMesh(core_axis_name="c", subcore_axis_name="s")

    @functools.partial(
        pl.kernel, mesh=mesh,
        out_type=jax.ShapeDtypeStruct((n_rows, width), table.dtype),
        scratch_types=[pltpu.VMEM((n_chunks, chunk_rows), jnp.int32), pltpu.VMEM((2, chunk_rows, width), table.dtype),
                       pltpu.SemaphoreType.DMA((2,)), pltpu.SemaphoreType.DMA((2,))],
    )
    def gather_kernel(table_hbm, idx_hbm, out_hbm, idx_v, rows_v, gather_sem, write_sem):
        worker = lax.axis_index("s") * n_cores + lax.axis_index("c")
        pltpu.sync_copy(idx_hbm.at[worker], idx_v)

        def gather(c, slot):
            return pltpu.make_async_copy(table_hbm.at[idx_v.at[c]], rows_v.at[slot], gather_sem.at[slot])

        def write(c, slot):
            first_row = worker * per_worker + c * chunk_rows
            return pltpu.make_async_copy(rows_v.at[slot], out_hbm.at[pl.ds(first_row, chunk_rows)], write_sem.at[slot])

        gather(0, 0).start()

        @pl.loop(0, n_chunks, step=2)
        def _(c0):
            for slot in (0, 1):
                c = c0 + slot
                gather(c, slot).wait()

                @pl.when(c >= 1)
                def _():
                    write(c - 1, 1 - slot).wait()

                @pl.when(c + 1 < n_chunks)
                def _():
                    gather(c + 1, 1 - slot).start()

                write(c, slot).start()

        write(n_chunks - 1, 1).wait()

    return gather_kernel(table, idx.reshape(n_workers, n_chunks, chunk_rows))


def _row_scatter(rows, dest, n_out):
    info = plsc.get_sparse_core_info()
    n_cores, n_workers = info.num_cores, info.num_cores * info.num_subcores
    n_choices, n_rows = dest.shape
    width = rows.shape[1]
    chunk_rows = SC_GATHER_ROWS
    per_worker = n_rows // (n_workers * chunk_rows)
    assert per_worker * n_workers * chunk_rows == n_rows == rows.shape[0]
    idx = dest.reshape(n_choices, n_workers, per_worker, chunk_rows).transpose(1, 0, 2, 3)
    mesh = plsc.VectorSubcoreMesh(core_axis_name="c", subcore_axis_name="s")

    @functools.partial(
        pl.kernel, mesh=mesh,
        out_type=jax.ShapeDtypeStruct((n_out, width), rows.dtype),
        scratch_types=[pltpu.VMEM((n_choices, per_worker, chunk_rows), jnp.int32),
                       pltpu.VMEM((chunk_rows, width), rows.dtype), pltpu.SemaphoreType.DMA],
    )
    def scatter_kernel(rows_hbm, idx_hbm, out_hbm, idx_v, rows_v, sem):
        worker = lax.axis_index("s") * n_cores + lax.axis_index("c")
        pltpu.sync_copy(idx_hbm.at[worker], idx_v)

        @pl.loop(0, per_worker)
        def _(c):
            first_row = (worker * per_worker + c) * chunk_rows
            pltpu.sync_copy(rows_hbm.at[pl.ds(first_row, chunk_rows)], rows_v)
            copies = [pltpu.async_copy(rows_v, out_hbm.at[idx_v.at[k, c]], sem) for k in range(n_choices)]
            for cp in copies:
                cp.wait()

    return scatter_kernel(rows, idx)


def _moe(h2, ids, layer, w_exp):
    n = h2.shape[0]
    nk = n * TOP_K
    tm = EXPERT_TILE
    dest, counts = _route(ids)
    dest = dest[:TOP_K]
    padded = (counts + tm - 1) // tm * tm
    pends = jnp.cumsum(padded)
    n_blocks = -(-nk // tm) + N_EXPERTS
    n_slots = n_blocks * tm
    block_start = jnp.arange(n_blocks, dtype=jnp.int32) * tm
    block_e = jnp.minimum(jnp.sum((pends[None] <= block_start[:, None]).astype(jnp.int32), axis=1), N_EXPERTS - 1)
    is_e = (block_e[:, None] == jnp.arange(N_EXPERTS, dtype=jnp.int32)[None]).astype(jnp.int32)
    block_rows = jnp.clip(jnp.sum(is_e * (pends - padded + counts)[None], axis=1) - block_start, 0, tm)
    n_used = (pends[-1] // tm).astype(jnp.int32).reshape(1)
    e_ids = jnp.arange(N_EXPERTS, dtype=jnp.int32)
    later = (e_ids[None] > e_ids[:, None]) & (counts[None] > 0)
    next_nonempty = jnp.min(jnp.where(later, e_ids[None], N_EXPERTS), axis=1)
    next_nonempty = jnp.where(next_nonempty == N_EXPERTS, -1, next_nonempty)
    next_e = jnp.sum(is_e * next_nonempty[None], axis=1)
    xb = _row_scatter(h2, dest, n_slots)
    return _experts(xb, block_e, next_e, block_rows, n_used, layer, *w_exp), dest


def _gather_choices(yb, dest, lo, hi):
    return _row_gather(yb, dest[:, lo:hi].reshape(-1)).reshape(TOP_K, hi - lo, -1)


def _mod_parts(mod_l, b):
    d = mod_l.shape[-1] // 6
    lat = [mod_l[:b, k * d:(k + 1) * d].reshape(b, 1, d) for k in range(6)]
    ctx = [jnp.broadcast_to(mod_l[b, k * d:(k + 1) * d].reshape(1, 1, d), (b, 1, d)) for k in range(6)]
    return lat, ctx


def kernel(x, c, ctx, c_ctx, ada_w, ada_b, norm1_g, norm2_g, ev_w_in, ev_w_out, ev_q_gain, ev_k_gain, ev_rpb, ev_conv_w, ev_conv_b, od_w_in, od_w_out, od_conv_w, od_conv_b, od_fwd_wa, od_fwd_ba, od_fwd_wx, od_fwd_bx, od_fwd_lam, od_bwd_wa, od_bwd_ba, od_bwd_wx, od_bwd_bx, od_bwd_lam, router_w, router_b, exp_w_gate, exp_b_gate, exp_w_up, exp_b_up, exp_w_down, exp_b_down):
    b, t, d = x.shape
    l = ctx.shape[1]
    assert ada_w.shape[0] == DEPTH == 2 and t % GRID_W == 0 and t // GRID_W >= WIN_H

    n_rows_c = -(-(b + 1) // SUBLANES) * SUBLANES
    cvec = jnp.zeros((n_rows_c, d), F32).at[:b].set(c).at[b].set(c_ctx)
    mod = _ada_mod(cvec, ada_w, ada_b)

    def router(layer):
        w_r = jnp.zeros((d, ROUTER_PAD), F32).at[:, :N_EXPERTS].set(router_w[layer]).astype(BF16)
        b_r = jnp.zeros((1, ROUTER_PAD), F32).at[0, :N_EXPERTS].set(router_b[layer])
        return w_r, b_r

    w_exp = (exp_w_gate, exp_b_gate, exp_w_up, exp_b_up, exp_w_down, exp_b_down)

    (sh1, sc1, g1, sh2, sc2, g2), (csh1, csc1, cg1, csh2, csc2, cg2) = _mod_parts(mod[0], b)
    n1 = norm1_g[0].reshape(1, d)
    n2 = norm2_g[0].reshape(1, d)
    w_in = ev_w_in[0].astype(BF16)
    w_out = ev_w_out[0].astype(BF16)
    q_scale = NA_HEAD_DIM ** -0.5 * LOG2_E
    head_gain = jnp.stack([jnp.tile(ev_q_gain[0] * q_scale, NA_HEADS), jnp.tile(ev_k_gain[0], NA_HEADS)])
    head_gain = head_gain.reshape(2, 1, NA_WIDTH).astype(F32)
    proj = _inproj(x, n1, sh1, sc1, w_in, head_gain)
    proj_c = _inproj(ctx, n1, csh1, csc1, w_in, head_gain)
    kh, r0, row_type, patterns = _na_tables(t // GRID_W)
    bias_tab = _na_bias_table(ev_rpb[0], patterns)
    o_a = _neighbourhood_attention(proj, proj_c, bias_tab, r0, row_type, kh)
    oc_a = _context_attention(proj_c)
    w_r, b_r = router(0)
    n_tok = b * (l + t)
    fresh = (jnp.zeros((n_tok, d // 2), jnp.int32), jnp.zeros((n_tok, ROUTER_PAD), jnp.int32))
    c1, tokens, ids_all, gates_c = _even_out(oc_a, proj_c, ev_conv_w[0], ev_conv_b[0], w_out, ctx, cg1, n2, csh2,
                                             csc2, w_r, b_r, (n_tok, 0), fresh)
    x1, tokens, ids_all, gates = _even_out(o_a, proj, ev_conv_w[0], ev_conv_b[0], w_out, x, g1, n2, sh2, sc2, w_r,
                                           b_r, (n_tok, b * l), (tokens, ids_all))
    yb, dest = _moe(tokens, ids_all, 0, w_exp)
    y_sel = _gather_choices(yb, dest, 0, b * (l + t))
    g2_prev, cg2_prev = g2, cg2

    (sh1, sc1, g1, sh2, sc2, g2), (csh1, csc1, _, _, _, _) = _mod_parts(mod[1], b)
    n1 = norm1_g[1].reshape(1, d)
    n2 = norm2_g[1].reshape(1, d)
    w_in = od_w_in[0].astype(BF16)
    width = w_in.shape[1] // 2
    x, proj = _combine_inproj_conv(x1, g2_prev, gates, y_sel, b * l, n1, sh1, sc1, w_in, od_conv_w[0], od_conv_b[0],
                                   width)
    _, u_ctx = _combine_inproj_conv(c1, cg2_prev, gates_c, y_sel, 0, n1, csh1, csc1, w_in[:, width:], od_conv_w[0],
                                    od_conv_b[0], 0)
    h_dir = []
    for reverse, (wa, ba, wx, bx, lam) in ((False, (od_fwd_wa, od_fwd_ba, od_fwd_wx, od_fwd_bx, od_fwd_lam)),
                                           (True, (od_bwd_wa, od_bwd_ba, od_bwd_wx, od_bwd_bx, od_bwd_lam))):
        w_cat = jnp.concatenate([wa[0], wx[0]], axis=-1).astype(BF16)
        h_dir.append(_lru_scan(proj, u_ctx, w_cat, ba[0], bx[0], lam[0], reverse))
    w_r, b_r = router(1)
    x1, h2, ids, gates = _odd_out(h_dir[0], h_dir[1], proj, od_w_out[0].astype(BF16), x, g1, n2, sh2, sc2, w_r, b_r)
    yb, dest = _moe(h2, ids, 1, w_exp)
    return _combine(x1, g2, gates, _gather_choices(yb, dest, 0, b * t), 0)
```

```python
import functools

import numpy as np
import jax
import jax.numpy as jnp
from jax import lax
from jax.experimental import pallas as pl
from jax.experimental.pallas import tpu as pltpu
from jax.experimental.pallas import tpu_sc as plsc

DEPTH = 2
GRID_W = 64
EPS = 1e-6
NEG_INF = -1e30
LOG2_E = 1.4426950408889634
NA_HEADS = 8
NA_HEAD_DIM = 64
NA_WIDTH = NA_HEADS * NA_HEAD_DIM
HEAD_PAIRS = NA_HEADS // 2
WIN_H = 8
WIN_W = 16
SC_CONV = 3
LRU_BLOCKS = 4
LRU_CONV = 4
LRU_C = 8.0
N_EXPERTS = 32
TOP_K = 4
SWIGLU_LIMIT = 7.0
SWIGLU_ALPHA = 1.702

LANES = 128
SUBLANES = 8
MXU_DEPTH = 256
HALO = 16
ROUTER_PAD = LANES
NA_ROWS_PER_STEP = 4
LRU_SUB_TILES = 4
OUT_TILE = 1024
OUT_SUB_TILE = 512
ROUTE_TILE = 2048
EXPERT_TILE = 2048
EXPERT_SUB = 512
SC_GATHER_ROWS = 64
VMEM_LIMIT = 56 * 1024 * 1024

F32 = jnp.float32
BF16 = jnp.bfloat16


def _params(sem, vmem=VMEM_LIMIT):
    return pltpu.CompilerParams(dimension_semantics=sem, vmem_limit_bytes=vmem)


def _dot(a, b):
    return jnp.dot(a, b, preferred_element_type=F32)


def _dot_nt(a, b):
    return lax.dot_general(a, b, (((1,), (1,)), ((), ())), preferred_element_type=F32)


def _pack_rows(v):
    w = v.shape[-1] // 2
    lo = lax.bitcast_convert_type(v[:, :w].astype(BF16).astype(F32), jnp.int32)
    hi = lax.bitcast_convert_type(v[:, w:].astype(BF16).astype(F32), jnp.int32)
    return lax.shift_right_logical(lo, 16) | (hi & jnp.int32(-65536))


def _unpack_rows(p):
    lo = lax.bitcast_convert_type(lax.shift_left(p, 16), F32)
    hi = lax.bitcast_convert_type(p & jnp.int32(-65536), F32)
    return lo, hi


def _rms_mod(x, g, shift, scale):
    ms = jnp.mean(x * x, axis=-1, keepdims=True)
    return (x * lax.rsqrt(ms + EPS)) * (g * (1.0 + scale)) + shift


def _ada_kernel(c_ref, w_ref, b_ref, o_ref):
    c = c_ref[...]
    s = (c * jax.nn.sigmoid(c)).astype(BF16)
    o_ref[0] = _dot(s, w_ref[0].astype(BF16)) + b_ref[0]


def _ada_mod(cvec, ada_w, ada_b):
    depth, d, n = ada_w.shape
    r = cvec.shape[0]
    tn = 1536
    return pl.pallas_call(
        _ada_kernel,
        out_shape=jax.ShapeDtypeStruct((depth, r, n), F32),
        grid=(depth, n // tn),
        in_specs=[
            pl.BlockSpec((r, d), lambda l, j: (0, 0)),
            pl.BlockSpec((1, d, tn), lambda l, j: (l, 0, j)),
            pl.BlockSpec((1, 1, tn), lambda l, j: (l, 0, j)),
        ],
        out_specs=pl.BlockSpec((1, r, tn), lambda l, j: (l, 0, j)),
        compiler_params=_params(("parallel", "parallel")),
        name="ada_mod",
    )(cvec, ada_w, ada_b.reshape(depth, 1, n))


def _inproj_kernel(x_ref, g_ref, sh_ref, sc_ref, w_ref, hg_ref, ones_ref, o_ref, *, n_tiles, tn, n_headnorm):
    h = _rms_mod(x_ref[0], g_ref[...], sh_ref[0], sc_ref[0]).astype(BF16)
    for j in range(n_tiles):
        y = _dot(h, w_ref[:, j * tn:(j + 1) * tn])
        if j < n_headnorm:
            ysq = (y * y).astype(BF16)
            kw = ones_ref.shape[0]
            ms = jnp.concatenate([_dot(ysq[:, c:c + kw], ones_ref[...]) for c in range(0, tn, kw)], axis=1)
            y = y * lax.rsqrt(ms * (1.0 / NA_HEAD_DIM) + EPS) * hg_ref[j]
        o_ref[0, :, j * tn:(j + 1) * tn] = y.astype(o_ref.dtype)


def _inproj(x, g, shift, scale, w, head_gain):
    b, t, d = x.shape
    n = w.shape[1]
    tn = NA_WIDTH
    tm = min(t, 512)
    n_headnorm = head_gain.shape[0]
    hid = np.arange(MXU_DEPTH) // NA_HEAD_DIM
    ones_bd = jnp.asarray((hid[:, None] == hid[None, :]), BF16)
    kern = functools.partial(_inproj_kernel, n_tiles=n // tn, tn=tn, n_headnorm=n_headnorm)
    return pl.pallas_call(
        kern,
        out_shape=jax.ShapeDtypeStruct((b, t, n), BF16),
        grid=(b, t // tm),
        in_specs=[
            pl.BlockSpec((1, tm, d), lambda bi, i: (bi, i, 0)),
            pl.BlockSpec((1, d), lambda bi, i: (0, 0)),
            pl.BlockSpec((1, 1, d), lambda bi, i: (bi, 0, 0)),
            pl.BlockSpec((1, 1, d), lambda bi, i: (bi, 0, 0)),
            pl.BlockSpec((d, n), lambda bi, i: (0, 0)),
            pl.BlockSpec(head_gain.shape, lambda bi, i: (0, 0, 0)),
            pl.BlockSpec((MXU_DEPTH, MXU_DEPTH), lambda bi, i: (0, 0)),
        ],
        out_specs=pl.BlockSpec((1, tm, n), lambda bi, i: (bi, i, 0)),
        compiler_params=_params(("parallel", "parallel")),
        name="inproj",
    )(x, g, shift, scale, w, head_gain, ones_bd)


def _moe_residual(x, g2, gates, y):
    acc_lo = acc_hi = 0.0
    for k in range(TOP_K):
        lo, hi = _unpack_rows(y[k])
        acc_lo = acc_lo + gates[:, k:k + 1] * lo
        acc_hi = acc_hi + gates[:, k:k + 1] * hi
    return x + g2 * jnp.concatenate([acc_lo, acc_hi], axis=1)


def _combine_inproj_conv_kernel(x_ref, xp_ref, xn_ref, gt_ref, gtp_ref, gtn_ref, y_ref, yp_ref, yn_ref, g2_ref,
                                g_ref, sh_ref, sc_ref, w_ref, cw_ref, cb_ref, xo_ref, o_ref, *,
                                n_tiles, n_plain, tn, tm, sub):
    i = pl.program_id(1)
    edge = SUBLANES
    g2 = g2_ref[0]
    x_tile = _moe_residual(x_ref[0], g2, gt_ref[0], y_ref[...])
    xo_ref[0] = x_tile
    x_ext = jnp.concatenate([_moe_residual(xp_ref[0], g2, gtp_ref[0], yp_ref[...]), x_tile,
                             _moe_residual(xn_ref[0], g2, gtn_ref[0], yn_ref[...])], axis=0)
    h = _rms_mod(x_ext, g_ref[...], sh_ref[0], sc_ref[0]).astype(BF16)
    first_step = i == 0
    last_step = i == pl.num_programs(1) - 1
    left = (LRU_CONV - 1) // 2
    n_sub = tm // sub
    for s in range(n_sub):
        h_s = h[s * sub:(s + 1) * sub + 2 * edge]
        rows = slice(s * sub, (s + 1) * sub)
        keep_prev = jnp.where(first_step, 0.0, 1.0) if s == 0 else 1.0
        keep_next = jnp.where(last_step, 0.0, 1.0) if s == n_sub - 1 else 1.0
        for j in range(n_tiles):
            y = _dot(h_s, w_ref[:, j * tn:(j + 1) * tn])
            if j < n_plain:
                o_ref[0, rows, j * tn:(j + 1) * tn] = y[edge:edge + sub].astype(o_ref.dtype)
                continue
            u = jnp.concatenate([y[:edge] * keep_prev, y[edge:edge + sub], y[edge + sub:] * keep_next], axis=0)
            cw = cw_ref[:, (j - n_plain) * tn:(j - n_plain + 1) * tn]
            uc = cb_ref[:, (j - n_plain) * tn:(j - n_plain + 1) * tn]
            n_ext = u.shape[0]
            for k in range(LRU_CONV):
                shifted = u if k == left else pltpu.roll(u, (left - k) % n_ext, 0)
                uc = uc + shifted[edge:edge + sub] * cw[k:k + 1]
            o_ref[0, rows, j * tn:(j + 1) * tn] = uc.astype(o_ref.dtype)


def _combine_inproj_conv(x1, g2, gates, y_sel, tok_offset, g, shift, scale, w, conv_w, conv_b, n_plain_cols):
    b, t, d = x1.shape
    n = w.shape[1]
    tn = NA_WIDTH
    tm = min(t, OUT_TILE)
    hb = tm // SUBLANES
    n_hblocks = t // SUBLANES
    y_hblocks = y_sel.shape[1] // SUBLANES
    assert tok_offset % tm == 0
    kern = functools.partial(_combine_inproj_conv_kernel, n_tiles=n // tn, n_plain=n_plain_cols // tn, tn=tn, tm=tm,
                             sub=min(tm, OUT_SUB_TILE))
    const = lambda bi, i: (0, 0)
    tile = lambda bi, i: (bi, i, 0)
    prev = lambda bi, i: (bi, jnp.maximum(i * hb - 1, 0), 0)
    nxt = lambda bi, i: (bi, jnp.minimum((i + 1) * hb, n_hblocks - 1), 0)
    y_row = lambda bi, i: (tok_offset + bi * t + i * tm) // SUBLANES
    per_b = lambda bi, i: (bi, 0, 0)
    return pl.pallas_call(
        kern,
        out_shape=[jax.ShapeDtypeStruct((b, t, d), F32), jax.ShapeDtypeStruct((b, t, n), BF16)],
        grid=(b, t // tm),
        in_specs=[
            pl.BlockSpec((1, tm, d), tile),
            pl.BlockSpec((1, SUBLANES, d), prev),
            pl.BlockSpec((1, SUBLANES, d), nxt),
            pl.BlockSpec((1, tm, ROUTER_PAD), tile),
            pl.BlockSpec((1, SUBLANES, ROUTER_PAD), prev),
            pl.BlockSpec((1, SUBLANES, ROUTER_PAD), nxt),
            pl.BlockSpec((TOP_K, tm, d // 2), lambda bi, i: (0, y_row(bi, i) // hb, 0)),
            pl.BlockSpec((TOP_K, SUBLANES, d // 2), lambda bi, i: (0, jnp.maximum(y_row(bi, i) - 1, 0), 0)),
            pl.BlockSpec((TOP_K, SUBLANES, d // 2),
                         lambda bi, i: (0, jnp.minimum(y_row(bi, i) + hb, y_hblocks - 1), 0)),
            pl.BlockSpec((1, 1, d), per_b),
            pl.BlockSpec((1, d), const),
            pl.BlockSpec((1, 1, d), per_b),
            pl.BlockSpec((1, 1, d), per_b),
            pl.BlockSpec((d, n), const),
            pl.BlockSpec(conv_w.shape, const),
            pl.BlockSpec((1, conv_w.shape[1]), const),
        ],
        out_specs=[pl.BlockSpec((1, tm, d), tile), pl.BlockSpec((1, tm, n), tile)],
        compiler_params=_params(("parallel", "parallel")),
        name="combine_inproj_conv",
    )(x1, x1, x1, gates, gates, gates, y_sel, y_sel, y_sel, g2, g, shift, scale, w, conv_w, conv_b.reshape(1, -1))


def _na_tables(rows):
    kh = min(WIN_H, rows)
    r = np.arange(rows)
    r0 = np.clip(r - kh // 2, 0, rows - kh)
    dr = r0[:, None] + np.arange(kh)[None] - r[:, None] + WIN_H - 1
    patterns, row_type = np.unique(dr, axis=0, return_inverse=True)
    return kh, r0.astype(np.int32), row_type.reshape(-1).astype(np.int32), patterns


def _na_bias_table(rpb, patterns):
    qc = np.arange(GRID_W)
    kc = np.arange(GRID_W)
    c0 = np.clip(qc - WIN_W // 2, 0, GRID_W - WIN_W)[:, None]
    valid = (kc[None] >= c0) & (kc[None] < c0 + WIN_W)
    dc = np.clip(kc[None] - qc[:, None] + WIN_W - 1, 0, 2 * WIN_W - 2)
    n_pat, kh = patterns.shape
    onehot_dc = jnp.asarray(dc[None] == np.arange(2 * WIN_W - 1)[:, None, None], F32)
    tab = jnp.einsum('hpic,cqk->hpiqk', rpb.astype(F32)[:, patterns], onehot_dc,
                     precision=lax.Precision.HIGHEST)
    tab = jnp.where(valid[None, None, None], tab * LOG2_E, NEG_INF)
    tab = tab.reshape(HEAD_PAIRS, 2, n_pat, kh, GRID_W, GRID_W)
    tab = tab.transpose(2, 0, 1, 4, 3, 5)
    return tab.reshape(n_pat, HEAD_PAIRS, 2 * GRID_W, kh * GRID_W)


def _pair_attention(q, keys, values, biases):
    m = q.shape[0]
    qq = _stack_heads(q)
    scores = []
    for k, bias in zip(keys, biases):
        s = _dot_nt(qq, k)
        scores.append(s if bias is None else s + bias)
    s = jnp.concatenate(scores, axis=1)
    e = jnp.exp2(s - jnp.max(s, axis=-1, keepdims=True))
    denom = jnp.sum(e, axis=-1, keepdims=True)
    e = e.astype(BF16)
    o, start = 0.0, 0
    for v in values:
        o = o + _dot(e[:, start:start + v.shape[0]], v)
        start += v.shape[0]
    o = o * (1.0 / denom)
    lane_o = lax.broadcasted_iota(jnp.int32, (m, LANES), 1)
    return jnp.where(lane_o < NA_HEAD_DIM, o[:m], o[m:])


def _stack_heads(q):
    lane = lax.broadcasted_iota(jnp.int32, q.shape, 1)
    zero = jnp.zeros_like(q)
    return jnp.concatenate([jnp.where(lane < NA_HEAD_DIM, q, zero), jnp.where(lane >= NA_HEAD_DIM, q, zero)], axis=0)


def _na_kernel(r0_ref, type_ref, q_ref, k_ref, v_ref, kc_ref, vc_ref, bias_ref, o_ref, s_ref, p_ref, *,
               kh, rows_per_step):
    n_lat = kh * GRID_W
    tiles = [(j, p) for j in range(rows_per_step) for p in range(HEAD_PAIRS)]
    window = []
    for j in range(rows_per_step):
        r = pl.program_id(1) * rows_per_step + j
        window.append((pl.multiple_of(r0_ref[r] * GRID_W, GRID_W), type_ref[r]))

    for idx, (j, p) in enumerate(tiles):
        start, rtype = window[j]
        cols = slice(p * LANES, (p + 1) * LANES)
        qq = _stack_heads(q_ref[0, j * GRID_W:(j + 1) * GRID_W, cols])
        s_ref[idx, :, :n_lat] = _dot_nt(qq, k_ref[0, pl.ds(start, n_lat), cols]) + bias_ref[rtype, p]
        s_ref[idx, :, n_lat:] = _dot_nt(qq, kc_ref[0, :, cols])

    denoms = []
    for idx in range(len(tiles)):
        s = s_ref[idx]
        e = jnp.exp2(s - jnp.max(s, axis=-1, keepdims=True))
        denoms.append(jnp.sum(e, axis=-1, keepdims=True))
        p_ref[idx] = e.astype(BF16)

    lane = lax.broadcasted_iota(jnp.int32, (GRID_W, LANES), 1)
    for idx, (j, p) in enumerate(tiles):
        start, _ = window[j]
        cols = slice(p * LANES, (p + 1) * LANES)
        o = _dot(p_ref[idx, :, :n_lat], v_ref[0, pl.ds(start, n_lat), cols]) + _dot(p_ref[idx, :, n_lat:],
                                                                                 vc_ref[0, :, cols])
        o = o * (1.0 / denoms[idx])
        o = jnp.where(lane < NA_HEAD_DIM, o[:GRID_W], o[GRID_W:])
        o_ref[0, j * GRID_W:(j + 1) * GRID_W, cols] = o.astype(o_ref.dtype)


def _neighbourhood_attention(proj, proj_c, bias_tab, r0, row_type, kh):
    b, t, _ = proj.shape
    l = proj_c.shape[1]
    rows = t // GRID_W
    w = NA_WIDTH
    rps = int(np.gcd(rows, NA_ROWS_PER_STEP))
    q_rows = rps * GRID_W
    grid_spec = pltpu.PrefetchScalarGridSpec(
        num_scalar_prefetch=2,
        grid=(b, rows // rps),
        in_specs=[
            pl.BlockSpec((1, q_rows, w), lambda bi, r, *_: (bi, r, 0)),
            pl.BlockSpec((1, t, w), lambda bi, r, *_: (bi, 0, 1)),
            pl.BlockSpec((1, t, w), lambda bi, r, *_: (bi, 0, 2)),
            pl.BlockSpec((1, l, w), lambda bi, r, *_: (bi, 0, 1)),
            pl.BlockSpec((1, l, w), lambda bi, r, *_: (bi, 0, 2)),
            pl.BlockSpec(bias_tab.shape, lambda bi, r, *_: (0, 0, 0, 0)),
        ],
        out_specs=pl.BlockSpec((1, q_rows, w), lambda bi, r, *_: (bi, r, 0)),
        scratch_shapes=[pltpu.VMEM((rps * HEAD_PAIRS, 2 * GRID_W, kh * GRID_W + l), F32),
                        pltpu.VMEM((rps * HEAD_PAIRS, 2 * GRID_W, kh * GRID_W + l), BF16)],
    )
    return pl.pallas_call(
        functools.partial(_na_kernel, kh=kh, rows_per_step=rps),
        out_shape=jax.ShapeDtypeStruct((b, t, w), BF16),
        grid_spec=grid_spec,
        compiler_params=_params(("parallel", "arbitrary")),
        name="na_attention",
    )(jnp.asarray(r0), jnp.asarray(row_type), proj, proj, proj, proj_c, proj_c, bias_tab)


def _ctx_attn_kernel(q_ref, k_ref, v_ref, o_ref):
    for p in range(HEAD_PAIRS):
        cols = slice(p * LANES, (p + 1) * LANES)
        o = _pair_attention(q_ref[0, :, cols], [k_ref[0, :, cols]], [v_ref[0, :, cols]], [None])
        o_ref[0, :, cols] = o.astype(o_ref.dtype)


def _context_attention(proj_c):
    b, l, _ = proj_c.shape
    w = NA_WIDTH
    return pl.pallas_call(
        _ctx_attn_kernel,
        out_shape=jax.ShapeDtypeStruct((b, l, w), BF16),
        grid=(b,),
        in_specs=[pl.BlockSpec((1, l, w), lambda bi, j=j: (bi, 0, j)) for j in range(3)],
        out_specs=pl.BlockSpec((1, l, w), lambda bi: (bi, 0, 0)),
        compiler_params=_params(("parallel",)),
        name="ctx_attention",
    )(proj_c, proj_c, proj_c)


def _top4(logits):
    lane = lax.broadcasted_iota(jnp.int32, logits.shape, 1)
    cur = jnp.where(lane < N_EXPERTS, logits, -jnp.inf)
    vals, idxs = [], []
    for _ in range(TOP_K):
        m = jnp.max(cur, axis=-1, keepdims=True)
        first = jnp.min(jnp.where(cur == m, lane, ROUTER_PAD).astype(F32), axis=-1, keepdims=True)
        idx = first.astype(jnp.int32)
        vals.append(m)
        idxs.append(idx)
        cur = jnp.where(lane == idx, -jnp.inf, cur)
    exps = [jnp.exp(v - vals[0]) for v in vals]
    inv = 1.0 / functools.reduce(jnp.add, exps)
    ids = jnp.zeros(logits.shape, jnp.int32)
    gates = jnp.zeros(logits.shape, F32)
    for k in range(TOP_K):
        ids = jnp.where(lane == k, idxs[k], ids)
        gates = jnp.where(lane == k, exps[k] * inv, gates)
    return ids, gates


def _layer_tail(y, rows, x_ref, g1_ref, n2_ref, sh2_ref, sc2_ref, wr_ref, br_ref, *rest):
    xo_ref, h2_ref, ids_ref, gates_ref = rest[-4:]
    x_new = x_ref[0, rows] + g1_ref[0] * y
    xo_ref[0, rows] = x_new
    h2 = _rms_mod(x_new, n2_ref[...], sh2_ref[0], sc2_ref[0])
    h2_ref[rows] = _pack_rows(h2)
    logits = _dot(h2.astype(BF16), wr_ref[...]) + br_ref[...]
    ids, gates = _top4(logits)
    ids_ref[rows] = ids
    gates_ref[0, rows] = gates


def _sub_tiles(n_rows):
    sub = min(OUT_SUB_TILE, n_rows)
    return [slice(s, s + sub) for s in range(0, n_rows, sub)]


def _tail_specs(b, t, d, tm, n_inputs_before, tokens, shared):
    n_total, offset = tokens
    n_t = t // tm
    assert offset % tm == 0
    row = lambda bi, i: (bi, i, 0)
    flat = lambda bi, i: (offset // tm + bi * n_t + i, 0)
    per_b = lambda bi, i: (bi, 0, 0)
    const = lambda bi, i: (0, 0)
    extra, aliases = (), {}
    if shared is not None:
        extra = tuple(shared)
        aliases = {n_inputs_before + 7: 1, n_inputs_before + 8: 2}
    in_specs = [
        pl.BlockSpec((1, tm, d), row),
        pl.BlockSpec((1, 1, d), per_b),
        pl.BlockSpec((1, d), const),
        pl.BlockSpec((1, 1, d), per_b),
        pl.BlockSpec((1, 1, d), per_b),
        pl.BlockSpec((d, ROUTER_PAD), const),
        pl.BlockSpec((1, ROUTER_PAD), const),
    ] + [pl.BlockSpec(memory_space=pl.ANY)] * len(extra)
    out_specs = [
        pl.BlockSpec((1, tm, d), row),
        pl.BlockSpec((tm, d // 2), flat),
        pl.BlockSpec((tm, ROUTER_PAD), flat),
        pl.BlockSpec((1, tm, ROUTER_PAD), row),
    ]
    out_shape = [
        jax.ShapeDtypeStruct((b, t, d), F32),
        jax.ShapeDtypeStruct((n_total, d // 2), jnp.int32),
        jax.ShapeDtypeStruct((n_total, ROUTER_PAD), jnp.int32),
        jax.ShapeDtypeStruct((b, t, ROUTER_PAD), F32),
    ]
    return in_specs, out_specs, out_shape, extra, aliases


def _halo_fix(rolled, at_row, halo_row, present):
    n = rolled.shape[0]
    first = at_row < SUBLANES
    assert first or at_row >= n - SUBLANES
    slab = rolled[:SUBLANES] if first else rolled[n - SUBLANES:]
    sub = lax.broadcasted_iota(jnp.int32, slab.shape, 0)
    fill = jnp.where(present, halo_row, jnp.zeros_like(halo_row))
    slab = jnp.where(sub == at_row % SUBLANES, fill, slab)
    return jnp.concatenate([slab, rolled[SUBLANES:]] if first else [rolled[:n - SUBLANES], slab], axis=0)


def _even_out_kernel(oa_ref, bg_ref, cg_ref, xin_ref, cgp_ref, xinp_ref, cgn_ref, xinn_ref, cw_ref, cb_ref,
                     wa_ref, wb_ref, *tail_refs, tm):
    i = pl.program_id(1)
    has_prev = i > 0
    has_next = i < pl.num_programs(1) - 1
    u = cg_ref[0].astype(F32) * xin_ref[0].astype(F32)
    u_prev = (cgp_ref[0].astype(F32) * xinp_ref[0].astype(F32))[HALO - 1:HALO]
    u_next = (cgn_ref[0].astype(F32) * xinn_ref[0].astype(F32))[0:1]
    u_m1 = _halo_fix(pltpu.roll(u, 1, 0), 0, u_prev, has_prev)
    u_p1 = _halo_fix(pltpu.roll(u, tm - 1, 0), tm - 1, u_next, has_next)
    cw = cw_ref[...]
    conv = u_m1 * cw[0:1] + u * cw[1:2] + u_p1 * cw[2:3] + cb_ref[...]
    o_b = (bg_ref[0].astype(F32) * conv).astype(BF16)
    for rows in _sub_tiles(tm):
        y = _dot(oa_ref[0, rows], wa_ref[...]) + _dot(o_b[rows], wb_ref[...])
        _layer_tail(y, rows, *tail_refs)


def _even_out(o_a, proj, conv_w, conv_b, w_out, x, g1, n2, sh2, sc2, w_r, b_r, tokens, shared=None):
    b, t, d = x.shape
    w = NA_WIDTH
    tm = min(t, OUT_TILE)
    hb = tm // HALO
    n_hblocks = t // HALO
    row = lambda bi, i: (bi, i, 0)
    const = lambda bi, i: (0, 0)
    prev = lambda col: (lambda bi, i: (bi, jnp.maximum(i * hb - 1, 0), col))
    nxt = lambda col: (lambda bi, i: (bi, jnp.minimum((i + 1) * hb, n_hblocks - 1), col))
    tail_in, out_specs, out_shape, extra, aliases = _tail_specs(b, t, d, tm, 12, tokens, shared)
    in_specs = [
        pl.BlockSpec((1, tm, w), row),
        pl.BlockSpec((1, tm, w), lambda bi, i: (bi, i, 3)),
        pl.BlockSpec((1, tm, w), lambda bi, i: (bi, i, 4)),
        pl.BlockSpec((1, tm, w), lambda bi, i: (bi, i, 5)),
        pl.BlockSpec((1, HALO, w), prev(4)),
        pl.BlockSpec((1, HALO, w), prev(5)),
        pl.BlockSpec((1, HALO, w), nxt(4)),
        pl.BlockSpec((1, HALO, w), nxt(5)),
        pl.BlockSpec((SC_CONV, w), const),
        pl.BlockSpec((1, w), const),
        pl.BlockSpec((w, d), const),
        pl.BlockSpec((w, d), const),
    ] + tail_in
    return pl.pallas_call(
        functools.partial(_even_out_kernel, tm=tm),
        out_shape=out_shape,
        grid=(b, t // tm),
        in_specs=in_specs,
        out_specs=out_specs,
        input_output_aliases=aliases,
        compiler_params=_params(("parallel", "parallel")),
        name="even_out",
    )(o_a, proj, proj, proj, proj, proj, proj, proj, conv_w, conv_b.reshape(1, w),
      w_out[:w], w_out[w:], x, g1, n2, sh2, sc2, w_r, b_r, *extra)


def _log_sigmoid(x):
    return jnp.minimum(x, 0.0) - jnp.log1p(jnp.exp(-jnp.abs(x)))


def _sigmoid(x):
    return 0.5 + 0.5 * jnp.tanh(0.5 * x)


def _lru_sub_tile(ucb_time, state, perm_ref, w_ref, ba_ref, bx_ref, lam_ref, want_hidden, reverse):
    sub, width = ucb_time.shape
    blk = width // LRU_BLOCKS
    steps = sub // SUBLANES
    uc = _dot(perm_ref[0], ucb_time)
    ucb = uc.astype(BF16)
    za, zx = [], []
    for h in range(LRU_BLOCKS):
        z = _dot(ucb[:, h * blk:(h + 1) * blk], w_ref[h])
        za.append(z[:, :blk])
        zx.append(z[:, blk:])
    r = _sigmoid(jnp.concatenate(za, axis=1) + ba_ref[...])
    gate_i = _sigmoid(jnp.concatenate(zx, axis=1) + bx_ref[...])
    log_a = (LRU_C * _log_sigmoid(lam_ref[...])) * r
    a = jnp.exp(log_a)
    th = jnp.tanh(log_a)
    num = -2.0 * th
    mult = jnp.where(num > 0.0, num * lax.rsqrt(num * (1.0 - th)), 0.0)
    bcoef = mult * gate_i * uc

    grp = lambda v, j: v[j * SUBLANES:(j + 1) * SUBLANES]
    prods, local = [grp(a, 0)], [grp(bcoef, 0)]
    for j in range(1, steps):
        aj = grp(a, j)
        local.append(aj * local[-1] + grp(bcoef, j))
        prods.append(aj * prods[-1])

    carry_in = [None] * SUBLANES
    for s in (range(SUBLANES - 1, -1, -1) if reverse else range(SUBLANES)):
        carry_in[s] = state
        state = prods[-1][s:s + 1] * state + local[-1][s:s + 1]
    if not want_hidden:
        return state, None
    start = jnp.concatenate(carry_in, axis=0)
    hidden = jnp.concatenate([prods[j] * start + local[j] for j in range(steps)], axis=0)
    return state, _dot(perm_ref[1], hidden.astype(BF16)).astype(BF16)


def _lru_tile(u_ref, perm_ref, w_ref, ba_ref, bx_ref, lam_ref, carry_ref, o_ref, *, reverse):
    sub = perm_ref.shape[1]
    n_sub = u_ref.shape[1] // sub
    state = carry_ref[0:1, :]
    for k in (range(n_sub - 1, -1, -1) if reverse else range(n_sub)):
        rows = slice(k * sub, (k + 1) * sub)
        state, hidden = _lru_sub_tile(u_ref[0, rows], state, perm_ref, w_ref, ba_ref, bx_ref, lam_ref,
                                      o_ref is not None, reverse)
        if o_ref is not None:
            o_ref[0, rows] = hidden
    carry_ref[...] = jnp.broadcast_to(state, carry_ref.shape)


def _lru_kernel(uc_ref, ul_ref, perm_ref, w_ref, ba_ref, bx_ref, lam_ref, o_ref, carry_ref, *, n_ctx_tiles, reverse):
    j = pl.program_id(1)
    shared = (perm_ref, w_ref, ba_ref, bx_ref, lam_ref, carry_ref)

    @pl.when(j == 0)
    def _():
        carry_ref[...] = jnp.zeros_like(carry_ref)

    @pl.when(j < n_ctx_tiles)
    def _():
        _lru_tile(uc_ref, *shared, None, reverse=reverse)

    @pl.when(j >= n_ctx_tiles)
    def _():
        _lru_tile(ul_ref, *shared, o_ref, reverse=reverse)


def _lru_scan(proj, u_ctx, w_cat, ba, bx, lam, reverse):
    b, t, _ = proj.shape
    l, width = u_ctx.shape[1], u_ctx.shape[2]
    sub = min(256, l, t)
    tc = sub * min(LRU_SUB_TILES, l // sub)
    tl = sub * min(LRU_SUB_TILES, t // sub)
    n_c, n_l = l // tc, t // tl

    def pos_of(step, n):
        step = jnp.clip(step, 0, n - 1)
        return (n - 1 - step) if reverse else step

    def tile_map(off, n, col):
        return lambda bi, j: (bi, pos_of(j - off, n), col)

    const2 = lambda bi, j: (0, 0)
    in_specs = [
        pl.BlockSpec((1, tc, width), tile_map(0, n_c, 0)),
        pl.BlockSpec((1, tl, width), tile_map(n_c, n_l, 1)),
        pl.BlockSpec((2, sub, sub), lambda bi, j: (0, 0, 0)),
        pl.BlockSpec(w_cat.shape, lambda bi, j: (0, 0, 0)),
        pl.BlockSpec((1, width), const2),
        pl.BlockSpec((1, width), const2),
        pl.BlockSpec((1, width), const2),
    ]
    steps = sub // SUBLANES
    step, block = np.divmod(np.arange(sub), SUBLANES)
    time_of_row = block * steps + (steps - 1 - step if reverse else step)
    perm = (time_of_row[:, None] == np.arange(sub)[None]).astype(np.float32)
    perms = jnp.asarray(np.stack([perm, perm.T]), BF16)
    kern = functools.partial(_lru_kernel, n_ctx_tiles=n_c, reverse=reverse)
    return pl.pallas_call(
        kern,
        out_shape=jax.ShapeDtypeStruct((b, t, width), BF16),
        grid=(b, n_c + n_l),
        in_specs=in_specs,
        out_specs=pl.BlockSpec((1, tl, width), tile_map(n_c, n_l, 0)),
        scratch_shapes=[pltpu.VMEM((SUBLANES, width), F32)],
        compiler_params=_params(("parallel", "arbitrary")),
        name="lru_scan_bwd" if reverse else "lru_scan_fwd",
    )(u_ctx, proj, perms, w_cat, ba.reshape(1, width), bx.reshape(1, width), lam.reshape(1, width))


def _odd_out_kernel(hf_ref, hb_ref, gate_ref, w_ref, *tail_refs):
    for rows in _sub_tiles(hf_ref.shape[1]):
        hsum = hf_ref[0, rows].astype(F32) + hb_ref[0, rows].astype(F32)
        z = hsum * jax.nn.gelu(gate_ref[0, rows].astype(F32), approximate=True)
        y = _dot(z.astype(BF16), w_ref[...])
        _layer_tail(y, rows, *tail_refs)


def _odd_out(h_f, h_b, proj, w_out, x, g1, n2, sh2, sc2, w_r, b_r):
    b, t, d = x.shape
    width = h_f.shape[-1]
    tm = min(t, OUT_TILE)
    row = lambda bi, i: (bi, i, 0)
    tail_in, out_specs, out_shape, _, _ = _tail_specs(b, t, d, tm, 4, (b * t, 0), None)
    in_specs = [
        pl.BlockSpec((1, tm, width), row),
        pl.BlockSpec((1, tm, width), row),
        pl.BlockSpec((1, tm, width), row),
        pl.BlockSpec((width, d), lambda bi, i: (0, 0)),
    ] + tail_in
    return pl.pallas_call(
        _odd_out_kernel,
        out_shape=out_shape,
        grid=(b, t // tm),
        in_specs=in_specs,
        out_specs=out_specs,
        compiler_params=_params(("parallel", "parallel")),
        name="odd_out",
    )(h_f, h_b, proj, w_out, x, g1, n2, sh2, sc2, w_r, b_r)


def _expert_kernel(be_ref, next_ref, rows_ref, nb_ref, x_ref, wg_hbm, bg_ref, wu_hbm, bu_ref, wd_hbm, bd_ref,
                   o_ref, w_f32, wg_bf, wu_bf, wd_bf, h_bf, sem, *, layer, chunk):
    i = pl.program_id(0)
    expert = be_ref[i]
    used = i < nb_ref[0]

    def weight_copies(e):
        return [pltpu.make_async_copy(src.at[layer, e], w_f32.at[j], sem.at[j])
                for j, src in enumerate((wg_hbm, wu_hbm, wd_hbm))]

    @pl.when(i == 0)
    def _():
        for cp in weight_copies(expert):
            cp.start()

    @pl.when(used & ((i == 0) | (expert != be_ref[jnp.maximum(i - 1, 0)])))
    def _():
        for cp in weight_copies(expert):
            cp.wait()
        wg_bf[...] = w_f32[0].astype(BF16)
        wu_bf[...] = w_f32[1].astype(BF16)
        wd_bf[...] = w_f32[2].astype(BF16)

        @pl.when(next_ref[i] >= 0)
        def _():
            for cp in weight_copies(next_ref[i]):
                cp.start()

    sub = h_bf.shape[0]
    n_sub = x_ref.shape[0] // sub
    n_rows = jnp.where(used, rows_ref[i], 0)
    n_live = (n_rows + sub - 1) // sub

    def live_sub_block(s, carry):
        rows = pl.ds(pl.multiple_of(s * sub, sub), sub)
        xp = x_ref[rows]
        row = lax.broadcasted_iota(jnp.int32, xp.shape, 0) + s * sub
        xp = jnp.where(row < n_rows, xp, 0)
        x = jnp.concatenate(_unpack_rows(xp), axis=1).astype(BF16)
        for c in range(wg_bf.shape[1] // chunk):
            cs = slice(c * chunk, (c + 1) * chunk)
            g = jnp.minimum(_dot(x, wg_bf[:, cs]) + bg_ref[0, 0, :, cs], SWIGLU_LIMIT)
            u = jnp.clip(_dot(x, wu_bf[:, cs]) + bu_ref[0, 0, :, cs], -SWIGLU_LIMIT, SWIGLU_LIMIT)
            h_bf[:, cs] = (g * jax.nn.sigmoid(SWIGLU_ALPHA * g) * (u + 1.0)).astype(BF16)
        o_ref[rows] = _pack_rows(_dot(h_bf[...], wd_bf[...]) + bd_ref[0, 0])
        return carry

    def empty_sub_block(s, carry):
        o_ref[pl.ds(pl.multiple_of(s * sub, sub), sub)] = jnp.zeros((sub, o_ref.shape[1]), o_ref.dtype)
        return carry

    lax.fori_loop(0, n_live, live_sub_block, 0)
    lax.fori_loop(n_live, n_sub, empty_sub_block, 0)


def _experts(xb, block_e, next_e, block_rows, n_used, layer, wg, bg, wu, bu, wd, bd):
    n_slots = xb.shape[0]
    depth, n_e, d, d_exp = wg.shape
    assert d == d_exp
    tm = EXPERT_TILE
    n_blocks = n_slots // tm
    xmap = lambda i, be, ne, br, nb: (jnp.minimum(i, nb[0] - 1), 0)
    bmap = lambda i, be, ne, br, nb: (layer, be[i], 0, 0)
    hbm = pl.BlockSpec(memory_space=pl.ANY)
    grid_spec = pltpu.PrefetchScalarGridSpec(
        num_scalar_prefetch=4,
        grid=(n_blocks,),
        in_specs=[
            pl.BlockSpec((tm, d // 2), xmap),
            hbm,
            pl.BlockSpec((1, 1, 1, d_exp), bmap),
            hbm,
            pl.BlockSpec((1, 1, 1, d_exp), bmap),
            hbm,
            pl.BlockSpec((1, 1, 1, d), bmap),
        ],
        out_specs=pl.BlockSpec((tm, d // 2), lambda i, be, ne, br, nb: (i, 0)),
        scratch_shapes=[pltpu.VMEM((3, d, d_exp), F32), pltpu.VMEM((d, d_exp), BF16), pltpu.VMEM((d, d_exp), BF16),
                        pltpu.VMEM((d_exp, d), BF16), pltpu.VMEM((EXPERT_SUB, d_exp), BF16),
                        pltpu.SemaphoreType.DMA((3,))],
    )
    return pl.pallas_call(
        functools.partial(_expert_kernel, layer=layer, chunk=256),
        out_shape=jax.ShapeDtypeStruct((n_slots, d // 2), jnp.int32),
        grid_spec=grid_spec,
        compiler_params=_params(("arbitrary",)),
        name="experts",
    )(block_e, next_e, block_rows, n_used, xb, wg, bg.reshape(depth, n_e, 1, d_exp), wu,
      bu.reshape(depth, n_e, 1, d_exp), wd, bd.reshape(depth, n_e, 1, d))


def _combine_kernel(x_ref, g2_ref, gates_ref, y_ref, o_ref):
    o_ref[0] = _moe_residual(x_ref[0], g2_ref[0], gates_ref[0], y_ref[...])


def _combine(x, g2, gates, y_sel, tok_offset):
    b, t, d = x.shape
    tm = int(np.gcd(min(t, 512), tok_offset)) if tok_offset else min(t, 512)
    n_t = t // tm
    off = tok_offset // tm
    row = lambda bi, i: (bi, i, 0)
    return pl.pallas_call(
        _combine_kernel,
        out_shape=jax.ShapeDtypeStruct((b, t, d), F32),
        grid=(b, n_t),
        in_specs=[
            pl.BlockSpec((1, tm, d), row),
            pl.BlockSpec((1, 1, d), lambda bi, i: (bi, 0, 0)),
            pl.BlockSpec((1, tm, ROUTER_PAD), row),
            pl.BlockSpec((TOP_K, tm, d // 2), lambda bi, i: (0, off + bi * n_t + i, 0)),
        ],
        out_specs=pl.BlockSpec((1, tm, d), row),
        input_output_aliases={0: 0},
        compiler_params=_params(("parallel", "parallel")),
        name="moe_combine",
    )(x, g2, gates, y_sel)


def _route_kernel(ids_ref, upper_ref, dest_ref, counts_ref, run_ref, *, tile):
    p = pl.program_id(0)
    i = pl.program_id(1)
    tm = ids_ref.shape[0]
    ids_t = ids_ref[...].T
    expert = lax.broadcasted_iota(jnp.int32, (N_EXPERTS, tm), 0)
    chosen = [ids_t[k:k + 1, :] == expert for k in range(TOP_K)]
    picks = functools.reduce(jnp.add, [c.astype(F32) for c in chosen])
    tile_counts = jnp.sum(picks, axis=1, keepdims=True)

    @pl.when((p == 0) & (i == 0))
    def _():
        run_ref[...] = jnp.zeros_like(run_ref)

    @pl.when(p == 0)
    def _():
        run_ref[...] += tile_counts

    @pl.when((p == 1) & (i == 0))
    def _():
        counts = run_ref[...]
        counts_ref[...] = counts.astype(jnp.int32)
        padded = jnp.floor((counts + (tile - 1)) * (1.0 / tile)) * tile
        row = lax.broadcasted_iota(jnp.int32, counts.shape, 0)
        ends = padded
        for s in (1, 2, 4, 8, 16):
            ends = ends + jnp.where(row >= s, pltpu.roll(ends, s, 0), 0.0)
        run_ref[...] = ends - padded

    @pl.when(p == 1)
    def _():
        before = _dot(picks.astype(BF16), upper_ref[...])
        slot = before + run_ref[:, 0:1]
        rows = [jnp.sum(jnp.where(c, slot, 0.0), axis=0, keepdims=True) for c in chosen]
        rows += [jnp.zeros_like(rows[0])] * (dest_ref.shape[0] - TOP_K)
        dest_ref[...] = jnp.concatenate(rows, axis=0).astype(jnp.int32)
        run_ref[...] += tile_counts


def _route(ids):
    n = ids.shape[0]
    tm = int(np.gcd(n, ROUTE_TILE))
    n_tiles = n // tm
    upper = jnp.asarray(np.triu(np.ones((tm, tm), np.float32), 1), BF16)
    dest, counts = pl.pallas_call(
        functools.partial(_route_kernel, tile=EXPERT_TILE),
        out_shape=[jax.ShapeDtypeStruct((SUBLANES, n), jnp.int32),
                   jax.ShapeDtypeStruct((N_EXPERTS, LANES), jnp.int32)],
        grid=(2, n_tiles),
        in_specs=[pl.BlockSpec((tm, ROUTER_PAD), lambda p, i: (i, 0)),
                  pl.BlockSpec((tm, tm), lambda p, i: (0, 0))],
        out_specs=[pl.BlockSpec((SUBLANES, tm), lambda p, i: (0, i * p)),
                   pl.BlockSpec((N_EXPERTS, LANES), lambda p, i: (0, 0))],
        scratch_shapes=[pltpu.VMEM((N_EXPERTS, LANES), F32)],
        compiler_params=_params(("arbitrary", "arbitrary")),
        name="route",
    )(ids, upper)
    return dest, counts[:, 0]


def _row_gather(table, idx):
    info = plsc.get_sparse_core_info()
    n_cores, n_workers = info.num_cores, info.num_cores * info.num_subcores
    n_rows, width = idx.shape[0], table.shape[1]
    chunk_rows = SC_GATHER_ROWS
    per_worker = n_rows // n_workers
    n_chunks = per_worker // chunk_rows
    assert per_worker * n_workers == n_rows and n_chunks * chunk_rows == per_worker and n_chunks % 2 == 0
    mesh = plsc.VectorSubcoreMesh(core_axis_name="c", subcore_axis_name="s")

    @functools.partial(
        pl.kernel, mesh=mesh,
        out_type=jax.ShapeDtypeStruct((n_rows, width), table.dtype),
        scratch_types=[pltpu.VMEM((n_chunks, chunk_rows), jnp.int32), pltpu.VMEM((2, chunk_rows, width), table.dtype),
                       pltpu.SemaphoreType.DMA((2,)), pltpu.SemaphoreType.DMA((2,))],
    )
    def gather_kernel(table_hbm, idx_hbm, out_hbm, idx_v, rows_v, gather_sem, write_sem):
        worker = lax.axis_index("s") * n_cores + lax.axis_index("c")
        pltpu.sync_copy(idx_hbm.at[worker], idx_v)

        def gather(c, slot):
            return pltpu.make_async_copy(table_hbm.at[idx_v.at[c]], rows_v.at[slot], gather_sem.at[slot])

        def write(c, slot):
            first_row = worker * per_worker + c * chunk_rows
            return pltpu.make_async_copy(rows_v.at[slot], out_hbm.at[pl.ds(first_row, chunk_rows)], write_sem.at[slot])

        gather(0, 0).start()

        @pl.loop(0, n_chunks, step=2)
        def _(c0):
            for slot in (0, 1):
                c = c0 + slot
                gather(c, slot).wait()

                @pl.when(c >= 1)
                def _():
                    write(c - 1, 1 - slot).wait()

                @pl.when(c + 1 < n_chunks)
                def _():
                    gather(c + 1, 1 - slot).start()

                write(c, slot).start()

        write(n_chunks - 1, 1).wait()

    return gather_kernel(table, idx.reshape(n_workers, n_chunks, chunk_rows))


def _row_scatter(rows, dest, n_out):
    info = plsc.get_sparse_core_info()
    n_cores, n_workers = info.num_cores, info.num_cores * info.num_subcores
    n_choices, n_rows = dest.shape
    width = rows.shape[1]
    chunk_rows = SC_GATHER_ROWS
    per_worker = n_rows // (n_workers * chunk_rows)
    assert per_worker * n_workers * chunk_rows == n_rows == rows.shape[0]
    idx = dest.reshape(n_choices, n_workers, per_worker, chunk_rows).transpose(1, 0, 2, 3)
    mesh = plsc.VectorSubcoreMesh(core_axis_name="c", subcore_axis_name="s")

    @functools.partial(
        pl.kernel, mesh=mesh,
        out_type=jax.ShapeDtypeStruct((n_out, width), rows.dtype),
        scratch_types=[pltpu.VMEM((n_choices, per_worker, chunk_rows), jnp.int32),
                       pltpu.VMEM((chunk_rows, width), rows.dtype), pltpu.SemaphoreType.DMA],
    )
    def scatter_kernel(rows_hbm, idx_hbm, out_hbm, idx_v, rows_v, sem):
        worker = lax.axis_index("s") * n_cores + lax.axis_index("c")
        pltpu.sync_copy(idx_hbm.at[worker], idx_v)

        @pl.loop(0, per_worker)
        def _(c):
            first_row = (worker * per_worker + c) * chunk_rows
            pltpu.sync_copy(rows_hbm.at[pl.ds(first_row, chunk_rows)], rows_v)
            copies = [pltpu.async_copy(rows_v, out_hbm.at[idx_v.at[k, c]], sem) for k in range(n_choices)]
            for cp in copies:
                cp.wait()

    return scatter_kernel(rows, idx)


def _moe(h2, ids, layer, w_exp):
    n = h2.shape[0]
    nk = n * TOP_K
    tm = EXPERT_TILE
    dest, counts = _route(ids)
    dest = dest[:TOP_K]
    padded = (counts + tm - 1) // tm * tm
    pends = jnp.cumsum(padded)
    n_blocks = -(-nk // tm) + N_EXPERTS
    n_slots = n_blocks * tm
    block_start = jnp.arange(n_blocks, dtype=jnp.int32) * tm
    block_e = jnp.minimum(jnp.sum((pends[None] <= block_start[:, None]).astype(jnp.int32), axis=1), N_EXPERTS - 1)
    is_e = (block_e[:, None] == jnp.arange(N_EXPERTS, dtype=jnp.int32)[None]).astype(jnp.int32)
    block_rows = jnp.clip(jnp.sum(is_e * (pends - padded + counts)[None], axis=1) - block_start, 0, tm)
    n_used = (pends[-1] // tm).astype(jnp.int32).reshape(1)
    e_ids = jnp.arange(N_EXPERTS, dtype=jnp.int32)
    later = (e_ids[None] > e_ids[:, None]) & (counts[None] > 0)
    next_nonempty = jnp.min(jnp.where(later, e_ids[None], N_EXPERTS), axis=1)
    next_nonempty = jnp.where(next_nonempty == N_EXPERTS, -1, next_nonempty)
    next_e = jnp.sum(is_e * next_nonempty[None], axis=1)
    xb = _row_scatter(h2, dest, n_slots)
    return _experts(xb, block_e, next_e, block_rows, n_used, layer, *w_exp), dest


def _gather_choices(yb, dest, lo, hi):
    return _row_gather(yb, dest[:, lo:hi].reshape(-1)).reshape(TOP_K, hi - lo, -1)


def _mod_parts(mod_l, b):
    d = mod_l.shape[-1] // 6
    lat = [mod_l[:b, k * d:(k + 1) * d].reshape(b, 1, d) for k in range(6)]
    ctx = [jnp.broadcast_to(mod_l[b, k * d:(k + 1) * d].reshape(1, 1, d), (b, 1, d)) for k in range(6)]
    return lat, ctx


def kernel(x, c, ctx, c_ctx, ada_w, ada_b, norm1_g, norm2_g, ev_w_in, ev_w_out, ev_q_gain, ev_k_gain, ev_rpb, ev_conv_w, ev_conv_b, od_w_in, od_w_out, od_conv_w, od_conv_b, od_fwd_wa, od_fwd_ba, od_fwd_wx, od_fwd_bx, od_fwd_lam, od_bwd_wa, od_bwd_ba, od_bwd_wx, od_bwd_bx, od_bwd_lam, router_w, router_b, exp_w_gate, exp_b_gate, exp_w_up, exp_b_up, exp_w_down, exp_b_down):
    b, t, d = x.shape
    l = ctx.shape[1]
    assert ada_w.shape[0] == DEPTH == 2 and t % GRID_W == 0 and t // GRID_W >= WIN_H

    n_rows_c = -(-(b + 1) // SUBLANES) * SUBLANES
    cvec = jnp.zeros((n_rows_c, d), F32).at[:b].set(c).at[b].set(c_ctx)
    mod = _ada_mod(cvec, ada_w, ada_b)

    def router(layer):
        w_r = jnp.zeros((d, ROUTER_PAD), F32).at[:, :N_EXPERTS].set(router_w[layer]).astype(BF16)
        b_r = jnp.zeros((1, ROUTER_PAD), F32).at[0, :N_EXPERTS].set(router_b[layer])
        return w_r, b_r

    w_exp = (exp_w_gate, exp_b_gate, exp_w_up, exp_b_up, exp_w_down, exp_b_down)

    (sh1, sc1, g1, sh2, sc2, g2), (csh1, csc1, cg1, csh2, csc2, cg2) = _mod_parts(mod[0], b)
    n1 = norm1_g[0].reshape(1, d)
    n2 = norm2_g[0].reshape(1, d)
    w_in = ev_w_in[0].astype(BF16)
    w_out = ev_w_out[0].astype(BF16)
    q_scale = NA_HEAD_DIM ** -0.5 * LOG2_E
    head_gain = jnp.stack([jnp.tile(ev_q_gain[0] * q_scale, NA_HEADS), jnp.tile(ev_k_gain[0], NA_HEADS)])
    head_gain = head_gain.reshape(2, 1, NA_WIDTH).astype(F32)
    proj = _inproj(x, n1, sh1, sc1, w_in, head_gain)
    proj_c = _inproj(ctx, n1, csh1, csc1, w_in, head_gain)
    kh, r0, row_type, patterns = _na_tables(t // GRID_W)
    bias_tab = _na_bias_table(ev_rpb[0], patterns)
    o_a = _neighbourhood_attention(proj, proj_c, bias_tab, r0, row_type, kh)
    oc_a = _context_attention(proj_c)
    w_r, b_r = router(0)
    n_tok = b * (l + t)
    fresh = (jnp.zeros((n_tok, d // 2), jnp.int32), jnp.zeros((n_tok, ROUTER_PAD), jnp.int32))
    c1, tokens, ids_all, gates_c = _even_out(oc_a, proj_c, ev_conv_w[0], ev_conv_b[0], w_out, ctx, cg1, n2, csh2,
                                             csc2, w_r, b_r, (n_tok, 0), fresh)
    x1, tokens, ids_all, gates = _even_out(o_a, proj, ev_conv_w[0], ev_conv_b[0], w_out, x, g1, n2, sh2, sc2, w_r,
                                           b_r, (n_tok, b * l), (tokens, ids_all))
    yb, dest = _moe(tokens, ids_all, 0, w_exp)
    y_sel = _gather_choices(yb, dest, 0, b * (l + t))
    g2_prev, cg2_prev = g2, cg2

    (sh1, sc1, g1, sh2, sc2, g2), (csh1, csc1, _, _, _, _) = _mod_parts(mod[1], b)
    n1 = norm1_g[1].reshape(1, d)
    n2 = norm2_g[1].reshape(1, d)
    w_in = od_w_in[0].astype(BF16)
    width = w_in.shape[1] // 2
    x, proj = _combine_inproj_conv(x1, g2_prev, gates, y_sel, b * l, n1, sh1, sc1, w_in, od_conv_w[0], od_conv_b[0],
                                   width)
    _, u_ctx = _combine_inproj_conv(c1, cg2_prev, gates_c, y_sel, 0, n1, csh1, csc1, w_in[:, width:], od_conv_w[0],
                                    od_conv_b[0], 0)
    h_dir = []
    for reverse, (wa, ba, wx, bx, lam) in ((False, (od_fwd_wa, od_fwd_ba, od_fwd_wx, od_fwd_bx, od_fwd_lam)),
                                           (True, (od_bwd_wa, od_bwd_ba, od_bwd_wx, od_bwd_bx, od_bwd_lam))):
        w_cat = jnp.concatenate([wa[0], wx[0]], axis=-1).astype(BF16)
        h_dir.append(_lru_scan(proj, u_ctx, w_cat, ba[0], bx[0], lam[0], reverse))
    w_r, b_r = router(1)
    x1, h2, ids, gates = _odd_out(h_dir[0], h_dir[1], proj, od_w_out[0].astype(BF16), x, g1, n2, sh2, sc2, w_r, b_r)
    yb, dest = _moe(h2, ids, 1, w_exp)
    return _combine(x1, g2, gates, _gather_choices(yb, dest, 0, b * t), 0)
```

```python
import functools

import numpy as np
import jax
import jax.numpy as jnp
from jax import lax
from jax.experimental import pallas as pl
from jax.experimental.pallas import tpu as pltpu
from jax.experimental.pallas import tpu_sc as plsc

DEPTH = 2
GRID_W = 64
EPS = 1e-6
NEG_INF = -1e30
LOG2_E = 1.4426950408889634
NA_HEADS = 8
NA_HEAD_DIM = 64
NA_WIDTH = NA_HEADS * NA_HEAD_DIM
HEAD_PAIRS = NA_HEADS // 2
WIN_H = 8
WIN_W = 16
SC_CONV = 3
LRU_BLOCKS = 4
LRU_CONV = 4
LRU_C = 8.0
N_EXPERTS = 32
TOP_K = 4
SWIGLU_LIMIT = 7.0
SWIGLU_ALPHA = 1.702

LANES = 128
SUBLANES = 8
MXU_DEPTH = 256
HALO = 16
ROUTER_PAD = LANES
ADA_COL_TILE = 1536
ROW_TILE = 1024
NA_ROWS_PER_STEP = 4
LRU_SUB_TILES = 4
OUT_TILE = 1024
OUT_SUB_TILE = 512
ROUTE_TILE = 2048
EXPERT_TILE = 2048
EXPERT_SUB = 512
SC_GATHER_ROWS = 64
VMEM_LIMIT = 56 * 1024 * 1024

F32 = jnp.float32
BF16 = jnp.bfloat16


def _params(sem, vmem=VMEM_LIMIT):
    return pltpu.CompilerParams(dimension_semantics=sem, vmem_limit_bytes=vmem)


def _dot(a, b):
    return jnp.dot(a, b, preferred_element_type=F32)


def _dot_nt(a, b):
    return lax.dot_general(a, b, (((1,), (1,)), ((), ())), preferred_element_type=F32)


def _pack_rows(v):
    w = v.shape[-1] // 2
    lo = lax.bitcast_convert_type(v[:, :w].astype(BF16).astype(F32), jnp.int32)
    hi = lax.bitcast_convert_type(v[:, w:].astype(BF16).astype(F32), jnp.int32)
    return lax.shift_right_logical(lo, 16) | (hi & jnp.int32(-65536))


def _unpack_rows(p):
    lo = lax.bitcast_convert_type(lax.shift_left(p, 16), F32)
    hi = lax.bitcast_convert_type(p & jnp.int32(-65536), F32)
    return lo, hi


def _rms_mod(x, g, shift, scale):
    ms = jnp.mean(x * x, axis=-1, keepdims=True)
    return (x * lax.rsqrt(ms + EPS)) * (g * (1.0 + scale)) + shift


def _ada_kernel(c_ref, w_ref, b_ref, o_ref):
    c = c_ref[...]
    s = (c * jax.nn.sigmoid(c)).astype(BF16)
    o_ref[0] = _dot(s, w_ref[0].astype(BF16)) + b_ref[0]


def _ada_mod(cvec, ada_w, ada_b):
    depth, d, n = ada_w.shape
    r = cvec.shape[0]
    tn = ADA_COL_TILE
    return pl.pallas_call(
        _ada_kernel,
        out_shape=jax.ShapeDtypeStruct((depth, r, n), F32),
        grid=(depth, n // tn),
        in_specs=[
            pl.BlockSpec((r, d), lambda l, j: (0, 0)),
            pl.BlockSpec((1, d, tn), lambda l, j: (l, 0, j)),
            pl.BlockSpec((1, 1, tn), lambda l, j: (l, 0, j)),
        ],
        out_specs=pl.BlockSpec((1, r, tn), lambda l, j: (l, 0, j)),
        compiler_params=_params(("parallel", "parallel")),
        name="ada_mod",
    )(cvec, ada_w, ada_b.reshape(depth, 1, n))


def _inproj_kernel(x_ref, g_ref, sh_ref, sc_ref, w_ref, hg_ref, ones_ref, o_ref, *, n_tiles, tn, n_headnorm):
    h = _rms_mod(x_ref[0], g_ref[...], sh_ref[0], sc_ref[0]).astype(BF16)
    for j in range(n_tiles):
        y = _dot(h, w_ref[:, j * tn:(j + 1) * tn])
        if j < n_headnorm:
            ysq = (y * y).astype(BF16)
            kw = ones_ref.shape[0]
            ms = jnp.concatenate([_dot(ysq[:, c:c + kw], ones_ref[...]) for c in range(0, tn, kw)], axis=1)
            y = y * lax.rsqrt(ms * (1.0 / NA_HEAD_DIM) + EPS) * hg_ref[j]
        o_ref[0, :, j * tn:(j + 1) * tn] = y.astype(o_ref.dtype)


def _inproj(x, g, shift, scale, w, head_gain):
    b, t, d = x.shape
    n = w.shape[1]
    tn = NA_WIDTH
    tm = min(t, ROW_TILE)
    n_headnorm = head_gain.shape[0]
    hid = np.arange(MXU_DEPTH) // NA_HEAD_DIM
    ones_bd = jnp.asarray((hid[:, None] == hid[None, :]), BF16)
    kern = functools.partial(_inproj_kernel, n_tiles=n // tn, tn=tn, n_headnorm=n_headnorm)
    return pl.pallas_call(
        kern,
        out_shape=jax.ShapeDtypeStruct((b, t, n), BF16),
        grid=(b, t // tm),
        in_specs=[
            pl.BlockSpec((1, tm, d), lambda bi, i: (bi, i, 0)),
            pl.BlockSpec((1, d), lambda bi, i: (0, 0)),
            pl.BlockSpec((1, 1, d), lambda bi, i: (bi, 0, 0)),
            pl.BlockSpec((1, 1, d), lambda bi, i: (bi, 0, 0)),
            pl.BlockSpec((d, n), lambda bi, i: (0, 0)),
            pl.BlockSpec(head_gain.shape, lambda bi, i: (0, 0, 0)),
            pl.BlockSpec((MXU_DEPTH, MXU_DEPTH), lambda bi, i: (0, 0)),
        ],
        out_specs=pl.BlockSpec((1, tm, n), lambda bi, i: (bi, i, 0)),
        compiler_params=_params(("parallel", "parallel")),
        name="inproj",
    )(x, g, shift, scale, w, head_gain, ones_bd)


def _moe_residual(x, g2, gates, y):
    acc_lo = acc_hi = 0.0
    for k in range(TOP_K):
        lo, hi = _unpack_rows(y[k])
        acc_lo = acc_lo + gates[:, k:k + 1] * lo
        acc_hi = acc_hi + gates[:, k:k + 1] * hi
    return x + g2 * jnp.concatenate([acc_lo, acc_hi], axis=1)


def _combine_inproj_conv_kernel(x_ref, xp_ref, xn_ref, gt_ref, gtp_ref, gtn_ref, y_ref, yp_ref, yn_ref, g2_ref,
                                g_ref, sh_ref, sc_ref, w_ref, cw_ref, cb_ref, xo_ref, o_ref, *,
                                n_tiles, n_plain, tn, tm, sub):
    i = pl.program_id(1)
    edge = SUBLANES
    g2 = g2_ref[0]
    x_tile = _moe_residual(x_ref[0], g2, gt_ref[0], y_ref[...])
    xo_ref[0] = x_tile
    x_ext = jnp.concatenate([_moe_residual(xp_ref[0], g2, gtp_ref[0], yp_ref[...]), x_tile,
                             _moe_residual(xn_ref[0], g2, gtn_ref[0], yn_ref[...])], axis=0)
    h = _rms_mod(x_ext, g_ref[...], sh_ref[0], sc_ref[0]).astype(BF16)
    first_step = i == 0
    last_step = i == pl.num_programs(1) - 1
    left = (LRU_CONV - 1) // 2
    n_sub = tm // sub
    for s in range(n_sub):
        h_s = h[s * sub:(s + 1) * sub + 2 * edge]
        rows = slice(s * sub, (s + 1) * sub)
        keep_prev = jnp.where(first_step, 0.0, 1.0) if s == 0 else 1.0
        keep_next = jnp.where(last_step, 0.0, 1.0) if s == n_sub - 1 else 1.0
        for j in range(n_tiles):
            y = _dot(h_s, w_ref[:, j * tn:(j + 1) * tn])
            if j < n_plain:
                o_ref[0, rows, j * tn:(j + 1) * tn] = y[edge:edge + sub].astype(o_ref.dtype)
                continue
            u = jnp.concatenate([y[:edge] * keep_prev, y[edge:edge + sub], y[edge + sub:] * keep_next], axis=0)
            cw = cw_ref[:, (j - n_plain) * tn:(j - n_plain + 1) * tn]
            uc = cb_ref[:, (j - n_plain) * tn:(j - n_plain + 1) * tn]
            n_ext = u.shape[0]
            for k in range(LRU_CONV):
                shifted = u if k == left else pltpu.roll(u, (left - k) % n_ext, 0)
                uc = uc + shifted[edge:edge + sub] * cw[k:k + 1]
            o_ref[0, rows, j * tn:(j + 1) * tn] = uc.astype(o_ref.dtype)


def _combine_inproj_conv(x1, g2, gates, y_sel, tok_offset, g, shift, scale, w, conv_w, conv_b, n_plain_cols):
    b, t, d = x1.shape
    n = w.shape[1]
    tn = NA_WIDTH
    tm = min(t, OUT_TILE)
    hb = tm // SUBLANES
    n_hblocks = t // SUBLANES
    y_hblocks = y_sel.shape[1] // SUBLANES
    assert tok_offset % tm == 0
    kern = functools.partial(_combine_inproj_conv_kernel, n_tiles=n // tn, n_plain=n_plain_cols // tn, tn=tn, tm=tm,
                             sub=min(tm, OUT_SUB_TILE))
    const = lambda bi, i: (0, 0)
    tile = lambda bi, i: (bi, i, 0)
    prev = lambda bi, i: (bi, jnp.maximum(i * hb - 1, 0), 0)
    nxt = lambda bi, i: (bi, jnp.minimum((i + 1) * hb, n_hblocks - 1), 0)
    y_row = lambda bi, i: (tok_offset + bi * t + i * tm) // SUBLANES
    per_b = lambda bi, i: (bi, 0, 0)
    return pl.pallas_call(
        kern,
        out_shape=[jax.ShapeDtypeStruct((b, t, d), F32), jax.ShapeDtypeStruct((b, t, n), BF16)],
        grid=(b, t // tm),
        in_specs=[
            pl.BlockSpec((1, tm, d), tile),
            pl.BlockSpec((1, SUBLANES, d), prev),
            pl.BlockSpec((1, SUBLANES, d), nxt),
            pl.BlockSpec((1, tm, ROUTER_PAD), tile),
            pl.BlockSpec((1, SUBLANES, ROUTER_PAD), prev),
            pl.BlockSpec((1, SUBLANES, ROUTER_PAD), nxt),
            pl.BlockSpec((TOP_K, tm, d // 2), lambda bi, i: (0, y_row(bi, i) // hb, 0)),
            pl.BlockSpec((TOP_K, SUBLANES, d // 2), lambda bi, i: (0, jnp.maximum(y_row(bi, i) - 1, 0), 0)),
            pl.BlockSpec((TOP_K, SUBLANES, d // 2),
                         lambda bi, i: (0, jnp.minimum(y_row(bi, i) + hb, y_hblocks - 1), 0)),
            pl.BlockSpec((1, 1, d), per_b),
            pl.BlockSpec((1, d), const),
            pl.BlockSpec((1, 1, d), per_b),
            pl.BlockSpec((1, 1, d), per_b),
            pl.BlockSpec((d, n), const),
            pl.BlockSpec(conv_w.shape, const),
            pl.BlockSpec((1, conv_w.shape[1]), const),
        ],
        out_specs=[pl.BlockSpec((1, tm, d), tile), pl.BlockSpec((1, tm, n), tile)],
        compiler_params=_params(("parallel", "parallel")),
        name="combine_inproj_conv",
    )(x1, x1, x1, gates, gates, gates, y_sel, y_sel, y_sel, g2, g, shift, scale, w, conv_w, conv_b.reshape(1, -1))


def _na_tables(rows):
    kh = min(WIN_H, rows)
    r = np.arange(rows)
    r0 = np.clip(r - kh // 2, 0, rows - kh)
    dr = r0[:, None] + np.arange(kh)[None] - r[:, None] + WIN_H - 1
    patterns, row_type = np.unique(dr, axis=0, return_inverse=True)
    return kh, r0.astype(np.int32), row_type.reshape(-1).astype(np.int32), patterns


def _na_bias_table(rpb, patterns):
    qc = np.arange(GRID_W)
    kc = np.arange(GRID_W)
    c0 = np.clip(qc - WIN_W // 2, 0, GRID_W - WIN_W)[:, None]
    valid = (kc[None] >= c0) & (kc[None] < c0 + WIN_W)
    dc = np.clip(kc[None] - qc[:, None] + WIN_W - 1, 0, 2 * WIN_W - 2)
    n_pat, kh = patterns.shape
    onehot_dc = jnp.asarray(dc[None] == np.arange(2 * WIN_W - 1)[:, None, None], F32)
    tab = jnp.einsum('hpic,cqk->hpiqk', rpb.astype(F32)[:, patterns], onehot_dc,
                     precision=lax.Precision.HIGHEST)
    tab = jnp.where(valid[None, None, None], tab * LOG2_E, NEG_INF)
    tab = tab.reshape(HEAD_PAIRS, 2, n_pat, kh, GRID_W, GRID_W)
    tab = tab.transpose(2, 0, 1, 4, 3, 5)
    return tab.reshape(n_pat, HEAD_PAIRS, 2 * GRID_W, kh * GRID_W)


def _pair_attention(q, keys, values, biases):
    m = q.shape[0]
    qq = _stack_heads(q)
    scores = []
    for k, bias in zip(keys, biases):
        s = _dot_nt(qq, k)
        scores.append(s if bias is None else s + bias)
    s = jnp.concatenate(scores, axis=1)
    e = jnp.exp2(s - jnp.max(s, axis=-1, keepdims=True))
    denom = jnp.sum(e, axis=-1, keepdims=True)
    e = e.astype(BF16)
    o, start = 0.0, 0
    for v in values:
        o = o + _dot(e[:, start:start + v.shape[0]], v)
        start += v.shape[0]
    o = o * (1.0 / denom)
    lane_o = lax.broadcasted_iota(jnp.int32, (m, LANES), 1)
    return jnp.where(lane_o < NA_HEAD_DIM, o[:m], o[m:])


def _stack_heads(q):
    lane = lax.broadcasted_iota(jnp.int32, q.shape, 1)
    zero = jnp.zeros_like(q)
    return jnp.concatenate([jnp.where(lane < NA_HEAD_DIM, q, zero), jnp.where(lane >= NA_HEAD_DIM, q, zero)], axis=0)


def _na_kernel(r0_ref, type_ref, q_ref, k_ref, v_ref, kc_ref, vc_ref, bias_ref, o_ref, s_ref, p_ref, *,
               kh, rows_per_step):
    n_lat = kh * GRID_W
    tiles = [(j, p) for j in range(rows_per_step) for p in range(HEAD_PAIRS)]
    window = []
    for j in range(rows_per_step):
        r = pl.program_id(1) * rows_per_step + j
        window.append((pl.multiple_of(r0_ref[r] * GRID_W, GRID_W), type_ref[r]))

    for idx, (j, p) in enumerate(tiles):
        start, rtype = window[j]
        cols = slice(p * LANES, (p + 1) * LANES)
        qq = _stack_heads(q_ref[0, j * GRID_W:(j + 1) * GRID_W, cols])
        s_ref[idx, :, :n_lat] = _dot_nt(qq, k_ref[0, pl.ds(start, n_lat), cols]) + bias_ref[rtype, p]
        s_ref[idx, :, n_lat:] = _dot_nt(qq, kc_ref[0, :, cols])

    denoms = []
    for idx in range(len(tiles)):
        s = s_ref[idx]
        e = jnp.exp2(s - jnp.max(s, axis=-1, keepdims=True))
        denoms.append(jnp.sum(e, axis=-1, keepdims=True))
        p_ref[idx] = e.astype(BF16)

    lane = lax.broadcasted_iota(jnp.int32, (GRID_W, LANES), 1)
    for idx, (j, p) in enumerate(tiles):
        start, _ = window[j]
        cols = slice(p * LANES, (p + 1) * LANES)
        o = _dot(p_ref[idx, :, :n_lat], v_ref[0, pl.ds(start, n_lat), cols]) + _dot(p_ref[idx, :, n_lat:],
                                                                                 vc_ref[0, :, cols])
        o = o * (1.0 / denoms[idx])
        o = jnp.where(lane < NA_HEAD_DIM, o[:GRID_W], o[GRID_W:])
        o_ref[0, j * GRID_W:(j + 1) * GRID_W, cols] = o.astype(o_ref.dtype)


def _neighbourhood_attention(proj, proj_c, bias_tab, r0, row_type, kh):
    b, t, _ = proj.shape
    l = proj_c.shape[1]
    rows = t // GRID_W
    w = NA_WIDTH
    rps = int(np.gcd(rows, NA_ROWS_PER_STEP))
    q_rows = rps * GRID_W
    grid_spec = pltpu.PrefetchScalarGridSpec(
        num_scalar_prefetch=2,
        grid=(b, rows // rps),
        in_specs=[
            pl.BlockSpec((1, q_rows, w), lambda bi, r, *_: (bi, r, 0)),
            pl.BlockSpec((1, t, w), lambda bi, r, *_: (bi, 0, 1)),
            pl.BlockSpec((1, t, w), lambda bi, r, *_: (bi, 0, 2)),
            pl.BlockSpec((1, l, w), lambda bi, r, *_: (bi, 0, 1)),
            pl.BlockSpec((1, l, w), lambda bi, r, *_: (bi, 0, 2)),
            pl.BlockSpec(bias_tab.shape, lambda bi, r, *_: (0, 0, 0, 0)),
        ],
        out_specs=pl.BlockSpec((1, q_rows, w), lambda bi, r, *_: (bi, r, 0)),
        scratch_shapes=[pltpu.VMEM((rps * HEAD_PAIRS, 2 * GRID_W, kh * GRID_W + l), F32),
                        pltpu.VMEM((rps * HEAD_PAIRS, 2 * GRID_W, kh * GRID_W + l), BF16)],
    )
    return pl.pallas_call(
        functools.partial(_na_kernel, kh=kh, rows_per_step=rps),
        out_shape=jax.ShapeDtypeStruct((b, t, w), BF16),
        grid_spec=grid_spec,
        compiler_params=_params(("parallel", "arbitrary")),
        name="na_attention",
    )(jnp.asarray(r0), jnp.asarray(row_type), proj, proj, proj, proj_c, proj_c, bias_tab)


def _ctx_attn_kernel(q_ref, k_ref, v_ref, o_ref):
    for p in range(HEAD_PAIRS):
        cols = slice(p * LANES, (p + 1) * LANES)
        o = _pair_attention(q_ref[0, :, cols], [k_ref[0, :, cols]], [v_ref[0, :, cols]], [None])
        o_ref[0, :, cols] = o.astype(o_ref.dtype)


def _context_attention(proj_c):
    b, l, _ = proj_c.shape
    w = NA_WIDTH
    return pl.pallas_call(
        _ctx_attn_kernel,
        out_shape=jax.ShapeDtypeStruct((b, l, w), BF16),
        grid=(b,),
        in_specs=[pl.BlockSpec((1, l, w), lambda bi, j=j: (bi, 0, j)) for j in range(3)],
        out_specs=pl.BlockSpec((1, l, w), lambda bi: (bi, 0, 0)),
        compiler_params=_params(("parallel",)),
        name="ctx_attention",
    )(proj_c, proj_c, proj_c)


def _top4(logits):
    lane = lax.broadcasted_iota(jnp.int32, logits.shape, 1)
    cur = jnp.where(lane < N_EXPERTS, logits, -jnp.inf)
    vals, idxs = [], []
    for _ in range(TOP_K):
        m = jnp.max(cur, axis=-1, keepdims=True)
        first = jnp.min(jnp.where(cur == m, lane, ROUTER_PAD).astype(F32), axis=-1, keepdims=True)
        idx = first.astype(jnp.int32)
        vals.append(m)
        idxs.append(idx)
        cur = jnp.where(lane == idx, -jnp.inf, cur)
    exps = [jnp.exp(v - vals[0]) for v in vals]
    inv = 1.0 / functools.reduce(jnp.add, exps)
    ids = jnp.zeros(logits.shape, jnp.int32)
    gates = jnp.zeros(logits.shape, F32)
    for k in range(TOP_K):
        ids = jnp.where(lane == k, idxs[k], ids)
        gates = jnp.where(lane == k, exps[k] * inv, gates)
    return ids, gates


def _layer_tail(y, rows, x_ref, g1_ref, n2_ref, sh2_ref, sc2_ref, wr_ref, br_ref, *rest):
    xo_ref, h2_ref, ids_ref, gates_ref = rest[-4:]
    x_new = x_ref[0, rows] + g1_ref[0] * y
    xo_ref[0, rows] = x_new
    h2 = _rms_mod(x_new, n2_ref[...], sh2_ref[0], sc2_ref[0])
    h2_ref[rows] = _pack_rows(h2)
    logits = _dot(h2.astype(BF16), wr_ref[...]) + br_ref[...]
    ids, gates = _top4(logits)
    ids_ref[rows] = ids
    gates_ref[0, rows] = gates


def _sub_tiles(n_rows):
    sub = min(OUT_SUB_TILE, n_rows)
    return [slice(s, s + sub) for s in range(0, n_rows, sub)]


def _tail_specs(b, t, d, tm, n_inputs_before, tokens, shared):
    n_total, offset = tokens
    n_t = t // tm
    assert offset % tm == 0
    row = lambda bi, i: (bi, i, 0)
    flat = lambda bi, i: (offset // tm + bi * n_t + i, 0)
    per_b = lambda bi, i: (bi, 0, 0)
    const = lambda bi, i: (0, 0)
    extra, aliases = (), {}
    if shared is not None:
        extra = tuple(shared)
        aliases = {n_inputs_before + 7: 1, n_inputs_before + 8: 2}
    in_specs = [
        pl.BlockSpec((1, tm, d), row),
        pl.BlockSpec((1, 1, d), per_b),
        pl.BlockSpec((1, d), const),
        pl.BlockSpec((1, 1, d), per_b),
        pl.BlockSpec((1, 1, d), per_b),
        pl.BlockSpec((d, ROUTER_PAD), const),
        pl.BlockSpec((1, ROUTER_PAD), const),
    ] + [pl.BlockSpec(memory_space=pl.ANY)] * len(extra)
    out_specs = [
        pl.BlockSpec((1, tm, d), row),
        pl.BlockSpec((tm, d // 2), flat),
        pl.BlockSpec((tm, ROUTER_PAD), flat),
        pl.BlockSpec((1, tm, ROUTER_PAD), row),
    ]
    out_shape = [
        jax.ShapeDtypeStruct((b, t, d), F32),
        jax.ShapeDtypeStruct((n_total, d // 2), jnp.int32),
        jax.ShapeDtypeStruct((n_total, ROUTER_PAD), jnp.int32),
        jax.ShapeDtypeStruct((b, t, ROUTER_PAD), F32),
    ]
    return in_specs, out_specs, out_shape, extra, aliases


def _halo_fix(rolled, at_row, halo_row, present):
    n = rolled.shape[0]
    first = at_row < SUBLANES
    assert first or at_row >= n - SUBLANES
    slab = rolled[:SUBLANES] if first else rolled[n - SUBLANES:]
    sub = lax.broadcasted_iota(jnp.int32, slab.shape, 0)
    fill = jnp.where(present, halo_row, jnp.zeros_like(halo_row))
    slab = jnp.where(sub == at_row % SUBLANES, fill, slab)
    return jnp.concatenate([slab, rolled[SUBLANES:]] if first else [rolled[:n - SUBLANES], slab], axis=0)


def _even_out_kernel(oa_ref, bg_ref, cg_ref, xin_ref, cgp_ref, xinp_ref, cgn_ref, xinn_ref, cw_ref, cb_ref,
                     wa_ref, wb_ref, *tail_refs, tm):
    i = pl.program_id(1)
    has_prev = i > 0
    has_next = i < pl.num_programs(1) - 1
    u = cg_ref[0].astype(F32) * xin_ref[0].astype(F32)
    u_prev = (cgp_ref[0].astype(F32) * xinp_ref[0].astype(F32))[HALO - 1:HALO]
    u_next = (cgn_ref[0].astype(F32) * xinn_ref[0].astype(F32))[0:1]
    u_m1 = _halo_fix(pltpu.roll(u, 1, 0), 0, u_prev, has_prev)
    u_p1 = _halo_fix(pltpu.roll(u, tm - 1, 0), tm - 1, u_next, has_next)
    cw = cw_ref[...]
    conv = u_m1 * cw[0:1] + u * cw[1:2] + u_p1 * cw[2:3] + cb_ref[...]
    o_b = (bg_ref[0].astype(F32) * conv).astype(BF16)
    for rows in _sub_tiles(tm):
        y = _dot(oa_ref[0, rows], wa_ref[...]) + _dot(o_b[rows], wb_ref[...])
        _layer_tail(y, rows, *tail_refs)


def _even_out(o_a, proj, conv_w, conv_b, w_out, x, g1, n2, sh2, sc2, w_r, b_r, tokens, shared=None):
    b, t, d = x.shape
    w = NA_WIDTH
    tm = min(t, OUT_TILE)
    hb = tm // HALO
    n_hblocks = t // HALO
    row = lambda bi, i: (bi, i, 0)
    const = lambda bi, i: (0, 0)
    prev = lambda col: (lambda bi, i: (bi, jnp.maximum(i * hb - 1, 0), col))
    nxt = lambda col: (lambda bi, i: (bi, jnp.minimum((i + 1) * hb, n_hblocks - 1), col))
    tail_in, out_specs, out_shape, extra, aliases = _tail_specs(b, t, d, tm, 12, tokens, shared)
    in_specs = [
        pl.BlockSpec((1, tm, w), row),
        pl.BlockSpec((1, tm, w), lambda bi, i: (bi, i, 3)),
        pl.BlockSpec((1, tm, w), lambda bi, i: (bi, i, 4)),
        pl.BlockSpec((1, tm, w), lambda bi, i: (bi, i, 5)),
        pl.BlockSpec((1, HALO, w), prev(4)),
        pl.BlockSpec((1, HALO, w), prev(5)),
        pl.BlockSpec((1, HALO, w), nxt(4)),
        pl.BlockSpec((1, HALO, w), nxt(5)),
        pl.BlockSpec((SC_CONV, w), const),
        pl.BlockSpec((1, w), const),
        pl.BlockSpec((w, d), const),
        pl.BlockSpec((w, d), const),
    ] + tail_in
    return pl.pallas_call(
        functools.partial(_even_out_kernel, tm=tm),
        out_shape=out_shape,
        grid=(b, t // tm),
        in_specs=in_specs,
        out_specs=out_specs,
        input_output_aliases=aliases,
        compiler_params=_params(("parallel", "parallel")),
        name="even_out",
    )(o_a, proj, proj, proj, proj, proj, proj, proj, conv_w, conv_b.reshape(1, w),
      w_out[:w], w_out[w:], x, g1, n2, sh2, sc2, w_r, b_r, *extra)


def _log_sigmoid(x):
    return jnp.minimum(x, 0.0) - jnp.log1p(jnp.exp(-jnp.abs(x)))


def _sigmoid(x):
    return 0.5 + 0.5 * jnp.tanh(0.5 * x)


def _lru_sub_tile(ucb_time, state, perm_ref, w_ref, ba_ref, bx_ref, lam_ref, want_hidden, reverse):
    sub, width = ucb_time.shape
    blk = width // LRU_BLOCKS
    steps = sub // SUBLANES
    uc = _dot(perm_ref[0], ucb_time)
    ucb = uc.astype(BF16)
    za, zx = [], []
    for h in range(LRU_BLOCKS):
        z = _dot(ucb[:, h * blk:(h + 1) * blk], w_ref[h])
        za.append(z[:, :blk])
        zx.append(z[:, blk:])
    r = _sigmoid(jnp.concatenate(za, axis=1) + ba_ref[...])
    gate_i = _sigmoid(jnp.concatenate(zx, axis=1) + bx_ref[...])
    log_a = (LRU_C * _log_sigmoid(lam_ref[...])) * r
    a = jnp.exp(log_a)
    th = jnp.tanh(log_a)
    num = -2.0 * th
    mult = jnp.where(num > 0.0, num * lax.rsqrt(num * (1.0 - th)), 0.0)
    bcoef = mult * gate_i * uc

    grp = lambda v, j: v[j * SUBLANES:(j + 1) * SUBLANES]
    prods, local = [grp(a, 0)], [grp(bcoef, 0)]
    for j in range(1, steps):
        aj = grp(a, j)
        local.append(aj * local[-1] + grp(bcoef, j))
        prods.append(aj * prods[-1])

    carry_in = [None] * SUBLANES
    for s in (range(SUBLANES - 1, -1, -1) if reverse else range(SUBLANES)):
        carry_in[s] = state
        state = prods[-1][s:s + 1] * state + local[-1][s:s + 1]
    if not want_hidden:
        return state, None
    start = jnp.concatenate(carry_in, axis=0)
    hidden = jnp.concatenate([prods[j] * start + local[j] for j in range(steps)], axis=0)
    return state, _dot(perm_ref[1], hidden.astype(BF16)).astype(BF16)


def _lru_tile(u_ref, perm_ref, w_ref, ba_ref, bx_ref, lam_ref, carry_ref, o_ref, *, reverse):
    sub = perm_ref.shape[1]
    n_sub = u_ref.shape[1] // sub
    state = carry_ref[0:1, :]
    for k in (range(n_sub - 1, -1, -1) if reverse else range(n_sub)):
        rows = slice(k * sub, (k + 1) * sub)
        state, hidden = _lru_sub_tile(u_ref[0, rows], state, perm_ref, w_ref, ba_ref, bx_ref, lam_ref,
                                      o_ref is not None, reverse)
        if o_ref is not None:
            o_ref[0, rows] = hidden
    carry_ref[...] = jnp.broadcast_to(state, carry_ref.shape)


def _lru_kernel(uc_ref, ul_ref, perm_ref, w_ref, ba_ref, bx_ref, lam_ref, o_ref, carry_ref, *, n_ctx_tiles, reverse):
    j = pl.program_id(1)
    shared = (perm_ref, w_ref, ba_ref, bx_ref, lam_ref, carry_ref)

    @pl.when(j == 0)
    def _():
        carry_ref[...] = jnp.zeros_like(carry_ref)

    @pl.when(j < n_ctx_tiles)
    def _():
        _lru_tile(uc_ref, *shared, None, reverse=reverse)

    @pl.when(j >= n_ctx_tiles)
    def _():
        _lru_tile(ul_ref, *shared, o_ref, reverse=reverse)


def _lru_scan(proj, u_ctx, w_cat, ba, bx, lam, reverse):
    b, t, _ = proj.shape
    l, width = u_ctx.shape[1], u_ctx.shape[2]
    sub = min(256, l, t)
    tc = sub * min(LRU_SUB_TILES, l // sub)
    tl = sub * min(LRU_SUB_TILES, t // sub)
    n_c, n_l = l // tc, t // tl

    def pos_of(step, n):
        step = jnp.clip(step, 0, n - 1)
        return (n - 1 - step) if reverse else step

    def tile_map(off, n, col):
        return lambda bi, j: (bi, pos_of(j - off, n), col)

    const2 = lambda bi, j: (0, 0)
    in_specs = [
        pl.BlockSpec((1, tc, width), tile_map(0, n_c, 0)),
        pl.BlockSpec((1, tl, width), tile_map(n_c, n_l, 1)),
        pl.BlockSpec((2, sub, sub), lambda bi, j: (0, 0, 0)),
        pl.BlockSpec(w_cat.shape, lambda bi, j: (0, 0, 0)),
        pl.BlockSpec((1, width), const2),
        pl.BlockSpec((1, width), const2),
        pl.BlockSpec((1, width), const2),
    ]
    steps = sub // SUBLANES
    step, block = np.divmod(np.arange(sub), SUBLANES)
    time_of_row = block * steps + (steps - 1 - step if reverse else step)
    perm = (time_of_row[:, None] == np.arange(sub)[None]).astype(np.float32)
    perms = jnp.asarray(np.stack([perm, perm.T]), BF16)
    kern = functools.partial(_lru_kernel, n_ctx_tiles=n_c, reverse=reverse)
    return pl.pallas_call(
        kern,
        out_shape=jax.ShapeDtypeStruct((b, t, width), BF16),
        grid=(b, n_c + n_l),
        in_specs=in_specs,
        out_specs=pl.BlockSpec((1, tl, width), tile_map(n_c, n_l, 0)),
        scratch_shapes=[pltpu.VMEM((SUBLANES, width), F32)],
        compiler_params=_params(("parallel", "arbitrary")),
        name="lru_scan_bwd" if reverse else "lru_scan_fwd",
    )(u_ctx, proj, perms, w_cat, ba.reshape(1, width), bx.reshape(1, width), lam.reshape(1, width))


def _odd_out_kernel(hf_ref, hb_ref, gate_ref, w_ref, *tail_refs):
    for rows in _sub_tiles(hf_ref.shape[1]):
        hsum = hf_ref[0, rows].astype(F32) + hb_ref[0, rows].astype(F32)
        z = hsum * jax.nn.gelu(gate_ref[0, rows].astype(F32), approximate=True)
        y = _dot(z.astype(BF16), w_ref[...])
        _layer_tail(y, rows, *tail_refs)


def _odd_out(h_f, h_b, proj, w_out, x, g1, n2, sh2, sc2, w_r, b_r):
    b, t, d = x.shape
    width = h_f.shape[-1]
    tm = min(t, OUT_TILE)
    row = lambda bi, i: (bi, i, 0)
    tail_in, out_specs, out_shape, _, _ = _tail_specs(b, t, d, tm, 4, (b * t, 0), None)
    in_specs = [
        pl.BlockSpec((1, tm, width), row),
        pl.BlockSpec((1, tm, width), row),
        pl.BlockSpec((1, tm, width), row),
        pl.BlockSpec((width, d), lambda bi, i: (0, 0)),
    ] + tail_in
    return pl.pallas_call(
        _odd_out_kernel,
        out_shape=out_shape,
        grid=(b, t // tm),
        in_specs=in_specs,
        out_specs=out_specs,
        compiler_params=_params(("parallel", "parallel")),
        name="odd_out",
    )(h_f, h_b, proj, w_out, x, g1, n2, sh2, sc2, w_r, b_r)


def _expert_kernel(be_ref, next_ref, rows_ref, nb_ref, x_ref, wg_hbm, bg_ref, wu_hbm, bu_ref, wd_hbm, bd_ref,
                   o_ref, w_f32, wg_bf, wu_bf, wd_bf, h_bf, sem, *, layer, chunk):
    i = pl.program_id(0)
    expert = be_ref[i]
    used = i < nb_ref[0]

    def weight_copies(e):
        return [pltpu.make_async_copy(src.at[layer, e], w_f32.at[j], sem.at[j])
                for j, src in enumerate((wg_hbm, wu_hbm, wd_hbm))]

    @pl.when(i == 0)
    def _():
        for cp in weight_copies(expert):
            cp.start()

    @pl.when(used & ((i == 0) | (expert != be_ref[jnp.maximum(i - 1, 0)])))
    def _():
        for cp in weight_copies(expert):
            cp.wait()
        wg_bf[...] = w_f32[0].astype(BF16)
        wu_bf[...] = w_f32[1].astype(BF16)
        wd_bf[...] = w_f32[2].astype(BF16)

        @pl.when(next_ref[i] >= 0)
        def _():
            for cp in weight_copies(next_ref[i]):
                cp.start()

    sub = h_bf.shape[0]
    n_sub = x_ref.shape[0] // sub
    n_rows = jnp.where(used, rows_ref[i], 0)
    n_live = (n_rows + sub - 1) // sub

    def live_sub_block(s, carry):
        rows = pl.ds(pl.multiple_of(s * sub, sub), sub)
        xp = x_ref[rows]
        row = lax.broadcasted_iota(jnp.int32, xp.shape, 0) + s * sub
        xp = jnp.where(row < n_rows, xp, 0)
        x = jnp.concatenate(_unpack_rows(xp), axis=1).astype(BF16)
        for c in range(wg_bf.shape[1] // chunk):
            cs = slice(c * chunk, (c + 1) * chunk)
            g = jnp.minimum(_dot(x, wg_bf[:, cs]) + bg_ref[0, 0, :, cs], SWIGLU_LIMIT)
            u = jnp.clip(_dot(x, wu_bf[:, cs]) + bu_ref[0, 0, :, cs], -SWIGLU_LIMIT, SWIGLU_LIMIT)
            h_bf[:, cs] = (g * jax.nn.sigmoid(SWIGLU_ALPHA * g) * (u + 1.0)).astype(BF16)
        o_ref[rows] = _pack_rows(_dot(h_bf[...], wd_bf[...]) + bd_ref[0, 0])
        return carry

    def empty_sub_block(s, carry):
        o_ref[pl.ds(pl.multiple_of(s * sub, sub), sub)] = jnp.zeros((sub, o_ref.shape[1]), o_ref.dtype)
        return carry

    lax.fori_loop(0, n_live, live_sub_block, 0)
    lax.fori_loop(n_live, n_sub, empty_sub_block, 0)


def _experts(xb, block_e, next_e, block_rows, n_used, layer, wg, bg, wu, bu, wd, bd):
    n_slots = xb.shape[0]
    depth, n_e, d, d_exp = wg.shape
    assert d == d_exp
    tm = EXPERT_TILE
    n_blocks = n_slots // tm
    xmap = lambda i, be, ne, br, nb: (jnp.minimum(i, nb[0] - 1), 0)
    bmap = lambda i, be, ne, br, nb: (layer, be[i], 0, 0)
    hbm = pl.BlockSpec(memory_space=pl.ANY)
    grid_spec = pltpu.PrefetchScalarGridSpec(
        num_scalar_prefetch=4,
        grid=(n_blocks,),
        in_specs=[
            pl.BlockSpec((tm, d // 2), xmap),
            hbm,
            pl.BlockSpec((1, 1, 1, d_exp), bmap),
            hbm,
            pl.BlockSpec((1, 1, 1, d_exp), bmap),
            hbm,
            pl.BlockSpec((1, 1, 1, d), bmap),
        ],
        out_specs=pl.BlockSpec((tm, d // 2), lambda i, be, ne, br, nb: (i, 0)),
        scratch_shapes=[pltpu.VMEM((3, d, d_exp), F32), pltpu.VMEM((d, d_exp), BF16), pltpu.VMEM((d, d_exp), BF16),
                        pltpu.VMEM((d_exp, d), BF16), pltpu.VMEM((EXPERT_SUB, d_exp), BF16),
                        pltpu.SemaphoreType.DMA((3,))],
    )
    return pl.pallas_call(
        functools.partial(_expert_kernel, layer=layer, chunk=256),
        out_shape=jax.ShapeDtypeStruct((n_slots, d // 2), jnp.int32),
        grid_spec=grid_spec,
        compiler_params=_params(("arbitrary",)),
        name="experts",
    )(block_e, next_e, block_rows, n_used, xb, wg, bg.reshape(depth, n_e, 1, d_exp), wu,
      bu.reshape(depth, n_e, 1, d_exp), wd, bd.reshape(depth, n_e, 1, d))


def _combine_kernel(x_ref, g2_ref, gates_ref, y_ref, o_ref):
    o_ref[0] = _moe_residual(x_ref[0], g2_ref[0], gates_ref[0], y_ref[...])


def _combine(x, g2, gates, y_sel, tok_offset):
    b, t, d = x.shape
    tm = int(np.gcd(min(t, ROW_TILE), tok_offset)) if tok_offset else min(t, ROW_TILE)
    n_t = t // tm
    off = tok_offset // tm
    row = lambda bi, i: (bi, i, 0)
    return pl.pallas_call(
        _combine_kernel,
        out_shape=jax.ShapeDtypeStruct((b, t, d), F32),
        grid=(b, n_t),
        in_specs=[
            pl.BlockSpec((1, tm, d), row),
            pl.BlockSpec((1, 1, d), lambda bi, i: (bi, 0, 0)),
            pl.BlockSpec((1, tm, ROUTER_PAD), row),
            pl.BlockSpec((TOP_K, tm, d // 2), lambda bi, i: (0, off + bi * n_t + i, 0)),
        ],
        out_specs=pl.BlockSpec((1, tm, d), row),
        input_output_aliases={0: 0},
        compiler_params=_params(("parallel", "parallel")),
        name="moe_combine",
    )(x, g2, gates, y_sel)


def _route_kernel(ids_ref, upper_ref, dest_ref, counts_ref, run_ref, *, tile):
    p = pl.program_id(0)
    i = pl.program_id(1)
    tm = ids_ref.shape[0]
    ids_t = ids_ref[...].T
    expert = lax.broadcasted_iota(jnp.int32, (N_EXPERTS, tm), 0)
    chosen = [ids_t[k:k + 1, :] == expert for k in range(TOP_K)]
    picks = functools.reduce(jnp.add, [c.astype(F32) for c in chosen])
    tile_counts = jnp.sum(picks, axis=1, keepdims=True)

    @pl.when((p == 0) & (i == 0))
    def _():
        run_ref[...] = jnp.zeros_like(run_ref)

    @pl.when(p == 0)
    def _():
        run_ref[...] += tile_counts

    @pl.when((p == 1) & (i == 0))
    def _():
        counts = run_ref[...]
        counts_ref[...] = counts.astype(jnp.int32)
        padded = jnp.floor((counts + (tile - 1)) * (1.0 / tile)) * tile
        row = lax.broadcasted_iota(jnp.int32, counts.shape, 0)
        ends = padded
        for s in (1, 2, 4, 8, 16):
            ends = ends + jnp.where(row >= s, pltpu.roll(ends, s, 0), 0.0)
        run_ref[...] = ends - padded

    @pl.when(p == 1)
    def _():
        before = _dot(picks.astype(BF16), upper_ref[...])
        slot = before + run_ref[:, 0:1]
        rows = [jnp.sum(jnp.where(c, slot, 0.0), axis=0, keepdims=True) for c in chosen]
        rows += [jnp.zeros_like(rows[0])] * (dest_ref.shape[0] - TOP_K)
        dest_ref[...] = jnp.concatenate(rows, axis=0).astype(jnp.int32)
        run_ref[...] += tile_counts


def _route(ids):
    n = ids.shape[0]
    tm = int(np.gcd(n, ROUTE_TILE))
    n_tiles = n // tm
    upper = jnp.asarray(np.triu(np.ones((tm, tm), np.float32), 1), BF16)
    dest, counts = pl.pallas_call(
        functools.partial(_route_kernel, tile=EXPERT_TILE),
        out_shape=[jax.ShapeDtypeStruct((SUBLANES, n), jnp.int32),
                   jax.ShapeDtypeStruct((N_EXPERTS, LANES), jnp.int32)],
        grid=(2, n_tiles),
        in_specs=[pl.BlockSpec((tm, ROUTER_PAD), lambda p, i: (i, 0)),
                  pl.BlockSpec((tm, tm), lambda p, i: (0, 0))],
        out_specs=[pl.BlockSpec((SUBLANES, tm), lambda p, i: (0, i * p)),
                   pl.BlockSpec((N_EXPERTS, LANES), lambda p, i: (0, 0))],
        scratch_shapes=[pltpu.VMEM((N_EXPERTS, LANES), F32)],
        compiler_params=_params(("arbitrary", "arbitrary")),
        name="route",
    )(ids, upper)
    return dest, counts[:, 0]


def _row_gather(table, idx):
    info = plsc.get_sparse_core_info()
    n_cores, n_workers = info.num_cores, info.num_cores * info.num_subcores
    n_rows, width = idx.shape[0], table.shape[1]
    chunk_rows = SC_GATHER_ROWS
    per_worker = n_rows // n_workers
    n_chunks = per_worker // chunk_rows
    assert per_worker * n_workers == n_rows and n_chunks * chunk_rows == per_worker and n_chunks % 2 == 0
    mesh = plsc.VectorSubcoreMesh(core_axis_name="c", subcore_axis_name="s")

    @functools.partial(
        pl.kernel, mesh=mesh,
        out_type=jax.ShapeDtypeStruct((n_rows, width), table.dtype),
        scratch_types=[pltpu.VMEM((n_chunks, chunk_rows), jnp.int32), pltpu.VMEM((2, chunk_rows, width), table.dtype),
                       pltpu.SemaphoreType.DMA((2,)), pltpu.SemaphoreType.DMA((2,))],
    )
    def gather_kernel(table_hbm, idx_hbm, out_hbm, idx_v, rows_v, gather_sem, write_sem):
        worker = lax.axis_index("s") * n_cores + lax.axis_index("c")
        pltpu.sync_copy(idx_hbm.at[worker], idx_v)

        def gather(c, slot):
            return pltpu.make_async_copy(table_hbm.at[idx_v.at[c]], rows_v.at[slot], gather_sem.at[slot])

        def write(c, slot):
            first_row = worker * per_worker + c * chunk_rows
            return pltpu.make_async_copy(rows_v.at[slot], out_hbm.at[pl.ds(first_row, chunk_rows)], write_sem.at[slot])

        gather(0, 0).start()

        @pl.loop(0, n_chunks, step=2)
        def _(c0):
            for slot in (0, 1):
                c = c0 + slot
                gather(c, slot).wait()

                @pl.when(c >= 1)
                def _():
                    write(c - 1, 1 - slot).wait()

                @pl.when(c + 1 < n_chunks)
                def _():
                    gather(c + 1, 1 - slot).start()

                write(c, slot).start()

        write(n_chunks - 1, 1).wait()

    return gather_kernel(table, idx.reshape(n_workers, n_chunks, chunk_rows))


def _row_scatter(rows, dest, n_out):
    info = plsc.get_sparse_core_info()
    n_cores, n_workers = info.num_cores, info.num_cores * info.num_subcores
    n_choices, n_rows = dest.shape
    width = rows.shape[1]
    chunk_rows = SC_GATHER_ROWS
    per_worker = n_rows // (n_workers * chunk_rows)
    assert per_worker * n_workers * chunk_rows == n_rows == rows.shape[0]
    idx = dest.reshape(n_choices, n_workers, per_worker, chunk_rows).transpose(1, 0, 2, 3)
    mesh = plsc.VectorSubcoreMesh(core_axis_name="c", subcore_axis_name="s")

    @functools.partial(
        pl.kernel, mesh=mesh,
        out_type=jax.ShapeDtypeStruct((n_out, width), rows.dtype),
        scratch_types=[pltpu.VMEM((n_choices, per_worker, chunk_rows), jnp.int32),
                       pltpu.VMEM((chunk_rows, width), rows.dtype), pltpu.SemaphoreType.DMA],
    )
    def scatter_kernel(rows_hbm, idx_hbm, out_hbm, idx_v, rows_v, sem):
        worker = lax.axis_index("s") * n_cores + lax.axis_index("c")
        pltpu.sync_copy(idx_hbm.at[worker], idx_v)

        @pl.loop(0, per_worker)
        def _(c):
            first_row = (worker * per_worker + c) * chunk_rows
            pltpu.sync_copy(rows_hbm.at[pl.ds(first_row, chunk_rows)], rows_v)
            copies = [pltpu.async_copy(rows_v, out_hbm.at[idx_v.at[k, c]], sem) for k in range(n_choices)]
            for cp in copies:
                cp.wait()

    return scatter_kernel(rows, idx)


def _moe(h2, ids, layer, w_exp):
    n = h2.shape[0]
    nk = n * TOP_K
    tm = EXPERT_TILE
    dest, counts = _route(ids)
    dest = dest[:TOP_K]
    padded = (counts + tm - 1) // tm * tm
    pends = jnp.cumsum(padded)
    n_blocks = -(-nk // tm) + N_EXPERTS
    n_slots = n_blocks * tm
    block_start = jnp.arange(n_blocks, dtype=jnp.int32) * tm
    block_e = jnp.minimum(jnp.sum((pends[None] <= block_start[:, None]).astype(jnp.int32), axis=1), N_EXPERTS - 1)
    is_e = (block_e[:, None] == jnp.arange(N_EXPERTS, dtype=jnp.int32)[None]).astype(jnp.int32)
    block_rows = jnp.clip(jnp.sum(is_e * (pends - padded + counts)[None], axis=1) - block_start, 0, tm)
    n_used = (pends[-1] // tm).astype(jnp.int32).reshape(1)
    e_ids = jnp.arange(N_EXPERTS, dtype=jnp.int32)
    later = (e_ids[None] > e_ids[:, None]) & (counts[None] > 0)
    next_nonempty = jnp.min(jnp.where(later, e_ids[None], N_EXPERTS), axis=1)
    next_nonempty = jnp.where(next_nonempty == N_EXPERTS, -1, next_nonempty)
    next_e = jnp.sum(is_e * next_nonempty[None], axis=1)
    xb = _row_scatter(h2, dest, n_slots)
    return _experts(xb, block_e, next_e, block_rows, n_used, layer, *w_exp), dest


def _gather_choices(yb, dest, lo, hi):
    return _row_gather(yb, dest[:, lo:hi].reshape(-1)).reshape(TOP_K, hi - lo, -1)


def _mod_parts(mod_l, b):
    d = mod_l.shape[-1] // 6
    lat = [mod_l[:b, k * d:(k + 1) * d].reshape(b, 1, d) for k in range(6)]
    ctx = [jnp.broadcast_to(mod_l[b, k * d:(k + 1) * d].reshape(1, 1, d), (b, 1, d)) for k in range(6)]
    return lat, ctx


def kernel(x, c, ctx, c_ctx, ada_w, ada_b, norm1_g, norm2_g, ev_w_in, ev_w_out, ev_q_gain, ev_k_gain, ev_rpb, ev_conv_w, ev_conv_b, od_w_in, od_w_out, od_conv_w, od_conv_b, od_fwd_wa, od_fwd_ba, od_fwd_wx, od_fwd_bx, od_fwd_lam, od_bwd_wa, od_bwd_ba, od_bwd_wx, od_bwd_bx, od_bwd_lam, router_w, router_b, exp_w_gate, exp_b_gate, exp_w_up, exp_b_up, exp_w_down, exp_b_down):
    b, t, d = x.shape
    l = ctx.shape[1]
    assert ada_w.shape[0] == DEPTH == 2 and t % GRID_W == 0 and t // GRID_W >= WIN_H

    n_rows_c = -(-(b + 1) // SUBLANES) * SUBLANES
    cvec = jnp.zeros((n_rows_c, d), F32).at[:b].set(c).at[b].set(c_ctx)
    mod = _ada_mod(cvec, ada_w, ada_b)

    def router(layer):
        w_r = jnp.zeros((d, ROUTER_PAD), F32).at[:, :N_EXPERTS].set(router_w[layer]).astype(BF16)
        b_r = jnp.zeros((1, ROUTER_PAD), F32).at[0, :N_EXPERTS].set(router_b[layer])
        return w_r, b_r

    w_exp = (exp_w_gate, exp_b_gate, exp_w_up, exp_b_up, exp_w_down, exp_b_down)

    (sh1, sc1, g1, sh2, sc2, g2), (csh1, csc1, cg1, csh2, csc2, cg2) = _mod_parts(mod[0], b)
    n1 = norm1_g[0].reshape(1, d)
    n2 = norm2_g[0].reshape(1, d)
    w_in = ev_w_in[0].astype(BF16)
    w_out = ev_w_out[0].astype(BF16)
    q_scale = NA_HEAD_DIM ** -0.5 * LOG2_E
    head_gain = jnp.stack([jnp.tile(ev_q_gain[0] * q_scale, NA_HEADS), jnp.tile(ev_k_gain[0], NA_HEADS)])
    head_gain = head_gain.reshape(2, 1, NA_WIDTH).astype(F32)
    proj = _inproj(x, n1, sh1, sc1, w_in, head_gain)
    proj_c = _inproj(ctx, n1, csh1, csc1, w_in, head_gain)
    kh, r0, row_type, patterns = _na_tables(t // GRID_W)
    bias_tab = _na_bias_table(ev_rpb[0], patterns)
    o_a = _neighbourhood_attention(proj, proj_c, bias_tab, r0, row_type, kh)
    oc_a = _context_attention(proj_c)
    w_r, b_r = router(0)
    n_tok = b * (l + t)
    fresh = (jnp.zeros((n_tok, d // 2), jnp.int32), jnp.zeros((n_tok, ROUTER_PAD), jnp.int32))
    c1, tokens, ids_all, gates_c = _even_out(oc_a, proj_c, ev_conv_w[0], ev_conv_b[0], w_out, ctx, cg1, n2, csh2,
                                             csc2, w_r, b_r, (n_tok, 0), fresh)
    x1, tokens, ids_all, gates = _even_out(o_a, proj, ev_conv_w[0], ev_conv_b[0], w_out, x, g1, n2, sh2, sc2, w_r,
                                           b_r, (n_tok, b * l), (tokens, ids_all))
    yb, dest = _moe(tokens, ids_all, 0, w_exp)
    y_sel = _gather_choices(yb, dest, 0, b * (l + t))
    g2_prev, cg2_prev = g2, cg2

    (sh1, sc1, g1, sh2, sc2, g2), (csh1, csc1, _, _, _, _) = _mod_parts(mod[1], b)
    n1 = norm1_g[1].reshape(1, d)
    n2 = norm2_g[1].reshape(1, d)
    w_in = od_w_in[0].astype(BF16)
    width = w_in.shape[1] // 2
    x, proj = _combine_inproj_conv(x1, g2_prev, gates, y_sel, b * l, n1, sh1, sc1, w_in, od_conv_w[0], od_conv_b[0],
                                   width)
    _, u_ctx = _combine_inproj_conv(c1, cg2_prev, gates_c, y_sel, 0, n1, csh1, csc1, w_in[:, width:], od_conv_w[0],
                                    od_conv_b[0], 0)
    h_dir = []
    for reverse, (wa, ba, wx, bx, lam) in ((False, (od_fwd_wa, od_fwd_ba, od_fwd_wx, od_fwd_bx, od_fwd_lam)),
                                           (True, (od_bwd_wa, od_bwd_ba, od_bwd_wx, od_bwd_bx, od_bwd_lam))):
        w_cat = jnp.concatenate([wa[0], wx[0]], axis=-1).astype(BF16)
        h_dir.append(_lru_scan(proj, u_ctx, w_cat, ba[0], bx[0], lam[0], reverse))
    w_r, b_r = router(1)
    x1, h2, ids, gates = _odd_out(h_dir[0], h_dir[1], proj, od_w_out[0].astype(BF16), x, g1, n2, sh2, sc2, w_r, b_r)
    yb, dest = _moe(h2, ids, 1, w_exp)
    return _combine(x1, g2, gates, _gather_choices(yb, dest, 0, b * t), 0)
```

```python
import functools

import numpy as np
import jax
import jax.numpy as jnp
from jax import lax
from jax.experimental import pallas as pl
from jax.experimental.pallas import tpu as pltpu
from jax.experimental.pallas import tpu_sc as plsc

DEPTH = 2
GRID_W = 64
EPS = 1e-6
NEG_INF = -1e30
LOG2_E = 1.4426950408889634
NA_HEADS = 8
NA_HEAD_DIM = 64
NA_WIDTH = NA_HEADS * NA_HEAD_DIM
HEAD_PAIRS = NA_HEADS // 2
WIN_H = 8
WIN_W = 16
SC_CONV = 3
LRU_BLOCKS = 4
LRU_CONV = 4
LRU_C = 8.0
N_EXPERTS = 32
TOP_K = 4
SWIGLU_LIMIT = 7.0
SWIGLU_ALPHA = 1.702

LANES = 128
SUBLANES = 8
MXU_DEPTH = 256
HALO = 16
ROUTER_PAD = LANES
ADA_COL_TILE = 1536
ROW_TILE = 1024
NA_ROWS_PER_STEP = 4
LRU_SUB_TILES = 4
OUT_TILE = 1024
OUT_SUB_TILE = 512
ROUTE_TILE = 2048
EXPERT_TILE = 2048
EXPERT_SUB = 512
SC_GATHER_ROWS = 64
VMEM_LIMIT = 56 * 1024 * 1024

F32 = jnp.float32
BF16 = jnp.bfloat16


def _params(sem, vmem=VMEM_LIMIT):
    return pltpu.CompilerParams(dimension_semantics=sem, vmem_limit_bytes=vmem)


def _dot(a, b):
    return jnp.dot(a, b, preferred_element_type=F32)


def _dot_nt(a, b):
    return lax.dot_general(a, b, (((1,), (1,)), ((), ())), preferred_element_type=F32)


def _pack_rows(v):
    w = v.shape[-1] // 2
    lo = lax.bitcast_convert_type(v[:, :w].astype(BF16).astype(F32), jnp.int32)
    hi = lax.bitcast_convert_type(v[:, w:].astype(BF16).astype(F32), jnp.int32)
    return lax.shift_right_logical(lo, 16) | (hi & jnp.int32(-65536))


def _unpack_rows(p):
    lo = lax.bitcast_convert_type(lax.shift_left(p, 16), F32)
    hi = lax.bitcast_convert_type(p & jnp.int32(-65536), F32)
    return lo, hi


def _rms_mod(x, g, shift, scale):
    ms = jnp.mean(x * x, axis=-1, keepdims=True)
    return (x * lax.rsqrt(ms + EPS)) * (g * (1.0 + scale)) + shift


def _ada_kernel(c_ref, w_ref, b_ref, o_ref):
    c = c_ref[...]
    s = (c * jax.nn.sigmoid(c)).astype(BF16)
    o_ref[0] = _dot(s, w_ref[0].astype(BF16)) + b_ref[0]


def _ada_mod(cvec, ada_w, ada_b):
    depth, d, n = ada_w.shape
    r = cvec.shape[0]
    tn = ADA_COL_TILE
    return pl.pallas_call(
        _ada_kernel,
        out_shape=jax.ShapeDtypeStruct((depth, r, n), F32),
        grid=(depth, n // tn),
        in_specs=[
            pl.BlockSpec((r, d), lambda l, j: (0, 0)),
            pl.BlockSpec((1, d, tn), lambda l, j: (l, 0, j)),
            pl.BlockSpec((1, 1, tn), lambda l, j: (l, 0, j)),
        ],
        out_specs=pl.BlockSpec((1, r, tn), lambda l, j: (l, 0, j)),
        compiler_params=_params(("parallel", "parallel")),
        name="ada_mod",
    )(cvec, ada_w, ada_b.reshape(depth, 1, n))


def _inproj_kernel(x_ref, g_ref, sh_ref, sc_ref, w_ref, hg_ref, ones_ref, o_ref, *, n_tiles, tn, n_headnorm):
    h = _rms_mod(x_ref[0], g_ref[...], sh_ref[0], sc_ref[0]).astype(BF16)
    for j in range(n_tiles):
        y = _dot(h, w_ref[:, j * tn:(j + 1) * tn])
        if j < n_headnorm:
            ysq = (y * y).astype(BF16)
            kw = ones_ref.shape[0]
            ms = jnp.concatenate([_dot(ysq[:, c:c + kw], ones_ref[...]) for c in range(0, tn, kw)], axis=1)
            y = y * lax.rsqrt(ms * (1.0 / NA_HEAD_DIM) + EPS) * hg_ref[j]
        o_ref[0, :, j * tn:(j + 1) * tn] = y.astype(o_ref.dtype)


def _inproj(x, g, shift, scale, w, head_gain):
    b, t, d = x.shape
    n = w.shape[1]
    tn = NA_WIDTH
    tm = min(t, ROW_TILE)
    n_headnorm = head_gain.shape[0]
    hid = np.arange(MXU_DEPTH) // NA_HEAD_DIM
    ones_bd = jnp.asarray((hid[:, None] == hid[None, :]), BF16)
    kern = functools.partial(_inproj_kernel, n_tiles=n // tn, tn=tn, n_headnorm=n_headnorm)
    return pl.pallas_call(
        kern,
        out_shape=jax.ShapeDtypeStruct((b, t, n), BF16),
        grid=(b, t // tm),
        in_specs=[
            pl.BlockSpec((1, tm, d), lambda bi, i: (bi, i, 0)),
            pl.BlockSpec((1, d), lambda bi, i: (0, 0)),
            pl.BlockSpec((1, 1, d), lambda bi, i: (bi, 0, 0)),
            pl.BlockSpec((1, 1, d), lambda bi, i: (bi, 0, 0)),
            pl.BlockSpec((d, n), lambda bi, i: (0, 0)),
            pl.BlockSpec(head_gain.shape, lambda bi, i: (0, 0, 0)),
            pl.BlockSpec((MXU_DEPTH, MXU_DEPTH), lambda bi, i: (0, 0)),
        ],
        out_specs=pl.BlockSpec((1, tm, n), lambda bi, i: (bi, i, 0)),
        compiler_params=_params(("parallel", "parallel")),
        name="inproj",
    )(x, g, shift, scale, w, head_gain, ones_bd)


def _moe_residual(x, g2, gates, y):
    acc_lo = acc_hi = 0.0
    for k in range(TOP_K):
        lo, hi = _unpack_rows(y[k])
        acc_lo = acc_lo + gates[:, k:k + 1] * lo
        acc_hi = acc_hi + gates[:, k:k + 1] * hi
    return x + g2 * jnp.concatenate([acc_lo, acc_hi], axis=1)


def _combine_inproj_conv_kernel(x_ref, xp_ref, xn_ref, gt_ref, gtp_ref, gtn_ref, y_ref, yp_ref, yn_ref, g2_ref,
                                g_ref, sh_ref, sc_ref, w_ref, cw_ref, cb_ref, xo_ref, o_ref, *,
                                n_tiles, n_plain, tn, tm, sub):
    i = pl.program_id(1)
    edge = SUBLANES
    g2 = g2_ref[0]
    x_tile = _moe_residual(x_ref[0], g2, gt_ref[0], y_ref[...])
    xo_ref[0] = x_tile
    x_ext = jnp.concatenate([_moe_residual(xp_ref[0], g2, gtp_ref[0], yp_ref[...]), x_tile,
                             _moe_residual(xn_ref[0], g2, gtn_ref[0], yn_ref[...])], axis=0)
    h = _rms_mod(x_ext, g_ref[...], sh_ref[0], sc_ref[0]).astype(BF16)
    first_step = i == 0
    last_step = i == pl.num_programs(1) - 1
    left = (LRU_CONV - 1) // 2
    n_sub = tm // sub
    for s in range(n_sub):
        h_s = h[s * sub:(s + 1) * sub + 2 * edge]
        rows = slice(s * sub, (s + 1) * sub)
        keep_prev = jnp.where(first_step, 0.0, 1.0) if s == 0 else 1.0
        keep_next = jnp.where(last_step, 0.0, 1.0) if s == n_sub - 1 else 1.0
        for j in range(n_tiles):
            y = _dot(h_s, w_ref[:, j * tn:(j + 1) * tn])
            if j < n_plain:
                o_ref[0, rows, j * tn:(j + 1) * tn] = y[edge:edge + sub].astype(o_ref.dtype)
                continue
            u = jnp.concatenate([y[:edge] * keep_prev, y[edge:edge + sub], y[edge + sub:] * keep_next], axis=0)
            cw = cw_ref[:, (j - n_plain) * tn:(j - n_plain + 1) * tn]
            uc = cb_ref[:, (j - n_plain) * tn:(j - n_plain + 1) * tn]
            n_ext = u.shape[0]
            for k in range(LRU_CONV):
                shifted = u if k == left else pltpu.roll(u, (left - k) % n_ext, 0)
                uc = uc + shifted[edge:edge + sub] * cw[k:k + 1]
            o_ref[0, rows, j * tn:(j + 1) * tn] = uc.astype(o_ref.dtype)


def _combine_inproj_conv(x1, g2, gates, y_sel, tok_offset, g, shift, scale, w, conv_w, conv_b, n_plain_cols):
    b, t, d = x1.shape
    n = w.shape[1]
    tn = NA_WIDTH
    tm = min(t, OUT_TILE)
    hb = tm // SUBLANES
    n_hblocks = t // SUBLANES
    y_hblocks = y_sel.shape[1] // SUBLANES
    assert tok_offset % tm == 0
    kern = functools.partial(_combine_inproj_conv_kernel, n_tiles=n // tn, n_plain=n_plain_cols // tn, tn=tn, tm=tm,
                             sub=min(tm, OUT_SUB_TILE))
    const = lambda bi, i: (0, 0)
    tile = lambda bi, i: (bi, i, 0)
    prev = lambda bi, i: (bi, jnp.maximum(i * hb - 1, 0), 0)
    nxt = lambda bi, i: (bi, jnp.minimum((i + 1) * hb, n_hblocks - 1), 0)
    y_row = lambda bi, i: (tok_offset + bi * t + i * tm) // SUBLANES
    per_b = lambda bi, i: (bi, 0, 0)
    return pl.pallas_call(
        kern,
        out_shape=[jax.ShapeDtypeStruct((b, t, d), F32), jax.ShapeDtypeStruct((b, t, n), BF16)],
        grid=(b, t // tm),
        in_specs=[
            pl.BlockSpec((1, tm, d), tile),
            pl.BlockSpec((1, SUBLANES, d), prev),
            pl.BlockSpec((1, SUBLANES, d), nxt),
            pl.BlockSpec((1, tm, ROUTER_PAD), tile),
            pl.BlockSpec((1, SUBLANES, ROUTER_PAD), prev),
            pl.BlockSpec((1, SUBLANES, ROUTER_PAD), nxt),
            pl.BlockSpec((TOP_K, tm, d // 2), lambda bi, i: (0, y_row(bi, i) // hb, 0)),
            pl.BlockSpec((TOP_K, SUBLANES, d // 2), lambda bi, i: (0, jnp.maximum(y_row(bi, i) - 1, 0), 0)),
            pl.BlockSpec((TOP_K, SUBLANES, d // 2),
                         lambda bi, i: (0, jnp.minimum(y_row(bi, i) + hb, y_hblocks - 1), 0)),
            pl.BlockSpec((1, 1, d), per_b),
            pl.BlockSpec((1, d), const),
            pl.BlockSpec((1, 1, d), per_b),
            pl.BlockSpec((1, 1, d), per_b),
            pl.BlockSpec((d, n), const),
            pl.BlockSpec(conv_w.shape, const),
            pl.BlockSpec((1, conv_w.shape[1]), const),
        ],
        out_specs=[pl.BlockSpec((1, tm, d), tile), pl.BlockSpec((1, tm, n), tile)],
        compiler_params=_params(("parallel", "parallel")),
        name="combine_inproj_conv",
    )(x1, x1, x1, gates, gates, gates, y_sel, y_sel, y_sel, g2, g, shift, scale, w, conv_w, conv_b.reshape(1, -1))


def _na_tables(rows):
    kh = min(WIN_H, rows)
    r = np.arange(rows)
    r0 = np.clip(r - kh // 2, 0, rows - kh)
    dr = r0[:, None] + np.arange(kh)[None] - r[:, None] + WIN_H - 1
    patterns, row_type = np.unique(dr, axis=0, return_inverse=True)
    return kh, r0.astype(np.int32), row_type.reshape(-1).astype(np.int32), patterns


def _na_bias_table(rpb, patterns):
    qc = np.arange(GRID_W)
    kc = np.arange(GRID_W)
    c0 = np.clip(qc - WIN_W // 2, 0, GRID_W - WIN_W)[:, None]
    valid = (kc[None] >= c0) & (kc[None] < c0 + WIN_W)
    dc = np.clip(kc[None] - qc[:, None] + WIN_W - 1, 0, 2 * WIN_W - 2)
    n_pat, kh = patterns.shape
    onehot_dc = jnp.asarray(dc[None] == np.arange(2 * WIN_W - 1)[:, None, None], F32)
    tab = jnp.einsum('hpic,cqk->hpiqk', rpb.astype(F32)[:, patterns], onehot_dc,
                     precision=lax.Precision.HIGHEST)
    tab = jnp.where(valid[None, None, None], tab * LOG2_E, NEG_INF)
    tab = tab.reshape(HEAD_PAIRS, 2, n_pat, kh, GRID_W, GRID_W)
    tab = tab.transpose(2, 0, 1, 4, 3, 5)
    return tab.reshape(n_pat, HEAD_PAIRS, 2 * GRID_W, kh * GRID_W)


def _pair_attention(q, keys, values, biases):
    m = q.shape[0]
    qq = _stack_heads(q)
    scores = []
    for k, bias in zip(keys, biases):
        s = _dot_nt(qq, k)
        scores.append(s if bias is None else s + bias)
    s = jnp.concatenate(scores, axis=1)
    e = jnp.exp2(s - jnp.max(s, axis=-1, keepdims=True))
    denom = jnp.sum(e, axis=-1, keepdims=True)
    e = e.astype(BF16)
    o, start = 0.0, 0
    for v in values:
        o = o + _dot(e[:, start:start + v.shape[0]], v)
        start += v.shape[0]
    o = o * (1.0 / denom)
    lane_o = lax.broadcasted_iota(jnp.int32, (m, LANES), 1)
    return jnp.where(lane_o < NA_HEAD_DIM, o[:m], o[m:])


def _stack_heads(q):
    lane = lax.broadcasted_iota(jnp.int32, q.shape, 1)
    zero = jnp.zeros_like(q)
    return jnp.concatenate([jnp.where(lane < NA_HEAD_DIM, q, zero), jnp.where(lane >= NA_HEAD_DIM, q, zero)], axis=0)


def _na_kernel(r0_ref, type_ref, q_ref, k_ref, v_ref, kc_ref, vc_ref, bias_ref, o_ref, s_ref, p_ref, *,
               kh, rows_per_step):
    n_lat = kh * GRID_W
    tiles = [(j, p) for j in range(rows_per_step) for p in range(HEAD_PAIRS)]
    window = []
    for j in range(rows_per_step):
        r = pl.program_id(1) * rows_per_step + j
        window.append((pl.multiple_of(r0_ref[r] * GRID_W, GRID_W), type_ref[r]))

    for idx, (j, p) in enumerate(tiles):
        start, rtype = window[j]
        cols = slice(p * LANES, (p + 1) * LANES)
        qq = _stack_heads(q_ref[0, j * GRID_W:(j + 1) * GRID_W, cols])
        s_ref[idx, :, :n_lat] = _dot_nt(qq, k_ref[0, pl.ds(start, n_lat), cols]) + bias_ref[rtype, p]
        s_ref[idx, :, n_lat:] = _dot_nt(qq, kc_ref[0, :, cols])

    denoms = []
    for idx in range(len(tiles)):
        s = s_ref[idx]
        e = jnp.exp2(s - jnp.max(s, axis=-1, keepdims=True))
        denoms.append(jnp.sum(e, axis=-1, keepdims=True))
        p_ref[idx] = e.astype(BF16)

    lane = lax.broadcasted_iota(jnp.int32, (GRID_W, LANES), 1)
    for idx, (j, p) in enumerate(tiles):
        start, _ = window[j]
        cols = slice(p * LANES, (p + 1) * LANES)
        o = _dot(p_ref[idx, :, :n_lat], v_ref[0, pl.ds(start, n_lat), cols]) + _dot(p_ref[idx, :, n_lat:],
                                                                                 vc_ref[0, :, cols])
        o = o * (1.0 / denoms[idx])
        o = jnp.where(lane < NA_HEAD_DIM, o[:GRID_W], o[GRID_W:])
        o_ref[0, j * GRID_W:(j + 1) * GRID_W, cols] = o.astype(o_ref.dtype)


def _neighbourhood_attention(proj, proj_c, bias_tab, r0, row_type, kh):
    b, t, _ = proj.shape
    l = proj_c.shape[1]
    rows = t // GRID_W
    w = NA_WIDTH
    rps = int(np.gcd(rows, NA_ROWS_PER_STEP))
    q_rows = rps * GRID_W
    grid_spec = pltpu.PrefetchScalarGridSpec(
        num_scalar_prefetch=2,
        grid=(b, rows // rps),
        in_specs=[
            pl.BlockSpec((1, q_rows, w), lambda bi, r, *_: (bi, r, 0)),
            pl.BlockSpec((1, t, w), lambda bi, r, *_: (bi, 0, 1)),
            pl.BlockSpec((1, t, w), lambda bi, r, *_: (bi, 0, 2)),
            pl.BlockSpec((1, l, w), lambda bi, r, *_: (bi, 0, 1)),
            pl.BlockSpec((1, l, w), lambda bi, r, *_: (bi, 0, 2)),
            pl.BlockSpec(bias_tab.shape, lambda bi, r, *_: (0, 0, 0, 0)),
        ],
        out_specs=pl.BlockSpec((1, q_rows, w), lambda bi, r, *_: (bi, r, 0)),
        scratch_shapes=[pltpu.VMEM((rps * HEAD_PAIRS, 2 * GRID_W, kh * GRID_W + l), F32),
                        pltpu.VMEM((rps * HEAD_PAIRS, 2 * GRID_W, kh * GRID_W + l), BF16)],
    )
    return pl.pallas_call(
        functools.partial(_na_kernel, kh=kh, rows_per_step=rps),
        out_shape=jax.ShapeDtypeStruct((b, t, w), BF16),
        grid_spec=grid_spec,
        compiler_params=_params(("parallel", "arbitrary")),
        name="na_attention",
    )(jnp.asarray(r0), jnp.asarray(row_type), proj, proj, proj, proj_c, proj_c, bias_tab)


def _ctx_attn_kernel(q_ref, k_ref, v_ref, o_ref):
    for p in range(HEAD_PAIRS):
        cols = slice(p * LANES, (p + 1) * LANES)
        o = _pair_attention(q_ref[0, :, cols], [k_ref[0, :, cols]], [v_ref[0, :, cols]], [None])
        o_ref[0, :, cols] = o.astype(o_ref.dtype)


def _context_attention(proj_c):
    b, l, _ = proj_c.shape
    w = NA_WIDTH
    return pl.pallas_call(
        _ctx_attn_kernel,
        out_shape=jax.ShapeDtypeStruct((b, l, w), BF16),
        grid=(b,),
        in_specs=[pl.BlockSpec((1, l, w), lambda bi, j=j: (bi, 0, j)) for j in range(3)],
        out_specs=pl.BlockSpec((1, l, w), lambda bi: (bi, 0, 0)),
        compiler_params=_params(("parallel",)),
        name="ctx_attention",
    )(proj_c, proj_c, proj_c)


def _top4(logits):
    lane = lax.broadcasted_iota(jnp.int32, logits.shape, 1)
    cur = jnp.where(lane < N_EXPERTS, logits, -jnp.inf)
    vals, idxs = [], []
    for _ in range(TOP_K):
        m = jnp.max(cur, axis=-1, keepdims=True)
        first = jnp.min(jnp.where(cur == m, lane, ROUTER_PAD).astype(F32), axis=-1, keepdims=True)
        idx = first.astype(jnp.int32)
        vals.append(m)
        idxs.append(idx)
        cur = jnp.where(lane == idx, -jnp.inf, cur)
    exps = [jnp.exp(v - vals[0]) for v in vals]
    inv = 1.0 / functools.reduce(jnp.add, exps)
    ids = jnp.zeros(logits.shape, jnp.int32)
    gates = jnp.zeros(logits.shape, F32)
    for k in range(TOP_K):
        ids = jnp.where(lane == k, idxs[k], ids)
        gates = jnp.where(lane == k, exps[k] * inv, gates)
    return ids, gates


def _layer_tail(y, rows, x_ref, g1_ref, n2_ref, sh2_ref, sc2_ref, wr_ref, br_ref, *rest):
    xo_ref, h2_ref, ids_ref, gates_ref = rest[-4:]
    x_new = x_ref[0, rows] + g1_ref[0] * y
    xo_ref[0, rows] = x_new
    h2 = _rms_mod(x_new, n2_ref[...], sh2_ref[0], sc2_ref[0])
    h2_ref[rows] = _pack_rows(h2)
    logits = _dot(h2.astype(BF16), wr_ref[...]) + br_ref[...]
    ids, gates = _top4(logits)
    ids_ref[rows] = ids
    gates_ref[0, rows] = gates


def _sub_tiles(n_rows):
    sub = min(OUT_SUB_TILE, n_rows)
    return [slice(s, s + sub) for s in range(0, n_rows, sub)]


def _tail_specs(b, t, d, tm, n_inputs_before, tokens, shared):
    n_total, offset = tokens
    n_t = t // tm
    assert offset % tm == 0
    row = lambda bi, i: (bi, i, 0)
    flat = lambda bi, i: (offset // tm + bi * n_t + i, 0)
    per_b = lambda bi, i: (bi, 0, 0)
    const = lambda bi, i: (0, 0)
    extra, aliases = (), {}
    if shared is not None:
        extra = tuple(shared)
        aliases = {n_inputs_before + 7: 1, n_inputs_before + 8: 2}
    in_specs = [
        pl.BlockSpec((1, tm, d), row),
        pl.BlockSpec((1, 1, d), per_b),
        pl.BlockSpec((1, d), const),
        pl.BlockSpec((1, 1, d), per_b),
        pl.BlockSpec((1, 1, d), per_b),
        pl.BlockSpec((d, ROUTER_PAD), const),
        pl.BlockSpec((1, ROUTER_PAD), const),
    ] + [pl.BlockSpec(memory_space=pl.ANY)] * len(extra)
    out_specs = [
        pl.BlockSpec((1, tm, d), row),
        pl.BlockSpec((tm, d // 2), flat),
        pl.BlockSpec((tm, ROUTER_PAD), flat),
        pl.BlockSpec((1, tm, ROUTER_PAD), row),
    ]
    out_shape = [
        jax.ShapeDtypeStruct((b, t, d), F32),
        jax.ShapeDtypeStruct((n_total, d // 2), jnp.int32),
        jax.ShapeDtypeStruct((n_total, ROUTER_PAD), jnp.int32),
        jax.ShapeDtypeStruct((b, t, ROUTER_PAD), F32),
    ]
    return in_specs, out_specs, out_shape, extra, aliases


def _halo_fix(rolled, at_row, halo_row, present):
    n = rolled.shape[0]
    first = at_row < SUBLANES
    assert first or at_row >= n - SUBLANES
    slab = rolled[:SUBLANES] if first else rolled[n - SUBLANES:]
    sub = lax.broadcasted_iota(jnp.int32, slab.shape, 0)
    fill = jnp.where(present, halo_row, jnp.zeros_like(halo_row))
    slab = jnp.where(sub == at_row % SUBLANES, fill, slab)
    return jnp.concatenate([slab, rolled[SUBLANES:]] if first else [rolled[:n - SUBLANES], slab], axis=0)


def _even_out_kernel(oa_ref, bg_ref, cg_ref, xin_ref, cgp_ref, xinp_ref, cgn_ref, xinn_ref, cw_ref, cb_ref,
                     wa_ref, wb_ref, *tail_refs, tm):
    i = pl.program_id(1)
    has_prev = i > 0
    has_next = i < pl.num_programs(1) - 1
    u = cg_ref[0].astype(F32) * xin_ref[0].astype(F32)
    u_prev = (cgp_ref[0].astype(F32) * xinp_ref[0].astype(F32))[HALO - 1:HALO]
    u_next = (cgn_ref[0].astype(F32) * xinn_ref[0].astype(F32))[0:1]
    u_m1 = _halo_fix(pltpu.roll(u, 1, 0), 0, u_prev, has_prev)
    u_p1 = _halo_fix(pltpu.roll(u, tm - 1, 0), tm - 1, u_next, has_next)
    cw = cw_ref[...]
    conv = u_m1 * cw[0:1] + u * cw[1:2] + u_p1 * cw[2:3] + cb_ref[...]
    o_b = (bg_ref[0].astype(F32) * conv).astype(BF16)
    for rows in _sub_tiles(tm):
        y = _dot(oa_ref[0, rows], wa_ref[...]) + _dot(o_b[rows], wb_ref[...])
        _layer_tail(y, rows, *tail_refs)


def _even_out(o_a, proj, conv_w, conv_b, w_out, x, g1, n2, sh2, sc2, w_r, b_r, tokens, shared=None):
    b, t, d = x.shape
    w = NA_WIDTH
    tm = min(t, OUT_TILE)
    hb = tm // HALO
    n_hblocks = t // HALO
    row = lambda bi, i: (bi, i, 0)
    const = lambda bi, i: (0, 0)
    prev = lambda col: (lambda bi, i: (bi, jnp.maximum(i * hb - 1, 0), col))
    nxt = lambda col: (lambda bi, i: (bi, jnp.minimum((i + 1) * hb, n_hblocks - 1), col))
    tail_in, out_specs, out_shape, extra, aliases = _tail_specs(b, t, d, tm, 12, tokens, shared)
    in_specs = [
        pl.BlockSpec((1, tm, w), row),
        pl.BlockSpec((1, tm, w), lambda bi, i: (bi, i, 3)),
        pl.BlockSpec((1, tm, w), lambda bi, i: (bi, i, 4)),
        pl.BlockSpec((1, tm, w), lambda bi, i: (bi, i, 5)),
        pl.BlockSpec((1, HALO, w), prev(4)),
        pl.BlockSpec((1, HALO, w), prev(5)),
        pl.BlockSpec((1, HALO, w), nxt(4)),
        pl.BlockSpec((1, HALO, w), nxt(5)),
        pl.BlockSpec((SC_CONV, w), const),
        pl.BlockSpec((1, w), const),
        pl.BlockSpec((w, d), const),
        pl.BlockSpec((w, d), const),
    ] + tail_in
    return pl.pallas_call(
        functools.partial(_even_out_kernel, tm=tm),
        out_shape=out_shape,
        grid=(b, t // tm),
        in_specs=in_specs,
        out_specs=out_specs,
        input_output_aliases=aliases,
        compiler_params=_params(("parallel", "parallel")),
        name="even_out",
    )(o_a, proj, proj, proj, proj, proj, proj, proj, conv_w, conv_b.reshape(1, w),
      w_out[:w], w_out[w:], x, g1, n2, sh2, sc2, w_r, b_r, *extra)


def _log_sigmoid(x):
    return jnp.minimum(x, 0.0) - jnp.log1p(jnp.exp(-jnp.abs(x)))


def _sigmoid(x):
    return 0.5 + 0.5 * jnp.tanh(0.5 * x)


def _lru_sub_tile(ucb_time, state, perm_ref, w_ref, ba_ref, bx_ref, lam_ref, want_hidden, reverse):
    sub, width = ucb_time.shape
    blk = width // LRU_BLOCKS
    steps = sub // SUBLANES
    uc = _dot(perm_ref[0], ucb_time)
    ucb = uc.astype(BF16)
    za, zx = [], []
    for h in range(LRU_BLOCKS):
        z = _dot(ucb[:, h * blk:(h + 1) * blk], w_ref[h])
        za.append(z[:, :blk])
        zx.append(z[:, blk:])
    r = _sigmoid(jnp.concatenate(za, axis=1) + ba_ref[...])
    gate_i = _sigmoid(jnp.concatenate(zx, axis=1) + bx_ref[...])
    log_a = (LRU_C * _log_sigmoid(lam_ref[...])) * r
    a = jnp.exp(log_a)
    th = jnp.tanh(log_a)
    num = -2.0 * th
    mult = jnp.where(num > 0.0, num * lax.rsqrt(num * (1.0 - th)), 0.0)
    bcoef = mult * gate_i * uc

    grp = lambda v, j: v[j * SUBLANES:(j + 1) * SUBLANES]
    prods, local = [grp(a, 0)], [grp(bcoef, 0)]
    for j in range(1, steps):
        aj = grp(a, j)
        local.append(aj * local[-1] + grp(bcoef, j))
        prods.append(aj * prods[-1])

    carry_in = [None] * SUBLANES
    for s in (range(SUBLANES - 1, -1, -1) if reverse else range(SUBLANES)):
        carry_in[s] = state
        state = prods[-1][s:s + 1] * state + local[-1][s:s + 1]
    if not want_hidden:
        return state, None
    start = jnp.concatenate(carry_in, axis=0)
    hidden = jnp.concatenate([prods[j] * start + local[j] for j in range(steps)], axis=0)
    return state, _dot(perm_ref[1], hidden.astype(BF16)).astype(BF16)


def _lru_tile(u_ref, perm_ref, w_ref, ba_ref, bx_ref, lam_ref, carry_ref, o_ref, *, reverse):
    sub = perm_ref.shape[1]
    n_sub = u_ref.shape[1] // sub
    state = carry_ref[0:1, :]
    for k in (range(n_sub - 1, -1, -1) if reverse else range(n_sub)):
        rows = slice(k * sub, (k + 1) * sub)
        state, hidden = _lru_sub_tile(u_ref[0, rows], state, perm_ref, w_ref, ba_ref, bx_ref, lam_ref,
                                      o_ref is not None, reverse)
        if o_ref is not None:
            o_ref[0, rows] = hidden
    carry_ref[...] = jnp.broadcast_to(state, carry_ref.shape)


def _lru_kernel(uc_ref, ul_ref, perm_ref, w_ref, ba_ref, bx_ref, lam_ref, o_ref, carry_ref, *, n_ctx_tiles, reverse):
    j = pl.program_id(1)
    shared = (perm_ref, w_ref, ba_ref, bx_ref, lam_ref, carry_ref)

    @pl.when(j == 0)
    def _():
        carry_ref[...] = jnp.zeros_like(carry_ref)

    @pl.when(j < n_ctx_tiles)
    def _():
        _lru_tile(uc_ref, *shared, None, reverse=reverse)

    @pl.when(j >= n_ctx_tiles)
    def _():
        _lru_tile(ul_ref, *shared, o_ref, reverse=reverse)


def _lru_scan(proj, u_ctx, w_cat, ba, bx, lam, reverse):
    b, t, _ = proj.shape
    l, width = u_ctx.shape[1], u_ctx.shape[2]
    sub = min(256, l, t)
    tc = sub * min(LRU_SUB_TILES, l // sub)
    tl = sub * min(LRU_SUB_TILES, t // sub)
    n_c, n_l = l // tc, t // tl

    def pos_of(step, n):
        step = jnp.clip(step, 0, n - 1)
        return (n - 1 - step) if reverse else step

    def tile_map(off, n, col):
        return lambda bi, j: (bi, pos_of(j - off, n), col)

    const2 = lambda bi, j: (0, 0)
    in_specs = [
        pl.BlockSpec((1, tc, width), tile_map(0, n_c, 0)),
        pl.BlockSpec((1, tl, width), tile_map(n_c, n_l, 1)),
        pl.BlockSpec((2, sub, sub), lambda bi, j: (0, 0, 0)),
        pl.BlockSpec(w_cat.shape, lambda bi, j: (0, 0, 0)),
        pl.BlockSpec((1, width), const2),
        pl.BlockSpec((1, width), const2),
        pl.BlockSpec((1, width), const2),
    ]
    steps = sub // SUBLANES
    step, block = np.divmod(np.arange(sub), SUBLANES)
    time_of_row = block * steps + (steps - 1 - step if reverse else step)
    perm = (time_of_row[:, None] == np.arange(sub)[None]).astype(np.float32)
    perms = jnp.asarray(np.stack([perm, perm.T]), BF16)
    kern = functools.partial(_lru_kernel, n_ctx_tiles=n_c, reverse=reverse)
    return pl.pallas_call(
        kern,
        out_shape=jax.ShapeDtypeStruct((b, t, width), BF16),
        grid=(b, n_c + n_l),
        in_specs=in_specs,
        out_specs=pl.BlockSpec((1, tl, width), tile_map(n_c, n_l, 0)),
        scratch_shapes=[pltpu.VMEM((SUBLANES, width), F32)],
        compiler_params=_params(("parallel", "arbitrary")),
        name="lru_scan_bwd" if reverse else "lru_scan_fwd",
    )(u_ctx, proj, perms, w_cat, ba.reshape(1, width), bx.reshape(1, width), lam.reshape(1, width))


def _odd_out_kernel(hf_ref, hb_ref, gate_ref, w_ref, *tail_refs):
    for rows in _sub_tiles(hf_ref.shape[1]):
        hsum = hf_ref[0, rows].astype(F32) + hb_ref[0, rows].astype(F32)
        z = hsum * jax.nn.gelu(gate_ref[0, rows].astype(F32), approximate=True)
        y = _dot(z.astype(BF16), w_ref[...])
        _layer_tail(y, rows, *tail_refs)


def _odd_out(h_f, h_b, proj, w_out, x, g1, n2, sh2, sc2, w_r, b_r):
    b, t, d = x.shape
    width = h_f.shape[-1]
    tm = min(t, OUT_TILE)
    row = lambda bi, i: (bi, i, 0)
    tail_in, out_specs, out_shape, _, _ = _tail_specs(b, t, d, tm, 4, (b * t, 0), None)
    in_specs = [
        pl.BlockSpec((1, tm, width), row),
        pl.BlockSpec((1, tm, width), row),
        pl.BlockSpec((1, tm, width), row),
        pl.BlockSpec((width, d), lambda bi, i: (0, 0)),
    ] + tail_in
    return pl.pallas_call(
        _odd_out_kernel,
        out_shape=out_shape,
        grid=(b, t // tm),
        in_specs=in_specs,
        out_specs=out_specs,
        compiler_params=_params(("parallel", "parallel")),
        name="odd_out",
    )(h_f, h_b, proj, w_out, x, g1, n2, sh2, sc2, w_r, b_r)


def _expert_kernel(be_ref, next_ref, rows_ref, nb_ref, x_ref, wg_hbm, bg_ref, wu_hbm, bu_ref, wd_hbm, bd_ref,
                   o_ref, w_f32, wg_bf, wu_bf, wd_bf, h_bf, sem, *, layer, chunk):
    i = pl.program_id(0)
    expert = be_ref[i]
    used = i < nb_ref[0]

    def weight_copies(e):
        return [pltpu.make_async_copy(src.at[layer, e], w_f32.at[j], sem.at[j])
                for j, src in enumerate((wg_hbm, wu_hbm, wd_hbm))]

    @pl.when(i == 0)
    def _():
        for cp in weight_copies(expert):
            cp.start()

    @pl.when(used & ((i == 0) | (expert != be_ref[jnp.maximum(i - 1, 0)])))
    def _():
        for cp in weight_copies(expert):
            cp.wait()
        wg_bf[...] = w_f32[0].astype(BF16)
        wu_bf[...] = w_f32[1].astype(BF16)
        wd_bf[...] = w_f32[2].astype(BF16)

        @pl.when(next_ref[i] >= 0)
        def _():
            for cp in weight_copies(next_ref[i]):
                cp.start()

    sub = h_bf.shape[0]
    n_sub = x_ref.shape[0] // sub
    n_rows = jnp.where(used, rows_ref[i], 0)
    n_live = (n_rows + sub - 1) // sub

    def live_sub_block(s, carry):
        rows = pl.ds(pl.multiple_of(s * sub, sub), sub)
        xp = x_ref[rows]
        row = lax.broadcasted_iota(jnp.int32, xp.shape, 0) + s * sub
        xp = jnp.where(row < n_rows, xp, 0)
        x = jnp.concatenate(_unpack_rows(xp), axis=1).astype(BF16)
        for c in range(wg_bf.shape[1] // chunk):
            cs = slice(c * chunk, (c + 1) * chunk)
            g = jnp.minimum(_dot(x, wg_bf[:, cs]) + bg_ref[0, 0, :, cs], SWIGLU_LIMIT)
            u = jnp.clip(_dot(x, wu_bf[:, cs]) + bu_ref[0, 0, :, cs], -SWIGLU_LIMIT, SWIGLU_LIMIT)
            h_bf[:, cs] = (g * jax.nn.sigmoid(SWIGLU_ALPHA * g) * (u + 1.0)).astype(BF16)
        o_ref[rows] = _pack_rows(_dot(h_bf[...], wd_bf[...]) + bd_ref[0, 0])
        return carry

    def empty_sub_block(s, carry):
        o_ref[pl.ds(pl.multiple_of(s * sub, sub), sub)] = jnp.zeros((sub, o_ref.shape[1]), o_ref.dtype)
        return carry

    lax.fori_loop(0, n_live, live_sub_block, 0)
    lax.fori_loop(n_live, n_sub, empty_sub_block, 0)


def _experts(xb, block_e, next_e, block_rows, n_used, layer, wg, bg, wu, bu, wd, bd):
    n_slots = xb.shape[0]
    depth, n_e, d, d_exp = wg.shape
    assert d == d_exp
    tm = EXPERT_TILE
    n_blocks = n_slots // tm
    xmap = lambda i, be, ne, br, nb: (jnp.minimum(i, nb[0] - 1), 0)
    bmap = lambda i, be, ne, br, nb: (layer, be[i], 0, 0)
    hbm = pl.BlockSpec(memory_space=pl.ANY)
    grid_spec = pltpu.PrefetchScalarGridSpec(
        num_scalar_prefetch=4,
        grid=(n_blocks,),
        in_specs=[
            pl.BlockSpec((tm, d // 2), xmap),
            hbm,
            pl.BlockSpec((1, 1, 1, d_exp), bmap),
            hbm,
            pl.BlockSpec((1, 1, 1, d_exp), bmap),
            hbm,
            pl.BlockSpec((1, 1, 1, d), bmap),
        ],
        out_specs=pl.BlockSpec((tm, d // 2), lambda i, be, ne, br, nb: (i, 0)),
        scratch_shapes=[pltpu.VMEM((3, d, d_exp), F32), pltpu.VMEM((d, d_exp), BF16), pltpu.VMEM((d, d_exp), BF16),
                        pltpu.VMEM((d_exp, d), BF16), pltpu.VMEM((EXPERT_SUB, d_exp), BF16),
                        pltpu.SemaphoreType.DMA((3,))],
    )
    return pl.pallas_call(
        functools.partial(_expert_kernel, layer=layer, chunk=256),
        out_shape=jax.ShapeDtypeStruct((n_slots, d // 2), jnp.int32),
        grid_spec=grid_spec,
        compiler_params=_params(("arbitrary",)),
        name="experts",
    )(block_e, next_e, block_rows, n_used, xb, wg, bg.reshape(depth, n_e, 1, d_exp), wu,
      bu.reshape(depth, n_e, 1, d_exp), wd, bd.reshape(depth, n_e, 1, d))


def _combine_kernel(x_ref, g2_ref, gates_ref, y_ref, o_ref):
    o_ref[0] = _moe_residual(x_ref[0], g2_ref[0], gates_ref[0], y_ref[...])


def _combine(x, g2, gates, y_sel, tok_offset):
    b, t, d = x.shape
    tm = int(np.gcd(min(t, ROW_TILE), tok_offset)) if tok_offset else min(t, ROW_TILE)
    n_t = t // tm
    off = tok_offset // tm
    row = lambda bi, i: (bi, i, 0)
    return pl.pallas_call(
        _combine_kernel,
        out_shape=jax.ShapeDtypeStruct((b, t, d), F32),
        grid=(b, n_t),
        in_specs=[
            pl.BlockSpec((1, tm, d), row),
            pl.BlockSpec((1, 1, d), lambda bi, i: (bi, 0, 0)),
            pl.BlockSpec((1, tm, ROUTER_PAD), row),
            pl.BlockSpec((TOP_K, tm, d // 2), lambda bi, i: (0, off + bi * n_t + i, 0)),
        ],
        out_specs=pl.BlockSpec((1, tm, d), row),
        input_output_aliases={0: 0},
        compiler_params=_params(("parallel", "parallel")),
        name="moe_combine",
    )(x, g2, gates, y_sel)


def _route_kernel(ids_ref, upper_ref, dest_ref, counts_ref, run_ref, *, tile):
    p = pl.program_id(0)
    i = pl.program_id(1)
    tm = ids_ref.shape[0]
    ids_t = ids_ref[...].T
    expert = lax.broadcasted_iota(jnp.int32, (N_EXPERTS, tm), 0)
    chosen = [ids_t[k:k + 1, :] == expert for k in range(TOP_K)]
    picks = functools.reduce(jnp.add, [c.astype(F32) for c in chosen])
    tile_counts = jnp.sum(picks, axis=1, keepdims=True)

    @pl.when((p == 0) & (i == 0))
    def _():
        run_ref[...] = jnp.zeros_like(run_ref)

    @pl.when(p == 0)
    def _():
        run_ref[...] += tile_counts

    @pl.when((p == 1) & (i == 0))
    def _():
        counts = run_ref[...]
        counts_ref[...] = counts.astype(jnp.int32)
        padded = jnp.floor((counts + (tile - 1)) * (1.0 / tile)) * tile
        row = lax.broadcasted_iota(jnp.int32, counts.shape, 0)
        ends = padded
        for s in (1, 2, 4, 8, 16):
            ends = ends + jnp.where(row >= s, pltpu.roll(ends, s, 0), 0.0)
        run_ref[...] = ends - padded

    @pl.when(p == 1)
    def _():
        before = _dot(picks.astype(BF16), upper_ref[...])
        slot = before + run_ref[:, 0:1]
        rows = [jnp.sum(jnp.where(c, slot, 0.0), axis=0, keepdims=True) for c in chosen]
        rows += [jnp.zeros_like(rows[0])] * (dest_ref.shape[0] - TOP_K)
        dest_ref[...] = jnp.concatenate(rows, axis=0).astype(jnp.int32)
        run_ref[...] += tile_counts


def _route(ids):
    n = ids.shape[0]
    tm = int(np.gcd(n, ROUTE_TILE))
    n_tiles = n // tm
    upper = jnp.asarray(np.triu(np.ones((tm, tm), np.float32), 1), BF16)
    dest, counts = pl.pallas_call(
        functools.partial(_route_kernel, tile=EXPERT_TILE),
        out_shape=[jax.ShapeDtypeStruct((SUBLANES, n), jnp.int32),
                   jax.ShapeDtypeStruct((N_EXPERTS, LANES), jnp.int32)],
        grid=(2, n_tiles),
        in_specs=[pl.BlockSpec((tm, ROUTER_PAD), lambda p, i: (i, 0)),
                  pl.BlockSpec((tm, tm), lambda p, i: (0, 0))],
        out_specs=[pl.BlockSpec((SUBLANES, tm), lambda p, i: (0, i * p)),
                   pl.BlockSpec((N_EXPERTS, LANES), lambda p, i: (0, 0))],
        scratch_shapes=[pltpu.VMEM((N_EXPERTS, LANES), F32)],
        compiler_params=_params(("arbitrary", "arbitrary")),
        name="route",
    )(ids, upper)
    return dest, counts[:, 0]


def _row_gather(table, idx):
    info = plsc.get_sparse_core_info()
    n_cores, n_workers = info.num_cores, info.num_cores * info.num_subcores
    n_rows, width = idx.shape[0], table.shape[1]
    chunk_rows = SC_GATHER_ROWS
    per_worker = n_rows // n_workers
    n_chunks = per_worker // chunk_rows
    assert per_worker * n_workers == n_rows and n_chunks * chunk_rows == per_worker and n_chunks % 2 == 0
    mesh = plsc.VectorSubcoreMesh(core_axis_name="c", subcore_axis_name="s")

    @functools.partial(
        pl.kernel, mesh=mesh,
        out_type=jax.ShapeDtypeStruct((n_rows, width), table.dtype),
        scratch_types=[pltpu.VMEM((n_chunks, chunk_rows), jnp.int32), pltpu.VMEM((2, chunk_rows, width), table.dtype),
                       pltpu.SemaphoreType.DMA((2,)), pltpu.SemaphoreType.DMA((2,))],
    )
    def gather_kernel(table_hbm, idx_hbm, out_hbm, idx_v, rows_v, gather_sem, write_sem):
        worker = lax.axis_index("s") * n_cores + lax.axis_index("c")
        pltpu.sync_copy(idx_hbm.at[worker], idx_v)

        def gather(c, slot):
            return pltpu.make_async_copy(table_hbm.at[idx_v.at[c]], rows_v.at[slot], gather_sem.at[slot])

        def write(c, slot):
            first_row = worker * per_worker + c * chunk_rows
            return pltpu.make_async_copy(rows_v.at[slot], out_hbm.at[pl.ds(first_row, chunk_rows)], write_sem.at[slot])

        gather(0, 0).start()

        @pl.loop(0, n_chunks, step=2)
        def _(c0):
            for slot in (0, 1):
                c = c0 + slot
                gather(c, slot).wait()

                @pl.when(c >= 1)
                def _():
                    write(c - 1, 1 - slot).wait()

                @pl.when(c + 1 < n_chunks)
                def _():
                    gather(c + 1, 1 - slot).start()

                write(c, slot).start()

        write(n_chunks - 1, 1).wait()

    return gather_kernel(table, idx.reshape(n_workers, n_chunks, chunk_rows))


def _row_scatter(rows, dest, n_out):
    info = plsc.get_sparse_core_info()
    n_cores, n_workers = info.num_cores, info.num_cores * info.num_subcores
    n_choices, n_rows = dest.shape
    width = rows.shape[1]
    chunk_rows = SC_GATHER_ROWS
    per_worker = n_rows // (n_workers * chunk_rows)
    assert per_worker * n_workers * chunk_rows == n_rows == rows.shape[0]
    idx = dest.reshape(n_choices, n_workers, per_worker, chunk_rows)
    mesh = plsc.VectorSubcoreMesh(core_axis_name="c", subcore_axis_name="s")

    @functools.partial(
        pl.kernel, mesh=mesh,
        out_type=jax.ShapeDtypeStruct((n_out, width), rows.dtype),
        scratch_types=[pltpu.VMEM((n_choices, per_worker, chunk_rows), jnp.int32),
                       pltpu.VMEM((chunk_rows, width), rows.dtype), pltpu.SemaphoreType.DMA],
    )
    def scatter_kernel(rows_hbm, idx_hbm, out_hbm, idx_v, rows_v, sem):
        worker = lax.axis_index("s") * n_cores + lax.axis_index("c")
        for k in range(n_choices):
            pltpu.sync_copy(idx_hbm.at[k, worker], idx_v.at[k])

        @pl.loop(0, per_worker)
        def _(c):
            first_row = (worker * per_worker + c) * chunk_rows
            pltpu.sync_copy(rows_hbm.at[pl.ds(first_row, chunk_rows)], rows_v)
            copies = [pltpu.async_copy(rows_v, out_hbm.at[idx_v.at[k, c]], sem) for k in range(n_choices)]
            for cp in copies:
                cp.wait()

    return scatter_kernel(rows, idx)


def _moe(h2, ids, layer, w_exp):
    n = h2.shape[0]
    nk = n * TOP_K
    tm = EXPERT_TILE
    dest, counts = _route(ids)
    dest = dest[:TOP_K]
    padded = (counts + tm - 1) // tm * tm
    pends = jnp.cumsum(padded)
    n_blocks = -(-nk // tm) + N_EXPERTS
    n_slots = n_blocks * tm
    block_start = jnp.arange(n_blocks, dtype=jnp.int32) * tm
    block_e = jnp.minimum(jnp.sum((pends[None] <= block_start[:, None]).astype(jnp.int32), axis=1), N_EXPERTS - 1)
    is_e = (block_e[:, None] == jnp.arange(N_EXPERTS, dtype=jnp.int32)[None]).astype(jnp.int32)
    block_rows = jnp.clip(jnp.sum(is_e * (pends - padded + counts)[None], axis=1) - block_start, 0, tm)
    n_used = (pends[-1] // tm).astype(jnp.int32).reshape(1)
    e_ids = jnp.arange(N_EXPERTS, dtype=jnp.int32)
    later = (e_ids[None] > e_ids[:, None]) & (counts[None] > 0)
    next_nonempty = jnp.min(jnp.where(later, e_ids[None], N_EXPERTS), axis=1)
    next_nonempty = jnp.where(next_nonempty == N_EXPERTS, -1, next_nonempty)
    next_e = jnp.sum(is_e * next_nonempty[None], axis=1)
    xb = _row_scatter(h2, dest, n_slots)
    return _experts(xb, block_e, next_e, block_rows, n_used, layer, *w_exp), dest


def _gather_choices(yb, dest, lo, hi):
    return _row_gather(yb, dest[:, lo:hi].reshape(-1)).reshape(TOP_K, hi - lo, -1)


def _mod_parts(mod_l, b):
    d = mod_l.shape[-1] // 6
    lat = [mod_l[:b, k * d:(k + 1) * d].reshape(b, 1, d) for k in range(6)]
    ctx = [jnp.broadcast_to(mod_l[b, k * d:(k + 1) * d].reshape(1, 1, d), (b, 1, d)) for k in range(6)]
    return lat, ctx


def kernel(x, c, ctx, c_ctx, ada_w, ada_b, norm1_g, norm2_g, ev_w_in, ev_w_out, ev_q_gain, ev_k_gain, ev_rpb, ev_conv_w, ev_conv_b, od_w_in, od_w_out, od_conv_w, od_conv_b, od_fwd_wa, od_fwd_ba, od_fwd_wx, od_fwd_bx, od_fwd_lam, od_bwd_wa, od_bwd_ba, od_bwd_wx, od_bwd_bx, od_bwd_lam, router_w, router_b, exp_w_gate, exp_b_gate, exp_w_up, exp_b_up, exp_w_down, exp_b_down):
    b, t, d = x.shape
    l = ctx.shape[1]
    assert ada_w.shape[0] == DEPTH == 2 and t % GRID_W == 0 and t // GRID_W >= WIN_H

    n_rows_c = -(-(b + 1) // SUBLANES) * SUBLANES
    cvec = jnp.zeros((n_rows_c, d), F32).at[:b].set(c).at[b].set(c_ctx)
    mod = _ada_mod(cvec, ada_w, ada_b)

    def router(layer):
        w_r = jnp.zeros((d, ROUTER_PAD), F32).at[:, :N_EXPERTS].set(router_w[layer]).astype(BF16)
        b_r = jnp.zeros((1, ROUTER_PAD), F32).at[0, :N_EXPERTS].set(router_b[layer])
        return w_r, b_r

    w_exp = (exp_w_gate, exp_b_gate, exp_w_up, exp_b_up, exp_w_down, exp_b_down)

    (sh1, sc1, g1, sh2, sc2, g2), (csh1, csc1, cg1, csh2, csc2, cg2) = _mod_parts(mod[0], b)
    n1 = norm1_g[0].reshape(1, d)
    n2 = norm2_g[0].reshape(1, d)
    w_in = ev_w_in[0].astype(BF16)
    w_out = ev_w_out[0].astype(BF16)
    q_scale = NA_HEAD_DIM ** -0.5 * LOG2_E
    head_gain = jnp.stack([jnp.tile(ev_q_gain[0] * q_scale, NA_HEADS), jnp.tile(ev_k_gain[0], NA_HEADS)])
    head_gain = head_gain.reshape(2, 1, NA_WIDTH).astype(F32)
    proj = _inproj(x, n1, sh1, sc1, w_in, head_gain)
    proj_c = _inproj(ctx, n1, csh1, csc1, w_in, head_gain)
    kh, r0, row_type, patterns = _na_tables(t // GRID_W)
    bias_tab = _na_bias_table(ev_rpb[0], patterns)
    o_a = _neighbourhood_attention(proj, proj_c, bias_tab, r0, row_type, kh)
    oc_a = _context_attention(proj_c)
    w_r, b_r = router(0)
    n_tok = b * (l + t)
    fresh = (jnp.zeros((n_tok, d // 2), jnp.int32), jnp.zeros((n_tok, ROUTER_PAD), jnp.int32))
    c1, tokens, ids_all, gates_c = _even_out(oc_a, proj_c, ev_conv_w[0], ev_conv_b[0], w_out, ctx, cg1, n2, csh2,
                                             csc2, w_r, b_r, (n_tok, 0), fresh)
    x1, tokens, ids_all, gates = _even_out(o_a, proj, ev_conv_w[0], ev_conv_b[0], w_out, x, g1, n2, sh2, sc2, w_r,
                                           b_r, (n_tok, b * l), (tokens, ids_all))
    yb, dest = _moe(tokens, ids_all, 0, w_exp)
    y_sel = _gather_choices(yb, dest, 0, b * (l + t))
    g2_prev, cg2_prev = g2, cg2

    (sh1, sc1, g1, sh2, sc2, g2), (csh1, csc1, _, _, _, _) = _mod_parts(mod[1], b)
    n1 = norm1_g[1].reshape(1, d)
    n2 = norm2_g[1].reshape(1, d)
    w_in = od_w_in[0].astype(BF16)
    width = w_in.shape[1] // 2
    x, proj = _combine_inproj_conv(x1, g2_prev, gates, y_sel, b * l, n1, sh1, sc1, w_in, od_conv_w[0], od_conv_b[0],
                                   width)
    _, u_ctx = _combine_inproj_conv(c1, cg2_prev, gates_c, y_sel, 0, n1, csh1, csc1, w_in[:, width:], od_conv_w[0],
                                    od_conv_b[0], 0)
    h_dir = []
    for reverse, (wa, ba, wx, bx, lam) in ((False, (od_fwd_wa, od_fwd_ba, od_fwd_wx, od_fwd_bx, od_fwd_lam)),
                                           (True, (od_bwd_wa, od_bwd_ba, od_bwd_wx, od_bwd_bx, od_bwd_lam))):
        w_cat = jnp.concatenate([wa[0], wx[0]], axis=-1).astype(BF16)
        h_dir.append(_lru_scan(proj, u_ctx, w_cat, ba[0], bx[0], lam[0], reverse))
    w_r, b_r = router(1)
    x1, h2, ids, gates = _odd_out(h_dir[0], h_dir[1], proj, od_w_out[0].astype(BF16), x, g1, n2, sh2, sc2, w_r, b_r)
    yb, dest = _moe(h2, ids, 1, w_exp)
    return _combine(x1, g2, gates, _gather_choices(yb, dest, 0, b * t), 0)
```

```python
import functools

import numpy as np
import jax
import jax.numpy as jnp
from jax import lax
from jax.experimental import pallas as pl
from jax.experimental.pallas import tpu as pltpu
from jax.experimental.pallas import tpu_sc as plsc

DEPTH = 2
GRID_W = 64
EPS = 1e-6
NEG_INF = -1e30
LOG2_E = 1.4426950408889634
NA_HEADS = 8
NA_HEAD_DIM = 64
NA_WIDTH = NA_HEADS * NA_HEAD_DIM
HEAD_PAIRS = NA_HEADS // 2
WIN_H = 8
WIN_W = 16
SC_CONV = 3
LRU_BLOCKS = 4
LRU_CONV = 4
LRU_C = 8.0
N_EXPERTS = 32
TOP_K = 4
SWIGLU_LIMIT = 7.0
SWIGLU_ALPHA = 1.702

LANES = 128
SUBLANES = 8
MXU_DEPTH = 256
HALO = 16
ROUTER_PAD = LANES
ADA_COL_TILE = 1536
ROW_TILE = 1024
NA_ROWS_PER_STEP = 4
LRU_SUB_TILES = 4
OUT_TILE = 1024
OUT_SUB_TILE = 512
ROUTE_TILE = 2048
EXPERT_TILE = 2048
EXPERT_SUB = 256
SC_GATHER_ROWS = 64
VMEM_LIMIT = 56 * 1024 * 1024

F32 = jnp.float32
BF16 = jnp.bfloat16


def _params(sem, vmem=VMEM_LIMIT):
    return pltpu.CompilerParams(dimension_semantics=sem, vmem_limit_bytes=vmem)


def _dot(a, b):
    return jnp.dot(a, b, preferred_element_type=F32)


def _dot_nt(a, b):
    return lax.dot_general(a, b, (((1,), (1,)), ((), ())), preferred_element_type=F32)


def _pack_rows(v):
    w = v.shape[-1] // 2
    lo = lax.bitcast_convert_type(v[:, :w].astype(BF16).astype(F32), jnp.int32)
    hi = lax.bitcast_convert_type(v[:, w:].astype(BF16).astype(F32), jnp.int32)
    return lax.shift_right_logical(lo, 16) | (hi & jnp.int32(-65536))


def _unpack_rows(p):
    lo = lax.bitcast_convert_type(lax.shift_left(p, 16), F32)
    hi = lax.bitcast_convert_type(p & jnp.int32(-65536), F32)
    return lo, hi


def _rms_mod(x, g, shift, scale):
    ms = jnp.mean(x * x, axis=-1, keepdims=True)
    return (x * lax.rsqrt(ms + EPS)) * (g * (1.0 + scale)) + shift


def _ada_kernel(c_ref, w_ref, b_ref, o_ref):
    c = c_ref[...]
    s = (c * jax.nn.sigmoid(c)).astype(BF16)
    o_ref[0] = _dot(s, w_ref[0].astype(BF16)) + b_ref[0]


def _ada_mod(cvec, ada_w, ada_b):
    depth, d, n = ada_w.shape
    r = cvec.shape[0]
    tn = ADA_COL_TILE
    return pl.pallas_call(
        _ada_kernel,
        out_shape=jax.ShapeDtypeStruct((depth, r, n), F32),
        grid=(depth, n // tn),
        in_specs=[
            pl.BlockSpec((r, d), lambda l, j: (0, 0)),
            pl.BlockSpec((1, d, tn), lambda l, j: (l, 0, j)),
            pl.BlockSpec((1, 1, tn), lambda l, j: (l, 0, j)),
        ],
        out_specs=pl.BlockSpec((1, r, tn), lambda l, j: (l, 0, j)),
        compiler_params=_params(("parallel", "parallel")),
        name="ada_mod",
    )(cvec, ada_w, ada_b.reshape(depth, 1, n))


def _inproj_kernel(x_ref, g_ref, sh_ref, sc_ref, w_ref, hg_ref, ones_ref, o_ref, *, n_tiles, tn, n_headnorm):
    h = _rms_mod(x_ref[0], g_ref[...], sh_ref[0], sc_ref[0]).astype(BF16)
    for j in range(n_tiles):
        y = _dot(h, w_ref[:, j * tn:(j + 1) * tn])
        if j < n_headnorm:
            ysq = (y * y).astype(BF16)
            kw = ones_ref.shape[0]
            ms = jnp.concatenate([_dot(ysq[:, c:c + kw], ones_ref[...]) for c in range(0, tn, kw)], axis=1)
            y = y * lax.rsqrt(ms * (1.0 / NA_HEAD_DIM) + EPS) * hg_ref[j]
        o_ref[0, :, j * tn:(j + 1) * tn] = y.astype(o_ref.dtype)


def _inproj(x, g, shift, scale, w, head_gain):
    b, t, d = x.shape
    n = w.shape[1]
    tn = NA_WIDTH
    tm = min(t, ROW_TILE)
    n_headnorm = head_gain.shape[0]
    hid = np.arange(MXU_DEPTH) // NA_HEAD_DIM
    ones_bd = jnp.asarray((hid[:, None] == hid[None, :]), BF16)
    kern = functools.partial(_inproj_kernel, n_tiles=n // tn, tn=tn, n_headnorm=n_headnorm)
    return pl.pallas_call(
        kern,
        out_shape=jax.ShapeDtypeStruct((b, t, n), BF16),
        grid=(b, t // tm),
        in_specs=[
            pl.BlockSpec((1, tm, d), lambda bi, i: (bi, i, 0)),
            pl.BlockSpec((1, d), lambda bi, i: (0, 0)),
            pl.BlockSpec((1, 1, d), lambda bi, i: (bi, 0, 0)),
            pl.BlockSpec((1, 1, d), lambda bi, i: (bi, 0, 0)),
            pl.BlockSpec((d, n), lambda bi, i: (0, 0)),
            pl.BlockSpec(head_gain.shape, lambda bi, i: (0, 0, 0)),
            pl.BlockSpec((MXU_DEPTH, MXU_DEPTH), lambda bi, i: (0, 0)),
        ],
        out_specs=pl.BlockSpec((1, tm, n), lambda bi, i: (bi, i, 0)),
        compiler_params=_params(("parallel", "parallel")),
        name="inproj",
    )(x, g, shift, scale, w, head_gain, ones_bd)


def _moe_residual(x, g2, gates, y):
    acc_lo = acc_hi = 0.0
    for k in range(TOP_K):
        lo, hi = _unpack_rows(y[k])
        acc_lo = acc_lo + gates[:, k:k + 1] * lo
        acc_hi = acc_hi + gates[:, k:k + 1] * hi
    return x + g2 * jnp.concatenate([acc_lo, acc_hi], axis=1)


def _combine_inproj_conv_kernel(x_ref, xp_ref, xn_ref, gt_ref, gtp_ref, gtn_ref, y_ref, yp_ref, yn_ref, g2_ref,
                                g_ref, sh_ref, sc_ref, w_ref, cw_ref, cb_ref, xo_ref, o_ref, *,
                                n_tiles, n_plain, tn, tm, sub):
    i = pl.program_id(1)
    edge = SUBLANES
    g2 = g2_ref[0]
    x_tile = _moe_residual(x_ref[0], g2, gt_ref[0], y_ref[...])
    xo_ref[0] = x_tile
    x_ext = jnp.concatenate([_moe_residual(xp_ref[0], g2, gtp_ref[0], yp_ref[...]), x_tile,
                             _moe_residual(xn_ref[0], g2, gtn_ref[0], yn_ref[...])], axis=0)
    h = _rms_mod(x_ext, g_ref[...], sh_ref[0], sc_ref[0]).astype(BF16)
    first_step = i == 0
    last_step = i == pl.num_programs(1) - 1
    left = (LRU_CONV - 1) // 2
    n_sub = tm // sub
    for s in range(n_sub):
        h_s = h[s * sub:(s + 1) * sub + 2 * edge]
        rows = slice(s * sub, (s + 1) * sub)
        keep_prev = jnp.where(first_step, 0.0, 1.0) if s == 0 else 1.0
        keep_next = jnp.where(last_step, 0.0, 1.0) if s == n_sub - 1 else 1.0
        for j in range(n_tiles):
            y = _dot(h_s, w_ref[:, j * tn:(j + 1) * tn])
            if j < n_plain:
                o_ref[0, rows, j * tn:(j + 1) * tn] = y[edge:edge + sub].astype(o_ref.dtype)
                continue
            u = jnp.concatenate([y[:edge] * keep_prev, y[edge:edge + sub], y[edge + sub:] * keep_next], axis=0)
            cw = cw_ref[:, (j - n_plain) * tn:(j - n_plain + 1) * tn]
            uc = cb_ref[:, (j - n_plain) * tn:(j - n_plain + 1) * tn]
            n_ext = u.shape[0]
            for k in range(LRU_CONV):
                shifted = u if k == left else pltpu.roll(u, (left - k) % n_ext, 0)
                uc = uc + shifted[edge:edge + sub] * cw[k:k + 1]
            o_ref[0, rows, j * tn:(j + 1) * tn] = uc.astype(o_ref.dtype)


def _combine_inproj_conv(x1, g2, gates, y_sel, tok_offset, g, shift, scale, w, conv_w, conv_b, n_plain_cols):
    b, t, d = x1.shape
    n = w.shape[1]
    tn = NA_WIDTH
    tm = min(t, OUT_TILE)
    hb = tm // SUBLANES
    n_hblocks = t // SUBLANES
    y_hblocks = y_sel.shape[1] // SUBLANES
    assert tok_offset % tm == 0
    kern = functools.partial(_combine_inproj_conv_kernel, n_tiles=n // tn, n_plain=n_plain_cols // tn, tn=tn, tm=tm,
                             sub=min(tm, OUT_SUB_TILE))
    const = lambda bi, i: (0, 0)
    tile = lambda bi, i: (bi, i, 0)
    prev = lambda bi, i: (bi, jnp.maximum(i * hb - 1, 0), 0)
    nxt = lambda bi, i: (bi, jnp.minimum((i + 1) * hb, n_hblocks - 1), 0)
    y_row = lambda bi, i: (tok_offset + bi * t + i * tm) // SUBLANES
    per_b = lambda bi, i: (bi, 0, 0)
    return pl.pallas_call(
        kern,
        out_shape=[jax.ShapeDtypeStruct((b, t, d), F32), jax.ShapeDtypeStruct((b, t, n), BF16)],
        grid=(b, t // tm),
        in_specs=[
            pl.BlockSpec((1, tm, d), tile),
            pl.BlockSpec((1, SUBLANES, d), prev),
            pl.BlockSpec((1, SUBLANES, d), nxt),
            pl.BlockSpec((1, tm, ROUTER_PAD), tile),
            pl.BlockSpec((1, SUBLANES, ROUTER_PAD), prev),
            pl.BlockSpec((1, SUBLANES, ROUTER_PAD), nxt),
            pl.BlockSpec((TOP_K, tm, d // 2), lambda bi, i: (0, y_row(bi, i) // hb, 0)),
            pl.BlockSpec((TOP_K, SUBLANES, d // 2), lambda bi, i: (0, jnp.maximum(y_row(bi, i) - 1, 0), 0)),
            pl.BlockSpec((TOP_K, SUBLANES, d // 2),
                         lambda bi, i: (0, jnp.minimum(y_row(bi, i) + hb, y_hblocks - 1), 0)),
            pl.BlockSpec((1, 1, d), per_b),
            pl.BlockSpec((1, d), const),
            pl.BlockSpec((1, 1, d), per_b),
            pl.BlockSpec((1, 1, d), per_b),
            pl.BlockSpec((d, n), const),
            pl.BlockSpec(conv_w.shape, const),
            pl.BlockSpec((1, conv_w.shape[1]), const),
        ],
        out_specs=[pl.BlockSpec((1, tm, d), tile), pl.BlockSpec((1, tm, n), tile)],
        compiler_params=_params(("parallel", "parallel")),
        name="combine_inproj_conv",
    )(x1, x1, x1, gates, gates, gates, y_sel, y_sel, y_sel, g2, g, shift, scale, w, conv_w, conv_b.reshape(1, -1))


def _na_tables(rows):
    kh = min(WIN_H, rows)
    r = np.arange(rows)
    r0 = np.clip(r - kh // 2, 0, rows - kh)
    dr = r0[:, None] + np.arange(kh)[None] - r[:, None] + WIN_H - 1
    patterns, row_type = np.unique(dr, axis=0, return_inverse=True)
    return kh, r0.astype(np.int32), row_type.reshape(-1).astype(np.int32), patterns


def _na_bias_table(rpb, patterns):
    qc = np.arange(GRID_W)
    kc = np.arange(GRID_W)
    c0 = np.clip(qc - WIN_W // 2, 0, GRID_W - WIN_W)[:, None]
    valid = (kc[None] >= c0) & (kc[None] < c0 + WIN_W)
    dc = np.clip(kc[None] - qc[:, None] + WIN_W - 1, 0, 2 * WIN_W - 2)
    n_pat, kh = patterns.shape
    onehot_dc = jnp.asarray(dc[None] == np.arange(2 * WIN_W - 1)[:, None, None], F32)
    tab = jnp.einsum('hpic,cqk->hpiqk', rpb.astype(F32)[:, patterns], onehot_dc,
                     precision=lax.Precision.HIGHEST)
    tab = jnp.where(valid[None, None, None], tab * LOG2_E, NEG_INF)
    tab = tab.reshape(HEAD_PAIRS, 2, n_pat, kh, GRID_W, GRID_W)
    tab = tab.transpose(2, 0, 1, 4, 3, 5)
    return tab.reshape(n_pat, HEAD_PAIRS, 2 * GRID_W, kh * GRID_W)


def _pair_attention(q, keys, values, biases):
    m = q.shape[0]
    qq = _stack_heads(q)
    scores = []
    for k, bias in zip(keys, biases):
        s = _dot_nt(qq, k)
        scores.append(s if bias is None else s + bias)
    s = jnp.concatenate(scores, axis=1)
    e = jnp.exp2(s - jnp.max(s, axis=-1, keepdims=True))
    denom = jnp.sum(e, axis=-1, keepdims=True)
    e = e.astype(BF16)
    o, start = 0.0, 0
    for v in values:
        o = o + _dot(e[:, start:start + v.shape[0]], v)
        start += v.shape[0]
    o = o * (1.0 / denom)
    lane_o = lax.broadcasted_iota(jnp.int32, (m, LANES), 1)
    return jnp.where(lane_o < NA_HEAD_DIM, o[:m], o[m:])


def _stack_heads(q):
    lane = lax.broadcasted_iota(jnp.int32, q.shape, 1)
    zero = jnp.zeros_like(q)
    return jnp.concatenate([jnp.where(lane < NA_HEAD_DIM, q, zero), jnp.where(lane >= NA_HEAD_DIM, q, zero)], axis=0)


def _na_kernel(r0_ref, type_ref, q_ref, k_ref, v_ref, kc_ref, vc_ref, bias_ref, o_ref, s_ref, p_ref, *,
               kh, rows_per_step):
    n_lat = kh * GRID_W
    tiles = [(j, p) for j in range(rows_per_step) for p in range(HEAD_PAIRS)]
    window = []
    for j in range(rows_per_step):
        r = pl.program_id(1) * rows_per_step + j
        window.append((pl.multiple_of(r0_ref[r] * GRID_W, GRID_W), type_ref[r]))

    for idx, (j, p) in enumerate(tiles):
        start, rtype = window[j]
        cols = slice(p * LANES, (p + 1) * LANES)
        qq = _stack_heads(q_ref[0, j * GRID_W:(j + 1) * GRID_W, cols])
        s_ref[idx, :, :n_lat] = _dot_nt(qq, k_ref[0, pl.ds(start, n_lat), cols]) + bias_ref[rtype, p]
        s_ref[idx, :, n_lat:] = _dot_nt(qq, kc_ref[0, :, cols])

    denoms = []
    for idx in range(len(tiles)):
        s = s_ref[idx]
        e = jnp.exp2(s - jnp.max(s, axis=-1, keepdims=True))
        denoms.append(jnp.sum(e, axis=-1, keepdims=True))
        p_ref[idx] = e.astype(BF16)

    lane = lax.broadcasted_iota(jnp.int32, (GRID_W, LANES), 1)
    for idx, (j, p) in enumerate(tiles):
        start, _ = window[j]
        cols = slice(p * LANES, (p + 1) * LANES)
        o = _dot(p_ref[idx, :, :n_lat], v_ref[0, pl.ds(start, n_lat), cols]) + _dot(p_ref[idx, :, n_lat:],
                                                                                 vc_ref[0, :, cols])
        o = o * (1.0 / denoms[idx])
        o = jnp.where(lane < NA_HEAD_DIM, o[:GRID_W], o[GRID_W:])
        o_ref[0, j * GRID_W:(j + 1) * GRID_W, cols] = o.astype(o_ref.dtype)


def _neighbourhood_attention(proj, proj_c, bias_tab, r0, row_type, kh):
    b, t, _ = proj.shape
    l = proj_c.shape[1]
    rows = t // GRID_W
    w = NA_WIDTH
    rps = int(np.gcd(rows, NA_ROWS_PER_STEP))
    q_rows = rps * GRID_W
    grid_spec = pltpu.PrefetchScalarGridSpec(
        num_scalar_prefetch=2,
        grid=(b, rows // rps),
        in_specs=[
            pl.BlockSpec((1, q_rows, w), lambda bi, r, *_: (bi, r, 0)),
            pl.BlockSpec((1, t, w), lambda bi, r, *_: (bi, 0, 1)),
            pl.BlockSpec((1, t, w), lambda bi, r, *_: (bi, 0, 2)),
            pl.BlockSpec((1, l, w), lambda bi, r, *_: (bi, 0, 1)),
            pl.BlockSpec((1, l, w), lambda bi, r, *_: (bi, 0, 2)),
            pl.BlockSpec(bias_tab.shape, lambda bi, r, *_: (0, 0, 0, 0)),
        ],
        out_specs=pl.BlockSpec((1, q_rows, w), lambda bi, r, *_: (bi, r, 0)),
        scratch_shapes=[pltpu.VMEM((rps * HEAD_PAIRS, 2 * GRID_W, kh * GRID_W + l), F32),
                        pltpu.VMEM((rps * HEAD_PAIRS, 2 * GRID_W, kh * GRID_W + l), BF16)],
    )
    return pl.pallas_call(
        functools.partial(_na_kernel, kh=kh, rows_per_step=rps),
        out_shape=jax.ShapeDtypeStruct((b, t, w), BF16),
        grid_spec=grid_spec,
        compiler_params=_params(("parallel", "arbitrary")),
        name="na_attention",
    )(jnp.asarray(r0), jnp.asarray(row_type), proj, proj, proj, proj_c, proj_c, bias_tab)


def _ctx_attn_kernel(q_ref, k_ref, v_ref, o_ref):
    for p in range(HEAD_PAIRS):
        cols = slice(p * LANES, (p + 1) * LANES)
        o = _pair_attention(q_ref[0, :, cols], [k_ref[0, :, cols]], [v_ref[0, :, cols]], [None])
        o_ref[0, :, cols] = o.astype(o_ref.dtype)


def _context_attention(proj_c):
    b, l, _ = proj_c.shape
    w = NA_WIDTH
    return pl.pallas_call(
        _ctx_attn_kernel,
        out_shape=jax.ShapeDtypeStruct((b, l, w), BF16),
        grid=(b,),
        in_specs=[pl.BlockSpec((1, l, w), lambda bi, j=j: (bi, 0, j)) for j in range(3)],
        out_specs=pl.BlockSpec((1, l, w), lambda bi: (bi, 0, 0)),
        compiler_params=_params(("parallel",)),
        name="ctx_attention",
    )(proj_c, proj_c, proj_c)


def _top4(logits):
    lane = lax.broadcasted_iota(jnp.int32, logits.shape, 1)
    cur = jnp.where(lane < N_EXPERTS, logits, -jnp.inf)
    vals, idxs = [], []
    for _ in range(TOP_K):
        m = jnp.max(cur, axis=-1, keepdims=True)
        first = jnp.min(jnp.where(cur == m, lane, ROUTER_PAD).astype(F32), axis=-1, keepdims=True)
        idx = first.astype(jnp.int32)
        vals.append(m)
        idxs.append(idx)
        cur = jnp.where(lane == idx, -jnp.inf, cur)
    exps = [jnp.exp(v - vals[0]) for v in vals]
    inv = 1.0 / functools.reduce(jnp.add, exps)
    ids = jnp.zeros(logits.shape, jnp.int32)
    gates = jnp.zeros(logits.shape, F32)
    for k in range(TOP_K):
        ids = jnp.where(lane == k, idxs[k], ids)
        gates = jnp.where(lane == k, exps[k] * inv, gates)
    return ids, gates


def _layer_tail(y, rows, x_ref, g1_ref, n2_ref, sh2_ref, sc2_ref, wr_ref, br_ref, *rest):
    xo_ref, h2_ref, ids_ref, gates_ref = rest[-4:]
    x_new = x_ref[0, rows] + g1_ref[0] * y
    xo_ref[0, rows] = x_new
    h2 = _rms_mod(x_new, n2_ref[...], sh2_ref[0], sc2_ref[0])
    h2_ref[rows] = _pack_rows(h2)
    logits = _dot(h2.astype(BF16), wr_ref[...]) + br_ref[...]
    ids, gates = _top4(logits)
    ids_ref[rows] = ids
    gates_ref[0, rows] = gates


def _sub_tiles(n_rows):
    sub = min(OUT_SUB_TILE, n_rows)
    return [slice(s, s + sub) for s in range(0, n_rows, sub)]


def _tail_specs(b, t, d, tm, n_inputs_before, tokens, shared):
    n_total, offset = tokens
    n_t = t // tm
    assert offset % tm == 0
    row = lambda bi, i: (bi, i, 0)
    flat = lambda bi, i: (offset // tm + bi * n_t + i, 0)
    per_b = lambda bi, i: (bi, 0, 0)
    const = lambda bi, i: (0, 0)
    extra, aliases = (), {}
    if shared is not None:
        extra = tuple(shared)
        aliases = {n_inputs_before + 7: 1, n_inputs_before + 8: 2}
    in_specs = [
        pl.BlockSpec((1, tm, d), row),
        pl.BlockSpec((1, 1, d), per_b),
        pl.BlockSpec((1, d), const),
        pl.BlockSpec((1, 1, d), per_b),
        pl.BlockSpec((1, 1, d), per_b),
        pl.BlockSpec((d, ROUTER_PAD), const),
        pl.BlockSpec((1, ROUTER_PAD), const),
    ] + [pl.BlockSpec(memory_space=pl.ANY)] * len(extra)
    out_specs = [
        pl.BlockSpec((1, tm, d), row),
        pl.BlockSpec((tm, d // 2), flat),
        pl.BlockSpec((tm, ROUTER_PAD), flat),
        pl.BlockSpec((1, tm, ROUTER_PAD), row),
    ]
    out_shape = [
        jax.ShapeDtypeStruct((b, t, d), F32),
        jax.ShapeDtypeStruct((n_total, d // 2), jnp.int32),
        jax.ShapeDtypeStruct((n_total, ROUTER_PAD), jnp.int32),
        jax.ShapeDtypeStruct((b, t, ROUTER_PAD), F32),
    ]
    return in_specs, out_specs, out_shape, extra, aliases


def _halo_fix(rolled, at_row, halo_row, present):
    n = rolled.shape[0]
    first = at_row < SUBLANES
    assert first or at_row >= n - SUBLANES
    slab = rolled[:SUBLANES] if first else rolled[n - SUBLANES:]
    sub = lax.broadcasted_iota(jnp.int32, slab.shape, 0)
    fill = jnp.where(present, halo_row, jnp.zeros_like(halo_row))
    slab = jnp.where(sub == at_row % SUBLANES, fill, slab)
    return jnp.concatenate([slab, rolled[SUBLANES:]] if first else [rolled[:n - SUBLANES], slab], axis=0)


def _even_out_kernel(oa_ref, bg_ref, cg_ref, xin_ref, cgp_ref, xinp_ref, cgn_ref, xinn_ref, cw_ref, cb_ref,
                     wa_ref, wb_ref, *tail_refs, tm):
    i = pl.program_id(1)
    has_prev = i > 0
    has_next = i < pl.num_programs(1) - 1
    u = cg_ref[0].astype(F32) * xin_ref[0].astype(F32)
    u_prev = (cgp_ref[0].astype(F32) * xinp_ref[0].astype(F32))[HALO - 1:HALO]
    u_next = (cgn_ref[0].astype(F32) * xinn_ref[0].astype(F32))[0:1]
    u_m1 = _halo_fix(pltpu.roll(u, 1, 0), 0, u_prev, has_prev)
    u_p1 = _halo_fix(pltpu.roll(u, tm - 1, 0), tm - 1, u_next, has_next)
    cw = cw_ref[...]
    conv = u_m1 * cw[0:1] + u * cw[1:2] + u_p1 * cw[2:3] + cb_ref[...]
    o_b = (bg_ref[0].astype(F32) * conv).astype(BF16)
    for rows in _sub_tiles(tm):
        y = _dot(oa_ref[0, rows], wa_ref[...]) + _dot(o_b[rows], wb_ref[...])
        _layer_tail(y, rows, *tail_refs)


def _even_out(o_a, proj, conv_w, conv_b, w_out, x, g1, n2, sh2, sc2, w_r, b_r, tokens, shared=None):
    b, t, d = x.shape
    w = NA_WIDTH
    tm = min(t, OUT_TILE)
    hb = tm // HALO
    n_hblocks = t // HALO
    row = lambda bi, i: (bi, i, 0)
    const = lambda bi, i: (0, 0)
    prev = lambda col: (lambda bi, i: (bi, jnp.maximum(i * hb - 1, 0), col))
    nxt = lambda col: (lambda bi, i: (bi, jnp.minimum((i + 1) * hb, n_hblocks - 1), col))
    tail_in, out_specs, out_shape, extra, aliases = _tail_specs(b, t, d, tm, 12, tokens, shared)
    in_specs = [
        pl.BlockSpec((1, tm, w), row),
        pl.BlockSpec((1, tm, w), lambda bi, i: (bi, i, 3)),
        pl.BlockSpec((1, tm, w), lambda bi, i: (bi, i, 4)),
        pl.BlockSpec((1, tm, w), lambda bi, i: (bi, i, 5)),
        pl.BlockSpec((1, HALO, w), prev(4)),
        pl.BlockSpec((1, HALO, w), prev(5)),
        pl.BlockSpec((1, HALO, w), nxt(4)),
        pl.BlockSpec((1, HALO, w), nxt(5)),
        pl.BlockSpec((SC_CONV, w), const),
        pl.BlockSpec((1, w), const),
        pl.BlockSpec((w, d), const),
        pl.BlockSpec((w, d), const),
    ] + tail_in
    return pl.pallas_call(
        functools.partial(_even_out_kernel, tm=tm),
        out_shape=out_shape,
        grid=(b, t // tm),
        in_specs=in_specs,
        out_specs=out_specs,
        input_output_aliases=aliases,
        compiler_params=_params(("parallel", "parallel")),
        name="even_out",
    )(o_a, proj, proj, proj, proj, proj, proj, proj, conv_w, conv_b.reshape(1, w),
      w_out[:w], w_out[w:], x, g1, n2, sh2, sc2, w_r, b_r, *extra)


def _log_sigmoid(x):
    return jnp.minimum(x, 0.0) - jnp.log1p(jnp.exp(-jnp.abs(x)))


def _sigmoid(x):
    return 0.5 + 0.5 * jnp.tanh(0.5 * x)


def _lru_sub_tile(ucb_time, state, perm_ref, w_ref, ba_ref, bx_ref, lam_ref, want_hidden, reverse):
    sub, width = ucb_time.shape
    blk = width // LRU_BLOCKS
    steps = sub // SUBLANES
    uc = _dot(perm_ref[0], ucb_time)
    ucb = uc.astype(BF16)
    za, zx = [], []
    for h in range(LRU_BLOCKS):
        z = _dot(ucb[:, h * blk:(h + 1) * blk], w_ref[h])
        za.append(z[:, :blk])
        zx.append(z[:, blk:])
    r = _sigmoid(jnp.concatenate(za, axis=1) + ba_ref[...])
    gate_i = _sigmoid(jnp.concatenate(zx, axis=1) + bx_ref[...])
    log_a = (LRU_C * _log_sigmoid(lam_ref[...])) * r
    a = jnp.exp(log_a)
    th = jnp.tanh(log_a)
    num = -2.0 * th
    mult = jnp.where(num > 0.0, num * lax.rsqrt(num * (1.0 - th)), 0.0)
    bcoef = mult * gate_i * uc

    grp = lambda v, j: v[j * SUBLANES:(j + 1) * SUBLANES]
    prods, local = [grp(a, 0)], [grp(bcoef, 0)]
    for j in range(1, steps):
        aj = grp(a, j)
        local.append(aj * local[-1] + grp(bcoef, j))
        prods.append(aj * prods[-1])

    carry_in = [None] * SUBLANES
    for s in (range(SUBLANES - 1, -1, -1) if reverse else range(SUBLANES)):
        carry_in[s] = state
        state = prods[-1][s:s + 1] * state + local[-1][s:s + 1]
    if not want_hidden:
        return state, None
    start = jnp.concatenate(carry_in, axis=0)
    hidden = jnp.concatenate([prods[j] * start + local[j] for j in range(steps)], axis=0)
    return state, _dot(perm_ref[1], hidden.astype(BF16)).astype(BF16)


def _lru_tile(u_ref, perm_ref, w_ref, ba_ref, bx_ref, lam_ref, carry_ref, o_ref, *, reverse):
    sub = perm_ref.shape[1]
    n_sub = u_ref.shape[1] // sub
    state = carry_ref[0:1, :]
    for k in (range(n_sub - 1, -1, -1) if reverse else range(n_sub)):
        rows = slice(k * sub, (k + 1) * sub)
        state, hidden = _lru_sub_tile(u_ref[0, rows], state, perm_ref, w_ref, ba_ref, bx_ref, lam_ref,
                                      o_ref is not None, reverse)
        if o_ref is not None:
            o_ref[0, rows] = hidden
    carry_ref[...] = jnp.broadcast_to(state, carry_ref.shape)


def _lru_kernel(uc_ref, ul_ref, perm_ref, w_ref, ba_ref, bx_ref, lam_ref, o_ref, carry_ref, *, n_ctx_tiles, reverse):
    j = pl.program_id(1)
    shared = (perm_ref, w_ref, ba_ref, bx_ref, lam_ref, carry_ref)

    @pl.when(j == 0)
    def _():
        carry_ref[...] = jnp.zeros_like(carry_ref)

    @pl.when(j < n_ctx_tiles)
    def _():
        _lru_tile(uc_ref, *shared, None, reverse=reverse)

    @pl.when(j >= n_ctx_tiles)
    def _():
        _lru_tile(ul_ref, *shared, o_ref, reverse=reverse)


def _lru_scan(proj, u_ctx, w_cat, ba, bx, lam, reverse):
    b, t, _ = proj.shape
    l, width = u_ctx.shape[1], u_ctx.shape[2]
    sub = min(256, l, t)
    tc = sub * min(LRU_SUB_TILES, l // sub)
    tl = sub * min(LRU_SUB_TILES, t // sub)
    n_c, n_l = l // tc, t // tl

    def pos_of(step, n):
        step = jnp.clip(step, 0, n - 1)
        return (n - 1 - step) if reverse else step

    def tile_map(off, n, col):
        return lambda bi, j: (bi, pos_of(j - off, n), col)

    const2 = lambda bi, j: (0, 0)
    in_specs = [
        pl.BlockSpec((1, tc, width), tile_map(0, n_c, 0)),
        pl.BlockSpec((1, tl, width), tile_map(n_c, n_l, 1)),
        pl.BlockSpec((2, sub, sub), lambda bi, j: (0, 0, 0)),
        pl.BlockSpec(w_cat.shape, lambda bi, j: (0, 0, 0)),
        pl.BlockSpec((1, width), const2),
        pl.BlockSpec((1, width), const2),
        pl.BlockSpec((1, width), const2),
    ]
    steps = sub // SUBLANES
    step, block = np.divmod(np.arange(sub), SUBLANES)
    time_of_row = block * steps + (steps - 1 - step if reverse else step)
    perm = (time_of_row[:, None] == np.arange(sub)[None]).astype(np.float32)
    perms = jnp.asarray(np.stack([perm, perm.T]), BF16)
    kern = functools.partial(_lru_kernel, n_ctx_tiles=n_c, reverse=reverse)
    return pl.pallas_call(
        kern,
        out_shape=jax.ShapeDtypeStruct((b, t, width), BF16),
        grid=(b, n_c + n_l),
        in_specs=in_specs,
        out_specs=pl.BlockSpec((1, tl, width), tile_map(n_c, n_l, 0)),
        scratch_shapes=[pltpu.VMEM((SUBLANES, width), F32)],
        compiler_params=_params(("parallel", "arbitrary")),
        name="lru_scan_bwd" if reverse else "lru_scan_fwd",
    )(u_ctx, proj, perms, w_cat, ba.reshape(1, width), bx.reshape(1, width), lam.reshape(1, width))


def _odd_out_kernel(hf_ref, hb_ref, gate_ref, w_ref, *tail_refs):
    for rows in _sub_tiles(hf_ref.shape[1]):
        hsum = hf_ref[0, rows].astype(F32) + hb_ref[0, rows].astype(F32)
        z = hsum * jax.nn.gelu(gate_ref[0, rows].astype(F32), approximate=True)
        y = _dot(z.astype(BF16), w_ref[...])
        _layer_tail(y, rows, *tail_refs)


def _odd_out(h_f, h_b, proj, w_out, x, g1, n2, sh2, sc2, w_r, b_r):
    b, t, d = x.shape
    width = h_f.shape[-1]
    tm = min(t, OUT_TILE)
    row = lambda bi, i: (bi, i, 0)
    tail_in, out_specs, out_shape, _, _ = _tail_specs(b, t, d, tm, 4, (b * t, 0), None)
    in_specs = [
        pl.BlockSpec((1, tm, width), row),
        pl.BlockSpec((1, tm, width), row),
        pl.BlockSpec((1, tm, width), row),
        pl.BlockSpec((width, d), lambda bi, i: (0, 0)),
    ] + tail_in
    return pl.pallas_call(
        _odd_out_kernel,
        out_shape=out_shape,
        grid=(b, t // tm),
        in_specs=in_specs,
        out_specs=out_specs,
        compiler_params=_params(("parallel", "parallel")),
        name="odd_out",
    )(h_f, h_b, proj, w_out, x, g1, n2, sh2, sc2, w_r, b_r)


def _expert_kernel(be_ref, next_ref, rows_ref, nb_ref, x_ref, wg_hbm, bg_ref, wu_hbm, bu_ref, wd_hbm, bd_ref,
                   o_ref, w_f32, wg_bf, wu_bf, wd_bf, h_bf, sem, *, layer, chunk):
    i = pl.program_id(0)
    expert = be_ref[i]
    used = i < nb_ref[0]

    def weight_copies(e):
        return [pltpu.make_async_copy(src.at[layer, e], w_f32.at[j], sem.at[j])
                for j, src in enumerate((wg_hbm, wu_hbm, wd_hbm))]

    @pl.when(i == 0)
    def _():
        for cp in weight_copies(expert):
            cp.start()

    @pl.when(used & ((i == 0) | (expert != be_ref[jnp.maximum(i - 1, 0)])))
    def _():
        for cp in weight_copies(expert):
            cp.wait()
        wg_bf[...] = w_f32[0].astype(BF16)
        wu_bf[...] = w_f32[1].astype(BF16)
        wd_bf[...] = w_f32[2].astype(BF16)

        @pl.when(next_ref[i] >= 0)
        def _():
            for cp in weight_copies(next_ref[i]):
                cp.start()

    sub = h_bf.shape[0]
    n_sub = x_ref.shape[0] // sub
    n_rows = jnp.where(used, rows_ref[i], 0)
    n_live = (n_rows + sub - 1) // sub

    def live_sub_block(s, carry):
        rows = pl.ds(pl.multiple_of(s * sub, sub), sub)
        xp = x_ref[rows]
        row = lax.broadcasted_iota(jnp.int32, xp.shape, 0) + s * sub
        xp = jnp.where(row < n_rows, xp, 0)
        x = jnp.concatenate(_unpack_rows(xp), axis=1).astype(BF16)
        for c in range(wg_bf.shape[1] // chunk):
            cs = slice(c * chunk, (c + 1) * chunk)
            g = jnp.minimum(_dot(x, wg_bf[:, cs]) + bg_ref[0, 0, :, cs], SWIGLU_LIMIT)
            u = jnp.clip(_dot(x, wu_bf[:, cs]) + bu_ref[0, 0, :, cs], -SWIGLU_LIMIT, SWIGLU_LIMIT)
            h_bf[:, cs] = (g * jax.nn.sigmoid(SWIGLU_ALPHA * g) * (u + 1.0)).astype(BF16)
        o_ref[rows] = _pack_rows(_dot(h_bf[...], wd_bf[...]) + bd_ref[0, 0])
        return carry

    def empty_sub_block(s, carry):
        o_ref[pl.ds(pl.multiple_of(s * sub, sub), sub)] = jnp.zeros((sub, o_ref.shape[1]), o_ref.dtype)
        return carry

    lax.fori_loop(0, n_live, live_sub_block, 0)
    lax.fori_loop(n_live, n_sub, empty_sub_block, 0)


def _experts(xb, block_e, next_e, block_rows, n_used, layer, wg, bg, wu, bu, wd, bd):
    n_slots = xb.shape[0]
    depth, n_e, d, d_exp = wg.shape
    assert d == d_exp
    tm = EXPERT_TILE
    n_blocks = n_slots // tm
    xmap = lambda i, be, ne, br, nb: (jnp.minimum(i, nb[0] - 1), 0)
    bmap = lambda i, be, ne, br, nb: (layer, be[i], 0, 0)
    hbm = pl.BlockSpec(memory_space=pl.ANY)
    grid_spec = pltpu.PrefetchScalarGridSpec(
        num_scalar_prefetch=4,
        grid=(n_blocks,),
        in_specs=[
            pl.BlockSpec((tm, d // 2), xmap),
            hbm,
            pl.BlockSpec((1, 1, 1, d_exp), bmap),
            hbm,
            pl.BlockSpec((1, 1, 1, d_exp), bmap),
            hbm,
            pl.BlockSpec((1, 1, 1, d), bmap),
        ],
        out_specs=pl.BlockSpec((tm, d // 2), lambda i, be, ne, br, nb: (i, 0)),
        scratch_shapes=[pltpu.VMEM((3, d, d_exp), F32), pltpu.VMEM((d, d_exp), BF16), pltpu.VMEM((d, d_exp), BF16),
                        pltpu.VMEM((d_exp, d), BF16), pltpu.VMEM((EXPERT_SUB, d_exp), BF16),
                        pltpu.SemaphoreType.DMA((3,))],
    )
    return pl.pallas_call(
        functools.partial(_expert_kernel, layer=layer, chunk=256),
        out_shape=jax.ShapeDtypeStruct((n_slots, d // 2), jnp.int32),
        grid_spec=grid_spec,
        compiler_params=_params(("arbitrary",)),
        name="experts",
    )(block_e, next_e, block_rows, n_used, xb, wg, bg.reshape(depth, n_e, 1, d_exp), wu,
      bu.reshape(depth, n_e, 1, d_exp), wd, bd.reshape(depth, n_e, 1, d))


def _combine_kernel(x_ref, g2_ref, gates_ref, y_ref, o_ref):
    o_ref[0] = _moe_residual(x_ref[0], g2_ref[0], gates_ref[0], y_ref[...])


def _combine(x, g2, gates, y_sel, tok_offset):
    b, t, d = x.shape
    tm = int(np.gcd(min(t, ROW_TILE), tok_offset)) if tok_offset else min(t, ROW_TILE)
    n_t = t // tm
    off = tok_offset // tm
    row = lambda bi, i: (bi, i, 0)
    return pl.pallas_call(
        _combine_kernel,
        out_shape=jax.ShapeDtypeStruct((b, t, d), F32),
        grid=(b, n_t),
        in_specs=[
            pl.BlockSpec((1, tm, d), row),
            pl.BlockSpec((1, 1, d), lambda bi, i: (bi, 0, 0)),
            pl.BlockSpec((1, tm, ROUTER_PAD), row),
            pl.BlockSpec((TOP_K, tm, d // 2), lambda bi, i: (0, off + bi * n_t + i, 0)),
        ],
        out_specs=pl.BlockSpec((1, tm, d), row),
        input_output_aliases={0: 0},
        compiler_params=_params(("parallel", "parallel")),
        name="moe_combine",
    )(x, g2, gates, y_sel)


def _route_kernel(ids_ref, upper_ref, dest_ref, counts_ref, run_ref, *, tile):
    p = pl.program_id(0)
    i = pl.program_id(1)
    tm = ids_ref.shape[0]
    ids_t = ids_ref[...].T
    expert = lax.broadcasted_iota(jnp.int32, (N_EXPERTS, tm), 0)
    chosen = [ids_t[k:k + 1, :] == expert for k in range(TOP_K)]
    picks = functools.reduce(jnp.add, [c.astype(F32) for c in chosen])
    tile_counts = jnp.sum(picks, axis=1, keepdims=True)

    @pl.when((p == 0) & (i == 0))
    def _():
        run_ref[...] = jnp.zeros_like(run_ref)

    @pl.when(p == 0)
    def _():
        run_ref[...] += tile_counts

    @pl.when((p == 1) & (i == 0))
    def _():
        counts = run_ref[...]
        counts_ref[...] = counts.astype(jnp.int32)
        padded = jnp.floor((counts + (tile - 1)) * (1.0 / tile)) * tile
        row = lax.broadcasted_iota(jnp.int32, counts.shape, 0)
        ends = padded
        for s in (1, 2, 4, 8, 16):
            ends = ends + jnp.where(row >= s, pltpu.roll(ends, s, 0), 0.0)
        run_ref[...] = ends - padded

    @pl.when(p == 1)
    def _():
        before = _dot(picks.astype(BF16), upper_ref[...])
        slot = before + run_ref[:, 0:1]
        rows = [jnp.sum(jnp.where(c, slot, 0.0), axis=0, keepdims=True) for c in chosen]
        rows += [jnp.zeros_like(rows[0])] * (dest_ref.shape[0] - TOP_K)
        dest_ref[...] = jnp.concatenate(rows, axis=0).astype(jnp.int32)
        run_ref[...] += tile_counts


def _route(ids):
    n = ids.shape[0]
    tm = int(np.gcd(n, ROUTE_TILE))
    n_tiles = n // tm
    upper = jnp.asarray(np.triu(np.ones((tm, tm), np.float32), 1), BF16)
    dest, counts = pl.pallas_call(
        functools.partial(_route_kernel, tile=EXPERT_TILE),
        out_shape=[jax.ShapeDtypeStruct((SUBLANES, n), jnp.int32),
                   jax.ShapeDtypeStruct((N_EXPERTS, LANES), jnp.int32)],
        grid=(2, n_tiles),
        in_specs=[pl.BlockSpec((tm, ROUTER_PAD), lambda p, i: (i, 0)),
                  pl.BlockSpec((tm, tm), lambda p, i: (0, 0))],
        out_specs=[pl.BlockSpec((SUBLANES, tm), lambda p, i: (0, i * p)),
                   pl.BlockSpec((N_EXPERTS, LANES), lambda p, i: (0, 0))],
        scratch_shapes=[pltpu.VMEM((N_EXPERTS, LANES), F32)],
        compiler_params=_params(("arbitrary", "arbitrary")),
        name="route",
    )(ids, upper)
    return dest, counts[:, 0]


def _row_gather(table, idx):
    info = plsc.get_sparse_core_info()
    n_cores, n_workers = info.num_cores, info.num_cores * info.num_subcores
    n_rows, width = idx.shape[0], table.shape[1]
    chunk_rows = SC_GATHER_ROWS
    per_worker = n_rows // n_workers
    n_chunks = per_worker // chunk_rows
    assert per_worker * n_workers == n_rows and n_chunks * chunk_rows == per_worker and n_chunks % 2 == 0
    mesh = plsc.VectorSubcoreMesh(core_axis_name="c", subcore_axis_name="s")

    @functools.partial(
        pl.kernel, mesh=mesh,
        out_type=jax.ShapeDtypeStruct((n_rows, width), table.dtype),
        scratch_types=[pltpu.VMEM((n_chunks, chunk_rows), jnp.int32), pltpu.VMEM((2, chunk_rows, width), table.dtype),
                       pltpu.SemaphoreType.DMA((2,)), pltpu.SemaphoreType.DMA((2,))],
    )
    def gather_kernel(table_hbm, idx_hbm, out_hbm, idx_v, rows_v, gather_sem, write_sem):
        worker = lax.axis_index("s") * n_cores + lax.axis_index("c")
        pltpu.sync_copy(idx_hbm.at[worker], idx_v)

        def gather(c, slot):
            return pltpu.make_async_copy(table_hbm.at[idx_v.at[c]], rows_v.at[slot], gather_sem.at[slot])

        def write(c, slot):
            first_row = worker * per_worker + c * chunk_rows
            return pltpu.make_async_copy(rows_v.at[slot], out_hbm.at[pl.ds(first_row, chunk_rows)], write_sem.at[slot])

        gather(0, 0).start()

        @pl.loop(0, n_chunks, step=2)
        def _(c0):
            for slot in (0, 1):
                c = c0 + slot
                gather(c, slot).wait()

                @pl.when(c >= 1)
                def _():
                    write(c - 1, 1 - slot).wait()

                @pl.when(c + 1 < n_chunks)
                def _():
                    gather(c + 1, 1 - slot).start()

                write(c, slot).start()

        write(n_chunks - 1, 1).wait()

    return gather_kernel(table, idx.reshape(n_workers, n_chunks, chunk_rows))


def _row_scatter(rows, dest, n_out):
    info = plsc.get_sparse_core_info()
    n_cores, n_workers = info.num_cores, info.num_cores * info.num_subcores
    n_choices, n_rows = dest.shape
    width = rows.shape[1]
    chunk_rows = SC_GATHER_ROWS
    per_worker = n_rows // (n_workers * chunk_rows)
    assert per_worker * n_workers * chunk_rows == n_rows == rows.shape[0]
    idx = dest.reshape(n_choices, n_workers, per_worker, chunk_rows)
    mesh = plsc.VectorSubcoreMesh(core_axis_name="c", subcore_axis_name="s")

    @functools.partial(
        pl.kernel, mesh=mesh,
        out_type=jax.ShapeDtypeStruct((n_out, width), rows.dtype),
        scratch_types=[pltpu.VMEM((n_choices, per_worker, chunk_rows), jnp.int32),
                       pltpu.VMEM((chunk_rows, width), rows.dtype), pltpu.SemaphoreType.DMA],
    )
    def scatter_kernel(rows_hbm, idx_hbm, out_hbm, idx_v, rows_v, sem):
        worker = lax.axis_index("s") * n_cores + lax.axis_index("c")
        for k in range(n_choices):
            pltpu.sync_copy(idx_hbm.at[k, worker], idx_v.at[k])

        @pl.loop(0, per_worker)
        def _(c):
            first_row = (worker * per_worker + c) * chunk_rows
            pltpu.sync_copy(rows_hbm.at[pl.ds(first_row, chunk_rows)], rows_v)
            copies = [pltpu.async_copy(rows_v, out_hbm.at[idx_v.at[k, c]], sem) for k in range(n_choices)]
            for cp in copies:
                cp.wait()

    return scatter_kernel(rows, idx)


def _moe(h2, ids, layer, w_exp):
    n = h2.shape[0]
    nk = n * TOP_K
    tm = EXPERT_TILE
    dest, counts = _route(ids)
    dest = dest[:TOP_K]
    padded = (counts + tm - 1) // tm * tm
    pends = jnp.cumsum(padded)
    n_blocks = -(-nk // tm) + N_EXPERTS
    n_slots = n_blocks * tm
    block_start = jnp.arange(n_blocks, dtype=jnp.int32) * tm
    block_e = jnp.minimum(jnp.sum((pends[None] <= block_start[:, None]).astype(jnp.int32), axis=1), N_EXPERTS - 1)
    is_e = (block_e[:, None] == jnp.arange(N_EXPERTS, dtype=jnp.int32)[None]).astype(jnp.int32)
    block_rows = jnp.clip(jnp.sum(is_e * (pends - padded + counts)[None], axis=1) - block_start, 0, tm)
    n_used = (pends[-1] // tm).astype(jnp.int32).reshape(1)
    e_ids = jnp.arange(N_EXPERTS, dtype=jnp.int32)
    later = (e_ids[None] > e_ids[:, None]) & (counts[None] > 0)
    next_nonempty = jnp.min(jnp.where(later, e_ids[None], N_EXPERTS), axis=1)
    next_nonempty = jnp.where(next_nonempty == N_EXPERTS, -1, next_nonempty)
    next_e = jnp.sum(is_e * next_nonempty[None], axis=1)
    xb = _row_scatter(h2, dest, n_slots)
    return _experts(xb, block_e, next_e, block_rows, n_used, layer, *w_exp), dest


def _gather_choices(yb, dest, lo, hi):
    return _row_gather(yb, dest[:, lo:hi].reshape(-1)).reshape(TOP_K, hi - lo, -1)


def _mod_parts(mod_l, b):
    d = mod_l.shape[-1] // 6
    lat = [mod_l[:b, k * d:(k + 1) * d].reshape(b, 1, d) for k in range(6)]
    ctx = [jnp.broadcast_to(mod_l[b, k * d:(k + 1) * d].reshape(1, 1, d), (b, 1, d)) for k in range(6)]
    return lat, ctx


def kernel(x, c, ctx, c_ctx, ada_w, ada_b, norm1_g, norm2_g, ev_w_in, ev_w_out, ev_q_gain, ev_k_gain, ev_rpb, ev_conv_w, ev_conv_b, od_w_in, od_w_out, od_conv_w, od_conv_b, od_fwd_wa, od_fwd_ba, od_fwd_wx, od_fwd_bx, od_fwd_lam, od_bwd_wa, od_bwd_ba, od_bwd_wx, od_bwd_bx, od_bwd_lam, router_w, router_b, exp_w_gate, exp_b_gate, exp_w_up, exp_b_up, exp_w_down, exp_b_down):
    b, t, d = x.shape
    l = ctx.shape[1]
    assert ada_w.shape[0] == DEPTH == 2 and t % GRID_W == 0 and t // GRID_W >= WIN_H

    n_rows_c = -(-(b + 1) // SUBLANES) * SUBLANES
    cvec = jnp.zeros((n_rows_c, d), F32).at[:b].set(c).at[b].set(c_ctx)
    mod = _ada_mod(cvec, ada_w, ada_b)

    def router(layer):
        w_r = jnp.zeros((d, ROUTER_PAD), F32).at[:, :N_EXPERTS].set(router_w[layer]).astype(BF16)
        b_r = jnp.zeros((1, ROUTER_PAD), F32).at[0, :N_EXPERTS].set(router_b[layer])
        return w_r, b_r

    w_exp = (exp_w_gate, exp_b_gate, exp_w_up, exp_b_up, exp_w_down, exp_b_down)

    (sh1, sc1, g1, sh2, sc2, g2), (csh1, csc1, cg1, csh2, csc2, cg2) = _mod_parts(mod[0], b)
    n1 = norm1_g[0].reshape(1, d)
    n2 = norm2_g[0].reshape(1, d)
    w_in = ev_w_in[0].astype(BF16)
    w_out = ev_w_out[0].astype(BF16)
    q_scale = NA_HEAD_DIM ** -0.5 * LOG2_E
    head_gain = jnp.stack([jnp.tile(ev_q_gain[0] * q_scale, NA_HEADS), jnp.tile(ev_k_gain[0], NA_HEADS)])
    head_gain = head_gain.reshape(2, 1, NA_WIDTH).astype(F32)
    proj = _inproj(x, n1, sh1, sc1, w_in, head_gain)
    proj_c = _inproj(ctx, n1, csh1, csc1, w_in, head_gain)
    kh, r0, row_type, patterns = _na_tables(t // GRID_W)
    bias_tab = _na_bias_table(ev_rpb[0], patterns)
    o_a = _neighbourhood_attention(proj, proj_c, bias_tab, r0, row_type, kh)
    oc_a = _context_attention(proj_c)
    w_r, b_r = router(0)
    n_tok = b * (l + t)
    fresh = (jnp.zeros((n_tok, d // 2), jnp.int32), jnp.zeros((n_tok, ROUTER_PAD), jnp.int32))
    c1, tokens, ids_all, gates_c = _even_out(oc_a, proj_c, ev_conv_w[0], ev_conv_b[0], w_out, ctx, cg1, n2, csh2,
                                             csc2, w_r, b_r, (n_tok, 0), fresh)
    x1, tokens, ids_all, gates = _even_out(o_a, proj, ev_conv_w[0], ev_conv_b[0], w_out, x, g1, n2, sh2, sc2, w_r,
                                           b_r, (n_tok, b * l), (tokens, ids_all))
    yb, dest = _moe(tokens, ids_all, 0, w_exp)
    y_sel = _gather_choices(yb, dest, 0, b * (l + t))
    g2_prev, cg2_prev = g2, cg2

    (sh1, sc1, g1, sh2, sc2, g2), (csh1, csc1, _, _, _, _) = _mod_parts(mod[1], b)
    n1 = norm1_g[1].reshape(1, d)
    n2 = norm2_g[1].reshape(1, d)
    w_in = od_w_in[0].astype(BF16)
    width = w_in.shape[1] // 2
    x, proj = _combine_inproj_conv(x1, g2_prev, gates, y_sel, b * l, n1, sh1, sc1, w_in, od_conv_w[0], od_conv_b[0],
                                   width)
    _, u_ctx = _combine_inproj_conv(c1, cg2_prev, gates_c, y_sel, 0, n1, csh1, csc1, w_in[:, width:], od_conv_w[0],
                                    od_conv_b[0], 0)
    h_dir = []
    for reverse, (wa, ba, wx, bx, lam) in ((False, (od_fwd_wa, od_fwd_ba, od_fwd_wx, od_fwd_bx, od_fwd_lam)),
                                           (True, (od_bwd_wa, od_bwd_ba, od_bwd_wx, od_bwd_bx, od_bwd_lam))):
        w_cat = jnp.concatenate([wa[0], wx[0]], axis=-1).astype(BF16)
        h_dir.append(_lru_scan(proj, u_ctx, w_cat, ba[0], bx[0], lam[0], reverse))
    w_r, b_r = router(1)
    x1, h2, ids, gates = _odd_out(h_dir[0], h_dir[1], proj, od_w_out[0].astype(BF16), x, g1, n2, sh2, sc2, w_r, b_r)
    yb, dest = _moe(h2, ids, 1, w_exp)
    return _combine(x1, g2, gates, _gather_choices(yb, dest, 0, b * t), 0)
```

```python
import functools

import numpy as np
import jax
import jax.numpy as jnp
from jax import lax
from jax.experimental import pallas as pl
from jax.experimental.pallas import tpu as pltpu
from jax.experimental.pallas import tpu_sc as plsc

DEPTH = 2
GRID_W = 64
EPS = 1e-6
NEG_INF = -1e30
LOG2_E = 1.4426950408889634
NA_HEADS = 8
NA_HEAD_DIM = 64
NA_WIDTH = NA_HEADS * NA_HEAD_DIM
HEAD_PAIRS = NA_HEADS // 2
WIN_H = 8
WIN_W = 16
SC_CONV = 3
LRU_BLOCKS = 4
LRU_CONV = 4
LRU_C = 8.0
N_EXPERTS = 32
TOP_K = 4
SWIGLU_LIMIT = 7.0
SWIGLU_ALPHA = 1.702

LANES = 128
SUBLANES = 8
MXU_DEPTH = 256
HALO = 16
ROUTER_PAD = LANES
ADA_COL_TILE = 1536
ROW_TILE = 1024
NA_ROWS_PER_STEP = 4
LRU_SUB_TILES = 8
OUT_TILE = 1024
OUT_SUB_TILE = 512
ROUTE_TILE = 2048
EXPERT_TILE = 2048
EXPERT_SUB = 512
SC_GATHER_ROWS = 64
SC_SCATTER_ROWS = 128
VMEM_LIMIT = 56 * 1024 * 1024

F32 = jnp.float32
BF16 = jnp.bfloat16


def _params(sem, vmem=VMEM_LIMIT):
    return pltpu.CompilerParams(dimension_semantics=sem, vmem_limit_bytes=vmem)


def _dot(a, b):
    return jnp.dot(a, b, preferred_element_type=F32)


def _dot_nt(a, b):
    return lax.dot_general(a, b, (((1,), (1,)), ((), ())), preferred_element_type=F32)


def _pack_rows(v):
    w = v.shape[-1] // 2
    lo = lax.bitcast_convert_type(v[:, :w].astype(BF16).astype(F32), jnp.int32)
    hi = lax.bitcast_convert_type(v[:, w:].astype(BF16).astype(F32), jnp.int32)
    return lax.shift_right_logical(lo, 16) | (hi & jnp.int32(-65536))


def _unpack_rows(p):
    lo = lax.bitcast_convert_type(lax.shift_left(p, 16), F32)
    hi = lax.bitcast_convert_type(p & jnp.int32(-65536), F32)
    return lo, hi


def _rms_mod(x, g, shift, scale):
    ms = jnp.mean(x * x, axis=-1, keepdims=True)
    return (x * lax.rsqrt(ms + EPS)) * (g * (1.0 + scale)) + shift


def _ada_kernel(c_ref, w_ref, b_ref, o_ref):
    c = c_ref[...]
    s = (c * jax.nn.sigmoid(c)).astype(BF16)
    o_ref[0] = _dot(s, w_ref[0].astype(BF16)) + b_ref[0]


def _ada_mod(cvec, ada_w, ada_b):
    depth, d, n = ada_w.shape
    r = cvec.shape[0]
    tn = ADA_COL_TILE
    return pl.pallas_call(
        _ada_kernel,
        out_shape=jax.ShapeDtypeStruct((depth, r, n), F32),
        grid=(depth, n // tn),
        in_specs=[
            pl.BlockSpec((r, d), lambda l, j: (0, 0)),
            pl.BlockSpec((1, d, tn), lambda l, j: (l, 0, j)),
            pl.BlockSpec((1, 1, tn), lambda l, j: (l, 0, j)),
        ],
        out_specs=pl.BlockSpec((1, r, tn), lambda l, j: (l, 0, j)),
        compiler_params=_params(("parallel", "parallel")),
        name="ada_mod",
    )(cvec, ada_w, ada_b.reshape(depth, 1, n))


def _inproj_kernel(x_ref, g_ref, sh_ref, sc_ref, w_ref, hg_ref, ones_ref, o_ref, *, n_tiles, tn, n_headnorm):
    h = _rms_mod(x_ref[0], g_ref[...], sh_ref[0], sc_ref[0]).astype(BF16)
    for j in range(n_tiles):
        y = _dot(h, w_ref[:, j * tn:(j + 1) * tn])
        if j < n_headnorm:
            ysq = (y * y).astype(BF16)
            kw = ones_ref.shape[0]
            ms = jnp.concatenate([_dot(ysq[:, c:c + kw], ones_ref[...]) for c in range(0, tn, kw)], axis=1)
            y = y * lax.rsqrt(ms * (1.0 / NA_HEAD_DIM) + EPS) * hg_ref[j]
        o_ref[0, :, j * tn:(j + 1) * tn] = y.astype(o_ref.dtype)


def _inproj(x, g, shift, scale, w, head_gain):
    b, t, d = x.shape
    n = w.shape[1]
    tn = NA_WIDTH
    tm = min(t, ROW_TILE)
    n_headnorm = head_gain.shape[0]
    hid = np.arange(MXU_DEPTH) // NA_HEAD_DIM
    ones_bd = jnp.asarray((hid[:, None] == hid[None, :]), BF16)
    kern = functools.partial(_inproj_kernel, n_tiles=n // tn, tn=tn, n_headnorm=n_headnorm)
    return pl.pallas_call(
        kern,
        out_shape=jax.ShapeDtypeStruct((b, t, n), BF16),
        grid=(b, t // tm),
        in_specs=[
            pl.BlockSpec((1, tm, d), lambda bi, i: (bi, i, 0)),
            pl.BlockSpec((1, d), lambda bi, i: (0, 0)),
            pl.BlockSpec((1, 1, d), lambda bi, i: (bi, 0, 0)),
            pl.BlockSpec((1, 1, d), lambda bi, i: (bi, 0, 0)),
            pl.BlockSpec((d, n), lambda bi, i: (0, 0)),
            pl.BlockSpec(head_gain.shape, lambda bi, i: (0, 0, 0)),
            pl.BlockSpec((MXU_DEPTH, MXU_DEPTH), lambda bi, i: (0, 0)),
        ],
        out_specs=pl.BlockSpec((1, tm, n), lambda bi, i: (bi, i, 0)),
        compiler_params=_params(("parallel", "parallel")),
        name="inproj",
    )(x, g, shift, scale, w, head_gain, ones_bd)


def _moe_residual(x, g2, gates, y):
    acc_lo = acc_hi = 0.0
    for k in range(TOP_K):
        lo, hi = _unpack_rows(y[k])
        acc_lo = acc_lo + gates[:, k:k + 1] * lo
        acc_hi = acc_hi + gates[:, k:k + 1] * hi
    return x + g2 * jnp.concatenate([acc_lo, acc_hi], axis=1)


def _combine_inproj_conv_kernel(x_ref, xp_ref, xn_ref, gt_ref, gtp_ref, gtn_ref, y_ref, yp_ref, yn_ref, g2_ref,
                                g_ref, sh_ref, sc_ref, w_ref, cw_ref, cb_ref, xo_ref, o_ref, *,
                                n_tiles, n_plain, tn, tm, sub):
    i = pl.program_id(1)
    edge = SUBLANES
    g2 = g2_ref[0]
    x_tile = _moe_residual(x_ref[0], g2, gt_ref[0], y_ref[...])
    xo_ref[0] = x_tile
    x_ext = jnp.concatenate([_moe_residual(xp_ref[0], g2, gtp_ref[0], yp_ref[...]), x_tile,
                             _moe_residual(xn_ref[0], g2, gtn_ref[0], yn_ref[...])], axis=0)
    h = _rms_mod(x_ext, g_ref[...], sh_ref[0], sc_ref[0]).astype(BF16)
    first_step = i == 0
    last_step = i == pl.num_programs(1) - 1
    left = (LRU_CONV - 1) // 2
    n_sub = tm // sub
    for s in range(n_sub):
        h_s = h[s * sub:(s + 1) * sub + 2 * edge]
        rows = slice(s * sub, (s + 1) * sub)
        keep_prev = jnp.where(first_step, 0.0, 1.0) if s == 0 else 1.0
        keep_next = jnp.where(last_step, 0.0, 1.0) if s == n_sub - 1 else 1.0
        for j in range(n_tiles):
            y = _dot(h_s, w_ref[:, j * tn:(j + 1) * tn])
            if j < n_plain:
                o_ref[0, rows, j * tn:(j + 1) * tn] = y[edge:edge + sub].astype(o_ref.dtype)
                continue
            u = jnp.concatenate([y[:edge] * keep_prev, y[edge:edge + sub], y[edge + sub:] * keep_next], axis=0)
            cw = cw_ref[:, (j - n_plain) * tn:(j - n_plain + 1) * tn]
            uc = cb_ref[:, (j - n_plain) * tn:(j - n_plain + 1) * tn]
            n_ext = u.shape[0]
            for k in range(LRU_CONV):
                shifted = u if k == left else pltpu.roll(u, (left - k) % n_ext, 0)
                uc = uc + shifted[edge:edge + sub] * cw[k:k + 1]
            o_ref[0, rows, j * tn:(j + 1) * tn] = uc.astype(o_ref.dtype)


def _combine_inproj_conv(x1, g2, gates, y_sel, tok_offset, g, shift, scale, w, conv_w, conv_b, n_plain_cols):
    b, t, d = x1.shape
    n = w.shape[1]
    tn = NA_WIDTH
    tm = min(t, OUT_TILE)
    hb = tm // SUBLANES
    n_hblocks = t // SUBLANES
    y_hblocks = y_sel.shape[1] // SUBLANES
    assert tok_offset % tm == 0
    kern = functools.partial(_combine_inproj_conv_kernel, n_tiles=n // tn, n_plain=n_plain_cols // tn, tn=tn, tm=tm,
                             sub=min(tm, OUT_SUB_TILE))
    const = lambda bi, i: (0, 0)
    tile = lambda bi, i: (bi, i, 0)
    prev = lambda bi, i: (bi, jnp.maximum(i * hb - 1, 0), 0)
    nxt = lambda bi, i: (bi, jnp.minimum((i + 1) * hb, n_hblocks - 1), 0)
    y_row = lambda bi, i: (tok_offset + bi * t + i * tm) // SUBLANES
    per_b = lambda bi, i: (bi, 0, 0)
    return pl.pallas_call(
        kern,
        out_shape=[jax.ShapeDtypeStruct((b, t, d), F32), jax.ShapeDtypeStruct((b, t, n), BF16)],
        grid=(b, t // tm),
        in_specs=[
            pl.BlockSpec((1, tm, d), tile),
            pl.BlockSpec((1, SUBLANES, d), prev),
            pl.BlockSpec((1, SUBLANES, d), nxt),
            pl.BlockSpec((1, tm, ROUTER_PAD), tile),
            pl.BlockSpec((1, SUBLANES, ROUTER_PAD), prev),
            pl.BlockSpec((1, SUBLANES, ROUTER_PAD), nxt),
            pl.BlockSpec((TOP_K, tm, d // 2), lambda bi, i: (0, y_row(bi, i) // hb, 0)),
            pl.BlockSpec((TOP_K, SUBLANES, d // 2), lambda bi, i: (0, jnp.maximum(y_row(bi, i) - 1, 0), 0)),
            pl.BlockSpec((TOP_K, SUBLANES, d // 2),
                         lambda bi, i: (0, jnp.minimum(y_row(bi, i) + hb, y_hblocks - 1), 0)),
            pl.BlockSpec((1, 1, d), per_b),
            pl.BlockSpec((1, d), const),
            pl.BlockSpec((1, 1, d), per_b),
            pl.BlockSpec((1, 1, d), per_b),
            pl.BlockSpec((d, n), const),
            pl.BlockSpec(conv_w.shape, const),
            pl.BlockSpec((1, conv_w.shape[1]), const),
        ],
        out_specs=[pl.BlockSpec((1, tm, d), tile), pl.BlockSpec((1, tm, n), tile)],
        compiler_params=_params(("parallel", "parallel")),
        name="combine_inproj_conv",
    )(x1, x1, x1, gates, gates, gates, y_sel, y_sel, y_sel, g2, g, shift, scale, w, conv_w, conv_b.reshape(1, -1))


def _na_tables(rows):
    kh = min(WIN_H, rows)
    r = np.arange(rows)
    r0 = np.clip(r - kh // 2, 0, rows - kh)
    dr = r0[:, None] + np.arange(kh)[None] - r[:, None] + WIN_H - 1
    patterns, row_type = np.unique(dr, axis=0, return_inverse=True)
    return kh, r0.astype(np.int32), row_type.reshape(-1).astype(np.int32), patterns


def _na_bias_table(rpb, patterns):
    qc = np.arange(GRID_W)
    kc = np.arange(GRID_W)
    c0 = np.clip(qc - WIN_W // 2, 0, GRID_W - WIN_W)[:, None]
    valid = (kc[None] >= c0) & (kc[None] < c0 + WIN_W)
    dc = np.clip(kc[None] - qc[:, None] + WIN_W - 1, 0, 2 * WIN_W - 2)
    n_pat, kh = patterns.shape
    onehot_dc = jnp.asarray(dc[None] == np.arange(2 * WIN_W - 1)[:, None, None], F32)
    tab = jnp.einsum('hpic,cqk->hpiqk', rpb.astype(F32)[:, patterns], onehot_dc,
                     precision=lax.Precision.HIGHEST)
    tab = jnp.where(valid[None, None, None], tab * LOG2_E, NEG_INF)
    tab = tab.reshape(HEAD_PAIRS, 2, n_pat, kh, GRID_W, GRID_W)
    tab = tab.transpose(2, 0, 1, 4, 3, 5)
    return tab.reshape(n_pat, HEAD_PAIRS, 2 * GRID_W, kh * GRID_W)


def _pair_attention(q, keys, values, biases):
    m = q.shape[0]
    qq = _stack_heads(q)
    scores = []
    for k, bias in zip(keys, biases):
        s = _dot_nt(qq, k)
        scores.append(s if bias is None else s + bias)
    s = jnp.concatenate(scores, axis=1)
    e = jnp.exp2(s - jnp.max(s, axis=-1, keepdims=True))
    denom = jnp.sum(e, axis=-1, keepdims=True)
    e = e.astype(BF16)
    o, start = 0.0, 0
    for v in values:
        o = o + _dot(e[:, start:start + v.shape[0]], v)
        start += v.shape[0]
    o = o * (1.0 / denom)
    lane_o = lax.broadcasted_iota(jnp.int32, (m, LANES), 1)
    return jnp.where(lane_o < NA_HEAD_DIM, o[:m], o[m:])


def _stack_heads(q):
    lane = lax.broadcasted_iota(jnp.int32, q.shape, 1)
    zero = jnp.zeros_like(q)
    return jnp.concatenate([jnp.where(lane < NA_HEAD_DIM, q, zero), jnp.where(lane >= NA_HEAD_DIM, q, zero)], axis=0)


def _na_kernel(r0_ref, type_ref, q_ref, k_ref, v_ref, kc_ref, vc_ref, bias_ref, o_ref, s_ref, p_ref, *,
               kh, rows_per_step):
    n_lat = kh * GRID_W
    tiles = [(j, p) for j in range(rows_per_step) for p in range(HEAD_PAIRS)]
    window = []
    for j in range(rows_per_step):
        r = pl.program_id(1) * rows_per_step + j
        window.append((pl.multiple_of(r0_ref[r] * GRID_W, GRID_W), type_ref[r]))

    for idx, (j, p) in enumerate(tiles):
        start, rtype = window[j]
        cols = slice(p * LANES, (p + 1) * LANES)
        qq = _stack_heads(q_ref[0, j * GRID_W:(j + 1) * GRID_W, cols])
        s_ref[idx, :, :n_lat] = _dot_nt(qq, k_ref[0, pl.ds(start, n_lat), cols]) + bias_ref[rtype, p]
        s_ref[idx, :, n_lat:] = _dot_nt(qq, kc_ref[0, :, cols])

    denoms = []
    for idx in range(len(tiles)):
        s = s_ref[idx]
        e = jnp.exp2(s - jnp.max(s, axis=-1, keepdims=True))
        denoms.append(jnp.sum(e, axis=-1, keepdims=True))
        p_ref[idx] = e.astype(BF16)

    lane = lax.broadcasted_iota(jnp.int32, (GRID_W, LANES), 1)
    for idx, (j, p) in enumerate(tiles):
        start, _ = window[j]
        cols = slice(p * LANES, (p + 1) * LANES)
        o = _dot(p_ref[idx, :, :n_lat], v_ref[0, pl.ds(start, n_lat), cols]) + _dot(p_ref[idx, :, n_lat:],
                                                                                 vc_ref[0, :, cols])
        o = o * (1.0 / denoms[idx])
        o = jnp.where(lane < NA_HEAD_DIM, o[:GRID_W], o[GRID_W:])
        o_ref[0, j * GRID_W:(j + 1) * GRID_W, cols] = o.astype(o_ref.dtype)


def _neighbourhood_attention(proj, proj_c, bias_tab, r0, row_type, kh):
    b, t, _ = proj.shape
    l = proj_c.shape[1]
    rows = t // GRID_W
    w = NA_WIDTH
    rps = int(np.gcd(rows, NA_ROWS_PER_STEP))
    q_rows = rps * GRID_W
    grid_spec = pltpu.PrefetchScalarGridSpec(
        num_scalar_prefetch=2,
        grid=(b, rows // rps),
        in_specs=[
            pl.BlockSpec((1, q_rows, w), lambda bi, r, *_: (bi, r, 0)),
            pl.BlockSpec((1, t, w), lambda bi, r, *_: (bi, 0, 1)),
            pl.BlockSpec((1, t, w), lambda bi, r, *_: (bi, 0, 2)),
            pl.BlockSpec((1, l, w), lambda bi, r, *_: (bi, 0, 1)),
            pl.BlockSpec((1, l, w), lambda bi, r, *_: (bi, 0, 2)),
            pl.BlockSpec(bias_tab.shape, lambda bi, r, *_: (0, 0, 0, 0)),
        ],
        out_specs=pl.BlockSpec((1, q_rows, w), lambda bi, r, *_: (bi, r, 0)),
        scratch_shapes=[pltpu.VMEM((rps * HEAD_PAIRS, 2 * GRID_W, kh * GRID_W + l), F32),
                        pltpu.VMEM((rps * HEAD_PAIRS, 2 * GRID_W, kh * GRID_W + l), BF16)],
    )
    return pl.pallas_call(
        functools.partial(_na_kernel, kh=kh, rows_per_step=rps),
        out_shape=jax.ShapeDtypeStruct((b, t, w), BF16),
        grid_spec=grid_spec,
        compiler_params=_params(("parallel", "arbitrary")),
        name="na_attention",
    )(jnp.asarray(r0), jnp.asarray(row_type), proj, proj, proj, proj_c, proj_c, bias_tab)


def _ctx_attn_kernel(q_ref, k_ref, v_ref, o_ref):
    for p in range(HEAD_PAIRS):
        cols = slice(p * LANES, (p + 1) * LANES)
        o = _pair_attention(q_ref[0, :, cols], [k_ref[0, :, cols]], [v_ref[0, :, cols]], [None])
        o_ref[0, :, cols] = o.astype(o_ref.dtype)


def _context_attention(proj_c):
    b, l, _ = proj_c.shape
    w = NA_WIDTH
    return pl.pallas_call(
        _ctx_attn_kernel,
        out_shape=jax.ShapeDtypeStruct((b, l, w), BF16),
        grid=(b,),
        in_specs=[pl.BlockSpec((1, l, w), lambda bi, j=j: (bi, 0, j)) for j in range(3)],
        out_specs=pl.BlockSpec((1, l, w), lambda bi: (bi, 0, 0)),
        compiler_params=_params(("parallel",)),
        name="ctx_attention",
    )(proj_c, proj_c, proj_c)


def _top4(logits):
    lane = lax.broadcasted_iota(jnp.int32, logits.shape, 1)
    cur = jnp.where(lane < N_EXPERTS, logits, -jnp.inf)
    vals, idxs = [], []
    for _ in range(TOP_K):
        m = jnp.max(cur, axis=-1, keepdims=True)
        first = jnp.min(jnp.where(cur == m, lane, ROUTER_PAD).astype(F32), axis=-1, keepdims=True)
        idx = first.astype(jnp.int32)
        vals.append(m)
        idxs.append(idx)
        cur = jnp.where(lane == idx, -jnp.inf, cur)
    exps = [jnp.exp(v - vals[0]) for v in vals]
    inv = 1.0 / functools.reduce(jnp.add, exps)
    ids = jnp.zeros(logits.shape, jnp.int32)
    gates = jnp.zeros(logits.shape, F32)
    for k in range(TOP_K):
        ids = jnp.where(lane == k, idxs[k], ids)
        gates = jnp.where(lane == k, exps[k] * inv, gates)
    return ids, gates


def _layer_tail(y, rows, x_ref, g1_ref, n2_ref, sh2_ref, sc2_ref, wr_ref, br_ref, *rest):
    xo_ref, h2_ref, ids_ref, gates_ref = rest[-4:]
    x_new = x_ref[0, rows] + g1_ref[0] * y
    xo_ref[0, rows] = x_new
    h2 = _rms_mod(x_new, n2_ref[...], sh2_ref[0], sc2_ref[0])
    h2_ref[rows] = _pack_rows(h2)
    logits = _dot(h2.astype(BF16), wr_ref[...]) + br_ref[...]
    ids, gates = _top4(logits)
    ids_ref[rows] = ids
    gates_ref[0, rows] = gates


def _sub_tiles(n_rows):
    sub = min(OUT_SUB_TILE, n_rows)
    return [slice(s, s + sub) for s in range(0, n_rows, sub)]


def _tail_specs(b, t, d, tm, n_inputs_before, tokens, shared):
    n_total, offset = tokens
    n_t = t // tm
    assert offset % tm == 0
    row = lambda bi, i: (bi, i, 0)
    flat = lambda bi, i: (offset // tm + bi * n_t + i, 0)
    per_b = lambda bi, i: (bi, 0, 0)
    const = lambda bi, i: (0, 0)
    extra, aliases = (), {}
    if shared is not None:
        extra = tuple(shared)
        aliases = {n_inputs_before + 7: 1, n_inputs_before + 8: 2}
    in_specs = [
        pl.BlockSpec((1, tm, d), row),
        pl.BlockSpec((1, 1, d), per_b),
        pl.BlockSpec((1, d), const),
        pl.BlockSpec((1, 1, d), per_b),
        pl.BlockSpec((1, 1, d), per_b),
        pl.BlockSpec((d, ROUTER_PAD), const),
        pl.BlockSpec((1, ROUTER_PAD), const),
    ] + [pl.BlockSpec(memory_space=pl.ANY)] * len(extra)
    out_specs = [
        pl.BlockSpec((1, tm, d), row),
        pl.BlockSpec((tm, d // 2), flat),
        pl.BlockSpec((tm, ROUTER_PAD), flat),
        pl.BlockSpec((1, tm, ROUTER_PAD), row),
    ]
    out_shape = [
        jax.ShapeDtypeStruct((b, t, d), F32),
        jax.ShapeDtypeStruct((n_total, d // 2), jnp.int32),
        jax.ShapeDtypeStruct((n_total, ROUTER_PAD), jnp.int32),
        jax.ShapeDtypeStruct((b, t, ROUTER_PAD), F32),
    ]
    return in_specs, out_specs, out_shape, extra, aliases


def _halo_fix(rolled, at_row, halo_row, present):
    n = rolled.shape[0]
    first = at_row < SUBLANES
    assert first or at_row >= n - SUBLANES
    slab = rolled[:SUBLANES] if first else rolled[n - SUBLANES:]
    sub = lax.broadcasted_iota(jnp.int32, slab.shape, 0)
    fill = jnp.where(present, halo_row, jnp.zeros_like(halo_row))
    slab = jnp.where(sub == at_row % SUBLANES, fill, slab)
    return jnp.concatenate([slab, rolled[SUBLANES:]] if first else [rolled[:n - SUBLANES], slab], axis=0)


def _even_out_kernel(oa_ref, bg_ref, cg_ref, xin_ref, cgp_ref, xinp_ref, cgn_ref, xinn_ref, cw_ref, cb_ref,
                     wa_ref, wb_ref, *tail_refs, tm):
    i = pl.program_id(1)
    has_prev = i > 0
    has_next = i < pl.num_programs(1) - 1
    u = cg_ref[0].astype(F32) * xin_ref[0].astype(F32)
    u_prev = (cgp_ref[0].astype(F32) * xinp_ref[0].astype(F32))[HALO - 1:HALO]
    u_next = (cgn_ref[0].astype(F32) * xinn_ref[0].astype(F32))[0:1]
    u_m1 = _halo_fix(pltpu.roll(u, 1, 0), 0, u_prev, has_prev)
    u_p1 = _halo_fix(pltpu.roll(u, tm - 1, 0), tm - 1, u_next, has_next)
    cw = cw_ref[...]
    conv = u_m1 * cw[0:1] + u * cw[1:2] + u_p1 * cw[2:3] + cb_ref[...]
    o_b = (bg_ref[0].astype(F32) * conv).astype(BF16)
    for rows in _sub_tiles(tm):
        y = _dot(oa_ref[0, rows], wa_ref[...]) + _dot(o_b[rows], wb_ref[...])
        _layer_tail(y, rows, *tail_refs)


def _even_out(o_a, proj, conv_w, conv_b, w_out, x, g1, n2, sh2, sc2, w_r, b_r, tokens, shared=None):
    b, t, d = x.shape
    w = NA_WIDTH
    tm = min(t, OUT_TILE)
    hb = tm // HALO
    n_hblocks = t // HALO
    row = lambda bi, i: (bi, i, 0)
    const = lambda bi, i: (0, 0)
    prev = lambda col: (lambda bi, i: (bi, jnp.maximum(i * hb - 1, 0), col))
    nxt = lambda col: (lambda bi, i: (bi, jnp.minimum((i + 1) * hb, n_hblocks - 1), col))
    tail_in, out_specs, out_shape, extra, aliases = _tail_specs(b, t, d, tm, 12, tokens, shared)
    in_specs = [
        pl.BlockSpec((1, tm, w), row),
        pl.BlockSpec((1, tm, w), lambda bi, i: (bi, i, 3)),
        pl.BlockSpec((1, tm, w), lambda bi, i: (bi, i, 4)),
        pl.BlockSpec((1, tm, w), lambda bi, i: (bi, i, 5)),
        pl.BlockSpec((1, HALO, w), prev(4)),
        pl.BlockSpec((1, HALO, w), prev(5)),
        pl.BlockSpec((1, HALO, w), nxt(4)),
        pl.BlockSpec((1, HALO, w), nxt(5)),
        pl.BlockSpec((SC_CONV, w), const),
        pl.BlockSpec((1, w), const),
        pl.BlockSpec((w, d), const),
        pl.BlockSpec((w, d), const),
    ] + tail_in
    return pl.pallas_call(
        functools.partial(_even_out_kernel, tm=tm),
        out_shape=out_shape,
        grid=(b, t // tm),
        in_specs=in_specs,
        out_specs=out_specs,
        input_output_aliases=aliases,
        compiler_params=_params(("parallel", "parallel")),
        name="even_out",
    )(o_a, proj, proj, proj, proj, proj, proj, proj, conv_w, conv_b.reshape(1, w),
      w_out[:w], w_out[w:], x, g1, n2, sh2, sc2, w_r, b_r, *extra)


def _log_sigmoid(x):
    return jnp.minimum(x, 0.0) - jnp.log1p(jnp.exp(-jnp.abs(x)))


def _sigmoid(x):
    return 0.5 + 0.5 * jnp.tanh(0.5 * x)


def _lru_sub_tile(ucb_time, state, perm_ref, w_ref, ba_ref, bx_ref, lam_ref, want_hidden, reverse):
    sub, width = ucb_time.shape
    blk = width // LRU_BLOCKS
    steps = sub // SUBLANES
    uc = _dot(perm_ref[0], ucb_time)
    ucb = uc.astype(BF16)
    za, zx = [], []
    for h in range(LRU_BLOCKS):
        z = _dot(ucb[:, h * blk:(h + 1) * blk], w_ref[h])
        za.append(z[:, :blk])
        zx.append(z[:, blk:])
    r = _sigmoid(jnp.concatenate(za, axis=1) + ba_ref[...])
    gate_i = _sigmoid(jnp.concatenate(zx, axis=1) + bx_ref[...])
    log_a = (LRU_C * _log_sigmoid(lam_ref[...])) * r
    a = jnp.exp(log_a)
    th = jnp.tanh(log_a)
    num = -2.0 * th
    mult = jnp.where(num > 0.0, num * lax.rsqrt(num * (1.0 - th)), 0.0)
    bcoef = mult * gate_i * uc

    grp = lambda v, j: v[j * SUBLANES:(j + 1) * SUBLANES]
    prods, local = [grp(a, 0)], [grp(bcoef, 0)]
    for j in range(1, steps):
        aj = grp(a, j)
        local.append(aj * local[-1] + grp(bcoef, j))
        prods.append(aj * prods[-1])

    carry_in = [None] * SUBLANES
    for s in (range(SUBLANES - 1, -1, -1) if reverse else range(SUBLANES)):
        carry_in[s] = state
        state = prods[-1][s:s + 1] * state + local[-1][s:s + 1]
    if not want_hidden:
        return state, None
    start = jnp.concatenate(carry_in, axis=0)
    hidden = jnp.concatenate([prods[j] * start + local[j] for j in range(steps)], axis=0)
    return state, _dot(perm_ref[1], hidden.astype(BF16)).astype(BF16)


def _lru_tile(u_ref, perm_ref, w_ref, ba_ref, bx_ref, lam_ref, carry_ref, o_ref, *, reverse):
    sub = perm_ref.shape[1]
    n_sub = u_ref.shape[1] // sub
    state = carry_ref[0:1, :]
    for k in (range(n_sub - 1, -1, -1) if reverse else range(n_sub)):
        rows = slice(k * sub, (k + 1) * sub)
        state, hidden = _lru_sub_tile(u_ref[0, rows], state, perm_ref, w_ref, ba_ref, bx_ref, lam_ref,
                                      o_ref is not None, reverse)
        if o_ref is not None:
            o_ref[0, rows] = hidden
    carry_ref[...] = jnp.broadcast_to(state, carry_ref.shape)


def _lru_kernel(uc_ref, ul_ref, perm_ref, w_ref, ba_ref, bx_ref, lam_ref, o_ref, carry_ref, *, n_ctx_tiles, reverse):
    j = pl.program_id(1)
    shared = (perm_ref, w_ref, ba_ref, bx_ref, lam_ref, carry_ref)

    @pl.when(j == 0)
    def _():
        carry_ref[...] = jnp.zeros_like(carry_ref)

    @pl.when(j < n_ctx_tiles)
    def _():
        _lru_tile(uc_ref, *shared, None, reverse=reverse)

    @pl.when(j >= n_ctx_tiles)
    def _():
        _lru_tile(ul_ref, *shared, o_ref, reverse=reverse)


def _lru_scan(proj, u_ctx, w_cat, ba, bx, lam, reverse):
    b, t, _ = proj.shape
    l, width = u_ctx.shape[1], u_ctx.shape[2]
    sub = min(256, l, t)
    tc = sub * min(LRU_SUB_TILES, l // sub)
    tl = sub * min(LRU_SUB_TILES, t // sub)
    n_c, n_l = l // tc, t // tl

    def pos_of(step, n):
        step = jnp.clip(step, 0, n - 1)
        return (n - 1 - step) if reverse else step

    def tile_map(off, n, col):
        return lambda bi, j: (bi, pos_of(j - off, n), col)

    const2 = lambda bi, j: (0, 0)
    in_specs = [
        pl.BlockSpec((1, tc, width), tile_map(0, n_c, 0)),
        pl.BlockSpec((1, tl, width), tile_map(n_c, n_l, 1)),
        pl.BlockSpec((2, sub, sub), lambda bi, j: (0, 0, 0)),
        pl.BlockSpec(w_cat.shape, lambda bi, j: (0, 0, 0)),
        pl.BlockSpec((1, width), const2),
        pl.BlockSpec((1, width), const2),
        pl.BlockSpec((1, width), const2),
    ]
    steps = sub // SUBLANES
    step, block = np.divmod(np.arange(sub), SUBLANES)
    time_of_row = block * steps + (steps - 1 - step if reverse else step)
    perm = (time_of_row[:, None] == np.arange(sub)[None]).astype(np.float32)
    perms = jnp.asarray(np.stack([perm, perm.T]), BF16)
    kern = functools.partial(_lru_kernel, n_ctx_tiles=n_c, reverse=reverse)
    return pl.pallas_call(
        kern,
        out_shape=jax.ShapeDtypeStruct((b, t, width), BF16),
        grid=(b, n_c + n_l),
        in_specs=in_specs,
        out_specs=pl.BlockSpec((1, tl, width), tile_map(n_c, n_l, 0)),
        scratch_shapes=[pltpu.VMEM((SUBLANES, width), F32)],
        compiler_params=_params(("parallel", "arbitrary")),
        name="lru_scan_bwd" if reverse else "lru_scan_fwd",
    )(u_ctx, proj, perms, w_cat, ba.reshape(1, width), bx.reshape(1, width), lam.reshape(1, width))


def _odd_out_kernel(hf_ref, hb_ref, gate_ref, w_ref, *tail_refs):
    for rows in _sub_tiles(hf_ref.shape[1]):
        hsum = hf_ref[0, rows].astype(F32) + hb_ref[0, rows].astype(F32)
        z = hsum * jax.nn.gelu(gate_ref[0, rows].astype(F32), approximate=True)
        y = _dot(z.astype(BF16), w_ref[...])
        _layer_tail(y, rows, *tail_refs)


def _odd_out(h_f, h_b, proj, w_out, x, g1, n2, sh2, sc2, w_r, b_r):
    b, t, d = x.shape
    width = h_f.shape[-1]
    tm = min(t, OUT_TILE)
    row = lambda bi, i: (bi, i, 0)
    tail_in, out_specs, out_shape, _, _ = _tail_specs(b, t, d, tm, 4, (b * t, 0), None)
    in_specs = [
        pl.BlockSpec((1, tm, width), row),
        pl.BlockSpec((1, tm, width), row),
        pl.BlockSpec((1, tm, width), row),
        pl.BlockSpec((width, d), lambda bi, i: (0, 0)),
    ] + tail_in
    return pl.pallas_call(
        _odd_out_kernel,
        out_shape=out_shape,
        grid=(b, t // tm),
        in_specs=in_specs,
        out_specs=out_specs,
        compiler_params=_params(("parallel", "parallel")),
        name="odd_out",
    )(h_f, h_b, proj, w_out, x, g1, n2, sh2, sc2, w_r, b_r)


def _expert_kernel(be_ref, next_ref, rows_ref, nb_ref, x_ref, wg_hbm, bg_ref, wu_hbm, bu_ref, wd_hbm, bd_ref,
                   o_ref, w_f32, wg_bf, wu_bf, wd_bf, h_bf, sem, *, layer, chunk):
    i = pl.program_id(0)
    expert = be_ref[i]
    used = i < nb_ref[0]

    def weight_copies(e):
        return [pltpu.make_async_copy(src.at[layer, e], w_f32.at[j], sem.at[j])
                for j, src in enumerate((wg_hbm, wu_hbm, wd_hbm))]

    @pl.when(i == 0)
    def _():
        for cp in weight_copies(expert):
            cp.start()

    @pl.when(used & ((i == 0) | (expert != be_ref[jnp.maximum(i - 1, 0)])))
    def _():
        for cp in weight_copies(expert):
            cp.wait()
        wg_bf[...] = w_f32[0].astype(BF16)
        wu_bf[...] = w_f32[1].astype(BF16)
        wd_bf[...] = w_f32[2].astype(BF16)

        @pl.when(next_ref[i] >= 0)
        def _():
            for cp in weight_copies(next_ref[i]):
                cp.start()

    sub = h_bf.shape[0]
    n_sub = x_ref.shape[0] // sub
    n_rows = jnp.where(used, rows_ref[i], 0)
    n_live = (n_rows + sub - 1) // sub

    def live_sub_block(s, carry):
        rows = pl.ds(pl.multiple_of(s * sub, sub), sub)
        xp = x_ref[rows]
        row = lax.broadcasted_iota(jnp.int32, xp.shape, 0) + s * sub
        xp = jnp.where(row < n_rows, xp, 0)
        x = jnp.concatenate(_unpack_rows(xp), axis=1).astype(BF16)
        for c in range(wg_bf.shape[1] // chunk):
            cs = slice(c * chunk, (c + 1) * chunk)
            g = jnp.minimum(_dot(x, wg_bf[:, cs]) + bg_ref[0, 0, :, cs], SWIGLU_LIMIT)
            u = jnp.clip(_dot(x, wu_bf[:, cs]) + bu_ref[0, 0, :, cs], -SWIGLU_LIMIT, SWIGLU_LIMIT)
            h_bf[:, cs] = (g * jax.nn.sigmoid(SWIGLU_ALPHA * g) * (u + 1.0)).astype(BF16)
        o_ref[rows] = _pack_rows(_dot(h_bf[...], wd_bf[...]) + bd_ref[0, 0])
        return carry

    def empty_sub_block(s, carry):
        o_ref[pl.ds(pl.multiple_of(s * sub, sub), sub)] = jnp.zeros((sub, o_ref.shape[1]), o_ref.dtype)
        return carry

    lax.fori_loop(0, n_live, live_sub_block, 0)
    lax.fori_loop(n_live, n_sub, empty_sub_block, 0)


def _experts(xb, block_e, next_e, block_rows, n_used, layer, wg, bg, wu, bu, wd, bd):
    n_slots = xb.shape[0]
    depth, n_e, d, d_exp = wg.shape
    assert d == d_exp
    tm = EXPERT_TILE
    n_blocks = n_slots // tm
    xmap = lambda i, be, ne, br, nb: (jnp.minimum(i, nb[0] - 1), 0)
    bmap = lambda i, be, ne, br, nb: (layer, be[i], 0, 0)
    hbm = pl.BlockSpec(memory_space=pl.ANY)
    grid_spec = pltpu.PrefetchScalarGridSpec(
        num_scalar_prefetch=4,
        grid=(n_blocks,),
        in_specs=[
            pl.BlockSpec((tm, d // 2), xmap),
            hbm,
            pl.BlockSpec((1, 1, 1, d_exp), bmap),
            hbm,
            pl.BlockSpec((1, 1, 1, d_exp), bmap),
            hbm,
            pl.BlockSpec((1, 1, 1, d), bmap),
        ],
        out_specs=pl.BlockSpec((tm, d // 2), lambda i, be, ne, br, nb: (i, 0)),
        scratch_shapes=[pltpu.VMEM((3, d, d_exp), F32), pltpu.VMEM((d, d_exp), BF16), pltpu.VMEM((d, d_exp), BF16),
                        pltpu.VMEM((d_exp, d), BF16), pltpu.VMEM((EXPERT_SUB, d_exp), BF16),
                        pltpu.SemaphoreType.DMA((3,))],
    )
    return pl.pallas_call(
        functools.partial(_expert_kernel, layer=layer, chunk=256),
        out_shape=jax.ShapeDtypeStruct((n_slots, d // 2), jnp.int32),
        grid_spec=grid_spec,
        compiler_params=_params(("arbitrary",)),
        name="experts",
    )(block_e, next_e, block_rows, n_used, xb, wg, bg.reshape(depth, n_e, 1, d_exp), wu,
      bu.reshape(depth, n_e, 1, d_exp), wd, bd.reshape(depth, n_e, 1, d))


def _combine_kernel(x_ref, g2_ref, gates_ref, y_ref, o_ref):
    o_ref[0] = _moe_residual(x_ref[0], g2_ref[0], gates_ref[0], y_ref[...])


def _combine(x, g2, gates, y_sel, tok_offset):
    b, t, d = x.shape
    tm = int(np.gcd(min(t, ROW_TILE), tok_offset)) if tok_offset else min(t, ROW_TILE)
    n_t = t // tm
    off = tok_offset // tm
    row = lambda bi, i: (bi, i, 0)
    return pl.pallas_call(
        _combine_kernel,
        out_shape=jax.ShapeDtypeStruct((b, t, d), F32),
        grid=(b, n_t),
        in_specs=[
            pl.BlockSpec((1, tm, d), row),
            pl.BlockSpec((1, 1, d), lambda bi, i: (bi, 0, 0)),
            pl.BlockSpec((1, tm, ROUTER_PAD), row),
            pl.BlockSpec((TOP_K, tm, d // 2), lambda bi, i: (0, off + bi * n_t + i, 0)),
        ],
        out_specs=pl.BlockSpec((1, tm, d), row),
        input_output_aliases={0: 0},
        compiler_params=_params(("parallel", "parallel")),
        name="moe_combine",
    )(x, g2, gates, y_sel)


def _route_kernel(ids_ref, upper_ref, dest_ref, counts_ref, run_ref, *, tile):
    p = pl.program_id(0)
    i = pl.program_id(1)
    tm = ids_ref.shape[0]
    ids_t = ids_ref[...].T
    expert = lax.broadcasted_iota(jnp.int32, (N_EXPERTS, tm), 0)
    chosen = [ids_t[k:k + 1, :] == expert for k in range(TOP_K)]
    picks = functools.reduce(jnp.add, [c.astype(F32) for c in chosen])
    tile_counts = jnp.sum(picks, axis=1, keepdims=True)

    @pl.when((p == 0) & (i == 0))
    def _():
        run_ref[...] = jnp.zeros_like(run_ref)

    @pl.when(p == 0)
    def _():
        run_ref[...] += tile_counts

    @pl.when((p == 1) & (i == 0))
    def _():
        counts = run_ref[...]
        counts_ref[...] = counts.astype(jnp.int32)
        padded = jnp.floor((counts + (tile - 1)) * (1.0 / tile)) * tile
        row = lax.broadcasted_iota(jnp.int32, counts.shape, 0)
        ends = padded
        for s in (1, 2, 4, 8, 16):
            ends = ends + jnp.where(row >= s, pltpu.roll(ends, s, 0), 0.0)
        run_ref[...] = ends - padded

    @pl.when(p == 1)
    def _():
        before = _dot(picks.astype(BF16), upper_ref[...])
        slot = before + run_ref[:, 0:1]
        rows = [jnp.sum(jnp.where(c, slot, 0.0), axis=0, keepdims=True) for c in chosen]
        rows += [jnp.zeros_like(rows[0])] * (dest_ref.shape[0] - TOP_K)
        dest_ref[...] = jnp.concatenate(rows, axis=0).astype(jnp.int32)
        run_ref[...] += tile_counts


def _route(ids):
    n = ids.shape[0]
    tm = int(np.gcd(n, ROUTE_TILE))
    n_tiles = n // tm
    upper = jnp.asarray(np.triu(np.ones((tm, tm), np.float32), 1), BF16)
    dest, counts = pl.pallas_call(
        functools.partial(_route_kernel, tile=EXPERT_TILE),
        out_shape=[jax.ShapeDtypeStruct((SUBLANES, n), jnp.int32),
                   jax.ShapeDtypeStruct((N_EXPERTS, LANES), jnp.int32)],
        grid=(2, n_tiles),
        in_specs=[pl.BlockSpec((tm, ROUTER_PAD), lambda p, i: (i, 0)),
                  pl.BlockSpec((tm, tm), lambda p, i: (0, 0))],
        out_specs=[pl.BlockSpec((SUBLANES, tm), lambda p, i: (0, i * p)),
                   pl.BlockSpec((N_EXPERTS, LANES), lambda p, i: (0, 0))],
        scratch_shapes=[pltpu.VMEM((N_EXPERTS, LANES), F32)],
        compiler_params=_params(("arbitrary", "arbitrary")),
        name="route",
    )(ids, upper)
    return dest, counts[:, 0]


def _row_gather(table, idx):
    info = plsc.get_sparse_core_info()
    n_cores, n_workers = info.num_cores, info.num_cores * info.num_subcores
    n_rows, width = idx.shape[0], table.shape[1]
    chunk_rows = SC_GATHER_ROWS
    per_worker = n_rows // n_workers
    n_chunks = per_worker // chunk_rows
    assert per_worker * n_workers == n_rows and n_chunks * chunk_rows == per_worker and n_chunks % 2 == 0
    mesh = plsc.VectorSubcoreMesh(core_axis_name="c", subcore_axis_name="s")

    @functools.partial(
        pl.kernel, mesh=mesh,
        out_type=jax.ShapeDtypeStruct((n_rows, width), table.dtype),
        scratch_types=[pltpu.VMEM((n_chunks, chunk_rows), jnp.int32), pltpu.VMEM((2, chunk_rows, width), table.dtype),
                       pltpu.SemaphoreType.DMA((2,)), pltpu.SemaphoreType.DMA((2,))],
    )
    def gather_kernel(table_hbm, idx_hbm, out_hbm, idx_v, rows_v, gather_sem, write_sem):
        worker = lax.axis_index("s") * n_cores + lax.axis_index("c")
        pltpu.sync_copy(idx_hbm.at[worker], idx_v)

        def gather(c, slot):
            return pltpu.make_async_copy(table_hbm.at[idx_v.at[c]], rows_v.at[slot], gather_sem.at[slot])

        def write(c, slot):
            first_row = worker * per_worker + c * chunk_rows
            return pltpu.make_async_copy(rows_v.at[slot], out_hbm.at[pl.ds(first_row, chunk_rows)], write_sem.at[slot])

        gather(0, 0).start()

        @pl.loop(0, n_chunks, step=2)
        def _(c0):
            for slot in (0, 1):
                c = c0 + slot
                gather(c, slot).wait()

                @pl.when(c >= 1)
                def _():
                    write(c - 1, 1 - slot).wait()

                @pl.when(c + 1 < n_chunks)
                def _():
                    gather(c + 1, 1 - slot).start()

                write(c, slot).start()

        write(n_chunks - 1, 1).wait()

    return gather_kernel(table, idx.reshape(n_workers, n_chunks, chunk_rows))


def _row_scatter(rows, dest, n_out):
    info = plsc.get_sparse_core_info()
    n_cores, n_workers = info.num_cores, info.num_cores * info.num_subcores
    n_choices, n_rows = dest.shape
    width = rows.shape[1]
    chunk_rows = SC_SCATTER_ROWS
    per_worker = n_rows // (n_workers * chunk_rows)
    assert per_worker * n_workers * chunk_rows == n_rows == rows.shape[0]
    idx = dest.reshape(n_choices, n_workers, per_worker, chunk_rows)
    mesh = plsc.VectorSubcoreMesh(core_axis_name="c", subcore_axis_name="s")

    @functools.partial(
        pl.kernel, mesh=mesh,
        out_type=jax.ShapeDtypeStruct((n_out, width), rows.dtype),
        scratch_types=[pltpu.VMEM((n_choices, per_worker, chunk_rows), jnp.int32),
                       pltpu.VMEM((chunk_rows, width), rows.dtype), pltpu.SemaphoreType.DMA],
    )
    def scatter_kernel(rows_hbm, idx_hbm, out_hbm, idx_v, rows_v, sem):
        worker = lax.axis_index("s") * n_cores + lax.axis_index("c")
        for k in range(n_choices):
            pltpu.sync_copy(idx_hbm.at[k, worker], idx_v.at[k])

        @pl.loop(0, per_worker)
        def _(c):
            first_row = (worker * per_worker + c) * chunk_rows
            pltpu.sync_copy(rows_hbm.at[pl.ds(first_row, chunk_rows)], rows_v)
            copies = [pltpu.async_copy(rows_v, out_hbm.at[idx_v.at[k, c]], sem) for k in range(n_choices)]
            for cp in copies:
                cp.wait()

    return scatter_kernel(rows, idx)


def _moe(h2, ids, layer, w_exp):
    n = h2.shape[0]
    nk = n * TOP_K
    tm = EXPERT_TILE
    dest, counts = _route(ids)
    dest = dest[:TOP_K]
    padded = (counts + tm - 1) // tm * tm
    pends = jnp.cumsum(padded)
    n_blocks = -(-nk // tm) + N_EXPERTS
    n_slots = n_blocks * tm
    block_start = jnp.arange(n_blocks, dtype=jnp.int32) * tm
    block_e = jnp.minimum(jnp.sum((pends[None] <= block_start[:, None]).astype(jnp.int32), axis=1), N_EXPERTS - 1)
    is_e = (block_e[:, None] == jnp.arange(N_EXPERTS, dtype=jnp.int32)[None]).astype(jnp.int32)
    block_rows = jnp.clip(jnp.sum(is_e * (pends - padded + counts)[None], axis=1) - block_start, 0, tm)
    n_used = (pends[-1] // tm).astype(jnp.int32).reshape(1)
    e_ids = jnp.arange(N_EXPERTS, dtype=jnp.int32)
    later = (e_ids[None] > e_ids[:, None]) & (counts[None] > 0)
    next_nonempty = jnp.min(jnp.where(later, e_ids[None], N_EXPERTS), axis=1)
    next_nonempty = jnp.where(next_nonempty == N_EXPERTS, -1, next_nonempty)
    next_e = jnp.sum(is_e * next_nonempty[None], axis=1)
    xb = _row_scatter(h2, dest, n_slots)
    return _experts(xb, block_e, next_e, block_rows, n_used, layer, *w_exp), dest


def _gather_choices(yb, dest, lo, hi):
    return _row_gather(yb, dest[:, lo:hi].reshape(-1)).reshape(TOP_K, hi - lo, -1)


def _mod_parts(mod_l, b):
    d = mod_l.shape[-1] // 6
    lat = [mod_l[:b, k * d:(k + 1) * d].reshape(b, 1, d) for k in range(6)]
    ctx = [jnp.broadcast_to(mod_l[b, k * d:(k + 1) * d].reshape(1, 1, d), (b, 1, d)) for k in range(6)]
    return lat, ctx


def kernel(x, c, ctx, c_ctx, ada_w, ada_b, norm1_g, norm2_g, ev_w_in, ev_w_out, ev_q_gain, ev_k_gain, ev_rpb, ev_conv_w, ev_conv_b, od_w_in, od_w_out, od_conv_w, od_conv_b, od_fwd_wa, od_fwd_ba, od_fwd_wx, od_fwd_bx, od_fwd_lam, od_bwd_wa, od_bwd_ba, od_bwd_wx, od_bwd_bx, od_bwd_lam, router_w, router_b, exp_w_gate, exp_b_gate, exp_w_up, exp_b_up, exp_w_down, exp_b_down):
    b, t, d = x.shape
    l = ctx.shape[1]
    assert ada_w.shape[0] == DEPTH == 2 and t % GRID_W == 0 and t // GRID_W >= WIN_H

    n_rows_c = -(-(b + 1) // SUBLANES) * SUBLANES
    cvec = jnp.zeros((n_rows_c, d), F32).at[:b].set(c).at[b].set(c_ctx)
    mod = _ada_mod(cvec, ada_w, ada_b)

    def router(layer):
        w_r = jnp.zeros((d, ROUTER_PAD), F32).at[:, :N_EXPERTS].set(router_w[layer]).astype(BF16)
        b_r = jnp.zeros((1, ROUTER_PAD), F32).at[0, :N_EXPERTS].set(router_b[layer])
        return w_r, b_r

    w_exp = (exp_w_gate, exp_b_gate, exp_w_up, exp_b_up, exp_w_down, exp_b_down)

    (sh1, sc1, g1, sh2, sc2, g2), (csh1, csc1, cg1, csh2, csc2, cg2) = _mod_parts(mod[0], b)
    n1 = norm1_g[0].reshape(1, d)
    n2 = norm2_g[0].reshape(1, d)
    w_in = ev_w_in[0].astype(BF16)
    w_out = ev_w_out[0].astype(BF16)
    q_scale = NA_HEAD_DIM ** -0.5 * LOG2_E
    head_gain = jnp.stack([jnp.tile(ev_q_gain[0] * q_scale, NA_HEADS), jnp.tile(ev_k_gain[0], NA_HEADS)])
    head_gain = head_gain.reshape(2, 1, NA_WIDTH).astype(F32)
    proj = _inproj(x, n1, sh1, sc1, w_in, head_gain)
    proj_c = _inproj(ctx, n1, csh1, csc1, w_in, head_gain)
    kh, r0, row_type, patterns = _na_tables(t // GRID_W)
    bias_tab = _na_bias_table(ev_rpb[0], patterns)
    o_a = _neighbourhood_attention(proj, proj_c, bias_tab, r0, row_type, kh)
    oc_a = _context_attention(proj_c)
    w_r, b_r = router(0)
    n_tok = b * (l + t)
    fresh = (jnp.zeros((n_tok, d // 2), jnp.int32), jnp.zeros((n_tok, ROUTER_PAD), jnp.int32))
    c1, tokens, ids_all, gates_c = _even_out(oc_a, proj_c, ev_conv_w[0], ev_conv_b[0], w_out, ctx, cg1, n2, csh2,
                                             csc2, w_r, b_r, (n_tok, 0), fresh)
    x1, tokens, ids_all, gates = _even_out(o_a, proj, ev_conv_w[0], ev_conv_b[0], w_out, x, g1, n2, sh2, sc2, w_r,
                                           b_r, (n_tok, b * l), (tokens, ids_all))
    yb, dest = _moe(tokens, ids_all, 0, w_exp)
    y_sel = _gather_choices(yb, dest, 0, b * (l + t))
    g2_prev, cg2_prev = g2, cg2

    (sh1, sc1, g1, sh2, sc2, g2), (csh1, csc1, _, _, _, _) = _mod_parts(mod[1], b)
    n1 = norm1_g[1].reshape(1, d)
    n2 = norm2_g[1].reshape(1, d)
    w_in = od_w_in[0].astype(BF16)
    width = w_in.shape[1] // 2
    x, proj = _combine_inproj_conv(x1, g2_prev, gates, y_sel, b * l, n1, sh1, sc1, w_in, od_conv_w[0], od_conv_b[0],
                                   width)
    _, u_ctx = _combine_inproj_conv(c1, cg2_prev, gates_c, y_sel, 0, n1, csh1, csc1, w_in[:, width:], od_conv_w[0],
                                    od_conv_b[0], 0)
    h_dir = []
    for reverse, (wa, ba, wx, bx, lam) in ((False, (od_fwd_wa, od_fwd_ba, od_fwd_wx, od_fwd_bx, od_fwd_lam)),
                                           (True, (od_bwd_wa, od_bwd_ba, od_bwd_wx, od_bwd_bx, od_bwd_lam))):
        w_cat = jnp.concatenate([wa[0], wx[0]], axis=-1).astype(BF16)
        h_dir.append(_lru_scan(proj, u_ctx, w_cat, ba[0], bx[0], lam[0], reverse))
    w_r, b_r = router(1)
    x1, h2, ids, gates = _odd_out(h_dir[0], h_dir[1], proj, od_w_out[0].astype(BF16), x, g1, n2, sh2, sc2, w_r, b_r)
    yb, dest = _moe(h2, ids, 1, w_exp)
    return _combine(x1, g2, gates, _gather_choices(yb, dest, 0, b * t), 0)
```

```python
import functools

import numpy as np
import jax
import jax.numpy as jnp
from jax import lax
from jax.experimental import pallas as pl
from jax.experimental.pallas import tpu as pltpu
from jax.experimental.pallas import tpu_sc as plsc

DEPTH = 2
GRID_W = 64
EPS = 1e-6
NEG_INF = -1e30
LOG2_E = 1.4426950408889634
NA_HEADS = 8
NA_HEAD_DIM = 64
NA_WIDTH = NA_HEADS * NA_HEAD_DIM
HEAD_PAIRS = NA_HEADS // 2
WIN_H = 8
WIN_W = 16
SC_CONV = 3
LRU_BLOCKS = 4
LRU_CONV = 4
LRU_C = 8.0
N_EXPERTS = 32
TOP_K = 4
SWIGLU_LIMIT = 7.0
SWIGLU_ALPHA = 1.702

LANES = 128
SUBLANES = 8
MXU_DEPTH = 256
HALO = 16
ROUTER_PAD = LANES
ADA_COL_TILE = 1536
ROW_TILE = 1024
NA_ROWS_PER_STEP = 8
LRU_SUB_TILES = 8
OUT_TILE = 1024
OUT_SUB_TILE = 512
ROUTE_TILE = 2048
EXPERT_TILE = 2048
EXPERT_SUB = 512
SC_GATHER_ROWS = 64
SC_SCATTER_ROWS = 128
VMEM_LIMIT = 56 * 1024 * 1024

F32 = jnp.float32
BF16 = jnp.bfloat16


def _params(sem, vmem=VMEM_LIMIT):
    return pltpu.CompilerParams(dimension_semantics=sem, vmem_limit_bytes=vmem)


def _dot(a, b):
    return jnp.dot(a, b, preferred_element_type=F32)


def _dot_nt(a, b):
    return lax.dot_general(a, b, (((1,), (1,)), ((), ())), preferred_element_type=F32)


def _pack_rows(v):
    w = v.shape[-1] // 2
    lo = lax.bitcast_convert_type(v[:, :w].astype(BF16).astype(F32), jnp.int32)
    hi = lax.bitcast_convert_type(v[:, w:].astype(BF16).astype(F32), jnp.int32)
    return lax.shift_right_logical(lo, 16) | (hi & jnp.int32(-65536))


def _unpack_rows(p):
    lo = lax.bitcast_convert_type(lax.shift_left(p, 16), F32)
    hi = lax.bitcast_convert_type(p & jnp.int32(-65536), F32)
    return lo, hi


def _rms_mod(x, g, shift, scale):
    ms = jnp.mean(x * x, axis=-1, keepdims=True)
    return (x * lax.rsqrt(ms + EPS)) * (g * (1.0 + scale)) + shift


def _ada_kernel(c_ref, w_ref, b_ref, o_ref):
    c = c_ref[...]
    s = (c * jax.nn.sigmoid(c)).astype(BF16)
    o_ref[0] = _dot(s, w_ref[0].astype(BF16)) + b_ref[0]


def _ada_mod(cvec, ada_w, ada_b):
    depth, d, n = ada_w.shape
    r = cvec.shape[0]
    tn = ADA_COL_TILE
    return pl.pallas_call(
        _ada_kernel,
        out_shape=jax.ShapeDtypeStruct((depth, r, n), F32),
        grid=(depth, n // tn),
        in_specs=[
            pl.BlockSpec((r, d), lambda l, j: (0, 0)),
            pl.BlockSpec((1, d, tn), lambda l, j: (l, 0, j)),
            pl.BlockSpec((1, 1, tn), lambda l, j: (l, 0, j)),
        ],
        out_specs=pl.BlockSpec((1, r, tn), lambda l, j: (l, 0, j)),
        compiler_params=_params(("parallel", "parallel")),
        name="ada_mod",
    )(cvec, ada_w, ada_b.reshape(depth, 1, n))


def _inproj_kernel(x_ref, g_ref, sh_ref, sc_ref, w_ref, hg_ref, ones_ref, o_ref, *, n_tiles, tn, n_headnorm):
    h = _rms_mod(x_ref[0], g_ref[...], sh_ref[0], sc_ref[0]).astype(BF16)
    for j in range(n_tiles):
        y = _dot(h, w_ref[:, j * tn:(j + 1) * tn])
        if j < n_headnorm:
            ysq = (y * y).astype(BF16)
            kw = ones_ref.shape[0]
            ms = jnp.concatenate([_dot(ysq[:, c:c + kw], ones_ref[...]) for c in range(0, tn, kw)], axis=1)
            y = y * lax.rsqrt(ms * (1.0 / NA_HEAD_DIM) + EPS) * hg_ref[j]
        o_ref[0, :, j * tn:(j + 1) * tn] = y.astype(o_ref.dtype)


def _inproj(x, g, shift, scale, w, head_gain):
    b, t, d = x.shape
    n = w.shape[1]
    tn = NA_WIDTH
    tm = min(t, ROW_TILE)
    n_headnorm = head_gain.shape[0]
    hid = np.arange(MXU_DEPTH) // NA_HEAD_DIM
    ones_bd = jnp.asarray((hid[:, None] == hid[None, :]), BF16)
    kern = functools.partial(_inproj_kernel, n_tiles=n // tn, tn=tn, n_headnorm=n_headnorm)
    return pl.pallas_call(
        kern,
        out_shape=jax.ShapeDtypeStruct((b, t, n), BF16),
        grid=(b, t // tm),
        in_specs=[
            pl.BlockSpec((1, tm, d), lambda bi, i: (bi, i, 0)),
            pl.BlockSpec((1, d), lambda bi, i: (0, 0)),
            pl.BlockSpec((1, 1, d), lambda bi, i: (bi, 0, 0)),
            pl.BlockSpec((1, 1, d), lambda bi, i: (bi, 0, 0)),
            pl.BlockSpec((d, n), lambda bi, i: (0, 0)),
            pl.BlockSpec(head_gain.shape, lambda bi, i: (0, 0, 0)),
            pl.BlockSpec((MXU_DEPTH, MXU_DEPTH), lambda bi, i: (0, 0)),
        ],
        out_specs=pl.BlockSpec((1, tm, n), lambda bi, i: (bi, i, 0)),
        compiler_params=_params(("parallel", "parallel")),
        name="inproj",
    )(x, g, shift, scale, w, head_gain, ones_bd)


def _moe_residual(x, g2, gates, y):
    acc_lo = acc_hi = 0.0
    for k in range(TOP_K):
        lo, hi = _unpack_rows(y[k])
        acc_lo = acc_lo + gates[:, k:k + 1] * lo
        acc_hi = acc_hi + gates[:, k:k + 1] * hi
    return x + g2 * jnp.concatenate([acc_lo, acc_hi], axis=1)


def _combine_inproj_conv_kernel(x_ref, xp_ref, xn_ref, gt_ref, gtp_ref, gtn_ref, y_ref, yp_ref, yn_ref, g2_ref,
                                g_ref, sh_ref, sc_ref, w_ref, cw_ref, cb_ref, xo_ref, o_ref, *,
                                n_tiles, n_plain, tn, tm, sub):
    i = pl.program_id(1)
    edge = SUBLANES
    g2 = g2_ref[0]
    x_tile = _moe_residual(x_ref[0], g2, gt_ref[0], y_ref[...])
    xo_ref[0] = x_tile
    x_ext = jnp.concatenate([_moe_residual(xp_ref[0], g2, gtp_ref[0], yp_ref[...]), x_tile,
                             _moe_residual(xn_ref[0], g2, gtn_ref[0], yn_ref[...])], axis=0)
    h = _rms_mod(x_ext, g_ref[...], sh_ref[0], sc_ref[0]).astype(BF16)
    first_step = i == 0
    last_step = i == pl.num_programs(1) - 1
    left = (LRU_CONV - 1) // 2
    n_sub = tm // sub
    for s in range(n_sub):
        h_s = h[s * sub:(s + 1) * sub + 2 * edge]
        rows = slice(s * sub, (s + 1) * sub)
        keep_prev = jnp.where(first_step, 0.0, 1.0) if s == 0 else 1.0
        keep_next = jnp.where(last_step, 0.0, 1.0) if s == n_sub - 1 else 1.0
        for j in range(n_tiles):
            y = _dot(h_s, w_ref[:, j * tn:(j + 1) * tn])
            if j < n_plain:
                o_ref[0, rows, j * tn:(j + 1) * tn] = y[edge:edge + sub].astype(o_ref.dtype)
                continue
            u = jnp.concatenate([y[:edge] * keep_prev, y[edge:edge + sub], y[edge + sub:] * keep_next], axis=0)
            cw = cw_ref[:, (j - n_plain) * tn:(j - n_plain + 1) * tn]
            uc = cb_ref[:, (j - n_plain) * tn:(j - n_plain + 1) * tn]
            n_ext = u.shape[0]
            for k in range(LRU_CONV):
                shifted = u if k == left else pltpu.roll(u, (left - k) % n_ext, 0)
                uc = uc + shifted[edge:edge + sub] * cw[k:k + 1]
            o_ref[0, rows, j * tn:(j + 1) * tn] = uc.astype(o_ref.dtype)


def _combine_inproj_conv(x1, g2, gates, y_sel, tok_offset, g, shift, scale, w, conv_w, conv_b, n_plain_cols):
    b, t, d = x1.shape
    n = w.shape[1]
    tn = NA_WIDTH
    tm = min(t, OUT_TILE)
    hb = tm // SUBLANES
    n_hblocks = t // SUBLANES
    y_hblocks = y_sel.shape[1] // SUBLANES
    assert tok_offset % tm == 0
    kern = functools.partial(_combine_inproj_conv_kernel, n_tiles=n // tn, n_plain=n_plain_cols // tn, tn=tn, tm=tm,
                             sub=min(tm, OUT_SUB_TILE))
    const = lambda bi, i: (0, 0)
    tile = lambda bi, i: (bi, i, 0)
    prev = lambda bi, i: (bi, jnp.maximum(i * hb - 1, 0), 0)
    nxt = lambda bi, i: (bi, jnp.minimum((i + 1) * hb, n_hblocks - 1), 0)
    y_row = lambda bi, i: (tok_offset + bi * t + i * tm) // SUBLANES
    per_b = lambda bi, i: (bi, 0, 0)
    return pl.pallas_call(
        kern,
        out_shape=[jax.ShapeDtypeStruct((b, t, d), F32), jax.ShapeDtypeStruct((b, t, n), BF16)],
        grid=(b, t // tm),
        in_specs=[
            pl.BlockSpec((1, tm, d), tile),
            pl.BlockSpec((1, SUBLANES, d), prev),
            pl.BlockSpec((1, SUBLANES, d), nxt),
            pl.BlockSpec((1, tm, ROUTER_PAD), tile),
            pl.BlockSpec((1, SUBLANES, ROUTER_PAD), prev),
            pl.BlockSpec((1, SUBLANES, ROUTER_PAD), nxt),
            pl.BlockSpec((TOP_K, tm, d // 2), lambda bi, i: (0, y_row(bi, i) // hb, 0)),
            pl.BlockSpec((TOP_K, SUBLANES, d // 2), lambda bi, i: (0, jnp.maximum(y_row(bi, i) - 1, 0), 0)),
            pl.BlockSpec((TOP_K, SUBLANES, d // 2),
                         lambda bi, i: (0, jnp.minimum(y_row(bi, i) + hb, y_hblocks - 1), 0)),
            pl.BlockSpec((1, 1, d), per_b),
            pl.BlockSpec((1, d), const),
            pl.BlockSpec((1, 1, d), per_b),
            pl.BlockSpec((1, 1, d), per_b),
            pl.BlockSpec((d, n), const),
            pl.BlockSpec(conv_w.shape, const),
            pl.BlockSpec((1, conv_w.shape[1]), const),
        ],
        out_specs=[pl.BlockSpec((1, tm, d), tile), pl.BlockSpec((1, tm, n), tile)],
        compiler_params=_params(("parallel", "parallel")),
        name="combine_inproj_conv",
    )(x1, x1, x1, gates, gates, gates, y_sel, y_sel, y_sel, g2, g, shift, scale, w, conv_w, conv_b.reshape(1, -1))


def _na_tables(rows):
    kh = min(WIN_H, rows)
    r = np.arange(rows)
    r0 = np.clip(r - kh // 2, 0, rows - kh)
    dr = r0[:, None] + np.arange(kh)[None] - r[:, None] + WIN_H - 1
    patterns, row_type = np.unique(dr, axis=0, return_inverse=True)
    return kh, r0.astype(np.int32), row_type.reshape(-1).astype(np.int32), patterns


def _na_bias_table(rpb, patterns):
    qc = np.arange(GRID_W)
    kc = np.arange(GRID_W)
    c0 = np.clip(qc - WIN_W // 2, 0, GRID_W - WIN_W)[:, None]
    valid = (kc[None] >= c0) & (kc[None] < c0 + WIN_W)
    dc = np.clip(kc[None] - qc[:, None] + WIN_W - 1, 0, 2 * WIN_W - 2)
    n_pat, kh = patterns.shape
    onehot_dc = jnp.asarray(dc[None] == np.arange(2 * WIN_W - 1)[:, None, None], F32)
    tab = jnp.einsum('hpic,cqk->hpiqk', rpb.astype(F32)[:, patterns], onehot_dc,
                     precision=lax.Precision.HIGHEST)
    tab = jnp.where(valid[None, None, None], tab * LOG2_E, NEG_INF)
    tab = tab.reshape(HEAD_PAIRS, 2, n_pat, kh, GRID_W, GRID_W)
    tab = tab.transpose(2, 0, 1, 4, 3, 5)
    return tab.reshape(n_pat, HEAD_PAIRS, 2 * GRID_W, kh * GRID_W)


def _pair_attention(q, keys, values, biases):
    m = q.shape[0]
    qq = _stack_heads(q)
    scores = []
    for k, bias in zip(keys, biases):
        s = _dot_nt(qq, k)
        scores.append(s if bias is None else s + bias)
    s = jnp.concatenate(scores, axis=1)
    e = jnp.exp2(s - jnp.max(s, axis=-1, keepdims=True))
    denom = jnp.sum(e, axis=-1, keepdims=True)
    e = e.astype(BF16)
    o, start = 0.0, 0
    for v in values:
        o = o + _dot(e[:, start:start + v.shape[0]], v)
        start += v.shape[0]
    o = o * (1.0 / denom)
    lane_o = lax.broadcasted_iota(jnp.int32, (m, LANES), 1)
    return jnp.where(lane_o < NA_HEAD_DIM, o[:m], o[m:])


def _stack_heads(q):
    lane = lax.broadcasted_iota(jnp.int32, q.shape, 1)
    zero = jnp.zeros_like(q)
    return jnp.concatenate([jnp.where(lane < NA_HEAD_DIM, q, zero), jnp.where(lane >= NA_HEAD_DIM, q, zero)], axis=0)


def _na_kernel(r0_ref, type_ref, q_ref, k_ref, v_ref, kc_ref, vc_ref, bias_ref, o_ref, s_ref, p_ref, *,
               kh, rows_per_step):
    n_lat = kh * GRID_W
    tiles = [(j, p) for j in range(rows_per_step) for p in range(HEAD_PAIRS)]
    window = []
    for j in range(rows_per_step):
        r = pl.program_id(1) * rows_per_step + j
        window.append((pl.multiple_of(r0_ref[r] * GRID_W, GRID_W), type_ref[r]))

    for idx, (j, p) in enumerate(tiles):
        start, rtype = window[j]
        cols = slice(p * LANES, (p + 1) * LANES)
        qq = _stack_heads(q_ref[0, j * GRID_W:(j + 1) * GRID_W, cols])
        s_ref[idx, :, :n_lat] = _dot_nt(qq, k_ref[0, pl.ds(start, n_lat), cols]) + bias_ref[rtype, p]
        s_ref[idx, :, n_lat:] = _dot_nt(qq, kc_ref[0, :, cols])

    denoms = []
    for idx in range(len(tiles)):
        s = s_ref[idx]
        e = jnp.exp2(s - jnp.max(s, axis=-1, keepdims=True))
        denoms.append(jnp.sum(e, axis=-1, keepdims=True))
        p_ref[idx] = e.astype(BF16)

    lane = lax.broadcasted_iota(jnp.int32, (GRID_W, LANES), 1)
    for idx, (j, p) in enumerate(tiles):
        start, _ = window[j]
        cols = slice(p * LANES, (p + 1) * LANES)
        o = _dot(p_ref[idx, :, :n_lat], v_ref[0, pl.ds(start, n_lat), cols]) + _dot(p_ref[idx, :, n_lat:],
                                                                                 vc_ref[0, :, cols])
        o = o * (1.0 / denoms[idx])
        o = jnp.where(lane < NA_HEAD_DIM, o[:GRID_W], o[GRID_W:])
        o_ref[0, j * GRID_W:(j + 1) * GRID_W, cols] = o.astype(o_ref.dtype)


def _neighbourhood_attention(proj, proj_c, bias_tab, r0, row_type, kh):
    b, t, _ = proj.shape
    l = proj_c.shape[1]
    rows = t // GRID_W
    w = NA_WIDTH
    rps = int(np.gcd(rows, NA_ROWS_PER_STEP))
    q_rows = rps * GRID_W
    grid_spec = pltpu.PrefetchScalarGridSpec(
        num_scalar_prefetch=2,
        grid=(b, rows // rps),
        in_specs=[
            pl.BlockSpec((1, q_rows, w), lambda bi, r, *_: (bi, r, 0)),
            pl.BlockSpec((1, t, w), lambda bi, r, *_: (bi, 0, 1)),
            pl.BlockSpec((1, t, w), lambda bi, r, *_: (bi, 0, 2)),
            pl.BlockSpec((1, l, w), lambda bi, r, *_: (bi, 0, 1)),
            pl.BlockSpec((1, l, w), lambda bi, r, *_: (bi, 0, 2)),
            pl.BlockSpec(bias_tab.shape, lambda bi, r, *_: (0, 0, 0, 0)),
        ],
        out_specs=pl.BlockSpec((1, q_rows, w), lambda bi, r, *_: (bi, r, 0)),
        scratch_shapes=[pltpu.VMEM((rps * HEAD_PAIRS, 2 * GRID_W, kh * GRID_W + l), F32),
                        pltpu.VMEM((rps * HEAD_PAIRS, 2 * GRID_W, kh * GRID_W + l), BF16)],
    )
    return pl.pallas_call(
        functools.partial(_na_kernel, kh=kh, rows_per_step=rps),
        out_shape=jax.ShapeDtypeStruct((b, t, w), BF16),
        grid_spec=grid_spec,
        compiler_params=_params(("parallel", "arbitrary")),
        name="na_attention",
    )(jnp.asarray(r0), jnp.asarray(row_type), proj, proj, proj, proj_c, proj_c, bias_tab)


def _ctx_attn_kernel(q_ref, k_ref, v_ref, o_ref):
    for p in range(HEAD_PAIRS):
        cols = slice(p * LANES, (p + 1) * LANES)
        o = _pair_attention(q_ref[0, :, cols], [k_ref[0, :, cols]], [v_ref[0, :, cols]], [None])
        o_ref[0, :, cols] = o.astype(o_ref.dtype)


def _context_attention(proj_c):
    b, l, _ = proj_c.shape
    w = NA_WIDTH
    return pl.pallas_call(
        _ctx_attn_kernel,
        out_shape=jax.ShapeDtypeStruct((b, l, w), BF16),
        grid=(b,),
        in_specs=[pl.BlockSpec((1, l, w), lambda bi, j=j: (bi, 0, j)) for j in range(3)],
        out_specs=pl.BlockSpec((1, l, w), lambda bi: (bi, 0, 0)),
        compiler_params=_params(("parallel",)),
        name="ctx_attention",
    )(proj_c, proj_c, proj_c)


def _top4(logits):
    lane = lax.broadcasted_iota(jnp.int32, logits.shape, 1)
    cur = jnp.where(lane < N_EXPERTS, logits, -jnp.inf)
    vals, idxs = [], []
    for _ in range(TOP_K):
        m = jnp.max(cur, axis=-1, keepdims=True)
        first = jnp.min(jnp.where(cur == m, lane, ROUTER_PAD).astype(F32), axis=-1, keepdims=True)
        idx = first.astype(jnp.int32)
        vals.append(m)
        idxs.append(idx)
        cur = jnp.where(lane == idx, -jnp.inf, cur)
    exps = [jnp.exp(v - vals[0]) for v in vals]
    inv = 1.0 / functools.reduce(jnp.add, exps)
    ids = jnp.zeros(logits.shape, jnp.int32)
    gates = jnp.zeros(logits.shape, F32)
    for k in range(TOP_K):
        ids = jnp.where(lane == k, idxs[k], ids)
        gates = jnp.where(lane == k, exps[k] * inv, gates)
    return ids, gates


def _layer_tail(y, rows, x_ref, g1_ref, n2_ref, sh2_ref, sc2_ref, wr_ref, br_ref, *rest):
    xo_ref, h2_ref, ids_ref, gates_ref = rest[-4:]
    x_new = x_ref[0, rows] + g1_ref[0] * y
    xo_ref[0, rows] = x_new
    h2 = _rms_mod(x_new, n2_ref[...], sh2_ref[0], sc2_ref[0])
    h2_ref[rows] = _pack_rows(h2)
    logits = _dot(h2.astype(BF16), wr_ref[...]) + br_ref[...]
    ids, gates = _top4(logits)
    ids_ref[rows] = ids
    gates_ref[0, rows] = gates


def _sub_tiles(n_rows):
    sub = min(OUT_SUB_TILE, n_rows)
    return [slice(s, s + sub) for s in range(0, n_rows, sub)]


def _tail_specs(b, t, d, tm, n_inputs_before, tokens, shared):
    n_total, offset = tokens
    n_t = t // tm
    assert offset % tm == 0
    row = lambda bi, i: (bi, i, 0)
    flat = lambda bi, i: (offset // tm + bi * n_t + i, 0)
    per_b = lambda bi, i: (bi, 0, 0)
    const = lambda bi, i: (0, 0)
    extra, aliases = (), {}
    if shared is not None:
        extra = tuple(shared)
        aliases = {n_inputs_before + 7: 1, n_inputs_before + 8: 2}
    in_specs = [
        pl.BlockSpec((1, tm, d), row),
        pl.BlockSpec((1, 1, d), per_b),
        pl.BlockSpec((1, d), const),
        pl.BlockSpec((1, 1, d), per_b),
        pl.BlockSpec((1, 1, d), per_b),
        pl.BlockSpec((d, ROUTER_PAD), const),
        pl.BlockSpec((1, ROUTER_PAD), const),
    ] + [pl.BlockSpec(memory_space=pl.ANY)] * len(extra)
    out_specs = [
        pl.BlockSpec((1, tm, d), row),
        pl.BlockSpec((tm, d // 2), flat),
        pl.BlockSpec((tm, ROUTER_PAD), flat),
        pl.BlockSpec((1, tm, ROUTER_PAD), row),
    ]
    out_shape = [
        jax.ShapeDtypeStruct((b, t, d), F32),
        jax.ShapeDtypeStruct((n_total, d // 2), jnp.int32),
        jax.ShapeDtypeStruct((n_total, ROUTER_PAD), jnp.int32),
        jax.ShapeDtypeStruct((b, t, ROUTER_PAD), F32),
    ]
    return in_specs, out_specs, out_shape, extra, aliases


def _halo_fix(rolled, at_row, halo_row, present):
    n = rolled.shape[0]
    first = at_row < SUBLANES
    assert first or at_row >= n - SUBLANES
    slab = rolled[:SUBLANES] if first else rolled[n - SUBLANES:]
    sub = lax.broadcasted_iota(jnp.int32, slab.shape, 0)
    fill = jnp.where(present, halo_row, jnp.zeros_like(halo_row))
    slab = jnp.where(sub == at_row % SUBLANES, fill, slab)
    return jnp.concatenate([slab, rolled[SUBLANES:]] if first else [rolled[:n - SUBLANES], slab], axis=0)


def _even_out_kernel(oa_ref, bg_ref, cg_ref, xin_ref, cgp_ref, xinp_ref, cgn_ref, xinn_ref, cw_ref, cb_ref,
                     wa_ref, wb_ref, *tail_refs, tm):
    i = pl.program_id(1)
    has_prev = i > 0
    has_next = i < pl.num_programs(1) - 1
    u = cg_ref[0].astype(F32) * xin_ref[0].astype(F32)
    u_prev = (cgp_ref[0].astype(F32) * xinp_ref[0].astype(F32))[HALO - 1:HALO]
    u_next = (cgn_ref[0].astype(F32) * xinn_ref[0].astype(F32))[0:1]
    u_m1 = _halo_fix(pltpu.roll(u, 1, 0), 0, u_prev, has_prev)
    u_p1 = _halo_fix(pltpu.roll(u, tm - 1, 0), tm - 1, u_next, has_next)
    cw = cw_ref[...]
    conv = u_m1 * cw[0:1] + u * cw[1:2] + u_p1 * cw[2:3] + cb_ref[...]
    o_b = (bg_ref[0].astype(F32) * conv).astype(BF16)
    for rows in _sub_tiles(tm):
        y = _dot(oa_ref[0, rows], wa_ref[...]) + _dot(o_b[rows], wb_ref[...])
        _layer_tail(y, rows, *tail_refs)


def _even_out(o_a, proj, conv_w, conv_b, w_out, x, g1, n2, sh2, sc2, w_r, b_r, tokens, shared=None):
    b, t, d = x.shape
    w = NA_WIDTH
    tm = min(t, OUT_TILE)
    hb = tm // HALO
    n_hblocks = t // HALO
    row = lambda bi, i: (bi, i, 0)
    const = lambda bi, i: (0, 0)
    prev = lambda col: (lambda bi, i: (bi, jnp.maximum(i * hb - 1, 0), col))
    nxt = lambda col: (lambda bi, i: (bi, jnp.minimum((i + 1) * hb, n_hblocks - 1), col))
    tail_in, out_specs, out_shape, extra, aliases = _tail_specs(b, t, d, tm, 12, tokens, shared)
    in_specs = [
        pl.BlockSpec((1, tm, w), row),
        pl.BlockSpec((1, tm, w), lambda bi, i: (bi, i, 3)),
        pl.BlockSpec((1, tm, w), lambda bi, i: (bi, i, 4)),
        pl.BlockSpec((1, tm, w), lambda bi, i: (bi, i, 5)),
        pl.BlockSpec((1, HALO, w), prev(4)),
        pl.BlockSpec((1, HALO, w), prev(5)),
        pl.BlockSpec((1, HALO, w), nxt(4)),
        pl.BlockSpec((1, HALO, w), nxt(5)),
        pl.BlockSpec((SC_CONV, w), const),
        pl.BlockSpec((1, w), const),
        pl.BlockSpec((w, d), const),
        pl.BlockSpec((w, d), const),
    ] + tail_in
    return pl.pallas_call(
        functools.partial(_even_out_kernel, tm=tm),
        out_shape=out_shape,
        grid=(b, t // tm),
        in_specs=in_specs,
        out_specs=out_specs,
        input_output_aliases=aliases,
        compiler_params=_params(("parallel", "parallel")),
        name="even_out",
    )(o_a, proj, proj, proj, proj, proj, proj, proj, conv_w, conv_b.reshape(1, w),
      w_out[:w], w_out[w:], x, g1, n2, sh2, sc2, w_r, b_r, *extra)


def _log_sigmoid(x):
    return jnp.minimum(x, 0.0) - jnp.log1p(jnp.exp(-jnp.abs(x)))


def _sigmoid(x):
    return 0.5 + 0.5 * jnp.tanh(0.5 * x)


def _lru_sub_tile(ucb_time, state, perm_ref, w_ref, ba_ref, bx_ref, lam_ref, want_hidden, reverse):
    sub, width = ucb_time.shape
    blk = width // LRU_BLOCKS
    steps = sub // SUBLANES
    uc = _dot(perm_ref[0], ucb_time)
    ucb = uc.astype(BF16)
    za, zx = [], []
    for h in range(LRU_BLOCKS):
        z = _dot(ucb[:, h * blk:(h + 1) * blk], w_ref[h])
        za.append(z[:, :blk])
        zx.append(z[:, blk:])
    r = _sigmoid(jnp.concatenate(za, axis=1) + ba_ref[...])
    gate_i = _sigmoid(jnp.concatenate(zx, axis=1) + bx_ref[...])
    log_a = (LRU_C * _log_sigmoid(lam_ref[...])) * r
    a = jnp.exp(log_a)
    th = jnp.tanh(log_a)
    num = -2.0 * th
    mult = jnp.where(num > 0.0, num * lax.rsqrt(num * (1.0 - th)), 0.0)
    bcoef = mult * gate_i * uc

    grp = lambda v, j: v[j * SUBLANES:(j + 1) * SUBLANES]
    prods, local = [grp(a, 0)], [grp(bcoef, 0)]
    for j in range(1, steps):
        aj = grp(a, j)
        local.append(aj * local[-1] + grp(bcoef, j))
        prods.append(aj * prods[-1])

    carry_in = [None] * SUBLANES
    for s in (range(SUBLANES - 1, -1, -1) if reverse else range(SUBLANES)):
        carry_in[s] = state
        state = prods[-1][s:s + 1] * state + local[-1][s:s + 1]
    if not want_hidden:
        return state, None
    start = jnp.concatenate(carry_in, axis=0)
    hidden = jnp.concatenate([prods[j] * start + local[j] for j in range(steps)], axis=0)
    return state, _dot(perm_ref[1], hidden.astype(BF16)).astype(BF16)


def _lru_tile(u_ref, perm_ref, w_ref, ba_ref, bx_ref, lam_ref, carry_ref, o_ref, *, reverse):
    sub = perm_ref.shape[1]
    n_sub = u_ref.shape[1] // sub
    state = carry_ref[0:1, :]
    for k in (range(n_sub - 1, -1, -1) if reverse else range(n_sub)):
        rows = slice(k * sub, (k + 1) * sub)
        state, hidden = _lru_sub_tile(u_ref[0, rows], state, perm_ref, w_ref, ba_ref, bx_ref, lam_ref,
                                      o_ref is not None, reverse)
        if o_ref is not None:
            o_ref[0, rows] = hidden
    carry_ref[...] = jnp.broadcast_to(state, carry_ref.shape)


def _lru_kernel(uc_ref, ul_ref, perm_ref, w_ref, ba_ref, bx_ref, lam_ref, o_ref, carry_ref, *, n_ctx_tiles, reverse):
    j = pl.program_id(1)
    shared = (perm_ref, w_ref, ba_ref, bx_ref, lam_ref, carry_ref)

    @pl.when(j == 0)
    def _():
        carry_ref[...] = jnp.zeros_like(carry_ref)

    @pl.when(j < n_ctx_tiles)
    def _():
        _lru_tile(uc_ref, *shared, None, reverse=reverse)

    @pl.when(j >= n_ctx_tiles)
    def _():
        _lru_tile(ul_ref, *shared, o_ref, reverse=reverse)


def _lru_scan(proj, u_ctx, w_cat, ba, bx, lam, reverse):
    b, t, _ = proj.shape
    l, width = u_ctx.shape[1], u_ctx.shape[2]
    sub = min(256, l, t)
    tc = sub * min(LRU_SUB_TILES, l // sub)
    tl = sub * min(LRU_SUB_TILES, t // sub)
    n_c, n_l = l // tc, t // tl

    def pos_of(step, n):
        step = jnp.clip(step, 0, n - 1)
        return (n - 1 - step) if reverse else step

    def tile_map(off, n, col):
        return lambda bi, j: (bi, pos_of(j - off, n), col)

    const2 = lambda bi, j: (0, 0)
    in_specs = [
        pl.BlockSpec((1, tc, width), tile_map(0, n_c, 0)),
        pl.BlockSpec((1, tl, width), tile_map(n_c, n_l, 1)),
        pl.BlockSpec((2, sub, sub), lambda bi, j: (0, 0, 0)),
        pl.BlockSpec(w_cat.shape, lambda bi, j: (0, 0, 0)),
        pl.BlockSpec((1, width), const2),
        pl.BlockSpec((1, width), const2),
        pl.BlockSpec((1, width), const2),
    ]
    steps = sub // SUBLANES
    step, block = np.divmod(np.arange(sub), SUBLANES)
    time_of_row = block * steps + (steps - 1 - step if reverse else step)
    perm = (time_of_row[:, None] == np.arange(sub)[None]).astype(np.float32)
    perms = jnp.asarray(np.stack([perm, perm.T]), BF16)
    kern = functools.partial(_lru_kernel, n_ctx_tiles=n_c, reverse=reverse)
    return pl.pallas_call(
        kern,
        out_shape=jax.ShapeDtypeStruct((b, t, width), BF16),
        grid=(b, n_c + n_l),
        in_specs=in_specs,
        out_specs=pl.BlockSpec((1, tl, width), tile_map(n_c, n_l, 0)),
        scratch_shapes=[pltpu.VMEM((SUBLANES, width), F32)],
        compiler_params=_params(("parallel", "arbitrary")),
        name="lru_scan_bwd" if reverse else "lru_scan_fwd",
    )(u_ctx, proj, perms, w_cat, ba.reshape(1, width), bx.reshape(1, width), lam.reshape(1, width))


def _odd_out_kernel(hf_ref, hb_ref, gate_ref, w_ref, *tail_refs):
    for rows in _sub_tiles(hf_ref.shape[1]):
        hsum = hf_ref[0, rows].astype(F32) + hb_ref[0, rows].astype(F32)
        z = hsum * jax.nn.gelu(gate_ref[0, rows].astype(F32), approximate=True)
        y = _dot(z.astype(BF16), w_ref[...])
        _layer_tail(y, rows, *tail_refs)


def _odd_out(h_f, h_b, proj, w_out, x, g1, n2, sh2, sc2, w_r, b_r):
    b, t, d = x.shape
    width = h_f.shape[-1]
    tm = min(t, OUT_TILE)
    row = lambda bi, i: (bi, i, 0)
    tail_in, out_specs, out_shape, _, _ = _tail_specs(b, t, d, tm, 4, (b * t, 0), None)
    in_specs = [
        pl.BlockSpec((1, tm, width), row),
        pl.BlockSpec((1, tm, width), row),
        pl.BlockSpec((1, tm, width), row),
        pl.BlockSpec((width, d), lambda bi, i: (0, 0)),
    ] + tail_in
    return pl.pallas_call(
        _odd_out_kernel,
        out_shape=out_shape,
        grid=(b, t // tm),
        in_specs=in_specs,
        out_specs=out_specs,
        compiler_params=_params(("parallel", "parallel")),
        name="odd_out",
    )(h_f, h_b, proj, w_out, x, g1, n2, sh2, sc2, w_r, b_r)


def _expert_kernel(be_ref, next_ref, rows_ref, nb_ref, x_ref, wg_hbm, bg_ref, wu_hbm, bu_ref, wd_hbm, bd_ref,
                   o_ref, w_f32, wg_bf, wu_bf, wd_bf, h_bf, sem, *, layer, chunk):
    i = pl.program_id(0)
    expert = be_ref[i]
    used = i < nb_ref[0]

    def weight_copies(e):
        return [pltpu.make_async_copy(src.at[layer, e], w_f32.at[j], sem.at[j])
                for j, src in enumerate((wg_hbm, wu_hbm, wd_hbm))]

    @pl.when(i == 0)
    def _():
        for cp in weight_copies(expert):
            cp.start()

    @pl.when(used & ((i == 0) | (expert != be_ref[jnp.maximum(i - 1, 0)])))
    def _():
        for cp in weight_copies(expert):
            cp.wait()
        wg_bf[...] = w_f32[0].astype(BF16)
        wu_bf[...] = w_f32[1].astype(BF16)
        wd_bf[...] = w_f32[2].astype(BF16)

        @pl.when(next_ref[i] >= 0)
        def _():
            for cp in weight_copies(next_ref[i]):
                cp.start()

    sub = h_bf.shape[0]
    n_sub = x_ref.shape[0] // sub
    n_rows = jnp.where(used, rows_ref[i], 0)
    n_live = (n_rows + sub - 1) // sub

    def live_sub_block(s, carry):
        rows = pl.ds(pl.multiple_of(s * sub, sub), sub)
        xp = x_ref[rows]
        row = lax.broadcasted_iota(jnp.int32, xp.shape, 0) + s * sub
        xp = jnp.where(row < n_rows, xp, 0)
        x = jnp.concatenate(_unpack_rows(xp), axis=1).astype(BF16)
        for c in range(wg_bf.shape[1] // chunk):
            cs = slice(c * chunk, (c + 1) * chunk)
            g = jnp.minimum(_dot(x, wg_bf[:, cs]) + bg_ref[0, 0, :, cs], SWIGLU_LIMIT)
            u = jnp.clip(_dot(x, wu_bf[:, cs]) + bu_ref[0, 0, :, cs], -SWIGLU_LIMIT, SWIGLU_LIMIT)
            h_bf[:, cs] = (g * jax.nn.sigmoid(SWIGLU_ALPHA * g) * (u + 1.0)).astype(BF16)
        o_ref[rows] = _pack_rows(_dot(h_bf[...], wd_bf[...]) + bd_ref[0, 0])
        return carry

    def empty_sub_block(s, carry):
        o_ref[pl.ds(pl.multiple_of(s * sub, sub), sub)] = jnp.zeros((sub, o_ref.shape[1]), o_ref.dtype)
        return carry

    lax.fori_loop(0, n_live, live_sub_block, 0)
    lax.fori_loop(n_live, n_sub, empty_sub_block, 0)


def _experts(xb, block_e, next_e, block_rows, n_used, layer, wg, bg, wu, bu, wd, bd):
    n_slots = xb.shape[0]
    depth, n_e, d, d_exp = wg.shape
    assert d == d_exp
    tm = EXPERT_TILE
    n_blocks = n_slots // tm
    xmap = lambda i, be, ne, br, nb: (jnp.minimum(i, nb[0] - 1), 0)
    bmap = lambda i, be, ne, br, nb: (layer, be[i], 0, 0)
    hbm = pl.BlockSpec(memory_space=pl.ANY)
    grid_spec = pltpu.PrefetchScalarGridSpec(
        num_scalar_prefetch=4,
        grid=(n_blocks,),
        in_specs=[
            pl.BlockSpec((tm, d // 2), xmap),
            hbm,
            pl.BlockSpec((1, 1, 1, d_exp), bmap),
            hbm,
            pl.BlockSpec((1, 1, 1, d_exp), bmap),
            hbm,
            pl.BlockSpec((1, 1, 1, d), bmap),
        ],
        out_specs=pl.BlockSpec((tm, d // 2), lambda i, be, ne, br, nb: (i, 0)),
        scratch_shapes=[pltpu.VMEM((3, d, d_exp), F32), pltpu.VMEM((d, d_exp), BF16), pltpu.VMEM((d, d_exp), BF16),
                        pltpu.VMEM((d_exp, d), BF16), pltpu.VMEM((EXPERT_SUB, d_exp), BF16),
                        pltpu.SemaphoreType.DMA((3,))],
    )
    return pl.pallas_call(
        functools.partial(_expert_kernel, layer=layer, chunk=256),
        out_shape=jax.ShapeDtypeStruct((n_slots, d // 2), jnp.int32),
        grid_spec=grid_spec,
        compiler_params=_params(("arbitrary",)),
        name="experts",
    )(block_e, next_e, block_rows, n_used, xb, wg, bg.reshape(depth, n_e, 1, d_exp), wu,
      bu.reshape(depth, n_e, 1, d_exp), wd, bd.reshape(depth, n_e, 1, d))


def _combine_kernel(x_ref, g2_ref, gates_ref, y_ref, o_ref):
    o_ref[0] = _moe_residual(x_ref[0], g2_ref[0], gates_ref[0], y_ref[...])


def _combine(x, g2, gates, y_sel, tok_offset):
    b, t, d = x.shape
    tm = int(np.gcd(min(t, ROW_TILE), tok_offset)) if tok_offset else min(t, ROW_TILE)
    n_t = t // tm
    off = tok_offset // tm
    row = lambda bi, i: (bi, i, 0)
    return pl.pallas_call(
        _combine_kernel,
        out_shape=jax.ShapeDtypeStruct((b, t, d), F32),
        grid=(b, n_t),
        in_specs=[
            pl.BlockSpec((1, tm, d), row),
            pl.BlockSpec((1, 1, d), lambda bi, i: (bi, 0, 0)),
            pl.BlockSpec((1, tm, ROUTER_PAD), row),
            pl.BlockSpec((TOP_K, tm, d // 2), lambda bi, i: (0, off + bi * n_t + i, 0)),
        ],
        out_specs=pl.BlockSpec((1, tm, d), row),
        input_output_aliases={0: 0},
        compiler_params=_params(("parallel", "parallel")),
        name="moe_combine",
    )(x, g2, gates, y_sel)


def _route_kernel(ids_ref, upper_ref, dest_ref, counts_ref, run_ref, *, tile):
    p = pl.program_id(0)
    i = pl.program_id(1)
    tm = ids_ref.shape[0]
    ids_t = ids_ref[...].T
    expert = lax.broadcasted_iota(jnp.int32, (N_EXPERTS, tm), 0)
    chosen = [ids_t[k:k + 1, :] == expert for k in range(TOP_K)]
    picks = functools.reduce(jnp.add, [c.astype(F32) for c in chosen])
    tile_counts = jnp.sum(picks, axis=1, keepdims=True)

    @pl.when((p == 0) & (i == 0))
    def _():
        run_ref[...] = jnp.zeros_like(run_ref)

    @pl.when(p == 0)
    def _():
        run_ref[...] += tile_counts

    @pl.when((p == 1) & (i == 0))
    def _():
        counts = run_ref[...]
        counts_ref[...] = counts.astype(jnp.int32)
        padded = jnp.floor((counts + (tile - 1)) * (1.0 / tile)) * tile
        row = lax.broadcasted_iota(jnp.int32, counts.shape, 0)
        ends = padded
        for s in (1, 2, 4, 8, 16):
            ends = ends + jnp.where(row >= s, pltpu.roll(ends, s, 0), 0.0)
        run_ref[...] = ends - padded

    @pl.when(p == 1)
    def _():
        before = _dot(picks.astype(BF16), upper_ref[...])
        slot = before + run_ref[:, 0:1]
        rows = [jnp.sum(jnp.where(c, slot, 0.0), axis=0, keepdims=True) for c in chosen]
        rows += [jnp.zeros_like(rows[0])] * (dest_ref.shape[0] - TOP_K)
        dest_ref[...] = jnp.concatenate(rows, axis=0).astype(jnp.int32)
        run_ref[...] += tile_counts


def _route(ids):
    n = ids.shape[0]
    tm = int(np.gcd(n, ROUTE_TILE))
    n_tiles = n // tm
    upper = jnp.asarray(np.triu(np.ones((tm, tm), np.float32), 1), BF16)
    dest, counts = pl.pallas_call(
        functools.partial(_route_kernel, tile=EXPERT_TILE),
        out_shape=[jax.ShapeDtypeStruct((SUBLANES, n), jnp.int32),
                   jax.ShapeDtypeStruct((N_EXPERTS, LANES), jnp.int32)],
        grid=(2, n_tiles),
        in_specs=[pl.BlockSpec((tm, ROUTER_PAD), lambda p, i: (i, 0)),
                  pl.BlockSpec((tm, tm), lambda p, i: (0, 0))],
        out_specs=[pl.BlockSpec((SUBLANES, tm), lambda p, i: (0, i * p)),
                   pl.BlockSpec((N_EXPERTS, LANES), lambda p, i: (0, 0))],
        scratch_shapes=[pltpu.VMEM((N_EXPERTS, LANES), F32)],
        compiler_params=_params(("arbitrary", "arbitrary")),
        name="route",
    )(ids, upper)
    return dest, counts[:, 0]


def _row_gather(table, idx):
    info = plsc.get_sparse_core_info()
    n_cores, n_workers = info.num_cores, info.num_cores * info.num_subcores
    n_rows, width = idx.shape[0], table.shape[1]
    chunk_rows = SC_GATHER_ROWS
    per_worker = n_rows // n_workers
    n_chunks = per_worker // chunk_rows
    assert per_worker * n_workers == n_rows and n_chunks * chunk_rows == per_worker and n_chunks % 2 == 0
    mesh = plsc.VectorSubcoreMesh(core_axis_name="c", subcore_axis_name="s")

    @functools.partial(
        pl.kernel, mesh=mesh,
        out_type=jax.ShapeDtypeStruct((n_rows, width), table.dtype),
        scratch_types=[pltpu.VMEM((n_chunks, chunk_rows), jnp.int32), pltpu.VMEM((2, chunk_rows, width), table.dtype),
                       pltpu.SemaphoreType.DMA((2,)), pltpu.SemaphoreType.DMA((2,))],
    )
    def gather_kernel(table_hbm, idx_hbm, out_hbm, idx_v, rows_v, gather_sem, write_sem):
        worker = lax.axis_index("s") * n_cores + lax.axis_index("c")
        pltpu.sync_copy(idx_hbm.at[worker], idx_v)

        def gather(c, slot):
            return pltpu.make_async_copy(table_hbm.at[idx_v.at[c]], rows_v.at[slot], gather_sem.at[slot])

        def write(c, slot):
            first_row = worker * per_worker + c * chunk_rows
            return pltpu.make_async_copy(rows_v.at[slot], out_hbm.at[pl.ds(first_row, chunk_rows)], write_sem.at[slot])

        gather(0, 0).start()

        @pl.loop(0, n_chunks, step=2)
        def _(c0):
            for slot in (0, 1):
                c = c0 + slot
                gather(c, slot).wait()

                @pl.when(c >= 1)
                def _():
                    write(c - 1, 1 - slot).wait()

                @pl.when(c + 1 < n_chunks)
                def _():
                    gather(c + 1, 1 - slot).start()

                write(c, slot).start()

        write(n_chunks - 1, 1).wait()

    return gather_kernel(table, idx.reshape(n_workers, n_chunks, chunk_rows))


def _row_scatter(rows, dest, n_out):
    info = plsc.get_sparse_core_info()
    n_cores, n_workers = info.num_cores, info.num_cores * info.num_subcores
    n_choices, n_rows = dest.shape
    width = rows.shape[1]
    chunk_rows = SC_SCATTER_ROWS
    per_worker = n_rows // (n_workers * chunk_rows)
    assert per_worker * n_workers * chunk_rows == n_rows == rows.shape[0]
    idx = dest.reshape(n_choices, n_workers, per_worker, chunk_rows)
    mesh = plsc.VectorSubcoreMesh(core_axis_name="c", subcore_axis_name="s")

    @functools.partial(
        pl.kernel, mesh=mesh,
        out_type=jax.ShapeDtypeStruct((n_out, width), rows.dtype),
        scratch_types=[pltpu.VMEM((n_choices, per_worker, chunk_rows), jnp.int32),
                       pltpu.VMEM((chunk_rows, width), rows.dtype), pltpu.SemaphoreType.DMA],
    )
    def scatter_kernel(rows_hbm, idx_hbm, out_hbm, idx_v, rows_v, sem):
        worker = lax.axis_index("s") * n_cores + lax.axis_index("c")
        for k in range(n_choices):
            pltpu.sync_copy(idx_hbm.at[k, worker], idx_v.at[k])

        @pl.loop(0, per_worker)
        def _(c):
            first_row = (worker * per_worker + c) * chunk_rows
            pltpu.sync_copy(rows_hbm.at[pl.ds(first_row, chunk_rows)], rows_v)
            copies = [pltpu.async_copy(rows_v, out_hbm.at[idx_v.at[k, c]], sem) for k in range(n_choices)]
            for cp in copies:
                cp.wait()

    return scatter_kernel(rows, idx)


def _moe(h2, ids, layer, w_exp):
    n = h2.shape[0]
    nk = n * TOP_K
    tm = EXPERT_TILE
    dest, counts = _route(ids)
    dest = dest[:TOP_K]
    padded = (counts + tm - 1) // tm * tm
    pends = jnp.cumsum(padded)
    n_blocks = -(-nk // tm) + N_EXPERTS
    n_slots = n_blocks * tm
    block_start = jnp.arange(n_blocks, dtype=jnp.int32) * tm
    block_e = jnp.minimum(jnp.sum((pends[None] <= block_start[:, None]).astype(jnp.int32), axis=1), N_EXPERTS - 1)
    is_e = (block_e[:, None] == jnp.arange(N_EXPERTS, dtype=jnp.int32)[None]).astype(jnp.int32)
    block_rows = jnp.clip(jnp.sum(is_e * (pends - padded + counts)[None], axis=1) - block_start, 0, tm)
    n_used = (pends[-1] // tm).astype(jnp.int32).reshape(1)
    e_ids = jnp.arange(N_EXPERTS, dtype=jnp.int32)
    later = (e_ids[None] > e_ids[:, None]) & (counts[None] > 0)
    next_nonempty = jnp.min(jnp.where(later, e_ids[None], N_EXPERTS), axis=1)
    next_nonempty = jnp.where(next_nonempty == N_EXPERTS, -1, next_nonempty)
    next_e = jnp.sum(is_e * next_nonempty[None], axis=1)
    xb = _row_scatter(h2, dest, n_slots)
    return _experts(xb, block_e, next_e, block_rows, n_used, layer, *w_exp), dest


def _gather_choices(yb, dest, lo, hi):
    return _row_gather(yb, dest[:, lo:hi].reshape(-1)).reshape(TOP_K, hi - lo, -1)


def _mod_parts(mod_l, b):
    d = mod_l.shape[-1] // 6
    lat = [mod_l[:b, k * d:(k + 1) * d].reshape(b, 1, d) for k in range(6)]
    ctx = [jnp.broadcast_to(mod_l[b, k * d:(k + 1) * d].reshape(1, 1, d), (b, 1, d)) for k in range(6)]
    return lat, ctx


def kernel(x, c, ctx, c_ctx, ada_w, ada_b, norm1_g, norm2_g, ev_w_in, ev_w_out, ev_q_gain, ev_k_gain, ev_rpb, ev_conv_w, ev_conv_b, od_w_in, od_w_out, od_conv_w, od_conv_b, od_fwd_wa, od_fwd_ba, od_fwd_wx, od_fwd_bx, od_fwd_lam, od_bwd_wa, od_bwd_ba, od_bwd_wx, od_bwd_bx, od_bwd_lam, router_w, router_b, exp_w_gate, exp_b_gate, exp_w_up, exp_b_up, exp_w_down, exp_b_down):
    b, t, d = x.shape
    l = ctx.shape[1]
    assert ada_w.shape[0] == DEPTH == 2 and t % GRID_W == 0 and t // GRID_W >= WIN_H

    n_rows_c = -(-(b + 1) // SUBLANES) * SUBLANES
    cvec = jnp.zeros((n_rows_c, d), F32).at[:b].set(c).at[b].set(c_ctx)
    mod = _ada_mod(cvec, ada_w, ada_b)

    def router(layer):
        w_r = jnp.zeros((d, ROUTER_PAD), F32).at[:, :N_EXPERTS].set(router_w[layer]).astype(BF16)
        b_r = jnp.zeros((1, ROUTER_PAD), F32).at[0, :N_EXPERTS].set(router_b[layer])
        return w_r, b_r

    w_exp = (exp_w_gate, exp_b_gate, exp_w_up, exp_b_up, exp_w_down, exp_b_down)

    (sh1, sc1, g1, sh2, sc2, g2), (csh1, csc1, cg1, csh2, csc2, cg2) = _mod_parts(mod[0], b)
    n1 = norm1_g[0].reshape(1, d)
    n2 = norm2_g[0].reshape(1, d)
    w_in = ev_w_in[0].astype(BF16)
    w_out = ev_w_out[0].astype(BF16)
    q_scale = NA_HEAD_DIM ** -0.5 * LOG2_E
    head_gain = jnp.stack([jnp.tile(ev_q_gain[0] * q_scale, NA_HEADS), jnp.tile(ev_k_gain[0], NA_HEADS)])
    head_gain = head_gain.reshape(2, 1, NA_WIDTH).astype(F32)
    proj = _inproj(x, n1, sh1, sc1, w_in, head_gain)
    proj_c = _inproj(ctx, n1, csh1, csc1, w_in, head_gain)
    kh, r0, row_type, patterns = _na_tables(t // GRID_W)
    bias_tab = _na_bias_table(ev_rpb[0], patterns)
    o_a = _neighbourhood_attention(proj, proj_c, bias_tab, r0, row_type, kh)
    oc_a = _context_attention(proj_c)
    w_r, b_r = router(0)
    n_tok = b * (l + t)
    fresh = (jnp.zeros((n_tok, d // 2), jnp.int32), jnp.zeros((n_tok, ROUTER_PAD), jnp.int32))
    c1, tokens, ids_all, gates_c = _even_out(oc_a, proj_c, ev_conv_w[0], ev_conv_b[0], w_out, ctx, cg1, n2, csh2,
                                             csc2, w_r, b_r, (n_tok, 0), fresh)
    x1, tokens, ids_all, gates = _even_out(o_a, proj, ev_conv_w[0], ev_conv_b[0], w_out, x, g1, n2, sh2, sc2, w_r,
                                           b_r, (n_tok, b * l), (tokens, ids_all))
    yb, dest = _moe(tokens, ids_all, 0, w_exp)
    y_sel = _gather_choices(yb, dest, 0, b * (l + t))
    g2_prev, cg2_prev = g2, cg2

    (sh1, sc1, g1, sh2, sc2, g2), (csh1, csc1, _, _, _, _) = _mod_parts(mod[1], b)
    n1 = norm1_g[1].reshape(1, d)
    n2 = norm2_g[1].reshape(1, d)
    w_in = od_w_in[0].astype(BF16)
    width = w_in.shape[1] // 2
    x, proj = _combine_inproj_conv(x1, g2_prev, gates, y_sel, b * l, n1, sh1, sc1, w_in, od_conv_w[0], od_conv_b[0],
                                   width)
    _, u_ctx = _combine_inproj_conv(c1, cg2_prev, gates_c, y_sel, 0, n1, csh1, csc1, w_in[:, width:], od_conv_w[0],
                                    od_conv_b[0], 0)
    h_dir = []
    for reverse, (wa, ba, wx, bx, lam) in ((False, (od_fwd_wa, od_fwd_ba, od_fwd_wx, od_fwd_bx, od_fwd_lam)),
                                           (True, (od_bwd_wa, od_bwd_ba, od_bwd_wx, od_bwd_bx, od_bwd_lam))):
        w_cat = jnp.concatenate([wa[0], wx[0]], axis=-1).astype(BF16)
        h_dir.append(_lru_scan(proj, u_ctx, w_cat, ba[0], bx[0], lam[0], reverse))
    w_r, b_r = router(1)
    x1, h2, ids, gates = _odd_out(h_dir[0], h_dir[1], proj, od_w_out[0].astype(BF16), x, g1, n2, sh2, sc2, w_r, b_r)
    yb, dest = _moe(h2, ids, 1, w_exp)
    return _combine(x1, g2, gates, _gather_choices(yb, dest, 0, b * t), 0)
```

```python
import functools

import numpy as np
import jax
import jax.numpy as jnp
from jax import lax
from jax.experimental import pallas as pl
from jax.experimental.pallas import tpu as pltpu
from jax.experimental.pallas import tpu_sc as plsc

DEPTH = 2
GRID_W = 64
EPS = 1e-6
NEG_INF = -1e30
LOG2_E = 1.4426950408889634
NA_HEADS = 8
NA_HEAD_DIM = 64
NA_WIDTH = NA_HEADS * NA_HEAD_DIM
HEAD_PAIRS = NA_HEADS // 2
WIN_H = 8
WIN_W = 16
SC_CONV = 3
LRU_BLOCKS = 4
LRU_CONV = 4
LRU_C = 8.0
N_EXPERTS = 32
TOP_K = 4
SWIGLU_LIMIT = 7.0
SWIGLU_ALPHA = 1.702

LANES = 128
SUBLANES = 8
MXU_DEPTH = 256
HALO = 16
ROUTER_PAD = LANES
ADA_COL_TILE = 1536
ROW_TILE = 1024
NA_ROWS_PER_STEP = 8
LRU_SUB_TILES = 8
OUT_TILE = 1024
OUT_SUB_TILE = 512
ROUTE_TILE = 2048
EXPERT_TILE = 2048
EXPERT_SUB = 512
SC_GATHER_ROWS = 64
SC_SCATTER_ROWS = 128
VMEM_LIMIT = 56 * 1024 * 1024

F32 = jnp.float32
BF16 = jnp.bfloat16


def _params(sem, vmem=VMEM_LIMIT):
    return pltpu.CompilerParams(dimension_semantics=sem, vmem_limit_bytes=vmem)


def _dot(a, b):
    return jnp.dot(a, b, preferred_element_type=F32)


def _dot_nt(a, b):
    return lax.dot_general(a, b, (((1,), (1,)), ((), ())), preferred_element_type=F32)


def _pack_rows(v):
    w = v.shape[-1] // 2
    lo = lax.bitcast_convert_type(v[:, :w].astype(BF16).astype(F32), jnp.int32)
    hi = lax.bitcast_convert_type(v[:, w:].astype(BF16).astype(F32), jnp.int32)
    return lax.shift_right_logical(lo, 16) | (hi & jnp.int32(-65536))


def _unpack_rows(p):
    lo = lax.bitcast_convert_type(lax.shift_left(p, 16), F32)
    hi = lax.bitcast_convert_type(p & jnp.int32(-65536), F32)
    return lo, hi


def _rms_mod(x, g, shift, scale):
    ms = jnp.mean(x * x, axis=-1, keepdims=True)
    return (x * lax.rsqrt(ms + EPS)) * (g * (1.0 + scale)) + shift


def _ada_kernel(c_ref, w_ref, b_ref, o_ref):
    c = c_ref[...]
    s = (c * jax.nn.sigmoid(c)).astype(BF16)
    o_ref[0] = _dot(s, w_ref[0].astype(BF16)) + b_ref[0]


def _ada_mod(cvec, ada_w, ada_b):
    depth, d, n = ada_w.shape
    r = cvec.shape[0]
    tn = ADA_COL_TILE
    return pl.pallas_call(
        _ada_kernel,
        out_shape=jax.ShapeDtypeStruct((depth, r, n), F32),
        grid=(depth, n // tn),
        in_specs=[
            pl.BlockSpec((r, d), lambda l, j: (0, 0)),
            pl.BlockSpec((1, d, tn), lambda l, j: (l, 0, j)),
            pl.BlockSpec((1, 1, tn), lambda l, j: (l, 0, j)),
        ],
        out_specs=pl.BlockSpec((1, r, tn), lambda l, j: (l, 0, j)),
        compiler_params=_params(("parallel", "parallel")),
        name="ada_mod",
    )(cvec, ada_w, ada_b.reshape(depth, 1, n))


def _inproj_kernel(x_ref, g_ref, sh_ref, sc_ref, w_ref, hg_ref, ones_ref, o_ref, *, n_tiles, tn, n_headnorm):
    h = _rms_mod(x_ref[0], g_ref[...], sh_ref[0], sc_ref[0]).astype(BF16)
    for j in range(n_tiles):
        y = _dot(h, w_ref[:, j * tn:(j + 1) * tn])
        if j < n_headnorm:
            ysq = (y * y).astype(BF16)
            kw = ones_ref.shape[0]
            ms = jnp.concatenate([_dot(ysq[:, c:c + kw], ones_ref[...]) for c in range(0, tn, kw)], axis=1)
            y = y * lax.rsqrt(ms * (1.0 / NA_HEAD_DIM) + EPS) * hg_ref[j]
        o_ref[0, :, j * tn:(j + 1) * tn] = y.astype(o_ref.dtype)


def _inproj(x, g, shift, scale, w, head_gain):
    b, t, d = x.shape
    n = w.shape[1]
    tn = NA_WIDTH
    tm = min(t, ROW_TILE)
    n_headnorm = head_gain.shape[0]
    hid = np.arange(MXU_DEPTH) // NA_HEAD_DIM
    ones_bd = jnp.asarray((hid[:, None] == hid[None, :]), BF16)
    kern = functools.partial(_inproj_kernel, n_tiles=n // tn, tn=tn, n_headnorm=n_headnorm)
    return pl.pallas_call(
        kern,
        out_shape=jax.ShapeDtypeStruct((b, t, n), BF16),
        grid=(b, t // tm),
        in_specs=[
            pl.BlockSpec((1, tm, d), lambda bi, i: (bi, i, 0)),
            pl.BlockSpec((1, d), lambda bi, i: (0, 0)),
            pl.BlockSpec((1, 1, d), lambda bi, i: (bi, 0, 0)),
            pl.BlockSpec((1, 1, d), lambda bi, i: (bi, 0, 0)),
            pl.BlockSpec((d, n), lambda bi, i: (0, 0)),
            pl.BlockSpec(head_gain.shape, lambda bi, i: (0, 0, 0)),
            pl.BlockSpec((MXU_DEPTH, MXU_DEPTH), lambda bi, i: (0, 0)),
        ],
        out_specs=pl.BlockSpec((1, tm, n), lambda bi, i: (bi, i, 0)),
        compiler_params=_params(("parallel", "parallel")),
        name="inproj",
    )(x, g, shift, scale, w, head_gain, ones_bd)


def _moe_residual(x, g2, gates, y):
    acc_lo = acc_hi = 0.0
    for k in range(TOP_K):
        lo, hi = _unpack_rows(y[k])
        acc_lo = acc_lo + gates[:, k:k + 1] * lo
        acc_hi = acc_hi + gates[:, k:k + 1] * hi
    return x + g2 * jnp.concatenate([acc_lo, acc_hi], axis=1)


def _combine_inproj_conv_kernel(x_ref, xp_ref, xn_ref, gt_ref, gtp_ref, gtn_ref, y_ref, yp_ref, yn_ref, g2_ref,
                                g_ref, sh_ref, sc_ref, w_ref, cw_ref, cb_ref, xo_ref, o_ref, *,
                                n_tiles, n_plain, tn, tm, sub):
    i = pl.program_id(1)
    edge = SUBLANES
    g2 = g2_ref[0]
    x_tile = _moe_residual(x_ref[0], g2, gt_ref[0], y_ref[...])
    xo_ref[0] = x_tile
    x_ext = jnp.concatenate([_moe_residual(xp_ref[0], g2, gtp_ref[0], yp_ref[...]), x_tile,
                             _moe_residual(xn_ref[0], g2, gtn_ref[0], yn_ref[...])], axis=0)
    h = _rms_mod(x_ext, g_ref[...], sh_ref[0], sc_ref[0]).astype(BF16)
    first_step = i == 0
    last_step = i == pl.num_programs(1) - 1
    left = (LRU_CONV - 1) // 2
    n_sub = tm // sub
    for s in range(n_sub):
        h_s = h[s * sub:(s + 1) * sub + 2 * edge]
        rows = slice(s * sub, (s + 1) * sub)
        keep_prev = jnp.where(first_step, 0.0, 1.0) if s == 0 else 1.0
        keep_next = jnp.where(last_step, 0.0, 1.0) if s == n_sub - 1 else 1.0
        for j in range(n_tiles):
            y = _dot(h_s, w_ref[:, j * tn:(j + 1) * tn])
            if j < n_plain:
                o_ref[0, rows, j * tn:(j + 1) * tn] = y[edge:edge + sub].astype(o_ref.dtype)
                continue
            u = jnp.concatenate([y[:edge] * keep_prev, y[edge:edge + sub], y[edge + sub:] * keep_next], axis=0)
            cw = cw_ref[:, (j - n_plain) * tn:(j - n_plain + 1) * tn]
            uc = cb_ref[:, (j - n_plain) * tn:(j - n_plain + 1) * tn]
            n_ext = u.shape[0]
            for k in range(LRU_CONV):
                shifted = u if k == left else pltpu.roll(u, (left - k) % n_ext, 0)
                uc = uc + shifted[edge:edge + sub] * cw[k:k + 1]
            o_ref[0, rows, j * tn:(j + 1) * tn] = uc.astype(o_ref.dtype)


def _combine_inproj_conv(x1, g2, gates, y_sel, tok_offset, g, shift, scale, w, conv_w, conv_b, n_plain_cols):
    b, t, d = x1.shape
    n = w.shape[1]
    tn = NA_WIDTH
    tm = min(t, OUT_TILE)
    hb = tm // SUBLANES
    n_hblocks = t // SUBLANES
    y_hblocks = y_sel.shape[1] // SUBLANES
    assert tok_offset % tm == 0
    kern = functools.partial(_combine_inproj_conv_kernel, n_tiles=n // tn, n_plain=n_plain_cols // tn, tn=tn, tm=tm,
                             sub=min(tm, OUT_SUB_TILE))
    const = lambda bi, i: (0, 0)
    tile = lambda bi, i: (bi, i, 0)
    prev = lambda bi, i: (bi, jnp.maximum(i * hb - 1, 0), 0)
    nxt = lambda bi, i: (bi, jnp.minimum((i + 1) * hb, n_hblocks - 1), 0)
    y_row = lambda bi, i: (tok_offset + bi * t + i * tm) // SUBLANES
    per_b = lambda bi, i: (bi, 0, 0)
    return pl.pallas_call(
        kern,
        out_shape=[jax.ShapeDtypeStruct((b, t, d), F32), jax.ShapeDtypeStruct((b, t, n), BF16)],
        grid=(b, t // tm),
        in_specs=[
            pl.BlockSpec((1, tm, d), tile),
            pl.BlockSpec((1, SUBLANES, d), prev),
            pl.BlockSpec((1, SUBLANES, d), nxt),
            pl.BlockSpec((1, tm, ROUTER_PAD), tile),
            pl.BlockSpec((1, SUBLANES, ROUTER_PAD), prev),
            pl.BlockSpec((1, SUBLANES, ROUTER_PAD), nxt),
            pl.BlockSpec((TOP_K, tm, d // 2), lambda bi, i: (0, y_row(bi, i) // hb, 0)),
            pl.BlockSpec((TOP_K, SUBLANES, d // 2), lambda bi, i: (0, jnp.maximum(y_row(bi, i) - 1, 0), 0)),
            pl.BlockSpec((TOP_K, SUBLANES, d // 2),
                         lambda bi, i: (0, jnp.minimum(y_row(bi, i) + hb, y_hblocks - 1), 0)),
            pl.BlockSpec((1, 1, d), per_b),
            pl.BlockSpec((1, d), const),
            pl.BlockSpec((1, 1, d), per_b),
            pl.BlockSpec((1, 1, d), per_b),
            pl.BlockSpec((d, n), const),
            pl.BlockSpec(conv_w.shape, const),
            pl.BlockSpec((1, conv_w.shape[1]), const),
        ],
        out_specs=[pl.BlockSpec((1, tm, d), tile), pl.BlockSpec((1, tm, n), tile)],
        compiler_params=_params(("parallel", "parallel")),
        name="combine_inproj_conv",
    )(x1, x1, x1, gates, gates, gates, y_sel, y_sel, y_sel, g2, g, shift, scale, w, conv_w, conv_b.reshape(1, -1))


def _na_tables(rows):
    kh = min(WIN_H, rows)
    r = np.arange(rows)
    r0 = np.clip(r - kh // 2, 0, rows - kh)
    dr = r0[:, None] + np.arange(kh)[None] - r[:, None] + WIN_H - 1
    patterns, row_type = np.unique(dr, axis=0, return_inverse=True)
    return kh, r0.astype(np.int32), row_type.reshape(-1).astype(np.int32), patterns


def _na_bias_table(rpb, patterns):
    qc = np.arange(GRID_W)
    kc = np.arange(GRID_W)
    c0 = np.clip(qc - WIN_W // 2, 0, GRID_W - WIN_W)[:, None]
    valid = (kc[None] >= c0) & (kc[None] < c0 + WIN_W)
    dc = np.clip(kc[None] - qc[:, None] + WIN_W - 1, 0, 2 * WIN_W - 2)
    n_pat, kh = patterns.shape
    onehot_dc = jnp.asarray(dc[None] == np.arange(2 * WIN_W - 1)[:, None, None], F32)
    tab = jnp.einsum('hpic,cqk->hpiqk', rpb.astype(F32)[:, patterns], onehot_dc,
                     precision=lax.Precision.HIGHEST)
    tab = jnp.where(valid[None, None, None], tab * LOG2_E, NEG_INF)
    tab = tab.reshape(HEAD_PAIRS, 2, n_pat, kh, GRID_W, GRID_W)
    tab = tab.transpose(2, 0, 1, 4, 3, 5)
    return tab.reshape(n_pat, HEAD_PAIRS, 2 * GRID_W, kh * GRID_W)


def _pair_attention(q, keys, values, biases):
    m = q.shape[0]
    qq = _stack_heads(q)
    scores = []
    for k, bias in zip(keys, biases):
        s = _dot_nt(qq, k)
        scores.append(s if bias is None else s + bias)
    s = jnp.concatenate(scores, axis=1)
    e = jnp.exp2(s - jnp.max(s, axis=-1, keepdims=True))
    denom = jnp.sum(e, axis=-1, keepdims=True)
    e = e.astype(BF16)
    o, start = 0.0, 0
    for v in values:
        o = o + _dot(e[:, start:start + v.shape[0]], v)
        start += v.shape[0]
    o = o * (1.0 / denom)
    lane_o = lax.broadcasted_iota(jnp.int32, (m, LANES), 1)
    return jnp.where(lane_o < NA_HEAD_DIM, o[:m], o[m:])


def _stack_heads(q):
    lane = lax.broadcasted_iota(jnp.int32, q.shape, 1)
    zero = jnp.zeros_like(q)
    return jnp.concatenate([jnp.where(lane < NA_HEAD_DIM, q, zero), jnp.where(lane >= NA_HEAD_DIM, q, zero)], axis=0)


def _na_kernel(r0_ref, type_ref, q_ref, k_ref, v_ref, kc_ref, vc_ref, bias_ref, o_ref, s_ref, p_ref, *,
               kh, rows_per_step):
    n_lat = kh * GRID_W
    tiles = [(j, p) for j in range(rows_per_step) for p in range(HEAD_PAIRS)]
    window = []
    for j in range(rows_per_step):
        r = pl.program_id(1) * rows_per_step + j
        window.append((pl.multiple_of(r0_ref[r] * GRID_W, GRID_W), type_ref[r]))

    for idx, (j, p) in enumerate(tiles):
        start, rtype = window[j]
        cols = slice(p * LANES, (p + 1) * LANES)
        qq = _stack_heads(q_ref[0, j * GRID_W:(j + 1) * GRID_W, cols])
        s_ref[idx, :, :n_lat] = _dot_nt(qq, k_ref[0, pl.ds(start, n_lat), cols]) + bias_ref[rtype, p]
        s_ref[idx, :, n_lat:] = _dot_nt(qq, kc_ref[0, :, cols])

    denoms = []
    for idx in range(len(tiles)):
        s = s_ref[idx]
        e = jnp.exp2(s - jnp.max(s, axis=-1, keepdims=True))
        denoms.append(jnp.sum(e, axis=-1, keepdims=True))
        p_ref[idx] = e.astype(BF16)

    lane = lax.broadcasted_iota(jnp.int32, (GRID_W, LANES), 1)
    for idx, (j, p) in enumerate(tiles):
        start, _ = window[j]
        cols = slice(p * LANES, (p + 1) * LANES)
        o = _dot(p_ref[idx, :, :n_lat], v_ref[0, pl.ds(start, n_lat), cols]) + _dot(p_ref[idx, :, n_lat:],
                                                                                 vc_ref[0, :, cols])
        o = o * (1.0 / denoms[idx])
        o = jnp.where(lane < NA_HEAD_DIM, o[:GRID_W], o[GRID_W:])
        o_ref[0, j * GRID_W:(j + 1) * GRID_W, cols] = o.astype(o_ref.dtype)


def _neighbourhood_attention(proj, proj_c, bias_tab, r0, row_type, kh):
    b, t, _ = proj.shape
    l = proj_c.shape[1]
    rows = t // GRID_W
    w = NA_WIDTH
    rps = int(np.gcd(rows, NA_ROWS_PER_STEP))
    q_rows = rps * GRID_W
    grid_spec = pltpu.PrefetchScalarGridSpec(
        num_scalar_prefetch=2,
        grid=(b, rows // rps),
        in_specs=[
            pl.BlockSpec((1, q_rows, w), lambda bi, r, *_: (bi, r, 0)),
            pl.BlockSpec((1, t, w), lambda bi, r, *_: (bi, 0, 1)),
            pl.BlockSpec((1, t, w), lambda bi, r, *_: (bi, 0, 2)),
            pl.BlockSpec((1, l, w), lambda bi, r, *_: (bi, 0, 1)),
            pl.BlockSpec((1, l, w), lambda bi, r, *_: (bi, 0, 2)),
            pl.BlockSpec(bias_tab.shape, lambda bi, r, *_: (0, 0, 0, 0)),
        ],
        out_specs=pl.BlockSpec((1, q_rows, w), lambda bi, r, *_: (bi, r, 0)),
        scratch_shapes=[pltpu.VMEM((rps * HEAD_PAIRS, 2 * GRID_W, kh * GRID_W + l), F32),
                        pltpu.VMEM((rps * HEAD_PAIRS, 2 * GRID_W, kh * GRID_W + l), BF16)],
    )
    return pl.pallas_call(
        functools.partial(_na_kernel, kh=kh, rows_per_step=rps),
        out_shape=jax.ShapeDtypeStruct((b, t, w), BF16),
        grid_spec=grid_spec,
        compiler_params=_params(("parallel", "arbitrary")),
        name="na_attention",
    )(jnp.asarray(r0), jnp.asarray(row_type), proj, proj, proj, proj_c, proj_c, bias_tab)


def _ctx_attn_kernel(q_ref, k_ref, v_ref, o_ref):
    for p in range(HEAD_PAIRS):
        cols = slice(p * LANES, (p + 1) * LANES)
        o = _pair_attention(q_ref[0, :, cols], [k_ref[0, :, cols]], [v_ref[0, :, cols]], [None])
        o_ref[0, :, cols] = o.astype(o_ref.dtype)


def _context_attention(proj_c):
    b, l, _ = proj_c.shape
    w = NA_WIDTH
    return pl.pallas_call(
        _ctx_attn_kernel,
        out_shape=jax.ShapeDtypeStruct((b, l, w), BF16),
        grid=(b,),
        in_specs=[pl.BlockSpec((1, l, w), lambda bi, j=j: (bi, 0, j)) for j in range(3)],
        out_specs=pl.BlockSpec((1, l, w), lambda bi: (bi, 0, 0)),
        compiler_params=_params(("parallel",)),
        name="ctx_attention",
    )(proj_c, proj_c, proj_c)


def _top4(logits):
    lane = lax.broadcasted_iota(jnp.int32, logits.shape, 1)
    cur = jnp.where(lane < N_EXPERTS, logits, -jnp.inf)
    picked = jnp.zeros(logits.shape, F32)
    vals, idxs = [], []
    for _ in range(TOP_K):
        m = jnp.max(cur, axis=-1, keepdims=True)
        first = jnp.min(jnp.where(cur == m, lane, ROUTER_PAD).astype(F32), axis=-1, keepdims=True)
        idx = first.astype(jnp.int32)
        vals.append(m)
        idxs.append(idx)
        hit = lane == idx
        picked = jnp.where(hit, 1.0, picked)
        cur = jnp.where(hit, -jnp.inf, cur)
    exps = [jnp.exp(v - vals[0]) for v in vals]
    inv = 1.0 / functools.reduce(jnp.add, exps)
    ids = jnp.zeros(logits.shape, jnp.int32)
    gates = jnp.zeros(logits.shape, F32)
    for k in range(TOP_K):
        ids = jnp.where(lane == k, idxs[k], ids)
        gates = jnp.where(lane == k, exps[k] * inv, gates)
    return ids, gates, jnp.sum(picked, axis=0, keepdims=True)


def _layer_tail(y, rows, x_ref, g1_ref, n2_ref, sh2_ref, sc2_ref, wr_ref, br_ref, *rest):
    xo_ref, h2_ref, ids_ref, gates_ref = rest[-5:-1]
    x_new = x_ref[0, rows] + g1_ref[0] * y
    xo_ref[0, rows] = x_new
    h2 = _rms_mod(x_new, n2_ref[...], sh2_ref[0], sc2_ref[0])
    h2_ref[rows] = _pack_rows(h2)
    logits = _dot(h2.astype(BF16), wr_ref[...]) + br_ref[...]
    ids, gates, counts = _top4(logits)
    ids_ref[rows] = ids
    gates_ref[0, rows] = gates
    return counts


def _sub_tiles(n_rows):
    sub = min(OUT_SUB_TILE, n_rows)
    return [slice(s, s + sub) for s in range(0, n_rows, sub)]


def _tail_specs(b, t, d, tm, n_inputs_before, tokens, shared):
    n_total, offset = tokens
    n_t = t // tm
    assert offset % tm == 0
    row = lambda bi, i: (bi, i, 0)
    flat = lambda bi, i: (offset // tm + bi * n_t + i, 0)
    per_b = lambda bi, i: (bi, 0, 0)
    const = lambda bi, i: (0, 0)
    extra, aliases = (), {}
    if shared is not None:
        extra = tuple(shared)
        aliases = {n_inputs_before + 7: 1, n_inputs_before + 8: 2}
    in_specs = [
        pl.BlockSpec((1, tm, d), row),
        pl.BlockSpec((1, 1, d), per_b),
        pl.BlockSpec((1, d), const),
        pl.BlockSpec((1, 1, d), per_b),
        pl.BlockSpec((1, 1, d), per_b),
        pl.BlockSpec((d, ROUTER_PAD), const),
        pl.BlockSpec((1, ROUTER_PAD), const),
    ] + [pl.BlockSpec(memory_space=pl.ANY)] * len(extra)
    out_specs = [
        pl.BlockSpec((1, tm, d), row),
        pl.BlockSpec((tm, d // 2), flat),
        pl.BlockSpec((tm, ROUTER_PAD), flat),
        pl.BlockSpec((1, tm, ROUTER_PAD), row),
        pl.BlockSpec((1, SUBLANES, ROUTER_PAD), lambda bi, i: (bi * n_t + i, 0, 0)),
    ]
    out_shape = [
        jax.ShapeDtypeStruct((b, t, d), F32),
        jax.ShapeDtypeStruct((n_total, d // 2), jnp.int32),
        jax.ShapeDtypeStruct((n_total, ROUTER_PAD), jnp.int32),
        jax.ShapeDtypeStruct((b, t, ROUTER_PAD), F32),
        jax.ShapeDtypeStruct((b * n_t, SUBLANES, ROUTER_PAD), F32),
    ]
    return in_specs, out_specs, out_shape, extra, aliases


def _halo_fix(rolled, at_row, halo_row, present):
    n = rolled.shape[0]
    first = at_row < SUBLANES
    assert first or at_row >= n - SUBLANES
    slab = rolled[:SUBLANES] if first else rolled[n - SUBLANES:]
    sub = lax.broadcasted_iota(jnp.int32, slab.shape, 0)
    fill = jnp.where(present, halo_row, jnp.zeros_like(halo_row))
    slab = jnp.where(sub == at_row % SUBLANES, fill, slab)
    return jnp.concatenate([slab, rolled[SUBLANES:]] if first else [rolled[:n - SUBLANES], slab], axis=0)


def _even_out_kernel(oa_ref, bg_ref, cg_ref, xin_ref, cgp_ref, xinp_ref, cgn_ref, xinn_ref, cw_ref, cb_ref,
                     wa_ref, wb_ref, *tail_refs, tm):
    i = pl.program_id(1)
    has_prev = i > 0
    has_next = i < pl.num_programs(1) - 1
    u = cg_ref[0].astype(F32) * xin_ref[0].astype(F32)
    u_prev = (cgp_ref[0].astype(F32) * xinp_ref[0].astype(F32))[HALO - 1:HALO]
    u_next = (cgn_ref[0].astype(F32) * xinn_ref[0].astype(F32))[0:1]
    u_m1 = _halo_fix(pltpu.roll(u, 1, 0), 0, u_prev, has_prev)
    u_p1 = _halo_fix(pltpu.roll(u, tm - 1, 0), tm - 1, u_next, has_next)
    cw = cw_ref[...]
    conv = u_m1 * cw[0:1] + u * cw[1:2] + u_p1 * cw[2:3] + cb_ref[...]
    o_b = (bg_ref[0].astype(F32) * conv).astype(BF16)
    counts = 0.0
    for rows in _sub_tiles(tm):
        y = _dot(oa_ref[0, rows], wa_ref[...]) + _dot(o_b[rows], wb_ref[...])
        counts = counts + _layer_tail(y, rows, *tail_refs)
    tail_refs[-1][0] = jnp.broadcast_to(counts, tail_refs[-1].shape[1:])


def _even_out(o_a, proj, conv_w, conv_b, w_out, x, g1, n2, sh2, sc2, w_r, b_r, tokens, shared=None):
    b, t, d = x.shape
    w = NA_WIDTH
    tm = min(t, OUT_TILE)
    hb = tm // HALO
    n_hblocks = t // HALO
    row = lambda bi, i: (bi, i, 0)
    const = lambda bi, i: (0, 0)
    prev = lambda col: (lambda bi, i: (bi, jnp.maximum(i * hb - 1, 0), col))
    nxt = lambda col: (lambda bi, i: (bi, jnp.minimum((i + 1) * hb, n_hblocks - 1), col))
    tail_in, out_specs, out_shape, extra, aliases = _tail_specs(b, t, d, tm, 12, tokens, shared)
    in_specs = [
        pl.BlockSpec((1, tm, w), row),
        pl.BlockSpec((1, tm, w), lambda bi, i: (bi, i, 3)),
        pl.BlockSpec((1, tm, w), lambda bi, i: (bi, i, 4)),
        pl.BlockSpec((1, tm, w), lambda bi, i: (bi, i, 5)),
        pl.BlockSpec((1, HALO, w), prev(4)),
        pl.BlockSpec((1, HALO, w), prev(5)),
        pl.BlockSpec((1, HALO, w), nxt(4)),
        pl.BlockSpec((1, HALO, w), nxt(5)),
        pl.BlockSpec((SC_CONV, w), const),
        pl.BlockSpec((1, w), const),
        pl.BlockSpec((w, d), const),
        pl.BlockSpec((w, d), const),
    ] + tail_in
    return pl.pallas_call(
        functools.partial(_even_out_kernel, tm=tm),
        out_shape=out_shape,
        grid=(b, t // tm),
        in_specs=in_specs,
        out_specs=out_specs,
        input_output_aliases=aliases,
        compiler_params=_params(("parallel", "parallel")),
        name="even_out",
    )(o_a, proj, proj, proj, proj, proj, proj, proj, conv_w, conv_b.reshape(1, w),
      w_out[:w], w_out[w:], x, g1, n2, sh2, sc2, w_r, b_r, *extra)


def _log_sigmoid(x):
    return jnp.minimum(x, 0.0) - jnp.log1p(jnp.exp(-jnp.abs(x)))


def _sigmoid(x):
    return 0.5 + 0.5 * jnp.tanh(0.5 * x)


def _lru_sub_tile(ucb_time, state, perm_ref, w_ref, ba_ref, bx_ref, lam_ref, want_hidden, reverse):
    sub, width = ucb_time.shape
    blk = width // LRU_BLOCKS
    steps = sub // SUBLANES
    uc = _dot(perm_ref[0], ucb_time)
    ucb = uc.astype(BF16)
    za, zx = [], []
    for h in range(LRU_BLOCKS):
        z = _dot(ucb[:, h * blk:(h + 1) * blk], w_ref[h])
        za.append(z[:, :blk])
        zx.append(z[:, blk:])
    r = _sigmoid(jnp.concatenate(za, axis=1) + ba_ref[...])
    gate_i = _sigmoid(jnp.concatenate(zx, axis=1) + bx_ref[...])
    log_a = (LRU_C * _log_sigmoid(lam_ref[...])) * r
    a = jnp.exp(log_a)
    th = jnp.tanh(log_a)
    num = -2.0 * th
    mult = jnp.where(num > 0.0, num * lax.rsqrt(num * (1.0 - th)), 0.0)
    bcoef = mult * gate_i * uc

    grp = lambda v, j: v[j * SUBLANES:(j + 1) * SUBLANES]
    prods, local = [grp(a, 0)], [grp(bcoef, 0)]
    for j in range(1, steps):
        aj = grp(a, j)
        local.append(aj * local[-1] + grp(bcoef, j))
        prods.append(aj * prods[-1])

    carry_in = [None] * SUBLANES
    for s in (range(SUBLANES - 1, -1, -1) if reverse else range(SUBLANES)):
        carry_in[s] = state
        state = prods[-1][s:s + 1] * state + local[-1][s:s + 1]
    if not want_hidden:
        return state, None
    start = jnp.concatenate(carry_in, axis=0)
    hidden = jnp.concatenate([prods[j] * start + local[j] for j in range(steps)], axis=0)
    return state, _dot(perm_ref[1], hidden.astype(BF16)).astype(BF16)


def _lru_tile(u_ref, perm_ref, w_ref, ba_ref, bx_ref, lam_ref, carry_ref, o_ref, *, reverse):
    sub = perm_ref.shape[1]
    n_sub = u_ref.shape[1] // sub
    state = carry_ref[0:1, :]
    for k in (range(n_sub - 1, -1, -1) if reverse else range(n_sub)):
        rows = slice(k * sub, (k + 1) * sub)
        state, hidden = _lru_sub_tile(u_ref[0, rows], state, perm_ref, w_ref, ba_ref, bx_ref, lam_ref,
                                      o_ref is not None, reverse)
        if o_ref is not None:
            o_ref[0, rows] = hidden
    carry_ref[...] = jnp.broadcast_to(state, carry_ref.shape)


def _lru_kernel(uc_ref, ul_ref, perm_ref, w_ref, ba_ref, bx_ref, lam_ref, o_ref, carry_ref, *, n_ctx_tiles, reverse):
    j = pl.program_id(1)
    shared = (perm_ref, w_ref, ba_ref, bx_ref, lam_ref, carry_ref)

    @pl.when(j == 0)
    def _():
        carry_ref[...] = jnp.zeros_like(carry_ref)

    @pl.when(j < n_ctx_tiles)
    def _():
        _lru_tile(uc_ref, *shared, None, reverse=reverse)

    @pl.when(j >= n_ctx_tiles)
    def _():
        _lru_tile(ul_ref, *shared, o_ref, reverse=reverse)


def _lru_scan(proj, u_ctx, w_cat, ba, bx, lam, reverse):
    b, t, _ = proj.shape
    l, width = u_ctx.shape[1], u_ctx.shape[2]
    sub = min(256, l, t)
    tc = sub * min(LRU_SUB_TILES, l // sub)
    tl = sub * min(LRU_SUB_TILES, t // sub)
    n_c, n_l = l // tc, t // tl

    def pos_of(step, n):
        step = jnp.clip(step, 0, n - 1)
        return (n - 1 - step) if reverse else step

    def tile_map(off, n, col):
        return lambda bi, j: (bi, pos_of(j - off, n), col)

    const2 = lambda bi, j: (0, 0)
    in_specs = [
        pl.BlockSpec((1, tc, width), tile_map(0, n_c, 0)),
        pl.BlockSpec((1, tl, width), tile_map(n_c, n_l, 1)),
        pl.BlockSpec((2, sub, sub), lambda bi, j: (0, 0, 0)),
        pl.BlockSpec(w_cat.shape, lambda bi, j: (0, 0, 0)),
        pl.BlockSpec((1, width), const2),
        pl.BlockSpec((1, width), const2),
        pl.BlockSpec((1, width), const2),
    ]
    steps = sub // SUBLANES
    step, block = np.divmod(np.arange(sub), SUBLANES)
    time_of_row = block * steps + (steps - 1 - step if reverse else step)
    perm = (time_of_row[:, None] == np.arange(sub)[None]).astype(np.float32)
    perms = jnp.asarray(np.stack([perm, perm.T]), BF16)
    kern = functools.partial(_lru_kernel, n_ctx_tiles=n_c, reverse=reverse)
    return pl.pallas_call(
        kern,
        out_shape=jax.ShapeDtypeStruct((b, t, width), BF16),
        grid=(b, n_c + n_l),
        in_specs=in_specs,
        out_specs=pl.BlockSpec((1, tl, width), tile_map(n_c, n_l, 0)),
        scratch_shapes=[pltpu.VMEM((SUBLANES, width), F32)],
        compiler_params=_params(("parallel", "arbitrary")),
        name="lru_scan_bwd" if reverse else "lru_scan_fwd",
    )(u_ctx, proj, perms, w_cat, ba.reshape(1, width), bx.reshape(1, width), lam.reshape(1, width))


def _odd_out_kernel(hf_ref, hb_ref, gate_ref, w_ref, *tail_refs):
    counts = 0.0
    for rows in _sub_tiles(hf_ref.shape[1]):
        hsum = hf_ref[0, rows].astype(F32) + hb_ref[0, rows].astype(F32)
        z = hsum * jax.nn.gelu(gate_ref[0, rows].astype(F32), approximate=True)
        y = _dot(z.astype(BF16), w_ref[...])
        counts = counts + _layer_tail(y, rows, *tail_refs)
    tail_refs[-1][0] = jnp.broadcast_to(counts, tail_refs[-1].shape[1:])


def _odd_out(h_f, h_b, proj, w_out, x, g1, n2, sh2, sc2, w_r, b_r):
    b, t, d = x.shape
    width = h_f.shape[-1]
    tm = min(t, OUT_TILE)
    row = lambda bi, i: (bi, i, 0)
    tail_in, out_specs, out_shape, _, _ = _tail_specs(b, t, d, tm, 4, (b * t, 0), None)
    in_specs = [
        pl.BlockSpec((1, tm, width), row),
        pl.BlockSpec((1, tm, width), row),
        pl.BlockSpec((1, tm, width), row),
        pl.BlockSpec((width, d), lambda bi, i: (0, 0)),
    ] + tail_in
    return pl.pallas_call(
        _odd_out_kernel,
        out_shape=out_shape,
        grid=(b, t // tm),
        in_specs=in_specs,
        out_specs=out_specs,
        compiler_params=_params(("parallel", "parallel")),
        name="odd_out",
    )(h_f, h_b, proj, w_out, x, g1, n2, sh2, sc2, w_r, b_r)


def _expert_kernel(be_ref, next_ref, rows_ref, nb_ref, x_ref, wg_hbm, bg_ref, wu_hbm, bu_ref, wd_hbm, bd_ref,
                   o_ref, w_f32, wg_bf, wu_bf, wd_bf, h_bf, sem, *, layer, chunk):
    i = pl.program_id(0)
    expert = be_ref[i]
    used = i < nb_ref[0]

    def weight_copies(e):
        return [pltpu.make_async_copy(src.at[layer, e], w_f32.at[j], sem.at[j])
                for j, src in enumerate((wg_hbm, wu_hbm, wd_hbm))]

    @pl.when(i == 0)
    def _():
        for cp in weight_copies(expert):
            cp.start()

    @pl.when(used & ((i == 0) | (expert != be_ref[jnp.maximum(i - 1, 0)])))
    def _():
        for cp in weight_copies(expert):
            cp.wait()
        wg_bf[...] = w_f32[0].astype(BF16)
        wu_bf[...] = w_f32[1].astype(BF16)
        wd_bf[...] = w_f32[2].astype(BF16)

        @pl.when(next_ref[i] >= 0)
        def _():
            for cp in weight_copies(next_ref[i]):
                cp.start()

    sub = h_bf.shape[0]
    n_sub = x_ref.shape[0] // sub
    n_rows = jnp.where(used, rows_ref[i], 0)
    n_live = (n_rows + sub - 1) // sub

    def live_sub_block(s, carry):
        rows = pl.ds(pl.multiple_of(s * sub, sub), sub)
        xp = x_ref[rows]
        row = lax.broadcasted_iota(jnp.int32, xp.shape, 0) + s * sub
        xp = jnp.where(row < n_rows, xp, 0)
        x = jnp.concatenate(_unpack_rows(xp), axis=1).astype(BF16)
        for c in range(wg_bf.shape[1] // chunk):
            cs = slice(c * chunk, (c + 1) * chunk)
            g = jnp.minimum(_dot(x, wg_bf[:, cs]) + bg_ref[0, 0, :, cs], SWIGLU_LIMIT)
            u = jnp.clip(_dot(x, wu_bf[:, cs]) + bu_ref[0, 0, :, cs], -SWIGLU_LIMIT, SWIGLU_LIMIT)
            h_bf[:, cs] = (g * jax.nn.sigmoid(SWIGLU_ALPHA * g) * (u + 1.0)).astype(BF16)
        o_ref[rows] = _pack_rows(_dot(h_bf[...], wd_bf[...]) + bd_ref[0, 0])
        return carry

    def empty_sub_block(s, carry):
        o_ref[pl.ds(pl.multiple_of(s * sub, sub), sub)] = jnp.zeros((sub, o_ref.shape[1]), o_ref.dtype)
        return carry

    lax.fori_loop(0, n_live, live_sub_block, 0)
    lax.fori_loop(n_live, n_sub, empty_sub_block, 0)


def _experts(xb, block_e, next_e, block_rows, n_used, layer, wg, bg, wu, bu, wd, bd):
    n_slots = xb.shape[0]
    depth, n_e, d, d_exp = wg.shape
    assert d == d_exp
    tm = EXPERT_TILE
    n_blocks = n_slots // tm
    xmap = lambda i, be, ne, br, nb: (jnp.minimum(i, nb[0] - 1), 0)
    bmap = lambda i, be, ne, br, nb: (layer, be[i], 0, 0)
    hbm = pl.BlockSpec(memory_space=pl.ANY)
    grid_spec = pltpu.PrefetchScalarGridSpec(
        num_scalar_prefetch=4,
        grid=(n_blocks,),
        in_specs=[
            pl.BlockSpec((tm, d // 2), xmap),
            hbm,
            pl.BlockSpec((1, 1, 1, d_exp), bmap),
            hbm,
            pl.BlockSpec((1, 1, 1, d_exp), bmap),
            hbm,
            pl.BlockSpec((1, 1, 1, d), bmap),
        ],
        out_specs=pl.BlockSpec((tm, d // 2), lambda i, be, ne, br, nb: (i, 0)),
        scratch_shapes=[pltpu.VMEM((3, d, d_exp), F32), pltpu.VMEM((d, d_exp), BF16), pltpu.VMEM((d, d_exp), BF16),
                        pltpu.VMEM((d_exp, d), BF16), pltpu.VMEM((EXPERT_SUB, d_exp), BF16),
                        pltpu.SemaphoreType.DMA((3,))],
    )
    return pl.pallas_call(
        functools.partial(_expert_kernel, layer=layer, chunk=256),
        out_shape=jax.ShapeDtypeStruct((n_slots, d // 2), jnp.int32),
        grid_spec=grid_spec,
        compiler_params=_params(("arbitrary",)),
        name="experts",
    )(block_e, next_e, block_rows, n_used, xb, wg, bg.reshape(depth, n_e, 1, d_exp), wu,
      bu.reshape(depth, n_e, 1, d_exp), wd, bd.reshape(depth, n_e, 1, d))


def _combine_kernel(x_ref, g2_ref, gates_ref, y_ref, o_ref):
    o_ref[0] = _moe_residual(x_ref[0], g2_ref[0], gates_ref[0], y_ref[...])


def _combine(x, g2, gates, y_sel, tok_offset):
    b, t, d = x.shape
    tm = int(np.gcd(min(t, ROW_TILE), tok_offset)) if tok_offset else min(t, ROW_TILE)
    n_t = t // tm
    off = tok_offset // tm
    row = lambda bi, i: (bi, i, 0)
    return pl.pallas_call(
        _combine_kernel,
        out_shape=jax.ShapeDtypeStruct((b, t, d), F32),
        grid=(b, n_t),
        in_specs=[
            pl.BlockSpec((1, tm, d), row),
            pl.BlockSpec((1, 1, d), lambda bi, i: (bi, 0, 0)),
            pl.BlockSpec((1, tm, ROUTER_PAD), row),
            pl.BlockSpec((TOP_K, tm, d // 2), lambda bi, i: (0, off + bi * n_t + i, 0)),
        ],
        out_specs=pl.BlockSpec((1, tm, d), row),
        input_output_aliases={0: 0},
        compiler_params=_params(("parallel", "parallel")),
        name="moe_combine",
    )(x, g2, gates, y_sel)


def _route_kernel(ids_ref, upper_ref, starts_ref, dest_ref, run_ref):
    i = pl.program_id(0)
    tm = ids_ref.shape[0]
    ids_t = ids_ref[...].T
    expert = lax.broadcasted_iota(jnp.int32, (N_EXPERTS, tm), 0)
    chosen = [ids_t[k:k + 1, :] == expert for k in range(TOP_K)]
    picks = functools.reduce(jnp.add, [c.astype(F32) for c in chosen])

    @pl.when(i == 0)
    def _():
        run_ref[...] = starts_ref[...]

    before = _dot(picks.astype(BF16), upper_ref[...])
    slot = before + run_ref[:, 0:1]
    rows = [jnp.sum(jnp.where(c, slot, 0.0), axis=0, keepdims=True) for c in chosen]
    rows += [jnp.zeros_like(rows[0])] * (dest_ref.shape[0] - TOP_K)
    dest_ref[...] = jnp.concatenate(rows, axis=0).astype(jnp.int32)
    run_ref[...] += jnp.sum(picks, axis=1, keepdims=True)


def _route(ids, region_starts):
    n = ids.shape[0]
    tm = int(np.gcd(n, ROUTE_TILE))
    upper = jnp.asarray(np.triu(np.ones((tm, tm), np.float32), 1), BF16)
    starts = jnp.broadcast_to(region_starts.astype(F32)[:, None], (N_EXPERTS, LANES))
    return pl.pallas_call(
        _route_kernel,
        out_shape=jax.ShapeDtypeStruct((SUBLANES, n), jnp.int32),
        grid=(n // tm,),
        in_specs=[pl.BlockSpec((tm, ROUTER_PAD), lambda i: (i, 0)),
                  pl.BlockSpec((tm, tm), lambda i: (0, 0)),
                  pl.BlockSpec((N_EXPERTS, LANES), lambda i: (0, 0))],
        out_specs=pl.BlockSpec((SUBLANES, tm), lambda i: (0, i)),
        scratch_shapes=[pltpu.VMEM((N_EXPERTS, LANES), F32)],
        compiler_params=_params(("arbitrary",)),
        name="route",
    )(ids, upper, starts)


def _row_gather(table, idx):
    info = plsc.get_sparse_core_info()
    n_cores, n_workers = info.num_cores, info.num_cores * info.num_subcores
    n_rows, width = idx.shape[0], table.shape[1]
    chunk_rows = SC_GATHER_ROWS
    per_worker = n_rows // n_workers
    n_chunks = per_worker // chunk_rows
    assert per_worker * n_workers == n_rows and n_chunks * chunk_rows == per_worker and n_chunks % 2 == 0
    mesh = plsc.VectorSubcoreMesh(core_axis_name="c", subcore_axis_name="s")

    @functools.partial(
        pl.kernel, mesh=mesh,
        out_type=jax.ShapeDtypeStruct((n_rows, width), table.dtype),
        scratch_types=[pltpu.VMEM((n_chunks, chunk_rows), jnp.int32), pltpu.VMEM((2, chunk_rows, width), table.dtype),
                       pltpu.SemaphoreType.DMA((2,)), pltpu.SemaphoreType.DMA((2,))],
    )
    def gather_kernel(table_hbm, idx_hbm, out_hbm, idx_v, rows_v, gather_sem, write_sem):
        worker = lax.axis_index("s") * n_cores + lax.axis_index("c")
        pltpu.sync_copy(idx_hbm.at[worker], idx_v)

        def gather(c, slot):
            return pltpu.make_async_copy(table_hbm.at[idx_v.at[c]], rows_v.at[slot], gather_sem.at[slot])

        def write(c, slot):
            first_row = worker * per_worker + c * chunk_rows
            return pltpu.make_async_copy(rows_v.at[slot], out_hbm.at[pl.ds(first_row, chunk_rows)], write_sem.at[slot])

        gather(0, 0).start()

        @pl.loop(0, n_chunks, step=2)
        def _(c0):
            for slot in (0, 1):
                c = c0 + slot
                gather(c, slot).wait()

                @pl.when(c >= 1)
                def _():
                    write(c - 1, 1 - slot).wait()

                @pl.when(c + 1 < n_chunks)
                def _():
                    gather(c + 1, 1 - slot).start()

                write(c, slot).start()

        write(n_chunks - 1, 1).wait()

    return gather_kernel(table, idx.reshape(n_workers, n_chunks, chunk_rows))


def _row_scatter(rows, dest, n_out):
    info = plsc.get_sparse_core_info()
    n_cores, n_workers = info.num_cores, info.num_cores * info.num_subcores
    n_choices, n_rows = dest.shape
    width = rows.shape[1]
    chunk_rows = SC_SCATTER_ROWS
    per_worker = n_rows // (n_workers * chunk_rows)
    assert per_worker * n_workers * chunk_rows == n_rows == rows.shape[0]
    idx = dest.reshape(n_choices, n_workers, per_worker, chunk_rows)
    mesh = plsc.VectorSubcoreMesh(core_axis_name="c", subcore_axis_name="s")

    @functools.partial(
        pl.kernel, mesh=mesh,
        out_type=jax.ShapeDtypeStruct((n_out, width), rows.dtype),
        scratch_types=[pltpu.VMEM((n_choices, per_worker, chunk_rows), jnp.int32),
                       pltpu.VMEM((chunk_rows, width), rows.dtype), pltpu.SemaphoreType.DMA],
    )
    def scatter_kernel(rows_hbm, idx_hbm, out_hbm, idx_v, rows_v, sem):
        worker = lax.axis_index("s") * n_cores + lax.axis_index("c")
        for k in range(n_choices):
            pltpu.sync_copy(idx_hbm.at[k, worker], idx_v.at[k])

        @pl.loop(0, per_worker)
        def _(c):
            first_row = (worker * per_worker + c) * chunk_rows
            pltpu.sync_copy(rows_hbm.at[pl.ds(first_row, chunk_rows)], rows_v)
            copies = [pltpu.async_copy(rows_v, out_hbm.at[idx_v.at[k, c]], sem) for k in range(n_choices)]
            for cp in copies:
                cp.wait()

    return scatter_kernel(rows, idx)


def _expert_counts(step_picks):
    return jnp.sum(step_picks[:, 0, :N_EXPERTS], axis=0).astype(jnp.int32)


def _moe(h2, ids, counts, layer, w_exp):
    n = h2.shape[0]
    nk = n * TOP_K
    tm = EXPERT_TILE
    padded = (counts + tm - 1) // tm * tm
    pends = jnp.cumsum(padded)
    dest = _route(ids, pends - padded)[:TOP_K]
    n_blocks = -(-nk // tm) + N_EXPERTS
    n_slots = n_blocks * tm
    block_start = jnp.arange(n_blocks, dtype=jnp.int32) * tm
    block_e = jnp.minimum(jnp.sum((pends[None] <= block_start[:, None]).astype(jnp.int32), axis=1), N_EXPERTS - 1)
    is_e = (block_e[:, None] == jnp.arange(N_EXPERTS, dtype=jnp.int32)[None]).astype(jnp.int32)
    block_rows = jnp.clip(jnp.sum(is_e * (pends - padded + counts)[None], axis=1) - block_start, 0, tm)
    n_used = (pends[-1] // tm).astype(jnp.int32).reshape(1)
    e_ids = jnp.arange(N_EXPERTS, dtype=jnp.int32)
    later = (e_ids[None] > e_ids[:, None]) & (counts[None] > 0)
    next_nonempty = jnp.min(jnp.where(later, e_ids[None], N_EXPERTS), axis=1)
    next_nonempty = jnp.where(next_nonempty == N_EXPERTS, -1, next_nonempty)
    next_e = jnp.sum(is_e * next_nonempty[None], axis=1)
    xb = _row_scatter(h2, dest, n_slots)
    return _experts(xb, block_e, next_e, block_rows, n_used, layer, *w_exp), dest


def _gather_choices(yb, dest, lo, hi):
    return _row_gather(yb, dest[:, lo:hi].reshape(-1)).reshape(TOP_K, hi - lo, -1)


def _mod_parts(mod_l, b):
    d = mod_l.shape[-1] // 6
    lat = [mod_l[:b, k * d:(k + 1) * d].reshape(b, 1, d) for k in range(6)]
    ctx = [jnp.broadcast_to(mod_l[b, k * d:(k + 1) * d].reshape(1, 1, d), (b, 1, d)) for k in range(6)]
    return lat, ctx


def kernel(x, c, ctx, c_ctx, ada_w, ada_b, norm1_g, norm2_g, ev_w_in, ev_w_out, ev_q_gain, ev_k_gain, ev_rpb, ev_conv_w, ev_conv_b, od_w_in, od_w_out, od_conv_w, od_conv_b, od_fwd_wa, od_fwd_ba, od_fwd_wx, od_fwd_bx, od_fwd_lam, od_bwd_wa, od_bwd_ba, od_bwd_wx, od_bwd_bx, od_bwd_lam, router_w, router_b, exp_w_gate, exp_b_gate, exp_w_up, exp_b_up, exp_w_down, exp_b_down):
    b, t, d = x.shape
    l = ctx.shape[1]
    assert ada_w.shape[0] == DEPTH == 2 and t % GRID_W == 0 and t // GRID_W >= WIN_H

    n_rows_c = -(-(b + 1) // SUBLANES) * SUBLANES
    cvec = jnp.zeros((n_rows_c, d), F32).at[:b].set(c).at[b].set(c_ctx)
    mod = _ada_mod(cvec, ada_w, ada_b)

    def router(layer):
        w_r = jnp.zeros((d, ROUTER_PAD), F32).at[:, :N_EXPERTS].set(router_w[layer]).astype(BF16)
        b_r = jnp.zeros((1, ROUTER_PAD), F32).at[0, :N_EXPERTS].set(router_b[layer])
        return w_r, b_r

    w_exp = (exp_w_gate, exp_b_gate, exp_w_up, exp_b_up, exp_w_down, exp_b_down)

    (sh1, sc1, g1, sh2, sc2, g2), (csh1, csc1, cg1, csh2, csc2, cg2) = _mod_parts(mod[0], b)
    n1 = norm1_g[0].reshape(1, d)
    n2 = norm2_g[0].reshape(1, d)
    w_in = ev_w_in[0].astype(BF16)
    w_out = ev_w_out[0].astype(BF16)
    q_scale = NA_HEAD_DIM ** -0.5 * LOG2_E
    head_gain = jnp.stack([jnp.tile(ev_q_gain[0] * q_scale, NA_HEADS), jnp.tile(ev_k_gain[0], NA_HEADS)])
    head_gain = head_gain.reshape(2, 1, NA_WIDTH).astype(F32)
    proj = _inproj(x, n1, sh1, sc1, w_in, head_gain)
    proj_c = _inproj(ctx, n1, csh1, csc1, w_in, head_gain)
    kh, r0, row_type, patterns = _na_tables(t // GRID_W)
    bias_tab = _na_bias_table(ev_rpb[0], patterns)
    o_a = _neighbourhood_attention(proj, proj_c, bias_tab, r0, row_type, kh)
    oc_a = _context_attention(proj_c)
    w_r, b_r = router(0)
    n_tok = b * (l + t)
    fresh = (jnp.zeros((n_tok, d // 2), jnp.int32), jnp.zeros((n_tok, ROUTER_PAD), jnp.int32))
    c1, tokens, ids_all, gates_c, picks_c = _even_out(oc_a, proj_c, ev_conv_w[0], ev_conv_b[0], w_out, ctx, cg1, n2,
                                                      csh2, csc2, w_r, b_r, (n_tok, 0), fresh)
    x1, tokens, ids_all, gates, picks = _even_out(o_a, proj, ev_conv_w[0], ev_conv_b[0], w_out, x, g1, n2, sh2, sc2,
                                                  w_r, b_r, (n_tok, b * l), (tokens, ids_all))
    yb, dest = _moe(tokens, ids_all, _expert_counts(picks_c) + _expert_counts(picks), 0, w_exp)
    y_sel = _gather_choices(yb, dest, 0, b * (l + t))
    g2_prev, cg2_prev = g2, cg2

    (sh1, sc1, g1, sh2, sc2, g2), (csh1, csc1, _, _, _, _) = _mod_parts(mod[1], b)
    n1 = norm1_g[1].reshape(1, d)
    n2 = norm2_g[1].reshape(1, d)
    w_in = od_w_in[0].astype(BF16)
    width = w_in.shape[1] // 2
    x, proj = _combine_inproj_conv(x1, g2_prev, gates, y_sel, b * l, n1, sh1, sc1, w_in, od_conv_w[0], od_conv_b[0],
                                   width)
    _, u_ctx = _combine_inproj_conv(c1, cg2_prev, gates_c, y_sel, 0, n1, csh1, csc1, w_in[:, width:], od_conv_w[0],
                                    od_conv_b[0], 0)
    h_dir = []
    for reverse, (wa, ba, wx, bx, lam) in ((False, (od_fwd_wa, od_fwd_ba, od_fwd_wx, od_fwd_bx, od_fwd_lam)),
                                           (True, (od_bwd_wa, od_bwd_ba, od_bwd_wx, od_bwd_bx, od_bwd_lam))):
        w_cat = jnp.concatenate([wa[0], wx[0]], axis=-1).astype(BF16)
        h_dir.append(_lru_scan(proj, u_ctx, w_cat, ba[0], bx[0], lam[0], reverse))
    w_r, b_r = router(1)
    x1, h2, ids, gates, picks = _odd_out(h_dir[0], h_dir[1], proj, od_w_out[0].astype(BF16), x, g1, n2, sh2, sc2,
                                         w_r, b_r)
    yb, dest = _moe(h2, ids, _expert_counts(picks), 1, w_exp)
    return _combine(x1, g2, gates, _gather_choices(yb, dest, 0, b * t), 0)
```

```python
import functools

import numpy as np
import jax
import jax.numpy as jnp
from jax import lax
from jax.experimental import pallas as pl
from jax.experimental.pallas import tpu as pltpu
from jax.experimental.pallas import tpu_sc as plsc

DEPTH = 2
GRID_W = 64
EPS = 1e-6
NEG_INF = -1e30
LOG2_E = 1.4426950408889634
NA_HEADS = 8
NA_HEAD_DIM = 64
NA_WIDTH = NA_HEADS * NA_HEAD_DIM
HEAD_PAIRS = NA_HEADS // 2
WIN_H = 8
WIN_W = 16
SC_CONV = 3
LRU_BLOCKS = 4
LRU_CONV = 4
LRU_C = 8.0
N_EXPERTS = 32
TOP_K = 4
SWIGLU_LIMIT = 7.0
SWIGLU_ALPHA = 1.702

LANES = 128
SUBLANES = 8
MXU_DEPTH = 256
HALO = 16
ROUTER_PAD = LANES
ADA_COL_TILE = 1536
ROW_TILE = 1024
NA_ROWS_PER_STEP = 8
LRU_SUB_TILES = 8
OUT_TILE = 1024
OUT_SUB_TILE = 512
ROUTE_TILE = 2048
EXPERT_TILE = 1024
EXPERT_SUB = 512
SC_GATHER_ROWS = 64
SC_SCATTER_ROWS = 128
VMEM_LIMIT = 56 * 1024 * 1024

F32 = jnp.float32
BF16 = jnp.bfloat16


def _params(sem, vmem=VMEM_LIMIT):
    return pltpu.CompilerParams(dimension_semantics=sem, vmem_limit_bytes=vmem)


def _dot(a, b):
    return jnp.dot(a, b, preferred_element_type=F32)


def _dot_nt(a, b):
    return lax.dot_general(a, b, (((1,), (1,)), ((), ())), preferred_element_type=F32)


def _pack_rows(v):
    w = v.shape[-1] // 2
    lo = lax.bitcast_convert_type(v[:, :w].astype(BF16).astype(F32), jnp.int32)
    hi = lax.bitcast_convert_type(v[:, w:].astype(BF16).astype(F32), jnp.int32)
    return lax.shift_right_logical(lo, 16) | (hi & jnp.int32(-65536))


def _unpack_rows(p):
    lo = lax.bitcast_convert_type(lax.shift_left(p, 16), F32)
    hi = lax.bitcast_convert_type(p & jnp.int32(-65536), F32)
    return lo, hi


def _rms_mod(x, g, shift, scale):
    ms = jnp.mean(x * x, axis=-1, keepdims=True)
    return (x * lax.rsqrt(ms + EPS)) * (g * (1.0 + scale)) + shift


def _ada_kernel(c_ref, w_ref, b_ref, o_ref):
    c = c_ref[...]
    s = (c * jax.nn.sigmoid(c)).astype(BF16)
    o_ref[0] = _dot(s, w_ref[0].astype(BF16)) + b_ref[0]


def _ada_mod(cvec, ada_w, ada_b):
    depth, d, n = ada_w.shape
    r = cvec.shape[0]
    tn = ADA_COL_TILE
    return pl.pallas_call(
        _ada_kernel,
        out_shape=jax.ShapeDtypeStruct((depth, r, n), F32),
        grid=(depth, n // tn),
        in_specs=[
            pl.BlockSpec((r, d), lambda l, j: (0, 0)),
            pl.BlockSpec((1, d, tn), lambda l, j: (l, 0, j)),
            pl.BlockSpec((1, 1, tn), lambda l, j: (l, 0, j)),
        ],
        out_specs=pl.BlockSpec((1, r, tn), lambda l, j: (l, 0, j)),
        compiler_params=_params(("parallel", "parallel")),
        name="ada_mod",
    )(cvec, ada_w, ada_b.reshape(depth, 1, n))


def _inproj_kernel(x_ref, g_ref, sh_ref, sc_ref, w_ref, hg_ref, ones_ref, o_ref, *, n_tiles, tn, n_headnorm):
    h = _rms_mod(x_ref[0], g_ref[...], sh_ref[0], sc_ref[0]).astype(BF16)
    for j in range(n_tiles):
        y = _dot(h, w_ref[:, j * tn:(j + 1) * tn])
        if j < n_headnorm:
            ysq = (y * y).astype(BF16)
            kw = ones_ref.shape[0]
            ms = jnp.concatenate([_dot(ysq[:, c:c + kw], ones_ref[...]) for c in range(0, tn, kw)], axis=1)
            y = y * lax.rsqrt(ms * (1.0 / NA_HEAD_DIM) + EPS) * hg_ref[j]
        o_ref[0, :, j * tn:(j + 1) * tn] = y.astype(o_ref.dtype)


def _inproj(x, g, shift, scale, w, head_gain):
    b, t, d = x.shape
    n = w.shape[1]
    tn = NA_WIDTH
    tm = min(t, ROW_TILE)
    n_headnorm = head_gain.shape[0]
    hid = np.arange(MXU_DEPTH) // NA_HEAD_DIM
    ones_bd = jnp.asarray((hid[:, None] == hid[None, :]), BF16)
    kern = functools.partial(_inproj_kernel, n_tiles=n // tn, tn=tn, n_headnorm=n_headnorm)
    return pl.pallas_call(
        kern,
        out_shape=jax.ShapeDtypeStruct((b, t, n), BF16),
        grid=(b, t // tm),
        in_specs=[
            pl.BlockSpec((1, tm, d), lambda bi, i: (bi, i, 0)),
            pl.BlockSpec((1, d), lambda bi, i: (0, 0)),
            pl.BlockSpec((1, 1, d), lambda bi, i: (bi, 0, 0)),
            pl.BlockSpec((1, 1, d), lambda bi, i: (bi, 0, 0)),
            pl.BlockSpec((d, n), lambda bi, i: (0, 0)),
            pl.BlockSpec(head_gain.shape, lambda bi, i: (0, 0, 0)),
            pl.BlockSpec((MXU_DEPTH, MXU_DEPTH), lambda bi, i: (0, 0)),
        ],
        out_specs=pl.BlockSpec((1, tm, n), lambda bi, i: (bi, i, 0)),
        compiler_params=_params(("parallel", "parallel")),
        name="inproj",
    )(x, g, shift, scale, w, head_gain, ones_bd)


def _moe_residual(x, g2, gates, y):
    acc_lo = acc_hi = 0.0
    for k in range(TOP_K):
        lo, hi = _unpack_rows(y[k])
        acc_lo = acc_lo + gates[:, k:k + 1] * lo
        acc_hi = acc_hi + gates[:, k:k + 1] * hi
    return x + g2 * jnp.concatenate([acc_lo, acc_hi], axis=1)


def _combine_inproj_conv_kernel(x_ref, xp_ref, xn_ref, gt_ref, gtp_ref, gtn_ref, y_ref, yp_ref, yn_ref, g2_ref,
                                g_ref, sh_ref, sc_ref, w_ref, cw_ref, cb_ref, xo_ref, o_ref, *,
                                n_tiles, n_plain, tn, tm, sub):
    i = pl.program_id(1)
    edge = SUBLANES
    g2 = g2_ref[0]
    x_tile = _moe_residual(x_ref[0], g2, gt_ref[0], y_ref[...])
    xo_ref[0] = x_tile
    x_ext = jnp.concatenate([_moe_residual(xp_ref[0], g2, gtp_ref[0], yp_ref[...]), x_tile,
                             _moe_residual(xn_ref[0], g2, gtn_ref[0], yn_ref[...])], axis=0)
    h = _rms_mod(x_ext, g_ref[...], sh_ref[0], sc_ref[0]).astype(BF16)
    first_step = i == 0
    last_step = i == pl.num_programs(1) - 1
    left = (LRU_CONV - 1) // 2
    n_sub = tm // sub
    for s in range(n_sub):
        h_s = h[s * sub:(s + 1) * sub + 2 * edge]
        rows = slice(s * sub, (s + 1) * sub)
        keep_prev = jnp.where(first_step, 0.0, 1.0) if s == 0 else 1.0
        keep_next = jnp.where(last_step, 0.0, 1.0) if s == n_sub - 1 else 1.0
        for j in range(n_tiles):
            y = _dot(h_s, w_ref[:, j * tn:(j + 1) * tn])
            if j < n_plain:
                o_ref[0, rows, j * tn:(j + 1) * tn] = y[edge:edge + sub].astype(o_ref.dtype)
                continue
            u = jnp.concatenate([y[:edge] * keep_prev, y[edge:edge + sub], y[edge + sub:] * keep_next], axis=0)
            cw = cw_ref[:, (j - n_plain) * tn:(j - n_plain + 1) * tn]
            uc = cb_ref[:, (j - n_plain) * tn:(j - n_plain + 1) * tn]
            n_ext = u.shape[0]
            for k in range(LRU_CONV):
                shifted = u if k == left else pltpu.roll(u, (left - k) % n_ext, 0)
                uc = uc + shifted[edge:edge + sub] * cw[k:k + 1]
            o_ref[0, rows, j * tn:(j + 1) * tn] = uc.astype(o_ref.dtype)


def _combine_inproj_conv(x1, g2, gates, y_sel, tok_offset, g, shift, scale, w, conv_w, conv_b, n_plain_cols):
    b, t, d = x1.shape
    n = w.shape[1]
    tn = NA_WIDTH
    tm = min(t, OUT_TILE)
    hb = tm // SUBLANES
    n_hblocks = t // SUBLANES
    y_hblocks = y_sel.shape[1] // SUBLANES
    assert tok_offset % tm == 0
    kern = functools.partial(_combine_inproj_conv_kernel, n_tiles=n // tn, n_plain=n_plain_cols // tn, tn=tn, tm=tm,
                             sub=min(tm, OUT_SUB_TILE))
    const = lambda bi, i: (0, 0)
    tile = lambda bi, i: (bi, i, 0)
    prev = lambda bi, i: (bi, jnp.maximum(i * hb - 1, 0), 0)
    nxt = lambda bi, i: (bi, jnp.minimum((i + 1) * hb, n_hblocks - 1), 0)
    y_row = lambda bi, i: (tok_offset + bi * t + i * tm) // SUBLANES
    per_b = lambda bi, i: (bi, 0, 0)
    return pl.pallas_call(
        kern,
        out_shape=[jax.ShapeDtypeStruct((b, t, d), F32), jax.ShapeDtypeStruct((b, t, n), BF16)],
        grid=(b, t // tm),
        in_specs=[
            pl.BlockSpec((1, tm, d), tile),
            pl.BlockSpec((1, SUBLANES, d), prev),
            pl.BlockSpec((1, SUBLANES, d), nxt),
            pl.BlockSpec((1, tm, ROUTER_PAD), tile),
            pl.BlockSpec((1, SUBLANES, ROUTER_PAD), prev),
            pl.BlockSpec((1, SUBLANES, ROUTER_PAD), nxt),
            pl.BlockSpec((TOP_K, tm, d // 2), lambda bi, i: (0, y_row(bi, i) // hb, 0)),
            pl.BlockSpec((TOP_K, SUBLANES, d // 2), lambda bi, i: (0, jnp.maximum(y_row(bi, i) - 1, 0), 0)),
            pl.BlockSpec((TOP_K, SUBLANES, d // 2),
                         lambda bi, i: (0, jnp.minimum(y_row(bi, i) + hb, y_hblocks - 1), 0)),
            pl.BlockSpec((1, 1, d), per_b),
            pl.BlockSpec((1, d), const),
            pl.BlockSpec((1, 1, d), per_b),
            pl.BlockSpec((1, 1, d), per_b),
            pl.BlockSpec((d, n), const),
            pl.BlockSpec(conv_w.shape, const),
            pl.BlockSpec((1, conv_w.shape[1]), const),
        ],
        out_specs=[pl.BlockSpec((1, tm, d), tile), pl.BlockSpec((1, tm, n), tile)],
        compiler_params=_params(("parallel", "parallel")),
        name="combine_inproj_conv",
    )(x1, x1, x1, gates, gates, gates, y_sel, y_sel, y_sel, g2, g, shift, scale, w, conv_w, conv_b.reshape(1, -1))


def _na_tables(rows):
    kh = min(WIN_H, rows)
    r = np.arange(rows)
    r0 = np.clip(r - kh // 2, 0, rows - kh)
    dr = r0[:, None] + np.arange(kh)[None] - r[:, None] + WIN_H - 1
    patterns, row_type = np.unique(dr, axis=0, return_inverse=True)
    return kh, r0.astype(np.int32), row_type.reshape(-1).astype(np.int32), patterns


def _na_bias_table(rpb, patterns):
    qc = np.arange(GRID_W)
    kc = np.arange(GRID_W)
    c0 = np.clip(qc - WIN_W // 2, 0, GRID_W - WIN_W)[:, None]
    valid = (kc[None] >= c0) & (kc[None] < c0 + WIN_W)
    dc = np.clip(kc[None] - qc[:, None] + WIN_W - 1, 0, 2 * WIN_W - 2)
    n_pat, kh = patterns.shape
    onehot_dc = jnp.asarray(dc[None] == np.arange(2 * WIN_W - 1)[:, None, None], F32)
    tab = jnp.einsum('hpic,cqk->hpiqk', rpb.astype(F32)[:, patterns], onehot_dc,
                     precision=lax.Precision.HIGHEST)
    tab = jnp.where(valid[None, None, None], tab * LOG2_E, NEG_INF)
    tab = tab.reshape(HEAD_PAIRS, 2, n_pat, kh, GRID_W, GRID_W)
    tab = tab.transpose(2, 0, 1, 4, 3, 5)
    return tab.reshape(n_pat, HEAD_PAIRS, 2 * GRID_W, kh * GRID_W)


def _pair_attention(q, keys, values, biases):
    m = q.shape[0]
    qq = _stack_heads(q)
    scores = []
    for k, bias in zip(keys, biases):
        s = _dot_nt(qq, k)
        scores.append(s if bias is None else s + bias)
    s = jnp.concatenate(scores, axis=1)
    e = jnp.exp2(s - jnp.max(s, axis=-1, keepdims=True))
    denom = jnp.sum(e, axis=-1, keepdims=True)
    e = e.astype(BF16)
    o, start = 0.0, 0
    for v in values:
        o = o + _dot(e[:, start:start + v.shape[0]], v)
        start += v.shape[0]
    o = o * (1.0 / denom)
    lane_o = lax.broadcasted_iota(jnp.int32, (m, LANES), 1)
    return jnp.where(lane_o < NA_HEAD_DIM, o[:m], o[m:])


def _stack_heads(q):
    lane = lax.broadcasted_iota(jnp.int32, q.shape, 1)
    zero = jnp.zeros_like(q)
    return jnp.concatenate([jnp.where(lane < NA_HEAD_DIM, q, zero), jnp.where(lane >= NA_HEAD_DIM, q, zero)], axis=0)


def _na_kernel(r0_ref, type_ref, q_ref, k_ref, v_ref, kc_ref, vc_ref, bias_ref, o_ref, s_ref, p_ref, *,
               kh, rows_per_step):
    n_lat = kh * GRID_W
    tiles = [(j, p) for j in range(rows_per_step) for p in range(HEAD_PAIRS)]
    window = []
    for j in range(rows_per_step):
        r = pl.program_id(1) * rows_per_step + j
        window.append((pl.multiple_of(r0_ref[r] * GRID_W, GRID_W), type_ref[r]))

    for idx, (j, p) in enumerate(tiles):
        start, rtype = window[j]
        cols = slice(p * LANES, (p + 1) * LANES)
        qq = _stack_heads(q_ref[0, j * GRID_W:(j + 1) * GRID_W, cols])
        s_ref[idx, :, :n_lat] = _dot_nt(qq, k_ref[0, pl.ds(start, n_lat), cols]) + bias_ref[rtype, p]
        s_ref[idx, :, n_lat:] = _dot_nt(qq, kc_ref[0, :, cols])

    denoms = []
    for idx in range(len(tiles)):
        s = s_ref[idx]
        e = jnp.exp2(s - jnp.max(s, axis=-1, keepdims=True))
        denoms.append(jnp.sum(e, axis=-1, keepdims=True))
        p_ref[idx] = e.astype(BF16)

    lane = lax.broadcasted_iota(jnp.int32, (GRID_W, LANES), 1)
    for idx, (j, p) in enumerate(tiles):
        start, _ = window[j]
        cols = slice(p * LANES, (p + 1) * LANES)
        o = _dot(p_ref[idx, :, :n_lat], v_ref[0, pl.ds(start, n_lat), cols]) + _dot(p_ref[idx, :, n_lat:],
                                                                                 vc_ref[0, :, cols])
        o = o * (1.0 / denoms[idx])
        o = jnp.where(lane < NA_HEAD_DIM, o[:GRID_W], o[GRID_W:])
        o_ref[0, j * GRID_W:(j + 1) * GRID_W, cols] = o.astype(o_ref.dtype)


def _neighbourhood_attention(proj, proj_c, bias_tab, r0, row_type, kh):
    b, t, _ = proj.shape
    l = proj_c.shape[1]
    rows = t // GRID_W
    w = NA_WIDTH
    rps = int(np.gcd(rows, NA_ROWS_PER_STEP))
    q_rows = rps * GRID_W
    grid_spec = pltpu.PrefetchScalarGridSpec(
        num_scalar_prefetch=2,
        grid=(b, rows // rps),
        in_specs=[
            pl.BlockSpec((1, q_rows, w), lambda bi, r, *_: (bi, r, 0)),
            pl.BlockSpec((1, t, w), lambda bi, r, *_: (bi, 0, 1)),
            pl.BlockSpec((1, t, w), lambda bi, r, *_: (bi, 0, 2)),
            pl.BlockSpec((1, l, w), lambda bi, r, *_: (bi, 0, 1)),
            pl.BlockSpec((1, l, w), lambda bi, r, *_: (bi, 0, 2)),
            pl.BlockSpec(bias_tab.shape, lambda bi, r, *_: (0, 0, 0, 0)),
        ],
        out_specs=pl.BlockSpec((1, q_rows, w), lambda bi, r, *_: (bi, r, 0)),
        scratch_shapes=[pltpu.VMEM((rps * HEAD_PAIRS, 2 * GRID_W, kh * GRID_W + l), F32),
                        pltpu.VMEM((rps * HEAD_PAIRS, 2 * GRID_W, kh * GRID_W + l), BF16)],
    )
    return pl.pallas_call(
        functools.partial(_na_kernel, kh=kh, rows_per_step=rps),
        out_shape=jax.ShapeDtypeStruct((b, t, w), BF16),
        grid_spec=grid_spec,
        compiler_params=_params(("parallel", "arbitrary")),
        name="na_attention",
    )(jnp.asarray(r0), jnp.asarray(row_type), proj, proj, proj, proj_c, proj_c, bias_tab)


def _ctx_attn_kernel(q_ref, k_ref, v_ref, o_ref):
    for p in range(HEAD_PAIRS):
        cols = slice(p * LANES, (p + 1) * LANES)
        o = _pair_attention(q_ref[0, :, cols], [k_ref[0, :, cols]], [v_ref[0, :, cols]], [None])
        o_ref[0, :, cols] = o.astype(o_ref.dtype)


def _context_attention(proj_c):
    b, l, _ = proj_c.shape
    w = NA_WIDTH
    return pl.pallas_call(
        _ctx_attn_kernel,
        out_shape=jax.ShapeDtypeStruct((b, l, w), BF16),
        grid=(b,),
        in_specs=[pl.BlockSpec((1, l, w), lambda bi, j=j: (bi, 0, j)) for j in range(3)],
        out_specs=pl.BlockSpec((1, l, w), lambda bi: (bi, 0, 0)),
        compiler_params=_params(("parallel",)),
        name="ctx_attention",
    )(proj_c, proj_c, proj_c)


def _top4(logits):
    lane = lax.broadcasted_iota(jnp.int32, logits.shape, 1)
    cur = jnp.where(lane < N_EXPERTS, logits, -jnp.inf)
    picked = jnp.zeros(logits.shape, F32)
    vals, idxs = [], []
    for _ in range(TOP_K):
        m = jnp.max(cur, axis=-1, keepdims=True)
        first = jnp.min(jnp.where(cur == m, lane, ROUTER_PAD).astype(F32), axis=-1, keepdims=True)
        idx = first.astype(jnp.int32)
        vals.append(m)
        idxs.append(idx)
        hit = lane == idx
        picked = jnp.where(hit, 1.0, picked)
        cur = jnp.where(hit, -jnp.inf, cur)
    exps = [jnp.exp(v - vals[0]) for v in vals]
    inv = 1.0 / functools.reduce(jnp.add, exps)
    ids = jnp.zeros(logits.shape, jnp.int32)
    gates = jnp.zeros(logits.shape, F32)
    for k in range(TOP_K):
        ids = jnp.where(lane == k, idxs[k], ids)
        gates = jnp.where(lane == k, exps[k] * inv, gates)
    return ids, gates, jnp.sum(picked, axis=0, keepdims=True)


def _layer_tail(y, rows, x_ref, g1_ref, n2_ref, sh2_ref, sc2_ref, wr_ref, br_ref, *rest):
    xo_ref, h2_ref, ids_ref, gates_ref = rest[-5:-1]
    x_new = x_ref[0, rows] + g1_ref[0] * y
    xo_ref[0, rows] = x_new
    h2 = _rms_mod(x_new, n2_ref[...], sh2_ref[0], sc2_ref[0])
    h2_ref[rows] = _pack_rows(h2)
    logits = _dot(h2.astype(BF16), wr_ref[...]) + br_ref[...]
    ids, gates, counts = _top4(logits)
    ids_ref[rows] = ids
    gates_ref[0, rows] = gates
    return counts


def _sub_tiles(n_rows):
    sub = min(OUT_SUB_TILE, n_rows)
    return [slice(s, s + sub) for s in range(0, n_rows, sub)]


def _tail_specs(b, t, d, tm, n_inputs_before, tokens, shared):
    n_total, offset = tokens
    n_t = t // tm
    assert offset % tm == 0
    row = lambda bi, i: (bi, i, 0)
    flat = lambda bi, i: (offset // tm + bi * n_t + i, 0)
    per_b = lambda bi, i: (bi, 0, 0)
    const = lambda bi, i: (0, 0)
    extra, aliases = (), {}
    if shared is not None:
        extra = tuple(shared)
        aliases = {n_inputs_before + 7: 1, n_inputs_before + 8: 2}
    in_specs = [
        pl.BlockSpec((1, tm, d), row),
        pl.BlockSpec((1, 1, d), per_b),
        pl.BlockSpec((1, d), const),
        pl.BlockSpec((1, 1, d), per_b),
        pl.BlockSpec((1, 1, d), per_b),
        pl.BlockSpec((d, ROUTER_PAD), const),
        pl.BlockSpec((1, ROUTER_PAD), const),
    ] + [pl.BlockSpec(memory_space=pl.ANY)] * len(extra)
    out_specs = [
        pl.BlockSpec((1, tm, d), row),
        pl.BlockSpec((tm, d // 2), flat),
        pl.BlockSpec((tm, ROUTER_PAD), flat),
        pl.BlockSpec((1, tm, ROUTER_PAD), row),
        pl.BlockSpec((1, SUBLANES, ROUTER_PAD), lambda bi, i: (bi * n_t + i, 0, 0)),
    ]
    out_shape = [
        jax.ShapeDtypeStruct((b, t, d), F32),
        jax.ShapeDtypeStruct((n_total, d // 2), jnp.int32),
        jax.ShapeDtypeStruct((n_total, ROUTER_PAD), jnp.int32),
        jax.ShapeDtypeStruct((b, t, ROUTER_PAD), F32),
        jax.ShapeDtypeStruct((b * n_t, SUBLANES, ROUTER_PAD), F32),
    ]
    return in_specs, out_specs, out_shape, extra, aliases


def _halo_fix(rolled, at_row, halo_row, present):
    n = rolled.shape[0]
    first = at_row < SUBLANES
    assert first or at_row >= n - SUBLANES
    slab = rolled[:SUBLANES] if first else rolled[n - SUBLANES:]
    sub = lax.broadcasted_iota(jnp.int32, slab.shape, 0)
    fill = jnp.where(present, halo_row, jnp.zeros_like(halo_row))
    slab = jnp.where(sub == at_row % SUBLANES, fill, slab)
    return jnp.concatenate([slab, rolled[SUBLANES:]] if first else [rolled[:n - SUBLANES], slab], axis=0)


def _even_out_kernel(oa_ref, bg_ref, cg_ref, xin_ref, cgp_ref, xinp_ref, cgn_ref, xinn_ref, cw_ref, cb_ref,
                     wa_ref, wb_ref, *tail_refs, tm):
    i = pl.program_id(1)
    has_prev = i > 0
    has_next = i < pl.num_programs(1) - 1
    u = cg_ref[0].astype(F32) * xin_ref[0].astype(F32)
    u_prev = (cgp_ref[0].astype(F32) * xinp_ref[0].astype(F32))[HALO - 1:HALO]
    u_next = (cgn_ref[0].astype(F32) * xinn_ref[0].astype(F32))[0:1]
    u_m1 = _halo_fix(pltpu.roll(u, 1, 0), 0, u_prev, has_prev)
    u_p1 = _halo_fix(pltpu.roll(u, tm - 1, 0), tm - 1, u_next, has_next)
    cw = cw_ref[...]
    conv = u_m1 * cw[0:1] + u * cw[1:2] + u_p1 * cw[2:3] + cb_ref[...]
    o_b = (bg_ref[0].astype(F32) * conv).astype(BF16)
    counts = 0.0
    for rows in _sub_tiles(tm):
        y = _dot(oa_ref[0, rows], wa_ref[...]) + _dot(o_b[rows], wb_ref[...])
        counts = counts + _layer_tail(y, rows, *tail_refs)
    tail_refs[-1][0] = jnp.broadcast_to(counts, tail_refs[-1].shape[1:])


def _even_out(o_a, proj, conv_w, conv_b, w_out, x, g1, n2, sh2, sc2, w_r, b_r, tokens, shared=None):
    b, t, d = x.shape
    w = NA_WIDTH
    tm = min(t, OUT_TILE)
    hb = tm // HALO
    n_hblocks = t // HALO
    row = lambda bi, i: (bi, i, 0)
    const = lambda bi, i: (0, 0)
    prev = lambda col: (lambda bi, i: (bi, jnp.maximum(i * hb - 1, 0), col))
    nxt = lambda col: (lambda bi, i: (bi, jnp.minimum((i + 1) * hb, n_hblocks - 1), col))
    tail_in, out_specs, out_shape, extra, aliases = _tail_specs(b, t, d, tm, 12, tokens, shared)
    in_specs = [
        pl.BlockSpec((1, tm, w), row),
        pl.BlockSpec((1, tm, w), lambda bi, i: (bi, i, 3)),
        pl.BlockSpec((1, tm, w), lambda bi, i: (bi, i, 4)),
        pl.BlockSpec((1, tm, w), lambda bi, i: (bi, i, 5)),
        pl.BlockSpec((1, HALO, w), prev(4)),
        pl.BlockSpec((1, HALO, w), prev(5)),
        pl.BlockSpec((1, HALO, w), nxt(4)),
        pl.BlockSpec((1, HALO, w), nxt(5)),
        pl.BlockSpec((SC_CONV, w), const),
        pl.BlockSpec((1, w), const),
        pl.BlockSpec((w, d), const),
        pl.BlockSpec((w, d), const),
    ] + tail_in
    return pl.pallas_call(
        functools.partial(_even_out_kernel, tm=tm),
        out_shape=out_shape,
        grid=(b, t // tm),
        in_specs=in_specs,
        out_specs=out_specs,
        input_output_aliases=aliases,
        compiler_params=_params(("parallel", "parallel")),
        name="even_out",
    )(o_a, proj, proj, proj, proj, proj, proj, proj, conv_w, conv_b.reshape(1, w),
      w_out[:w], w_out[w:], x, g1, n2, sh2, sc2, w_r, b_r, *extra)


def _log_sigmoid(x):
    return jnp.minimum(x, 0.0) - jnp.log1p(jnp.exp(-jnp.abs(x)))


def _sigmoid(x):
    return 0.5 + 0.5 * jnp.tanh(0.5 * x)


def _lru_sub_tile(ucb_time, state, perm_ref, w_ref, ba_ref, bx_ref, lam_ref, want_hidden, reverse):
    sub, width = ucb_time.shape
    blk = width // LRU_BLOCKS
    steps = sub // SUBLANES
    uc = _dot(perm_ref[0], ucb_time)
    ucb = uc.astype(BF16)
    za, zx = [], []
    for h in range(LRU_BLOCKS):
        z = _dot(ucb[:, h * blk:(h + 1) * blk], w_ref[h])
        za.append(z[:, :blk])
        zx.append(z[:, blk:])
    r = _sigmoid(jnp.concatenate(za, axis=1) + ba_ref[...])
    gate_i = _sigmoid(jnp.concatenate(zx, axis=1) + bx_ref[...])
    log_a = (LRU_C * _log_sigmoid(lam_ref[...])) * r
    a = jnp.exp(log_a)
    th = jnp.tanh(log_a)
    num = -2.0 * th
    mult = jnp.where(num > 0.0, num * lax.rsqrt(num * (1.0 - th)), 0.0)
    bcoef = mult * gate_i * uc

    grp = lambda v, j: v[j * SUBLANES:(j + 1) * SUBLANES]
    prods, local = [grp(a, 0)], [grp(bcoef, 0)]
    for j in range(1, steps):
        aj = grp(a, j)
        local.append(aj * local[-1] + grp(bcoef, j))
        prods.append(aj * prods[-1])

    carry_in = [None] * SUBLANES
    for s in (range(SUBLANES - 1, -1, -1) if reverse else range(SUBLANES)):
        carry_in[s] = state
        state = prods[-1][s:s + 1] * state + local[-1][s:s + 1]
    if not want_hidden:
        return state, None
    start = jnp.concatenate(carry_in, axis=0)
    hidden = jnp.concatenate([prods[j] * start + local[j] for j in range(steps)], axis=0)
    return state, _dot(perm_ref[1], hidden.astype(BF16)).astype(BF16)


def _lru_tile(u_ref, perm_ref, w_ref, ba_ref, bx_ref, lam_ref, carry_ref, o_ref, *, reverse):
    sub = perm_ref.shape[1]
    n_sub = u_ref.shape[1] // sub
    state = carry_ref[0:1, :]
    for k in (range(n_sub - 1, -1, -1) if reverse else range(n_sub)):
        rows = slice(k * sub, (k + 1) * sub)
        state, hidden = _lru_sub_tile(u_ref[0, rows], state, perm_ref, w_ref, ba_ref, bx_ref, lam_ref,
                                      o_ref is not None, reverse)
        if o_ref is not None:
            o_ref[0, rows] = hidden
    carry_ref[...] = jnp.broadcast_to(state, carry_ref.shape)


def _lru_kernel(uc_ref, ul_ref, perm_ref, w_ref, ba_ref, bx_ref, lam_ref, o_ref, carry_ref, *, n_ctx_tiles, reverse):
    j = pl.program_id(1)
    shared = (perm_ref, w_ref, ba_ref, bx_ref, lam_ref, carry_ref)

    @pl.when(j == 0)
    def _():
        carry_ref[...] = jnp.zeros_like(carry_ref)

    @pl.when(j < n_ctx_tiles)
    def _():
        _lru_tile(uc_ref, *shared, None, reverse=reverse)

    @pl.when(j >= n_ctx_tiles)
    def _():
        _lru_tile(ul_ref, *shared, o_ref, reverse=reverse)


def _lru_scan(proj, u_ctx, w_cat, ba, bx, lam, reverse):
    b, t, _ = proj.shape
    l, width = u_ctx.shape[1], u_ctx.shape[2]
    sub = min(256, l, t)
    tc = sub * min(LRU_SUB_TILES, l // sub)
    tl = sub * min(LRU_SUB_TILES, t // sub)
    n_c, n_l = l // tc, t // tl

    def pos_of(step, n):
        step = jnp.clip(step, 0, n - 1)
        return (n - 1 - step) if reverse else step

    def tile_map(off, n, col):
        return lambda bi, j: (bi, pos_of(j - off, n), col)

    const2 = lambda bi, j: (0, 0)
    in_specs = [
        pl.BlockSpec((1, tc, width), tile_map(0, n_c, 0)),
        pl.BlockSpec((1, tl, width), tile_map(n_c, n_l, 1)),
        pl.BlockSpec((2, sub, sub), lambda bi, j: (0, 0, 0)),
        pl.BlockSpec(w_cat.shape, lambda bi, j: (0, 0, 0)),
        pl.BlockSpec((1, width), const2),
        pl.BlockSpec((1, width), const2),
        pl.BlockSpec((1, width), const2),
    ]
    steps = sub // SUBLANES
    step, block = np.divmod(np.arange(sub), SUBLANES)
    time_of_row = block * steps + (steps - 1 - step if reverse else step)
    perm = (time_of_row[:, None] == np.arange(sub)[None]).astype(np.float32)
    perms = jnp.asarray(np.stack([perm, perm.T]), BF16)
    kern = functools.partial(_lru_kernel, n_ctx_tiles=n_c, reverse=reverse)
    return pl.pallas_call(
        kern,
        out_shape=jax.ShapeDtypeStruct((b, t, width), BF16),
        grid=(b, n_c + n_l),
        in_specs=in_specs,
        out_specs=pl.BlockSpec((1, tl, width), tile_map(n_c, n_l, 0)),
        scratch_shapes=[pltpu.VMEM((SUBLANES, width), F32)],
        compiler_params=_params(("parallel", "arbitrary")),
        name="lru_scan_bwd" if reverse else "lru_scan_fwd",
    )(u_ctx, proj, perms, w_cat, ba.reshape(1, width), bx.reshape(1, width), lam.reshape(1, width))


def _odd_out_kernel(hf_ref, hb_ref, gate_ref, w_ref, *tail_refs):
    counts = 0.0
    for rows in _sub_tiles(hf_ref.shape[1]):
        hsum = hf_ref[0, rows].astype(F32) + hb_ref[0, rows].astype(F32)
        z = hsum * jax.nn.gelu(gate_ref[0, rows].astype(F32), approximate=True)
        y = _dot(z.astype(BF16), w_ref[...])
        counts = counts + _layer_tail(y, rows, *tail_refs)
    tail_refs[-1][0] = jnp.broadcast_to(counts, tail_refs[-1].shape[1:])


def _odd_out(h_f, h_b, proj, w_out, x, g1, n2, sh2, sc2, w_r, b_r):
    b, t, d = x.shape
    width = h_f.shape[-1]
    tm = min(t, OUT_TILE)
    row = lambda bi, i: (bi, i, 0)
    tail_in, out_specs, out_shape, _, _ = _tail_specs(b, t, d, tm, 4, (b * t, 0), None)
    in_specs = [
        pl.BlockSpec((1, tm, width), row),
        pl.BlockSpec((1, tm, width), row),
        pl.BlockSpec((1, tm, width), row),
        pl.BlockSpec((width, d), lambda bi, i: (0, 0)),
    ] + tail_in
    return pl.pallas_call(
        _odd_out_kernel,
        out_shape=out_shape,
        grid=(b, t // tm),
        in_specs=in_specs,
        out_specs=out_specs,
        compiler_params=_params(("parallel", "parallel")),
        name="odd_out",
    )(h_f, h_b, proj, w_out, x, g1, n2, sh2, sc2, w_r, b_r)


def _expert_kernel(be_ref, next_ref, rows_ref, nb_ref, x_ref, wg_hbm, bg_ref, wu_hbm, bu_ref, wd_hbm, bd_ref,
                   o_ref, w_f32, wg_bf, wu_bf, wd_bf, h_bf, sem, *, layer, chunk):
    i = pl.program_id(0)
    expert = be_ref[i]
    used = i < nb_ref[0]

    def weight_copies(e):
        return [pltpu.make_async_copy(src.at[layer, e], w_f32.at[j], sem.at[j])
                for j, src in enumerate((wg_hbm, wu_hbm, wd_hbm))]

    @pl.when(i == 0)
    def _():
        for cp in weight_copies(expert):
            cp.start()

    @pl.when(used & ((i == 0) | (expert != be_ref[jnp.maximum(i - 1, 0)])))
    def _():
        for cp in weight_copies(expert):
            cp.wait()
        wg_bf[...] = w_f32[0].astype(BF16)
        wu_bf[...] = w_f32[1].astype(BF16)
        wd_bf[...] = w_f32[2].astype(BF16)

        @pl.when(next_ref[i] >= 0)
        def _():
            for cp in weight_copies(next_ref[i]):
                cp.start()

    sub = h_bf.shape[0]
    n_sub = x_ref.shape[0] // sub
    n_rows = jnp.where(used, rows_ref[i], 0)
    n_live = (n_rows + sub - 1) // sub

    def live_sub_block(s, carry):
        rows = pl.ds(pl.multiple_of(s * sub, sub), sub)
        xp = x_ref[rows]
        row = lax.broadcasted_iota(jnp.int32, xp.shape, 0) + s * sub
        xp = jnp.where(row < n_rows, xp, 0)
        x = jnp.concatenate(_unpack_rows(xp), axis=1).astype(BF16)
        for c in range(wg_bf.shape[1] // chunk):
            cs = slice(c * chunk, (c + 1) * chunk)
            g = jnp.minimum(_dot(x, wg_bf[:, cs]) + bg_ref[0, 0, :, cs], SWIGLU_LIMIT)
            u = jnp.clip(_dot(x, wu_bf[:, cs]) + bu_ref[0, 0, :, cs], -SWIGLU_LIMIT, SWIGLU_LIMIT)
            h_bf[:, cs] = (g * jax.nn.sigmoid(SWIGLU_ALPHA * g) * (u + 1.0)).astype(BF16)
        o_ref[rows] = _pack_rows(_dot(h_bf[...], wd_bf[...]) + bd_ref[0, 0])
        return carry

    def empty_sub_block(s, carry):
        o_ref[pl.ds(pl.multiple_of(s * sub, sub), sub)] = jnp.zeros((sub, o_ref.shape[1]), o_ref.dtype)
        return carry

    lax.fori_loop(0, n_live, live_sub_block, 0)
    lax.fori_loop(n_live, n_sub, empty_sub_block, 0)


def _experts(xb, block_e, next_e, block_rows, n_used, layer, wg, bg, wu, bu, wd, bd):
    n_slots = xb.shape[0]
    depth, n_e, d, d_exp = wg.shape
    assert d == d_exp
    tm = EXPERT_TILE
    n_blocks = n_slots // tm
    xmap = lambda i, be, ne, br, nb: (jnp.minimum(i, nb[0] - 1), 0)
    bmap = lambda i, be, ne, br, nb: (layer, be[i], 0, 0)
    hbm = pl.BlockSpec(memory_space=pl.ANY)
    grid_spec = pltpu.PrefetchScalarGridSpec(
        num_scalar_prefetch=4,
        grid=(n_blocks,),
        in_specs=[
            pl.BlockSpec((tm, d // 2), xmap),
            hbm,
            pl.BlockSpec((1, 1, 1, d_exp), bmap),
            hbm,
            pl.BlockSpec((1, 1, 1, d_exp), bmap),
            hbm,
            pl.BlockSpec((1, 1, 1, d), bmap),
        ],
        out_specs=pl.BlockSpec((tm, d // 2), lambda i, be, ne, br, nb: (i, 0)),
        scratch_shapes=[pltpu.VMEM((3, d, d_exp), F32), pltpu.VMEM((d, d_exp), BF16), pltpu.VMEM((d, d_exp), BF16),
                        pltpu.VMEM((d_exp, d), BF16), pltpu.VMEM((EXPERT_SUB, d_exp), BF16),
                        pltpu.SemaphoreType.DMA((3,))],
    )
    return pl.pallas_call(
        functools.partial(_expert_kernel, layer=layer, chunk=256),
        out_shape=jax.ShapeDtypeStruct((n_slots, d // 2), jnp.int32),
        grid_spec=grid_spec,
        compiler_params=_params(("arbitrary",)),
        name="experts",
    )(block_e, next_e, block_rows, n_used, xb, wg, bg.reshape(depth, n_e, 1, d_exp), wu,
      bu.reshape(depth, n_e, 1, d_exp), wd, bd.reshape(depth, n_e, 1, d))


def _combine_kernel(x_ref, g2_ref, gates_ref, y_ref, o_ref):
    o_ref[0] = _moe_residual(x_ref[0], g2_ref[0], gates_ref[0], y_ref[...])


def _combine(x, g2, gates, y_sel, tok_offset):
    b, t, d = x.shape
    tm = int(np.gcd(min(t, ROW_TILE), tok_offset)) if tok_offset else min(t, ROW_TILE)
    n_t = t // tm
    off = tok_offset // tm
    row = lambda bi, i: (bi, i, 0)
    return pl.pallas_call(
        _combine_kernel,
        out_shape=jax.ShapeDtypeStruct((b, t, d), F32),
        grid=(b, n_t),
        in_specs=[
            pl.BlockSpec((1, tm, d), row),
            pl.BlockSpec((1, 1, d), lambda bi, i: (bi, 0, 0)),
            pl.BlockSpec((1, tm, ROUTER_PAD), row),
            pl.BlockSpec((TOP_K, tm, d // 2), lambda bi, i: (0, off + bi * n_t + i, 0)),
        ],
        out_specs=pl.BlockSpec((1, tm, d), row),
        input_output_aliases={0: 0},
        compiler_params=_params(("parallel", "parallel")),
        name="moe_combine",
    )(x, g2, gates, y_sel)


def _route_kernel(ids_ref, upper_ref, starts_ref, dest_ref, run_ref):
    i = pl.program_id(0)
    tm = ids_ref.shape[0]
    ids_t = ids_ref[...].T
    expert = lax.broadcasted_iota(jnp.int32, (N_EXPERTS, tm), 0)
    chosen = [ids_t[k:k + 1, :] == expert for k in range(TOP_K)]
    picks = functools.reduce(jnp.add, [c.astype(F32) for c in chosen])

    @pl.when(i == 0)
    def _():
        run_ref[...] = starts_ref[...]

    before = _dot(picks.astype(BF16), upper_ref[...])
    slot = before + run_ref[:, 0:1]
    rows = [jnp.sum(jnp.where(c, slot, 0.0), axis=0, keepdims=True) for c in chosen]
    rows += [jnp.zeros_like(rows[0])] * (dest_ref.shape[0] - TOP_K)
    dest_ref[...] = jnp.concatenate(rows, axis=0).astype(jnp.int32)
    run_ref[...] += jnp.sum(picks, axis=1, keepdims=True)


def _route(ids, region_starts):
    n = ids.shape[0]
    tm = int(np.gcd(n, ROUTE_TILE))
    upper = jnp.asarray(np.triu(np.ones((tm, tm), np.float32), 1), BF16)
    starts = jnp.broadcast_to(region_starts.astype(F32)[:, None], (N_EXPERTS, LANES))
    return pl.pallas_call(
        _route_kernel,
        out_shape=jax.ShapeDtypeStruct((SUBLANES, n), jnp.int32),
        grid=(n // tm,),
        in_specs=[pl.BlockSpec((tm, ROUTER_PAD), lambda i: (i, 0)),
                  pl.BlockSpec((tm, tm), lambda i: (0, 0)),
                  pl.BlockSpec((N_EXPERTS, LANES), lambda i: (0, 0))],
        out_specs=pl.BlockSpec((SUBLANES, tm), lambda i: (0, i)),
        scratch_shapes=[pltpu.VMEM((N_EXPERTS, LANES), F32)],
        compiler_params=_params(("arbitrary",)),
        name="route",
    )(ids, upper, starts)


def _row_gather(table, idx):
    info = plsc.get_sparse_core_info()
    n_cores, n_workers = info.num_cores, info.num_cores * info.num_subcores
    n_rows, width = idx.shape[0], table.shape[1]
    chunk_rows = SC_GATHER_ROWS
    per_worker = n_rows // n_workers
    n_chunks = per_worker // chunk_rows
    assert per_worker * n_workers == n_rows and n_chunks * chunk_rows == per_worker and n_chunks % 2 == 0
    mesh = plsc.VectorSubcoreMesh(core_axis_name="c", subcore_axis_name="s")

    @functools.partial(
        pl.kernel, mesh=mesh,
        out_type=jax.ShapeDtypeStruct((n_rows, width), table.dtype),
        scratch_types=[pltpu.VMEM((n_chunks, chunk_rows), jnp.int32), pltpu.VMEM((2, chunk_rows, width), table.dtype),
                       pltpu.SemaphoreType.DMA((2,)), pltpu.SemaphoreType.DMA((2,))],
    )
    def gather_kernel(table_hbm, idx_hbm, out_hbm, idx_v, rows_v, gather_sem, write_sem):
        worker = lax.axis_index("s") * n_cores + lax.axis_index("c")
        pltpu.sync_copy(idx_hbm.at[worker], idx_v)

        def gather(c, slot):
            return pltpu.make_async_copy(table_hbm.at[idx_v.at[c]], rows_v.at[slot], gather_sem.at[slot])

        def write(c, slot):
            first_row = worker * per_worker + c * chunk_rows
            return pltpu.make_async_copy(rows_v.at[slot], out_hbm.at[pl.ds(first_row, chunk_rows)], write_sem.at[slot])

        gather(0, 0).start()

        @pl.loop(0, n_chunks, step=2)
        def _(c0):
            for slot in (0, 1):
                c = c0 + slot
                gather(c, slot).wait()

                @pl.when(c >= 1)
                def _():
                    write(c - 1, 1 - slot).wait()

                @pl.when(c + 1 < n_chunks)
                def _():
                    gather(c + 1, 1 - slot).start()

                write(c, slot).start()

        write(n_chunks - 1, 1).wait()

    return gather_kernel(table, idx.reshape(n_workers, n_chunks, chunk_rows))


def _row_scatter(rows, dest, n_out):
    info = plsc.get_sparse_core_info()
    n_cores, n_workers = info.num_cores, info.num_cores * info.num_subcores
    n_choices, n_rows = dest.shape
    width = rows.shape[1]
    chunk_rows = SC_SCATTER_ROWS
    per_worker = n_rows // (n_workers * chunk_rows)
    assert per_worker * n_workers * chunk_rows == n_rows == rows.shape[0]
    idx = dest.reshape(n_choices, n_workers, per_worker, chunk_rows)
    mesh = plsc.VectorSubcoreMesh(core_axis_name="c", subcore_axis_name="s")

    @functools.partial(
        pl.kernel, mesh=mesh,
        out_type=jax.ShapeDtypeStruct((n_out, width), rows.dtype),
        scratch_types=[pltpu.VMEM((n_choices, per_worker, chunk_rows), jnp.int32),
                       pltpu.VMEM((chunk_rows, width), rows.dtype), pltpu.SemaphoreType.DMA],
    )
    def scatter_kernel(rows_hbm, idx_hbm, out_hbm, idx_v, rows_v, sem):
        worker = lax.axis_index("s") * n_cores + lax.axis_index("c")
        for k in range(n_choices):
            pltpu.sync_copy(idx_hbm.at[k, worker], idx_v.at[k])

        @pl.loop(0, per_worker)
        def _(c):
            first_row = (worker * per_worker + c) * chunk_rows
            pltpu.sync_copy(rows_hbm.at[pl.ds(first_row, chunk_rows)], rows_v)
            copies = [pltpu.async_copy(rows_v, out_hbm.at[idx_v.at[k, c]], sem) for k in range(n_choices)]
            for cp in copies:
                cp.wait()

    return scatter_kernel(rows, idx)


def _expert_counts(step_picks):
    return jnp.sum(step_picks[:, 0, :N_EXPERTS], axis=0).astype(jnp.int32)


def _moe(h2, ids, counts, layer, w_exp):
    n = h2.shape[0]
    nk = n * TOP_K
    tm = EXPERT_TILE
    padded = (counts + tm - 1) // tm * tm
    pends = jnp.cumsum(padded)
    dest = _route(ids, pends - padded)[:TOP_K]
    n_blocks = -(-nk // tm) + N_EXPERTS
    n_slots = n_blocks * tm
    block_start = jnp.arange(n_blocks, dtype=jnp.int32) * tm
    block_e = jnp.minimum(jnp.sum((pends[None] <= block_start[:, None]).astype(jnp.int32), axis=1), N_EXPERTS - 1)
    is_e = (block_e[:, None] == jnp.arange(N_EXPERTS, dtype=jnp.int32)[None]).astype(jnp.int32)
    block_rows = jnp.clip(jnp.sum(is_e * (pends - padded + counts)[None], axis=1) - block_start, 0, tm)
    n_used = (pends[-1] // tm).astype(jnp.int32).reshape(1)
    e_ids = jnp.arange(N_EXPERTS, dtype=jnp.int32)
    later = (e_ids[None] > e_ids[:, None]) & (counts[None] > 0)
    next_nonempty = jnp.min(jnp.where(later, e_ids[None], N_EXPERTS), axis=1)
    next_nonempty = jnp.where(next_nonempty == N_EXPERTS, -1, next_nonempty)
    next_e = jnp.sum(is_e * next_nonempty[None], axis=1)
    xb = _row_scatter(h2, dest, n_slots)
    return _experts(xb, block_e, next_e, block_rows, n_used, layer, *w_exp), dest


def _gather_choices(yb, dest, lo, hi):
    return _row_gather(yb, dest[:, lo:hi].reshape(-1)).reshape(TOP_K, hi - lo, -1)


def _mod_parts(mod_l, b):
    d = mod_l.shape[-1] // 6
    lat = [mod_l[:b, k * d:(k + 1) * d].reshape(b, 1, d) for k in range(6)]
    ctx = [jnp.broadcast_to(mod_l[b, k * d:(k + 1) * d].reshape(1, 1, d), (b, 1, d)) for k in range(6)]
    return lat, ctx


def kernel(x, c, ctx, c_ctx, ada_w, ada_b, norm1_g, norm2_g, ev_w_in, ev_w_out, ev_q_gain, ev_k_gain, ev_rpb, ev_conv_w, ev_conv_b, od_w_in, od_w_out, od_conv_w, od_conv_b, od_fwd_wa, od_fwd_ba, od_fwd_wx, od_fwd_bx, od_fwd_lam, od_bwd_wa, od_bwd_ba, od_bwd_wx, od_bwd_bx, od_bwd_lam, router_w, router_b, exp_w_gate, exp_b_gate, exp_w_up, exp_b_up, exp_w_down, exp_b_down):
    b, t, d = x.shape
    l = ctx.shape[1]
    assert ada_w.shape[0] == DEPTH == 2 and t % GRID_W == 0 and t // GRID_W >= WIN_H

    n_rows_c = -(-(b + 1) // SUBLANES) * SUBLANES
    cvec = jnp.zeros((n_rows_c, d), F32).at[:b].set(c).at[b].set(c_ctx)
    mod = _ada_mod(cvec, ada_w, ada_b)

    def router(layer):
        w_r = jnp.zeros((d, ROUTER_PAD), F32).at[:, :N_EXPERTS].set(router_w[layer]).astype(BF16)
        b_r = jnp.zeros((1, ROUTER_PAD), F32).at[0, :N_EXPERTS].set(router_b[layer])
        return w_r, b_r

    w_exp = (exp_w_gate, exp_b_gate, exp_w_up, exp_b_up, exp_w_down, exp_b_down)

    (sh1, sc1, g1, sh2, sc2, g2), (csh1, csc1, cg1, csh2, csc2, cg2) = _mod_parts(mod[0], b)
    n1 = norm1_g[0].reshape(1, d)
    n2 = norm2_g[0].reshape(1, d)
    w_in = ev_w_in[0].astype(BF16)
    w_out = ev_w_out[0].astype(BF16)
    q_scale = NA_HEAD_DIM ** -0.5 * LOG2_E
    head_gain = jnp.stack([jnp.tile(ev_q_gain[0] * q_scale, NA_HEADS), jnp.tile(ev_k_gain[0], NA_HEADS)])
    head_gain = head_gain.reshape(2, 1, NA_WIDTH).astype(F32)
    proj = _inproj(x, n1, sh1, sc1, w_in, head_gain)
    proj_c = _inproj(ctx, n1, csh1, csc1, w_in, head_gain)
    kh, r0, row_type, patterns = _na_tables(t // GRID_W)
    bias_tab = _na_bias_table(ev_rpb[0], patterns)
    o_a = _neighbourhood_attention(proj, proj_c, bias_tab, r0, row_type, kh)
    oc_a = _context_attention(proj_c)
    w_r, b_r = router(0)
    n_tok = b * (l + t)
    fresh = (jnp.zeros((n_tok, d // 2), jnp.int32), jnp.zeros((n_tok, ROUTER_PAD), jnp.int32))
    c1, tokens, ids_all, gates_c, picks_c = _even_out(oc_a, proj_c, ev_conv_w[0], ev_conv_b[0], w_out, ctx, cg1, n2,
                                                      csh2, csc2, w_r, b_r, (n_tok, 0), fresh)
    x1, tokens, ids_all, gates, picks = _even_out(o_a, proj, ev_conv_w[0], ev_conv_b[0], w_out, x, g1, n2, sh2, sc2,
                                                  w_r, b_r, (n_tok, b * l), (tokens, ids_all))
    yb, dest = _moe(tokens, ids_all, _expert_counts(picks_c) + _expert_counts(picks), 0, w_exp)
    y_sel = _gather_choices(yb, dest, 0, b * (l + t))
    g2_prev, cg2_prev = g2, cg2

    (sh1, sc1, g1, sh2, sc2, g2), (csh1, csc1, _, _, _, _) = _mod_parts(mod[1], b)
    n1 = norm1_g[1].reshape(1, d)
    n2 = norm2_g[1].reshape(1, d)
    w_in = od_w_in[0].astype(BF16)
    width = w_in.shape[1] // 2
    x, proj = _combine_inproj_conv(x1, g2_prev, gates, y_sel, b * l, n1, sh1, sc1, w_in, od_conv_w[0], od_conv_b[0],
                                   width)
    _, u_ctx = _combine_inproj_conv(c1, cg2_prev, gates_c, y_sel, 0, n1, csh1, csc1, w_in[:, width:], od_conv_w[0],
                                    od_conv_b[0], 0)
    h_dir = []
    for reverse, (wa, ba, wx, bx, lam) in ((False, (od_fwd_wa, od_fwd_ba, od_fwd_wx, od_fwd_bx, od_fwd_lam)),
                                           (True, (od_bwd_wa, od_bwd_ba, od_bwd_wx, od_bwd_bx, od_bwd_lam))):
        w_cat = jnp.concatenate([wa[0], wx[0]], axis=-1).astype(BF16)
        h_dir.append(_lru_scan(proj, u_ctx, w_cat, ba[0], bx[0], lam[0], reverse))
    w_r, b_r = router(1)
    x1, h2, ids, gates, picks = _odd_out(h_dir[0], h_dir[1], proj, od_w_out[0].astype(BF16), x, g1, n2, sh2, sc2,
                                         w_r, b_r)
    yb, dest = _moe(h2, ids, _expert_counts(picks), 1, w_exp)
    return _combine(x1, g2, gates, _gather_choices(yb, dest, 0, b * t), 0)
```

```python
import functools

import numpy as np
import jax
import jax.numpy as jnp
from jax import lax
from jax.experimental import pallas as pl
from jax.experimental.pallas import tpu as pltpu
from jax.experimental.pallas import tpu_sc as plsc

DEPTH = 2
GRID_W = 64
EPS = 1e-6
NEG_INF = -1e30
LOG2_E = 1.4426950408889634
NA_HEADS = 8
NA_HEAD_DIM = 64
NA_WIDTH = NA_HEADS * NA_HEAD_DIM
HEAD_PAIRS = NA_HEADS // 2
WIN_H = 8
WIN_W = 16
SC_CONV = 3
LRU_BLOCKS = 4
LRU_CONV = 4
LRU_C = 8.0
N_EXPERTS = 32
TOP_K = 4
SWIGLU_LIMIT = 7.0
SWIGLU_ALPHA = 1.702

LANES = 128
SUBLANES = 8
MXU_DEPTH = 256
HALO = 16
ROUTER_PAD = LANES
ADA_COL_TILE = 1536
ROW_TILE = 1024
NA_ROWS_PER_STEP = 8
LRU_SUB_TILES = 8
OUT_TILE = 1024
OUT_SUB_TILE = 512
ROUTE_TILE = 2048
EXPERT_TILE = 2048
EXPERT_SUB = 512
SC_GATHER_ROWS = 64
SC_SCATTER_ROWS = 128
VMEM_LIMIT = 56 * 1024 * 1024

F32 = jnp.float32
BF16 = jnp.bfloat16


def _params(sem, vmem=VMEM_LIMIT):
    return pltpu.CompilerParams(dimension_semantics=sem, vmem_limit_bytes=vmem)


def _dot(a, b):
    return jnp.dot(a, b, preferred_element_type=F32)


def _dot_nt(a, b):
    return lax.dot_general(a, b, (((1,), (1,)), ((), ())), preferred_element_type=F32)


def _pack_rows(v):
    w = v.shape[-1] // 2
    lo = lax.bitcast_convert_type(v[:, :w].astype(BF16).astype(F32), jnp.int32)
    hi = lax.bitcast_convert_type(v[:, w:].astype(BF16).astype(F32), jnp.int32)
    return lax.shift_right_logical(lo, 16) | (hi & jnp.int32(-65536))


def _unpack_rows(p):
    lo = lax.bitcast_convert_type(lax.shift_left(p, 16), F32)
    hi = lax.bitcast_convert_type(p & jnp.int32(-65536), F32)
    return lo, hi


def _rms_mod(x, g, shift, scale):
    ms = jnp.mean(x * x, axis=-1, keepdims=True)
    return (x * lax.rsqrt(ms + EPS)) * (g * (1.0 + scale)) + shift


def _ada_kernel(c_ref, w_ref, b_ref, o_ref):
    c = c_ref[...]
    s = (c * jax.nn.sigmoid(c)).astype(BF16)
    o_ref[0] = _dot(s, w_ref[0].astype(BF16)) + b_ref[0]


def _ada_mod(cvec, ada_w, ada_b):
    depth, d, n = ada_w.shape
    r = cvec.shape[0]
    tn = ADA_COL_TILE
    return pl.pallas_call(
        _ada_kernel,
        out_shape=jax.ShapeDtypeStruct((depth, r, n), F32),
        grid=(depth, n // tn),
        in_specs=[
            pl.BlockSpec((r, d), lambda l, j: (0, 0)),
            pl.BlockSpec((1, d, tn), lambda l, j: (l, 0, j)),
            pl.BlockSpec((1, 1, tn), lambda l, j: (l, 0, j)),
        ],
        out_specs=pl.BlockSpec((1, r, tn), lambda l, j: (l, 0, j)),
        compiler_params=_params(("parallel", "parallel")),
        name="ada_mod",
    )(cvec, ada_w, ada_b.reshape(depth, 1, n))


def _inproj_kernel(x_ref, g_ref, sh_ref, sc_ref, w_ref, hg_ref, ones_ref, o_ref, *zero_refs, n_tiles, tn,
                   n_headnorm):
    for z_ref in zero_refs:
        z_ref[...] = jnp.zeros_like(z_ref)
    h = _rms_mod(x_ref[0], g_ref[...], sh_ref[0], sc_ref[0]).astype(BF16)
    for j in range(n_tiles):
        y = _dot(h, w_ref[:, j * tn:(j + 1) * tn])
        if j < n_headnorm:
            ysq = (y * y).astype(BF16)
            kw = ones_ref.shape[0]
            ms = jnp.concatenate([_dot(ysq[:, c:c + kw], ones_ref[...]) for c in range(0, tn, kw)], axis=1)
            y = y * lax.rsqrt(ms * (1.0 / NA_HEAD_DIM) + EPS) * hg_ref[j]
        o_ref[0, :, j * tn:(j + 1) * tn] = y.astype(o_ref.dtype)


def _inproj(x, g, shift, scale, w, head_gain, zero_shapes=()):
    b, t, d = x.shape
    n = w.shape[1]
    tn = NA_WIDTH
    tm = min(t, ROW_TILE)
    n_t = t // tm
    n_headnorm = head_gain.shape[0]
    hid = np.arange(MXU_DEPTH) // NA_HEAD_DIM
    ones_bd = jnp.asarray((hid[:, None] == hid[None, :]), BF16)
    kern = functools.partial(_inproj_kernel, n_tiles=n // tn, tn=tn, n_headnorm=n_headnorm)
    steps = b * n_t
    assert all(rows % (steps * SUBLANES) == 0 for rows, _ in zero_shapes)
    zero_specs = [pl.BlockSpec((rows // steps, cols), lambda bi, i: (bi * n_t + i, 0)) for rows, cols in zero_shapes]
    zero_out = [jax.ShapeDtypeStruct(shape, jnp.int32) for shape in zero_shapes]
    return pl.pallas_call(
        kern,
        out_shape=[jax.ShapeDtypeStruct((b, t, n), BF16)] + zero_out,
        grid=(b, n_t),
        in_specs=[
            pl.BlockSpec((1, tm, d), lambda bi, i: (bi, i, 0)),
            pl.BlockSpec((1, d), lambda bi, i: (0, 0)),
            pl.BlockSpec((1, 1, d), lambda bi, i: (bi, 0, 0)),
            pl.BlockSpec((1, 1, d), lambda bi, i: (bi, 0, 0)),
            pl.BlockSpec((d, n), lambda bi, i: (0, 0)),
            pl.BlockSpec(head_gain.shape, lambda bi, i: (0, 0, 0)),
            pl.BlockSpec((MXU_DEPTH, MXU_DEPTH), lambda bi, i: (0, 0)),
        ],
        out_specs=[pl.BlockSpec((1, tm, n), lambda bi, i: (bi, i, 0))] + zero_specs,
        compiler_params=_params(("parallel", "parallel")),
        name="inproj",
    )(x, g, shift, scale, w, head_gain, ones_bd)


def _moe_residual(x, g2, gates, y):
    acc_lo = acc_hi = 0.0
    for k in range(TOP_K):
        lo, hi = _unpack_rows(y[k])
        acc_lo = acc_lo + gates[:, k:k + 1] * lo
        acc_hi = acc_hi + gates[:, k:k + 1] * hi
    return x + g2 * jnp.concatenate([acc_lo, acc_hi], axis=1)


def _combine_inproj_conv_kernel(x_ref, xp_ref, xn_ref, gt_ref, gtp_ref, gtn_ref, y_ref, yp_ref, yn_ref, g2_ref,
                                g_ref, sh_ref, sc_ref, w_ref, cw_ref, cb_ref, xo_ref, o_ref, *,
                                n_tiles, n_plain, tn, tm, sub):
    i = pl.program_id(1)
    edge = SUBLANES
    g2 = g2_ref[0]
    x_tile = _moe_residual(x_ref[0], g2, gt_ref[0], y_ref[...])
    xo_ref[0] = x_tile
    x_ext = jnp.concatenate([_moe_residual(xp_ref[0], g2, gtp_ref[0], yp_ref[...]), x_tile,
                             _moe_residual(xn_ref[0], g2, gtn_ref[0], yn_ref[...])], axis=0)
    h = _rms_mod(x_ext, g_ref[...], sh_ref[0], sc_ref[0]).astype(BF16)
    first_step = i == 0
    last_step = i == pl.num_programs(1) - 1
    left = (LRU_CONV - 1) // 2
    n_sub = tm // sub
    for s in range(n_sub):
        h_s = h[s * sub:(s + 1) * sub + 2 * edge]
        rows = slice(s * sub, (s + 1) * sub)
        keep_prev = jnp.where(first_step, 0.0, 1.0) if s == 0 else 1.0
        keep_next = jnp.where(last_step, 0.0, 1.0) if s == n_sub - 1 else 1.0
        for j in range(n_tiles):
            y = _dot(h_s, w_ref[:, j * tn:(j + 1) * tn])
            if j < n_plain:
                o_ref[0, rows, j * tn:(j + 1) * tn] = y[edge:edge + sub].astype(o_ref.dtype)
                continue
            u = jnp.concatenate([y[:edge] * keep_prev, y[edge:edge + sub], y[edge + sub:] * keep_next], axis=0)
            cw = cw_ref[:, (j - n_plain) * tn:(j - n_plain + 1) * tn]
            uc = cb_ref[:, (j - n_plain) * tn:(j - n_plain + 1) * tn]
            n_ext = u.shape[0]
            for k in range(LRU_CONV):
                shifted = u if k == left else pltpu.roll(u, (left - k) % n_ext, 0)
                uc = uc + shifted[edge:edge + sub] * cw[k:k + 1]
            o_ref[0, rows, j * tn:(j + 1) * tn] = uc.astype(o_ref.dtype)


def _combine_inproj_conv(x1, g2, gates, y_sel, tok_offset, g, shift, scale, w, conv_w, conv_b, n_plain_cols):
    b, t, d = x1.shape
    n = w.shape[1]
    tn = NA_WIDTH
    tm = min(t, OUT_TILE)
    hb = tm // SUBLANES
    n_hblocks = t // SUBLANES
    y_hblocks = y_sel.shape[1] // SUBLANES
    assert tok_offset % tm == 0
    kern = functools.partial(_combine_inproj_conv_kernel, n_tiles=n // tn, n_plain=n_plain_cols // tn, tn=tn, tm=tm,
                             sub=min(tm, OUT_SUB_TILE))
    const = lambda bi, i: (0, 0)
    tile = lambda bi, i: (bi, i, 0)
    prev = lambda bi, i: (bi, jnp.maximum(i * hb - 1, 0), 0)
    nxt = lambda bi, i: (bi, jnp.minimum((i + 1) * hb, n_hblocks - 1), 0)
    y_row = lambda bi, i: (tok_offset + bi * t + i * tm) // SUBLANES
    per_b = lambda bi, i: (bi, 0, 0)
    return pl.pallas_call(
        kern,
        out_shape=[jax.ShapeDtypeStruct((b, t, d), F32), jax.ShapeDtypeStruct((b, t, n), BF16)],
        grid=(b, t // tm),
        in_specs=[
            pl.BlockSpec((1, tm, d), tile),
            pl.BlockSpec((1, SUBLANES, d), prev),
            pl.BlockSpec((1, SUBLANES, d), nxt),
            pl.BlockSpec((1, tm, ROUTER_PAD), tile),
            pl.BlockSpec((1, SUBLANES, ROUTER_PAD), prev),
            pl.BlockSpec((1, SUBLANES, ROUTER_PAD), nxt),
            pl.BlockSpec((TOP_K, tm, d // 2), lambda bi, i: (0, y_row(bi, i) // hb, 0)),
            pl.BlockSpec((TOP_K, SUBLANES, d // 2), lambda bi, i: (0, jnp.maximum(y_row(bi, i) - 1, 0), 0)),
            pl.BlockSpec((TOP_K, SUBLANES, d // 2),
                         lambda bi, i: (0, jnp.minimum(y_row(bi, i) + hb, y_hblocks - 1), 0)),
            pl.BlockSpec((1, 1, d), per_b),
            pl.BlockSpec((1, d), const),
            pl.BlockSpec((1, 1, d), per_b),
            pl.BlockSpec((1, 1, d), per_b),
            pl.BlockSpec((d, n), const),
            pl.BlockSpec(conv_w.shape, const),
            pl.BlockSpec((1, conv_w.shape[1]), const),
        ],
        out_specs=[pl.BlockSpec((1, tm, d), tile), pl.BlockSpec((1, tm, n), tile)],
        compiler_params=_params(("parallel", "parallel")),
        name="combine_inproj_conv",
    )(x1, x1, x1, gates, gates, gates, y_sel, y_sel, y_sel, g2, g, shift, scale, w, conv_w, conv_b.reshape(1, -1))


def _na_tables(rows):
    kh = min(WIN_H, rows)
    r = np.arange(rows)
    r0 = np.clip(r - kh // 2, 0, rows - kh)
    dr = r0[:, None] + np.arange(kh)[None] - r[:, None] + WIN_H - 1
    patterns, row_type = np.unique(dr, axis=0, return_inverse=True)
    return kh, r0.astype(np.int32), row_type.reshape(-1).astype(np.int32), patterns


def _na_bias_table(rpb, patterns):
    qc = np.arange(GRID_W)
    kc = np.arange(GRID_W)
    c0 = np.clip(qc - WIN_W // 2, 0, GRID_W - WIN_W)[:, None]
    valid = (kc[None] >= c0) & (kc[None] < c0 + WIN_W)
    dc = np.clip(kc[None] - qc[:, None] + WIN_W - 1, 0, 2 * WIN_W - 2)
    n_pat, kh = patterns.shape
    onehot_dc = jnp.asarray(dc[None] == np.arange(2 * WIN_W - 1)[:, None, None], F32)
    tab = jnp.einsum('hpic,cqk->hpiqk', rpb.astype(F32)[:, patterns], onehot_dc,
                     precision=lax.Precision.HIGHEST)
    tab = jnp.where(valid[None, None, None], tab * LOG2_E, NEG_INF)
    tab = tab.reshape(HEAD_PAIRS, 2, n_pat, kh, GRID_W, GRID_W)
    tab = tab.transpose(2, 0, 1, 4, 3, 5)
    return tab.reshape(n_pat, HEAD_PAIRS, 2 * GRID_W, kh * GRID_W)


def _pair_attention(q, keys, values, biases):
    m = q.shape[0]
    qq = _stack_heads(q)
    scores = []
    for k, bias in zip(keys, biases):
        s = _dot_nt(qq, k)
        scores.append(s if bias is None else s + bias)
    s = jnp.concatenate(scores, axis=1)
    e = jnp.exp2(s - jnp.max(s, axis=-1, keepdims=True))
    denom = jnp.sum(e, axis=-1, keepdims=True)
    e = e.astype(BF16)
    o, start = 0.0, 0
    for v in values:
        o = o + _dot(e[:, start:start + v.shape[0]], v)
        start += v.shape[0]
    o = o * (1.0 / denom)
    lane_o = lax.broadcasted_iota(jnp.int32, (m, LANES), 1)
    return jnp.where(lane_o < NA_HEAD_DIM, o[:m], o[m:])


def _stack_heads(q):
    lane = lax.broadcasted_iota(jnp.int32, q.shape, 1)
    zero = jnp.zeros_like(q)
    return jnp.concatenate([jnp.where(lane < NA_HEAD_DIM, q, zero), jnp.where(lane >= NA_HEAD_DIM, q, zero)], axis=0)


def _na_kernel(r0_ref, type_ref, q_ref, k_ref, v_ref, kc_ref, vc_ref, bias_ref, o_ref, s_ref, p_ref, *,
               kh, rows_per_step):
    n_lat = kh * GRID_W
    tiles = [(j, p) for j in range(rows_per_step) for p in range(HEAD_PAIRS)]
    window = []
    for j in range(rows_per_step):
        r = pl.program_id(1) * rows_per_step + j
        window.append((pl.multiple_of(r0_ref[r] * GRID_W, GRID_W), type_ref[r]))

    for idx, (j, p) in enumerate(tiles):
        start, rtype = window[j]
        cols = slice(p * LANES, (p + 1) * LANES)
        qq = _stack_heads(q_ref[0, j * GRID_W:(j + 1) * GRID_W, cols])
        s_ref[idx, :, :n_lat] = _dot_nt(qq, k_ref[0, pl.ds(start, n_lat), cols]) + bias_ref[rtype, p]
        s_ref[idx, :, n_lat:] = _dot_nt(qq, kc_ref[0, :, cols])

    denoms = []
    for idx in range(len(tiles)):
        s = s_ref[idx]
        e = jnp.exp2(s - jnp.max(s, axis=-1, keepdims=True))
        denoms.append(jnp.sum(e, axis=-1, keepdims=True))
        p_ref[idx] = e.astype(BF16)

    lane = lax.broadcasted_iota(jnp.int32, (GRID_W, LANES), 1)
    for idx, (j, p) in enumerate(tiles):
        start, _ = window[j]
        cols = slice(p * LANES, (p + 1) * LANES)
        o = _dot(p_ref[idx, :, :n_lat], v_ref[0, pl.ds(start, n_lat), cols]) + _dot(p_ref[idx, :, n_lat:],
                                                                                 vc_ref[0, :, cols])
        o = o * (1.0 / denoms[idx])
        o = jnp.where(lane < NA_HEAD_DIM, o[:GRID_W], o[GRID_W:])
        o_ref[0, j * GRID_W:(j + 1) * GRID_W, cols] = o.astype(o_ref.dtype)


def _neighbourhood_attention(proj, proj_c, bias_tab, r0, row_type, kh):
    b, t, _ = proj.shape
    l = proj_c.shape[1]
    rows = t // GRID_W
    w = NA_WIDTH
    rps = int(np.gcd(rows, NA_ROWS_PER_STEP))
    q_rows = rps * GRID_W
    grid_spec = pltpu.PrefetchScalarGridSpec(
        num_scalar_prefetch=2,
        grid=(b, rows // rps),
        in_specs=[
            pl.BlockSpec((1, q_rows, w), lambda bi, r, *_: (bi, r, 0)),
            pl.BlockSpec((1, t, w), lambda bi, r, *_: (bi, 0, 1)),
            pl.BlockSpec((1, t, w), lambda bi, r, *_: (bi, 0, 2)),
            pl.BlockSpec((1, l, w), lambda bi, r, *_: (bi, 0, 1)),
            pl.BlockSpec((1, l, w), lambda bi, r, *_: (bi, 0, 2)),
            pl.BlockSpec(bias_tab.shape, lambda bi, r, *_: (0, 0, 0, 0)),
        ],
        out_specs=pl.BlockSpec((1, q_rows, w), lambda bi, r, *_: (bi, r, 0)),
        scratch_shapes=[pltpu.VMEM((rps * HEAD_PAIRS, 2 * GRID_W, kh * GRID_W + l), F32),
                        pltpu.VMEM((rps * HEAD_PAIRS, 2 * GRID_W, kh * GRID_W + l), BF16)],
    )
    return pl.pallas_call(
        functools.partial(_na_kernel, kh=kh, rows_per_step=rps),
        out_shape=jax.ShapeDtypeStruct((b, t, w), BF16),
        grid_spec=grid_spec,
        compiler_params=_params(("parallel", "arbitrary")),
        name="na_attention",
    )(jnp.asarray(r0), jnp.asarray(row_type), proj, proj, proj, proj_c, proj_c, bias_tab)


def _ctx_attn_kernel(q_ref, k_ref, v_ref, o_ref):
    for p in range(HEAD_PAIRS):
        cols = slice(p * LANES, (p + 1) * LANES)
        o = _pair_attention(q_ref[0, :, cols], [k_ref[0, :, cols]], [v_ref[0, :, cols]], [None])
        o_ref[0, :, cols] = o.astype(o_ref.dtype)


def _context_attention(proj_c):
    b, l, _ = proj_c.shape
    w = NA_WIDTH
    return pl.pallas_call(
        _ctx_attn_kernel,
        out_shape=jax.ShapeDtypeStruct((b, l, w), BF16),
        grid=(b,),
        in_specs=[pl.BlockSpec((1, l, w), lambda bi, j=j: (bi, 0, j)) for j in range(3)],
        out_specs=pl.BlockSpec((1, l, w), lambda bi: (bi, 0, 0)),
        compiler_params=_params(("parallel",)),
        name="ctx_attention",
    )(proj_c, proj_c, proj_c)


def _top4(logits):
    lane = lax.broadcasted_iota(jnp.int32, logits.shape, 1)
    cur = jnp.where(lane < N_EXPERTS, logits, -jnp.inf)
    picked = jnp.zeros(logits.shape, F32)
    vals, idxs = [], []
    for _ in range(TOP_K):
        m = jnp.max(cur, axis=-1, keepdims=True)
        first = jnp.min(jnp.where(cur == m, lane, ROUTER_PAD).astype(F32), axis=-1, keepdims=True)
        idx = first.astype(jnp.int32)
        vals.append(m)
        idxs.append(idx)
        hit = lane == idx
        picked = jnp.where(hit, 1.0, picked)
        cur = jnp.where(hit, -jnp.inf, cur)
    exps = [jnp.exp(v - vals[0]) for v in vals]
    inv = 1.0 / functools.reduce(jnp.add, exps)
    ids = jnp.zeros(logits.shape, jnp.int32)
    gates = jnp.zeros(logits.shape, F32)
    for k in range(TOP_K):
        ids = jnp.where(lane == k, idxs[k], ids)
        gates = jnp.where(lane == k, exps[k] * inv, gates)
    return ids, gates, jnp.sum(picked, axis=0, keepdims=True)


def _layer_tail(y, rows, x_ref, g1_ref, n2_ref, sh2_ref, sc2_ref, wr_ref, br_ref, *rest):
    xo_ref, h2_ref, ids_ref, gates_ref = rest[-5:-1]
    x_new = x_ref[0, rows] + g1_ref[0] * y
    xo_ref[0, rows] = x_new
    h2 = _rms_mod(x_new, n2_ref[...], sh2_ref[0], sc2_ref[0])
    h2_ref[rows] = _pack_rows(h2)
    logits = _dot(h2.astype(BF16), wr_ref[...]) + br_ref[...]
    ids, gates, counts = _top4(logits)
    ids_ref[rows] = ids
    gates_ref[0, rows] = gates
    return counts


def _sub_tiles(n_rows):
    sub = min(OUT_SUB_TILE, n_rows)
    return [slice(s, s + sub) for s in range(0, n_rows, sub)]


def _tail_specs(b, t, d, tm, n_inputs_before, tokens, shared):
    n_total, offset = tokens
    n_t = t // tm
    assert offset % tm == 0
    row = lambda bi, i: (bi, i, 0)
    flat = lambda bi, i: (offset // tm + bi * n_t + i, 0)
    per_b = lambda bi, i: (bi, 0, 0)
    const = lambda bi, i: (0, 0)
    extra, aliases = (), {}
    if shared is not None:
        extra = tuple(shared)
        aliases = {n_inputs_before + 7: 1, n_inputs_before + 8: 2}
    in_specs = [
        pl.BlockSpec((1, tm, d), row),
        pl.BlockSpec((1, 1, d), per_b),
        pl.BlockSpec((1, d), const),
        pl.BlockSpec((1, 1, d), per_b),
        pl.BlockSpec((1, 1, d), per_b),
        pl.BlockSpec((d, ROUTER_PAD), const),
        pl.BlockSpec((1, ROUTER_PAD), const),
    ] + [pl.BlockSpec(memory_space=pl.ANY)] * len(extra)
    out_specs = [
        pl.BlockSpec((1, tm, d), row),
        pl.BlockSpec((tm, d // 2), flat),
        pl.BlockSpec((tm, ROUTER_PAD), flat),
        pl.BlockSpec((1, tm, ROUTER_PAD), row),
        pl.BlockSpec((1, SUBLANES, ROUTER_PAD), lambda bi, i: (bi * n_t + i, 0, 0)),
    ]
    out_shape = [
        jax.ShapeDtypeStruct((b, t, d), F32),
        jax.ShapeDtypeStruct((n_total, d // 2), jnp.int32),
        jax.ShapeDtypeStruct((n_total, ROUTER_PAD), jnp.int32),
        jax.ShapeDtypeStruct((b, t, ROUTER_PAD), F32),
        jax.ShapeDtypeStruct((b * n_t, SUBLANES, ROUTER_PAD), F32),
    ]
    return in_specs, out_specs, out_shape, extra, aliases


def _halo_fix(rolled, at_row, halo_row, present):
    n = rolled.shape[0]
    first = at_row < SUBLANES
    assert first or at_row >= n - SUBLANES
    slab = rolled[:SUBLANES] if first else rolled[n - SUBLANES:]
    sub = lax.broadcasted_iota(jnp.int32, slab.shape, 0)
    fill = jnp.where(present, halo_row, jnp.zeros_like(halo_row))
    slab = jnp.where(sub == at_row % SUBLANES, fill, slab)
    return jnp.concatenate([slab, rolled[SUBLANES:]] if first else [rolled[:n - SUBLANES], slab], axis=0)


def _even_out_kernel(oa_ref, bg_ref, cg_ref, xin_ref, cgp_ref, xinp_ref, cgn_ref, xinn_ref, cw_ref, cb_ref,
                     wa_ref, wb_ref, *tail_refs, tm):
    i = pl.program_id(1)
    has_prev = i > 0
    has_next = i < pl.num_programs(1) - 1
    u = cg_ref[0].astype(F32) * xin_ref[0].astype(F32)
    u_prev = (cgp_ref[0].astype(F32) * xinp_ref[0].astype(F32))[HALO - 1:HALO]
    u_next = (cgn_ref[0].astype(F32) * xinn_ref[0].astype(F32))[0:1]
    u_m1 = _halo_fix(pltpu.roll(u, 1, 0), 0, u_prev, has_prev)
    u_p1 = _halo_fix(pltpu.roll(u, tm - 1, 0), tm - 1, u_next, has_next)
    cw = cw_ref[...]
    conv = u_m1 * cw[0:1] + u * cw[1:2] + u_p1 * cw[2:3] + cb_ref[...]
    o_b = (bg_ref[0].astype(F32) * conv).astype(BF16)
    counts = 0.0
    for rows in _sub_tiles(tm):
        y = _dot(oa_ref[0, rows], wa_ref[...]) + _dot(o_b[rows], wb_ref[...])
        counts = counts + _layer_tail(y, rows, *tail_refs)
    tail_refs[-1][0] = jnp.broadcast_to(counts, tail_refs[-1].shape[1:])


def _even_out(o_a, proj, conv_w, conv_b, w_out, x, g1, n2, sh2, sc2, w_r, b_r, tokens, shared=None):
    b, t, d = x.shape
    w = NA_WIDTH
    tm = min(t, OUT_TILE)
    hb = tm // HALO
    n_hblocks = t // HALO
    row = lambda bi, i: (bi, i, 0)
    const = lambda bi, i: (0, 0)
    prev = lambda col: (lambda bi, i: (bi, jnp.maximum(i * hb - 1, 0), col))
    nxt = lambda col: (lambda bi, i: (bi, jnp.minimum((i + 1) * hb, n_hblocks - 1), col))
    tail_in, out_specs, out_shape, extra, aliases = _tail_specs(b, t, d, tm, 12, tokens, shared)
    in_specs = [
        pl.BlockSpec((1, tm, w), row),
        pl.BlockSpec((1, tm, w), lambda bi, i: (bi, i, 3)),
        pl.BlockSpec((1, tm, w), lambda bi, i: (bi, i, 4)),
        pl.BlockSpec((1, tm, w), lambda bi, i: (bi, i, 5)),
        pl.BlockSpec((1, HALO, w), prev(4)),
        pl.BlockSpec((1, HALO, w), prev(5)),
        pl.BlockSpec((1, HALO, w), nxt(4)),
        pl.BlockSpec((1, HALO, w), nxt(5)),
        pl.BlockSpec((SC_CONV, w), const),
        pl.BlockSpec((1, w), const),
        pl.BlockSpec((w, d), const),
        pl.BlockSpec((w, d), const),
    ] + tail_in
    return pl.pallas_call(
        functools.partial(_even_out_kernel, tm=tm),
        out_shape=out_shape,
        grid=(b, t // tm),
        in_specs=in_specs,
        out_specs=out_specs,
        input_output_aliases=aliases,
        compiler_params=_params(("parallel", "parallel")),
        name="even_out",
    )(o_a, proj, proj, proj, proj, proj, proj, proj, conv_w, conv_b.reshape(1, w),
      w_out[:w], w_out[w:], x, g1, n2, sh2, sc2, w_r, b_r, *extra)


def _log_sigmoid(x):
    return jnp.minimum(x, 0.0) - jnp.log1p(jnp.exp(-jnp.abs(x)))


def _sigmoid(x):
    return 0.5 + 0.5 * jnp.tanh(0.5 * x)


def _lru_sub_tile(ucb_time, state, perm_ref, w_ref, ba_ref, bx_ref, lam_ref, want_hidden, reverse):
    sub, width = ucb_time.shape
    blk = width // LRU_BLOCKS
    steps = sub // SUBLANES
    uc = _dot(perm_ref[0], ucb_time)
    ucb = uc.astype(BF16)
    za, zx = [], []
    for h in range(LRU_BLOCKS):
        z = _dot(ucb[:, h * blk:(h + 1) * blk], w_ref[h])
        za.append(z[:, :blk])
        zx.append(z[:, blk:])
    r = _sigmoid(jnp.concatenate(za, axis=1) + ba_ref[...])
    gate_i = _sigmoid(jnp.concatenate(zx, axis=1) + bx_ref[...])
    log_a = (LRU_C * _log_sigmoid(lam_ref[...])) * r
    a = jnp.exp(log_a)
    th = jnp.tanh(log_a)
    num = -2.0 * th
    mult = jnp.where(num > 0.0, num * lax.rsqrt(num * (1.0 - th)), 0.0)
    bcoef = mult * gate_i * uc

    grp = lambda v, j: v[j * SUBLANES:(j + 1) * SUBLANES]
    prods, local = [grp(a, 0)], [grp(bcoef, 0)]
    for j in range(1, steps):
        aj = grp(a, j)
        local.append(aj * local[-1] + grp(bcoef, j))
        prods.append(aj * prods[-1])

    carry_in = [None] * SUBLANES
    for s in (range(SUBLANES - 1, -1, -1) if reverse else range(SUBLANES)):
        carry_in[s] = state
        state = prods[-1][s:s + 1] * state + local[-1][s:s + 1]
    if not want_hidden:
        return state, None
    start = jnp.concatenate(carry_in, axis=0)
    hidden = jnp.concatenate([prods[j] * start + local[j] for j in range(steps)], axis=0)
    return state, _dot(perm_ref[1], hidden.astype(BF16)).astype(BF16)


def _lru_tile(u_ref, perm_ref, w_ref, ba_ref, bx_ref, lam_ref, carry_ref, o_ref, *, reverse):
    sub = perm_ref.shape[1]
    n_sub = u_ref.shape[1] // sub
    state = carry_ref[0:1, :]
    for k in (range(n_sub - 1, -1, -1) if reverse else range(n_sub)):
        rows = slice(k * sub, (k + 1) * sub)
        state, hidden = _lru_sub_tile(u_ref[0, rows], state, perm_ref, w_ref, ba_ref, bx_ref, lam_ref,
                                      o_ref is not None, reverse)
        if o_ref is not None:
            o_ref[0, rows] = hidden
    carry_ref[...] = jnp.broadcast_to(state, carry_ref.shape)


def _lru_kernel(uc_ref, ul_ref, perm_ref, w_ref, ba_ref, bx_ref, lam_ref, o_ref, carry_ref, *, n_ctx_tiles, reverse):
    j = pl.program_id(1)
    shared = (perm_ref, w_ref, ba_ref, bx_ref, lam_ref, carry_ref)

    @pl.when(j == 0)
    def _():
        carry_ref[...] = jnp.zeros_like(carry_ref)

    @pl.when(j < n_ctx_tiles)
    def _():
        _lru_tile(uc_ref, *shared, None, reverse=reverse)

    @pl.when(j >= n_ctx_tiles)
    def _():
        _lru_tile(ul_ref, *shared, o_ref, reverse=reverse)


def _lru_scan(proj, u_ctx, w_cat, ba, bx, lam, reverse):
    b, t, _ = proj.shape
    l, width = u_ctx.shape[1], u_ctx.shape[2]
    sub = min(256, l, t)
    tc = sub * min(LRU_SUB_TILES, l // sub)
    tl = sub * min(LRU_SUB_TILES, t // sub)
    n_c, n_l = l // tc, t // tl

    def pos_of(step, n):
        step = jnp.clip(step, 0, n - 1)
        return (n - 1 - step) if reverse else step

    def tile_map(off, n, col):
        return lambda bi, j: (bi, pos_of(j - off, n), col)

    const2 = lambda bi, j: (0, 0)
    in_specs = [
        pl.BlockSpec((1, tc, width), tile_map(0, n_c, 0)),
        pl.BlockSpec((1, tl, width), tile_map(n_c, n_l, 1)),
        pl.BlockSpec((2, sub, sub), lambda bi, j: (0, 0, 0)),
        pl.BlockSpec(w_cat.shape, lambda bi, j: (0, 0, 0)),
        pl.BlockSpec((1, width), const2),
        pl.BlockSpec((1, width), const2),
        pl.BlockSpec((1, width), const2),
    ]
    steps = sub // SUBLANES
    step, block = np.divmod(np.arange(sub), SUBLANES)
    time_of_row = block * steps + (steps - 1 - step if reverse else step)
    perm = (time_of_row[:, None] == np.arange(sub)[None]).astype(np.float32)
    perms = jnp.asarray(np.stack([perm, perm.T]), BF16)
    kern = functools.partial(_lru_kernel, n_ctx_tiles=n_c, reverse=reverse)
    return pl.pallas_call(
        kern,
        out_shape=jax.ShapeDtypeStruct((b, t, width), BF16),
        grid=(b, n_c + n_l),
        in_specs=in_specs,
        out_specs=pl.BlockSpec((1, tl, width), tile_map(n_c, n_l, 0)),
        scratch_shapes=[pltpu.VMEM((SUBLANES, width), F32)],
        compiler_params=_params(("parallel", "arbitrary")),
        name="lru_scan_bwd" if reverse else "lru_scan_fwd",
    )(u_ctx, proj, perms, w_cat, ba.reshape(1, width), bx.reshape(1, width), lam.reshape(1, width))


def _odd_out_kernel(hf_ref, hb_ref, gate_ref, w_ref, *tail_refs):
    counts = 0.0
    for rows in _sub_tiles(hf_ref.shape[1]):
        hsum = hf_ref[0, rows].astype(F32) + hb_ref[0, rows].astype(F32)
        z = hsum * jax.nn.gelu(gate_ref[0, rows].astype(F32), approximate=True)
        y = _dot(z.astype(BF16), w_ref[...])
        counts = counts + _layer_tail(y, rows, *tail_refs)
    tail_refs[-1][0] = jnp.broadcast_to(counts, tail_refs[-1].shape[1:])


def _odd_out(h_f, h_b, proj, w_out, x, g1, n2, sh2, sc2, w_r, b_r):
    b, t, d = x.shape
    width = h_f.shape[-1]
    tm = min(t, OUT_TILE)
    row = lambda bi, i: (bi, i, 0)
    tail_in, out_specs, out_shape, _, _ = _tail_specs(b, t, d, tm, 4, (b * t, 0), None)
    in_specs = [
        pl.BlockSpec((1, tm, width), row),
        pl.BlockSpec((1, tm, width), row),
        pl.BlockSpec((1, tm, width), row),
        pl.BlockSpec((width, d), lambda bi, i: (0, 0)),
    ] + tail_in
    return pl.pallas_call(
        _odd_out_kernel,
        out_shape=out_shape,
        grid=(b, t // tm),
        in_specs=in_specs,
        out_specs=out_specs,
        compiler_params=_params(("parallel", "parallel")),
        name="odd_out",
    )(h_f, h_b, proj, w_out, x, g1, n2, sh2, sc2, w_r, b_r)


def _expert_kernel(be_ref, next_ref, rows_ref, nb_ref, x_ref, wg_hbm, bg_ref, wu_hbm, bu_ref, wd_hbm, bd_ref,
                   o_ref, w_f32, wg_bf, wu_bf, wd_bf, h_bf, sem, *, layer, chunk):
    i = pl.program_id(0)
    expert = be_ref[i]
    used = i < nb_ref[0]

    def weight_copies(e):
        return [pltpu.make_async_copy(src.at[layer, e], w_f32.at[j], sem.at[j])
                for j, src in enumerate((wg_hbm, wu_hbm, wd_hbm))]

    @pl.when(i == 0)
    def _():
        for cp in weight_copies(expert):
            cp.start()

    @pl.when(used & ((i == 0) | (expert != be_ref[jnp.maximum(i - 1, 0)])))
    def _():
        for cp in weight_copies(expert):
            cp.wait()
        wg_bf[...] = w_f32[0].astype(BF16)
        wu_bf[...] = w_f32[1].astype(BF16)
        wd_bf[...] = w_f32[2].astype(BF16)

        @pl.when(next_ref[i] >= 0)
        def _():
            for cp in weight_copies(next_ref[i]):
                cp.start()

    sub = h_bf.shape[0]
    n_sub = x_ref.shape[0] // sub
    n_rows = jnp.where(used, rows_ref[i], 0)
    n_live = (n_rows + sub - 1) // sub

    def live_sub_block(s, carry):
        rows = pl.ds(pl.multiple_of(s * sub, sub), sub)
        xp = x_ref[rows]
        row = lax.broadcasted_iota(jnp.int32, xp.shape, 0) + s * sub
        xp = jnp.where(row < n_rows, xp, 0)
        x = jnp.concatenate(_unpack_rows(xp), axis=1).astype(BF16)
        for c in range(wg_bf.shape[1] // chunk):
            cs = slice(c * chunk, (c + 1) * chunk)
            g = jnp.minimum(_dot(x, wg_bf[:, cs]) + bg_ref[0, 0, :, cs], SWIGLU_LIMIT)
            u = jnp.clip(_dot(x, wu_bf[:, cs]) + bu_ref[0, 0, :, cs], -SWIGLU_LIMIT, SWIGLU_LIMIT)
            h_bf[:, cs] = (g * jax.nn.sigmoid(SWIGLU_ALPHA * g) * (u + 1.0)).astype(BF16)
        o_ref[rows] = _pack_rows(_dot(h_bf[...], wd_bf[...]) + bd_ref[0, 0])
        return carry

    def empty_sub_block(s, carry):
        o_ref[pl.ds(pl.multiple_of(s * sub, sub), sub)] = jnp.zeros((sub, o_ref.shape[1]), o_ref.dtype)
        return carry

    lax.fori_loop(0, n_live, live_sub_block, 0)
    lax.fori_loop(n_live, n_sub, empty_sub_block, 0)


def _experts(xb, block_e, next_e, block_rows, n_used, layer, wg, bg, wu, bu, wd, bd):
    n_slots = xb.shape[0]
    depth, n_e, d, d_exp = wg.shape
    assert d == d_exp
    tm = EXPERT_TILE
    n_blocks = n_slots // tm
    xmap = lambda i, be, ne, br, nb: (jnp.minimum(i, nb[0] - 1), 0)
    bmap = lambda i, be, ne, br, nb: (layer, be[i], 0, 0)
    hbm = pl.BlockSpec(memory_space=pl.ANY)
    grid_spec = pltpu.PrefetchScalarGridSpec(
        num_scalar_prefetch=4,
        grid=(n_blocks,),
        in_specs=[
            pl.BlockSpec((tm, d // 2), xmap),
            hbm,
            pl.BlockSpec((1, 1, 1, d_exp), bmap),
            hbm,
            pl.BlockSpec((1, 1, 1, d_exp), bmap),
            hbm,
            pl.BlockSpec((1, 1, 1, d), bmap),
        ],
        out_specs=pl.BlockSpec((tm, d // 2), lambda i, be, ne, br, nb: (i, 0)),
        scratch_shapes=[pltpu.VMEM((3, d, d_exp), F32), pltpu.VMEM((d, d_exp), BF16), pltpu.VMEM((d, d_exp), BF16),
                        pltpu.VMEM((d_exp, d), BF16), pltpu.VMEM((EXPERT_SUB, d_exp), BF16),
                        pltpu.SemaphoreType.DMA((3,))],
    )
    return pl.pallas_call(
        functools.partial(_expert_kernel, layer=layer, chunk=256),
        out_shape=jax.ShapeDtypeStruct((n_slots, d // 2), jnp.int32),
        grid_spec=grid_spec,
        compiler_params=_params(("arbitrary",)),
        name="experts",
    )(block_e, next_e, block_rows, n_used, xb, wg, bg.reshape(depth, n_e, 1, d_exp), wu,
      bu.reshape(depth, n_e, 1, d_exp), wd, bd.reshape(depth, n_e, 1, d))


def _combine_kernel(x_ref, g2_ref, gates_ref, y_ref, o_ref):
    o_ref[0] = _moe_residual(x_ref[0], g2_ref[0], gates_ref[0], y_ref[...])


def _combine(x, g2, gates, y_sel, tok_offset):
    b, t, d = x.shape
    tm = int(np.gcd(min(t, ROW_TILE), tok_offset)) if tok_offset else min(t, ROW_TILE)
    n_t = t // tm
    off = tok_offset // tm
    row = lambda bi, i: (bi, i, 0)
    return pl.pallas_call(
        _combine_kernel,
        out_shape=jax.ShapeDtypeStruct((b, t, d), F32),
        grid=(b, n_t),
        in_specs=[
            pl.BlockSpec((1, tm, d), row),
            pl.BlockSpec((1, 1, d), lambda bi, i: (bi, 0, 0)),
            pl.BlockSpec((1, tm, ROUTER_PAD), row),
            pl.BlockSpec((TOP_K, tm, d // 2), lambda bi, i: (0, off + bi * n_t + i, 0)),
        ],
        out_specs=pl.BlockSpec((1, tm, d), row),
        input_output_aliases={0: 0},
        compiler_params=_params(("parallel", "parallel")),
        name="moe_combine",
    )(x, g2, gates, y_sel)


def _route_kernel(ids_ref, upper_ref, starts_ref, dest_ref, run_ref):
    i = pl.program_id(0)
    tm = ids_ref.shape[0]
    ids_t = ids_ref[...].T
    expert = lax.broadcasted_iota(jnp.int32, (N_EXPERTS, tm), 0)
    chosen = [ids_t[k:k + 1, :] == expert for k in range(TOP_K)]
    picks = functools.reduce(jnp.add, [c.astype(F32) for c in chosen])

    @pl.when(i == 0)
    def _():
        run_ref[...] = starts_ref[...]

    before = _dot(picks.astype(BF16), upper_ref[...])
    slot = before + run_ref[:, 0:1]
    rows = [jnp.sum(jnp.where(c, slot, 0.0), axis=0, keepdims=True) for c in chosen]
    rows += [jnp.zeros_like(rows[0])] * (dest_ref.shape[0] - TOP_K)
    dest_ref[...] = jnp.concatenate(rows, axis=0).astype(jnp.int32)
    run_ref[...] += jnp.sum(picks, axis=1, keepdims=True)


def _route(ids, region_starts):
    n = ids.shape[0]
    tm = int(np.gcd(n, ROUTE_TILE))
    upper = jnp.asarray(np.triu(np.ones((tm, tm), np.float32), 1), BF16)
    starts = jnp.broadcast_to(region_starts.astype(F32)[:, None], (N_EXPERTS, LANES))
    return pl.pallas_call(
        _route_kernel,
        out_shape=jax.ShapeDtypeStruct((SUBLANES, n), jnp.int32),
        grid=(n // tm,),
        in_specs=[pl.BlockSpec((tm, ROUTER_PAD), lambda i: (i, 0)),
                  pl.BlockSpec((tm, tm), lambda i: (0, 0)),
                  pl.BlockSpec((N_EXPERTS, LANES), lambda i: (0, 0))],
        out_specs=pl.BlockSpec((SUBLANES, tm), lambda i: (0, i)),
        scratch_shapes=[pltpu.VMEM((N_EXPERTS, LANES), F32)],
        compiler_params=_params(("arbitrary",)),
        name="route",
    )(ids, upper, starts)


def _row_gather(table, idx):
    info = plsc.get_sparse_core_info()
    n_cores, n_workers = info.num_cores, info.num_cores * info.num_subcores
    n_rows, width = idx.shape[0], table.shape[1]
    chunk_rows = SC_GATHER_ROWS
    per_worker = n_rows // n_workers
    n_chunks = per_worker // chunk_rows
    assert per_worker * n_workers == n_rows and n_chunks * chunk_rows == per_worker and n_chunks % 2 == 0
    mesh = plsc.VectorSubcoreMesh(core_axis_name="c", subcore_axis_name="s")

    @functools.partial(
        pl.kernel, mesh=mesh,
        out_type=jax.ShapeDtypeStruct((n_rows, width), table.dtype),
        scratch_types=[pltpu.VMEM((n_chunks, chunk_rows), jnp.int32), pltpu.VMEM((2, chunk_rows, width), table.dtype),
                       pltpu.SemaphoreType.DMA((2,)), pltpu.SemaphoreType.DMA((2,))],
    )
    def gather_kernel(table_hbm, idx_hbm, out_hbm, idx_v, rows_v, gather_sem, write_sem):
        worker = lax.axis_index("s") * n_cores + lax.axis_index("c")
        pltpu.sync_copy(idx_hbm.at[worker], idx_v)

        def gather(c, slot):
            return pltpu.make_async_copy(table_hbm.at[idx_v.at[c]], rows_v.at[slot], gather_sem.at[slot])

        def write(c, slot):
            first_row = worker * per_worker + c * chunk_rows
            return pltpu.make_async_copy(rows_v.at[slot], out_hbm.at[pl.ds(first_row, chunk_rows)], write_sem.at[slot])

        gather(0, 0).start()

        @pl.loop(0, n_chunks, step=2)
        def _(c0):
            for slot in (0, 1):
                c = c0 + slot
                gather(c, slot).wait()

                @pl.when(c >= 1)
                def _():
                    write(c - 1, 1 - slot).wait()

                @pl.when(c + 1 < n_chunks)
                def _():
                    gather(c + 1, 1 - slot).start()

                write(c, slot).start()

        write(n_chunks - 1, 1).wait()

    return gather_kernel(table, idx.reshape(n_workers, n_chunks, chunk_rows))


def _row_scatter(rows, dest, n_out):
    info = plsc.get_sparse_core_info()
    n_cores, n_workers = info.num_cores, info.num_cores * info.num_subcores
    n_choices, n_rows = dest.shape
    width = rows.shape[1]
    chunk_rows = SC_SCATTER_ROWS
    per_worker = n_rows // (n_workers * chunk_rows)
    assert per_worker * n_workers * chunk_rows == n_rows == rows.shape[0]
    idx = dest.reshape(n_choices, n_workers, per_worker, chunk_rows)
    mesh = plsc.VectorSubcoreMesh(core_axis_name="c", subcore_axis_name="s")

    @functools.partial(
        pl.kernel, mesh=mesh,
        out_type=jax.ShapeDtypeStruct((n_out, width), rows.dtype),
        scratch_types=[pltpu.VMEM((n_choices, per_worker, chunk_rows), jnp.int32),
                       pltpu.VMEM((chunk_rows, width), rows.dtype), pltpu.SemaphoreType.DMA],
    )
    def scatter_kernel(rows_hbm, idx_hbm, out_hbm, idx_v, rows_v, sem):
        worker = lax.axis_index("s") * n_cores + lax.axis_index("c")
        for k in range(n_choices):
            pltpu.sync_copy(idx_hbm.at[k, worker], idx_v.at[k])

        @pl.loop(0, per_worker)
        def _(c):
            first_row = (worker * per_worker + c) * chunk_rows
            pltpu.sync_copy(rows_hbm.at[pl.ds(first_row, chunk_rows)], rows_v)
            copies = [pltpu.async_copy(rows_v, out_hbm.at[idx_v.at[k, c]], sem) for k in range(n_choices)]
            for cp in copies:
                cp.wait()

    return scatter_kernel(rows, idx)


def _expert_counts(step_picks):
    return jnp.sum(step_picks[:, 0, :N_EXPERTS], axis=0).astype(jnp.int32)


def _moe(h2, ids, counts, layer, w_exp):
    n = h2.shape[0]
    nk = n * TOP_K
    tm = EXPERT_TILE
    padded = (counts + tm - 1) // tm * tm
    pends = jnp.cumsum(padded)
    dest = _route(ids, pends - padded)[:TOP_K]
    n_blocks = -(-nk // tm) + N_EXPERTS
    n_slots = n_blocks * tm
    block_start = jnp.arange(n_blocks, dtype=jnp.int32) * tm
    block_e = jnp.minimum(jnp.sum((pends[None] <= block_start[:, None]).astype(jnp.int32), axis=1), N_EXPERTS - 1)
    is_e = (block_e[:, None] == jnp.arange(N_EXPERTS, dtype=jnp.int32)[None]).astype(jnp.int32)
    block_rows = jnp.clip(jnp.sum(is_e * (pends - padded + counts)[None], axis=1) - block_start, 0, tm)
    n_used = (pends[-1] // tm).astype(jnp.int32).reshape(1)
    e_ids = jnp.arange(N_EXPERTS, dtype=jnp.int32)
    later = (e_ids[None] > e_ids[:, None]) & (counts[None] > 0)
    next_nonempty = jnp.min(jnp.where(later, e_ids[None], N_EXPERTS), axis=1)
    next_nonempty = jnp.where(next_nonempty == N_EXPERTS, -1, next_nonempty)
    next_e = jnp.sum(is_e * next_nonempty[None], axis=1)
    xb = _row_scatter(h2, dest, n_slots)
    return _experts(xb, block_e, next_e, block_rows, n_used, layer, *w_exp), dest


def _gather_choices(yb, dest, lo, hi):
    return _row_gather(yb, dest[:, lo:hi].reshape(-1)).reshape(TOP_K, hi - lo, -1)


def _mod_parts(mod_l, b):
    d = mod_l.shape[-1] // 6
    lat = [mod_l[:b, k * d:(k + 1) * d].reshape(b, 1, d) for k in range(6)]
    ctx = [jnp.broadcast_to(mod_l[b, k * d:(k + 1) * d].reshape(1, 1, d), (b, 1, d)) for k in range(6)]
    return lat, ctx


def kernel(x, c, ctx, c_ctx, ada_w, ada_b, norm1_g, norm2_g, ev_w_in, ev_w_out, ev_q_gain, ev_k_gain, ev_rpb, ev_conv_w, ev_conv_b, od_w_in, od_w_out, od_conv_w, od_conv_b, od_fwd_wa, od_fwd_ba, od_fwd_wx, od_fwd_bx, od_fwd_lam, od_bwd_wa, od_bwd_ba, od_bwd_wx, od_bwd_bx, od_bwd_lam, router_w, router_b, exp_w_gate, exp_b_gate, exp_w_up, exp_b_up, exp_w_down, exp_b_down):
    b, t, d = x.shape
    l = ctx.shape[1]
    assert ada_w.shape[0] == DEPTH == 2 and t % GRID_W == 0 and t // GRID_W >= WIN_H

    n_rows_c = -(-(b + 1) // SUBLANES) * SUBLANES
    cvec = jnp.zeros((n_rows_c, d), F32).at[:b].set(c).at[b].set(c_ctx)
    mod = _ada_mod(cvec, ada_w, ada_b)

    def router(layer):
        w_r = jnp.zeros((d, ROUTER_PAD), F32).at[:, :N_EXPERTS].set(router_w[layer]).astype(BF16)
        b_r = jnp.zeros((1, ROUTER_PAD), F32).at[0, :N_EXPERTS].set(router_b[layer])
        return w_r, b_r

    w_exp = (exp_w_gate, exp_b_gate, exp_w_up, exp_b_up, exp_w_down, exp_b_down)

    (sh1, sc1, g1, sh2, sc2, g2), (csh1, csc1, cg1, csh2, csc2, cg2) = _mod_parts(mod[0], b)
    n1 = norm1_g[0].reshape(1, d)
    n2 = norm2_g[0].reshape(1, d)
    w_in = ev_w_in[0].astype(BF16)
    w_out = ev_w_out[0].astype(BF16)
    q_scale = NA_HEAD_DIM ** -0.5 * LOG2_E
    head_gain = jnp.stack([jnp.tile(ev_q_gain[0] * q_scale, NA_HEADS), jnp.tile(ev_k_gain[0], NA_HEADS)])
    head_gain = head_gain.reshape(2, 1, NA_WIDTH).astype(F32)
    n_tok = b * (l + t)
    proj, *fresh = _inproj(x, n1, sh1, sc1, w_in, head_gain, ((n_tok, d // 2), (n_tok, ROUTER_PAD)))
    proj_c, = _inproj(ctx, n1, csh1, csc1, w_in, head_gain)
    kh, r0, row_type, patterns = _na_tables(t // GRID_W)
    bias_tab = _na_bias_table(ev_rpb[0], patterns)
    o_a = _neighbourhood_attention(proj, proj_c, bias_tab, r0, row_type, kh)
    oc_a = _context_attention(proj_c)
    w_r, b_r = router(0)
    c1, tokens, ids_all, gates_c, picks_c = _even_out(oc_a, proj_c, ev_conv_w[0], ev_conv_b[0], w_out, ctx, cg1, n2,
                                                      csh2, csc2, w_r, b_r, (n_tok, 0), fresh)
    x1, tokens, ids_all, gates, picks = _even_out(o_a, proj, ev_conv_w[0], ev_conv_b[0], w_out, x, g1, n2, sh2, sc2,
                                                  w_r, b_r, (n_tok, b * l), (tokens, ids_all))
    yb, dest = _moe(tokens, ids_all, _expert_counts(picks_c) + _expert_counts(picks), 0, w_exp)
    y_sel = _gather_choices(yb, dest, 0, b * (l + t))
    g2_prev, cg2_prev = g2, cg2

    (sh1, sc1, g1, sh2, sc2, g2), (csh1, csc1, _, _, _, _) = _mod_parts(mod[1], b)
    n1 = norm1_g[1].reshape(1, d)
    n2 = norm2_g[1].reshape(1, d)
    w_in = od_w_in[0].astype(BF16)
    width = w_in.shape[1] // 2
    x, proj = _combine_inproj_conv(x1, g2_prev, gates, y_sel, b * l, n1, sh1, sc1, w_in, od_conv_w[0], od_conv_b[0],
                                   width)
    _, u_ctx = _combine_inproj_conv(c1, cg2_prev, gates_c, y_sel, 0, n1, csh1, csc1, w_in[:, width:], od_conv_w[0],
                                    od_conv_b[0], 0)
    h_dir = []
    for reverse, (wa, ba, wx, bx, lam) in ((False, (od_fwd_wa, od_fwd_ba, od_fwd_wx, od_fwd_bx, od_fwd_lam)),
                                           (True, (od_bwd_wa, od_bwd_ba, od_bwd_wx, od_bwd_bx, od_bwd_lam))):
        w_cat = jnp.concatenate([wa[0], wx[0]], axis=-1).astype(BF16)
        h_dir.append(_lru_scan(proj, u_ctx, w_cat, ba[0], bx[0], lam[0], reverse))
    w_r, b_r = router(1)
    x1, h2, ids, gates, picks = _odd_out(h_dir[0], h_dir[1], proj, od_w_out[0].astype(BF16), x, g1, n2, sh2, sc2,
                                         w_r, b_r)
    yb, dest = _moe(h2, ids, _expert_counts(picks), 1, w_exp)
    return _combine(x1, g2, gates, _gather_choices(yb, dest, 0, b * t), 0)
```

```python
import functools

import numpy as np
import jax
import jax.numpy as jnp
from jax import lax
from jax.experimental import pallas as pl
from jax.experimental.pallas import tpu as pltpu
from jax.experimental.pallas import tpu_sc as plsc

DEPTH = 2
GRID_W = 64
EPS = 1e-6
NEG_INF = -1e30
LOG2_E = 1.4426950408889634
NA_HEADS = 8
NA_HEAD_DIM = 64
NA_WIDTH = NA_HEADS * NA_HEAD_DIM
HEAD_PAIRS = NA_HEADS // 2
WIN_H = 8
WIN_W = 16
SC_CONV = 3
LRU_BLOCKS = 4
LRU_CONV = 4
LRU_C = 8.0
N_EXPERTS = 32
TOP_K = 4
SWIGLU_LIMIT = 7.0
SWIGLU_ALPHA = 1.702

LANES = 128
SUBLANES = 8
MXU_DEPTH = 256
HALO = 16
ROUTER_PAD = LANES
ADA_COL_TILE = 1536
ROW_TILE = 1024
NA_ROWS_PER_STEP = 8
LRU_SUB_TILES = 8
OUT_TILE = 1024
OUT_SUB_TILE = 512
ROUTE_TILE = 2048
EXPERT_TILE = 2048
EXPERT_SUB = 512
SC_GATHER_ROWS = 64
SC_SCATTER_ROWS = 128
VMEM_LIMIT = 56 * 1024 * 1024

F32 = jnp.float32
BF16 = jnp.bfloat16


def _params(sem, vmem=VMEM_LIMIT):
    return pltpu.CompilerParams(dimension_semantics=sem, vmem_limit_bytes=vmem)


def _dot(a, b):
    return jnp.dot(a, b, preferred_element_type=F32)


def _dot_nt(a, b):
    return lax.dot_general(a, b, (((1,), (1,)), ((), ())), preferred_element_type=F32)


def _pack_rows(v):
    w = v.shape[-1] // 2
    lo = lax.bitcast_convert_type(v[:, :w].astype(BF16).astype(F32), jnp.int32)
    hi = lax.bitcast_convert_type(v[:, w:].astype(BF16).astype(F32), jnp.int32)
    return lax.shift_right_logical(lo, 16) | (hi & jnp.int32(-65536))


def _unpack_rows(p):
    lo = lax.bitcast_convert_type(lax.shift_left(p, 16), F32)
    hi = lax.bitcast_convert_type(p & jnp.int32(-65536), F32)
    return lo, hi


def _rms_mod(x, g, shift, scale):
    ms = jnp.mean(x * x, axis=-1, keepdims=True)
    return (x * lax.rsqrt(ms + EPS)) * (g * (1.0 + scale)) + shift


def _ada_kernel(c_ref, w_ref, b_ref, o_ref):
    c = c_ref[...]
    s = (c * jax.nn.sigmoid(c)).astype(BF16)
    o_ref[0] = _dot(s, w_ref[0].astype(BF16)) + b_ref[0]


def _ada_mod(cvec, ada_w, ada_b):
    depth, d, n = ada_w.shape
    r = cvec.shape[0]
    tn = ADA_COL_TILE
    return pl.pallas_call(
        _ada_kernel,
        out_shape=jax.ShapeDtypeStruct((depth, r, n), F32),
        grid=(depth, n // tn),
        in_specs=[
            pl.BlockSpec((r, d), lambda l, j: (0, 0)),
            pl.BlockSpec((1, d, tn), lambda l, j: (l, 0, j)),
            pl.BlockSpec((1, 1, tn), lambda l, j: (l, 0, j)),
        ],
        out_specs=pl.BlockSpec((1, r, tn), lambda l, j: (l, 0, j)),
        compiler_params=_params(("parallel", "parallel")),
        name="ada_mod",
    )(cvec, ada_w, ada_b.reshape(depth, 1, n))


def _inproj_kernel(x_ref, g_ref, sh_ref, sc_ref, w_ref, hg_ref, ones_ref, o_ref, *zero_refs, n_tiles, tn,
                   n_headnorm):
    for z_ref in zero_refs:
        z_ref[...] = jnp.zeros_like(z_ref)
    h = _rms_mod(x_ref[0], g_ref[...], sh_ref[0], sc_ref[0]).astype(BF16)
    for j in range(n_tiles):
        y = _dot(h, w_ref[:, j * tn:(j + 1) * tn])
        if j < n_headnorm:
            ysq = (y * y).astype(BF16)
            kw = ones_ref.shape[0]
            ms = jnp.concatenate([_dot(ysq[:, c:c + kw], ones_ref[...]) for c in range(0, tn, kw)], axis=1)
            y = y * lax.rsqrt(ms * (1.0 / NA_HEAD_DIM) + EPS) * hg_ref[j]
        o_ref[0, :, j * tn:(j + 1) * tn] = y.astype(o_ref.dtype)


def _inproj(x, g, shift, scale, w, head_gain, zero_shapes=()):
    b, t, d = x.shape
    n = w.shape[1]
    tn = NA_WIDTH
    tm = min(t, ROW_TILE)
    n_t = t // tm
    n_headnorm = head_gain.shape[0]
    hid = np.arange(MXU_DEPTH) // NA_HEAD_DIM
    ones_bd = jnp.asarray((hid[:, None] == hid[None, :]), BF16)
    kern = functools.partial(_inproj_kernel, n_tiles=n // tn, tn=tn, n_headnorm=n_headnorm)
    steps = b * n_t
    assert all(rows % (steps * SUBLANES) == 0 for rows, _ in zero_shapes)
    zero_specs = [pl.BlockSpec((rows // steps, cols), lambda bi, i: (bi * n_t + i, 0)) for rows, cols in zero_shapes]
    zero_out = [jax.ShapeDtypeStruct(shape, jnp.int32) for shape in zero_shapes]
    return pl.pallas_call(
        kern,
        out_shape=[jax.ShapeDtypeStruct((b, t, n), BF16)] + zero_out,
        grid=(b, n_t),
        in_specs=[
            pl.BlockSpec((1, tm, d), lambda bi, i: (bi, i, 0)),
            pl.BlockSpec((1, d), lambda bi, i: (0, 0)),
            pl.BlockSpec((1, 1, d), lambda bi, i: (bi, 0, 0)),
            pl.BlockSpec((1, 1, d), lambda bi, i: (bi, 0, 0)),
            pl.BlockSpec((d, n), lambda bi, i: (0, 0)),
            pl.BlockSpec(head_gain.shape, lambda bi, i: (0, 0, 0)),
            pl.BlockSpec((MXU_DEPTH, MXU_DEPTH), lambda bi, i: (0, 0)),
        ],
        out_specs=[pl.BlockSpec((1, tm, n), lambda bi, i: (bi, i, 0))] + zero_specs,
        compiler_params=_params(("parallel", "parallel")),
        name="inproj",
    )(x, g, shift, scale, w, head_gain, ones_bd)


def _moe_residual(x, g2, gates, y):
    acc_lo = acc_hi = 0.0
    for k in range(TOP_K):
        lo, hi = _unpack_rows(y[k])
        acc_lo = acc_lo + gates[:, k:k + 1] * lo
        acc_hi = acc_hi + gates[:, k:k + 1] * hi
    return x + g2 * jnp.concatenate([acc_lo, acc_hi], axis=1)


def _combine_inproj_conv_kernel(x_ref, xp_ref, xn_ref, gt_ref, gtp_ref, gtn_ref, y_ref, yp_ref, yn_ref, g2_ref,
                                g_ref, sh_ref, sc_ref, w_ref, cw_ref, cb_ref, xo_ref, o_ref, *,
                                n_tiles, n_plain, tn, tm, sub):
    i = pl.program_id(1)
    edge = SUBLANES
    g2 = g2_ref[0]
    x_tile = _moe_residual(x_ref[0], g2, gt_ref[0], y_ref[...])
    xo_ref[0] = x_tile
    x_ext = jnp.concatenate([_moe_residual(xp_ref[0], g2, gtp_ref[0], yp_ref[...]), x_tile,
                             _moe_residual(xn_ref[0], g2, gtn_ref[0], yn_ref[...])], axis=0)
    h = _rms_mod(x_ext, g_ref[...], sh_ref[0], sc_ref[0]).astype(BF16)
    first_step = i == 0
    last_step = i == pl.num_programs(1) - 1
    left = (LRU_CONV - 1) // 2
    n_sub = tm // sub
    for s in range(n_sub):
        h_s = h[s * sub:(s + 1) * sub + 2 * edge]
        rows = slice(s * sub, (s + 1) * sub)
        keep_prev = jnp.where(first_step, 0.0, 1.0) if s == 0 else 1.0
        keep_next = jnp.where(last_step, 0.0, 1.0) if s == n_sub - 1 else 1.0
        for j in range(n_tiles):
            y = _dot(h_s, w_ref[:, j * tn:(j + 1) * tn])
            if j < n_plain:
                o_ref[0, rows, j * tn:(j + 1) * tn] = y[edge:edge + sub].astype(o_ref.dtype)
                continue
            u = jnp.concatenate([y[:edge] * keep_prev, y[edge:edge + sub], y[edge + sub:] * keep_next], axis=0)
            cw = cw_ref[:, (j - n_plain) * tn:(j - n_plain + 1) * tn]
            uc = cb_ref[:, (j - n_plain) * tn:(j - n_plain + 1) * tn]
            n_ext = u.shape[0]
            for k in range(LRU_CONV):
                shifted = u if k == left else pltpu.roll(u, (left - k) % n_ext, 0)
                uc = uc + shifted[edge:edge + sub] * cw[k:k + 1]
            o_ref[0, rows, j * tn:(j + 1) * tn] = uc.astype(o_ref.dtype)


def _combine_inproj_conv(x1, g2, gates, y_sel, tok_offset, g, shift, scale, w, conv_w, conv_b, n_plain_cols):
    b, t, d = x1.shape
    n = w.shape[1]
    tn = NA_WIDTH
    tm = min(t, OUT_TILE)
    hb = tm // SUBLANES
    n_hblocks = t // SUBLANES
    y_hblocks = y_sel.shape[1] // SUBLANES
    assert tok_offset % tm == 0
    kern = functools.partial(_combine_inproj_conv_kernel, n_tiles=n // tn, n_plain=n_plain_cols // tn, tn=tn, tm=tm,
                             sub=min(tm, OUT_SUB_TILE))
    const = lambda bi, i: (0, 0)
    tile = lambda bi, i: (bi, i, 0)
    prev = lambda bi, i: (bi, jnp.maximum(i * hb - 1, 0), 0)
    nxt = lambda bi, i: (bi, jnp.minimum((i + 1) * hb, n_hblocks - 1), 0)
    y_row = lambda bi, i: (tok_offset + bi * t + i * tm) // SUBLANES
    per_b = lambda bi, i: (bi, 0, 0)
    return pl.pallas_call(
        kern,
        out_shape=[jax.ShapeDtypeStruct((b, t, d), F32), jax.ShapeDtypeStruct((b, t, n), BF16)],
        grid=(b, t // tm),
        in_specs=[
            pl.BlockSpec((1, tm, d), tile),
            pl.BlockSpec((1, SUBLANES, d), prev),
            pl.BlockSpec((1, SUBLANES, d), nxt),
            pl.BlockSpec((1, tm, ROUTER_PAD), tile),
            pl.BlockSpec((1, SUBLANES, ROUTER_PAD), prev),
            pl.BlockSpec((1, SUBLANES, ROUTER_PAD), nxt),
            pl.BlockSpec((TOP_K, tm, d // 2), lambda bi, i: (0, y_row(bi, i) // hb, 0)),
            pl.BlockSpec((TOP_K, SUBLANES, d // 2), lambda bi, i: (0, jnp.maximum(y_row(bi, i) - 1, 0), 0)),
            pl.BlockSpec((TOP_K, SUBLANES, d // 2),
                         lambda bi, i: (0, jnp.minimum(y_row(bi, i) + hb, y_hblocks - 1), 0)),
            pl.BlockSpec((1, 1, d), per_b),
            pl.BlockSpec((1, d), const),
            pl.BlockSpec((1, 1, d), per_b),
            pl.BlockSpec((1, 1, d), per_b),
            pl.BlockSpec((d, n), const),
            pl.BlockSpec(conv_w.shape, const),
            pl.BlockSpec((1, conv_w.shape[1]), const),
        ],
        out_specs=[pl.BlockSpec((1, tm, d), tile), pl.BlockSpec((1, tm, n), tile)],
        compiler_params=_params(("parallel", "parallel")),
        name="combine_inproj_conv",
    )(x1, x1, x1, gates, gates, gates, y_sel, y_sel, y_sel, g2, g, shift, scale, w, conv_w, conv_b.reshape(1, -1))


def _na_tables(rows):
    kh = min(WIN_H, rows)
    r = np.arange(rows)
    r0 = np.clip(r - kh // 2, 0, rows - kh)
    dr = r0[:, None] + np.arange(kh)[None] - r[:, None] + WIN_H - 1
    patterns, row_type = np.unique(dr, axis=0, return_inverse=True)
    return kh, r0.astype(np.int32), row_type.reshape(-1).astype(np.int32), patterns


def _na_bias_table(rpb, patterns):
    qc = np.arange(GRID_W)
    kc = np.arange(GRID_W)
    c0 = np.clip(qc - WIN_W // 2, 0, GRID_W - WIN_W)[:, None]
    valid = (kc[None] >= c0) & (kc[None] < c0 + WIN_W)
    dc = np.clip(kc[None] - qc[:, None] + WIN_W - 1, 0, 2 * WIN_W - 2)
    n_pat, kh = patterns.shape
    onehot_dc = jnp.asarray(dc[None] == np.arange(2 * WIN_W - 1)[:, None, None], F32)
    tab = jnp.einsum('hpic,cqk->phqik', rpb.astype(F32)[:, patterns], onehot_dc,
                     precision=lax.Precision.HIGHEST)
    tab = jnp.where(valid[None, None, :, None, :], tab * LOG2_E, NEG_INF)
    return tab.reshape(n_pat, HEAD_PAIRS, 2 * GRID_W, kh * GRID_W)


def _pair_attention(q, keys, values, biases):
    m = q.shape[0]
    qq = _stack_heads(q)
    scores = []
    for k, bias in zip(keys, biases):
        s = _dot_nt(qq, k)
        scores.append(s if bias is None else s + bias)
    s = jnp.concatenate(scores, axis=1)
    e = jnp.exp2(s - jnp.max(s, axis=-1, keepdims=True))
    denom = jnp.sum(e, axis=-1, keepdims=True)
    e = e.astype(BF16)
    o, start = 0.0, 0
    for v in values:
        o = o + _dot(e[:, start:start + v.shape[0]], v)
        start += v.shape[0]
    o = o * (1.0 / denom)
    lane_o = lax.broadcasted_iota(jnp.int32, (m, LANES), 1)
    return jnp.where(lane_o < NA_HEAD_DIM, o[:m], o[m:])


def _stack_heads(q):
    lane = lax.broadcasted_iota(jnp.int32, q.shape, 1)
    zero = jnp.zeros_like(q)
    return jnp.concatenate([jnp.where(lane < NA_HEAD_DIM, q, zero), jnp.where(lane >= NA_HEAD_DIM, q, zero)], axis=0)


def _na_kernel(r0_ref, type_ref, q_ref, k_ref, v_ref, kc_ref, vc_ref, bias_ref, o_ref, s_ref, p_ref, *,
               kh, rows_per_step):
    n_lat = kh * GRID_W
    tiles = [(j, p) for j in range(rows_per_step) for p in range(HEAD_PAIRS)]
    window = []
    for j in range(rows_per_step):
        r = pl.program_id(1) * rows_per_step + j
        window.append((pl.multiple_of(r0_ref[r] * GRID_W, GRID_W), type_ref[r]))

    for idx, (j, p) in enumerate(tiles):
        start, rtype = window[j]
        cols = slice(p * LANES, (p + 1) * LANES)
        qq = _stack_heads(q_ref[0, j * GRID_W:(j + 1) * GRID_W, cols])
        s_ref[idx, :, :n_lat] = _dot_nt(qq, k_ref[0, pl.ds(start, n_lat), cols]) + bias_ref[rtype, p]
        s_ref[idx, :, n_lat:] = _dot_nt(qq, kc_ref[0, :, cols])

    denoms = []
    for idx in range(len(tiles)):
        s = s_ref[idx]
        e = jnp.exp2(s - jnp.max(s, axis=-1, keepdims=True))
        denoms.append(jnp.sum(e, axis=-1, keepdims=True))
        p_ref[idx] = e.astype(BF16)

    lane = lax.broadcasted_iota(jnp.int32, (GRID_W, LANES), 1)
    for idx, (j, p) in enumerate(tiles):
        start, _ = window[j]
        cols = slice(p * LANES, (p + 1) * LANES)
        o = _dot(p_ref[idx, :, :n_lat], v_ref[0, pl.ds(start, n_lat), cols]) + _dot(p_ref[idx, :, n_lat:],
                                                                                 vc_ref[0, :, cols])
        o = o * (1.0 / denoms[idx])
        o = jnp.where(lane < NA_HEAD_DIM, o[:GRID_W], o[GRID_W:])
        o_ref[0, j * GRID_W:(j + 1) * GRID_W, cols] = o.astype(o_ref.dtype)


def _neighbourhood_attention(proj, proj_c, bias_tab, r0, row_type, kh):
    b, t, _ = proj.shape
    l = proj_c.shape[1]
    rows = t // GRID_W
    w = NA_WIDTH
    rps = int(np.gcd(rows, NA_ROWS_PER_STEP))
    q_rows = rps * GRID_W
    grid_spec = pltpu.PrefetchScalarGridSpec(
        num_scalar_prefetch=2,
        grid=(b, rows // rps),
        in_specs=[
            pl.BlockSpec((1, q_rows, w), lambda bi, r, *_: (bi, r, 0)),
            pl.BlockSpec((1, t, w), lambda bi, r, *_: (bi, 0, 1)),
            pl.BlockSpec((1, t, w), lambda bi, r, *_: (bi, 0, 2)),
            pl.BlockSpec((1, l, w), lambda bi, r, *_: (bi, 0, 1)),
            pl.BlockSpec((1, l, w), lambda bi, r, *_: (bi, 0, 2)),
            pl.BlockSpec(bias_tab.shape, lambda bi, r, *_: (0, 0, 0, 0)),
        ],
        out_specs=pl.BlockSpec((1, q_rows, w), lambda bi, r, *_: (bi, r, 0)),
        scratch_shapes=[pltpu.VMEM((rps * HEAD_PAIRS, 2 * GRID_W, kh * GRID_W + l), F32),
                        pltpu.VMEM((rps * HEAD_PAIRS, 2 * GRID_W, kh * GRID_W + l), BF16)],
    )
    return pl.pallas_call(
        functools.partial(_na_kernel, kh=kh, rows_per_step=rps),
        out_shape=jax.ShapeDtypeStruct((b, t, w), BF16),
        grid_spec=grid_spec,
        compiler_params=_params(("parallel", "arbitrary")),
        name="na_attention",
    )(jnp.asarray(r0), jnp.asarray(row_type), proj, proj, proj, proj_c, proj_c, bias_tab)


def _ctx_attn_kernel(q_ref, k_ref, v_ref, o_ref):
    for p in range(HEAD_PAIRS):
        cols = slice(p * LANES, (p + 1) * LANES)
        o = _pair_attention(q_ref[0, :, cols], [k_ref[0, :, cols]], [v_ref[0, :, cols]], [None])
        o_ref[0, :, cols] = o.astype(o_ref.dtype)


def _context_attention(proj_c):
    b, l, _ = proj_c.shape
    w = NA_WIDTH
    return pl.pallas_call(
        _ctx_attn_kernel,
        out_shape=jax.ShapeDtypeStruct((b, l, w), BF16),
        grid=(b,),
        in_specs=[pl.BlockSpec((1, l, w), lambda bi, j=j: (bi, 0, j)) for j in range(3)],
        out_specs=pl.BlockSpec((1, l, w), lambda bi: (bi, 0, 0)),
        compiler_params=_params(("parallel",)),
        name="ctx_attention",
    )(proj_c, proj_c, proj_c)


def _top4(logits):
    lane = lax.broadcasted_iota(jnp.int32, logits.shape, 1)
    cur = jnp.where(lane < N_EXPERTS, logits, -jnp.inf)
    picked = jnp.zeros(logits.shape, F32)
    vals, idxs = [], []
    for _ in range(TOP_K):
        m = jnp.max(cur, axis=-1, keepdims=True)
        first = jnp.min(jnp.where(cur == m, lane, ROUTER_PAD).astype(F32), axis=-1, keepdims=True)
        idx = first.astype(jnp.int32)
        vals.append(m)
        idxs.append(idx)
        hit = lane == idx
        picked = jnp.where(hit, 1.0, picked)
        cur = jnp.where(hit, -jnp.inf, cur)
    exps = [jnp.exp(v - vals[0]) for v in vals]
    inv = 1.0 / functools.reduce(jnp.add, exps)
    ids = jnp.zeros(logits.shape, jnp.int32)
    gates = jnp.zeros(logits.shape, F32)
    for k in range(TOP_K):
        ids = jnp.where(lane == k, idxs[k], ids)
        gates = jnp.where(lane == k, exps[k] * inv, gates)
    return ids, gates, jnp.sum(picked, axis=0, keepdims=True)


def _layer_tail(y, rows, x_ref, g1_ref, n2_ref, sh2_ref, sc2_ref, wr_ref, br_ref, *rest):
    xo_ref, h2_ref, ids_ref, gates_ref = rest[-5:-1]
    x_new = x_ref[0, rows] + g1_ref[0] * y
    xo_ref[0, rows] = x_new
    h2 = _rms_mod(x_new, n2_ref[...], sh2_ref[0], sc2_ref[0])
    h2_ref[rows] = _pack_rows(h2)
    logits = _dot(h2.astype(BF16), wr_ref[...]) + br_ref[...]
    ids, gates, counts = _top4(logits)
    ids_ref[rows] = ids
    gates_ref[0, rows] = gates
    return counts


def _sub_tiles(n_rows):
    sub = min(OUT_SUB_TILE, n_rows)
    return [slice(s, s + sub) for s in range(0, n_rows, sub)]


def _tail_specs(b, t, d, tm, n_inputs_before, tokens, shared):
    n_total, offset = tokens
    n_t = t // tm
    assert offset % tm == 0
    row = lambda bi, i: (bi, i, 0)
    flat = lambda bi, i: (offset // tm + bi * n_t + i, 0)
    per_b = lambda bi, i: (bi, 0, 0)
    const = lambda bi, i: (0, 0)
    extra, aliases = (), {}
    if shared is not None:
        extra = tuple(shared)
        aliases = {n_inputs_before + 7: 1, n_inputs_before + 8: 2}
    in_specs = [
        pl.BlockSpec((1, tm, d), row),
        pl.BlockSpec((1, 1, d), per_b),
        pl.BlockSpec((1, d), const),
        pl.BlockSpec((1, 1, d), per_b),
        pl.BlockSpec((1, 1, d), per_b),
        pl.BlockSpec((d, ROUTER_PAD), const),
        pl.BlockSpec((1, ROUTER_PAD), const),
    ] + [pl.BlockSpec(memory_space=pl.ANY)] * len(extra)
    out_specs = [
        pl.BlockSpec((1, tm, d), row),
        pl.BlockSpec((tm, d // 2), flat),
        pl.BlockSpec((tm, ROUTER_PAD), flat),
        pl.BlockSpec((1, tm, ROUTER_PAD), row),
        pl.BlockSpec((1, SUBLANES, ROUTER_PAD), lambda bi, i: (bi * n_t + i, 0, 0)),
    ]
    out_shape = [
        jax.ShapeDtypeStruct((b, t, d), F32),
        jax.ShapeDtypeStruct((n_total, d // 2), jnp.int32),
        jax.ShapeDtypeStruct((n_total, ROUTER_PAD), jnp.int32),
        jax.ShapeDtypeStruct((b, t, ROUTER_PAD), F32),
        jax.ShapeDtypeStruct((b * n_t, SUBLANES, ROUTER_PAD), F32),
    ]
    return in_specs, out_specs, out_shape, extra, aliases


def _halo_fix(rolled, at_row, halo_row, present):
    n = rolled.shape[0]
    first = at_row < SUBLANES
    assert first or at_row >= n - SUBLANES
    slab = rolled[:SUBLANES] if first else rolled[n - SUBLANES:]
    sub = lax.broadcasted_iota(jnp.int32, slab.shape, 0)
    fill = jnp.where(present, halo_row, jnp.zeros_like(halo_row))
    slab = jnp.where(sub == at_row % SUBLANES, fill, slab)
    return jnp.concatenate([slab, rolled[SUBLANES:]] if first else [rolled[:n - SUBLANES], slab], axis=0)


def _even_out_kernel(oa_ref, bg_ref, cg_ref, xin_ref, cgp_ref, xinp_ref, cgn_ref, xinn_ref, cw_ref, cb_ref,
                     wa_ref, wb_ref, *tail_refs, tm):
    i = pl.program_id(1)
    has_prev = i > 0
    has_next = i < pl.num_programs(1) - 1
    u = cg_ref[0].astype(F32) * xin_ref[0].astype(F32)
    u_prev = (cgp_ref[0].astype(F32) * xinp_ref[0].astype(F32))[HALO - 1:HALO]
    u_next = (cgn_ref[0].astype(F32) * xinn_ref[0].astype(F32))[0:1]
    u_m1 = _halo_fix(pltpu.roll(u, 1, 0), 0, u_prev, has_prev)
    u_p1 = _halo_fix(pltpu.roll(u, tm - 1, 0), tm - 1, u_next, has_next)
    cw = cw_ref[...]
    conv = u_m1 * cw[0:1] + u * cw[1:2] + u_p1 * cw[2:3] + cb_ref[...]
    o_b = (bg_ref[0].astype(F32) * conv).astype(BF16)
    counts = 0.0
    for rows in _sub_tiles(tm):
        y = _dot(oa_ref[0, rows], wa_ref[...]) + _dot(o_b[rows], wb_ref[...])
        counts = counts + _layer_tail(y, rows, *tail_refs)
    tail_refs[-1][0] = jnp.broadcast_to(counts, tail_refs[-1].shape[1:])


def _even_out(o_a, proj, conv_w, conv_b, w_out, x, g1, n2, sh2, sc2, w_r, b_r, tokens, shared=None):
    b, t, d = x.shape
    w = NA_WIDTH
    tm = min(t, OUT_TILE)
    hb = tm // HALO
    n_hblocks = t // HALO
    row = lambda bi, i: (bi, i, 0)
    const = lambda bi, i: (0, 0)
    prev = lambda col: (lambda bi, i: (bi, jnp.maximum(i * hb - 1, 0), col))
    nxt = lambda col: (lambda bi, i: (bi, jnp.minimum((i + 1) * hb, n_hblocks - 1), col))
    tail_in, out_specs, out_shape, extra, aliases = _tail_specs(b, t, d, tm, 12, tokens, shared)
    in_specs = [
        pl.BlockSpec((1, tm, w), row),
        pl.BlockSpec((1, tm, w), lambda bi, i: (bi, i, 3)),
        pl.BlockSpec((1, tm, w), lambda bi, i: (bi, i, 4)),
        pl.BlockSpec((1, tm, w), lambda bi, i: (bi, i, 5)),
        pl.BlockSpec((1, HALO, w), prev(4)),
        pl.BlockSpec((1, HALO, w), prev(5)),
        pl.BlockSpec((1, HALO, w), nxt(4)),
        pl.BlockSpec((1, HALO, w), nxt(5)),
        pl.BlockSpec((SC_CONV, w), const),
        pl.BlockSpec((1, w), const),
        pl.BlockSpec((w, d), const),
        pl.BlockSpec((w, d), const),
    ] + tail_in
    return pl.pallas_call(
        functools.partial(_even_out_kernel, tm=tm),
        out_shape=out_shape,
        grid=(b, t // tm),
        in_specs=in_specs,
        out_specs=out_specs,
        input_output_aliases=aliases,
        compiler_params=_params(("parallel", "parallel")),
        name="even_out",
    )(o_a, proj, proj, proj, proj, proj, proj, proj, conv_w, conv_b.reshape(1, w),
      w_out[:w], w_out[w:], x, g1, n2, sh2, sc2, w_r, b_r, *extra)


def _log_sigmoid(x):
    return jnp.minimum(x, 0.0) - jnp.log1p(jnp.exp(-jnp.abs(x)))


def _sigmoid(x):
    return 0.5 + 0.5 * jnp.tanh(0.5 * x)


def _lru_sub_tile(ucb_time, state, perm_ref, w_ref, ba_ref, bx_ref, lam_ref, want_hidden, reverse):
    sub, width = ucb_time.shape
    blk = width // LRU_BLOCKS
    steps = sub // SUBLANES
    uc = _dot(perm_ref[0], ucb_time)
    ucb = uc.astype(BF16)
    za, zx = [], []
    for h in range(LRU_BLOCKS):
        z = _dot(ucb[:, h * blk:(h + 1) * blk], w_ref[h])
        za.append(z[:, :blk])
        zx.append(z[:, blk:])
    r = _sigmoid(jnp.concatenate(za, axis=1) + ba_ref[...])
    gate_i = _sigmoid(jnp.concatenate(zx, axis=1) + bx_ref[...])
    log_a = (LRU_C * _log_sigmoid(lam_ref[...])) * r
    a = jnp.exp(log_a)
    th = jnp.tanh(log_a)
    num = -2.0 * th
    mult = jnp.where(num > 0.0, num * lax.rsqrt(num * (1.0 - th)), 0.0)
    bcoef = mult * gate_i * uc

    grp = lambda v, j: v[j * SUBLANES:(j + 1) * SUBLANES]
    prods, local = [grp(a, 0)], [grp(bcoef, 0)]
    for j in range(1, steps):
        aj = grp(a, j)
        local.append(aj * local[-1] + grp(bcoef, j))
        prods.append(aj * prods[-1])

    carry_in = [None] * SUBLANES
    for s in (range(SUBLANES - 1, -1, -1) if reverse else range(SUBLANES)):
        carry_in[s] = state
        state = prods[-1][s:s + 1] * state + local[-1][s:s + 1]
    if not want_hidden:
        return state, None
    start = jnp.concatenate(carry_in, axis=0)
    hidden = jnp.concatenate([prods[j] * start + local[j] for j in range(steps)], axis=0)
    return state, _dot(perm_ref[1], hidden.astype(BF16)).astype(BF16)


def _lru_tile(u_ref, perm_ref, w_ref, ba_ref, bx_ref, lam_ref, carry_ref, o_ref, *, reverse):
    sub = perm_ref.shape[1]
    n_sub = u_ref.shape[1] // sub
    state = carry_ref[0:1, :]
    for k in (range(n_sub - 1, -1, -1) if reverse else range(n_sub)):
        rows = slice(k * sub, (k + 1) * sub)
        state, hidden = _lru_sub_tile(u_ref[0, rows], state, perm_ref, w_ref, ba_ref, bx_ref, lam_ref,
                                      o_ref is not None, reverse)
        if o_ref is not None:
            o_ref[0, rows] = hidden
    carry_ref[...] = jnp.broadcast_to(state, carry_ref.shape)


def _lru_kernel(uc_ref, ul_ref, perm_ref, w_ref, ba_ref, bx_ref, lam_ref, o_ref, carry_ref, *, n_ctx_tiles, reverse):
    j = pl.program_id(1)
    shared = (perm_ref, w_ref, ba_ref, bx_ref, lam_ref, carry_ref)

    @pl.when(j == 0)
    def _():
        carry_ref[...] = jnp.zeros_like(carry_ref)

    @pl.when(j < n_ctx_tiles)
    def _():
        _lru_tile(uc_ref, *shared, None, reverse=reverse)

    @pl.when(j >= n_ctx_tiles)
    def _():
        _lru_tile(ul_ref, *shared, o_ref, reverse=reverse)


def _lru_scan(proj, u_ctx, w_cat, ba, bx, lam, reverse):
    b, t, _ = proj.shape
    l, width = u_ctx.shape[1], u_ctx.shape[2]
    sub = min(256, l, t)
    tc = sub * min(LRU_SUB_TILES, l // sub)
    tl = sub * min(LRU_SUB_TILES, t // sub)
    n_c, n_l = l // tc, t // tl

    def pos_of(step, n):
        step = jnp.clip(step, 0, n - 1)
        return (n - 1 - step) if reverse else step

    def tile_map(off, n, col):
        return lambda bi, j: (bi, pos_of(j - off, n), col)

    const2 = lambda bi, j: (0, 0)
    in_specs = [
        pl.BlockSpec((1, tc, width), tile_map(0, n_c, 0)),
        pl.BlockSpec((1, tl, width), tile_map(n_c, n_l, 1)),
        pl.BlockSpec((2, sub, sub), lambda bi, j: (0, 0, 0)),
        pl.BlockSpec(w_cat.shape, lambda bi, j: (0, 0, 0)),
        pl.BlockSpec((1, width), const2),
        pl.BlockSpec((1, width), const2),
        pl.BlockSpec((1, width), const2),
    ]
    steps = sub // SUBLANES
    step, block = np.divmod(np.arange(sub), SUBLANES)
    time_of_row = block * steps + (steps - 1 - step if reverse else step)
    perm = (time_of_row[:, None] == np.arange(sub)[None]).astype(np.float32)
    perms = jnp.asarray(np.stack([perm, perm.T]), BF16)
    kern = functools.partial(_lru_kernel, n_ctx_tiles=n_c, reverse=reverse)
    return pl.pallas_call(
        kern,
        out_shape=jax.ShapeDtypeStruct((b, t, width), BF16),
        grid=(b, n_c + n_l),
        in_specs=in_specs,
        out_specs=pl.BlockSpec((1, tl, width), tile_map(n_c, n_l, 0)),
        scratch_shapes=[pltpu.VMEM((SUBLANES, width), F32)],
        compiler_params=_params(("parallel", "arbitrary")),
        name="lru_scan_bwd" if reverse else "lru_scan_fwd",
    )(u_ctx, proj, perms, w_cat, ba.reshape(1, width), bx.reshape(1, width), lam.reshape(1, width))


def _odd_out_kernel(hf_ref, hb_ref, gate_ref, w_ref, *tail_refs):
    counts = 0.0
    for rows in _sub_tiles(hf_ref.shape[1]):
        hsum = hf_ref[0, rows].astype(F32) + hb_ref[0, rows].astype(F32)
        z = hsum * jax.nn.gelu(gate_ref[0, rows].astype(F32), approximate=True)
        y = _dot(z.astype(BF16), w_ref[...])
        counts = counts + _layer_tail(y, rows, *tail_refs)
    tail_refs[-1][0] = jnp.broadcast_to(counts, tail_refs[-1].shape[1:])


def _odd_out(h_f, h_b, proj, w_out, x, g1, n2, sh2, sc2, w_r, b_r):
    b, t, d = x.shape
    width = h_f.shape[-1]
    tm = min(t, OUT_TILE)
    row = lambda bi, i: (bi, i, 0)
    tail_in, out_specs, out_shape, _, _ = _tail_specs(b, t, d, tm, 4, (b * t, 0), None)
    in_specs = [
        pl.BlockSpec((1, tm, width), row),
        pl.BlockSpec((1, tm, width), row),
        pl.BlockSpec((1, tm, width), row),
        pl.BlockSpec((width, d), lambda bi, i: (0, 0)),
    ] + tail_in
    return pl.pallas_call(
        _odd_out_kernel,
        out_shape=out_shape,
        grid=(b, t // tm),
        in_specs=in_specs,
        out_specs=out_specs,
        compiler_params=_params(("parallel", "parallel")),
        name="odd_out",
    )(h_f, h_b, proj, w_out, x, g1, n2, sh2, sc2, w_r, b_r)


def _expert_kernel(be_ref, next_ref, rows_ref, nb_ref, x_ref, wg_hbm, bg_ref, wu_hbm, bu_ref, wd_hbm, bd_ref,
                   o_ref, w_f32, wg_bf, wu_bf, wd_bf, h_bf, sem, *, layer, chunk):
    i = pl.program_id(0)
    expert = be_ref[i]
    used = i < nb_ref[0]

    def weight_copies(e):
        return [pltpu.make_async_copy(src.at[layer, e], w_f32.at[j], sem.at[j])
                for j, src in enumerate((wg_hbm, wu_hbm, wd_hbm))]

    @pl.when(i == 0)
    def _():
        for cp in weight_copies(expert):
            cp.start()

    @pl.when(used & ((i == 0) | (expert != be_ref[jnp.maximum(i - 1, 0)])))
    def _():
        for cp in weight_copies(expert):
            cp.wait()
        wg_bf[...] = w_f32[0].astype(BF16)
        wu_bf[...] = w_f32[1].astype(BF16)
        wd_bf[...] = w_f32[2].astype(BF16)

        @pl.when(next_ref[i] >= 0)
        def _():
            for cp in weight_copies(next_ref[i]):
                cp.start()

    sub = h_bf.shape[0]
    n_sub = x_ref.shape[0] // sub
    n_rows = jnp.where(used, rows_ref[i], 0)
    n_live = (n_rows + sub - 1) // sub

    def live_sub_block(s, carry):
        rows = pl.ds(pl.multiple_of(s * sub, sub), sub)
        xp = x_ref[rows]
        row = lax.broadcasted_iota(jnp.int32, xp.shape, 0) + s * sub
        xp = jnp.where(row < n_rows, xp, 0)
        x = jnp.concatenate(_unpack_rows(xp), axis=1).astype(BF16)
        for c in range(wg_bf.shape[1] // chunk):
            cs = slice(c * chunk, (c + 1) * chunk)
            g = jnp.minimum(_dot(x, wg_bf[:, cs]) + bg_ref[0, 0, :, cs], SWIGLU_LIMIT)
            u = jnp.clip(_dot(x, wu_bf[:, cs]) + bu_ref[0, 0, :, cs], -SWIGLU_LIMIT, SWIGLU_LIMIT)
            h_bf[:, cs] = (g * jax.nn.sigmoid(SWIGLU_ALPHA * g) * (u + 1.0)).astype(BF16)
        o_ref[rows] = _pack_rows(_dot(h_bf[...], wd_bf[...]) + bd_ref[0, 0])
        return carry

    def empty_sub_block(s, carry):
        o_ref[pl.ds(pl.multiple_of(s * sub, sub), sub)] = jnp.zeros((sub, o_ref.shape[1]), o_ref.dtype)
        return carry

    lax.fori_loop(0, n_live, live_sub_block, 0)
    lax.fori_loop(n_live, n_sub, empty_sub_block, 0)


def _experts(xb, block_e, next_e, block_rows, n_used, layer, wg, bg, wu, bu, wd, bd):
    n_slots = xb.shape[0]
    depth, n_e, d, d_exp = wg.shape
    assert d == d_exp
    tm = EXPERT_TILE
    n_blocks = n_slots // tm
    xmap = lambda i, be, ne, br, nb: (jnp.minimum(i, nb[0] - 1), 0)
    bmap = lambda i, be, ne, br, nb: (layer, be[i], 0, 0)
    hbm = pl.BlockSpec(memory_space=pl.ANY)
    grid_spec = pltpu.PrefetchScalarGridSpec(
        num_scalar_prefetch=4,
        grid=(n_blocks,),
        in_specs=[
            pl.BlockSpec((tm, d // 2), xmap),
            hbm,
            pl.BlockSpec((1, 1, 1, d_exp), bmap),
            hbm,
            pl.BlockSpec((1, 1, 1, d_exp), bmap),
            hbm,
            pl.BlockSpec((1, 1, 1, d), bmap),
        ],
        out_specs=pl.BlockSpec((tm, d // 2), lambda i, be, ne, br, nb: (i, 0)),
        scratch_shapes=[pltpu.VMEM((3, d, d_exp), F32), pltpu.VMEM((d, d_exp), BF16), pltpu.VMEM((d, d_exp), BF16),
                        pltpu.VMEM((d_exp, d), BF16), pltpu.VMEM((EXPERT_SUB, d_exp), BF16),
                        pltpu.SemaphoreType.DMA((3,))],
    )
    return pl.pallas_call(
        functools.partial(_expert_kernel, layer=layer, chunk=256),
        out_shape=jax.ShapeDtypeStruct((n_slots, d // 2), jnp.int32),
        grid_spec=grid_spec,
        compiler_params=_params(("arbitrary",)),
        name="experts",
    )(block_e, next_e, block_rows, n_used, xb, wg, bg.reshape(depth, n_e, 1, d_exp), wu,
      bu.reshape(depth, n_e, 1, d_exp), wd, bd.reshape(depth, n_e, 1, d))


def _combine_kernel(x_ref, g2_ref, gates_ref, y_ref, o_ref):
    o_ref[0] = _moe_residual(x_ref[0], g2_ref[0], gates_ref[0], y_ref[...])


def _combine(x, g2, gates, y_sel, tok_offset):
    b, t, d = x.shape
    tm = int(np.gcd(min(t, ROW_TILE), tok_offset)) if tok_offset else min(t, ROW_TILE)
    n_t = t // tm
    off = tok_offset // tm
    row = lambda bi, i: (bi, i, 0)
    return pl.pallas_call(
        _combine_kernel,
        out_shape=jax.ShapeDtypeStruct((b, t, d), F32),
        grid=(b, n_t),
        in_specs=[
            pl.BlockSpec((1, tm, d), row),
            pl.BlockSpec((1, 1, d), lambda bi, i: (bi, 0, 0)),
            pl.BlockSpec((1, tm, ROUTER_PAD), row),
            pl.BlockSpec((TOP_K, tm, d // 2), lambda bi, i: (0, off + bi * n_t + i, 0)),
        ],
        out_specs=pl.BlockSpec((1, tm, d), row),
        input_output_aliases={0: 0},
        compiler_params=_params(("parallel", "parallel")),
        name="moe_combine",
    )(x, g2, gates, y_sel)


def _route_kernel(ids_ref, upper_ref, starts_ref, dest_ref, run_ref):
    i = pl.program_id(0)
    tm = ids_ref.shape[0]
    ids_t = ids_ref[...].T
    expert = lax.broadcasted_iota(jnp.int32, (N_EXPERTS, tm), 0)
    chosen = [ids_t[k:k + 1, :] == expert for k in range(TOP_K)]
    picks = functools.reduce(jnp.add, [c.astype(F32) for c in chosen])

    @pl.when(i == 0)
    def _():
        run_ref[...] = starts_ref[...]

    before = _dot(picks.astype(BF16), upper_ref[...])
    slot = before + run_ref[:, 0:1]
    rows = [jnp.sum(jnp.where(c, slot, 0.0), axis=0, keepdims=True) for c in chosen]
    rows += [jnp.zeros_like(rows[0])] * (dest_ref.shape[0] - TOP_K)
    dest_ref[...] = jnp.concatenate(rows, axis=0).astype(jnp.int32)
    run_ref[...] += jnp.sum(picks, axis=1, keepdims=True)


def _route(ids, region_starts):
    n = ids.shape[0]
    tm = int(np.gcd(n, ROUTE_TILE))
    upper = jnp.asarray(np.triu(np.ones((tm, tm), np.float32), 1), BF16)
    starts = jnp.broadcast_to(region_starts.astype(F32)[:, None], (N_EXPERTS, LANES))
    return pl.pallas_call(
        _route_kernel,
        out_shape=jax.ShapeDtypeStruct((SUBLANES, n), jnp.int32),
        grid=(n // tm,),
        in_specs=[pl.BlockSpec((tm, ROUTER_PAD), lambda i: (i, 0)),
                  pl.BlockSpec((tm, tm), lambda i: (0, 0)),
                  pl.BlockSpec((N_EXPERTS, LANES), lambda i: (0, 0))],
        out_specs=pl.BlockSpec((SUBLANES, tm), lambda i: (0, i)),
        scratch_shapes=[pltpu.VMEM((N_EXPERTS, LANES), F32)],
        compiler_params=_params(("arbitrary",)),
        name="route",
    )(ids, upper, starts)


def _row_gather(table, idx):
    info = plsc.get_sparse_core_info()
    n_cores, n_workers = info.num_cores, info.num_cores * info.num_subcores
    n_rows, width = idx.shape[0], table.shape[1]
    chunk_rows = SC_GATHER_ROWS
    per_worker = n_rows // n_workers
    n_chunks = per_worker // chunk_rows
    assert per_worker * n_workers == n_rows and n_chunks * chunk_rows == per_worker and n_chunks % 2 == 0
    mesh = plsc.VectorSubcoreMesh(core_axis_name="c", subcore_axis_name="s")

    @functools.partial(
        pl.kernel, mesh=mesh,
        out_type=jax.ShapeDtypeStruct((n_rows, width), table.dtype),
        scratch_types=[pltpu.VMEM((n_chunks, chunk_rows), jnp.int32), pltpu.VMEM((2, chunk_rows, width), table.dtype),
                       pltpu.SemaphoreType.DMA((2,)), pltpu.SemaphoreType.DMA((2,))],
    )
    def gather_kernel(table_hbm, idx_hbm, out_hbm, idx_v, rows_v, gather_sem, write_sem):
        worker = lax.axis_index("s") * n_cores + lax.axis_index("c")
        pltpu.sync_copy(idx_hbm.at[worker], idx_v)

        def gather(c, slot):
            return pltpu.make_async_copy(table_hbm.at[idx_v.at[c]], rows_v.at[slot], gather_sem.at[slot])

        def write(c, slot):
            first_row = worker * per_worker + c * chunk_rows
            return pltpu.make_async_copy(rows_v.at[slot], out_hbm.at[pl.ds(first_row, chunk_rows)], write_sem.at[slot])

        gather(0, 0).start()

        @pl.loop(0, n_chunks, step=2)
        def _(c0):
            for slot in (0, 1):
                c = c0 + slot
                gather(c, slot).wait()

                @pl.when(c >= 1)
                def _():
                    write(c - 1, 1 - slot).wait()

                @pl.when(c + 1 < n_chunks)
                def _():
                    gather(c + 1, 1 - slot).start()

                write(c, slot).start()

        write(n_chunks - 1, 1).wait()

    return gather_kernel(table, idx.reshape(n_workers, n_chunks, chunk_rows))


def _row_scatter(rows, dest, n_out):
    info = plsc.get_sparse_core_info()
    n_cores, n_workers = info.num_cores, info.num_cores * info.num_subcores
    n_choices, n_rows = dest.shape
    width = rows.shape[1]
    chunk_rows = SC_SCATTER_ROWS
    per_worker = n_rows // (n_workers * chunk_rows)
    assert per_worker * n_workers * chunk_rows == n_rows == rows.shape[0]
    idx = dest.reshape(n_choices, n_workers, per_worker, chunk_rows)
    mesh = plsc.VectorSubcoreMesh(core_axis_name="c", subcore_axis_name="s")

    @functools.partial(
        pl.kernel, mesh=mesh,
        out_type=jax.ShapeDtypeStruct((n_out, width), rows.dtype),
        scratch_types=[pltpu.VMEM((n_choices, per_worker, chunk_rows), jnp.int32),
                       pltpu.VMEM((chunk_rows, width), rows.dtype), pltpu.SemaphoreType.DMA],
    )
    def scatter_kernel(rows_hbm, idx_hbm, out_hbm, idx_v, rows_v, sem):
        worker = lax.axis_index("s") * n_cores + lax.axis_index("c")
        for k in range(n_choices):
            pltpu.sync_copy(idx_hbm.at[k, worker], idx_v.at[k])

        @pl.loop(0, per_worker)
        def _(c):
            first_row = (worker * per_worker + c) * chunk_rows
            pltpu.sync_copy(rows_hbm.at[pl.ds(first_row, chunk_rows)], rows_v)
            copies = [pltpu.async_copy(rows_v, out_hbm.at[idx_v.at[k, c]], sem) for k in range(n_choices)]
            for cp in copies:
                cp.wait()

    return scatter_kernel(rows, idx)


def _expert_counts(step_picks):
    return jnp.sum(step_picks[:, 0, :N_EXPERTS], axis=0).astype(jnp.int32)


def _moe(h2, ids, counts, layer, w_exp):
    n = h2.shape[0]
    nk = n * TOP_K
    tm = EXPERT_TILE
    padded = (counts + tm - 1) // tm * tm
    pends = jnp.cumsum(padded)
    dest = _route(ids, pends - padded)[:TOP_K]
    n_blocks = -(-nk // tm) + N_EXPERTS
    n_slots = n_blocks * tm
    block_start = jnp.arange(n_blocks, dtype=jnp.int32) * tm
    block_e = jnp.minimum(jnp.sum((pends[None] <= block_start[:, None]).astype(jnp.int32), axis=1), N_EXPERTS - 1)
    is_e = (block_e[:, None] == jnp.arange(N_EXPERTS, dtype=jnp.int32)[None]).astype(jnp.int32)
    block_rows = jnp.clip(jnp.sum(is_e * (pends - padded + counts)[None], axis=1) - block_start, 0, tm)
    n_used = (pends[-1] // tm).astype(jnp.int32).reshape(1)
    e_ids = jnp.arange(N_EXPERTS, dtype=jnp.int32)
    later = (e_ids[None] > e_ids[:, None]) & (counts[None] > 0)
    next_nonempty = jnp.min(jnp.where(later, e_ids[None], N_EXPERTS), axis=1)
    next_nonempty = jnp.where(next_nonempty == N_EXPERTS, -1, next_nonempty)
    next_e = jnp.sum(is_e * next_nonempty[None], axis=1)
    xb = _row_scatter(h2, dest, n_slots)
    return _experts(xb, block_e, next_e, block_rows, n_used, layer, *w_exp), dest


def _gather_choices(yb, dest, lo, hi):
    return _row_gather(yb, dest[:, lo:hi].reshape(-1)).reshape(TOP_K, hi - lo, -1)


def _mod_parts(mod_l, b):
    d = mod_l.shape[-1] // 6
    lat = [mod_l[:b, k * d:(k + 1) * d].reshape(b, 1, d) for k in range(6)]
    ctx = [jnp.broadcast_to(mod_l[b, k * d:(k + 1) * d].reshape(1, 1, d), (b, 1, d)) for k in range(6)]
    return lat, ctx


def kernel(x, c, ctx, c_ctx, ada_w, ada_b, norm1_g, norm2_g, ev_w_in, ev_w_out, ev_q_gain, ev_k_gain, ev_rpb, ev_conv_w, ev_conv_b, od_w_in, od_w_out, od_conv_w, od_conv_b, od_fwd_wa, od_fwd_ba, od_fwd_wx, od_fwd_bx, od_fwd_lam, od_bwd_wa, od_bwd_ba, od_bwd_wx, od_bwd_bx, od_bwd_lam, router_w, router_b, exp_w_gate, exp_b_gate, exp_w_up, exp_b_up, exp_w_down, exp_b_down):
    b, t, d = x.shape
    l = ctx.shape[1]
    assert ada_w.shape[0] == DEPTH == 2 and t % GRID_W == 0 and t // GRID_W >= WIN_H

    n_rows_c = -(-(b + 1) // SUBLANES) * SUBLANES
    cvec = jnp.zeros((n_rows_c, d), F32).at[:b].set(c).at[b].set(c_ctx)
    mod = _ada_mod(cvec, ada_w, ada_b)

    def router(layer):
        w_r = jnp.zeros((d, ROUTER_PAD), F32).at[:, :N_EXPERTS].set(router_w[layer]).astype(BF16)
        b_r = jnp.zeros((1, ROUTER_PAD), F32).at[0, :N_EXPERTS].set(router_b[layer])
        return w_r, b_r

    w_exp = (exp_w_gate, exp_b_gate, exp_w_up, exp_b_up, exp_w_down, exp_b_down)

    (sh1, sc1, g1, sh2, sc2, g2), (csh1, csc1, cg1, csh2, csc2, cg2) = _mod_parts(mod[0], b)
    n1 = norm1_g[0].reshape(1, d)
    n2 = norm2_g[0].reshape(1, d)
    w_in = ev_w_in[0].astype(BF16)
    w_out = ev_w_out[0].astype(BF16)
    q_scale = NA_HEAD_DIM ** -0.5 * LOG2_E
    head_gain = jnp.stack([jnp.tile(ev_q_gain[0] * q_scale, NA_HEADS), jnp.tile(ev_k_gain[0], NA_HEADS)])
    head_gain = head_gain.reshape(2, 1, NA_WIDTH).astype(F32)
    n_tok = b * (l + t)
    proj, *fresh = _inproj(x, n1, sh1, sc1, w_in, head_gain, ((n_tok, d // 2), (n_tok, ROUTER_PAD)))
    proj_c, = _inproj(ctx, n1, csh1, csc1, w_in, head_gain)
    kh, r0, row_type, patterns = _na_tables(t // GRID_W)
    bias_tab = _na_bias_table(ev_rpb[0], patterns)
    o_a = _neighbourhood_attention(proj, proj_c, bias_tab, r0, row_type, kh)
    oc_a = _context_attention(proj_c)
    w_r, b_r = router(0)
    c1, tokens, ids_all, gates_c, picks_c = _even_out(oc_a, proj_c, ev_conv_w[0], ev_conv_b[0], w_out, ctx, cg1, n2,
                                                      csh2, csc2, w_r, b_r, (n_tok, 0), fresh)
    x1, tokens, ids_all, gates, picks = _even_out(o_a, proj, ev_conv_w[0], ev_conv_b[0], w_out, x, g1, n2, sh2, sc2,
                                                  w_r, b_r, (n_tok, b * l), (tokens, ids_all))
    yb, dest = _moe(tokens, ids_all, _expert_counts(picks_c) + _expert_counts(picks), 0, w_exp)
    y_sel = _gather_choices(yb, dest, 0, b * (l + t))
    g2_prev, cg2_prev = g2, cg2

    (sh1, sc1, g1, sh2, sc2, g2), (csh1, csc1, _, _, _, _) = _mod_parts(mod[1], b)
    n1 = norm1_g[1].reshape(1, d)
    n2 = norm2_g[1].reshape(1, d)
    w_in = od_w_in[0].astype(BF16)
    width = w_in.shape[1] // 2
    x, proj = _combine_inproj_conv(x1, g2_prev, gates, y_sel, b * l, n1, sh1, sc1, w_in, od_conv_w[0], od_conv_b[0],
                                   width)
    _, u_ctx = _combine_inproj_conv(c1, cg2_prev, gates_c, y_sel, 0, n1, csh1, csc1, w_in[:, width:], od_conv_w[0],
                                    od_conv_b[0], 0)
    h_dir = []
    for reverse, (wa, ba, wx, bx, lam) in ((False, (od_fwd_wa, od_fwd_ba, od_fwd_wx, od_fwd_bx, od_fwd_lam)),
                                           (True, (od_bwd_wa, od_bwd_ba, od_bwd_wx, od_bwd_bx, od_bwd_lam))):
        w_cat = jnp.concatenate([wa[0], wx[0]], axis=-1).astype(BF16)
        h_dir.append(_lru_scan(proj, u_ctx, w_cat, ba[0], bx[0], lam[0], reverse))
    w_r, b_r = router(1)
    x1, h2, ids, gates, picks = _odd_out(h_dir[0], h_dir[1], proj, od_w_out[0].astype(BF16), x, g1, n2, sh2, sc2,
                                         w_r, b_r)
    yb, dest = _moe(h2, ids, _expert_counts(picks), 1, w_exp)
    return _combine(x1, g2, gates, _gather_choices(yb, dest, 0, b * t), 0)
```

```python
import functools

import numpy as np
import jax
import jax.numpy as jnp
from jax import lax
from jax.experimental import pallas as pl
from jax.experimental.pallas import tpu as pltpu
from jax.experimental.pallas import tpu_sc as plsc

DEPTH = 2
GRID_W = 64
EPS = 1e-6
NEG_INF = -1e30
LOG2_E = 1.4426950408889634
NA_HEADS = 8
NA_HEAD_DIM = 64
NA_WIDTH = NA_HEADS * NA_HEAD_DIM
HEAD_PAIRS = NA_HEADS // 2
WIN_H = 8
WIN_W = 16
SC_CONV = 3
LRU_BLOCKS = 4
LRU_CONV = 4
LRU_C = 8.0
N_EXPERTS = 32
TOP_K = 4
SWIGLU_LIMIT = 7.0
SWIGLU_ALPHA = 1.702

LANES = 128
SUBLANES = 8
MXU_DEPTH = 256
HALO = 16
ROUTER_PAD = LANES
ADA_COL_TILE = 1536
ROW_TILE = 1024
NA_ROWS_PER_STEP = 8
LRU_SUB_TILES = 4
OUT_TILE = 1024
OUT_SUB_TILE = 512
ROUTE_TILE = 2048
EXPERT_TILE = 2048
EXPERT_SUB = 512
SC_GATHER_ROWS = 64
SC_SCATTER_ROWS = 128
VMEM_LIMIT = 56 * 1024 * 1024

F32 = jnp.float32
BF16 = jnp.bfloat16


def _params(sem, vmem=VMEM_LIMIT):
    return pltpu.CompilerParams(dimension_semantics=sem, vmem_limit_bytes=vmem)


def _dot(a, b):
    return jnp.dot(a, b, preferred_element_type=F32)


def _dot_nt(a, b):
    return lax.dot_general(a, b, (((1,), (1,)), ((), ())), preferred_element_type=F32)


def _pack_rows(v):
    w = v.shape[-1] // 2
    lo = lax.bitcast_convert_type(v[:, :w].astype(BF16).astype(F32), jnp.int32)
    hi = lax.bitcast_convert_type(v[:, w:].astype(BF16).astype(F32), jnp.int32)
    return lax.shift_right_logical(lo, 16) | (hi & jnp.int32(-65536))


def _unpack_rows(p):
    lo = lax.bitcast_convert_type(lax.shift_left(p, 16), F32)
    hi = lax.bitcast_convert_type(p & jnp.int32(-65536), F32)
    return lo, hi


def _rms_mod(x, g, shift, scale):
    ms = jnp.mean(x * x, axis=-1, keepdims=True)
    return (x * lax.rsqrt(ms + EPS)) * (g * (1.0 + scale)) + shift


def _ada_kernel(c_ref, w_ref, b_ref, o_ref):
    c = c_ref[...]
    s = (c * jax.nn.sigmoid(c)).astype(BF16)
    o_ref[0] = _dot(s, w_ref[0].astype(BF16)) + b_ref[0]


def _ada_mod(cvec, ada_w, ada_b):
    depth, d, n = ada_w.shape
    r = cvec.shape[0]
    tn = ADA_COL_TILE
    return pl.pallas_call(
        _ada_kernel,
        out_shape=jax.ShapeDtypeStruct((depth, r, n), F32),
        grid=(depth, n // tn),
        in_specs=[
            pl.BlockSpec((r, d), lambda l, j: (0, 0)),
            pl.BlockSpec((1, d, tn), lambda l, j: (l, 0, j)),
            pl.BlockSpec((1, 1, tn), lambda l, j: (l, 0, j)),
        ],
        out_specs=pl.BlockSpec((1, r, tn), lambda l, j: (l, 0, j)),
        compiler_params=_params(("parallel", "parallel")),
        name="ada_mod",
    )(cvec, ada_w, ada_b.reshape(depth, 1, n))


def _inproj_kernel(x_ref, g_ref, sh_ref, sc_ref, w_ref, hg_ref, ones_ref, o_ref, *zero_refs, n_tiles, tn,
                   n_headnorm):
    for z_ref in zero_refs:
        z_ref[...] = jnp.zeros_like(z_ref)
    h = _rms_mod(x_ref[0], g_ref[...], sh_ref[0], sc_ref[0]).astype(BF16)
    for j in range(n_tiles):
        y = _dot(h, w_ref[:, j * tn:(j + 1) * tn])
        if j < n_headnorm:
            ysq = (y * y).astype(BF16)
            kw = ones_ref.shape[0]
            ms = jnp.concatenate([_dot(ysq[:, c:c + kw], ones_ref[...]) for c in range(0, tn, kw)], axis=1)
            y = y * lax.rsqrt(ms * (1.0 / NA_HEAD_DIM) + EPS) * hg_ref[j]
        o_ref[0, :, j * tn:(j + 1) * tn] = y.astype(o_ref.dtype)


def _inproj(x, g, shift, scale, w, head_gain, zero_shapes=()):
    b, t, d = x.shape
    n = w.shape[1]
    tn = NA_WIDTH
    tm = min(t, ROW_TILE)
    n_t = t // tm
    n_headnorm = head_gain.shape[0]
    hid = np.arange(MXU_DEPTH) // NA_HEAD_DIM
    ones_bd = jnp.asarray((hid[:, None] == hid[None, :]), BF16)
    kern = functools.partial(_inproj_kernel, n_tiles=n // tn, tn=tn, n_headnorm=n_headnorm)
    steps = b * n_t
    assert all(rows % (steps * SUBLANES) == 0 for rows, _ in zero_shapes)
    zero_specs = [pl.BlockSpec((rows // steps, cols), lambda bi, i: (bi * n_t + i, 0)) for rows, cols in zero_shapes]
    zero_out = [jax.ShapeDtypeStruct(shape, jnp.int32) for shape in zero_shapes]
    return pl.pallas_call(
        kern,
        out_shape=[jax.ShapeDtypeStruct((b, t, n), BF16)] + zero_out,
        grid=(b, n_t),
        in_specs=[
            pl.BlockSpec((1, tm, d), lambda bi, i: (bi, i, 0)),
            pl.BlockSpec((1, d), lambda bi, i: (0, 0)),
            pl.BlockSpec((1, 1, d), lambda bi, i: (bi, 0, 0)),
            pl.BlockSpec((1, 1, d), lambda bi, i: (bi, 0, 0)),
            pl.BlockSpec((d, n), lambda bi, i: (0, 0)),
            pl.BlockSpec(head_gain.shape, lambda bi, i: (0, 0, 0)),
            pl.BlockSpec((MXU_DEPTH, MXU_DEPTH), lambda bi, i: (0, 0)),
        ],
        out_specs=[pl.BlockSpec((1, tm, n), lambda bi, i: (bi, i, 0))] + zero_specs,
        compiler_params=_params(("parallel", "parallel")),
        name="inproj",
    )(x, g, shift, scale, w, head_gain, ones_bd)


def _moe_residual(x, g2, gates, y):
    acc_lo = acc_hi = 0.0
    for k in range(TOP_K):
        lo, hi = _unpack_rows(y[k])
        acc_lo = acc_lo + gates[:, k:k + 1] * lo
        acc_hi = acc_hi + gates[:, k:k + 1] * hi
    return x + g2 * jnp.concatenate([acc_lo, acc_hi], axis=1)


def _combine_inproj_conv_kernel(x_ref, xp_ref, xn_ref, gt_ref, gtp_ref, gtn_ref, y_ref, yp_ref, yn_ref, g2_ref,
                                g_ref, sh_ref, sc_ref, w_ref, cw_ref, cb_ref, xo_ref, o_ref, *,
                                n_tiles, n_plain, tn, tm, sub):
    i = pl.program_id(1)
    edge = SUBLANES
    g2 = g2_ref[0]
    x_tile = _moe_residual(x_ref[0], g2, gt_ref[0], y_ref[...])
    xo_ref[0] = x_tile
    x_ext = jnp.concatenate([_moe_residual(xp_ref[0], g2, gtp_ref[0], yp_ref[...]), x_tile,
                             _moe_residual(xn_ref[0], g2, gtn_ref[0], yn_ref[...])], axis=0)
    h = _rms_mod(x_ext, g_ref[...], sh_ref[0], sc_ref[0]).astype(BF16)
    first_step = i == 0
    last_step = i == pl.num_programs(1) - 1
    left = (LRU_CONV - 1) // 2
    n_sub = tm // sub
    for s in range(n_sub):
        h_s = h[s * sub:(s + 1) * sub + 2 * edge]
        rows = slice(s * sub, (s + 1) * sub)
        keep_prev = jnp.where(first_step, 0.0, 1.0) if s == 0 else 1.0
        keep_next = jnp.where(last_step, 0.0, 1.0) if s == n_sub - 1 else 1.0
        for j in range(n_tiles):
            y = _dot(h_s, w_ref[:, j * tn:(j + 1) * tn])
            if j < n_plain:
                o_ref[0, rows, j * tn:(j + 1) * tn] = y[edge:edge + sub].astype(o_ref.dtype)
                continue
            u = jnp.concatenate([y[:edge] * keep_prev, y[edge:edge + sub], y[edge + sub:] * keep_next], axis=0)
            cw = cw_ref[:, (j - n_plain) * tn:(j - n_plain + 1) * tn]
            uc = cb_ref[:, (j - n_plain) * tn:(j - n_plain + 1) * tn]
            n_ext = u.shape[0]
            for k in range(LRU_CONV):
                shifted = u if k == left else pltpu.roll(u, (left - k) % n_ext, 0)
                uc = uc + shifted[edge:edge + sub] * cw[k:k + 1]
            o_ref[0, rows, j * tn:(j + 1) * tn] = uc.astype(o_ref.dtype)


def _combine_inproj_conv(x1, g2, gates, y_sel, tok_offset, g, shift, scale, w, conv_w, conv_b, n_plain_cols):
    b, t, d = x1.shape
    n = w.shape[1]
    tn = NA_WIDTH
    tm = min(t, OUT_TILE)
    hb = tm // SUBLANES
    n_hblocks = t // SUBLANES
    y_hblocks = y_sel.shape[1] // SUBLANES
    assert tok_offset % tm == 0
    kern = functools.partial(_combine_inproj_conv_kernel, n_tiles=n // tn, n_plain=n_plain_cols // tn, tn=tn, tm=tm,
                             sub=min(tm, OUT_SUB_TILE))
    const = lambda bi, i: (0, 0)
    tile = lambda bi, i: (bi, i, 0)
    prev = lambda bi, i: (bi, jnp.maximum(i * hb - 1, 0), 0)
    nxt = lambda bi, i: (bi, jnp.minimum((i + 1) * hb, n_hblocks - 1), 0)
    y_row = lambda bi, i: (tok_offset + bi * t + i * tm) // SUBLANES
    per_b = lambda bi, i: (bi, 0, 0)
    return pl.pallas_call(
        kern,
        out_shape=[jax.ShapeDtypeStruct((b, t, d), F32), jax.ShapeDtypeStruct((b, t, n), BF16)],
        grid=(b, t // tm),
        in_specs=[
            pl.BlockSpec((1, tm, d), tile),
            pl.BlockSpec((1, SUBLANES, d), prev),
            pl.BlockSpec((1, SUBLANES, d), nxt),
            pl.BlockSpec((1, tm, ROUTER_PAD), tile),
            pl.BlockSpec((1, SUBLANES, ROUTER_PAD), prev),
            pl.BlockSpec((1, SUBLANES, ROUTER_PAD), nxt),
            pl.BlockSpec((TOP_K, tm, d // 2), lambda bi, i: (0, y_row(bi, i) // hb, 0)),
            pl.BlockSpec((TOP_K, SUBLANES, d // 2), lambda bi, i: (0, jnp.maximum(y_row(bi, i) - 1, 0), 0)),
            pl.BlockSpec((TOP_K, SUBLANES, d // 2),
                         lambda bi, i: (0, jnp.minimum(y_row(bi, i) + hb, y_hblocks - 1), 0)),
            pl.BlockSpec((1, 1, d), per_b),
            pl.BlockSpec((1, d), const),
            pl.BlockSpec((1, 1, d), per_b),
            pl.BlockSpec((1, 1, d), per_b),
            pl.BlockSpec((d, n), const),
            pl.BlockSpec(conv_w.shape, const),
            pl.BlockSpec((1, conv_w.shape[1]), const),
        ],
        out_specs=[pl.BlockSpec((1, tm, d), tile), pl.BlockSpec((1, tm, n), tile)],
        compiler_params=_params(("parallel", "parallel")),
        name="combine_inproj_conv",
    )(x1, x1, x1, gates, gates, gates, y_sel, y_sel, y_sel, g2, g, shift, scale, w, conv_w, conv_b.reshape(1, -1))


def _na_tables(rows):
    kh = min(WIN_H, rows)
    r = np.arange(rows)
    r0 = np.clip(r - kh // 2, 0, rows - kh)
    dr = r0[:, None] + np.arange(kh)[None] - r[:, None] + WIN_H - 1
    patterns, row_type = np.unique(dr, axis=0, return_inverse=True)
    return kh, r0.astype(np.int32), row_type.reshape(-1).astype(np.int32), patterns


def _na_bias_table(rpb, patterns):
    qc = np.arange(GRID_W)
    kc = np.arange(GRID_W)
    c0 = np.clip(qc - WIN_W // 2, 0, GRID_W - WIN_W)[:, None]
    valid = (kc[None] >= c0) & (kc[None] < c0 + WIN_W)
    dc = np.clip(kc[None] - qc[:, None] + WIN_W - 1, 0, 2 * WIN_W - 2)
    n_pat, kh = patterns.shape
    onehot_dc = jnp.asarray(dc[None] == np.arange(2 * WIN_W - 1)[:, None, None], F32)
    tab = jnp.einsum('hpic,cqk->phqik', rpb.astype(F32)[:, patterns], onehot_dc,
                     precision=lax.Precision.HIGHEST)
    tab = jnp.where(valid[None, None, :, None, :], tab * LOG2_E, NEG_INF)
    return tab.reshape(n_pat, HEAD_PAIRS, 2 * GRID_W, kh * GRID_W)


def _pair_attention(q, keys, values, biases):
    m = q.shape[0]
    qq = _stack_heads(q)
    scores = []
    for k, bias in zip(keys, biases):
        s = _dot_nt(qq, k)
        scores.append(s if bias is None else s + bias)
    s = jnp.concatenate(scores, axis=1)
    e = jnp.exp2(s - jnp.max(s, axis=-1, keepdims=True))
    denom = jnp.sum(e, axis=-1, keepdims=True)
    e = e.astype(BF16)
    o, start = 0.0, 0
    for v in values:
        o = o + _dot(e[:, start:start + v.shape[0]], v)
        start += v.shape[0]
    o = o * (1.0 / denom)
    lane_o = lax.broadcasted_iota(jnp.int32, (m, LANES), 1)
    return jnp.where(lane_o < NA_HEAD_DIM, o[:m], o[m:])


def _stack_heads(q):
    lane = lax.broadcasted_iota(jnp.int32, q.shape, 1)
    zero = jnp.zeros_like(q)
    return jnp.concatenate([jnp.where(lane < NA_HEAD_DIM, q, zero), jnp.where(lane >= NA_HEAD_DIM, q, zero)], axis=0)


def _na_kernel(r0_ref, type_ref, q_ref, k_ref, v_ref, kc_ref, vc_ref, bias_ref, o_ref, s_ref, p_ref, *,
               kh, rows_per_step):
    n_lat = kh * GRID_W
    tiles = [(j, p) for j in range(rows_per_step) for p in range(HEAD_PAIRS)]
    window = []
    for j in range(rows_per_step):
        r = pl.program_id(1) * rows_per_step + j
        window.append((pl.multiple_of(r0_ref[r] * GRID_W, GRID_W), type_ref[r]))

    for idx, (j, p) in enumerate(tiles):
        start, rtype = window[j]
        cols = slice(p * LANES, (p + 1) * LANES)
        qq = _stack_heads(q_ref[0, j * GRID_W:(j + 1) * GRID_W, cols])
        s_ref[idx, :, :n_lat] = _dot_nt(qq, k_ref[0, pl.ds(start, n_lat), cols]) + bias_ref[rtype, p]
        s_ref[idx, :, n_lat:] = _dot_nt(qq, kc_ref[0, :, cols])

    denoms = []
    for idx in range(len(tiles)):
        s = s_ref[idx]
        e = jnp.exp2(s - jnp.max(s, axis=-1, keepdims=True))
        denoms.append(jnp.sum(e, axis=-1, keepdims=True))
        p_ref[idx] = e.astype(BF16)

    lane = lax.broadcasted_iota(jnp.int32, (GRID_W, LANES), 1)
    for idx, (j, p) in enumerate(tiles):
        start, _ = window[j]
        cols = slice(p * LANES, (p + 1) * LANES)
        o = _dot(p_ref[idx, :, :n_lat], v_ref[0, pl.ds(start, n_lat), cols]) + _dot(p_ref[idx, :, n_lat:],
                                                                                 vc_ref[0, :, cols])
        o = o * (1.0 / denoms[idx])
        o = jnp.where(lane < NA_HEAD_DIM, o[:GRID_W], o[GRID_W:])
        o_ref[0, j * GRID_W:(j + 1) * GRID_W, cols] = o.astype(o_ref.dtype)


def _neighbourhood_attention(proj, proj_c, bias_tab, r0, row_type, kh):
    b, t, _ = proj.shape
    l = proj_c.shape[1]
    rows = t // GRID_W
    w = NA_WIDTH
    rps = int(np.gcd(rows, NA_ROWS_PER_STEP))
    q_rows = rps * GRID_W
    grid_spec = pltpu.PrefetchScalarGridSpec(
        num_scalar_prefetch=2,
        grid=(b, rows // rps),
        in_specs=[
            pl.BlockSpec((1, q_rows, w), lambda bi, r, *_: (bi, r, 0)),
            pl.BlockSpec((1, t, w), lambda bi, r, *_: (bi, 0, 1)),
            pl.BlockSpec((1, t, w), lambda bi, r, *_: (bi, 0, 2)),
            pl.BlockSpec((1, l, w), lambda bi, r, *_: (bi, 0, 1)),
            pl.BlockSpec((1, l, w), lambda bi, r, *_: (bi, 0, 2)),
            pl.BlockSpec(bias_tab.shape, lambda bi, r, *_: (0, 0, 0, 0)),
        ],
        out_specs=pl.BlockSpec((1, q_rows, w), lambda bi, r, *_: (bi, r, 0)),
        scratch_shapes=[pltpu.VMEM((rps * HEAD_PAIRS, 2 * GRID_W, kh * GRID_W + l), F32),
                        pltpu.VMEM((rps * HEAD_PAIRS, 2 * GRID_W, kh * GRID_W + l), BF16)],
    )
    return pl.pallas_call(
        functools.partial(_na_kernel, kh=kh, rows_per_step=rps),
        out_shape=jax.ShapeDtypeStruct((b, t, w), BF16),
        grid_spec=grid_spec,
        compiler_params=_params(("parallel", "arbitrary")),
        name="na_attention",
    )(jnp.asarray(r0), jnp.asarray(row_type), proj, proj, proj, proj_c, proj_c, bias_tab)


def _ctx_attn_kernel(q_ref, k_ref, v_ref, o_ref):
    for p in range(HEAD_PAIRS):
        cols = slice(p * LANES, (p + 1) * LANES)
        o = _pair_attention(q_ref[0, :, cols], [k_ref[0, :, cols]], [v_ref[0, :, cols]], [None])
        o_ref[0, :, cols] = o.astype(o_ref.dtype)


def _context_attention(proj_c):
    b, l, _ = proj_c.shape
    w = NA_WIDTH
    return pl.pallas_call(
        _ctx_attn_kernel,
        out_shape=jax.ShapeDtypeStruct((b, l, w), BF16),
        grid=(b,),
        in_specs=[pl.BlockSpec((1, l, w), lambda bi, j=j: (bi, 0, j)) for j in range(3)],
        out_specs=pl.BlockSpec((1, l, w), lambda bi: (bi, 0, 0)),
        compiler_params=_params(("parallel",)),
        name="ctx_attention",
    )(proj_c, proj_c, proj_c)


def _top4(logits):
    lane = lax.broadcasted_iota(jnp.int32, logits.shape, 1)
    cur = jnp.where(lane < N_EXPERTS, logits, -jnp.inf)
    picked = jnp.zeros(logits.shape, F32)
    vals, idxs = [], []
    for _ in range(TOP_K):
        m = jnp.max(cur, axis=-1, keepdims=True)
        first = jnp.min(jnp.where(cur == m, lane, ROUTER_PAD).astype(F32), axis=-1, keepdims=True)
        idx = first.astype(jnp.int32)
        vals.append(m)
        idxs.append(idx)
        hit = lane == idx
        picked = jnp.where(hit, 1.0, picked)
        cur = jnp.where(hit, -jnp.inf, cur)
    exps = [jnp.exp(v - vals[0]) for v in vals]
    inv = 1.0 / functools.reduce(jnp.add, exps)
    ids = jnp.zeros(logits.shape, jnp.int32)
    gates = jnp.zeros(logits.shape, F32)
    for k in range(TOP_K):
        ids = jnp.where(lane == k, idxs[k], ids)
        gates = jnp.where(lane == k, exps[k] * inv, gates)
    return ids, gates, jnp.sum(picked, axis=0, keepdims=True)


def _layer_tail(y, rows, x_ref, g1_ref, n2_ref, sh2_ref, sc2_ref, wr_ref, br_ref, *rest):
    xo_ref, h2_ref, ids_ref, gates_ref = rest[-5:-1]
    x_new = x_ref[0, rows] + g1_ref[0] * y
    xo_ref[0, rows] = x_new
    h2 = _rms_mod(x_new, n2_ref[...], sh2_ref[0], sc2_ref[0])
    h2_ref[rows] = _pack_rows(h2)
    logits = _dot(h2.astype(BF16), wr_ref[...]) + br_ref[...]
    ids, gates, counts = _top4(logits)
    ids_ref[rows] = ids
    gates_ref[0, rows] = gates
    return counts


def _sub_tiles(n_rows):
    sub = min(OUT_SUB_TILE, n_rows)
    return [slice(s, s + sub) for s in range(0, n_rows, sub)]


def _tail_specs(b, t, d, tm, n_inputs_before, tokens, shared):
    n_total, offset = tokens
    n_t = t // tm
    assert offset % tm == 0
    row = lambda bi, i: (bi, i, 0)
    flat = lambda bi, i: (offset // tm + bi * n_t + i, 0)
    per_b = lambda bi, i: (bi, 0, 0)
    const = lambda bi, i: (0, 0)
    extra, aliases = (), {}
    if shared is not None:
        extra = tuple(shared)
        aliases = {n_inputs_before + 7: 1, n_inputs_before + 8: 2}
    in_specs = [
        pl.BlockSpec((1, tm, d), row),
        pl.BlockSpec((1, 1, d), per_b),
        pl.BlockSpec((1, d), const),
        pl.BlockSpec((1, 1, d), per_b),
        pl.BlockSpec((1, 1, d), per_b),
        pl.BlockSpec((d, ROUTER_PAD), const),
        pl.BlockSpec((1, ROUTER_PAD), const),
    ] + [pl.BlockSpec(memory_space=pl.ANY)] * len(extra)
    out_specs = [
        pl.BlockSpec((1, tm, d), row),
        pl.BlockSpec((tm, d // 2), flat),
        pl.BlockSpec((tm, ROUTER_PAD), flat),
        pl.BlockSpec((1, tm, ROUTER_PAD), row),
        pl.BlockSpec((1, SUBLANES, ROUTER_PAD), lambda bi, i: (bi * n_t + i, 0, 0)),
    ]
    out_shape = [
        jax.ShapeDtypeStruct((b, t, d), F32),
        jax.ShapeDtypeStruct((n_total, d // 2), jnp.int32),
        jax.ShapeDtypeStruct((n_total, ROUTER_PAD), jnp.int32),
        jax.ShapeDtypeStruct((b, t, ROUTER_PAD), F32),
        jax.ShapeDtypeStruct((b * n_t, SUBLANES, ROUTER_PAD), F32),
    ]
    return in_specs, out_specs, out_shape, extra, aliases


def _halo_fix(rolled, at_row, halo_row, present):
    n = rolled.shape[0]
    first = at_row < SUBLANES
    assert first or at_row >= n - SUBLANES
    slab = rolled[:SUBLANES] if first else rolled[n - SUBLANES:]
    sub = lax.broadcasted_iota(jnp.int32, slab.shape, 0)
    fill = jnp.where(present, halo_row, jnp.zeros_like(halo_row))
    slab = jnp.where(sub == at_row % SUBLANES, fill, slab)
    return jnp.concatenate([slab, rolled[SUBLANES:]] if first else [rolled[:n - SUBLANES], slab], axis=0)


def _even_out_kernel(oa_ref, bg_ref, cg_ref, xin_ref, cgp_ref, xinp_ref, cgn_ref, xinn_ref, cw_ref, cb_ref,
                     wa_ref, wb_ref, *tail_refs, tm):
    i = pl.program_id(1)
    has_prev = i > 0
    has_next = i < pl.num_programs(1) - 1
    u = cg_ref[0].astype(F32) * xin_ref[0].astype(F32)
    u_prev = (cgp_ref[0].astype(F32) * xinp_ref[0].astype(F32))[HALO - 1:HALO]
    u_next = (cgn_ref[0].astype(F32) * xinn_ref[0].astype(F32))[0:1]
    u_m1 = _halo_fix(pltpu.roll(u, 1, 0), 0, u_prev, has_prev)
    u_p1 = _halo_fix(pltpu.roll(u, tm - 1, 0), tm - 1, u_next, has_next)
    cw = cw_ref[...]
    conv = u_m1 * cw[0:1] + u * cw[1:2] + u_p1 * cw[2:3] + cb_ref[...]
    o_b = (bg_ref[0].astype(F32) * conv).astype(BF16)
    counts = 0.0
    for rows in _sub_tiles(tm):
        y = _dot(oa_ref[0, rows], wa_ref[...]) + _dot(o_b[rows], wb_ref[...])
        counts = counts + _layer_tail(y, rows, *tail_refs)
    tail_refs[-1][0] = jnp.broadcast_to(counts, tail_refs[-1].shape[1:])


def _even_out(o_a, proj, conv_w, conv_b, w_out, x, g1, n2, sh2, sc2, w_r, b_r, tokens, shared=None):
    b, t, d = x.shape
    w = NA_WIDTH
    tm = min(t, OUT_TILE)
    hb = tm // HALO
    n_hblocks = t // HALO
    row = lambda bi, i: (bi, i, 0)
    const = lambda bi, i: (0, 0)
    prev = lambda col: (lambda bi, i: (bi, jnp.maximum(i * hb - 1, 0), col))
    nxt = lambda col: (lambda bi, i: (bi, jnp.minimum((i + 1) * hb, n_hblocks - 1), col))
    tail_in, out_specs, out_shape, extra, aliases = _tail_specs(b, t, d, tm, 12, tokens, shared)
    in_specs = [
        pl.BlockSpec((1, tm, w), row),
        pl.BlockSpec((1, tm, w), lambda bi, i: (bi, i, 3)),
        pl.BlockSpec((1, tm, w), lambda bi, i: (bi, i, 4)),
        pl.BlockSpec((1, tm, w), lambda bi, i: (bi, i, 5)),
        pl.BlockSpec((1, HALO, w), prev(4)),
        pl.BlockSpec((1, HALO, w), prev(5)),
        pl.BlockSpec((1, HALO, w), nxt(4)),
        pl.BlockSpec((1, HALO, w), nxt(5)),
        pl.BlockSpec((SC_CONV, w), const),
        pl.BlockSpec((1, w), const),
        pl.BlockSpec((w, d), const),
        pl.BlockSpec((w, d), const),
    ] + tail_in
    return pl.pallas_call(
        functools.partial(_even_out_kernel, tm=tm),
        out_shape=out_shape,
        grid=(b, t // tm),
        in_specs=in_specs,
        out_specs=out_specs,
        input_output_aliases=aliases,
        compiler_params=_params(("parallel", "parallel")),
        name="even_out",
    )(o_a, proj, proj, proj, proj, proj, proj, proj, conv_w, conv_b.reshape(1, w),
      w_out[:w], w_out[w:], x, g1, n2, sh2, sc2, w_r, b_r, *extra)


def _log_sigmoid(x):
    return jnp.minimum(x, 0.0) - jnp.log1p(jnp.exp(-jnp.abs(x)))


def _sigmoid(x):
    return 0.5 + 0.5 * jnp.tanh(0.5 * x)


def _lru_sub_tile(ucb_time, state, perm_ref, w_ref, ba_ref, bx_ref, lam_ref, want_hidden, reverse):
    sub, width = ucb_time.shape
    blk = width // LRU_BLOCKS
    steps = sub // SUBLANES
    uc = _dot(perm_ref[0], ucb_time)
    ucb = uc.astype(BF16)
    za, zx = [], []
    for h in range(LRU_BLOCKS):
        z = _dot(ucb[:, h * blk:(h + 1) * blk], w_ref[h])
        za.append(z[:, :blk])
        zx.append(z[:, blk:])
    r = _sigmoid(jnp.concatenate(za, axis=1) + ba_ref[...])
    gate_i = _sigmoid(jnp.concatenate(zx, axis=1) + bx_ref[...])
    log_a = (LRU_C * _log_sigmoid(lam_ref[...])) * r
    a = jnp.exp(log_a)
    th = jnp.tanh(log_a)
    num = -2.0 * th
    mult = jnp.where(num > 0.0, num * lax.rsqrt(num * (1.0 - th)), 0.0)
    bcoef = mult * gate_i * uc

    grp = lambda v, j: v[j * SUBLANES:(j + 1) * SUBLANES]
    prods, local = [grp(a, 0)], [grp(bcoef, 0)]
    for j in range(1, steps):
        aj = grp(a, j)
        local.append(aj * local[-1] + grp(bcoef, j))
        prods.append(aj * prods[-1])

    carry_in = [None] * SUBLANES
    for s in (range(SUBLANES - 1, -1, -1) if reverse else range(SUBLANES)):
        carry_in[s] = state
        state = prods[-1][s:s + 1] * state + local[-1][s:s + 1]
    if not want_hidden:
        return state, None
    start = jnp.concatenate(carry_in, axis=0)
    hidden = jnp.concatenate([prods[j] * start + local[j] for j in range(steps)], axis=0)
    return state, _dot(perm_ref[1], hidden.astype(BF16)).astype(BF16)


def _lru_tile(u_ref, perm_ref, w_ref, ba_ref, bx_ref, lam_ref, carry_ref, o_ref, *, reverse):
    sub = perm_ref.shape[1]
    n_sub = u_ref.shape[1] // sub
    state = carry_ref[0:1, :]
    for k in (range(n_sub - 1, -1, -1) if reverse else range(n_sub)):
        rows = slice(k * sub, (k + 1) * sub)
        state, hidden = _lru_sub_tile(u_ref[0, rows], state, perm_ref, w_ref, ba_ref, bx_ref, lam_ref,
                                      o_ref is not None, reverse)
        if o_ref is not None:
            o_ref[0, rows] = hidden
    carry_ref[...] = jnp.broadcast_to(state, carry_ref.shape)


def _lru_kernel(uc_ref, ul_ref, perm_ref, w_ref, ba_ref, bx_ref, lam_ref, o_ref, carry_ref, *, n_ctx_tiles, reverse):
    j = pl.program_id(1)
    shared = (perm_ref, w_ref, ba_ref, bx_ref, lam_ref, carry_ref)

    @pl.when(j == 0)
    def _():
        carry_ref[...] = jnp.zeros_like(carry_ref)

    @pl.when(j < n_ctx_tiles)
    def _():
        _lru_tile(uc_ref, *shared, None, reverse=reverse)

    @pl.when(j >= n_ctx_tiles)
    def _():
        _lru_tile(ul_ref, *shared, o_ref, reverse=reverse)


def _lru_scan(proj, u_ctx, w_cat, ba, bx, lam, reverse):
    b, t, _ = proj.shape
    l, width = u_ctx.shape[1], u_ctx.shape[2]
    sub = min(256, l, t)
    tc = sub * min(LRU_SUB_TILES, l // sub)
    tl = sub * min(LRU_SUB_TILES, t // sub)
    n_c, n_l = l // tc, t // tl

    def pos_of(step, n):
        step = jnp.clip(step, 0, n - 1)
        return (n - 1 - step) if reverse else step

    def tile_map(off, n, col):
        return lambda bi, j: (bi, pos_of(j - off, n), col)

    const2 = lambda bi, j: (0, 0)
    in_specs = [
        pl.BlockSpec((1, tc, width), tile_map(0, n_c, 0)),
        pl.BlockSpec((1, tl, width), tile_map(n_c, n_l, 1)),
        pl.BlockSpec((2, sub, sub), lambda bi, j: (0, 0, 0)),
        pl.BlockSpec(w_cat.shape, lambda bi, j: (0, 0, 0)),
        pl.BlockSpec((1, width), const2),
        pl.BlockSpec((1, width), const2),
        pl.BlockSpec((1, width), const2),
    ]
    steps = sub // SUBLANES
    step, block = np.divmod(np.arange(sub), SUBLANES)
    time_of_row = block * steps + (steps - 1 - step if reverse else step)
    perm = (time_of_row[:, None] == np.arange(sub)[None]).astype(np.float32)
    perms = jnp.asarray(np.stack([perm, perm.T]), BF16)
    kern = functools.partial(_lru_kernel, n_ctx_tiles=n_c, reverse=reverse)
    return pl.pallas_call(
        kern,
        out_shape=jax.ShapeDtypeStruct((b, t, width), BF16),
        grid=(b, n_c + n_l),
        in_specs=in_specs,
        out_specs=pl.BlockSpec((1, tl, width), tile_map(n_c, n_l, 0)),
        scratch_shapes=[pltpu.VMEM((SUBLANES, width), F32)],
        compiler_params=_params(("parallel", "arbitrary")),
        name="lru_scan_bwd" if reverse else "lru_scan_fwd",
    )(u_ctx, proj, perms, w_cat, ba.reshape(1, width), bx.reshape(1, width), lam.reshape(1, width))


def _odd_out_kernel(hf_ref, hb_ref, gate_ref, w_ref, *tail_refs):
    counts = 0.0
    for rows in _sub_tiles(hf_ref.shape[1]):
        hsum = hf_ref[0, rows].astype(F32) + hb_ref[0, rows].astype(F32)
        z = hsum * jax.nn.gelu(gate_ref[0, rows].astype(F32), approximate=True)
        y = _dot(z.astype(BF16), w_ref[...])
        counts = counts + _layer_tail(y, rows, *tail_refs)
    tail_refs[-1][0] = jnp.broadcast_to(counts, tail_refs[-1].shape[1:])


def _odd_out(h_f, h_b, proj, w_out, x, g1, n2, sh2, sc2, w_r, b_r):
    b, t, d = x.shape
    width = h_f.shape[-1]
    tm = min(t, OUT_TILE)
    row = lambda bi, i: (bi, i, 0)
    tail_in, out_specs, out_shape, _, _ = _tail_specs(b, t, d, tm, 4, (b * t, 0), None)
    in_specs = [
        pl.BlockSpec((1, tm, width), row),
        pl.BlockSpec((1, tm, width), row),
        pl.BlockSpec((1, tm, width), row),
        pl.BlockSpec((width, d), lambda bi, i: (0, 0)),
    ] + tail_in
    return pl.pallas_call(
        _odd_out_kernel,
        out_shape=out_shape,
        grid=(b, t // tm),
        in_specs=in_specs,
        out_specs=out_specs,
        compiler_params=_params(("parallel", "parallel")),
        name="odd_out",
    )(h_f, h_b, proj, w_out, x, g1, n2, sh2, sc2, w_r, b_r)


def _expert_kernel(be_ref, next_ref, rows_ref, nb_ref, x_ref, wg_hbm, bg_ref, wu_hbm, bu_ref, wd_hbm, bd_ref,
                   o_ref, w_f32, wg_bf, wu_bf, wd_bf, h_bf, sem, *, layer, chunk):
    i = pl.program_id(0)
    expert = be_ref[i]
    used = i < nb_ref[0]

    def weight_copies(e):
        return [pltpu.make_async_copy(src.at[layer, e], w_f32.at[j], sem.at[j])
                for j, src in enumerate((wg_hbm, wu_hbm, wd_hbm))]

    @pl.when(i == 0)
    def _():
        for cp in weight_copies(expert):
            cp.start()

    @pl.when(used & ((i == 0) | (expert != be_ref[jnp.maximum(i - 1, 0)])))
    def _():
        for cp in weight_copies(expert):
            cp.wait()
        wg_bf[...] = w_f32[0].astype(BF16)
        wu_bf[...] = w_f32[1].astype(BF16)
        wd_bf[...] = w_f32[2].astype(BF16)

        @pl.when(next_ref[i] >= 0)
        def _():
            for cp in weight_copies(next_ref[i]):
                cp.start()

    sub = h_bf.shape[0]
    n_sub = x_ref.shape[0] // sub
    n_rows = jnp.where(used, rows_ref[i], 0)
    n_live = (n_rows + sub - 1) // sub

    def live_sub_block(s, carry):
        rows = pl.ds(pl.multiple_of(s * sub, sub), sub)
        xp = x_ref[rows]
        row = lax.broadcasted_iota(jnp.int32, xp.shape, 0) + s * sub
        xp = jnp.where(row < n_rows, xp, 0)
        x = jnp.concatenate(_unpack_rows(xp), axis=1).astype(BF16)
        for c in range(wg_bf.shape[1] // chunk):
            cs = slice(c * chunk, (c + 1) * chunk)
            g = jnp.minimum(_dot(x, wg_bf[:, cs]) + bg_ref[0, 0, :, cs], SWIGLU_LIMIT)
            u = jnp.clip(_dot(x, wu_bf[:, cs]) + bu_ref[0, 0, :, cs], -SWIGLU_LIMIT, SWIGLU_LIMIT)
            h_bf[:, cs] = (g * jax.nn.sigmoid(SWIGLU_ALPHA * g) * (u + 1.0)).astype(BF16)
        o_ref[rows] = _pack_rows(_dot(h_bf[...], wd_bf[...]) + bd_ref[0, 0])
        return carry

    def empty_sub_block(s, carry):
        o_ref[pl.ds(pl.multiple_of(s * sub, sub), sub)] = jnp.zeros((sub, o_ref.shape[1]), o_ref.dtype)
        return carry

    lax.fori_loop(0, n_live, live_sub_block, 0)
    lax.fori_loop(n_live, n_sub, empty_sub_block, 0)


def _experts(xb, block_e, next_e, block_rows, n_used, layer, wg, bg, wu, bu, wd, bd):
    n_slots = xb.shape[0]
    depth, n_e, d, d_exp = wg.shape
    assert d == d_exp
    tm = EXPERT_TILE
    n_blocks = n_slots // tm
    xmap = lambda i, be, ne, br, nb: (jnp.minimum(i, nb[0] - 1), 0)
    bmap = lambda i, be, ne, br, nb: (layer, be[i], 0, 0)
    hbm = pl.BlockSpec(memory_space=pl.ANY)
    grid_spec = pltpu.PrefetchScalarGridSpec(
        num_scalar_prefetch=4,
        grid=(n_blocks,),
        in_specs=[
            pl.BlockSpec((tm, d // 2), xmap),
            hbm,
            pl.BlockSpec((1, 1, 1, d_exp), bmap),
            hbm,
            pl.BlockSpec((1, 1, 1, d_exp), bmap),
            hbm,
            pl.BlockSpec((1, 1, 1, d), bmap),
        ],
        out_specs=pl.BlockSpec((tm, d // 2), lambda i, be, ne, br, nb: (i, 0)),
        scratch_shapes=[pltpu.VMEM((3, d, d_exp), F32), pltpu.VMEM((d, d_exp), BF16), pltpu.VMEM((d, d_exp), BF16),
                        pltpu.VMEM((d_exp, d), BF16), pltpu.VMEM((EXPERT_SUB, d_exp), BF16),
                        pltpu.SemaphoreType.DMA((3,))],
    )
    return pl.pallas_call(
        functools.partial(_expert_kernel, layer=layer, chunk=256),
        out_shape=jax.ShapeDtypeStruct((n_slots, d // 2), jnp.int32),
        grid_spec=grid_spec,
        compiler_params=_params(("arbitrary",)),
        name="experts",
    )(block_e, next_e, block_rows, n_used, xb, wg, bg.reshape(depth, n_e, 1, d_exp), wu,
      bu.reshape(depth, n_e, 1, d_exp), wd, bd.reshape(depth, n_e, 1, d))


def _combine_kernel(x_ref, g2_ref, gates_ref, y_ref, o_ref):
    o_ref[0] = _moe_residual(x_ref[0], g2_ref[0], gates_ref[0], y_ref[...])


def _combine(x, g2, gates, y_sel, tok_offset):
    b, t, d = x.shape
    tm = int(np.gcd(min(t, ROW_TILE), tok_offset)) if tok_offset else min(t, ROW_TILE)
    n_t = t // tm
    off = tok_offset // tm
    row = lambda bi, i: (bi, i, 0)
    return pl.pallas_call(
        _combine_kernel,
        out_shape=jax.ShapeDtypeStruct((b, t, d), F32),
        grid=(b, n_t),
        in_specs=[
            pl.BlockSpec((1, tm, d), row),
            pl.BlockSpec((1, 1, d), lambda bi, i: (bi, 0, 0)),
            pl.BlockSpec((1, tm, ROUTER_PAD), row),
            pl.BlockSpec((TOP_K, tm, d // 2), lambda bi, i: (0, off + bi * n_t + i, 0)),
        ],
        out_specs=pl.BlockSpec((1, tm, d), row),
        input_output_aliases={0: 0},
        compiler_params=_params(("parallel", "parallel")),
        name="moe_combine",
    )(x, g2, gates, y_sel)


def _route_kernel(ids_ref, upper_ref, starts_ref, dest_ref, run_ref):
    i = pl.program_id(0)
    tm = ids_ref.shape[0]
    ids_t = ids_ref[...].T
    expert = lax.broadcasted_iota(jnp.int32, (N_EXPERTS, tm), 0)
    chosen = [ids_t[k:k + 1, :] == expert for k in range(TOP_K)]
    picks = functools.reduce(jnp.add, [c.astype(F32) for c in chosen])

    @pl.when(i == 0)
    def _():
        run_ref[...] = starts_ref[...]

    before = _dot(picks.astype(BF16), upper_ref[...])
    slot = before + run_ref[:, 0:1]
    rows = [jnp.sum(jnp.where(c, slot, 0.0), axis=0, keepdims=True) for c in chosen]
    rows += [jnp.zeros_like(rows[0])] * (dest_ref.shape[0] - TOP_K)
    dest_ref[...] = jnp.concatenate(rows, axis=0).astype(jnp.int32)
    run_ref[...] += jnp.sum(picks, axis=1, keepdims=True)


def _route(ids, region_starts):
    n = ids.shape[0]
    tm = int(np.gcd(n, ROUTE_TILE))
    upper = jnp.asarray(np.triu(np.ones((tm, tm), np.float32), 1), BF16)
    starts = jnp.broadcast_to(region_starts.astype(F32)[:, None], (N_EXPERTS, LANES))
    return pl.pallas_call(
        _route_kernel,
        out_shape=jax.ShapeDtypeStruct((SUBLANES, n), jnp.int32),
        grid=(n // tm,),
        in_specs=[pl.BlockSpec((tm, ROUTER_PAD), lambda i: (i, 0)),
                  pl.BlockSpec((tm, tm), lambda i: (0, 0)),
                  pl.BlockSpec((N_EXPERTS, LANES), lambda i: (0, 0))],
        out_specs=pl.BlockSpec((SUBLANES, tm), lambda i: (0, i)),
        scratch_shapes=[pltpu.VMEM((N_EXPERTS, LANES), F32)],
        compiler_params=_params(("arbitrary",)),
        name="route",
    )(ids, upper, starts)


def _row_gather(table, idx):
    info = plsc.get_sparse_core_info()
    n_cores, n_workers = info.num_cores, info.num_cores * info.num_subcores
    n_rows, width = idx.shape[0], table.shape[1]
    chunk_rows = SC_GATHER_ROWS
    per_worker = n_rows // n_workers
    n_chunks = per_worker // chunk_rows
    assert per_worker * n_workers == n_rows and n_chunks * chunk_rows == per_worker and n_chunks % 2 == 0
    mesh = plsc.VectorSubcoreMesh(core_axis_name="c", subcore_axis_name="s")

    @functools.partial(
        pl.kernel, mesh=mesh,
        out_type=jax.ShapeDtypeStruct((n_rows, width), table.dtype),
        scratch_types=[pltpu.VMEM((n_chunks, chunk_rows), jnp.int32), pltpu.VMEM((2, chunk_rows, width), table.dtype),
                       pltpu.SemaphoreType.DMA((2,)), pltpu.SemaphoreType.DMA((2,))],
    )
    def gather_kernel(table_hbm, idx_hbm, out_hbm, idx_v, rows_v, gather_sem, write_sem):
        worker = lax.axis_index("s") * n_cores + lax.axis_index("c")
        pltpu.sync_copy(idx_hbm.at[worker], idx_v)

        def gather(c, slot):
            return pltpu.make_async_copy(table_hbm.at[idx_v.at[c]], rows_v.at[slot], gather_sem.at[slot])

        def write(c, slot):
            first_row = worker * per_worker + c * chunk_rows
            return pltpu.make_async_copy(rows_v.at[slot], out_hbm.at[pl.ds(first_row, chunk_rows)], write_sem.at[slot])

        gather(0, 0).start()

        @pl.loop(0, n_chunks, step=2)
        def _(c0):
            for slot in (0, 1):
                c = c0 + slot
                gather(c, slot).wait()

                @pl.when(c >= 1)
                def _():
                    write(c - 1, 1 - slot).wait()

                @pl.when(c + 1 < n_chunks)
                def _():
                    gather(c + 1, 1 - slot).start()

                write(c, slot).start()

        write(n_chunks - 1, 1).wait()

    return gather_kernel(table, idx.reshape(n_workers, n_chunks, chunk_rows))


def _row_scatter(rows, dest, n_out):
    info = plsc.get_sparse_core_info()
    n_cores, n_workers = info.num_cores, info.num_cores * info.num_subcores
    n_choices, n_rows = dest.shape
    width = rows.shape[1]
    chunk_rows = SC_SCATTER_ROWS
    per_worker = n_rows // (n_workers * chunk_rows)
    assert per_worker * n_workers * chunk_rows == n_rows == rows.shape[0]
    idx = dest.reshape(n_choices, n_workers, per_worker, chunk_rows)
    mesh = plsc.VectorSubcoreMesh(core_axis_name="c", subcore_axis_name="s")

    @functools.partial(
        pl.kernel, mesh=mesh,
        out_type=jax.ShapeDtypeStruct((n_out, width), rows.dtype),
        scratch_types=[pltpu.VMEM((n_choices, per_worker, chunk_rows), jnp.int32),
                       pltpu.VMEM((chunk_rows, width), rows.dtype), pltpu.SemaphoreType.DMA],
    )
    def scatter_kernel(rows_hbm, idx_hbm, out_hbm, idx_v, rows_v, sem):
        worker = lax.axis_index("s") * n_cores + lax.axis_index("c")
        for k in range(n_choices):
            pltpu.sync_copy(idx_hbm.at[k, worker], idx_v.at[k])

        @pl.loop(0, per_worker)
        def _(c):
            first_row = (worker * per_worker + c) * chunk_rows
            pltpu.sync_copy(rows_hbm.at[pl.ds(first_row, chunk_rows)], rows_v)
            copies = [pltpu.async_copy(rows_v, out_hbm.at[idx_v.at[k, c]], sem) for k in range(n_choices)]
            for cp in copies:
                cp.wait()

    return scatter_kernel(rows, idx)


def _expert_counts(step_picks):
    return jnp.sum(step_picks[:, 0, :N_EXPERTS], axis=0).astype(jnp.int32)


def _moe(h2, ids, counts, layer, w_exp):
    n = h2.shape[0]
    nk = n * TOP_K
    tm = EXPERT_TILE
    padded = (counts + tm - 1) // tm * tm
    pends = jnp.cumsum(padded)
    dest = _route(ids, pends - padded)[:TOP_K]
    n_blocks = -(-nk // tm) + N_EXPERTS
    n_slots = n_blocks * tm
    block_start = jnp.arange(n_blocks, dtype=jnp.int32) * tm
    block_e = jnp.minimum(jnp.sum((pends[None] <= block_start[:, None]).astype(jnp.int32), axis=1), N_EXPERTS - 1)
    is_e = (block_e[:, None] == jnp.arange(N_EXPERTS, dtype=jnp.int32)[None]).astype(jnp.int32)
    block_rows = jnp.clip(jnp.sum(is_e * (pends - padded + counts)[None], axis=1) - block_start, 0, tm)
    n_used = (pends[-1] // tm).astype(jnp.int32).reshape(1)
    e_ids = jnp.arange(N_EXPERTS, dtype=jnp.int32)
    later = (e_ids[None] > e_ids[:, None]) & (counts[None] > 0)
    next_nonempty = jnp.min(jnp.where(later, e_ids[None], N_EXPERTS), axis=1)
    next_nonempty = jnp.where(next_nonempty == N_EXPERTS, -1, next_nonempty)
    next_e = jnp.sum(is_e * next_nonempty[None], axis=1)
    xb = _row_scatter(h2, dest, n_slots)
    return _experts(xb, block_e, next_e, block_rows, n_used, layer, *w_exp), dest


def _gather_choices(yb, dest, lo, hi):
    return _row_gather(yb, dest[:, lo:hi].reshape(-1)).reshape(TOP_K, hi - lo, -1)


def _mod_parts(mod_l, b):
    d = mod_l.shape[-1] // 6
    lat = [mod_l[:b, k * d:(k + 1) * d].reshape(b, 1, d) for k in range(6)]
    ctx = [jnp.broadcast_to(mod_l[b, k * d:(k + 1) * d].reshape(1, 1, d), (b, 1, d)) for k in range(6)]
    return lat, ctx


def kernel(x, c, ctx, c_ctx, ada_w, ada_b, norm1_g, norm2_g, ev_w_in, ev_w_out, ev_q_gain, ev_k_gain, ev_rpb, ev_conv_w, ev_conv_b, od_w_in, od_w_out, od_conv_w, od_conv_b, od_fwd_wa, od_fwd_ba, od_fwd_wx, od_fwd_bx, od_fwd_lam, od_bwd_wa, od_bwd_ba, od_bwd_wx, od_bwd_bx, od_bwd_lam, router_w, router_b, exp_w_gate, exp_b_gate, exp_w_up, exp_b_up, exp_w_down, exp_b_down):
    b, t, d = x.shape
    l = ctx.shape[1]
    assert ada_w.shape[0] == DEPTH == 2 and t % GRID_W == 0 and t // GRID_W >= WIN_H

    n_rows_c = -(-(b + 1) // SUBLANES) * SUBLANES
    cvec = jnp.zeros((n_rows_c, d), F32).at[:b].set(c).at[b].set(c_ctx)
    mod = _ada_mod(cvec, ada_w, ada_b)

    def router(layer):
        w_r = jnp.zeros((d, ROUTER_PAD), F32).at[:, :N_EXPERTS].set(router_w[layer]).astype(BF16)
        b_r = jnp.zeros((1, ROUTER_PAD), F32).at[0, :N_EXPERTS].set(router_b[layer])
        return w_r, b_r

    w_exp = (exp_w_gate, exp_b_gate, exp_w_up, exp_b_up, exp_w_down, exp_b_down)

    (sh1, sc1, g1, sh2, sc2, g2), (csh1, csc1, cg1, csh2, csc2, cg2) = _mod_parts(mod[0], b)
    n1 = norm1_g[0].reshape(1, d)
    n2 = norm2_g[0].reshape(1, d)
    w_in = ev_w_in[0].astype(BF16)
    w_out = ev_w_out[0].astype(BF16)
    q_scale = NA_HEAD_DIM ** -0.5 * LOG2_E
    head_gain = jnp.stack([jnp.tile(ev_q_gain[0] * q_scale, NA_HEADS), jnp.tile(ev_k_gain[0], NA_HEADS)])
    head_gain = head_gain.reshape(2, 1, NA_WIDTH).astype(F32)
    n_tok = b * (l + t)
    proj, *fresh = _inproj(x, n1, sh1, sc1, w_in, head_gain, ((n_tok, d // 2), (n_tok, ROUTER_PAD)))
    proj_c, = _inproj(ctx, n1, csh1, csc1, w_in, head_gain)
    kh, r0, row_type, patterns = _na_tables(t // GRID_W)
    bias_tab = _na_bias_table(ev_rpb[0], patterns)
    o_a = _neighbourhood_attention(proj, proj_c, bias_tab, r0, row_type, kh)
    oc_a = _context_attention(proj_c)
    w_r, b_r = router(0)
    c1, tokens, ids_all, gates_c, picks_c = _even_out(oc_a, proj_c, ev_conv_w[0], ev_conv_b[0], w_out, ctx, cg1, n2,
                                                      csh2, csc2, w_r, b_r, (n_tok, 0), fresh)
    x1, tokens, ids_all, gates, picks = _even_out(o_a, proj, ev_conv_w[0], ev_conv_b[0], w_out, x, g1, n2, sh2, sc2,
                                                  w_r, b_r, (n_tok, b * l), (tokens, ids_all))
    yb, dest = _moe(tokens, ids_all, _expert_counts(picks_c) + _expert_counts(picks), 0, w_exp)
    y_sel = _gather_choices(yb, dest, 0, b * (l + t))
    g2_prev, cg2_prev = g2, cg2

    (sh1, sc1, g1, sh2, sc2, g2), (csh1, csc1, _, _, _, _) = _mod_parts(mod[1], b)
    n1 = norm1_g[1].reshape(1, d)
    n2 = norm2_g[1].reshape(1, d)
    w_in = od_w_in[0].astype(BF16)
    width = w_in.shape[1] // 2
    x, proj = _combine_inproj_conv(x1, g2_prev, gates, y_sel, b * l, n1, sh1, sc1, w_in, od_conv_w[0], od_conv_b[0],
                                   width)
    _, u_ctx = _combine_inproj_conv(c1, cg2_prev, gates_c, y_sel, 0, n1, csh1, csc1, w_in[:, width:], od_conv_w[0],
                                    od_conv_b[0], 0)
    h_dir = []
    for reverse, (wa, ba, wx, bx, lam) in ((False, (od_fwd_wa, od_fwd_ba, od_fwd_wx, od_fwd_bx, od_fwd_lam)),
                                           (True, (od_bwd_wa, od_bwd_ba, od_bwd_wx, od_bwd_bx, od_bwd_lam))):
        w_cat = jnp.concatenate([wa[0], wx[0]], axis=-1).astype(BF16)
        h_dir.append(_lru_scan(proj, u_ctx, w_cat, ba[0], bx[0], lam[0], reverse))
    w_r, b_r = router(1)
    x1, h2, ids, gates, picks = _odd_out(h_dir[0], h_dir[1], proj, od_w_out[0].astype(BF16), x, g1, n2, sh2, sc2,
                                         w_r, b_r)
    yb, dest = _moe(h2, ids, _expert_counts(picks), 1, w_exp)
    return _combine(x1, g2, gates, _gather_choices(yb, dest, 0, b * t), 0)
```

```python
import functools

import numpy as np
import jax
import jax.numpy as jnp
from jax import lax
from jax.experimental import pallas as pl
from jax.experimental.pallas import tpu as pltpu
from jax.experimental.pallas import tpu_sc as plsc

DEPTH = 2
GRID_W = 64
EPS = 1e-6
NEG_INF = -1e30
LOG2_E = 1.4426950408889634
NA_HEADS = 8
NA_HEAD_DIM = 64
NA_WIDTH = NA_HEADS * NA_HEAD_DIM
HEAD_PAIRS = NA_HEADS // 2
WIN_H = 8
WIN_W = 16
SC_CONV = 3
LRU_BLOCKS = 4
LRU_CONV = 4
LRU_C = 8.0
N_EXPERTS = 32
TOP_K = 4
SWIGLU_LIMIT = 7.0
SWIGLU_ALPHA = 1.702

LANES = 128
SUBLANES = 8
MXU_DEPTH = 256
HALO = 16
ROUTER_PAD = LANES
ADA_COL_TILE = 1536
ROW_TILE = 1024
NA_ROWS_PER_STEP = 8
LRU_SUB_TILES = 4
OUT_TILE = 1024
OUT_SUB_TILE = 512
ROUTE_TILE = 2048
EXPERT_TILE = 2048
EXPERT_SUB = 512
SC_GATHER_ROWS = 64
SC_SCATTER_ROWS = 128
VMEM_LIMIT = 56 * 1024 * 1024

F32 = jnp.float32
BF16 = jnp.bfloat16


def _params(sem, vmem=VMEM_LIMIT):
    return pltpu.CompilerParams(dimension_semantics=sem, vmem_limit_bytes=vmem)


def _dot(a, b):
    return jnp.dot(a, b, preferred_element_type=F32)


def _dot_nt(a, b):
    return lax.dot_general(a, b, (((1,), (1,)), ((), ())), preferred_element_type=F32)


def _pack_rows(v):
    w = v.shape[-1] // 2
    lo = lax.bitcast_convert_type(v[:, :w].astype(BF16).astype(F32), jnp.int32)
    hi = lax.bitcast_convert_type(v[:, w:].astype(BF16).astype(F32), jnp.int32)
    return lax.shift_right_logical(lo, 16) | (hi & jnp.int32(-65536))


def _unpack_rows(p):
    lo = lax.bitcast_convert_type(lax.shift_left(p, 16), F32)
    hi = lax.bitcast_convert_type(p & jnp.int32(-65536), F32)
    return lo, hi


def _rms_mod(x, g, shift, scale):
    ms = jnp.mean(x * x, axis=-1, keepdims=True)
    return (x * lax.rsqrt(ms + EPS)) * (g * (1.0 + scale)) + shift


def _ada_kernel(c_ref, w_ref, b_ref, o_ref):
    c = c_ref[...]
    s = (c * jax.nn.sigmoid(c)).astype(BF16)
    o_ref[0] = _dot(s, w_ref[0].astype(BF16)) + b_ref[0]


def _ada_mod(cvec, ada_w, ada_b):
    depth, d, n = ada_w.shape
    r = cvec.shape[0]
    tn = ADA_COL_TILE
    return pl.pallas_call(
        _ada_kernel,
        out_shape=jax.ShapeDtypeStruct((depth, r, n), F32),
        grid=(depth, n // tn),
        in_specs=[
            pl.BlockSpec((r, d), lambda l, j: (0, 0)),
            pl.BlockSpec((1, d, tn), lambda l, j: (l, 0, j)),
            pl.BlockSpec((1, 1, tn), lambda l, j: (l, 0, j)),
        ],
        out_specs=pl.BlockSpec((1, r, tn), lambda l, j: (l, 0, j)),
        compiler_params=_params(("parallel", "parallel")),
        name="ada_mod",
    )(cvec, ada_w, ada_b.reshape(depth, 1, n))


def _inproj_kernel(x_ref, g_ref, sh_ref, sc_ref, w_ref, hg_ref, ones_ref, o_ref, *zero_refs, n_tiles, tn,
                   n_headnorm):
    for z_ref in zero_refs:
        z_ref[...] = jnp.zeros_like(z_ref)
    h = _rms_mod(x_ref[0], g_ref[...], sh_ref[0], sc_ref[0]).astype(BF16)
    for j in range(n_tiles):
        y = _dot(h, w_ref[:, j * tn:(j + 1) * tn])
        if j < n_headnorm:
            ysq = (y * y).astype(BF16)
            kw = ones_ref.shape[0]
            ms = jnp.concatenate([_dot(ysq[:, c:c + kw], ones_ref[...]) for c in range(0, tn, kw)], axis=1)
            y = y * lax.rsqrt(ms * (1.0 / NA_HEAD_DIM) + EPS) * hg_ref[j]
        o_ref[0, :, j * tn:(j + 1) * tn] = y.astype(o_ref.dtype)


def _inproj(x, g, shift, scale, w, head_gain, zero_shapes=()):
    b, t, d = x.shape
    n = w.shape[1]
    tn = NA_WIDTH
    tm = min(t, ROW_TILE)
    n_t = t // tm
    n_headnorm = head_gain.shape[0]
    hid = np.arange(MXU_DEPTH) // NA_HEAD_DIM
    ones_bd = jnp.asarray((hid[:, None] == hid[None, :]), BF16)
    kern = functools.partial(_inproj_kernel, n_tiles=n // tn, tn=tn, n_headnorm=n_headnorm)
    steps = b * n_t
    assert all(rows % (steps * SUBLANES) == 0 for rows, _ in zero_shapes)
    zero_specs = [pl.BlockSpec((rows // steps, cols), lambda bi, i: (bi * n_t + i, 0)) for rows, cols in zero_shapes]
    zero_out = [jax.ShapeDtypeStruct(shape, jnp.int32) for shape in zero_shapes]
    return pl.pallas_call(
        kern,
        out_shape=[jax.ShapeDtypeStruct((b, t, n), BF16)] + zero_out,
        grid=(b, n_t),
        in_specs=[
            pl.BlockSpec((1, tm, d), lambda bi, i: (bi, i, 0)),
            pl.BlockSpec((1, d), lambda bi, i: (0, 0)),
            pl.BlockSpec((1, 1, d), lambda bi, i: (bi, 0, 0)),
            pl.BlockSpec((1, 1, d), lambda bi, i: (bi, 0, 0)),
            pl.BlockSpec((d, n), lambda bi, i: (0, 0)),
            pl.BlockSpec(head_gain.shape, lambda bi, i: (0, 0, 0)),
            pl.BlockSpec((MXU_DEPTH, MXU_DEPTH), lambda bi, i: (0, 0)),
        ],
        out_specs=[pl.BlockSpec((1, tm, n), lambda bi, i: (bi, i, 0))] + zero_specs,
        compiler_params=_params(("parallel", "parallel")),
        name="inproj",
    )(x, g, shift, scale, w, head_gain, ones_bd)


def _moe_residual(x, g2, gates, y):
    acc_lo = acc_hi = 0.0
    for k in range(TOP_K):
        lo, hi = _unpack_rows(y[k])
        acc_lo = acc_lo + gates[:, k:k + 1] * lo
        acc_hi = acc_hi + gates[:, k:k + 1] * hi
    return x + g2 * jnp.concatenate([acc_lo, acc_hi], axis=1)


def _combine_inproj_conv_kernel(x_ref, xp_ref, xn_ref, gt_ref, gtp_ref, gtn_ref, y_ref, yp_ref, yn_ref, g2_ref,
                                g_ref, sh_ref, sc_ref, w_ref, cw_ref, cb_ref, xo_ref, o_ref, *,
                                n_tiles, n_plain, tn, tm, sub):
    i = pl.program_id(1)
    edge = SUBLANES
    g2 = g2_ref[0]
    x_tile = _moe_residual(x_ref[0], g2, gt_ref[0], y_ref[...])
    xo_ref[0] = x_tile
    x_ext = jnp.concatenate([_moe_residual(xp_ref[0], g2, gtp_ref[0], yp_ref[...]), x_tile,
                             _moe_residual(xn_ref[0], g2, gtn_ref[0], yn_ref[...])], axis=0)
    h = _rms_mod(x_ext, g_ref[...], sh_ref[0], sc_ref[0]).astype(BF16)
    first_step = i == 0
    last_step = i == pl.num_programs(1) - 1
    left = (LRU_CONV - 1) // 2
    n_sub = tm // sub
    for s in range(n_sub):
        h_s = h[s * sub:(s + 1) * sub + 2 * edge]
        rows = slice(s * sub, (s + 1) * sub)
        keep_prev = jnp.where(first_step, 0.0, 1.0) if s == 0 else 1.0
        keep_next = jnp.where(last_step, 0.0, 1.0) if s == n_sub - 1 else 1.0
        for j in range(n_tiles):
            y = _dot(h_s, w_ref[:, j * tn:(j + 1) * tn])
            if j < n_plain:
                o_ref[0, rows, j * tn:(j + 1) * tn] = y[edge:edge + sub].astype(o_ref.dtype)
                continue
            u = jnp.concatenate([y[:edge] * keep_prev, y[edge:edge + sub], y[edge + sub:] * keep_next], axis=0)
            cw = cw_ref[:, (j - n_plain) * tn:(j - n_plain + 1) * tn]
            uc = cb_ref[:, (j - n_plain) * tn:(j - n_plain + 1) * tn]
            n_ext = u.shape[0]
            for k in range(LRU_CONV):
                shifted = u if k == left else pltpu.roll(u, (left - k) % n_ext, 0)
                uc = uc + shifted[edge:edge + sub] * cw[k:k + 1]
            o_ref[0, rows, j * tn:(j + 1) * tn] = uc.astype(o_ref.dtype)


def _combine_inproj_conv(x1, g2, gates, y_sel, tok_offset, g, shift, scale, w, conv_w, conv_b, n_plain_cols):
    b, t, d = x1.shape
    n = w.shape[1]
    tn = NA_WIDTH
    tm = min(t, OUT_TILE)
    hb = tm // SUBLANES
    n_hblocks = t // SUBLANES
    y_hblocks = y_sel.shape[1] // SUBLANES
    assert tok_offset % tm == 0
    kern = functools.partial(_combine_inproj_conv_kernel, n_tiles=n // tn, n_plain=n_plain_cols // tn, tn=tn, tm=tm,
                             sub=min(tm, OUT_SUB_TILE))
    const = lambda bi, i: (0, 0)
    tile = lambda bi, i: (bi, i, 0)
    prev = lambda bi, i: (bi, jnp.maximum(i * hb - 1, 0), 0)
    nxt = lambda bi, i: (bi, jnp.minimum((i + 1) * hb, n_hblocks - 1), 0)
    y_row = lambda bi, i: (tok_offset + bi * t + i * tm) // SUBLANES
    per_b = lambda bi, i: (bi, 0, 0)
    return pl.pallas_call(
        kern,
        out_shape=[jax.ShapeDtypeStruct((b, t, d), F32), jax.ShapeDtypeStruct((b, t, n), BF16)],
        grid=(b, t // tm),
        in_specs=[
            pl.BlockSpec((1, tm, d), tile),
            pl.BlockSpec((1, SUBLANES, d), prev),
            pl.BlockSpec((1, SUBLANES, d), nxt),
            pl.BlockSpec((1, tm, ROUTER_PAD), tile),
            pl.BlockSpec((1, SUBLANES, ROUTER_PAD), prev),
            pl.BlockSpec((1, SUBLANES, ROUTER_PAD), nxt),
            pl.BlockSpec((TOP_K, tm, d // 2), lambda bi, i: (0, y_row(bi, i) // hb, 0)),
            pl.BlockSpec((TOP_K, SUBLANES, d // 2), lambda bi, i: (0, jnp.maximum(y_row(bi, i) - 1, 0), 0)),
            pl.BlockSpec((TOP_K, SUBLANES, d // 2),
                         lambda bi, i: (0, jnp.minimum(y_row(bi, i) + hb, y_hblocks - 1), 0)),
            pl.BlockSpec((1, 1, d), per_b),
            pl.BlockSpec((1, d), const),
            pl.BlockSpec((1, 1, d), per_b),
            pl.BlockSpec((1, 1, d), per_b),
            pl.BlockSpec((d, n), const),
            pl.BlockSpec(conv_w.shape, const),
            pl.BlockSpec((1, conv_w.shape[1]), const),
        ],
        out_specs=[pl.BlockSpec((1, tm, d), tile), pl.BlockSpec((1, tm, n), tile)],
        compiler_params=_params(("parallel", "parallel")),
        name="combine_inproj_conv",
    )(x1, x1, x1, gates, gates, gates, y_sel, y_sel, y_sel, g2, g, shift, scale, w, conv_w, conv_b.reshape(1, -1))


def _na_tables(rows):
    kh = min(WIN_H, rows)
    r = np.arange(rows)
    r0 = np.clip(r - kh // 2, 0, rows - kh)
    dr = r0[:, None] + np.arange(kh)[None] - r[:, None] + WIN_H - 1
    patterns, row_type = np.unique(dr, axis=0, return_inverse=True)
    return kh, r0.astype(np.int32), row_type.reshape(-1).astype(np.int32), patterns


def _na_bias_table(rpb, patterns):
    qc = np.arange(GRID_W)
    kc = np.arange(GRID_W)
    c0 = np.clip(qc - WIN_W // 2, 0, GRID_W - WIN_W)[:, None]
    valid = (kc[None] >= c0) & (kc[None] < c0 + WIN_W)
    dc = np.clip(kc[None] - qc[:, None] + WIN_W - 1, 0, 2 * WIN_W - 2)
    n_pat, kh = patterns.shape
    onehot_dc = jnp.asarray(dc[None] == np.arange(2 * WIN_W - 1)[:, None, None], F32)
    tab = jnp.einsum('hpic,cqk->phqik', rpb.astype(F32)[:, patterns], onehot_dc,
                     precision=lax.Precision.HIGHEST)
    tab = jnp.where(valid[None, None, :, None, :], tab * LOG2_E, NEG_INF)
    return tab.reshape(n_pat, HEAD_PAIRS, 2 * GRID_W, kh * GRID_W)


def _pair_attention(q, keys, values, biases):
    m = q.shape[0]
    qq = _stack_heads(q)
    scores = []
    for k, bias in zip(keys, biases):
        s = _dot_nt(qq, k)
        scores.append(s if bias is None else s + bias)
    s = jnp.concatenate(scores, axis=1)
    e = jnp.exp2(s - jnp.max(s, axis=-1, keepdims=True))
    denom = jnp.sum(e, axis=-1, keepdims=True)
    e = e.astype(BF16)
    o, start = 0.0, 0
    for v in values:
        o = o + _dot(e[:, start:start + v.shape[0]], v)
        start += v.shape[0]
    o = o * (1.0 / denom)
    lane_o = lax.broadcasted_iota(jnp.int32, (m, LANES), 1)
    return jnp.where(lane_o < NA_HEAD_DIM, o[:m], o[m:])


def _stack_heads(q):
    lane = lax.broadcasted_iota(jnp.int32, q.shape, 1)
    zero = jnp.zeros_like(q)
    return jnp.concatenate([jnp.where(lane < NA_HEAD_DIM, q, zero), jnp.where(lane >= NA_HEAD_DIM, q, zero)], axis=0)


def _na_kernel(r0_ref, type_ref, q_ref, k_ref, v_ref, kc_ref, vc_ref, bias_ref, o_ref, s_ref, p_ref, *,
               kh, rows_per_step):
    n_lat = kh * GRID_W
    tiles = [(j, p) for j in range(rows_per_step) for p in range(HEAD_PAIRS)]
    window = []
    for j in range(rows_per_step):
        r = pl.program_id(1) * rows_per_step + j
        window.append((pl.multiple_of(r0_ref[r] * GRID_W, GRID_W), type_ref[r]))

    for idx, (j, p) in enumerate(tiles):
        start, rtype = window[j]
        cols = slice(p * LANES, (p + 1) * LANES)
        qq = _stack_heads(q_ref[0, j * GRID_W:(j + 1) * GRID_W, cols])
        s_ref[idx, :, :n_lat] = _dot_nt(qq, k_ref[0, pl.ds(start, n_lat), cols]) + bias_ref[rtype, p]
        s_ref[idx, :, n_lat:] = _dot_nt(qq, kc_ref[0, :, cols])

    denoms = []
    for idx in range(len(tiles)):
        s = s_ref[idx]
        e = jnp.exp2(s - jnp.max(s, axis=-1, keepdims=True))
        denoms.append(jnp.sum(e, axis=-1, keepdims=True))
        p_ref[idx] = e.astype(BF16)

    lane = lax.broadcasted_iota(jnp.int32, (GRID_W, LANES), 1)
    for idx, (j, p) in enumerate(tiles):
        start, _ = window[j]
        cols = slice(p * LANES, (p + 1) * LANES)
        o = _dot(p_ref[idx, :, :n_lat], v_ref[0, pl.ds(start, n_lat), cols]) + _dot(p_ref[idx, :, n_lat:],
                                                                                 vc_ref[0, :, cols])
        o = o * (1.0 / denoms[idx])
        o = jnp.where(lane < NA_HEAD_DIM, o[:GRID_W], o[GRID_W:])
        o_ref[0, j * GRID_W:(j + 1) * GRID_W, cols] = o.astype(o_ref.dtype)


def _neighbourhood_attention(proj, proj_c, bias_tab, r0, row_type, kh):
    b, t, _ = proj.shape
    l = proj_c.shape[1]
    rows = t // GRID_W
    w = NA_WIDTH
    rps = int(np.gcd(rows, NA_ROWS_PER_STEP))
    q_rows = rps * GRID_W
    grid_spec = pltpu.PrefetchScalarGridSpec(
        num_scalar_prefetch=2,
        grid=(b, rows // rps),
        in_specs=[
            pl.BlockSpec((1, q_rows, w), lambda bi, r, *_: (bi, r, 0)),
            pl.BlockSpec((1, t, w), lambda bi, r, *_: (bi, 0, 1)),
            pl.BlockSpec((1, t, w), lambda bi, r, *_: (bi, 0, 2)),
            pl.BlockSpec((1, l, w), lambda bi, r, *_: (bi, 0, 1)),
            pl.BlockSpec((1, l, w), lambda bi, r, *_: (bi, 0, 2)),
            pl.BlockSpec(bias_tab.shape, lambda bi, r, *_: (0, 0, 0, 0)),
        ],
        out_specs=pl.BlockSpec((1, q_rows, w), lambda bi, r, *_: (bi, r, 0)),
        scratch_shapes=[pltpu.VMEM((rps * HEAD_PAIRS, 2 * GRID_W, kh * GRID_W + l), F32),
                        pltpu.VMEM((rps * HEAD_PAIRS, 2 * GRID_W, kh * GRID_W + l), BF16)],
    )
    return pl.pallas_call(
        functools.partial(_na_kernel, kh=kh, rows_per_step=rps),
        out_shape=jax.ShapeDtypeStruct((b, t, w), BF16),
        grid_spec=grid_spec,
        compiler_params=_params(("parallel", "arbitrary")),
        name="na_attention",
    )(jnp.asarray(r0), jnp.asarray(row_type), proj, proj, proj, proj_c, proj_c, bias_tab)


def _ctx_attn_kernel(q_ref, k_ref, v_ref, o_ref):
    for p in range(HEAD_PAIRS):
        cols = slice(p * LANES, (p + 1) * LANES)
        o = _pair_attention(q_ref[0, :, cols], [k_ref[0, :, cols]], [v_ref[0, :, cols]], [None])
        o_ref[0, :, cols] = o.astype(o_ref.dtype)


def _context_attention(proj_c):
    b, l, _ = proj_c.shape
    w = NA_WIDTH
    return pl.pallas_call(
        _ctx_attn_kernel,
        out_shape=jax.ShapeDtypeStruct((b, l, w), BF16),
        grid=(b,),
        in_specs=[pl.BlockSpec((1, l, w), lambda bi, j=j: (bi, 0, j)) for j in range(3)],
        out_specs=pl.BlockSpec((1, l, w), lambda bi: (bi, 0, 0)),
        compiler_params=_params(("parallel",)),
        name="ctx_attention",
    )(proj_c, proj_c, proj_c)


def _top4(logits):
    lane = lax.broadcasted_iota(jnp.int32, logits.shape, 1)
    cur = jnp.where(lane < N_EXPERTS, logits, -jnp.inf)
    picked = jnp.zeros(logits.shape, F32)
    vals, idxs = [], []
    for _ in range(TOP_K):
        m = jnp.max(cur, axis=-1, keepdims=True)
        first = jnp.min(jnp.where(cur == m, lane, ROUTER_PAD).astype(F32), axis=-1, keepdims=True)
        idx = first.astype(jnp.int32)
        vals.append(m)
        idxs.append(idx)
        hit = lane == idx
        picked = jnp.where(hit, 1.0, picked)
        cur = jnp.where(hit, -jnp.inf, cur)
    exps = [jnp.exp(v - vals[0]) for v in vals]
    inv = 1.0 / functools.reduce(jnp.add, exps)
    ids = jnp.zeros(logits.shape, jnp.int32)
    gates = jnp.zeros(logits.shape, F32)
    for k in range(TOP_K):
        ids = jnp.where(lane == k, idxs[k], ids)
        gates = jnp.where(lane == k, exps[k] * inv, gates)
    return ids, gates, jnp.sum(picked, axis=0, keepdims=True)


def _layer_tail(y, rows, x_ref, g1_ref, n2_ref, sh2_ref, sc2_ref, wr_ref, br_ref, *rest):
    xo_ref, h2_ref, ids_ref, gates_ref = rest[-5:-1]
    x_new = x_ref[0, rows] + g1_ref[0] * y
    xo_ref[0, rows] = x_new
    h2 = _rms_mod(x_new, n2_ref[...], sh2_ref[0], sc2_ref[0])
    h2_ref[rows] = _pack_rows(h2)
    logits = _dot(h2.astype(BF16), wr_ref[...]) + br_ref[...]
    ids, gates, counts = _top4(logits)
    ids_ref[rows] = ids
    gates_ref[0, rows] = gates
    return counts


def _sub_tiles(n_rows):
    sub = min(OUT_SUB_TILE, n_rows)
    return [slice(s, s + sub) for s in range(0, n_rows, sub)]


def _tail_specs(b, t, d, tm, n_inputs_before, tokens, shared):
    n_total, offset = tokens
    n_t = t // tm
    assert offset % tm == 0
    row = lambda bi, i: (bi, i, 0)
    flat = lambda bi, i: (offset // tm + bi * n_t + i, 0)
    per_b = lambda bi, i: (bi, 0, 0)
    const = lambda bi, i: (0, 0)
    extra, aliases = (), {}
    if shared is not None:
        extra = tuple(shared)
        aliases = {n_inputs_before + 7: 1, n_inputs_before + 8: 2}
    in_specs = [
        pl.BlockSpec((1, tm, d), row),
        pl.BlockSpec((1, 1, d), per_b),
        pl.BlockSpec((1, d), const),
        pl.BlockSpec((1, 1, d), per_b),
        pl.BlockSpec((1, 1, d), per_b),
        pl.BlockSpec((d, ROUTER_PAD), const),
        pl.BlockSpec((1, ROUTER_PAD), const),
    ] + [pl.BlockSpec(memory_space=pl.ANY)] * len(extra)
    out_specs = [
        pl.BlockSpec((1, tm, d), row),
        pl.BlockSpec((tm, d // 2), flat),
        pl.BlockSpec((tm, ROUTER_PAD), flat),
        pl.BlockSpec((1, tm, ROUTER_PAD), row),
        pl.BlockSpec((1, SUBLANES, ROUTER_PAD), lambda bi, i: (bi * n_t + i, 0, 0)),
    ]
    out_shape = [
        jax.ShapeDtypeStruct((b, t, d), F32),
        jax.ShapeDtypeStruct((n_total, d // 2), jnp.int32),
        jax.ShapeDtypeStruct((n_total, ROUTER_PAD), jnp.int32),
        jax.ShapeDtypeStruct((b, t, ROUTER_PAD), F32),
        jax.ShapeDtypeStruct((b * n_t, SUBLANES, ROUTER_PAD), F32),
    ]
    return in_specs, out_specs, out_shape, extra, aliases


def _halo_fix(rolled, at_row, halo_row, present):
    n = rolled.shape[0]
    first = at_row < SUBLANES
    assert first or at_row >= n - SUBLANES
    slab = rolled[:SUBLANES] if first else rolled[n - SUBLANES:]
    sub = lax.broadcasted_iota(jnp.int32, slab.shape, 0)
    fill = jnp.where(present, halo_row, jnp.zeros_like(halo_row))
    slab = jnp.where(sub == at_row % SUBLANES, fill, slab)
    return jnp.concatenate([slab, rolled[SUBLANES:]] if first else [rolled[:n - SUBLANES], slab], axis=0)


def _even_out_kernel(oa_ref, bg_ref, cg_ref, xin_ref, cgp_ref, xinp_ref, cgn_ref, xinn_ref, cw_ref, cb_ref,
                     wa_ref, wb_ref, *tail_refs, tm):
    i = pl.program_id(1)
    has_prev = i > 0
    has_next = i < pl.num_programs(1) - 1
    u = cg_ref[0].astype(F32) * xin_ref[0].astype(F32)
    u_prev = (cgp_ref[0].astype(F32) * xinp_ref[0].astype(F32))[HALO - 1:HALO]
    u_next = (cgn_ref[0].astype(F32) * xinn_ref[0].astype(F32))[0:1]
    u_m1 = _halo_fix(pltpu.roll(u, 1, 0), 0, u_prev, has_prev)
    u_p1 = _halo_fix(pltpu.roll(u, tm - 1, 0), tm - 1, u_next, has_next)
    cw = cw_ref[...]
    conv = u_m1 * cw[0:1] + u * cw[1:2] + u_p1 * cw[2:3] + cb_ref[...]
    o_b = (bg_ref[0].astype(F32) * conv).astype(BF16)
    counts = 0.0
    for rows in _sub_tiles(tm):
        y = _dot(oa_ref[0, rows], wa_ref[...]) + _dot(o_b[rows], wb_ref[...])
        counts = counts + _layer_tail(y, rows, *tail_refs)
    tail_refs[-1][0] = jnp.broadcast_to(counts, tail_refs[-1].shape[1:])


def _even_out(o_a, proj, conv_w, conv_b, w_out, x, g1, n2, sh2, sc2, w_r, b_r, tokens, shared=None):
    b, t, d = x.shape
    w = NA_WIDTH
    tm = min(t, OUT_TILE)
    hb = tm // HALO
    n_hblocks = t // HALO
    row = lambda bi, i: (bi, i, 0)
    const = lambda bi, i: (0, 0)
    prev = lambda col: (lambda bi, i: (bi, jnp.maximum(i * hb - 1, 0), col))
    nxt = lambda col: (lambda bi, i: (bi, jnp.minimum((i + 1) * hb, n_hblocks - 1), col))
    tail_in, out_specs, out_shape, extra, aliases = _tail_specs(b, t, d, tm, 12, tokens, shared)
    in_specs = [
        pl.BlockSpec((1, tm, w), row),
        pl.BlockSpec((1, tm, w), lambda bi, i: (bi, i, 3)),
        pl.BlockSpec((1, tm, w), lambda bi, i: (bi, i, 4)),
        pl.BlockSpec((1, tm, w), lambda bi, i: (bi, i, 5)),
        pl.BlockSpec((1, HALO, w), prev(4)),
        pl.BlockSpec((1, HALO, w), prev(5)),
        pl.BlockSpec((1, HALO, w), nxt(4)),
        pl.BlockSpec((1, HALO, w), nxt(5)),
        pl.BlockSpec((SC_CONV, w), const),
        pl.BlockSpec((1, w), const),
        pl.BlockSpec((w, d), const),
        pl.BlockSpec((w, d), const),
    ] + tail_in
    return pl.pallas_call(
        functools.partial(_even_out_kernel, tm=tm),
        out_shape=out_shape,
        grid=(b, t // tm),
        in_specs=in_specs,
        out_specs=out_specs,
        input_output_aliases=aliases,
        compiler_params=_params(("parallel", "parallel")),
        name="even_out",
    )(o_a, proj, proj, proj, proj, proj, proj, proj, conv_w, conv_b.reshape(1, w),
      w_out[:w], w_out[w:], x, g1, n2, sh2, sc2, w_r, b_r, *extra)


def _log_sigmoid(x):
    return jnp.minimum(x, 0.0) - jnp.log1p(jnp.exp(-jnp.abs(x)))


def _sigmoid(x):
    return 0.5 + 0.5 * jnp.tanh(0.5 * x)


def _lru_sub_tile(ucb_time, state, perm_ref, w_ref, ba_ref, bx_ref, lam_ref, want_hidden, reverse):
    sub, width = ucb_time.shape
    blk = width // LRU_BLOCKS
    steps = sub // SUBLANES
    uc = _dot(perm_ref[0], ucb_time)
    ucb = uc.astype(BF16)
    za, zx = [], []
    for h in range(LRU_BLOCKS):
        z = _dot(ucb[:, h * blk:(h + 1) * blk], w_ref[h])
        za.append(z[:, :blk])
        zx.append(z[:, blk:])
    r = _sigmoid(jnp.concatenate(za, axis=1) + ba_ref[...])
    gate_i = _sigmoid(jnp.concatenate(zx, axis=1) + bx_ref[...])
    log_a = (LRU_C * _log_sigmoid(lam_ref[...])) * r
    a = jnp.exp(log_a)
    th = jnp.tanh(log_a)
    num = -2.0 * th
    mult = jnp.where(num > 0.0, num * lax.rsqrt(num * (1.0 - th)), 0.0)
    bcoef = mult * gate_i * uc

    grp = lambda v, j: v[j * SUBLANES:(j + 1) * SUBLANES]
    prods, local = [grp(a, 0)], [grp(bcoef, 0)]
    for j in range(1, steps):
        aj = grp(a, j)
        local.append(aj * local[-1] + grp(bcoef, j))
        prods.append(aj * prods[-1])

    carry_in = [None] * SUBLANES
    for s in (range(SUBLANES - 1, -1, -1) if reverse else range(SUBLANES)):
        carry_in[s] = state
        state = prods[-1][s:s + 1] * state + local[-1][s:s + 1]
    if not want_hidden:
        return state, None
    start = jnp.concatenate(carry_in, axis=0)
    hidden = jnp.concatenate([prods[j] * start + local[j] for j in range(steps)], axis=0)
    return state, _dot(perm_ref[1], hidden.astype(BF16)).astype(BF16)


def _lru_tile(u_ref, perm_ref, w_ref, ba_ref, bx_ref, lam_ref, carry_ref, o_ref, *, reverse):
    sub = perm_ref.shape[1]
    n_sub = u_ref.shape[1] // sub
    state = carry_ref[0:1, :]
    for k in (range(n_sub - 1, -1, -1) if reverse else range(n_sub)):
        rows = slice(k * sub, (k + 1) * sub)
        state, hidden = _lru_sub_tile(u_ref[0, rows], state, perm_ref, w_ref, ba_ref, bx_ref, lam_ref,
                                      o_ref is not None, reverse)
        if o_ref is not None:
            o_ref[0, rows] = hidden
    carry_ref[...] = jnp.broadcast_to(state, carry_ref.shape)


def _lru_kernel(ucf_ref, ulf_ref, ucb_ref, ulb_ref, pf_ref, pb_ref, wf_ref, baf_ref, bxf_ref, lamf_ref,
                wb_ref, bab_ref, bxb_ref, lamb_ref, of_ref, ob_ref, cf_ref, cb_ref, *, n_ctx_tiles):
    j = pl.program_id(1)
    fwd = (pf_ref, wf_ref, baf_ref, bxf_ref, lamf_ref, cf_ref)
    bwd = (pb_ref, wb_ref, bab_ref, bxb_ref, lamb_ref, cb_ref)

    @pl.when(j == 0)
    def _():
        cf_ref[...] = jnp.zeros_like(cf_ref)
        cb_ref[...] = jnp.zeros_like(cb_ref)

    @pl.when(j < n_ctx_tiles)
    def _():
        _lru_tile(ucf_ref, *fwd, None, reverse=False)
        _lru_tile(ucb_ref, *bwd, None, reverse=True)

    @pl.when(j >= n_ctx_tiles)
    def _():
        _lru_tile(ulf_ref, *fwd, of_ref, reverse=False)
        _lru_tile(ulb_ref, *bwd, ob_ref, reverse=True)


def _scan_perms(sub, reverse):
    steps = sub // SUBLANES
    step, block = np.divmod(np.arange(sub), SUBLANES)
    time_of_row = block * steps + (steps - 1 - step if reverse else step)
    perm = (time_of_row[:, None] == np.arange(sub)[None]).astype(np.float32)
    return jnp.asarray(np.stack([perm, perm.T]), BF16)


def _lru_scan(proj, u_ctx, fwd, bwd):
    b, t, _ = proj.shape
    l, width = u_ctx.shape[1], u_ctx.shape[2]
    sub = min(256, l, t)
    tc = sub * min(LRU_SUB_TILES, l // sub)
    tl = sub * min(LRU_SUB_TILES, t // sub)
    n_c, n_l = l // tc, t // tl

    def tile_map(off, n, col, reverse):
        def index(bi, j):
            step = jnp.clip(j - off, 0, n - 1)
            return bi, (n - 1 - step) if reverse else step, col
        return index

    const2 = lambda bi, j: (0, 0)
    const3 = lambda bi, j: (0, 0, 0)
    vec = pl.BlockSpec((1, width), const2)
    in_specs = [
        pl.BlockSpec((1, tc, width), tile_map(0, n_c, 0, False)),
        pl.BlockSpec((1, tl, width), tile_map(n_c, n_l, 1, False)),
        pl.BlockSpec((1, tc, width), tile_map(0, n_c, 0, True)),
        pl.BlockSpec((1, tl, width), tile_map(n_c, n_l, 1, True)),
        pl.BlockSpec((2, sub, sub), const3),
        pl.BlockSpec((2, sub, sub), const3),
        pl.BlockSpec(fwd[0].shape, const3), vec, vec, vec,
        pl.BlockSpec(bwd[0].shape, const3), vec, vec, vec,
    ]
    row = lambda v: v.reshape(1, width)
    return pl.pallas_call(
        functools.partial(_lru_kernel, n_ctx_tiles=n_c),
        out_shape=[jax.ShapeDtypeStruct((b, t, width), BF16)] * 2,
        grid=(b, n_c + n_l),
        in_specs=in_specs,
        out_specs=[pl.BlockSpec((1, tl, width), tile_map(n_c, n_l, 0, False)),
                   pl.BlockSpec((1, tl, width), tile_map(n_c, n_l, 0, True))],
        scratch_shapes=[pltpu.VMEM((SUBLANES, width), F32), pltpu.VMEM((SUBLANES, width), F32)],
        compiler_params=_params(("parallel", "arbitrary")),
        name="lru_scan",
    )(u_ctx, proj, u_ctx, proj, _scan_perms(sub, False), _scan_perms(sub, True),
      fwd[0], row(fwd[1]), row(fwd[2]), row(fwd[3]), bwd[0], row(bwd[1]), row(bwd[2]), row(bwd[3]))


def _odd_out_kernel(hf_ref, hb_ref, gate_ref, w_ref, *tail_refs):
    counts = 0.0
    for rows in _sub_tiles(hf_ref.shape[1]):
        hsum = hf_ref[0, rows].astype(F32) + hb_ref[0, rows].astype(F32)
        z = hsum * jax.nn.gelu(gate_ref[0, rows].astype(F32), approximate=True)
        y = _dot(z.astype(BF16), w_ref[...])
        counts = counts + _layer_tail(y, rows, *tail_refs)
    tail_refs[-1][0] = jnp.broadcast_to(counts, tail_refs[-1].shape[1:])


def _odd_out(h_f, h_b, proj, w_out, x, g1, n2, sh2, sc2, w_r, b_r):
    b, t, d = x.shape
    width = h_f.shape[-1]
    tm = min(t, OUT_TILE)
    row = lambda bi, i: (bi, i, 0)
    tail_in, out_specs, out_shape, _, _ = _tail_specs(b, t, d, tm, 4, (b * t, 0), None)
    in_specs = [
        pl.BlockSpec((1, tm, width), row),
        pl.BlockSpec((1, tm, width), row),
        pl.BlockSpec((1, tm, width), row),
        pl.BlockSpec((width, d), lambda bi, i: (0, 0)),
    ] + tail_in
    return pl.pallas_call(
        _odd_out_kernel,
        out_shape=out_shape,
        grid=(b, t // tm),
        in_specs=in_specs,
        out_specs=out_specs,
        compiler_params=_params(("parallel", "parallel")),
        name="odd_out",
    )(h_f, h_b, proj, w_out, x, g1, n2, sh2, sc2, w_r, b_r)


def _expert_kernel(be_ref, next_ref, rows_ref, nb_ref, x_ref, wg_hbm, bg_ref, wu_hbm, bu_ref, wd_hbm, bd_ref,
                   o_ref, w_f32, wg_bf, wu_bf, wd_bf, h_bf, sem, *, layer, chunk):
    i = pl.program_id(0)
    expert = be_ref[i]
    used = i < nb_ref[0]

    def weight_copies(e):
        return [pltpu.make_async_copy(src.at[layer, e], w_f32.at[j], sem.at[j])
                for j, src in enumerate((wg_hbm, wu_hbm, wd_hbm))]

    @pl.when(i == 0)
    def _():
        for cp in weight_copies(expert):
            cp.start()

    @pl.when(used & ((i == 0) | (expert != be_ref[jnp.maximum(i - 1, 0)])))
    def _():
        for cp in weight_copies(expert):
            cp.wait()
        wg_bf[...] = w_f32[0].astype(BF16)
        wu_bf[...] = w_f32[1].astype(BF16)
        wd_bf[...] = w_f32[2].astype(BF16)

        @pl.when(next_ref[i] >= 0)
        def _():
            for cp in weight_copies(next_ref[i]):
                cp.start()

    sub = h_bf.shape[0]
    n_sub = x_ref.shape[0] // sub
    n_rows = jnp.where(used, rows_ref[i], 0)
    n_live = (n_rows + sub - 1) // sub

    def live_sub_block(s, carry):
        rows = pl.ds(pl.multiple_of(s * sub, sub), sub)
        xp = x_ref[rows]
        row = lax.broadcasted_iota(jnp.int32, xp.shape, 0) + s * sub
        xp = jnp.where(row < n_rows, xp, 0)
        x = jnp.concatenate(_unpack_rows(xp), axis=1).astype(BF16)
        for c in range(wg_bf.shape[1] // chunk):
            cs = slice(c * chunk, (c + 1) * chunk)
            g = jnp.minimum(_dot(x, wg_bf[:, cs]) + bg_ref[0, 0, :, cs], SWIGLU_LIMIT)
            u = jnp.clip(_dot(x, wu_bf[:, cs]) + bu_ref[0, 0, :, cs], -SWIGLU_LIMIT, SWIGLU_LIMIT)
            h_bf[:, cs] = (g * jax.nn.sigmoid(SWIGLU_ALPHA * g) * (u + 1.0)).astype(BF16)
        o_ref[rows] = _pack_rows(_dot(h_bf[...], wd_bf[...]) + bd_ref[0, 0])
        return carry

    def empty_sub_block(s, carry):
        o_ref[pl.ds(pl.multiple_of(s * sub, sub), sub)] = jnp.zeros((sub, o_ref.shape[1]), o_ref.dtype)
        return carry

    lax.fori_loop(0, n_live, live_sub_block, 0)
    lax.fori_loop(n_live, n_sub, empty_sub_block, 0)


def _experts(xb, block_e, next_e, block_rows, n_used, layer, wg, bg, wu, bu, wd, bd):
    n_slots = xb.shape[0]
    depth, n_e, d, d_exp = wg.shape
    assert d == d_exp
    tm = EXPERT_TILE
    n_blocks = n_slots // tm
    xmap = lambda i, be, ne, br, nb: (jnp.minimum(i, nb[0] - 1), 0)
    bmap = lambda i, be, ne, br, nb: (layer, be[i], 0, 0)
    hbm = pl.BlockSpec(memory_space=pl.ANY)
    grid_spec = pltpu.PrefetchScalarGridSpec(
        num_scalar_prefetch=4,
        grid=(n_blocks,),
        in_specs=[
            pl.BlockSpec((tm, d // 2), xmap),
            hbm,
            pl.BlockSpec((1, 1, 1, d_exp), bmap),
            hbm,
            pl.BlockSpec((1, 1, 1, d_exp), bmap),
            hbm,
            pl.BlockSpec((1, 1, 1, d), bmap),
        ],
        out_specs=pl.BlockSpec((tm, d // 2), lambda i, be, ne, br, nb: (i, 0)),
        scratch_shapes=[pltpu.VMEM((3, d, d_exp), F32), pltpu.VMEM((d, d_exp), BF16), pltpu.VMEM((d, d_exp), BF16),
                        pltpu.VMEM((d_exp, d), BF16), pltpu.VMEM((EXPERT_SUB, d_exp), BF16),
                        pltpu.SemaphoreType.DMA((3,))],
    )
    return pl.pallas_call(
        functools.partial(_expert_kernel, layer=layer, chunk=256),
        out_shape=jax.ShapeDtypeStruct((n_slots, d // 2), jnp.int32),
        grid_spec=grid_spec,
        compiler_params=_params(("arbitrary",)),
        name="experts",
    )(block_e, next_e, block_rows, n_used, xb, wg, bg.reshape(depth, n_e, 1, d_exp), wu,
      bu.reshape(depth, n_e, 1, d_exp), wd, bd.reshape(depth, n_e, 1, d))


def _combine_kernel(x_ref, g2_ref, gates_ref, y_ref, o_ref):
    o_ref[0] = _moe_residual(x_ref[0], g2_ref[0], gates_ref[0], y_ref[...])


def _combine(x, g2, gates, y_sel, tok_offset):
    b, t, d = x.shape
    tm = int(np.gcd(min(t, ROW_TILE), tok_offset)) if tok_offset else min(t, ROW_TILE)
    n_t = t // tm
    off = tok_offset // tm
    row = lambda bi, i: (bi, i, 0)
    return pl.pallas_call(
        _combine_kernel,
        out_shape=jax.ShapeDtypeStruct((b, t, d), F32),
        grid=(b, n_t),
        in_specs=[
            pl.BlockSpec((1, tm, d), row),
            pl.BlockSpec((1, 1, d), lambda bi, i: (bi, 0, 0)),
            pl.BlockSpec((1, tm, ROUTER_PAD), row),
            pl.BlockSpec((TOP_K, tm, d // 2), lambda bi, i: (0, off + bi * n_t + i, 0)),
        ],
        out_specs=pl.BlockSpec((1, tm, d), row),
        input_output_aliases={0: 0},
        compiler_params=_params(("parallel", "parallel")),
        name="moe_combine",
    )(x, g2, gates, y_sel)


def _route_kernel(ids_ref, upper_ref, starts_ref, dest_ref, run_ref):
    i = pl.program_id(0)
    tm = ids_ref.shape[0]
    ids_t = ids_ref[...].T
    expert = lax.broadcasted_iota(jnp.int32, (N_EXPERTS, tm), 0)
    chosen = [ids_t[k:k + 1, :] == expert for k in range(TOP_K)]
    picks = functools.reduce(jnp.add, [c.astype(F32) for c in chosen])

    @pl.when(i == 0)
    def _():
        run_ref[...] = starts_ref[...]

    before = _dot(picks.astype(BF16), upper_ref[...])
    slot = before + run_ref[:, 0:1]
    rows = [jnp.sum(jnp.where(c, slot, 0.0), axis=0, keepdims=True) for c in chosen]
    rows += [jnp.zeros_like(rows[0])] * (dest_ref.shape[0] - TOP_K)
    dest_ref[...] = jnp.concatenate(rows, axis=0).astype(jnp.int32)
    run_ref[...] += jnp.sum(picks, axis=1, keepdims=True)


def _route(ids, region_starts):
    n = ids.shape[0]
    tm = int(np.gcd(n, ROUTE_TILE))
    upper = jnp.asarray(np.triu(np.ones((tm, tm), np.float32), 1), BF16)
    starts = jnp.broadcast_to(region_starts.astype(F32)[:, None], (N_EXPERTS, LANES))
    return pl.pallas_call(
        _route_kernel,
        out_shape=jax.ShapeDtypeStruct((SUBLANES, n), jnp.int32),
        grid=(n // tm,),
        in_specs=[pl.BlockSpec((tm, ROUTER_PAD), lambda i: (i, 0)),
                  pl.BlockSpec((tm, tm), lambda i: (0, 0)),
                  pl.BlockSpec((N_EXPERTS, LANES), lambda i: (0, 0))],
        out_specs=pl.BlockSpec((SUBLANES, tm), lambda i: (0, i)),
        scratch_shapes=[pltpu.VMEM((N_EXPERTS, LANES), F32)],
        compiler_params=_params(("arbitrary",)),
        name="route",
    )(ids, upper, starts)


def _row_gather(table, idx):
    info = plsc.get_sparse_core_info()
    n_cores, n_workers = info.num_cores, info.num_cores * info.num_subcores
    n_rows, width = idx.shape[0], table.shape[1]
    chunk_rows = SC_GATHER_ROWS
    per_worker = n_rows // n_workers
    n_chunks = per_worker // chunk_rows
    assert per_worker * n_workers == n_rows and n_chunks * chunk_rows == per_worker and n_chunks % 2 == 0
    mesh = plsc.VectorSubcoreMesh(core_axis_name="c", subcore_axis_name="s")

    @functools.partial(
        pl.kernel, mesh=mesh,
        out_type=jax.ShapeDtypeStruct((n_rows, width), table.dtype),
        scratch_types=[pltpu.VMEM((n_chunks, chunk_rows), jnp.int32), pltpu.VMEM((2, chunk_rows, width), table.dtype),
                       pltpu.SemaphoreType.DMA((2,)), pltpu.SemaphoreType.DMA((2,))],
    )
    def gather_kernel(table_hbm, idx_hbm, out_hbm, idx_v, rows_v, gather_sem, write_sem):
        worker = lax.axis_index("s") * n_cores + lax.axis_index("c")
        pltpu.sync_copy(idx_hbm.at[worker], idx_v)

        def gather(c, slot):
            return pltpu.make_async_copy(table_hbm.at[idx_v.at[c]], rows_v.at[slot], gather_sem.at[slot])

        def write(c, slot):
            first_row = worker * per_worker + c * chunk_rows
            return pltpu.make_async_copy(rows_v.at[slot], out_hbm.at[pl.ds(first_row, chunk_rows)], write_sem.at[slot])

        gather(0, 0).start()

        @pl.loop(0, n_chunks, step=2)
        def _(c0):
            for slot in (0, 1):
                c = c0 + slot
                gather(c, slot).wait()

                @pl.when(c >= 1)
                def _():
                    write(c - 1, 1 - slot).wait()

                @pl.when(c + 1 < n_chunks)
                def _():
                    gather(c + 1, 1 - slot).start()

                write(c, slot).start()

        write(n_chunks - 1, 1).wait()

    return gather_kernel(table, idx.reshape(n_workers, n_chunks, chunk_rows))


def _row_scatter(rows, dest, n_out):
    info = plsc.get_sparse_core_info()
    n_cores, n_workers = info.num_cores, info.num_cores * info.num_subcores
    n_choices, n_rows = dest.shape
    width = rows.shape[1]
    chunk_rows = SC_SCATTER_ROWS
    per_worker = n_rows // (n_workers * chunk_rows)
    assert per_worker * n_workers * chunk_rows == n_rows == rows.shape[0]
    idx = dest.reshape(n_choices, n_workers, per_worker, chunk_rows)
    mesh = plsc.VectorSubcoreMesh(core_axis_name="c", subcore_axis_name="s")

    @functools.partial(
        pl.kernel, mesh=mesh,
        out_type=jax.ShapeDtypeStruct((n_out, width), rows.dtype),
        scratch_types=[pltpu.VMEM((n_choices, per_worker, chunk_rows), jnp.int32),
                       pltpu.VMEM((chunk_rows, width), rows.dtype), pltpu.SemaphoreType.DMA],
    )
    def scatter_kernel(rows_hbm, idx_hbm, out_hbm, idx_v, rows_v, sem):
        worker = lax.axis_index("s") * n_cores + lax.axis_index("c")
        for k in range(n_choices):
            pltpu.sync_copy(idx_hbm.at[k, worker], idx_v.at[k])

        @pl.loop(0, per_worker)
        def _(c):
            first_row = (worker * per_worker + c) * chunk_rows
            pltpu.sync_copy(rows_hbm.at[pl.ds(first_row, chunk_rows)], rows_v)
            copies = [pltpu.async_copy(rows_v, out_hbm.at[idx_v.at[k, c]], sem) for k in range(n_choices)]
            for cp in copies:
                cp.wait()

    return scatter_kernel(rows, idx)


def _expert_counts(step_picks):
    return jnp.sum(step_picks[:, 0, :N_EXPERTS], axis=0).astype(jnp.int32)


def _moe(h2, ids, counts, layer, w_exp):
    n = h2.shape[0]
    nk = n * TOP_K
    tm = EXPERT_TILE
    padded = (counts + tm - 1) // tm * tm
    pends = jnp.cumsum(padded)
    dest = _route(ids, pends - padded)[:TOP_K]
    n_blocks = -(-nk // tm) + N_EXPERTS
    n_slots = n_blocks * tm
    block_start = jnp.arange(n_blocks, dtype=jnp.int32) * tm
    block_e = jnp.minimum(jnp.sum((pends[None] <= block_start[:, None]).astype(jnp.int32), axis=1), N_EXPERTS - 1)
    is_e = (block_e[:, None] == jnp.arange(N_EXPERTS, dtype=jnp.int32)[None]).astype(jnp.int32)
    block_rows = jnp.clip(jnp.sum(is_e * (pends - padded + counts)[None], axis=1) - block_start, 0, tm)
    n_used = (pends[-1] // tm).astype(jnp.int32).reshape(1)
    e_ids = jnp.arange(N_EXPERTS, dtype=jnp.int32)
    later = (e_ids[None] > e_ids[:, None]) & (counts[None] > 0)
    next_nonempty = jnp.min(jnp.where(later, e_ids[None], N_EXPERTS), axis=1)
    next_nonempty = jnp.where(next_nonempty == N_EXPERTS, -1, next_nonempty)
    next_e = jnp.sum(is_e * next_nonempty[None], axis=1)
    xb = _row_scatter(h2, dest, n_slots)
    return _experts(xb, block_e, next_e, block_rows, n_used, layer, *w_exp), dest


def _gather_choices(yb, dest, lo, hi):
    return _row_gather(yb, dest[:, lo:hi].reshape(-1)).reshape(TOP_K, hi - lo, -1)


def _mod_parts(mod_l, b):
    d = mod_l.shape[-1] // 6
    lat = [mod_l[:b, k * d:(k + 1) * d].reshape(b, 1, d) for k in range(6)]
    ctx = [jnp.broadcast_to(mod_l[b, k * d:(k + 1) * d].reshape(1, 1, d), (b, 1, d)) for k in range(6)]
    return lat, ctx


def kernel(x, c, ctx, c_ctx, ada_w, ada_b, norm1_g, norm2_g, ev_w_in, ev_w_out, ev_q_gain, ev_k_gain, ev_rpb, ev_conv_w, ev_conv_b, od_w_in, od_w_out, od_conv_w, od_conv_b, od_fwd_wa, od_fwd_ba, od_fwd_wx, od_fwd_bx, od_fwd_lam, od_bwd_wa, od_bwd_ba, od_bwd_wx, od_bwd_bx, od_bwd_lam, router_w, router_b, exp_w_gate, exp_b_gate, exp_w_up, exp_b_up, exp_w_down, exp_b_down):
    b, t, d = x.shape
    l = ctx.shape[1]
    assert ada_w.shape[0] == DEPTH == 2 and t % GRID_W == 0 and t // GRID_W >= WIN_H

    n_rows_c = -(-(b + 1) // SUBLANES) * SUBLANES
    cvec = jnp.zeros((n_rows_c, d), F32).at[:b].set(c).at[b].set(c_ctx)
    mod = _ada_mod(cvec, ada_w, ada_b)

    def router(layer):
        w_r = jnp.zeros((d, ROUTER_PAD), F32).at[:, :N_EXPERTS].set(router_w[layer]).astype(BF16)
        b_r = jnp.zeros((1, ROUTER_PAD), F32).at[0, :N_EXPERTS].set(router_b[layer])
        return w_r, b_r

    w_exp = (exp_w_gate, exp_b_gate, exp_w_up, exp_b_up, exp_w_down, exp_b_down)

    (sh1, sc1, g1, sh2, sc2, g2), (csh1, csc1, cg1, csh2, csc2, cg2) = _mod_parts(mod[0], b)
    n1 = norm1_g[0].reshape(1, d)
    n2 = norm2_g[0].reshape(1, d)
    w_in = ev_w_in[0].astype(BF16)
    w_out = ev_w_out[0].astype(BF16)
    q_scale = NA_HEAD_DIM ** -0.5 * LOG2_E
    head_gain = jnp.stack([jnp.tile(ev_q_gain[0] * q_scale, NA_HEADS), jnp.tile(ev_k_gain[0], NA_HEADS)])
    head_gain = head_gain.reshape(2, 1, NA_WIDTH).astype(F32)
    n_tok = b * (l + t)
    proj, *fresh = _inproj(x, n1, sh1, sc1, w_in, head_gain, ((n_tok, d // 2), (n_tok, ROUTER_PAD)))
    proj_c, = _inproj(ctx, n1, csh1, csc1, w_in, head_gain)
    kh, r0, row_type, patterns = _na_tables(t // GRID_W)
    bias_tab = _na_bias_table(ev_rpb[0], patterns)
    o_a = _neighbourhood_attention(proj, proj_c, bias_tab, r0, row_type, kh)
    oc_a = _context_attention(proj_c)
    w_r, b_r = router(0)
    c1, tokens, ids_all, gates_c, picks_c = _even_out(oc_a, proj_c, ev_conv_w[0], ev_conv_b[0], w_out, ctx, cg1, n2,
                                                      csh2, csc2, w_r, b_r, (n_tok, 0), fresh)
    x1, tokens, ids_all, gates, picks = _even_out(o_a, proj, ev_conv_w[0], ev_conv_b[0], w_out, x, g1, n2, sh2, sc2,
                                                  w_r, b_r, (n_tok, b * l), (tokens, ids_all))
    yb, dest = _moe(tokens, ids_all, _expert_counts(picks_c) + _expert_counts(picks), 0, w_exp)
    y_sel = _gather_choices(yb, dest, 0, b * (l + t))
    g2_prev, cg2_prev = g2, cg2

    (sh1, sc1, g1, sh2, sc2, g2), (csh1, csc1, _, _, _, _) = _mod_parts(mod[1], b)
    n1 = norm1_g[1].reshape(1, d)
    n2 = norm2_g[1].reshape(1, d)
    w_in = od_w_in[0].astype(BF16)
    width = w_in.shape[1] // 2
    x, proj = _combine_inproj_conv(x1, g2_prev, gates, y_sel, b * l, n1, sh1, sc1, w_in, od_conv_w[0], od_conv_b[0],
                                   width)
    _, u_ctx = _combine_inproj_conv(c1, cg2_prev, gates_c, y_sel, 0, n1, csh1, csc1, w_in[:, width:], od_conv_w[0],
                                    od_conv_b[0], 0)
    directions = []
    for wa, ba, wx, bx, lam in ((od_fwd_wa, od_fwd_ba, od_fwd_wx, od_fwd_bx, od_fwd_lam),
                                (od_bwd_wa, od_bwd_ba, od_bwd_wx, od_bwd_bx, od_bwd_lam)):
        directions.append((jnp.concatenate([wa[0], wx[0]], axis=-1).astype(BF16), ba[0], bx[0], lam[0]))
    h_dir = _lru_scan(proj, u_ctx, *directions)
    w_r, b_r = router(1)
    x1, h2, ids, gates, picks = _odd_out(h_dir[0], h_dir[1], proj, od_w_out[0].astype(BF16), x, g1, n2, sh2, sc2,
                                         w_r, b_r)
    yb, dest = _moe(h2, ids, _expert_counts(picks), 1, w_exp)
    return _combine(x1, g2, gates, _gather_choices(yb, dest, 0, b * t), 0)
```
